```python
import math
import jax, jax.numpy as jnp
from jax import lax
import numpy as np

D_MODEL = 1024
BATCH = 4
SEQ = 8192
DEPTH = 1

N_META = 16
MIX_WIDTH = D_MODEL
ATTN_WIDTH = D_MODEL // 2
SSM_WIDTH = MIX_WIDTH - ATTN_WIDTH
N_HEADS = 8
V_HEAD_DIM = ATTN_WIDTH // N_HEADS
QK_NOPE_DIM = 64
QK_ROPE_DIM = 32
QK_HEAD_DIM = QK_NOPE_DIM + QK_ROPE_DIM
Q_LORA_RANK = 384
KV_LORA_RANK = 256
ROPE_BASE = 10000.0
Q_BLOCK = 128
SSM_GROUP = 16
SSM_GROUPS = SSM_WIDTH // SSM_GROUP
SSM_STATE = 64
N_DIR = 2
DT_MIN = 1e-3
DT_MAX = 1e-1
D_FF = 4 * D_MODEL
EPS = 1e-6
OFF_KV = Q_LORA_RANK
OFF_KR = OFF_KV + KV_LORA_RANK
OFF_U = OFF_KR + QK_ROPE_DIM
IN_WIDTH = OFF_U + SSM_WIDTH

kernel_name = "hymba_mla_s5_sandwich_encoder_layer"


def rms_norm(x, g):
    x32 = x.astype(jnp.float32)
    y = x32 * lax.rsqrt(jnp.mean(x32 * x32, axis=-1, keepdims=True) + EPS)
    return (y * g.astype(jnp.float32)).astype(x.dtype)


def rope_tables(pos):
    inv = 1.0 / (ROPE_BASE ** (jnp.arange(0, QK_ROPE_DIM, 2, dtype=jnp.float32) / QK_ROPE_DIM))
    ang = pos.astype(jnp.float32)[..., None] * inv
    return jnp.cos(ang), jnp.sin(ang)


def apply_rope(x, cos, sin):
    x1, x2 = jnp.split(x.astype(jnp.float32), 2, axis=-1)
    return jnp.concatenate([x1 * cos - x2 * sin, x1 * sin + x2 * cos], axis=-1).astype(x.dtype)


def mla_bidirectional(c_q, c_kv, k_r, cos, sin, g_q_lat, w_uq, g_kv_lat, w_ukv):
    bsz, L, _ = c_q.shape
    q = (rms_norm(c_q, g_q_lat) @ w_uq).reshape(bsz, L, N_HEADS, QK_HEAD_DIM)
    q_nope = q[..., :QK_NOPE_DIM]
    q_rope = apply_rope(q[..., QK_NOPE_DIM:], cos[:, :, None, :], sin[:, :, None, :])
    kv = (rms_norm(c_kv, g_kv_lat) @ w_ukv).reshape(bsz, L, N_HEADS, QK_NOPE_DIM + V_HEAD_DIM)
    k_nope = kv[..., :QK_NOPE_DIM]
    v = kv[..., QK_NOPE_DIM:]
    k_rope = apply_rope(k_r, cos, sin)
    scale = QK_HEAD_DIM ** -0.5
    n_blk = -(-L // Q_BLOCK)
    pad = n_blk * Q_BLOCK - L

    def to_blocks(t):
        t = jnp.pad(t, ((0, 0), (0, pad), (0, 0), (0, 0)))
        return t.reshape(bsz, n_blk, Q_BLOCK, N_HEADS, t.shape[-1]).transpose(1, 0, 2, 3, 4)

    def attend(blk):
        qn, qr = blk
        s = (jnp.einsum('bqhd,bkhd->bhqk', qn, k_nope, preferred_element_type=jnp.float32)
             + jnp.einsum('bqhr,bkr->bhqk', qr, k_rope, preferred_element_type=jnp.float32))
        p = jax.nn.softmax(s * scale, axis=-1)
        return jnp.einsum('bhqk,bkhd->bqhd', p.astype(v.dtype), v)

    o = lax.map(attend, (to_blocks(q_nope), to_blocks(q_rope)))
    o = o.transpose(1, 0, 2, 3, 4).reshape(bsz, n_blk * Q_BLOCK, N_HEADS * V_HEAD_DIM)
    return o[:, :L]


def _ssm_combine(left, right):
    a_l, b_l = left
    a_r, b_r = right
    return a_r * a_l, a_r * b_l + b_r


def s5_bidirectional(u, A_re, A_im, log_dt, B_re, B_im, C_re, C_im, D):
    bsz, L, _ = u.shape
    u32 = u.astype(jnp.float32).reshape(bsz, L, SSM_GROUPS, SSM_GROUP)
    uc = u32.astype(jnp.complex64)
    y = D.astype(jnp.float32).reshape(SSM_GROUPS, SSM_GROUP) * u32
    for d in range(N_DIR):
        lam = lax.complex(jnp.minimum(A_re[d].astype(jnp.float32), -1e-4), A_im[d].astype(jnp.float32))
        dt = jnp.exp(log_dt[d].astype(jnp.float32))[:, None]
        lam_bar = jnp.exp(lam * dt)
        b_c = lax.complex(B_re[d].astype(jnp.float32), B_im[d].astype(jnp.float32))
        b_bar = ((lam_bar - 1.0) / lam)[..., None] * b_c
        bu = jnp.einsum('blgh,gph->blgp', uc, b_bar)
        a = jnp.broadcast_to(lam_bar, bu.shape)
        _, xs = lax.associative_scan(_ssm_combine, (a, bu), axis=1, reverse=(d == 1))
        c_c = lax.complex(C_re[d].astype(jnp.float32), C_im[d].astype(jnp.float32))
        y = y + jnp.real(jnp.einsum('ghp,blgp->blgh', c_c, xs))
    return y.reshape(bsz, L, SSM_WIDTH)


def setup_inputs(seed: int = 0) -> dict:
    key = jax.random.key(seed)
    ks = jax.random.split(key, 32)
    nrm = lambda k, shape, s: jax.random.normal(k, shape, jnp.float32) * s
    gain = lambda k, shape: 1.0 + 0.02 * jax.random.normal(k, shape, jnp.float32)
    x = jax.random.normal(ks[0], (BATCH, SEQ, D_MODEL), jnp.float32)
    positions = jnp.broadcast_to(jnp.arange(SEQ, dtype=jnp.int32), (BATCH, SEQ))
    meta_tokens = nrm(ks[1], (N_META, D_MODEL), 1.0)
    n_idx = jnp.arange(SSM_STATE, dtype=jnp.float32)
    ssm_A_re = -0.5 + 0.01 * jax.random.normal(ks[8], (DEPTH, N_DIR, SSM_GROUPS, SSM_STATE), jnp.float32)
    ssm_A_im = jnp.broadcast_to(math.pi * n_idx, (DEPTH, N_DIR, SSM_GROUPS, SSM_STATE)) \
        + 0.01 * jax.random.normal(ks[9], (DEPTH, N_DIR, SSM_GROUPS, SSM_STATE), jnp.float32)
    ssm_log_dt = jax.random.uniform(ks[10], (DEPTH, N_DIR, SSM_GROUPS), jnp.float32,
                                    math.log(DT_MIN), math.log(DT_MAX))
    return {
        "x": x,
        "positions": positions,
        "meta_tokens": meta_tokens,
        "g_pre_mix": gain(ks[2], (DEPTH, D_MODEL)),
        "w_in": nrm(ks[3], (DEPTH, D_MODEL, IN_WIDTH), D_MODEL ** -0.5),
        "g_q_lat": gain(ks[4], (DEPTH, Q_LORA_RANK)),
        "w_uq": nrm(ks[5], (DEPTH, Q_LORA_RANK, N_HEADS * QK_HEAD_DIM), Q_LORA_RANK ** -0.5),
        "g_kv_lat": gain(ks[6], (DEPTH, KV_LORA_RANK)),
        "w_ukv": nrm(ks[7], (DEPTH, KV_LORA_RANK, N_HEADS * (QK_NOPE_DIM + V_HEAD_DIM)), KV_LORA_RANK ** -0.5),
        "ssm_A_re": ssm_A_re,
        "ssm_A_im": ssm_A_im,
        "ssm_log_dt": ssm_log_dt,
        "ssm_B_re": nrm(ks[11], (DEPTH, N_DIR, SSM_GROUPS, SSM_STATE, SSM_GROUP), (2 * SSM_GROUP) ** -0.5),
        "ssm_B_im": nrm(ks[12], (DEPTH, N_DIR, SSM_GROUPS, SSM_STATE, SSM_GROUP), (2 * SSM_GROUP) ** -0.5),
        "ssm_C_re": nrm(ks[13], (DEPTH, N_DIR, SSM_GROUPS, SSM_GROUP, SSM_STATE), SSM_STATE ** -0.5),
        "ssm_C_im": nrm(ks[14], (DEPTH, N_DIR, SSM_GROUPS, SSM_GROUP, SSM_STATE), SSM_STATE ** -0.5),
        "ssm_D": nrm(ks[15], (DEPTH, SSM_WIDTH), 1.0),
        "w_glu": nrm(ks[16], (DEPTH, SSM_WIDTH, 2 * SSM_WIDTH), SSM_WIDTH ** -0.5),
        "g_mix_out": gain(ks[17], (DEPTH, MIX_WIDTH)),
        "w_out": nrm(ks[18], (DEPTH, MIX_WIDTH, D_MODEL), MIX_WIDTH ** -0.5),
        "g_post_mix": gain(ks[19], (DEPTH, D_MODEL)),
        "g_pre_mlp": gain(ks[20], (DEPTH, D_MODEL)),
        "w_mlp_up": nrm(ks[21], (DEPTH, D_MODEL, D_FF), D_MODEL ** -0.5),
        "w_mlp_down": nrm(ks[22], (DEPTH, D_FF, D_MODEL), D_FF ** -0.5),
        "g_post_mlp": gain(ks[23], (DEPTH, D_MODEL)),
    }


def reference(x, positions, meta_tokens, g_pre_mix, w_in, g_q_lat, w_uq, g_kv_lat, w_ukv,
              ssm_A_re, ssm_A_im, ssm_log_dt, ssm_B_re, ssm_B_im, ssm_C_re, ssm_C_im, ssm_D,
              w_glu, g_mix_out, w_out, g_post_mix, g_pre_mlp, w_mlp_up, w_mlp_down, g_post_mlp):
    bsz = x.shape[0]
    meta = jnp.broadcast_to(meta_tokens.astype(x.dtype)[None], (bsz, N_META, x.shape[-1]))
    h = jnp.concatenate([meta, x], axis=1)
    meta_pos = jnp.broadcast_to(jnp.arange(N_META, dtype=jnp.int32), (bsz, N_META))
    pos = jnp.concatenate([meta_pos, positions.astype(jnp.int32) + N_META], axis=1)
    cos, sin = rope_tables(pos)
    for i in range(DEPTH):
        xn = rms_norm(h, g_pre_mix[i])
        proj = xn @ w_in[i]
        c_q = proj[..., :OFF_KV]
        c_kv = proj[..., OFF_KV:OFF_KR]
        k_r = proj[..., OFF_KR:OFF_U]
        u = proj[..., OFF_U:]
        attn = mla_bidirectional(c_q, c_kv, k_r, cos, sin, g_q_lat[i], w_uq[i], g_kv_lat[i], w_ukv[i])
        y = s5_bidirectional(u, ssm_A_re[i], ssm_A_im[i], ssm_log_dt[i], ssm_B_re[i], ssm_B_im[i],
                             ssm_C_re[i], ssm_C_im[i], ssm_D[i])
        z = jax.nn.gelu(y) @ w_glu[i].astype(jnp.float32)
        ssm = (z[..., :SSM_WIDTH] * jax.nn.sigmoid(z[..., SSM_WIDTH:])).astype(h.dtype)
        g_mix = g_mix_out[i]
        mix = jnp.concatenate([rms_norm(attn, g_mix[:ATTN_WIDTH]),
                               rms_norm(ssm, g_mix[ATTN_WIDTH:])], axis=-1)
        h = h + rms_norm(mix @ w_out[i], g_post_mix[i])
        m = rms_norm(h, g_pre_mlp[i]) @ w_mlp_up[i]
        m = jnp.square(jax.nn.relu(m)) @ w_mlp_down[i]
        h = h + rms_norm(m, g_post_mlp[i])
    return h[:, N_META:]
```

```python
import functools
import math

import jax
import jax.numpy as jnp
from jax import lax
from jax.experimental import pallas as pl
from jax.experimental.pallas import tpu as pltpu

D_MODEL = 1024
N_META = 16
ATTN_WIDTH = 512
SSM_WIDTH = 512
N_HEADS = 8
V_HEAD_DIM = 64
QK_NOPE_DIM = 64
QK_ROPE_DIM = 32
QK_HEAD_DIM = QK_NOPE_DIM + QK_ROPE_DIM
Q_LORA_RANK = 384
KV_LORA_RANK = 256
ROPE_BASE = 10000.0
SSM_GROUP = 16
SSM_GROUPS = 32
SSM_STATE = 64
D_FF = 4 * D_MODEL
EPS = 1e-6
OFF_KV = Q_LORA_RANK
OFF_KR = OFF_KV + KV_LORA_RANK
OFF_U = OFF_KR + QK_ROPE_DIM

LANES = 128
SUBLANES = 8
HEAD_PAD = LANES
ONES_LANE = V_HEAD_DIM
PW_Q = 0
PW_KV = PW_Q + Q_LORA_RANK
PW_KRA = PW_KV + KV_LORA_RANK
PW_KRB = PW_KRA + HEAD_PAD
PW_U = PW_KRB + HEAD_PAD
PW_END = PW_U + SSM_WIDTH

SSM_CHUNK = 32
CHUNK_W = SSM_CHUNK * SSM_GROUP
STATE_W = 4 * SSM_STATE

ROW_TILE = 512
Q_TILE = 512
K_TILE = 512
FF_TILE = 1024
VMEM_LIMIT = 56 * 1024 * 1024

_BF = jnp.bfloat16
_F32 = jnp.float32


def _dot(a, b):
    return jnp.dot(a, b, preferred_element_type=_F32)


def _rms(x, g):
    return x * lax.rsqrt(jnp.mean(x * x, axis=-1, keepdims=True) + EPS) * g


def _proj_kernel(x_ref, cos_ref, sin_ref, gpre_ref, win_ref, gq_ref, wq_ref, gkv_ref, wkv_ref,
                 q_ref, k_ref, v_ref, u_ref, *, transpose_k):
    x = x_ref[...]
    xn = _rms(x, gpre_ref[...]).astype(_BF)
    proj = _dot(xn, win_ref[...])
    cq = proj[:, PW_Q:PW_KV]
    ckv = proj[:, PW_KV:PW_KRA]
    cos = cos_ref[...]
    sin = sin_ref[...]
    kr = proj[:, PW_KRA:PW_KRB] * cos + proj[:, PW_KRB:PW_U] * sin
    u_ref[...] = proj[:, PW_U:PW_END].astype(_BF)
    qq = _dot(_rms(cq, gq_ref[...]).astype(_BF), wq_ref[...])
    kv = _dot(_rms(ckv, gkv_ref[...]).astype(_BF), wkv_ref[...])
    lane = lax.broadcasted_iota(jnp.int32, (1, HEAD_PAD), 1)
    ones_col = (lane == ONES_LANE).astype(_F32)
    hw = N_HEADS * HEAD_PAD
    for h in range(N_HEADS):
        lo = h * HEAD_PAD
        q_h = qq[:, lo:lo + HEAD_PAD] * cos + qq[:, hw + lo:hw + lo + HEAD_PAD] * sin
        q_ref[h] = q_h.astype(_BF)
        k_h = kv[:, lo:lo + HEAD_PAD] + kr
        if transpose_k:
            k_ref[h] = k_h.T.astype(_BF)
        else:
            k_ref[h] = k_h.astype(_BF)
        v_ref[h] = (kv[:, hw + lo:hw + lo + HEAD_PAD] + ones_col).astype(_BF)


def _const_spec(shape):
    nd = len(shape)
    return pl.BlockSpec(shape, lambda *_: (0,) * nd)


def _proj_call(x, cos_t, sin_t, gpre, win, gq, wq, gkv, wkv, *, tile, transpose_k):
    bsz, seq, _ = x.shape
    nt = seq // tile
    hw = N_HEADS * HEAD_PAD
    if transpose_k:
        k_shape = (bsz, N_HEADS, nt, HEAD_PAD, tile)
        k_spec = pl.BlockSpec((None, N_HEADS, None, HEAD_PAD, tile), lambda b, i: (b, 0, i, 0, 0))
    else:
        k_shape = (bsz, N_HEADS, seq, HEAD_PAD)
        k_spec = pl.BlockSpec((None, N_HEADS, tile, HEAD_PAD), lambda b, i: (b, 0, i, 0))
    head_spec = pl.BlockSpec((None, N_HEADS, tile, HEAD_PAD), lambda b, i: (b, 0, i, 0))
    row_spec = lambda w: pl.BlockSpec((None, tile, w), lambda b, i: (b, i, 0))
    return pl.pallas_call(
        functools.partial(_proj_kernel, transpose_k=transpose_k),
        grid=(bsz, nt),
        in_specs=[row_spec(D_MODEL), row_spec(HEAD_PAD), row_spec(HEAD_PAD),
                  _const_spec((1, D_MODEL)), _const_spec((D_MODEL, PW_END)),
                  _const_spec((1, Q_LORA_RANK)), _const_spec((Q_LORA_RANK, 2 * hw)),
                  _const_spec((1, KV_LORA_RANK)), _const_spec((KV_LORA_RANK, 2 * hw))],
        out_specs=[head_spec, k_spec, head_spec, row_spec(SSM_WIDTH)],
        out_shape=[jax.ShapeDtypeStruct((bsz, N_HEADS, seq, HEAD_PAD), _BF),
                   jax.ShapeDtypeStruct(k_shape, _BF),
                   jax.ShapeDtypeStruct((bsz, N_HEADS, seq, HEAD_PAD), _BF),
                   jax.ShapeDtypeStruct((bsz, seq, SSM_WIDTH), _BF)],
        compiler_params=pltpu.CompilerParams(
            dimension_semantics=("parallel", "parallel"), vmem_limit_bytes=VMEM_LIMIT),
        name="proj",
    )(x, cos_t, sin_t, gpre, win, gq, wq, gkv, wkv)


def _attn_kernel(q_ref, kt_ref, v_ref, ktm_ref, vm_ref, o_ref, *, nk, tk):
    tq = q_ref.shape[1]
    lane = lax.broadcasted_iota(jnp.int32, (tq, HEAD_PAD), 1)
    outs = []
    for hh in range(2):
        q = q_ref[hh]
        s0 = jnp.where(lane < N_META, _dot(q, ktm_ref[hh]), -jnp.inf)
        m0 = jnp.max(s0, axis=1, keepdims=True)
        acc0 = _dot(jnp.exp(s0 - m0).astype(_BF), vm_ref[hh])

        def body(j, carry, hh=hh, q=q):
            m, acc = carry
            s = _dot(q, kt_ref[hh, j])
            m_new = jnp.maximum(m, jnp.max(s, axis=1, keepdims=True))
            alpha = jnp.exp(m - m_new)
            p = jnp.exp(s - m_new).astype(_BF)
            off = pl.multiple_of(j * tk, tk)
            acc = alpha * acc + _dot(p, v_ref[hh, pl.ds(off, tk), :])
            return m_new, acc

        _, acc = lax.fori_loop(0, nk, body, (m0, acc0))
        outs.append(acc * (1.0 / acc[:, ONES_LANE:ONES_LANE + 1]))
    o_ref[...] = jnp.where(lane < V_HEAD_DIM, outs[0], pltpu.roll(outs[1], V_HEAD_DIM, 1))


def _attn_call(q, kt, v, ktm, vm, *, tq):
    bsz, _, seq, _ = q.shape
    nk, tk = kt.shape[2], kt.shape[4]
    return pl.pallas_call(
        functools.partial(_attn_kernel, nk=nk, tk=tk),
        grid=(bsz, N_HEADS // 2, seq // tq),
        in_specs=[pl.BlockSpec((None, 2, tq, HEAD_PAD), lambda b, hp, i: (b, hp, i, 0)),
                  pl.BlockSpec((None, 2, nk, HEAD_PAD, tk), lambda b, hp, i: (b, hp, 0, 0, 0)),
                  pl.BlockSpec((None, 2, seq, HEAD_PAD), lambda b, hp, i: (b, hp, 0, 0)),
                  pl.BlockSpec((2, HEAD_PAD, LANES), lambda b, hp, i: (hp, 0, 0)),
                  pl.BlockSpec((2, LANES, HEAD_PAD), lambda b, hp, i: (hp, 0, 0))],
        out_specs=pl.BlockSpec((None, tq, LANES), lambda b, hp, i: (b, i, hp)),
        out_shape=jax.ShapeDtypeStruct((bsz, seq, ATTN_WIDTH), _F32),
        compiler_params=pltpu.CompilerParams(
            dimension_semantics=("parallel", "parallel", "arbitrary"), vmem_limit_bytes=VMEM_LIMIT),
        name="attn",
    )(q, kt, v, ktm, vm)


def _s5_kernel(u_ref, um_ref, m_ref, w_ref, v_ref, ar_ref, ai_ref, y_ref, s_scr, x_scr, *, nc):
    u = u_ref[...]
    s_scr[...] = _dot(u, w_ref[...])
    half = STATE_W // 2
    lane = lax.broadcasted_iota(jnp.int32, (SUBLANES, half), 1)
    fwd = lane < SSM_STATE
    sm = _dot(um_ref[...], w_ref[...])
    xr0 = jnp.where(fwd, sm[:, :half], 0.0)
    xi0 = jnp.where(fwd, sm[:, half:], 0.0)
    ar = ar_ref[...]
    ai = ai_ref[...]

    def body(j, carry):
        xr, xi = carry
        rf = pl.multiple_of(j * SUBLANES, SUBLANES)
        rb = pl.multiple_of((nc - 1 - j) * SUBLANES, SUBLANES)
        x_scr[pl.ds(rf, SUBLANES), 0:SSM_STATE] = xr[:, 0:SSM_STATE]
        x_scr[pl.ds(rb, SUBLANES), SSM_STATE:half] = xr[:, SSM_STATE:half]
        x_scr[pl.ds(rf, SUBLANES), half:half + SSM_STATE] = xi[:, 0:SSM_STATE]
        x_scr[pl.ds(rb, SUBLANES), half + SSM_STATE:STATE_W] = xi[:, SSM_STATE:half]
        sf = s_scr[pl.ds(rf, SUBLANES), :]
        sb = s_scr[pl.ds(rb, SUBLANES), :]
        sr = jnp.where(fwd, sf[:, :half], sb[:, :half])
        si = jnp.where(fwd, sf[:, half:], sb[:, half:])
        return ar * xr - ai * xi + sr, ar * xi + ai * xr + si

    lax.fori_loop(0, nc, body, (xr0, xi0))
    y_ref[...] = _dot(u, m_ref[...]) + _dot(x_scr[...].astype(_BF), v_ref[...])


def _s5_call(ug, um, m_mat, w_mat, v_mat, ar, ai, *, nc):
    rows = nc * SUBLANES
    g_spec = lambda *shape: pl.BlockSpec((None,) + shape, lambda g: (g,) + (0,) * len(shape))
    return pl.pallas_call(
        functools.partial(_s5_kernel, nc=nc),
        grid=(SSM_GROUPS,),
        in_specs=[g_spec(rows, CHUNK_W), g_spec(SUBLANES, CHUNK_W), g_spec(CHUNK_W, CHUNK_W),
                  g_spec(CHUNK_W, STATE_W), g_spec(STATE_W, CHUNK_W),
                  g_spec(SUBLANES, STATE_W // 2), g_spec(SUBLANES, STATE_W // 2)],
        out_specs=g_spec(rows, CHUNK_W),
        out_shape=jax.ShapeDtypeStruct((SSM_GROUPS, rows, CHUNK_W), _F32),
        scratch_shapes=[pltpu.VMEM((rows, STATE_W), _F32), pltpu.VMEM((rows, STATE_W), _F32)],
        compiler_params=pltpu.CompilerParams(
            dimension_semantics=("parallel",), vmem_limit_bytes=VMEM_LIMIT),
        name="s5",
    )(ug, um, m_mat, w_mat, v_mat, ar, ai)


def _s5_matrices(a_re, a_im, log_dt, b_re, b_im, c_re, c_im, d_skip):
    tc = SSM_CHUNK
    lam = lax.complex(jnp.minimum(a_re.astype(_F32), -1e-4), a_im.astype(_F32))
    dt = jnp.exp(log_dt.astype(_F32))[..., None]
    lam_dt = lam * dt
    lam_bar = jnp.exp(lam_dt)
    b_bar = ((lam_bar - 1.0) / lam)[..., None] * lax.complex(b_re.astype(_F32), b_im.astype(_F32))
    c_c = lax.complex(c_re.astype(_F32), c_im.astype(_F32))
    k_idx = jnp.arange(tc + 1, dtype=_F32)
    pw = jnp.exp(lam_dt[:, :, None, :] * k_idx[None, None, :, None])
    kern = jnp.real(jnp.einsum('dgop,dgkp,dgpi->dgkoi', c_c, pw[:, :, :tc], b_bar))
    s_idx = jnp.arange(tc)[:, None]
    t_idx = jnp.arange(tc)[None, :]
    kf = jnp.where((t_idx >= s_idx)[None, :, :, None, None], kern[0][:, jnp.clip(t_idx - s_idx, 0, tc - 1)], 0.0)
    kb = jnp.where((s_idx >= t_idx)[None, :, :, None, None], kern[1][:, jnp.clip(s_idx - t_idx, 0, tc - 1)], 0.0)
    d_g = d_skip.astype(_F32).reshape(SSM_GROUPS, SSM_GROUP)
    skip = (jnp.eye(tc, dtype=_F32)[None, :, :, None, None]
            * (jnp.eye(SSM_GROUP, dtype=_F32)[None] * d_g[:, :, None])[:, None, None, :, :])
    m_full = kf + kb + skip
    m_mat = m_full.transpose(0, 1, 4, 2, 3).reshape(SSM_GROUPS, CHUNK_W, CHUNK_W)
    pw_f = pw[0][:, tc - 1 - jnp.arange(tc)]
    pw_b = pw[1][:, jnp.arange(tc)]
    wf = pw_f[:, :, None, :] * b_bar[0].transpose(0, 2, 1)[:, None, :, :]
    wb = pw_b[:, :, None, :] * b_bar[1].transpose(0, 2, 1)[:, None, :, :]
    w_mat = jnp.concatenate([jnp.real(wf), jnp.real(wb), jnp.imag(wf), jnp.imag(wb)], axis=-1)
    w_mat = w_mat.reshape(SSM_GROUPS, CHUNK_W, STATE_W)
    gf = pw[0][:, 1 + jnp.arange(tc)][:, :, None, :] * c_c[0][:, None, :, :]
    gb = pw[1][:, tc - jnp.arange(tc)][:, :, None, :] * c_c[1][:, None, :, :]
    v_mat = jnp.concatenate([jnp.real(gf), jnp.real(gb), -jnp.imag(gf), -jnp.imag(gb)], axis=-1)
    v_mat = v_mat.reshape(SSM_GROUPS, CHUNK_W, STATE_W).transpose(0, 2, 1)
    a_chunk = pw[:, :, tc]
    ar = jnp.concatenate([jnp.real(a_chunk[0]), jnp.real(a_chunk[1])], axis=-1)
    ai = jnp.concatenate([jnp.imag(a_chunk[0]), jnp.imag(a_chunk[1])], axis=-1)
    return m_mat, w_mat, v_mat, ar, ai


def _post_kernel(x_ref, attn_ref, y_ref, wglu_ref, gmix_ref, wout_ref, gpm_ref, gpre_ref,
                 wup_ref, wdn_ref, gpost_ref, o_ref):
    y = y_ref[...]
    gy = 0.5 * y * (1.0 + jnp.tanh(math.sqrt(2.0 / math.pi) * (y + 0.044715 * (y * y * y))))
    z = _dot(gy.astype(_BF), wglu_ref[...])
    ssm = z[:, :SSM_WIDTH] * (1.0 / (1.0 + jnp.exp(-z[:, SSM_WIDTH:])))
    gmix = gmix_ref[...]
    mix = jnp.concatenate([_rms(attn_ref[...], gmix[:, :ATTN_WIDTH]),
                           _rms(ssm, gmix[:, ATTN_WIDTH:])], axis=-1).astype(_BF)
    h1 = x_ref[...] + _rms(_dot(mix, wout_ref[...]), gpm_ref[...])
    hn = _rms(h1, gpre_ref[...]).astype(_BF)
    acc = None
    for c in range(D_FF // FF_TILE):
        up = _dot(hn, wup_ref[:, c * FF_TILE:(c + 1) * FF_TILE])
        up = jnp.maximum(up, 0.0)
        part = _dot((up * up).astype(_BF), wdn_ref[c * FF_TILE:(c + 1) * FF_TILE, :])
        acc = part if acc is None else acc + part
    o_ref[...] = h1 + _rms(acc, gpost_ref[...])


def _post_call(x, attn, y, wglu, gmix, wout, gpm, gpre, wup, wdn, gpost, *, tile):
    bsz, seq, _ = x.shape
    row_spec = lambda w: pl.BlockSpec((None, tile, w), lambda b, i: (b, i, 0))
    wspec = lambda shape: pl.BlockSpec(shape, lambda b, i: (0, 0), pipeline_mode=pl.Buffered(1))
    return pl.pallas_call(
        _post_kernel,
        grid=(bsz, seq // tile),
        in_specs=[row_spec(D_MODEL), row_spec(ATTN_WIDTH), row_spec(SSM_WIDTH),
                  wspec((SSM_WIDTH, 2 * SSM_WIDTH)), wspec((1, D_MODEL)), wspec((D_MODEL, D_MODEL)),
                  wspec((1, D_MODEL)), wspec((1, D_MODEL)), wspec((D_MODEL, D_FF)),
                  wspec((D_FF, D_MODEL)), wspec((1, D_MODEL))],
        out_specs=row_spec(D_MODEL),
        out_shape=jax.ShapeDtypeStruct((bsz, seq, D_MODEL), _F32),
        compiler_params=pltpu.CompilerParams(
            dimension_semantics=("parallel", "parallel"), vmem_limit_bytes=VMEM_LIMIT),
        name="post",
    )(x, attn, y, wglu, gmix, wout, gpm, gpre, wup, wdn, gpost)


def _rope_partner(w):
    half = QK_ROPE_DIM // 2
    return jnp.concatenate([-w[..., half:], w[..., :half]], axis=-1)


def _rope_tables(pos):
    inv = 1.0 / (ROPE_BASE ** (jnp.arange(0, QK_ROPE_DIM, 2, dtype=_F32) / QK_ROPE_DIM))
    ang = pos.astype(_F32)[..., None] * inv
    cos, sin = jnp.cos(ang), jnp.sin(ang)
    lead = cos.shape[:-1]
    ones = jnp.ones(lead + (QK_NOPE_DIM,), _F32)
    zeros_n = jnp.zeros(lead + (QK_NOPE_DIM,), _F32)
    zeros_p = jnp.zeros(lead + (HEAD_PAD - QK_HEAD_DIM,), _F32)
    return (jnp.concatenate([ones, cos, cos, zeros_p], axis=-1),
            jnp.concatenate([zeros_n, sin, sin, zeros_p], axis=-1))


def _prep_weights(w_in, w_uq, w_ukv):
    scale = QK_HEAD_DIM ** -0.5
    pad_lo = jnp.zeros((D_MODEL, QK_NOPE_DIM), _F32)
    pad_hi = jnp.zeros((D_MODEL, HEAD_PAD - QK_HEAD_DIM), _F32)
    w_kr = w_in[:, OFF_KR:OFF_U]
    win = jnp.concatenate([w_in[:, :OFF_KR],
                           pad_lo, w_kr, pad_hi,
                           pad_lo, _rope_partner(w_kr), pad_hi,
                           w_in[:, OFF_U:]], axis=1).astype(_BF)
    wq3 = w_uq.reshape(Q_LORA_RANK, N_HEADS, QK_HEAD_DIM) * scale
    zq = lambda n: jnp.zeros((Q_LORA_RANK, N_HEADS, n), _F32)
    q_plain = jnp.concatenate([wq3, zq(HEAD_PAD - QK_HEAD_DIM)], axis=-1)
    q_part = jnp.concatenate([zq(QK_NOPE_DIM), _rope_partner(wq3[..., QK_NOPE_DIM:]),
                              zq(HEAD_PAD - QK_HEAD_DIM)], axis=-1)
    wq = jnp.concatenate([q_plain.reshape(Q_LORA_RANK, -1), q_part.reshape(Q_LORA_RANK, -1)], axis=1).astype(_BF)
    wkv3 = w_ukv.reshape(KV_LORA_RANK, N_HEADS, QK_NOPE_DIM + V_HEAD_DIM)
    zk = lambda n: jnp.zeros((KV_LORA_RANK, N_HEADS, n), _F32)
    k_slab = jnp.concatenate([wkv3[..., :QK_NOPE_DIM], zk(HEAD_PAD - QK_NOPE_DIM)], axis=-1)
    v_slab = jnp.concatenate([wkv3[..., QK_NOPE_DIM:], zk(HEAD_PAD - V_HEAD_DIM)], axis=-1)
    wkv = jnp.concatenate([k_slab.reshape(KV_LORA_RANK, -1), v_slab.reshape(KV_LORA_RANK, -1)], axis=1).astype(_BF)
    return win, wq, wkv


def kernel(x, positions, meta_tokens, g_pre_mix, w_in, g_q_lat, w_uq, g_kv_lat, w_ukv,
           ssm_A_re, ssm_A_im, ssm_log_dt, ssm_B_re, ssm_B_im, ssm_C_re, ssm_C_im, ssm_D,
           w_glu, g_mix_out, w_out, g_post_mix, g_pre_mlp, w_mlp_up, w_mlp_down, g_post_mlp):
    bsz, seq, _ = x.shape
    assert seq % ROW_TILE == 0 and seq % Q_TILE == 0 and seq % SSM_CHUNK == 0 and bsz <= SUBLANES // 2
    assert N_META <= SSM_CHUNK and ROW_TILE == K_TILE
    row = lambda g: g.reshape(1, -1).astype(_F32)

    win, wq, wkv = _prep_weights(w_in[0], w_uq[0], w_ukv[0])
    cos_x, sin_x = _rope_tables(positions.astype(jnp.int32) + N_META)
    cos_m, sin_m = _rope_tables(jnp.arange(N_META, dtype=jnp.int32)[None])
    norm_args = (row(g_pre_mix[0]), win, row(g_q_lat[0]), wq, row(g_kv_lat[0]), wkv)

    q, kt, v, u = _proj_call(x, cos_x, sin_x, *norm_args, tile=ROW_TILE, transpose_k=True)
    _, k_m, v_m, u_m = _proj_call(meta_tokens.astype(x.dtype)[None], cos_m, sin_m, *norm_args,
                                  tile=N_META, transpose_k=False)
    ktm = jnp.pad(k_m[0].transpose(0, 2, 1), ((0, 0), (0, 0), (0, LANES - N_META)))
    vm = jnp.pad(v_m[0], ((0, 0), (0, LANES - N_META), (0, 0)))
    attn = _attn_call(q, kt, v, ktm, vm, tq=Q_TILE)

    nc = seq // SSM_CHUNK
    ug = u.reshape(bsz, nc, SSM_CHUNK, SSM_GROUPS, SSM_GROUP).transpose(3, 1, 0, 2, 4)
    ug = jnp.pad(ug, ((0, 0), (0, 0), (0, SUBLANES - bsz), (0, 0), (0, 0)))
    ug = ug.reshape(SSM_GROUPS, nc * SUBLANES, CHUNK_W)
    um = u_m[0].reshape(N_META, SSM_GROUPS, SSM_GROUP).transpose(1, 0, 2)
    um = jnp.pad(um, ((0, 0), (SSM_CHUNK - N_META, 0), (0, 0))).reshape(SSM_GROUPS, 1, CHUNK_W)
    um = jnp.concatenate([jnp.broadcast_to(um, (SSM_GROUPS, bsz, CHUNK_W)),
                          jnp.zeros((SSM_GROUPS, SUBLANES - bsz, CHUNK_W), um.dtype)], axis=1)
    m_mat, w_mat, v_mat, ar, ai = _s5_matrices(ssm_A_re[0], ssm_A_im[0], ssm_log_dt[0], ssm_B_re[0],
                                               ssm_B_im[0], ssm_C_re[0], ssm_C_im[0], ssm_D[0])
    bc = lambda a: jnp.broadcast_to(a[:, None, :], (SSM_GROUPS, SUBLANES, a.shape[-1]))
    yg = _s5_call(ug, um, m_mat.astype(_BF), w_mat.astype(_BF), v_mat.astype(_BF), bc(ar), bc(ai), nc=nc)
    y = yg.reshape(SSM_GROUPS, nc, SUBLANES, SSM_CHUNK, SSM_GROUP)[:, :, :bsz]
    y = y.transpose(2, 1, 3, 0, 4).reshape(bsz, seq, SSM_WIDTH)

    return _post_call(x, attn, y, w_glu[0].astype(_BF), row(g_mix_out[0]), w_out[0].astype(_BF),
                      row(g_post_mix[0]), row(g_pre_mlp[0]), w_mlp_up[0].astype(_BF),
                      w_mlp_down[0].astype(_BF), row(g_post_mlp[0]), tile=ROW_TILE)
```

```python
import functools
import math

import jax
import jax.numpy as jnp
from jax import lax
from jax.experimental import pallas as pl
from jax.experimental.pallas import tpu as pltpu

D_MODEL = 1024
N_META = 16
ATTN_WIDTH = 512
SSM_WIDTH = 512
N_HEADS = 8
V_HEAD_DIM = 64
QK_NOPE_DIM = 64
QK_ROPE_DIM = 32
QK_HEAD_DIM = QK_NOPE_DIM + QK_ROPE_DIM
Q_LORA_RANK = 384
KV_LORA_RANK = 256
ROPE_BASE = 10000.0
SSM_GROUP = 16
SSM_GROUPS = 32
SSM_STATE = 64
D_FF = 4 * D_MODEL
EPS = 1e-6
OFF_KV = Q_LORA_RANK
OFF_KR = OFF_KV + KV_LORA_RANK
OFF_U = OFF_KR + QK_ROPE_DIM

LANES = 128
SUBLANES = 8
HEAD_PAD = LANES
ONES_LANE = V_HEAD_DIM
PW_Q = 0
PW_KV = PW_Q + Q_LORA_RANK
PW_KRA = PW_KV + KV_LORA_RANK
PW_KRB = PW_KRA + HEAD_PAD
PW_U = PW_KRB + HEAD_PAD
PW_END = PW_U + SSM_WIDTH

SSM_CHUNK = 32
CHUNK_W = SSM_CHUNK * SSM_GROUP
STATE_W = 4 * SSM_STATE

ROW_TILE = 512
FF_TILE = 1024
VMEM_LIMIT = 56 * 1024 * 1024

_BF = jnp.bfloat16
_F32 = jnp.float32


def _dot(a, b):
    return jnp.dot(a, b, preferred_element_type=_F32)


def _rms(x, g):
    return x * lax.rsqrt(jnp.mean(x * x, axis=-1, keepdims=True) + EPS) * g


def _proj_kernel(x_ref, cos_ref, sin_ref, gpre_ref, win_ref, gq_ref, wq_ref, gkv_ref, wkv_ref,
                 q_ref, k_ref, v_ref, u_ref, *, transpose_qv):
    x = x_ref[...]
    xn = _rms(x, gpre_ref[...]).astype(_BF)
    proj = _dot(xn, win_ref[...])
    cq = proj[:, PW_Q:PW_KV]
    ckv = proj[:, PW_KV:PW_KRA]
    cos = cos_ref[...]
    sin = sin_ref[...]
    kr = proj[:, PW_KRA:PW_KRB] * cos + proj[:, PW_KRB:PW_U] * sin
    u_ref[...] = proj[:, PW_U:PW_END].astype(_BF)
    qq = _dot(_rms(cq, gq_ref[...]).astype(_BF), wq_ref[...])
    kv = _dot(_rms(ckv, gkv_ref[...]).astype(_BF), wkv_ref[...])
    lane = lax.broadcasted_iota(jnp.int32, (1, HEAD_PAD), 1)
    ones_col = (lane == ONES_LANE).astype(_F32)
    hw = N_HEADS * HEAD_PAD
    for h in range(N_HEADS):
        lo = h * HEAD_PAD
        q_h = qq[:, lo:lo + HEAD_PAD] * cos + qq[:, hw + lo:hw + lo + HEAD_PAD] * sin
        k_h = kv[:, lo:lo + HEAD_PAD] + kr
        v_h = kv[:, hw + lo:hw + lo + HEAD_PAD] + ones_col
        k_ref[h] = k_h.astype(_BF)
        if transpose_qv:
            q_ref[h] = q_h.T.astype(_BF)
            v_ref[h] = v_h.T.astype(_BF)
        else:
            q_ref[h] = q_h.astype(_BF)
            v_ref[h] = v_h.astype(_BF)


def _const_spec(shape):
    nd = len(shape)
    return pl.BlockSpec(shape, lambda *_: (0,) * nd)


def _proj_call(x, cos_t, sin_t, gpre, win, gq, wq, gkv, wkv, *, tile, transpose_qv):
    bsz, seq, _ = x.shape
    nt = seq // tile
    hw = N_HEADS * HEAD_PAD
    head_shape = (bsz, N_HEADS, seq, HEAD_PAD)
    head_spec = pl.BlockSpec((None, N_HEADS, tile, HEAD_PAD), lambda b, i: (b, 0, i, 0))
    if transpose_qv:
        t_shape = (bsz, N_HEADS, nt, HEAD_PAD, tile)
        t_spec = pl.BlockSpec((None, N_HEADS, None, HEAD_PAD, tile), lambda b, i: (b, 0, i, 0, 0))
    else:
        t_shape, t_spec = head_shape, head_spec
    row_spec = lambda w: pl.BlockSpec((None, tile, w), lambda b, i: (b, i, 0))
    return pl.pallas_call(
        functools.partial(_proj_kernel, transpose_qv=transpose_qv),
        grid=(bsz, nt),
        in_specs=[row_spec(D_MODEL), row_spec(HEAD_PAD), row_spec(HEAD_PAD),
                  _const_spec((1, D_MODEL)), _const_spec((D_MODEL, PW_END)),
                  _const_spec((1, Q_LORA_RANK)), _const_spec((Q_LORA_RANK, 2 * hw)),
                  _const_spec((1, KV_LORA_RANK)), _const_spec((KV_LORA_RANK, 2 * hw))],
        out_specs=[t_spec, head_spec, t_spec, row_spec(SSM_WIDTH)],
        out_shape=[jax.ShapeDtypeStruct(t_shape, _BF),
                   jax.ShapeDtypeStruct(head_shape, _BF),
                   jax.ShapeDtypeStruct(t_shape, _BF),
                   jax.ShapeDtypeStruct((bsz, seq, SSM_WIDTH), _BF)],
        compiler_params=pltpu.CompilerParams(
            dimension_semantics=("parallel", "parallel"), vmem_limit_bytes=VMEM_LIMIT),
        name="proj",
    )(x, cos_t, sin_t, gpre, win, gq, wq, gkv, wkv)


def _attn_kernel(qt_ref, k_ref, vt_ref, km_ref, vtm_ref, o_ref, s0_scr, s1_scr, m_scr, acc_scr, *, nk, tk):
    tq = qt_ref.shape[2]
    key_row = lax.broadcasted_iota(jnp.int32, (LANES, tq), 0)
    for hh in range(2):
        s0 = jnp.where(key_row < N_META, _dot(km_ref[hh], qt_ref[hh]), -jnp.inf)
        m0 = jnp.max(s0, axis=0, keepdims=True)
        m_scr[hh] = m0
        acc_scr[hh] = _dot(vtm_ref[hh], jnp.exp2(s0 - m0).astype(_BF))

    def scores(buf, c):
        off = pl.multiple_of(c * tk, tk)
        for hh in range(2):
            buf[hh] = _dot(k_ref[hh, pl.ds(off, tk), :], qt_ref[hh])

    def accumulate(buf, c):
        for hh in range(2):
            s = buf[hh]
            m = m_scr[hh]
            m_new = jnp.maximum(m, jnp.max(s, axis=0, keepdims=True))
            m_scr[hh] = m_new
            p = jnp.exp2(s - m_new).astype(_BF)
            acc_scr[hh] = jnp.exp2(m - m_new) * acc_scr[hh] + _dot(vt_ref[hh, c], p)

    scores(s0_scr, 0)

    def body(t, _):
        scores(s1_scr, 2 * t + 1)
        accumulate(s0_scr, 2 * t)
        scores(s0_scr, 2 * t + 2)
        accumulate(s1_scr, 2 * t + 1)
        return 0

    lax.fori_loop(0, nk // 2 - 1, body, 0)
    scores(s1_scr, nk - 1)
    accumulate(s0_scr, nk - 2)
    accumulate(s1_scr, nk - 1)
    halves = []
    for hh in range(2):
        acc = acc_scr[hh]
        halves.append((acc * (1.0 / acc[ONES_LANE:ONES_LANE + 1, :]))[:V_HEAD_DIM])
    o_ref[...] = jnp.concatenate(halves, axis=0).T


def _attn_call(qt, k, vt, km, vtm):
    bsz, _, nq, _, tq = qt.shape
    nk, tk = vt.shape[2], vt.shape[4]
    seq = k.shape[2]
    return pl.pallas_call(
        functools.partial(_attn_kernel, nk=nk, tk=tk),
        grid=(bsz, N_HEADS // 2, nq),
        in_specs=[pl.BlockSpec((None, 2, None, HEAD_PAD, tq), lambda b, hp, i: (b, hp, i, 0, 0)),
                  pl.BlockSpec((None, 2, seq, HEAD_PAD), lambda b, hp, i: (b, hp, 0, 0)),
                  pl.BlockSpec((None, 2, nk, HEAD_PAD, tk), lambda b, hp, i: (b, hp, 0, 0, 0)),
                  pl.BlockSpec((2, LANES, HEAD_PAD), lambda b, hp, i: (hp, 0, 0)),
                  pl.BlockSpec((2, HEAD_PAD, LANES), lambda b, hp, i: (hp, 0, 0))],
        out_specs=pl.BlockSpec((None, tq, LANES), lambda b, hp, i: (b, i, hp)),
        out_shape=jax.ShapeDtypeStruct((bsz, nq * tq, ATTN_WIDTH), _F32),
        scratch_shapes=[pltpu.VMEM((2, tk, tq), _F32), pltpu.VMEM((2, tk, tq), _F32),
                        pltpu.VMEM((2, 1, tq), _F32), pltpu.VMEM((2, HEAD_PAD, tq), _F32)],
        compiler_params=pltpu.CompilerParams(
            dimension_semantics=("parallel", "parallel", "arbitrary"), vmem_limit_bytes=VMEM_LIMIT),
        name="attn",
    )(qt, k, vt, km, vtm)


def _s5_kernel(u_ref, um_ref, m_ref, w_ref, v_ref, ar_ref, ai_ref, y_ref, s_scr, x_scr, *, nc):
    u = u_ref[...]
    s_scr[...] = _dot(u, w_ref[...])
    half = STATE_W // 2
    lane = lax.broadcasted_iota(jnp.int32, (SUBLANES, half), 1)
    fwd = lane < SSM_STATE
    sm = _dot(um_ref[...], w_ref[...])
    xr0 = jnp.where(fwd, sm[:, :half], 0.0)
    xi0 = jnp.where(fwd, sm[:, half:], 0.0)
    ar = ar_ref[...]
    ai = ai_ref[...]

    def body(j, carry):
        xr, xi = carry
        rf = pl.multiple_of(j * SUBLANES, SUBLANES)
        rb = pl.multiple_of((nc - 1 - j) * SUBLANES, SUBLANES)
        x_scr[pl.ds(rf, SUBLANES), 0:SSM_STATE] = xr[:, 0:SSM_STATE]
        x_scr[pl.ds(rb, SUBLANES), SSM_STATE:half] = xr[:, SSM_STATE:half]
        x_scr[pl.ds(rf, SUBLANES), half:half + SSM_STATE] = xi[:, 0:SSM_STATE]
        x_scr[pl.ds(rb, SUBLANES), half + SSM_STATE:STATE_W] = xi[:, SSM_STATE:half]
        sf = s_scr[pl.ds(rf, SUBLANES), :]
        sb = s_scr[pl.ds(rb, SUBLANES), :]
        sr = jnp.where(fwd, sf[:, :half], sb[:, :half])
        si = jnp.where(fwd, sf[:, half:], sb[:, half:])
        return ar * xr - ai * xi + sr, ar * xi + ai * xr + si

    lax.fori_loop(0, nc, body, (xr0, xi0))
    y_ref[...] = _dot(u, m_ref[...]) + _dot(x_scr[...].astype(_BF), v_ref[...])


def _s5_call(ug, um, m_mat, w_mat, v_mat, ar, ai, *, nc):
    rows = nc * SUBLANES
    g_spec = lambda *shape: pl.BlockSpec((None,) + shape, lambda g: (g,) + (0,) * len(shape))
    return pl.pallas_call(
        functools.partial(_s5_kernel, nc=nc),
        grid=(SSM_GROUPS,),
        in_specs=[g_spec(rows, CHUNK_W), g_spec(SUBLANES, CHUNK_W), g_spec(CHUNK_W, CHUNK_W),
                  g_spec(CHUNK_W, STATE_W), g_spec(STATE_W, CHUNK_W),
                  g_spec(SUBLANES, STATE_W // 2), g_spec(SUBLANES, STATE_W // 2)],
        out_specs=g_spec(rows, CHUNK_W),
        out_shape=jax.ShapeDtypeStruct((SSM_GROUPS, rows, CHUNK_W), _F32),
        scratch_shapes=[pltpu.VMEM((rows, STATE_W), _F32), pltpu.VMEM((rows, STATE_W), _F32)],
        compiler_params=pltpu.CompilerParams(
            dimension_semantics=("parallel",), vmem_limit_bytes=VMEM_LIMIT),
        name="s5",
    )(ug, um, m_mat, w_mat, v_mat, ar, ai)


def _s5_matrices(a_re, a_im, log_dt, b_re, b_im, c_re, c_im, d_skip):
    tc = SSM_CHUNK
    lam = lax.complex(jnp.minimum(a_re.astype(_F32), -1e-4), a_im.astype(_F32))
    dt = jnp.exp(log_dt.astype(_F32))[..., None]
    lam_dt = lam * dt
    lam_bar = jnp.exp(lam_dt)
    b_bar = ((lam_bar - 1.0) / lam)[..., None] * lax.complex(b_re.astype(_F32), b_im.astype(_F32))
    c_c = lax.complex(c_re.astype(_F32), c_im.astype(_F32))
    k_idx = jnp.arange(tc + 1, dtype=_F32)
    pw = jnp.exp(lam_dt[:, :, None, :] * k_idx[None, None, :, None])
    kern = jnp.real(jnp.einsum('dgop,dgkp,dgpi->dgkoi', c_c, pw[:, :, :tc], b_bar))
    s_idx = jnp.arange(tc)[:, None]
    t_idx = jnp.arange(tc)[None, :]
    kf = jnp.where((t_idx >= s_idx)[None, :, :, None, None], kern[0][:, jnp.clip(t_idx - s_idx, 0, tc - 1)], 0.0)
    kb = jnp.where((s_idx >= t_idx)[None, :, :, None, None], kern[1][:, jnp.clip(s_idx - t_idx, 0, tc - 1)], 0.0)
    d_g = d_skip.astype(_F32).reshape(SSM_GROUPS, SSM_GROUP)
    skip = (jnp.eye(tc, dtype=_F32)[None, :, :, None, None]
            * (jnp.eye(SSM_GROUP, dtype=_F32)[None] * d_g[:, :, None])[:, None, None, :, :])
    m_full = kf + kb + skip
    m_mat = m_full.transpose(0, 1, 4, 2, 3).reshape(SSM_GROUPS, CHUNK_W, CHUNK_W)
    pw_f = pw[0][:, tc - 1 - jnp.arange(tc)]
    pw_b = pw[1][:, jnp.arange(tc)]
    wf = pw_f[:, :, None, :] * b_bar[0].transpose(0, 2, 1)[:, None, :, :]
    wb = pw_b[:, :, None, :] * b_bar[1].transpose(0, 2, 1)[:, None, :, :]
    w_mat = jnp.concatenate([jnp.real(wf), jnp.real(wb), jnp.imag(wf), jnp.imag(wb)], axis=-1)
    w_mat = w_mat.reshape(SSM_GROUPS, CHUNK_W, STATE_W)
    gf = pw[0][:, 1 + jnp.arange(tc)][:, :, None, :] * c_c[0][:, None, :, :]
    gb = pw[1][:, tc - jnp.arange(tc)][:, :, None, :] * c_c[1][:, None, :, :]
    v_mat = jnp.concatenate([jnp.real(gf), jnp.real(gb), -jnp.imag(gf), -jnp.imag(gb)], axis=-1)
    v_mat = v_mat.reshape(SSM_GROUPS, CHUNK_W, STATE_W).transpose(0, 2, 1)
    a_chunk = pw[:, :, tc]
    ar = jnp.concatenate([jnp.real(a_chunk[0]), jnp.real(a_chunk[1])], axis=-1)
    ai = jnp.concatenate([jnp.imag(a_chunk[0]), jnp.imag(a_chunk[1])], axis=-1)
    return m_mat, w_mat, v_mat, ar, ai


def _post_kernel(x_ref, attn_ref, y_ref, wglu_ref, gmix_ref, wout_ref, gpm_ref, gpre_ref,
                 wup_ref, wdn_ref, gpost_ref, o_ref):
    y = y_ref[...]
    gy = 0.5 * y * (1.0 + jnp.tanh(math.sqrt(2.0 / math.pi) * (y + 0.044715 * (y * y * y))))
    z = _dot(gy.astype(_BF), wglu_ref[...])
    ssm = z[:, :SSM_WIDTH] * (1.0 / (1.0 + jnp.exp(-z[:, SSM_WIDTH:])))
    gmix = gmix_ref[...]
    mix = jnp.concatenate([_rms(attn_ref[...], gmix[:, :ATTN_WIDTH]),
                           _rms(ssm, gmix[:, ATTN_WIDTH:])], axis=-1).astype(_BF)
    h1 = x_ref[...] + _rms(_dot(mix, wout_ref[...]), gpm_ref[...])
    hn = _rms(h1, gpre_ref[...]).astype(_BF)
    acc = None
    for c in range(D_FF // FF_TILE):
        up = _dot(hn, wup_ref[:, c * FF_TILE:(c + 1) * FF_TILE])
        up = jnp.maximum(up, 0.0)
        part = _dot((up * up).astype(_BF), wdn_ref[c * FF_TILE:(c + 1) * FF_TILE, :])
        acc = part if acc is None else acc + part
    o_ref[...] = h1 + _rms(acc, gpost_ref[...])


def _post_call(x, attn, y, wglu, gmix, wout, gpm, gpre, wup, wdn, gpost, *, tile):
    bsz, seq, _ = x.shape
    row_spec = lambda w: pl.BlockSpec((None, tile, w), lambda b, i: (b, i, 0))
    wspec = lambda shape: pl.BlockSpec(shape, lambda b, i: (0, 0), pipeline_mode=pl.Buffered(1))
    return pl.pallas_call(
        _post_kernel,
        grid=(bsz, seq // tile),
        in_specs=[row_spec(D_MODEL), row_spec(ATTN_WIDTH), row_spec(SSM_WIDTH),
                  wspec((SSM_WIDTH, 2 * SSM_WIDTH)), wspec((1, D_MODEL)), wspec((D_MODEL, D_MODEL)),
                  wspec((1, D_MODEL)), wspec((1, D_MODEL)), wspec((D_MODEL, D_FF)),
                  wspec((D_FF, D_MODEL)), wspec((1, D_MODEL))],
        out_specs=row_spec(D_MODEL),
        out_shape=jax.ShapeDtypeStruct((bsz, seq, D_MODEL), _F32),
        compiler_params=pltpu.CompilerParams(
            dimension_semantics=("parallel", "parallel"), vmem_limit_bytes=VMEM_LIMIT),
        name="post",
    )(x, attn, y, wglu, gmix, wout, gpm, gpre, wup, wdn, gpost)


def _rope_partner(w):
    half = QK_ROPE_DIM // 2
    return jnp.concatenate([-w[..., half:], w[..., :half]], axis=-1)


def _rope_tables(pos):
    inv = 1.0 / (ROPE_BASE ** (jnp.arange(0, QK_ROPE_DIM, 2, dtype=_F32) / QK_ROPE_DIM))
    ang = pos.astype(_F32)[..., None] * inv
    cos, sin = jnp.cos(ang), jnp.sin(ang)
    lead = cos.shape[:-1]
    ones = jnp.ones(lead + (QK_NOPE_DIM,), _F32)
    zeros_n = jnp.zeros(lead + (QK_NOPE_DIM,), _F32)
    zeros_p = jnp.zeros(lead + (HEAD_PAD - QK_HEAD_DIM,), _F32)
    return (jnp.concatenate([ones, cos, cos, zeros_p], axis=-1),
            jnp.concatenate([zeros_n, sin, sin, zeros_p], axis=-1))


def _prep_weights(w_in, w_uq, w_ukv):
    scale = QK_HEAD_DIM ** -0.5 * math.log2(math.e)
    pad_lo = jnp.zeros((D_MODEL, QK_NOPE_DIM), _F32)
    pad_hi = jnp.zeros((D_MODEL, HEAD_PAD - QK_HEAD_DIM), _F32)
    w_kr = w_in[:, OFF_KR:OFF_U]
    win = jnp.concatenate([w_in[:, :OFF_KR],
                           pad_lo, w_kr, pad_hi,
                           pad_lo, _rope_partner(w_kr), pad_hi,
                           w_in[:, OFF_U:]], axis=1).astype(_BF)
    wq3 = w_uq.reshape(Q_LORA_RANK, N_HEADS, QK_HEAD_DIM) * scale
    zq = lambda n: jnp.zeros((Q_LORA_RANK, N_HEADS, n), _F32)
    q_plain = jnp.concatenate([wq3, zq(HEAD_PAD - QK_HEAD_DIM)], axis=-1)
    q_part = jnp.concatenate([zq(QK_NOPE_DIM), _rope_partner(wq3[..., QK_NOPE_DIM:]),
                              zq(HEAD_PAD - QK_HEAD_DIM)], axis=-1)
    wq = jnp.concatenate([q_plain.reshape(Q_LORA_RANK, -1), q_part.reshape(Q_LORA_RANK, -1)], axis=1).astype(_BF)
    wkv3 = w_ukv.reshape(KV_LORA_RANK, N_HEADS, QK_NOPE_DIM + V_HEAD_DIM)
    zk = lambda n: jnp.zeros((KV_LORA_RANK, N_HEADS, n), _F32)
    k_slab = jnp.concatenate([wkv3[..., :QK_NOPE_DIM], zk(HEAD_PAD - QK_NOPE_DIM)], axis=-1)
    v_slab = jnp.concatenate([wkv3[..., QK_NOPE_DIM:], zk(HEAD_PAD - V_HEAD_DIM)], axis=-1)
    wkv = jnp.concatenate([k_slab.reshape(KV_LORA_RANK, -1), v_slab.reshape(KV_LORA_RANK, -1)], axis=1).astype(_BF)
    return win, wq, wkv


def kernel(x, positions, meta_tokens, g_pre_mix, w_in, g_q_lat, w_uq, g_kv_lat, w_ukv,
           ssm_A_re, ssm_A_im, ssm_log_dt, ssm_B_re, ssm_B_im, ssm_C_re, ssm_C_im, ssm_D,
           w_glu, g_mix_out, w_out, g_post_mix, g_pre_mlp, w_mlp_up, w_mlp_down, g_post_mlp):
    bsz, seq, _ = x.shape
    assert seq % ROW_TILE == 0 and seq % SSM_CHUNK == 0 and bsz <= SUBLANES // 2
    assert N_META <= SSM_CHUNK
    row = lambda g: g.reshape(1, -1).astype(_F32)

    win, wq, wkv = _prep_weights(w_in[0], w_uq[0], w_ukv[0])
    cos_x, sin_x = _rope_tables(positions.astype(jnp.int32) + N_META)
    cos_m, sin_m = _rope_tables(jnp.arange(N_META, dtype=jnp.int32)[None])
    norm_args = (row(g_pre_mix[0]), win, row(g_q_lat[0]), wq, row(g_kv_lat[0]), wkv)

    qt, k, vt, u = _proj_call(x, cos_x, sin_x, *norm_args, tile=ROW_TILE, transpose_qv=True)
    _, k_m, v_m, u_m = _proj_call(meta_tokens.astype(x.dtype)[None], cos_m, sin_m, *norm_args,
                                  tile=N_META, transpose_qv=False)
    km = jnp.pad(k_m[0], ((0, 0), (0, LANES - N_META), (0, 0)))
    vtm = jnp.pad(v_m[0].transpose(0, 2, 1), ((0, 0), (0, 0), (0, LANES - N_META)))
    attn = _attn_call(qt, k, vt, km, vtm)

    nc = seq // SSM_CHUNK
    ug = u.reshape(bsz, nc, SSM_CHUNK, SSM_GROUPS, SSM_GROUP).transpose(3, 1, 0, 2, 4)
    ug = jnp.pad(ug, ((0, 0), (0, 0), (0, SUBLANES - bsz), (0, 0), (0, 0)))
    ug = ug.reshape(SSM_GROUPS, nc * SUBLANES, CHUNK_W)
    um = u_m[0].reshape(N_META, SSM_GROUPS, SSM_GROUP).transpose(1, 0, 2)
    um = jnp.pad(um, ((0, 0), (SSM_CHUNK - N_META, 0), (0, 0))).reshape(SSM_GROUPS, 1, CHUNK_W)
    um = jnp.concatenate([jnp.broadcast_to(um, (SSM_GROUPS, bsz, CHUNK_W)),
                          jnp.zeros((SSM_GROUPS, SUBLANES - bsz, CHUNK_W), um.dtype)], axis=1)
    m_mat, w_mat, v_mat, ar, ai = _s5_matrices(ssm_A_re[0], ssm_A_im[0], ssm_log_dt[0], ssm_B_re[0],
                                               ssm_B_im[0], ssm_C_re[0], ssm_C_im[0], ssm_D[0])
    bc = lambda a: jnp.broadcast_to(a[:, None, :], (SSM_GROUPS, SUBLANES, a.shape[-1]))
    yg = _s5_call(ug, um, m_mat.astype(_BF), w_mat.astype(_BF), v_mat.astype(_BF), bc(ar), bc(ai), nc=nc)
    y = yg.reshape(SSM_GROUPS, nc, SUBLANES, SSM_CHUNK, SSM_GROUP)[:, :, :bsz]
    y = y.transpose(2, 1, 3, 0, 4).reshape(bsz, seq, SSM_WIDTH)

    return _post_call(x, attn, y, w_glu[0].astype(_BF), row(g_mix_out[0]), w_out[0].astype(_BF),
                      row(g_post_mix[0]), row(g_pre_mlp[0]), w_mlp_up[0].astype(_BF),
                      w_mlp_down[0].astype(_BF), row(g_post_mlp[0]), tile=ROW_TILE)
```

```python
import functools
import math

import jax
import jax.numpy as jnp
from jax import lax
from jax.experimental import pallas as pl
from jax.experimental.pallas import tpu as pltpu

D_MODEL = 1024
N_META = 16
ATTN_WIDTH = 512
SSM_WIDTH = 512
N_HEADS = 8
V_HEAD_DIM = 64
QK_NOPE_DIM = 64
QK_ROPE_DIM = 32
QK_HEAD_DIM = QK_NOPE_DIM + QK_ROPE_DIM
Q_LORA_RANK = 384
KV_LORA_RANK = 256
ROPE_BASE = 10000.0
SSM_GROUP = 16
SSM_GROUPS = 32
SSM_STATE = 64
D_FF = 4 * D_MODEL
EPS = 1e-6
OFF_KV = Q_LORA_RANK
OFF_KR = OFF_KV + KV_LORA_RANK
OFF_U = OFF_KR + QK_ROPE_DIM

LANES = 128
SUBLANES = 8
HEAD_PAD = LANES
ONES_LANE = V_HEAD_DIM
PW_Q = 0
PW_KV = PW_Q + Q_LORA_RANK
PW_KRA = PW_KV + KV_LORA_RANK
PW_KRB = PW_KRA + HEAD_PAD
PW_END = PW_KRB + HEAD_PAD

SSM_CHUNK = 32
CHUNK_W = SSM_CHUNK * SSM_GROUP
STATE_W = 4 * SSM_STATE
SLAB_T = LANES
SLAB_CHUNKS = SLAB_T // SSM_CHUNK
SLAB_W = SSM_GROUP * SLAB_T

ROW_TILE = 512
FF_TILE = 1024
VMEM_LIMIT = 56 * 1024 * 1024

_BF = jnp.bfloat16
_F32 = jnp.float32


def _dot(a, b):
    return jnp.dot(a, b, preferred_element_type=_F32)


def _rms(x, g):
    return x * lax.rsqrt(jnp.mean(x * x, axis=-1, keepdims=True) + EPS) * g


def _proj_kernel(x_ref, cos_ref, sin_ref, gpre_ref, win_ref, wu_ref, gq_ref, wq_ref, gkv_ref, wkv_ref,
                 q_ref, k_ref, v_ref, u_ref, *, slabs):
    x = x_ref[...]
    xn = _rms(x, gpre_ref[...]).astype(_BF)
    proj = _dot(xn, win_ref[...])
    cq = proj[:, PW_Q:PW_KV]
    ckv = proj[:, PW_KV:PW_KRA]
    cos = cos_ref[...]
    sin = sin_ref[...]
    kr = proj[:, PW_KRA:PW_KRB] * cos + proj[:, PW_KRB:PW_END] * sin
    if slabs:
        ut = lax.dot_general(wu_ref[...], xn, (((1,), (1,)), ((), ())), preferred_element_type=_F32)
        for g in range(SSM_GROUPS):
            for c in range(ut.shape[1] // SLAB_T):
                u_ref[g, c] = ut[g * SSM_GROUP:(g + 1) * SSM_GROUP, c * SLAB_T:(c + 1) * SLAB_T]
    else:
        u_ref[...] = lax.dot_general(xn, wu_ref[...], (((1,), (1,)), ((), ())), preferred_element_type=_F32)
    qq = _dot(_rms(cq, gq_ref[...]).astype(_BF), wq_ref[...])
    kv = _dot(_rms(ckv, gkv_ref[...]).astype(_BF), wkv_ref[...])
    lane = lax.broadcasted_iota(jnp.int32, (1, HEAD_PAD), 1)
    ones_col = (lane == ONES_LANE).astype(_F32)
    hw = N_HEADS * HEAD_PAD
    for h in range(N_HEADS):
        lo = h * HEAD_PAD
        q_h = qq[:, lo:lo + HEAD_PAD] * cos + qq[:, hw + lo:hw + lo + HEAD_PAD] * sin
        k_h = kv[:, lo:lo + HEAD_PAD] + kr
        v_h = kv[:, hw + lo:hw + lo + HEAD_PAD] + ones_col
        k_ref[h] = k_h.astype(_BF)
        if slabs:
            q_ref[h] = q_h.T.astype(_BF)
            v_ref[h] = v_h.T.astype(_BF)
        else:
            q_ref[h] = q_h.astype(_BF)
            v_ref[h] = v_h.astype(_BF)


def _const_spec(shape):
    nd = len(shape)
    return pl.BlockSpec(shape, lambda *_: (0,) * nd)


def _proj_call(x, cos_t, sin_t, gpre, win, wu_t, gq, wq, gkv, wkv, *, tile, slabs):
    bsz, seq, _ = x.shape
    nt = seq // tile
    hw = N_HEADS * HEAD_PAD
    head_shape = (bsz, N_HEADS, seq, HEAD_PAD)
    head_spec = pl.BlockSpec((None, N_HEADS, tile, HEAD_PAD), lambda b, i: (b, 0, i, 0))
    row_spec = lambda w: pl.BlockSpec((None, tile, w), lambda b, i: (b, i, 0))
    if slabs:
        t_shape = (bsz, N_HEADS, nt, HEAD_PAD, tile)
        t_spec = pl.BlockSpec((None, N_HEADS, None, HEAD_PAD, tile), lambda b, i: (b, 0, i, 0, 0))
        u_shape = (SSM_GROUPS, seq // SLAB_T, bsz * SSM_GROUP, SLAB_T)
        u_spec = pl.BlockSpec((SSM_GROUPS, tile // SLAB_T, SSM_GROUP, SLAB_T), lambda b, i: (0, i, b, 0))
    else:
        t_shape, t_spec = head_shape, head_spec
        u_shape, u_spec = (bsz, seq, SSM_WIDTH), row_spec(SSM_WIDTH)
    return pl.pallas_call(
        functools.partial(_proj_kernel, slabs=slabs),
        grid=(bsz, nt),
        in_specs=[row_spec(D_MODEL), row_spec(HEAD_PAD), row_spec(HEAD_PAD),
                  _const_spec((1, D_MODEL)), _const_spec((D_MODEL, PW_END)), _const_spec((SSM_WIDTH, D_MODEL)),
                  _const_spec((1, Q_LORA_RANK)), _const_spec((Q_LORA_RANK, 2 * hw)),
                  _const_spec((1, KV_LORA_RANK)), _const_spec((KV_LORA_RANK, 2 * hw))],
        out_specs=[t_spec, head_spec, t_spec, u_spec],
        out_shape=[jax.ShapeDtypeStruct(t_shape, _BF),
                   jax.ShapeDtypeStruct(head_shape, _BF),
                   jax.ShapeDtypeStruct(t_shape, _BF),
                   jax.ShapeDtypeStruct(u_shape, _F32)],
        compiler_params=pltpu.CompilerParams(
            dimension_semantics=("parallel", "parallel"), vmem_limit_bytes=VMEM_LIMIT),
        name="proj",
    )(x, cos_t, sin_t, gpre, win, wu_t, gq, wq, gkv, wkv)


def _attn_kernel(qt_ref, k_ref, vt_ref, km_ref, vtm_ref, o_ref, s0_scr, s1_scr, m_scr, acc_scr, *, nk, tk):
    tq = qt_ref.shape[2]
    key_row = lax.broadcasted_iota(jnp.int32, (LANES, tq), 0)
    for hh in range(2):
        s0 = jnp.where(key_row < N_META, _dot(km_ref[hh], qt_ref[hh]), -jnp.inf)
        m0 = jnp.max(s0, axis=0, keepdims=True)
        m_scr[hh] = m0
        acc_scr[hh] = _dot(vtm_ref[hh], jnp.exp2(s0 - m0).astype(_BF))

    def scores(buf, c):
        off = pl.multiple_of(c * tk, tk)
        for hh in range(2):
            buf[hh] = _dot(k_ref[hh, pl.ds(off, tk), :], qt_ref[hh])

    def accumulate(buf, c):
        for hh in range(2):
            s = buf[hh]
            m = m_scr[hh]
            m_new = jnp.maximum(m, jnp.max(s, axis=0, keepdims=True))
            m_scr[hh] = m_new
            p = jnp.exp2(s - m_new).astype(_BF)
            acc_scr[hh] = jnp.exp2(m - m_new) * acc_scr[hh] + _dot(vt_ref[hh, c], p)

    scores(s0_scr, 0)

    def body(t, _):
        scores(s1_scr, 2 * t + 1)
        accumulate(s0_scr, 2 * t)
        scores(s0_scr, 2 * t + 2)
        accumulate(s1_scr, 2 * t + 1)
        return 0

    lax.fori_loop(0, nk // 2 - 1, body, 0)
    scores(s1_scr, nk - 1)
    accumulate(s0_scr, nk - 2)
    accumulate(s1_scr, nk - 1)
    halves = []
    for hh in range(2):
        acc = acc_scr[hh]
        halves.append((acc * (1.0 / acc[ONES_LANE:ONES_LANE + 1, :]))[:V_HEAD_DIM])
    o_ref[...] = jnp.concatenate(halves, axis=0).T


def _attn_call(qt, k, vt, km, vtm):
    bsz, _, nq, _, tq = qt.shape
    nk, tk = vt.shape[2], vt.shape[4]
    seq = k.shape[2]
    assert nk % 2 == 0 and nk >= 4
    return pl.pallas_call(
        functools.partial(_attn_kernel, nk=nk, tk=tk),
        grid=(bsz, N_HEADS // 2, nq),
        in_specs=[pl.BlockSpec((None, 2, None, HEAD_PAD, tq), lambda b, hp, i: (b, hp, i, 0, 0)),
                  pl.BlockSpec((None, 2, seq, HEAD_PAD), lambda b, hp, i: (b, hp, 0, 0)),
                  pl.BlockSpec((None, 2, nk, HEAD_PAD, tk), lambda b, hp, i: (b, hp, 0, 0, 0)),
                  pl.BlockSpec((2, LANES, HEAD_PAD), lambda b, hp, i: (hp, 0, 0)),
                  pl.BlockSpec((2, HEAD_PAD, LANES), lambda b, hp, i: (hp, 0, 0))],
        out_specs=pl.BlockSpec((None, tq, LANES), lambda b, hp, i: (b, i, hp)),
        out_shape=jax.ShapeDtypeStruct((bsz, nq * tq, ATTN_WIDTH), _F32),
        scratch_shapes=[pltpu.VMEM((2, tk, tq), _F32), pltpu.VMEM((2, tk, tq), _F32),
                        pltpu.VMEM((2, 1, tq), _F32), pltpu.VMEM((2, HEAD_PAD, tq), _F32)],
        compiler_params=pltpu.CompilerParams(
            dimension_semantics=("parallel", "parallel", "arbitrary"), vmem_limit_bytes=VMEM_LIMIT),
        name="attn",
    )(qt, k, vt, km, vtm)


def _cmul_add(ar, ai, xr, xi, sr, si):
    return ar * xr - ai * xi + sr, ar * xi + ai * xr + si


def _s5_kernel(a_ref, um_ref, pin_ref, pout_ref, m_ref, w_ref, v_ref, t_ref, y_ref, sup_scr, ent_scr,
               *, nslab, bsz):
    rows = nslab * bsz
    half = STATE_W // 2
    a = jnp.concatenate([a_ref[pl.ds(i, rows, stride=SSM_GROUP), :] for i in range(SSM_GROUP)], axis=1)
    ap = _dot(a.astype(_BF), pin_ref[...]).astype(_BF)
    uc = [ap[:, c * CHUNK_W:(c + 1) * CHUNK_W] for c in range(SLAB_CHUNKS)]
    w = w_ref[...]
    s = [_dot(u, w) for u in uc]
    sr = [x[:, :half] for x in s]
    si = [x[:, half:] for x in s]
    t = t_ref[...]
    trow = lambda r: t[r:r + 1, :]
    sup_r = sup_i = None
    for c in range(SLAB_CHUNKS):
        cr, ci = trow(c), trow(SLAB_CHUNKS + c)
        pr = cr * sr[c] - ci * si[c]
        pi = cr * si[c] + ci * sr[c]
        sup_r = pr if sup_r is None else sup_r + pr
        sup_i = pi if sup_i is None else sup_i + pi
    sup_scr[:, :half] = sup_r
    sup_scr[:, half:] = sup_i

    lane = lax.broadcasted_iota(jnp.int32, (bsz, half), 1)
    fwd = lane < SSM_STATE
    sm = _dot(um_ref[...], w)
    xr = jnp.where(fwd, sm[:bsz, :half], 0.0)
    xi = jnp.where(fwd, sm[:bsz, half:], 0.0)
    a_slab_r, a_slab_i = trow(2 * SLAB_CHUNKS), trow(2 * SLAB_CHUNKS + 1)
    for j in range(nslab):
        rf = j * bsz
        rb = (nslab - 1 - j) * bsz
        ent_scr[rf:rf + bsz, 0:SSM_STATE] = xr[:, 0:SSM_STATE]
        ent_scr[rb:rb + bsz, SSM_STATE:half] = xr[:, SSM_STATE:half]
        ent_scr[rf:rf + bsz, half:half + SSM_STATE] = xi[:, 0:SSM_STATE]
        ent_scr[rb:rb + bsz, half + SSM_STATE:STATE_W] = xi[:, SSM_STATE:half]
        s_r = jnp.where(fwd, sup_scr[rf:rf + bsz, :half], sup_scr[rb:rb + bsz, :half])
        s_i = jnp.where(fwd, sup_scr[rf:rf + bsz, half:], sup_scr[rb:rb + bsz, half:])
        xr, xi = _cmul_add(a_slab_r, a_slab_i, xr, xi, s_r, s_i)

    ent = ent_scr[...]
    a_r, a_i = trow(2 * SLAB_CHUNKS + 2), trow(2 * SLAB_CHUNKS + 3)
    xf = [(ent[:, :half], ent[:, half:])]
    for c in range(1, SLAB_CHUNKS):
        xf.append(_cmul_add(a_r, a_i, xf[-1][0], xf[-1][1], sr[c - 1], si[c - 1]))
    xb = [(ent[:, :half], ent[:, half:])]
    for c in range(SLAB_CHUNKS - 2, -1, -1):
        xb.insert(0, _cmul_add(a_r, a_i, xb[0][0], xb[0][1], sr[c + 1], si[c + 1]))
    fwd_rows = lax.broadcasted_iota(jnp.int32, (rows, half), 1) < SSM_STATE
    m = m_ref[...]
    v = v_ref[...]
    ys = []
    for c in range(SLAB_CHUNKS):
        xin = jnp.concatenate([jnp.where(fwd_rows, xf[c][0], xb[c][0]),
                               jnp.where(fwd_rows, xf[c][1], xb[c][1])], axis=1).astype(_BF)
        ys.append((_dot(uc[c], m) + _dot(xin, v)).astype(_BF))
    yp = _dot(jnp.concatenate(ys, axis=1), pout_ref[...])
    for o in range(SSM_GROUP):
        y_ref[pl.ds(o, rows, stride=SSM_GROUP), :] = yp[:, o * SLAB_T:(o + 1) * SLAB_T]


def _s5_call(a, um, pin, pout, m_mat, w_mat, v_mat, tab, *, nslab, bsz):
    n = nslab * bsz * SSM_GROUP
    rows = nslab * bsz
    g_spec = lambda *shape: pl.BlockSpec((None,) + shape, lambda g: (g,) + (0,) * len(shape))
    perm_spec = pl.BlockSpec((SLAB_W, SLAB_W), lambda g: (0, 0), pipeline_mode=pl.Buffered(1))
    return pl.pallas_call(
        functools.partial(_s5_kernel, nslab=nslab, bsz=bsz),
        grid=(SSM_GROUPS,),
        in_specs=[g_spec(n, SLAB_T), g_spec(SUBLANES, CHUNK_W), perm_spec, perm_spec,
                  g_spec(CHUNK_W, CHUNK_W), g_spec(CHUNK_W, STATE_W), g_spec(STATE_W, CHUNK_W),
                  g_spec(2 * SUBLANES, STATE_W // 2)],
        out_specs=g_spec(n, SLAB_T),
        out_shape=jax.ShapeDtypeStruct((SSM_GROUPS, n, SLAB_T), _F32),
        scratch_shapes=[pltpu.VMEM((rows, STATE_W), _F32), pltpu.VMEM((rows, STATE_W), _F32)],
        compiler_params=pltpu.CompilerParams(
            dimension_semantics=("parallel",), vmem_limit_bytes=VMEM_LIMIT),
        name="s5",
    )(a, um, pin, pout, m_mat, w_mat, v_mat, tab)


def _s5_matrices(a_re, a_im, log_dt, b_re, b_im, c_re, c_im, d_skip):
    tc = SSM_CHUNK
    lam = lax.complex(jnp.minimum(a_re.astype(_F32), -1e-4), a_im.astype(_F32))
    dt = jnp.exp(log_dt.astype(_F32))[..., None]
    lam_dt = lam * dt
    lam_bar = jnp.exp(lam_dt)
    b_bar = ((lam_bar - 1.0) / lam)[..., None] * lax.complex(b_re.astype(_F32), b_im.astype(_F32))
    c_c = lax.complex(c_re.astype(_F32), c_im.astype(_F32))
    k_idx = jnp.arange(tc + 1, dtype=_F32)
    pw = jnp.exp(lam_dt[:, :, None, :] * k_idx[None, None, :, None])
    kern = jnp.real(jnp.einsum('dgop,dgkp,dgpi->dgkoi', c_c, pw[:, :, :tc], b_bar))
    s_idx = jnp.arange(tc)[:, None]
    t_idx = jnp.arange(tc)[None, :]
    kf = jnp.where((t_idx >= s_idx)[None, :, :, None, None], kern[0][:, jnp.clip(t_idx - s_idx, 0, tc - 1)], 0.0)
    kb = jnp.where((s_idx >= t_idx)[None, :, :, None, None], kern[1][:, jnp.clip(s_idx - t_idx, 0, tc - 1)], 0.0)
    d_g = d_skip.astype(_F32).reshape(SSM_GROUPS, SSM_GROUP)
    skip = (jnp.eye(tc, dtype=_F32)[None, :, :, None, None]
            * (jnp.eye(SSM_GROUP, dtype=_F32)[None] * d_g[:, :, None])[:, None, None, :, :])
    m_full = kf + kb + skip
    m_mat = m_full.transpose(0, 1, 4, 2, 3).reshape(SSM_GROUPS, CHUNK_W, CHUNK_W)
    pw_f = pw[0][:, tc - 1 - jnp.arange(tc)]
    pw_b = pw[1][:, jnp.arange(tc)]
    wf = pw_f[:, :, None, :] * b_bar[0].transpose(0, 2, 1)[:, None, :, :]
    wb = pw_b[:, :, None, :] * b_bar[1].transpose(0, 2, 1)[:, None, :, :]
    w_mat = jnp.concatenate([jnp.real(wf), jnp.real(wb), jnp.imag(wf), jnp.imag(wb)], axis=-1)
    w_mat = w_mat.reshape(SSM_GROUPS, CHUNK_W, STATE_W)
    gf = pw[0][:, 1 + jnp.arange(tc)][:, :, None, :] * c_c[0][:, None, :, :]
    gb = pw[1][:, tc - jnp.arange(tc)][:, :, None, :] * c_c[1][:, None, :, :]
    v_mat = jnp.concatenate([jnp.real(gf), jnp.real(gb), -jnp.imag(gf), -jnp.imag(gb)], axis=-1)
    v_mat = v_mat.reshape(SSM_GROUPS, CHUNK_W, STATE_W).transpose(0, 2, 1)
    n_idx = jnp.arange(SLAB_CHUNKS + 1, dtype=_F32) * tc
    pc = jnp.exp(lam_dt[:, :, None, :] * n_idx[None, None, :, None])
    coef = jnp.concatenate([pc[0][:, SLAB_CHUNKS - 1 - jnp.arange(SLAB_CHUNKS)],
                            pc[1][:, jnp.arange(SLAB_CHUNKS)]], axis=-1)
    both = lambda n: jnp.concatenate([pc[0][:, n], pc[1][:, n]], axis=-1)[:, None, :]
    a_slab, a_chunk = both(SLAB_CHUNKS), both(1)
    tab = jnp.concatenate([jnp.real(coef), jnp.imag(coef), jnp.real(a_slab), jnp.imag(a_slab),
                           jnp.real(a_chunk), jnp.imag(a_chunk)], axis=1)
    tab = jnp.pad(tab, ((0, 0), (0, 2 * SUBLANES - tab.shape[1]), (0, 0)))
    return m_mat, w_mat, v_mat, tab


def _slab_permutation():
    r = lax.broadcasted_iota(jnp.int32, (SLAB_W, SLAB_W), 0)
    c = lax.broadcasted_iota(jnp.int32, (SLAB_W, SLAB_W), 1)
    return ((r // SLAB_T == c % SSM_GROUP) & (r % SLAB_T == c // SSM_GROUP)).astype(_BF)


def _post_kernel(x_ref, attn_ref, y_ref, wglu_ref, gmix_ref, wout_ref, gpm_ref, gpre_ref,
                 wup_ref, wdn_ref, gpost_ref, o_ref):
    nslab = y_ref.shape[1]
    yt = jnp.concatenate([jnp.concatenate([y_ref[g, c] for c in range(nslab)], axis=1)
                          for g in range(SSM_GROUPS)], axis=0)
    y = yt.T
    gy = 0.5 * y * (1.0 + jnp.tanh(math.sqrt(2.0 / math.pi) * (y + 0.044715 * (y * y * y))))
    z = _dot(gy.astype(_BF), wglu_ref[...])
    ssm = z[:, :SSM_WIDTH] * (1.0 / (1.0 + jnp.exp(-z[:, SSM_WIDTH:])))
    gmix = gmix_ref[...]
    mix = jnp.concatenate([_rms(attn_ref[...], gmix[:, :ATTN_WIDTH]),
                           _rms(ssm, gmix[:, ATTN_WIDTH:])], axis=-1).astype(_BF)
    h1 = x_ref[...] + _rms(_dot(mix, wout_ref[...]), gpm_ref[...])
    hn = _rms(h1, gpre_ref[...]).astype(_BF)
    acc = None
    for c in range(D_FF // FF_TILE):
        up = _dot(hn, wup_ref[:, c * FF_TILE:(c + 1) * FF_TILE])
        up = jnp.maximum(up, 0.0)
        part = _dot((up * up).astype(_BF), wdn_ref[c * FF_TILE:(c + 1) * FF_TILE, :])
        acc = part if acc is None else acc + part
    o_ref[...] = h1 + _rms(acc, gpost_ref[...])


def _post_call(x, attn, y, wglu, gmix, wout, gpm, gpre, wup, wdn, gpost, *, tile):
    bsz, seq, _ = x.shape
    row_spec = lambda w: pl.BlockSpec((None, tile, w), lambda b, i: (b, i, 0))
    wspec = lambda shape: pl.BlockSpec(shape, lambda b, i: (0, 0), pipeline_mode=pl.Buffered(1))
    y_spec = pl.BlockSpec((SSM_GROUPS, tile // SLAB_T, SSM_GROUP, SLAB_T), lambda b, i: (0, i, b, 0))
    return pl.pallas_call(
        _post_kernel,
        grid=(bsz, seq // tile),
        in_specs=[row_spec(D_MODEL), row_spec(ATTN_WIDTH), y_spec,
                  wspec((SSM_WIDTH, 2 * SSM_WIDTH)), wspec((1, D_MODEL)), wspec((D_MODEL, D_MODEL)),
                  wspec((1, D_MODEL)), wspec((1, D_MODEL)), wspec((D_MODEL, D_FF)),
                  wspec((D_FF, D_MODEL)), wspec((1, D_MODEL))],
        out_specs=row_spec(D_MODEL),
        out_shape=jax.ShapeDtypeStruct((bsz, seq, D_MODEL), _F32),
        compiler_params=pltpu.CompilerParams(
            dimension_semantics=("parallel", "parallel"), vmem_limit_bytes=VMEM_LIMIT),
        name="post",
    )(x, attn, y, wglu, gmix, wout, gpm, gpre, wup, wdn, gpost)


def _rope_partner(w):
    half = QK_ROPE_DIM // 2
    return jnp.concatenate([-w[..., half:], w[..., :half]], axis=-1)


def _rope_tables(pos):
    inv = 1.0 / (ROPE_BASE ** (jnp.arange(0, QK_ROPE_DIM, 2, dtype=_F32) / QK_ROPE_DIM))
    ang = pos.astype(_F32)[..., None] * inv
    cos, sin = jnp.cos(ang), jnp.sin(ang)
    lead = cos.shape[:-1]
    ones = jnp.ones(lead + (QK_NOPE_DIM,), _F32)
    zeros_n = jnp.zeros(lead + (QK_NOPE_DIM,), _F32)
    zeros_p = jnp.zeros(lead + (HEAD_PAD - QK_HEAD_DIM,), _F32)
    return (jnp.concatenate([ones, cos, cos, zeros_p], axis=-1),
            jnp.concatenate([zeros_n, sin, sin, zeros_p], axis=-1))


def _prep_weights(w_in, w_uq, w_ukv):
    scale = QK_HEAD_DIM ** -0.5 * math.log2(math.e)
    pad_lo = jnp.zeros((D_MODEL, QK_NOPE_DIM), _F32)
    pad_hi = jnp.zeros((D_MODEL, HEAD_PAD - QK_HEAD_DIM), _F32)
    w_kr = w_in[:, OFF_KR:OFF_U]
    win = jnp.concatenate([w_in[:, :OFF_KR],
                           pad_lo, w_kr, pad_hi,
                           pad_lo, _rope_partner(w_kr), pad_hi], axis=1).astype(_BF)
    wu_t = w_in[:, OFF_U:].T.astype(_BF)
    wq3 = w_uq.reshape(Q_LORA_RANK, N_HEADS, QK_HEAD_DIM) * scale
    zq = lambda n: jnp.zeros((Q_LORA_RANK, N_HEADS, n), _F32)
    q_plain = jnp.concatenate([wq3, zq(HEAD_PAD - QK_HEAD_DIM)], axis=-1)
    q_part = jnp.concatenate([zq(QK_NOPE_DIM), _rope_partner(wq3[..., QK_NOPE_DIM:]),
                              zq(HEAD_PAD - QK_HEAD_DIM)], axis=-1)
    wq = jnp.concatenate([q_plain.reshape(Q_LORA_RANK, -1), q_part.reshape(Q_LORA_RANK, -1)], axis=1).astype(_BF)
    wkv3 = w_ukv.reshape(KV_LORA_RANK, N_HEADS, QK_NOPE_DIM + V_HEAD_DIM)
    zk = lambda n: jnp.zeros((KV_LORA_RANK, N_HEADS, n), _F32)
    k_slab = jnp.concatenate([wkv3[..., :QK_NOPE_DIM], zk(HEAD_PAD - QK_NOPE_DIM)], axis=-1)
    v_slab = jnp.concatenate([wkv3[..., QK_NOPE_DIM:], zk(HEAD_PAD - V_HEAD_DIM)], axis=-1)
    wkv = jnp.concatenate([k_slab.reshape(KV_LORA_RANK, -1), v_slab.reshape(KV_LORA_RANK, -1)], axis=1).astype(_BF)
    return win, wu_t, wq, wkv


def kernel(x, positions, meta_tokens, g_pre_mix, w_in, g_q_lat, w_uq, g_kv_lat, w_ukv,
           ssm_A_re, ssm_A_im, ssm_log_dt, ssm_B_re, ssm_B_im, ssm_C_re, ssm_C_im, ssm_D,
           w_glu, g_mix_out, w_out, g_post_mix, g_pre_mlp, w_mlp_up, w_mlp_down, g_post_mlp):
    bsz, seq, _ = x.shape
    assert seq % ROW_TILE == 0 and ROW_TILE % SLAB_T == 0 and bsz <= SUBLANES
    assert N_META <= SSM_CHUNK
    row = lambda g: g.reshape(1, -1).astype(_F32)

    win, wu_t, wq, wkv = _prep_weights(w_in[0], w_uq[0], w_ukv[0])
    cos_x, sin_x = _rope_tables(positions.astype(jnp.int32) + N_META)
    cos_m, sin_m = _rope_tables(jnp.arange(N_META, dtype=jnp.int32)[None])
    norm_args = (row(g_pre_mix[0]), win, wu_t, row(g_q_lat[0]), wq, row(g_kv_lat[0]), wkv)

    qt, k, vt, u = _proj_call(x, cos_x, sin_x, *norm_args, tile=ROW_TILE, slabs=True)
    _, k_m, v_m, u_m = _proj_call(meta_tokens.astype(x.dtype)[None], cos_m, sin_m, *norm_args,
                                  tile=N_META, slabs=False)
    km = jnp.pad(k_m[0], ((0, 0), (0, LANES - N_META), (0, 0)))
    vtm = jnp.pad(v_m[0].transpose(0, 2, 1), ((0, 0), (0, 0), (0, LANES - N_META)))
    attn = _attn_call(qt, k, vt, km, vtm)

    um = u_m[0].astype(_BF).reshape(N_META, SSM_GROUPS, SSM_GROUP).transpose(1, 0, 2)
    um = jnp.pad(um, ((0, 0), (SSM_CHUNK - N_META, 0), (0, 0))).reshape(SSM_GROUPS, 1, CHUNK_W)
    um = jnp.broadcast_to(um, (SSM_GROUPS, SUBLANES, CHUNK_W))
    m_mat, w_mat, v_mat, tab = _s5_matrices(ssm_A_re[0], ssm_A_im[0], ssm_log_dt[0], ssm_B_re[0],
                                            ssm_B_im[0], ssm_C_re[0], ssm_C_im[0], ssm_D[0])
    pin = _slab_permutation()
    nslab = seq // SLAB_T
    yg = _s5_call(u.reshape(SSM_GROUPS, nslab * bsz * SSM_GROUP, SLAB_T), um, pin, pin.T,
                  m_mat.astype(_BF), w_mat.astype(_BF), v_mat.astype(_BF), tab, nslab=nslab, bsz=bsz)
    y = yg.reshape(SSM_GROUPS, nslab, bsz * SSM_GROUP, SLAB_T)

    return _post_call(x, attn, y, w_glu[0].astype(_BF), row(g_mix_out[0]), w_out[0].astype(_BF),
                      row(g_post_mix[0]), row(g_pre_mlp[0]), w_mlp_up[0].astype(_BF),
                      w_mlp_down[0].astype(_BF), row(g_post_mlp[0]), tile=ROW_TILE)
```

```python
import functools
import math

import jax
import jax.numpy as jnp
from jax import lax
from jax.experimental import pallas as pl
from jax.experimental.pallas import tpu as pltpu

D_MODEL = 1024
N_META = 16
ATTN_WIDTH = 512
SSM_WIDTH = 512
N_HEADS = 8
V_HEAD_DIM = 64
QK_NOPE_DIM = 64
QK_ROPE_DIM = 32
QK_HEAD_DIM = QK_NOPE_DIM + QK_ROPE_DIM
Q_LORA_RANK = 384
KV_LORA_RANK = 256
ROPE_BASE = 10000.0
SSM_GROUP = 16
SSM_GROUPS = 32
SSM_STATE = 64
D_FF = 4 * D_MODEL
EPS = 1e-6
OFF_KV = Q_LORA_RANK
OFF_KR = OFF_KV + KV_LORA_RANK
OFF_U = OFF_KR + QK_ROPE_DIM

LANES = 128
SUBLANES = 8
HEAD_PAD = LANES
ONES_LANE = V_HEAD_DIM
PW_Q = 0
PW_KV = PW_Q + Q_LORA_RANK
PW_KRA = PW_KV + KV_LORA_RANK
PW_KRB = PW_KRA + HEAD_PAD
PW_END = PW_KRB + HEAD_PAD

SSM_CHUNK = 32
CHUNK_W = SSM_CHUNK * SSM_GROUP
STATE_W = 4 * SSM_STATE
SLAB_T = LANES
SLAB_CHUNKS = SLAB_T // SSM_CHUNK
SLAB_W = SSM_GROUP * SLAB_T

ROW_TILE = 512
FF_TILE = 1024
BOUND_SLACK = 1.0 + 2.0 ** -6
BOUND_LIMIT = 60.0
VMEM_LIMIT = 56 * 1024 * 1024

_BF = jnp.bfloat16
_F32 = jnp.float32


def _dot(a, b):
    return jnp.dot(a, b, preferred_element_type=_F32)


def _rms(x, g):
    return x * lax.rsqrt(jnp.mean(x * x, axis=-1, keepdims=True) + EPS) * g


def _proj_kernel(x_ref, cos_ref, sin_ref, gpre_ref, win_ref, wu_ref, gq_ref, wq_ref, gkv_ref, wkv_ref,
                 q_ref, k_ref, v_ref, u_ref, *norm_refs, slabs):
    x = x_ref[...]
    xn = _rms(x, gpre_ref[...]).astype(_BF)
    proj = _dot(xn, win_ref[...])
    cq = proj[:, PW_Q:PW_KV]
    ckv = proj[:, PW_KV:PW_KRA]
    cos = cos_ref[...]
    sin = sin_ref[...]
    kr = proj[:, PW_KRA:PW_KRB] * cos + proj[:, PW_KRB:PW_END] * sin
    if slabs:
        ut = lax.dot_general(wu_ref[...], xn, (((1,), (1,)), ((), ())), preferred_element_type=_F32)
        for g in range(SSM_GROUPS):
            for c in range(ut.shape[1] // SLAB_T):
                u_ref[g, c] = ut[g * SSM_GROUP:(g + 1) * SSM_GROUP, c * SLAB_T:(c + 1) * SLAB_T]
    else:
        u_ref[...] = lax.dot_general(xn, wu_ref[...], (((1,), (1,)), ((), ())), preferred_element_type=_F32)
    qq = _dot(_rms(cq, gq_ref[...]).astype(_BF), wq_ref[...])
    kv = _dot(_rms(ckv, gkv_ref[...]).astype(_BF), wkv_ref[...])
    lane = lax.broadcasted_iota(jnp.int32, (1, HEAD_PAD), 1)
    ones_col = (lane == ONES_LANE).astype(_F32)
    hw = N_HEADS * HEAD_PAD
    for h in range(N_HEADS):
        lo = h * HEAD_PAD
        q_h = qq[:, lo:lo + HEAD_PAD] * cos + qq[:, hw + lo:hw + lo + HEAD_PAD] * sin
        k_h = kv[:, lo:lo + HEAD_PAD] + kr
        v_h = kv[:, hw + lo:hw + lo + HEAD_PAD] + ones_col
        k_ref[h] = k_h.astype(_BF)
        if slabs:
            qn_ref, kmx_ref = norm_refs
            qt_h = q_h.T.astype(_BF)
            q_ref[h] = qt_h
            v_ref[h] = v_h.T.astype(_BF)
            qt_f = qt_h.astype(_F32)
            qn_ref[h] = jnp.sqrt(jnp.sum(qt_f * qt_f, axis=0, keepdims=True))
            k_f = k_h.astype(_BF).astype(_F32)
            kmx_ref[h] = jnp.broadcast_to(jnp.max(jnp.sum(k_f * k_f, axis=1, keepdims=True), axis=0, keepdims=True),
                                          (1, LANES))
        else:
            q_ref[h] = q_h.astype(_BF)
            v_ref[h] = v_h.astype(_BF)


def _const_spec(shape):
    nd = len(shape)
    return pl.BlockSpec(shape, lambda *_: (0,) * nd)


def _proj_call(x, cos_t, sin_t, gpre, win, wu_t, gq, wq, gkv, wkv, *, tile, slabs):
    bsz, seq, _ = x.shape
    nt = seq // tile
    hw = N_HEADS * HEAD_PAD
    head_shape = (bsz, N_HEADS, seq, HEAD_PAD)
    head_spec = pl.BlockSpec((None, N_HEADS, tile, HEAD_PAD), lambda b, i: (b, 0, i, 0))
    row_spec = lambda w: pl.BlockSpec((None, tile, w), lambda b, i: (b, i, 0))
    if slabs:
        t_shape = (bsz, N_HEADS, nt, HEAD_PAD, tile)
        t_spec = pl.BlockSpec((None, N_HEADS, None, HEAD_PAD, tile), lambda b, i: (b, 0, i, 0, 0))
        u_shape = (SSM_GROUPS, seq // SLAB_T, bsz * SSM_GROUP, SLAB_T)
        u_spec = pl.BlockSpec((SSM_GROUPS, tile // SLAB_T, SSM_GROUP, SLAB_T), lambda b, i: (0, i, b, 0))
        norm_spec = lambda w: pl.BlockSpec((None, N_HEADS, None, 1, w), lambda b, i: (b, 0, i, 0, 0))
        norm_specs = [norm_spec(tile), norm_spec(LANES)]
        norm_shapes = [jax.ShapeDtypeStruct((bsz, N_HEADS, nt, 1, tile), _F32),
                       jax.ShapeDtypeStruct((bsz, N_HEADS, nt, 1, LANES), _F32)]
    else:
        t_shape, t_spec = head_shape, head_spec
        u_shape, u_spec = (bsz, seq, SSM_WIDTH), row_spec(SSM_WIDTH)
        norm_specs, norm_shapes = [], []
    return pl.pallas_call(
        functools.partial(_proj_kernel, slabs=slabs),
        grid=(bsz, nt),
        in_specs=[row_spec(D_MODEL), row_spec(HEAD_PAD), row_spec(HEAD_PAD),
                  _const_spec((1, D_MODEL)), _const_spec((D_MODEL, PW_END)), _const_spec((SSM_WIDTH, D_MODEL)),
                  _const_spec((1, Q_LORA_RANK)), _const_spec((Q_LORA_RANK, 2 * hw)),
                  _const_spec((1, KV_LORA_RANK)), _const_spec((KV_LORA_RANK, 2 * hw))],
        out_specs=[t_spec, head_spec, t_spec, u_spec] + norm_specs,
        out_shape=[jax.ShapeDtypeStruct(t_shape, _BF),
                   jax.ShapeDtypeStruct(head_shape, _BF),
                   jax.ShapeDtypeStruct(t_shape, _BF),
                   jax.ShapeDtypeStruct(u_shape, _F32)] + norm_shapes,
        compiler_params=pltpu.CompilerParams(
            dimension_semantics=("parallel", "parallel"), vmem_limit_bytes=VMEM_LIMIT),
        name="proj",
    )(x, cos_t, sin_t, gpre, win, wu_t, gq, wq, gkv, wkv)


def _attn_finish(accs, o_ref):
    halves = [(acc * (1.0 / acc[ONES_LANE:ONES_LANE + 1, :]))[:V_HEAD_DIM] for acc in accs]
    o_ref[...] = jnp.concatenate(halves, axis=0).T


def _attn_bounded_kernel(qt_ref, mrow_ref, k_ref, vt_ref, km_ref, vtm_ref, o_ref, *, nk, tk):
    tq = qt_ref.shape[2]
    key_row = lax.broadcasted_iota(jnp.int32, (LANES, tq), 0)
    accs = []
    for hh in range(2):
        qt = qt_ref[hh]
        mrow = mrow_ref[hh]
        s0 = jnp.where(key_row < N_META, _dot(km_ref[hh], qt), -jnp.inf)
        acc = _dot(vtm_ref[hh], jnp.exp2(s0 - mrow).astype(_BF))
        scores = lambda c: _dot(k_ref[hh, c * tk:(c + 1) * tk, :], qt)
        s_next = scores(0)
        for c in range(nk):
            s = s_next
            if c + 1 < nk:
                s_next = scores(c + 1)
            acc = acc + _dot(vt_ref[hh, c], jnp.exp2(s - mrow).astype(_BF))
        accs.append(acc)
    _attn_finish(accs, o_ref)


def _attn_online_kernel(qt_ref, k_ref, vt_ref, km_ref, vtm_ref, o_ref, s0_scr, s1_scr, m_scr, acc_scr, *, nk, tk):
    tq = qt_ref.shape[2]
    key_row = lax.broadcasted_iota(jnp.int32, (LANES, tq), 0)
    for hh in range(2):
        s0 = jnp.where(key_row < N_META, _dot(km_ref[hh], qt_ref[hh]), -jnp.inf)
        m0 = jnp.max(s0, axis=0, keepdims=True)
        m_scr[hh] = m0
        acc_scr[hh] = _dot(vtm_ref[hh], jnp.exp2(s0 - m0).astype(_BF))

    def scores(buf, c):
        off = pl.multiple_of(c * tk, tk)
        for hh in range(2):
            buf[hh] = _dot(k_ref[hh, pl.ds(off, tk), :], qt_ref[hh])

    def accumulate(buf, c):
        for hh in range(2):
            s = buf[hh]
            m = m_scr[hh]
            m_new = jnp.maximum(m, jnp.max(s, axis=0, keepdims=True))
            m_scr[hh] = m_new
            p = jnp.exp2(s - m_new).astype(_BF)
            acc_scr[hh] = jnp.exp2(m - m_new) * acc_scr[hh] + _dot(vt_ref[hh, c], p)

    scores(s0_scr, 0)

    def body(t, _):
        scores(s1_scr, 2 * t + 1)
        accumulate(s0_scr, 2 * t)
        scores(s0_scr, 2 * t + 2)
        accumulate(s1_scr, 2 * t + 1)
        return 0

    lax.fori_loop(0, nk // 2 - 1, body, 0)
    scores(s1_scr, nk - 1)
    accumulate(s0_scr, nk - 2)
    accumulate(s1_scr, nk - 1)
    _attn_finish([acc_scr[0], acc_scr[1]], o_ref)


def _attn_call(qt, mrow, k, vt, km, vtm, *, bounded):
    bsz, _, nq, _, tq = qt.shape
    nk, tk = vt.shape[2], vt.shape[4]
    seq = k.shape[2]
    assert nk % 2 == 0 and nk >= 4
    q_spec = pl.BlockSpec((None, 2, None, HEAD_PAD, tq), lambda b, hp, i: (b, hp, i, 0, 0))
    kv_specs = [pl.BlockSpec((None, 2, seq, HEAD_PAD), lambda b, hp, i: (b, hp, 0, 0)),
                pl.BlockSpec((None, 2, nk, HEAD_PAD, tk), lambda b, hp, i: (b, hp, 0, 0, 0)),
                pl.BlockSpec((2, LANES, HEAD_PAD), lambda b, hp, i: (hp, 0, 0)),
                pl.BlockSpec((2, HEAD_PAD, LANES), lambda b, hp, i: (hp, 0, 0))]
    if bounded:
        body = functools.partial(_attn_bounded_kernel, nk=nk, tk=tk)
        in_specs = [q_spec, pl.BlockSpec((None, 2, None, 1, tq), lambda b, hp, i: (b, hp, i, 0, 0))] + kv_specs
        args, scratch = (qt, mrow, k, vt, km, vtm), []
    else:
        body = functools.partial(_attn_online_kernel, nk=nk, tk=tk)
        in_specs = [q_spec] + kv_specs
        args = (qt, k, vt, km, vtm)
        scratch = [pltpu.VMEM((2, tk, tq), _F32), pltpu.VMEM((2, tk, tq), _F32),
                   pltpu.VMEM((2, 1, tq), _F32), pltpu.VMEM((2, HEAD_PAD, tq), _F32)]
    return pl.pallas_call(
        body,
        grid=(bsz, N_HEADS // 2, nq),
        in_specs=in_specs,
        out_specs=pl.BlockSpec((None, tq, LANES), lambda b, hp, i: (b, i, hp)),
        out_shape=jax.ShapeDtypeStruct((bsz, nq * tq, ATTN_WIDTH), _F32),
        scratch_shapes=scratch,
        compiler_params=pltpu.CompilerParams(
            dimension_semantics=("parallel", "parallel", "arbitrary"), vmem_limit_bytes=VMEM_LIMIT),
        name="attn_bounded" if bounded else "attn_online",
    )(*args)


def _cmul_add(ar, ai, xr, xi, sr, si):
    return ar * xr - ai * xi + sr, ar * xi + ai * xr + si


def _s5_kernel(a_ref, um_ref, pin_ref, pout_ref, m_ref, w_ref, v_ref, t_ref, y_ref, sup_scr, ent_scr,
               *, nslab, bsz):
    rows = nslab * bsz
    half = STATE_W // 2
    a = jnp.concatenate([a_ref[pl.ds(i, rows, stride=SSM_GROUP), :] for i in range(SSM_GROUP)], axis=1)
    ap = _dot(a.astype(_BF), pin_ref[...]).astype(_BF)
    uc = [ap[:, c * CHUNK_W:(c + 1) * CHUNK_W] for c in range(SLAB_CHUNKS)]
    w = w_ref[...]
    s = [_dot(u, w) for u in uc]
    sr = [x[:, :half] for x in s]
    si = [x[:, half:] for x in s]
    t = t_ref[...]
    trow = lambda r: t[r:r + 1, :]
    sup_r = sup_i = None
    for c in range(SLAB_CHUNKS):
        cr, ci = trow(c), trow(SLAB_CHUNKS + c)
        pr = cr * sr[c] - ci * si[c]
        pi = cr * si[c] + ci * sr[c]
        sup_r = pr if sup_r is None else sup_r + pr
        sup_i = pi if sup_i is None else sup_i + pi
    sup_scr[:, :half] = sup_r
    sup_scr[:, half:] = sup_i

    lane = lax.broadcasted_iota(jnp.int32, (bsz, half), 1)
    fwd = lane < SSM_STATE
    sm = _dot(um_ref[...], w)
    xr = jnp.where(fwd, sm[:bsz, :half], 0.0)
    xi = jnp.where(fwd, sm[:bsz, half:], 0.0)
    a_slab_r, a_slab_i = trow(2 * SLAB_CHUNKS), trow(2 * SLAB_CHUNKS + 1)
    for j in range(nslab):
        rf = j * bsz
        rb = (nslab - 1 - j) * bsz
        ent_scr[rf:rf + bsz, 0:SSM_STATE] = xr[:, 0:SSM_STATE]
        ent_scr[rb:rb + bsz, SSM_STATE:half] = xr[:, SSM_STATE:half]
        ent_scr[rf:rf + bsz, half:half + SSM_STATE] = xi[:, 0:SSM_STATE]
        ent_scr[rb:rb + bsz, half + SSM_STATE:STATE_W] = xi[:, SSM_STATE:half]
        s_r = jnp.where(fwd, sup_scr[rf:rf + bsz, :half], sup_scr[rb:rb + bsz, :half])
        s_i = jnp.where(fwd, sup_scr[rf:rf + bsz, half:], sup_scr[rb:rb + bsz, half:])
        xr, xi = _cmul_add(a_slab_r, a_slab_i, xr, xi, s_r, s_i)

    ent = ent_scr[...]
    a_r, a_i = trow(2 * SLAB_CHUNKS + 2), trow(2 * SLAB_CHUNKS + 3)
    xf = [(ent[:, :half], ent[:, half:])]
    for c in range(1, SLAB_CHUNKS):
        xf.append(_cmul_add(a_r, a_i, xf[-1][0], xf[-1][1], sr[c - 1], si[c - 1]))
    xb = [(ent[:, :half], ent[:, half:])]
    for c in range(SLAB_CHUNKS - 2, -1, -1):
        xb.insert(0, _cmul_add(a_r, a_i, xb[0][0], xb[0][1], sr[c + 1], si[c + 1]))
    fwd_rows = lax.broadcasted_iota(jnp.int32, (rows, half), 1) < SSM_STATE
    m = m_ref[...]
    v = v_ref[...]
    ys = []
    for c in range(SLAB_CHUNKS):
        xin = jnp.concatenate([jnp.where(fwd_rows, xf[c][0], xb[c][0]),
                               jnp.where(fwd_rows, xf[c][1], xb[c][1])], axis=1).astype(_BF)
        ys.append((_dot(uc[c], m) + _dot(xin, v)).astype(_BF))
    yp = _dot(jnp.concatenate(ys, axis=1), pout_ref[...])
    for o in range(SSM_GROUP):
        y_ref[pl.ds(o, rows, stride=SSM_GROUP), :] = yp[:, o * SLAB_T:(o + 1) * SLAB_T]


def _s5_call(a, um, pin, pout, m_mat, w_mat, v_mat, tab, *, nslab, bsz):
    n = nslab * bsz * SSM_GROUP
    rows = nslab * bsz
    g_spec = lambda *shape: pl.BlockSpec((None,) + shape, lambda g: (g,) + (0,) * len(shape))
    perm_spec = pl.BlockSpec((SLAB_W, SLAB_W), lambda g: (0, 0), pipeline_mode=pl.Buffered(1))
    return pl.pallas_call(
        functools.partial(_s5_kernel, nslab=nslab, bsz=bsz),
        grid=(SSM_GROUPS,),
        in_specs=[g_spec(n, SLAB_T), g_spec(SUBLANES, CHUNK_W), perm_spec, perm_spec,
                  g_spec(CHUNK_W, CHUNK_W), g_spec(CHUNK_W, STATE_W), g_spec(STATE_W, CHUNK_W),
                  g_spec(2 * SUBLANES, STATE_W // 2)],
        out_specs=g_spec(n, SLAB_T),
        out_shape=jax.ShapeDtypeStruct((SSM_GROUPS, n, SLAB_T), _F32),
        scratch_shapes=[pltpu.VMEM((rows, STATE_W), _F32), pltpu.VMEM((rows, STATE_W), _F32)],
        compiler_params=pltpu.CompilerParams(
            dimension_semantics=("parallel",), vmem_limit_bytes=VMEM_LIMIT),
        name="s5",
    )(a, um, pin, pout, m_mat, w_mat, v_mat, tab)


def _s5_matrices(a_re, a_im, log_dt, b_re, b_im, c_re, c_im, d_skip):
    tc = SSM_CHUNK
    lam = lax.complex(jnp.minimum(a_re.astype(_F32), -1e-4), a_im.astype(_F32))
    dt = jnp.exp(log_dt.astype(_F32))[..., None]
    lam_dt = lam * dt
    lam_bar = jnp.exp(lam_dt)
    b_bar = ((lam_bar - 1.0) / lam)[..., None] * lax.complex(b_re.astype(_F32), b_im.astype(_F32))
    c_c = lax.complex(c_re.astype(_F32), c_im.astype(_F32))
    k_idx = jnp.arange(tc + 1, dtype=_F32)
    pw = jnp.exp(lam_dt[:, :, None, :] * k_idx[None, None, :, None])
    kern = jnp.real(jnp.einsum('dgop,dgkp,dgpi->dgkoi', c_c, pw[:, :, :tc], b_bar))
    d_g = d_skip.astype(_F32).reshape(SSM_GROUPS, SSM_GROUP)
    center = kern[0][:, :1] + kern[1][:, :1] + (jnp.eye(SSM_GROUP, dtype=_F32)[None] * d_g[:, :, None])[:, None]
    lags = jnp.concatenate([kern[1][:, :0:-1], center, kern[0][:, 1:]], axis=1)
    lags = lags.transpose(0, 3, 1, 2).reshape(SSM_GROUPS, SSM_GROUP, (2 * tc - 1) * SSM_GROUP)
    m_mat = jnp.stack([lags[:, :, (tc - 1 - s) * SSM_GROUP:(2 * tc - 1 - s) * SSM_GROUP] for s in range(tc)],
                      axis=1).reshape(SSM_GROUPS, CHUNK_W, CHUNK_W)
    pw_f = pw[0][:, tc - 1 - jnp.arange(tc)]
    pw_b = pw[1][:, jnp.arange(tc)]
    wf = pw_f[:, :, None, :] * b_bar[0].transpose(0, 2, 1)[:, None, :, :]
    wb = pw_b[:, :, None, :] * b_bar[1].transpose(0, 2, 1)[:, None, :, :]
    w_mat = jnp.concatenate([jnp.real(wf), jnp.real(wb), jnp.imag(wf), jnp.imag(wb)], axis=-1)
    w_mat = w_mat.reshape(SSM_GROUPS, CHUNK_W, STATE_W)
    gf = pw[0][:, 1 + jnp.arange(tc)][:, :, None, :] * c_c[0][:, None, :, :]
    gb = pw[1][:, tc - jnp.arange(tc)][:, :, None, :] * c_c[1][:, None, :, :]
    v_mat = jnp.concatenate([jnp.real(gf), jnp.real(gb), -jnp.imag(gf), -jnp.imag(gb)], axis=-1)
    v_mat = v_mat.reshape(SSM_GROUPS, CHUNK_W, STATE_W).transpose(0, 2, 1)
    n_idx = jnp.arange(SLAB_CHUNKS + 1, dtype=_F32) * tc
    pc = jnp.exp(lam_dt[:, :, None, :] * n_idx[None, None, :, None])
    coef = jnp.concatenate([pc[0][:, SLAB_CHUNKS - 1 - jnp.arange(SLAB_CHUNKS)],
                            pc[1][:, jnp.arange(SLAB_CHUNKS)]], axis=-1)
    both = lambda n: jnp.concatenate([pc[0][:, n], pc[1][:, n]], axis=-1)[:, None, :]
    a_slab, a_chunk = both(SLAB_CHUNKS), both(1)
    tab = jnp.concatenate([jnp.real(coef), jnp.imag(coef), jnp.real(a_slab), jnp.imag(a_slab),
                           jnp.real(a_chunk), jnp.imag(a_chunk)], axis=1)
    tab = jnp.pad(tab, ((0, 0), (0, 2 * SUBLANES - tab.shape[1]), (0, 0)))
    return m_mat, w_mat, v_mat, tab


def _slab_permutation():
    r = lax.broadcasted_iota(jnp.int32, (SLAB_W, SLAB_W), 0)
    c = lax.broadcasted_iota(jnp.int32, (SLAB_W, SLAB_W), 1)
    return ((r // SLAB_T == c % SSM_GROUP) & (r % SLAB_T == c // SSM_GROUP)).astype(_BF)


def _post_kernel(x_ref, attn_ref, y_ref, wglu_ref, gmix_ref, wout_ref, gpm_ref, gpre_ref,
                 wup_ref, wdn_ref, gpost_ref, o_ref):
    nslab = y_ref.shape[1]
    yt = jnp.concatenate([jnp.concatenate([y_ref[g, c] for c in range(nslab)], axis=1)
                          for g in range(SSM_GROUPS)], axis=0)
    y = yt.T
    gy = 0.5 * y * (1.0 + jnp.tanh(math.sqrt(2.0 / math.pi) * (y + 0.044715 * (y * y * y))))
    z = _dot(gy.astype(_BF), wglu_ref[...])
    ssm = z[:, :SSM_WIDTH] * (1.0 / (1.0 + jnp.exp(-z[:, SSM_WIDTH:])))
    gmix = gmix_ref[...]
    mix = jnp.concatenate([_rms(attn_ref[...], gmix[:, :ATTN_WIDTH]),
                           _rms(ssm, gmix[:, ATTN_WIDTH:])], axis=-1).astype(_BF)
    h1 = x_ref[...] + _rms(_dot(mix, wout_ref[...]), gpm_ref[...])
    hn = _rms(h1, gpre_ref[...]).astype(_BF)
    acc = None
    for c in range(D_FF // FF_TILE):
        up = _dot(hn, wup_ref[:, c * FF_TILE:(c + 1) * FF_TILE])
        up = jnp.maximum(up, 0.0)
        part = _dot((up * up).astype(_BF), wdn_ref[c * FF_TILE:(c + 1) * FF_TILE, :])
        acc = part if acc is None else acc + part
    o_ref[...] = h1 + _rms(acc, gpost_ref[...])


def _post_call(x, attn, y, wglu, gmix, wout, gpm, gpre, wup, wdn, gpost, *, tile):
    bsz, seq, _ = x.shape
    row_spec = lambda w: pl.BlockSpec((None, tile, w), lambda b, i: (b, i, 0))
    wspec = lambda shape: pl.BlockSpec(shape, lambda b, i: (0, 0), pipeline_mode=pl.Buffered(1))
    y_spec = pl.BlockSpec((SSM_GROUPS, tile // SLAB_T, SSM_GROUP, SLAB_T), lambda b, i: (0, i, b, 0))
    return pl.pallas_call(
        _post_kernel,
        grid=(bsz, seq // tile),
        in_specs=[row_spec(D_MODEL), row_spec(ATTN_WIDTH), y_spec,
                  wspec((SSM_WIDTH, 2 * SSM_WIDTH)), wspec((1, D_MODEL)), wspec((D_MODEL, D_MODEL)),
                  wspec((1, D_MODEL)), wspec((1, D_MODEL)), wspec((D_MODEL, D_FF)),
                  wspec((D_FF, D_MODEL)), wspec((1, D_MODEL))],
        out_specs=row_spec(D_MODEL),
        out_shape=jax.ShapeDtypeStruct((bsz, seq, D_MODEL), _F32),
        compiler_params=pltpu.CompilerParams(
            dimension_semantics=("parallel", "parallel"), vmem_limit_bytes=VMEM_LIMIT),
        name="post",
    )(x, attn, y, wglu, gmix, wout, gpm, gpre, wup, wdn, gpost)


def _rope_partner(w):
    half = QK_ROPE_DIM // 2
    return jnp.concatenate([-w[..., half:], w[..., :half]], axis=-1)


def _rope_tables(pos):
    inv = 1.0 / (ROPE_BASE ** (jnp.arange(0, QK_ROPE_DIM, 2, dtype=_F32) / QK_ROPE_DIM))
    ang = pos.astype(_F32)[..., None] * inv
    cos, sin = jnp.cos(ang), jnp.sin(ang)
    lead = cos.shape[:-1]
    ones = jnp.ones(lead + (QK_NOPE_DIM,), _F32)
    zeros_n = jnp.zeros(lead + (QK_NOPE_DIM,), _F32)
    zeros_p = jnp.zeros(lead + (HEAD_PAD - QK_HEAD_DIM,), _F32)
    return (jnp.concatenate([ones, cos, cos, zeros_p], axis=-1),
            jnp.concatenate([zeros_n, sin, sin, zeros_p], axis=-1))


def _prep_weights(w_in, w_uq, w_ukv):
    scale = QK_HEAD_DIM ** -0.5 * math.log2(math.e)
    pad_lo = jnp.zeros((D_MODEL, QK_NOPE_DIM), _F32)
    pad_hi = jnp.zeros((D_MODEL, HEAD_PAD - QK_HEAD_DIM), _F32)
    w_kr = w_in[:, OFF_KR:OFF_U]
    win = jnp.concatenate([w_in[:, :OFF_KR],
                           pad_lo, w_kr, pad_hi,
                           pad_lo, _rope_partner(w_kr), pad_hi], axis=1).astype(_BF)
    wu_t = w_in[:, OFF_U:].T.astype(_BF)
    wq3 = w_uq.reshape(Q_LORA_RANK, N_HEADS, QK_HEAD_DIM) * scale
    zq = lambda n: jnp.zeros((Q_LORA_RANK, N_HEADS, n), _F32)
    q_plain = jnp.concatenate([wq3, zq(HEAD_PAD - QK_HEAD_DIM)], axis=-1)
    q_part = jnp.concatenate([zq(QK_NOPE_DIM), _rope_partner(wq3[..., QK_NOPE_DIM:]),
                              zq(HEAD_PAD - QK_HEAD_DIM)], axis=-1)
    wq = jnp.concatenate([q_plain.reshape(Q_LORA_RANK, -1), q_part.reshape(Q_LORA_RANK, -1)], axis=1).astype(_BF)
    wkv3 = w_ukv.reshape(KV_LORA_RANK, N_HEADS, QK_NOPE_DIM + V_HEAD_DIM)
    zk = lambda n: jnp.zeros((KV_LORA_RANK, N_HEADS, n), _F32)
    k_slab = jnp.concatenate([wkv3[..., :QK_NOPE_DIM], zk(HEAD_PAD - QK_NOPE_DIM)], axis=-1)
    v_slab = jnp.concatenate([wkv3[..., QK_NOPE_DIM:], zk(HEAD_PAD - V_HEAD_DIM)], axis=-1)
    wkv = jnp.concatenate([k_slab.reshape(KV_LORA_RANK, -1), v_slab.reshape(KV_LORA_RANK, -1)], axis=1).astype(_BF)
    return win, wu_t, wq, wkv


def kernel(x, positions, meta_tokens, g_pre_mix, w_in, g_q_lat, w_uq, g_kv_lat, w_ukv,
           ssm_A_re, ssm_A_im, ssm_log_dt, ssm_B_re, ssm_B_im, ssm_C_re, ssm_C_im, ssm_D,
           w_glu, g_mix_out, w_out, g_post_mix, g_pre_mlp, w_mlp_up, w_mlp_down, g_post_mlp):
    bsz, seq, _ = x.shape
    assert seq % ROW_TILE == 0 and ROW_TILE % SLAB_T == 0 and bsz <= SUBLANES
    assert N_META <= SSM_CHUNK
    row = lambda g: g.reshape(1, -1).astype(_F32)

    win, wu_t, wq, wkv = _prep_weights(w_in[0], w_uq[0], w_ukv[0])
    cos_x, sin_x = _rope_tables(positions.astype(jnp.int32) + N_META)
    cos_m, sin_m = _rope_tables(jnp.arange(N_META, dtype=jnp.int32)[None])
    norm_args = (row(g_pre_mix[0]), win, wu_t, row(g_q_lat[0]), wq, row(g_kv_lat[0]), wkv)

    qt, k, vt, u, q_norm, k_sq = _proj_call(x, cos_x, sin_x, *norm_args, tile=ROW_TILE, slabs=True)
    _, k_m, v_m, u_m = _proj_call(meta_tokens.astype(x.dtype)[None], cos_m, sin_m, *norm_args,
                                  tile=N_META, slabs=False)
    km = jnp.pad(k_m[0], ((0, 0), (0, LANES - N_META), (0, 0)))
    vtm = jnp.pad(v_m[0].transpose(0, 2, 1), ((0, 0), (0, 0), (0, LANES - N_META)))
    k_m_sq = jnp.max(jnp.sum(jnp.square(k_m[0].astype(_F32)), axis=-1), axis=-1)
    k_max = jnp.sqrt(jnp.maximum(jnp.max(k_sq, axis=(2, 3, 4)), k_m_sq[None]))
    mrow = q_norm * (k_max * BOUND_SLACK)[:, :, None, None, None]
    attn = lax.cond(jnp.max(mrow) <= BOUND_LIMIT,
                    lambda: _attn_call(qt, mrow, k, vt, km, vtm, bounded=True),
                    lambda: _attn_call(qt, mrow, k, vt, km, vtm, bounded=False))

    um = u_m[0].astype(_BF).reshape(N_META, SSM_GROUPS, SSM_GROUP).transpose(1, 0, 2)
    um = jnp.pad(um, ((0, 0), (SSM_CHUNK - N_META, 0), (0, 0))).reshape(SSM_GROUPS, 1, CHUNK_W)
    um = jnp.broadcast_to(um, (SSM_GROUPS, SUBLANES, CHUNK_W))
    m_mat, w_mat, v_mat, tab = _s5_matrices(ssm_A_re[0], ssm_A_im[0], ssm_log_dt[0], ssm_B_re[0],
                                            ssm_B_im[0], ssm_C_re[0], ssm_C_im[0], ssm_D[0])
    pin = _slab_permutation()
    nslab = seq // SLAB_T
    yg = _s5_call(u.reshape(SSM_GROUPS, nslab * bsz * SSM_GROUP, SLAB_T), um, pin, pin.T,
                  m_mat.astype(_BF), w_mat.astype(_BF), v_mat.astype(_BF), tab, nslab=nslab, bsz=bsz)
    y = yg.reshape(SSM_GROUPS, nslab, bsz * SSM_GROUP, SLAB_T)

    return _post_call(x, attn, y, w_glu[0].astype(_BF), row(g_mix_out[0]), w_out[0].astype(_BF),
                      row(g_post_mix[0]), row(g_pre_mlp[0]), w_mlp_up[0].astype(_BF),
                      w_mlp_down[0].astype(_BF), row(g_post_mlp[0]), tile=ROW_TILE)
```

```python
import functools
import math

import jax
import jax.numpy as jnp
from jax import lax
from jax.experimental import pallas as pl
from jax.experimental.pallas import tpu as pltpu

D_MODEL = 1024
N_META = 16
ATTN_WIDTH = 512
SSM_WIDTH = 512
N_HEADS = 8
V_HEAD_DIM = 64
QK_NOPE_DIM = 64
QK_ROPE_DIM = 32
QK_HEAD_DIM = QK_NOPE_DIM + QK_ROPE_DIM
Q_LORA_RANK = 384
KV_LORA_RANK = 256
ROPE_BASE = 10000.0
SSM_GROUP = 16
SSM_GROUPS = 32
SSM_STATE = 64
D_FF = 4 * D_MODEL
EPS = 1e-6
OFF_KV = Q_LORA_RANK
OFF_KR = OFF_KV + KV_LORA_RANK
OFF_U = OFF_KR + QK_ROPE_DIM

LANES = 128
SUBLANES = 8
HEAD_PAD = LANES
ONES_LANE = V_HEAD_DIM
PW_Q = 0
PW_KV = PW_Q + Q_LORA_RANK
PW_KR = PW_KV + KV_LORA_RANK
PW_END = PW_KR + HEAD_PAD

SSM_CHUNK = 32
CHUNK_W = SSM_CHUNK * SSM_GROUP
STATE_W = 4 * SSM_STATE
SLAB_T = LANES
SLAB_CHUNKS = SLAB_T // SSM_CHUNK
SLAB_W = SSM_GROUP * SLAB_T

ROW_TILE = 512
FF_TILE = 1024
BOUND_SLACK = 1.0 + 2.0 ** -6
BOUND_LIMIT = 60.0
VMEM_LIMIT = 56 * 1024 * 1024

_BF = jnp.bfloat16
_F32 = jnp.float32


def _dot(a, b):
    return jnp.dot(a, b, preferred_element_type=_F32)


def _rms(x, g):
    return x * lax.rsqrt(jnp.mean(x * x, axis=-1, keepdims=True) + EPS) * g


_NT = (((1,), (1,)), ((), ()))


def _proj_kernel(x_ref, rope_k_ref, *refs, meta):
    if meta:
        gpre_ref, win_ref, wu_ref, gkv_ref, wk_ref, wv_ref, k_ref, v_ref, u_ref = refs
    else:
        (rope_q_ref, gpre_ref, win_ref, wu_ref, gq_ref, wq_ref, gkv_ref, wk_ref, wv_ref,
         q_ref, k_ref, v_ref, u_ref, qn_ref, kmx_ref) = refs
    tile = x_ref.shape[0]
    xn = _rms(x_ref[...], gpre_ref[...]).astype(_BF)
    proj = _dot(xn, win_ref[...])
    kvn = _rms(proj[:, PW_KV:PW_KR], gkv_ref[...]).astype(_BF)
    kr_terms = proj[:, PW_KR:PW_END] * rope_k_ref[...]
    lane = lax.broadcasted_iota(jnp.int32, (1, HEAD_PAD), 1)
    rope_lanes = (lane >= QK_NOPE_DIM) & (lane < QK_HEAD_DIM)
    kr = jnp.where(rope_lanes, kr_terms + pltpu.roll(kr_terms, HEAD_PAD - QK_ROPE_DIM, 1), 0.0)
    kk = _dot(kvn, wk_ref[...])
    vt = lax.dot_general(wv_ref[...], kvn, _NT, preferred_element_type=_F32)
    ones_tail = (lax.broadcasted_iota(jnp.int32, (HEAD_PAD - V_HEAD_DIM, tile), 0) == 0).astype(_F32)
    if meta:
        u_ref[...] = lax.dot_general(xn, wu_ref[...], _NT, preferred_element_type=_F32)
    else:
        ut = lax.dot_general(wu_ref[...], xn, _NT, preferred_element_type=_F32)
        for g in range(SSM_GROUPS):
            for c in range(tile // SLAB_T):
                u_ref[g, c] = ut[g * SSM_GROUP:(g + 1) * SSM_GROUP, c * SLAB_T:(c + 1) * SLAB_T]
        qn = _rms(proj[:, PW_Q:PW_KV], gq_ref[...]).astype(_BF)
        qt = lax.dot_general(wq_ref[...], qn, _NT, preferred_element_type=_F32)
        cos_t, sin_t = rope_q_ref[0], rope_q_ref[1]
        zero_rows = jnp.zeros((HEAD_PAD - QK_HEAD_DIM, tile), _F32)
    half = QK_ROPE_DIM // 2
    for h in range(N_HEADS):
        k_h = (kk[:, h * HEAD_PAD:(h + 1) * HEAD_PAD] + kr).astype(_BF)
        k_ref[h] = k_h
        v_ref[h] = jnp.concatenate([vt[h * V_HEAD_DIM:(h + 1) * V_HEAD_DIM], ones_tail], axis=0).astype(_BF)
        if not meta:
            blk = qt[h * QK_HEAD_DIM:(h + 1) * QK_HEAD_DIM]
            x1 = blk[QK_NOPE_DIM:QK_NOPE_DIM + half]
            x2 = blk[QK_NOPE_DIM + half:]
            qt_h = jnp.concatenate([blk[:QK_NOPE_DIM], x1 * cos_t - x2 * sin_t, x1 * sin_t + x2 * cos_t,
                                    zero_rows], axis=0).astype(_BF)
            q_ref[h] = qt_h
            qt_f = qt_h.astype(_F32)
            qn_ref[h] = jnp.sqrt(jnp.sum(qt_f * qt_f, axis=0, keepdims=True))
            k_f = k_h.astype(_F32)
            kmx_ref[h] = jnp.broadcast_to(jnp.max(jnp.sum(k_f * k_f, axis=1, keepdims=True), axis=0, keepdims=True),
                                          (1, LANES))


def _const_spec(shape):
    nd = len(shape)
    return pl.BlockSpec(shape, lambda *_: (0,) * nd)


def _proj_call(x, rope_k, rope_q, gpre, win, wu_t, gq, wq_t, gkv, wk, wv_t, *, tile, meta):
    bsz, seq, _ = x.shape
    nt = seq // tile
    row_spec = lambda w: pl.BlockSpec((None, tile, w), lambda b, i: (b, i, 0))
    k_spec = pl.BlockSpec((None, N_HEADS, tile, HEAD_PAD), lambda b, i: (b, 0, i, 0))
    t_spec = pl.BlockSpec((None, N_HEADS, None, HEAD_PAD, tile), lambda b, i: (b, 0, i, 0, 0))
    k_shape = jax.ShapeDtypeStruct((bsz, N_HEADS, seq, HEAD_PAD), _BF)
    t_shape = jax.ShapeDtypeStruct((bsz, N_HEADS, nt, HEAD_PAD, tile), _BF)
    w_specs = lambda *ws: [_const_spec(w.shape) for w in ws]
    if meta:
        args = (x, rope_k, gpre, win, wu_t, gkv, wk, wv_t)
        in_specs = [row_spec(D_MODEL), row_spec(HEAD_PAD)] + w_specs(*args[2:])
        out_specs = [k_spec, t_spec, row_spec(SSM_WIDTH)]
        out_shape = [k_shape, t_shape, jax.ShapeDtypeStruct((bsz, seq, SSM_WIDTH), _F32)]
    else:
        args = (x, rope_k, rope_q, gpre, win, wu_t, gq, wq_t, gkv, wk, wv_t)
        rope_q_spec = pl.BlockSpec((None, None, 2, QK_ROPE_DIM // 2, tile), lambda b, i: (b, i, 0, 0, 0))
        in_specs = [row_spec(D_MODEL), row_spec(HEAD_PAD), rope_q_spec] + w_specs(*args[3:])
        norm_spec = lambda w: pl.BlockSpec((None, N_HEADS, None, 1, w), lambda b, i: (b, 0, i, 0, 0))
        u_spec = pl.BlockSpec((SSM_GROUPS, tile // SLAB_T, SSM_GROUP, SLAB_T), lambda b, i: (0, i, b, 0))
        out_specs = [t_spec, k_spec, t_spec, u_spec, norm_spec(tile), norm_spec(LANES)]
        out_shape = [t_shape, k_shape, t_shape,
                     jax.ShapeDtypeStruct((SSM_GROUPS, seq // SLAB_T, bsz * SSM_GROUP, SLAB_T), _F32),
                     jax.ShapeDtypeStruct((bsz, N_HEADS, nt, 1, tile), _F32),
                     jax.ShapeDtypeStruct((bsz, N_HEADS, nt, 1, LANES), _F32)]
    return pl.pallas_call(
        functools.partial(_proj_kernel, meta=meta),
        grid=(bsz, nt),
        in_specs=in_specs,
        out_specs=out_specs,
        out_shape=out_shape,
        compiler_params=pltpu.CompilerParams(
            dimension_semantics=("parallel", "parallel"), vmem_limit_bytes=VMEM_LIMIT),
        name="proj_meta" if meta else "proj",
    )(*args)


def _attn_finish(accs, o_ref):
    halves = [(acc * (1.0 / acc[ONES_LANE:ONES_LANE + 1, :]))[:V_HEAD_DIM] for acc in accs]
    o_ref[...] = jnp.concatenate(halves, axis=0).T


def _attn_bounded_kernel(qt_ref, mrow_ref, k_ref, vt_ref, km_ref, vtm_ref, o_ref, *, nk, tk):
    tq = qt_ref.shape[2]
    key_row = lax.broadcasted_iota(jnp.int32, (LANES, tq), 0)
    accs = []
    for hh in range(2):
        qt = qt_ref[hh]
        mrow = mrow_ref[hh]
        s0 = jnp.where(key_row < N_META, _dot(km_ref[hh], qt), -jnp.inf)
        acc = _dot(vtm_ref[hh], jnp.exp2(s0 - mrow).astype(_BF))
        scores = lambda c: _dot(k_ref[hh, c * tk:(c + 1) * tk, :], qt)
        s_next = scores(0)
        for c in range(nk):
            s = s_next
            if c + 1 < nk:
                s_next = scores(c + 1)
            acc = acc + _dot(vt_ref[hh, c], jnp.exp2(s - mrow).astype(_BF))
        accs.append(acc)
    _attn_finish(accs, o_ref)


def _attn_online_kernel(qt_ref, k_ref, vt_ref, km_ref, vtm_ref, o_ref, s0_scr, s1_scr, m_scr, acc_scr, *, nk, tk):
    tq = qt_ref.shape[2]
    key_row = lax.broadcasted_iota(jnp.int32, (LANES, tq), 0)
    for hh in range(2):
        s0 = jnp.where(key_row < N_META, _dot(km_ref[hh], qt_ref[hh]), -jnp.inf)
        m0 = jnp.max(s0, axis=0, keepdims=True)
        m_scr[hh] = m0
        acc_scr[hh] = _dot(vtm_ref[hh], jnp.exp2(s0 - m0).astype(_BF))

    def scores(buf, c):
        off = pl.multiple_of(c * tk, tk)
        for hh in range(2):
            buf[hh] = _dot(k_ref[hh, pl.ds(off, tk), :], qt_ref[hh])

    def accumulate(buf, c):
        for hh in range(2):
            s = buf[hh]
            m = m_scr[hh]
            m_new = jnp.maximum(m, jnp.max(s, axis=0, keepdims=True))
            m_scr[hh] = m_new
            p = jnp.exp2(s - m_new).astype(_BF)
            acc_scr[hh] = jnp.exp2(m - m_new) * acc_scr[hh] + _dot(vt_ref[hh, c], p)

    scores(s0_scr, 0)

    def body(t, _):
        scores(s1_scr, 2 * t + 1)
        accumulate(s0_scr, 2 * t)
        scores(s0_scr, 2 * t + 2)
        accumulate(s1_scr, 2 * t + 1)
        return 0

    lax.fori_loop(0, nk // 2 - 1, body, 0)
    scores(s1_scr, nk - 1)
    accumulate(s0_scr, nk - 2)
    accumulate(s1_scr, nk - 1)
    _attn_finish([acc_scr[0], acc_scr[1]], o_ref)


def _attn_call(qt, mrow, k, vt, km, vtm, *, bounded):
    bsz, _, nq, _, tq = qt.shape
    nk, tk = vt.shape[2], vt.shape[4]
    seq = k.shape[2]
    assert nk % 2 == 0 and nk >= 4
    q_spec = pl.BlockSpec((None, 2, None, HEAD_PAD, tq), lambda b, hp, i: (b, hp, i, 0, 0))
    kv_specs = [pl.BlockSpec((None, 2, seq, HEAD_PAD), lambda b, hp, i: (b, hp, 0, 0)),
                pl.BlockSpec((None, 2, nk, HEAD_PAD, tk), lambda b, hp, i: (b, hp, 0, 0, 0)),
                pl.BlockSpec((2, LANES, HEAD_PAD), lambda b, hp, i: (hp, 0, 0)),
                pl.BlockSpec((2, HEAD_PAD, LANES), lambda b, hp, i: (hp, 0, 0))]
    if bounded:
        body = functools.partial(_attn_bounded_kernel, nk=nk, tk=tk)
        in_specs = [q_spec, pl.BlockSpec((None, 2, None, 1, tq), lambda b, hp, i: (b, hp, i, 0, 0))] + kv_specs
        args, scratch = (qt, mrow, k, vt, km, vtm), []
    else:
        body = functools.partial(_attn_online_kernel, nk=nk, tk=tk)
        in_specs = [q_spec] + kv_specs
        args = (qt, k, vt, km, vtm)
        scratch = [pltpu.VMEM((2, tk, tq), _F32), pltpu.VMEM((2, tk, tq), _F32),
                   pltpu.VMEM((2, 1, tq), _F32), pltpu.VMEM((2, HEAD_PAD, tq), _F32)]
    return pl.pallas_call(
        body,
        grid=(bsz, N_HEADS // 2, nq),
        in_specs=in_specs,
        out_specs=pl.BlockSpec((None, tq, LANES), lambda b, hp, i: (b, i, hp)),
        out_shape=jax.ShapeDtypeStruct((bsz, nq * tq, ATTN_WIDTH), _F32),
        scratch_shapes=scratch,
        compiler_params=pltpu.CompilerParams(
            dimension_semantics=("parallel", "parallel", "arbitrary"), vmem_limit_bytes=VMEM_LIMIT),
        name="attn_bounded" if bounded else "attn_online",
    )(*args)


def _cmul_add(ar, ai, xr, xi, sr, si):
    return ar * xr - ai * xi + sr, ar * xi + ai * xr + si


def _s5_kernel(a_ref, um_ref, pin_ref, pout_ref, m_ref, w_ref, v_ref, t_ref, y_ref, sup_scr, ent_scr,
               *, nslab, bsz):
    rows = nslab * bsz
    half = STATE_W // 2
    a = jnp.concatenate([a_ref[pl.ds(i, rows, stride=SSM_GROUP), :] for i in range(SSM_GROUP)], axis=1)
    ap = _dot(a.astype(_BF), pin_ref[...]).astype(_BF)
    uc = [ap[:, c * CHUNK_W:(c + 1) * CHUNK_W] for c in range(SLAB_CHUNKS)]
    w = w_ref[...]
    s = [_dot(u, w) for u in uc]
    sr = [x[:, :half] for x in s]
    si = [x[:, half:] for x in s]
    t = t_ref[...]
    trow = lambda r: t[r:r + 1, :]
    sup_r = sup_i = None
    for c in range(SLAB_CHUNKS):
        cr, ci = trow(c), trow(SLAB_CHUNKS + c)
        pr = cr * sr[c] - ci * si[c]
        pi = cr * si[c] + ci * sr[c]
        sup_r = pr if sup_r is None else sup_r + pr
        sup_i = pi if sup_i is None else sup_i + pi
    sup_scr[:, :half] = sup_r
    sup_scr[:, half:] = sup_i

    lane = lax.broadcasted_iota(jnp.int32, (bsz, half), 1)
    fwd = lane < SSM_STATE
    sm = _dot(um_ref[...], w)
    xr = jnp.where(fwd, sm[:bsz, :half], 0.0)
    xi = jnp.where(fwd, sm[:bsz, half:], 0.0)
    a_slab_r, a_slab_i = trow(2 * SLAB_CHUNKS), trow(2 * SLAB_CHUNKS + 1)
    for j in range(nslab):
        rf = j * bsz
        rb = (nslab - 1 - j) * bsz
        ent_scr[rf:rf + bsz, 0:SSM_STATE] = xr[:, 0:SSM_STATE]
        ent_scr[rb:rb + bsz, SSM_STATE:half] = xr[:, SSM_STATE:half]
        ent_scr[rf:rf + bsz, half:half + SSM_STATE] = xi[:, 0:SSM_STATE]
        ent_scr[rb:rb + bsz, half + SSM_STATE:STATE_W] = xi[:, SSM_STATE:half]
        s_r = jnp.where(fwd, sup_scr[rf:rf + bsz, :half], sup_scr[rb:rb + bsz, :half])
        s_i = jnp.where(fwd, sup_scr[rf:rf + bsz, half:], sup_scr[rb:rb + bsz, half:])
        xr, xi = _cmul_add(a_slab_r, a_slab_i, xr, xi, s_r, s_i)

    ent = ent_scr[...]
    a_r, a_i = trow(2 * SLAB_CHUNKS + 2), trow(2 * SLAB_CHUNKS + 3)
    xf = [(ent[:, :half], ent[:, half:])]
    for c in range(1, SLAB_CHUNKS):
        xf.append(_cmul_add(a_r, a_i, xf[-1][0], xf[-1][1], sr[c - 1], si[c - 1]))
    xb = [(ent[:, :half], ent[:, half:])]
    for c in range(SLAB_CHUNKS - 2, -1, -1):
        xb.insert(0, _cmul_add(a_r, a_i, xb[0][0], xb[0][1], sr[c + 1], si[c + 1]))
    fwd_rows = lax.broadcasted_iota(jnp.int32, (rows, half), 1) < SSM_STATE
    m = m_ref[...]
    v = v_ref[...]
    ys = []
    for c in range(SLAB_CHUNKS):
        xin = jnp.concatenate([jnp.where(fwd_rows, xf[c][0], xb[c][0]),
                               jnp.where(fwd_rows, xf[c][1], xb[c][1])], axis=1).astype(_BF)
        ys.append((_dot(uc[c], m) + _dot(xin, v)).astype(_BF))
    yp = _dot(jnp.concatenate(ys, axis=1), pout_ref[...])
    for o in range(SSM_GROUP):
        y_ref[pl.ds(o, rows, stride=SSM_GROUP), :] = yp[:, o * SLAB_T:(o + 1) * SLAB_T]


def _s5_call(a, um, pin, pout, m_mat, w_mat, v_mat, tab, *, nslab, bsz):
    n = nslab * bsz * SSM_GROUP
    rows = nslab * bsz
    g_spec = lambda *shape: pl.BlockSpec((None,) + shape, lambda g: (g,) + (0,) * len(shape))
    perm_spec = pl.BlockSpec((SLAB_W, SLAB_W), lambda g: (0, 0), pipeline_mode=pl.Buffered(1))
    return pl.pallas_call(
        functools.partial(_s5_kernel, nslab=nslab, bsz=bsz),
        grid=(SSM_GROUPS,),
        in_specs=[g_spec(n, SLAB_T), g_spec(SUBLANES, CHUNK_W), perm_spec, perm_spec,
                  g_spec(CHUNK_W, CHUNK_W), g_spec(CHUNK_W, STATE_W), g_spec(STATE_W, CHUNK_W),
                  g_spec(2 * SUBLANES, STATE_W // 2)],
        out_specs=g_spec(n, SLAB_T),
        out_shape=jax.ShapeDtypeStruct((SSM_GROUPS, n, SLAB_T), _F32),
        scratch_shapes=[pltpu.VMEM((rows, STATE_W), _F32), pltpu.VMEM((rows, STATE_W), _F32)],
        compiler_params=pltpu.CompilerParams(
            dimension_semantics=("parallel",), vmem_limit_bytes=VMEM_LIMIT),
        name="s5",
    )(a, um, pin, pout, m_mat, w_mat, v_mat, tab)


def _s5_matrices(a_re, a_im, log_dt, b_re, b_im, c_re, c_im, d_skip):
    tc = SSM_CHUNK
    lam = lax.complex(jnp.minimum(a_re.astype(_F32), -1e-4), a_im.astype(_F32))
    dt = jnp.exp(log_dt.astype(_F32))[..., None]
    lam_dt = lam * dt
    lam_bar = jnp.exp(lam_dt)
    b_bar = ((lam_bar - 1.0) / lam)[..., None] * lax.complex(b_re.astype(_F32), b_im.astype(_F32))
    c_c = lax.complex(c_re.astype(_F32), c_im.astype(_F32))
    k_idx = jnp.arange(tc + 1, dtype=_F32)
    pw = jnp.exp(lam_dt[:, :, None, :] * k_idx[None, None, :, None])
    kern = jnp.real(jnp.einsum('dgop,dgkp,dgpi->dgkoi', c_c, pw[:, :, :tc], b_bar))
    d_g = d_skip.astype(_F32).reshape(SSM_GROUPS, SSM_GROUP)
    center = kern[0][:, :1] + kern[1][:, :1] + (jnp.eye(SSM_GROUP, dtype=_F32)[None] * d_g[:, :, None])[:, None]
    lags = jnp.concatenate([center, kern[0][:, 1:], jnp.zeros_like(center), kern[1][:, :0:-1]], axis=1)
    lags = lags.transpose(0, 3, 1, 2)
    m_mat = jnp.tile(lags, (1, 1, tc, 1))[:, :, :tc * (2 * tc - 1)]
    m_mat = m_mat.reshape(SSM_GROUPS, SSM_GROUP, tc, 2 * tc - 1, SSM_GROUP)[:, :, :, :tc]
    m_mat = m_mat.reshape(SSM_GROUPS, CHUNK_W, CHUNK_W)
    wf = b_bar[0].transpose(0, 2, 1)[:, :, None, :] * pw[0][:, tc - 1::-1][:, None, :, :]
    wb = b_bar[1].transpose(0, 2, 1)[:, :, None, :] * pw[1][:, :tc][:, None, :, :]
    w_mat = jnp.concatenate([jnp.real(wf), jnp.real(wb), jnp.imag(wf), jnp.imag(wb)], axis=-1)
    w_mat = w_mat.reshape(SSM_GROUPS, CHUNK_W, STATE_W)
    gf = pw[0][:, 1:tc + 1][:, :, None, :] * c_c[0][:, None, :, :]
    gb = pw[1][:, tc:0:-1][:, :, None, :] * c_c[1][:, None, :, :]
    v_mat = jnp.concatenate([jnp.real(gf), jnp.real(gb), -jnp.imag(gf), -jnp.imag(gb)], axis=-1)
    v_mat = v_mat.reshape(SSM_GROUPS, CHUNK_W, STATE_W).transpose(0, 2, 1)
    n_idx = jnp.arange(SLAB_CHUNKS + 1, dtype=_F32) * tc
    pc = jnp.exp(lam_dt[:, :, None, :] * n_idx[None, None, :, None])
    coef = jnp.concatenate([pc[0][:, SLAB_CHUNKS - 1::-1], pc[1][:, :SLAB_CHUNKS]], axis=-1)
    both = lambda n: jnp.concatenate([pc[0][:, n], pc[1][:, n]], axis=-1)[:, None, :]
    a_slab, a_chunk = both(SLAB_CHUNKS), both(1)
    tab = jnp.concatenate([jnp.real(coef), jnp.imag(coef), jnp.real(a_slab), jnp.imag(a_slab),
                           jnp.real(a_chunk), jnp.imag(a_chunk)], axis=1)
    tab = jnp.pad(tab, ((0, 0), (0, 2 * SUBLANES - tab.shape[1]), (0, 0)))
    return m_mat, w_mat, v_mat, tab


def _slab_permutations():
    r = lax.broadcasted_iota(jnp.int32, (SLAB_W, SLAB_W), 0)
    c = lax.broadcasted_iota(jnp.int32, (SLAB_W, SLAB_W), 1)
    ch, tok = r // SLAB_T, r % SLAB_T
    pin = c == (tok // SSM_CHUNK) * CHUNK_W + ch * SSM_CHUNK + tok % SSM_CHUNK
    ch, tok = c // SLAB_T, c % SLAB_T
    pout = r == (tok // SSM_CHUNK) * CHUNK_W + (tok % SSM_CHUNK) * SSM_GROUP + ch
    return pin.astype(_BF), pout.astype(_BF)


def _post_kernel(x_ref, attn_ref, y_ref, wglu_ref, gmix_ref, wout_ref, gpm_ref, gpre_ref,
                 wup_ref, wdn_ref, gpost_ref, o_ref):
    nslab = y_ref.shape[1]
    yt = jnp.concatenate([jnp.concatenate([y_ref[g, c] for c in range(nslab)], axis=1)
                          for g in range(SSM_GROUPS)], axis=0)
    y = yt.T
    gy = 0.5 * y * (1.0 + jnp.tanh(math.sqrt(2.0 / math.pi) * (y + 0.044715 * (y * y * y))))
    z = _dot(gy.astype(_BF), wglu_ref[...])
    ssm = z[:, :SSM_WIDTH] * (1.0 / (1.0 + jnp.exp(-z[:, SSM_WIDTH:])))
    gmix = gmix_ref[...]
    mix = jnp.concatenate([_rms(attn_ref[...], gmix[:, :ATTN_WIDTH]),
                           _rms(ssm, gmix[:, ATTN_WIDTH:])], axis=-1).astype(_BF)
    h1 = x_ref[...] + _rms(_dot(mix, wout_ref[...]), gpm_ref[...])
    hn = _rms(h1, gpre_ref[...]).astype(_BF)
    acc = None
    for c in range(D_FF // FF_TILE):
        up = _dot(hn, wup_ref[:, c * FF_TILE:(c + 1) * FF_TILE])
        up = jnp.maximum(up, 0.0)
        part = _dot((up * up).astype(_BF), wdn_ref[c * FF_TILE:(c + 1) * FF_TILE, :])
        acc = part if acc is None else acc + part
    o_ref[...] = h1 + _rms(acc, gpost_ref[...])


def _post_call(x, attn, y, wglu, gmix, wout, gpm, gpre, wup, wdn, gpost, *, tile):
    bsz, seq, _ = x.shape
    row_spec = lambda w: pl.BlockSpec((None, tile, w), lambda b, i: (b, i, 0))
    wspec = lambda shape: pl.BlockSpec(shape, lambda b, i: (0, 0), pipeline_mode=pl.Buffered(1))
    y_spec = pl.BlockSpec((SSM_GROUPS, tile // SLAB_T, SSM_GROUP, SLAB_T), lambda b, i: (0, i, b, 0))
    return pl.pallas_call(
        _post_kernel,
        grid=(bsz, seq // tile),
        in_specs=[row_spec(D_MODEL), row_spec(ATTN_WIDTH), y_spec,
                  wspec((SSM_WIDTH, 2 * SSM_WIDTH)), wspec((1, D_MODEL)), wspec((D_MODEL, D_MODEL)),
                  wspec((1, D_MODEL)), wspec((1, D_MODEL)), wspec((D_MODEL, D_FF)),
                  wspec((D_FF, D_MODEL)), wspec((1, D_MODEL))],
        out_specs=row_spec(D_MODEL),
        out_shape=jax.ShapeDtypeStruct((bsz, seq, D_MODEL), _F32),
        compiler_params=pltpu.CompilerParams(
            dimension_semantics=("parallel", "parallel"), vmem_limit_bytes=VMEM_LIMIT),
        name="post",
    )(x, attn, y, wglu, gmix, wout, gpm, gpre, wup, wdn, gpost)


def _rope_partner(w):
    half = QK_ROPE_DIM // 2
    return jnp.concatenate([-w[..., half:], w[..., :half]], axis=-1)


def _rope_tables(pos, tile):
    half = QK_ROPE_DIM // 2
    inv = 1.0 / (ROPE_BASE ** (jnp.arange(0, QK_ROPE_DIM, 2, dtype=_F32) / QK_ROPE_DIM))
    ang = pos.astype(_F32)[..., None] * inv
    cos, sin = jnp.cos(ang), jnp.sin(ang)
    bsz, seq = pos.shape
    rope_k = jnp.concatenate([jnp.zeros((bsz, seq, QK_NOPE_DIM), _F32), cos, cos, sin, sin], axis=-1)
    rope_q = jnp.stack([cos, sin], axis=1).reshape(bsz, 2, seq // tile, tile, half).transpose(0, 2, 1, 4, 3)
    return rope_k, rope_q


def _prep_weights(w_in, w_uq, w_ukv):
    scale = QK_HEAD_DIM ** -0.5 * math.log2(math.e)
    w_kr = w_in[:, OFF_KR:OFF_U]
    win = jnp.concatenate([w_in[:, :OFF_KR], jnp.zeros((D_MODEL, QK_NOPE_DIM), _F32),
                           w_kr, _rope_partner(w_kr)], axis=1).astype(_BF)
    wu_t = w_in[:, OFF_U:].T.astype(_BF)
    wq_t = (w_uq * scale).T.astype(_BF)
    wkv3 = w_ukv.reshape(KV_LORA_RANK, N_HEADS, QK_NOPE_DIM + V_HEAD_DIM)
    wk = jnp.concatenate([wkv3[..., :QK_NOPE_DIM],
                          jnp.zeros((KV_LORA_RANK, N_HEADS, HEAD_PAD - QK_NOPE_DIM), _F32)], axis=-1)
    wk = wk.reshape(KV_LORA_RANK, N_HEADS * HEAD_PAD).astype(_BF)
    wv_t = wkv3[..., QK_NOPE_DIM:].reshape(KV_LORA_RANK, N_HEADS * V_HEAD_DIM).T.astype(_BF)
    return win, wu_t, wq_t, wk, wv_t


def kernel(x, positions, meta_tokens, g_pre_mix, w_in, g_q_lat, w_uq, g_kv_lat, w_ukv,
           ssm_A_re, ssm_A_im, ssm_log_dt, ssm_B_re, ssm_B_im, ssm_C_re, ssm_C_im, ssm_D,
           w_glu, g_mix_out, w_out, g_post_mix, g_pre_mlp, w_mlp_up, w_mlp_down, g_post_mlp):
    bsz, seq, _ = x.shape
    assert seq % ROW_TILE == 0 and ROW_TILE % SLAB_T == 0 and bsz <= SUBLANES
    assert N_META <= SSM_CHUNK
    row = lambda g: g.reshape(1, -1).astype(_F32)

    win, wu_t, wq_t, wk, wv_t = _prep_weights(w_in[0], w_uq[0], w_ukv[0])
    rope_k, rope_q = _rope_tables(positions.astype(jnp.int32) + N_META, ROW_TILE)
    rope_k_m, _ = _rope_tables(jnp.arange(N_META, dtype=jnp.int32)[None], N_META)
    weights = (row(g_pre_mix[0]), win, wu_t, row(g_q_lat[0]), wq_t, row(g_kv_lat[0]), wk, wv_t)

    qt, k, vt, u, q_norm, k_sq = _proj_call(x, rope_k, rope_q, *weights, tile=ROW_TILE, meta=False)
    k_m, vt_m, u_m = _proj_call(meta_tokens.astype(x.dtype)[None], rope_k_m, None, *weights,
                                tile=N_META, meta=True)
    km = jnp.pad(k_m[0], ((0, 0), (0, LANES - N_META), (0, 0)))
    vtm = jnp.pad(vt_m[0, :, 0], ((0, 0), (0, 0), (0, LANES - N_META)))
    k_m_sq = jnp.max(jnp.sum(jnp.square(k_m[0].astype(_F32)), axis=-1), axis=-1)
    k_max = jnp.sqrt(jnp.maximum(jnp.max(k_sq, axis=(2, 3, 4)), k_m_sq[None]))
    mrow = q_norm * (k_max * BOUND_SLACK)[:, :, None, None, None]
    attn = lax.cond(jnp.max(mrow) <= BOUND_LIMIT,
                    lambda: _attn_call(qt, mrow, k, vt, km, vtm, bounded=True),
                    lambda: _attn_call(qt, mrow, k, vt, km, vtm, bounded=False))

    um = u_m[0].astype(_BF).reshape(N_META, SSM_GROUPS, SSM_GROUP).transpose(1, 2, 0)
    um = jnp.pad(um, ((0, 0), (0, 0), (SSM_CHUNK - N_META, 0))).reshape(SSM_GROUPS, 1, CHUNK_W)
    um = jnp.broadcast_to(um, (SSM_GROUPS, SUBLANES, CHUNK_W))
    m_mat, w_mat, v_mat, tab = _s5_matrices(ssm_A_re[0], ssm_A_im[0], ssm_log_dt[0], ssm_B_re[0],
                                            ssm_B_im[0], ssm_C_re[0], ssm_C_im[0], ssm_D[0])
    pin, pout = _slab_permutations()
    nslab = seq // SLAB_T
    yg = _s5_call(u.reshape(SSM_GROUPS, nslab * bsz * SSM_GROUP, SLAB_T), um, pin, pout,
                  m_mat.astype(_BF), w_mat.astype(_BF), v_mat.astype(_BF), tab, nslab=nslab, bsz=bsz)
    y = yg.reshape(SSM_GROUPS, nslab, bsz * SSM_GROUP, SLAB_T)

    return _post_call(x, attn, y, w_glu[0].astype(_BF), row(g_mix_out[0]), w_out[0].astype(_BF),
                      row(g_post_mix[0]), row(g_pre_mlp[0]), w_mlp_up[0].astype(_BF),
                      w_mlp_down[0].astype(_BF), row(g_post_mlp[0]), tile=ROW_TILE)
```

```python
import functools
import math

import jax
import jax.numpy as jnp
from jax import lax
from jax.experimental import pallas as pl
from jax.experimental.pallas import tpu as pltpu

D_MODEL = 1024
N_META = 16
ATTN_WIDTH = 512
SSM_WIDTH = 512
N_HEADS = 8
V_HEAD_DIM = 64
QK_NOPE_DIM = 64
QK_ROPE_DIM = 32
QK_HEAD_DIM = QK_NOPE_DIM + QK_ROPE_DIM
Q_LORA_RANK = 384
KV_LORA_RANK = 256
ROPE_BASE = 10000.0
SSM_GROUP = 16
SSM_GROUPS = 32
SSM_STATE = 64
D_FF = 4 * D_MODEL
EPS = 1e-6
OFF_KV = Q_LORA_RANK
OFF_KR = OFF_KV + KV_LORA_RANK
OFF_U = OFF_KR + QK_ROPE_DIM

LANES = 128
SUBLANES = 8
HEAD_PAD = LANES
ONES_LANE = V_HEAD_DIM
PW_Q = 0
PW_KV = PW_Q + Q_LORA_RANK
PW_KR = PW_KV + KV_LORA_RANK
PW_END = PW_KR + HEAD_PAD

SSM_CHUNK = 32
CHUNK_W = SSM_CHUNK * SSM_GROUP
STATE_W = 4 * SSM_STATE
SLAB_T = LANES
SLAB_CHUNKS = SLAB_T // SSM_CHUNK
SLAB_W = SSM_GROUP * SLAB_T

ROW_TILE = 512
FF_TILE = 1024
BOUND_SLACK = 1.0 + 2.0 ** -6
BOUND_LIMIT = 60.0
VMEM_LIMIT = 56 * 1024 * 1024

_BF = jnp.bfloat16
_F32 = jnp.float32


def _dot(a, b):
    return jnp.dot(a, b, preferred_element_type=_F32)


def _rms(x, g):
    return x * lax.rsqrt(jnp.mean(x * x, axis=-1, keepdims=True) + EPS) * g


_NT = (((1,), (1,)), ((), ()))


def _proj_kernel(x_ref, rope_k_ref, *refs, meta):
    if meta:
        gpre_ref, win_ref, wu_ref, gkv_ref, wk_ref, wv_ref, k_ref, v_ref, u_ref = refs
    else:
        (rope_q_ref, gpre_ref, win_ref, wu_ref, gq_ref, wq_ref, gkv_ref, wk_ref, wv_ref,
         q_ref, k_ref, v_ref, u_ref, qn_ref, kmx_ref) = refs
    tile = x_ref.shape[0]
    xn = _rms(x_ref[...], gpre_ref[...]).astype(_BF)
    proj = _dot(xn, win_ref[...])
    kvn = _rms(proj[:, PW_KV:PW_KR], gkv_ref[...]).astype(_BF)
    kr_terms = proj[:, PW_KR:PW_END] * rope_k_ref[...]
    lane = lax.broadcasted_iota(jnp.int32, (1, HEAD_PAD), 1)
    rope_lanes = (lane >= QK_NOPE_DIM) & (lane < QK_HEAD_DIM)
    kr = jnp.where(rope_lanes, kr_terms + pltpu.roll(kr_terms, HEAD_PAD - QK_ROPE_DIM, 1), 0.0)
    kk = _dot(kvn, wk_ref[...])
    vt = lax.dot_general(wv_ref[...], kvn, _NT, preferred_element_type=_F32)
    ones_tail = (lax.broadcasted_iota(jnp.int32, (HEAD_PAD - V_HEAD_DIM, tile), 0) == 0).astype(_F32)
    if meta:
        u_ref[...] = lax.dot_general(xn, wu_ref[...], _NT, preferred_element_type=_F32)
    else:
        ut = lax.dot_general(wu_ref[...], xn, _NT, preferred_element_type=_F32)
        for g in range(SSM_GROUPS):
            for c in range(tile // SLAB_T):
                u_ref[g, c] = ut[g * SSM_GROUP:(g + 1) * SSM_GROUP, c * SLAB_T:(c + 1) * SLAB_T]
        qn = _rms(proj[:, PW_Q:PW_KV], gq_ref[...]).astype(_BF)
        qt = lax.dot_general(wq_ref[...], qn, _NT, preferred_element_type=_F32)
        cos_t, sin_t = rope_q_ref[0], rope_q_ref[1]
        zero_rows = jnp.zeros((HEAD_PAD - QK_HEAD_DIM, tile), _F32)
    half = QK_ROPE_DIM // 2
    for h in range(N_HEADS):
        k_h = (kk[:, h * HEAD_PAD:(h + 1) * HEAD_PAD] + kr).astype(_BF)
        k_ref[h] = k_h
        v_ref[h] = jnp.concatenate([vt[h * V_HEAD_DIM:(h + 1) * V_HEAD_DIM], ones_tail], axis=0).astype(_BF)
        if not meta:
            blk = qt[h * QK_HEAD_DIM:(h + 1) * QK_HEAD_DIM]
            x1 = blk[QK_NOPE_DIM:QK_NOPE_DIM + half]
            x2 = blk[QK_NOPE_DIM + half:]
            qt_h = jnp.concatenate([blk[:QK_NOPE_DIM], x1 * cos_t - x2 * sin_t, x1 * sin_t + x2 * cos_t,
                                    zero_rows], axis=0).astype(_BF)
            q_ref[h] = qt_h
            qt_f = qt_h.astype(_F32)
            qn_ref[h] = jnp.sqrt(jnp.sum(qt_f * qt_f, axis=0, keepdims=True))
            k_f = k_h.astype(_F32)
            kmx_ref[h] = jnp.broadcast_to(jnp.max(jnp.sum(k_f * k_f, axis=1, keepdims=True), axis=0, keepdims=True),
                                          (1, LANES))


def _const_spec(shape):
    nd = len(shape)
    return pl.BlockSpec(shape, lambda *_: (0,) * nd)


def _proj_call(x, rope_k, rope_q, gpre, win, wu_t, gq, wq_t, gkv, wk, wv_t, *, tile, meta):
    bsz, seq, _ = x.shape
    nt = seq // tile
    row_spec = lambda w: pl.BlockSpec((None, tile, w), lambda b, i: (b, i, 0))
    k_spec = pl.BlockSpec((None, N_HEADS, tile, HEAD_PAD), lambda b, i: (b, 0, i, 0))
    t_spec = pl.BlockSpec((None, N_HEADS, None, HEAD_PAD, tile), lambda b, i: (b, 0, i, 0, 0))
    k_shape = jax.ShapeDtypeStruct((bsz, N_HEADS, seq, HEAD_PAD), _BF)
    t_shape = jax.ShapeDtypeStruct((bsz, N_HEADS, nt, HEAD_PAD, tile), _BF)
    w_specs = lambda *ws: [_const_spec(w.shape) for w in ws]
    if meta:
        args = (x, rope_k, gpre, win, wu_t, gkv, wk, wv_t)
        in_specs = [row_spec(D_MODEL), row_spec(HEAD_PAD)] + w_specs(*args[2:])
        out_specs = [k_spec, t_spec, row_spec(SSM_WIDTH)]
        out_shape = [k_shape, t_shape, jax.ShapeDtypeStruct((bsz, seq, SSM_WIDTH), _F32)]
    else:
        args = (x, rope_k, rope_q, gpre, win, wu_t, gq, wq_t, gkv, wk, wv_t)
        rope_q_spec = pl.BlockSpec((None, None, 2, QK_ROPE_DIM // 2, tile), lambda b, i: (b, i, 0, 0, 0))
        in_specs = [row_spec(D_MODEL), row_spec(HEAD_PAD), rope_q_spec] + w_specs(*args[3:])
        norm_spec = lambda w: pl.BlockSpec((None, N_HEADS, None, 1, w), lambda b, i: (b, 0, i, 0, 0))
        u_spec = pl.BlockSpec((SSM_GROUPS, tile // SLAB_T, SSM_GROUP, SLAB_T), lambda b, i: (0, i, b, 0))
        out_specs = [t_spec, k_spec, t_spec, u_spec, norm_spec(tile), norm_spec(LANES)]
        out_shape = [t_shape, k_shape, t_shape,
                     jax.ShapeDtypeStruct((SSM_GROUPS, seq // SLAB_T, bsz * SSM_GROUP, SLAB_T), _F32),
                     jax.ShapeDtypeStruct((bsz, N_HEADS, nt, 1, tile), _F32),
                     jax.ShapeDtypeStruct((bsz, N_HEADS, nt, 1, LANES), _F32)]
    return pl.pallas_call(
        functools.partial(_proj_kernel, meta=meta),
        grid=(bsz, nt),
        in_specs=in_specs,
        out_specs=out_specs,
        out_shape=out_shape,
        compiler_params=pltpu.CompilerParams(
            dimension_semantics=("parallel", "parallel"), vmem_limit_bytes=VMEM_LIMIT),
        name="proj_meta" if meta else "proj",
    )(*args)


def _attn_finish(accs, o_ref):
    halves = [(acc * (1.0 / acc[ONES_LANE:ONES_LANE + 1, :]))[:V_HEAD_DIM] for acc in accs]
    o_ref[...] = jnp.concatenate(halves, axis=0).T


def _attn_bounded_kernel(qt_ref, mrow_ref, k_ref, vt_ref, km_ref, vtm_ref, o_ref, *, nk, tk):
    tq = qt_ref.shape[2]
    key_row = lax.broadcasted_iota(jnp.int32, (LANES, tq), 0)
    accs = []
    for hh in range(2):
        qt = qt_ref[hh]
        mrow = mrow_ref[hh]
        s0 = jnp.where(key_row < N_META, _dot(km_ref[hh], qt), -jnp.inf)
        acc = _dot(vtm_ref[hh], jnp.exp2(s0 - mrow).astype(_BF))
        scores = lambda c: _dot(k_ref[hh, c * tk:(c + 1) * tk, :], qt)
        s_next = scores(0)
        for c in range(nk):
            s = s_next
            if c + 1 < nk:
                s_next = scores(c + 1)
            acc = acc + _dot(vt_ref[hh, c], jnp.exp2(s - mrow).astype(_BF))
        accs.append(acc)
    _attn_finish(accs, o_ref)


def _attn_online_kernel(qt_ref, k_ref, vt_ref, km_ref, vtm_ref, o_ref, s0_scr, s1_scr, m_scr, acc_scr, *, nk, tk):
    tq = qt_ref.shape[2]
    key_row = lax.broadcasted_iota(jnp.int32, (LANES, tq), 0)
    for hh in range(2):
        s0 = jnp.where(key_row < N_META, _dot(km_ref[hh], qt_ref[hh]), -jnp.inf)
        m0 = jnp.max(s0, axis=0, keepdims=True)
        m_scr[hh] = m0
        acc_scr[hh] = _dot(vtm_ref[hh], jnp.exp2(s0 - m0).astype(_BF))

    def scores(buf, c):
        off = pl.multiple_of(c * tk, tk)
        for hh in range(2):
            buf[hh] = _dot(k_ref[hh, pl.ds(off, tk), :], qt_ref[hh])

    def accumulate(buf, c):
        for hh in range(2):
            s = buf[hh]
            m = m_scr[hh]
            m_new = jnp.maximum(m, jnp.max(s, axis=0, keepdims=True))
            m_scr[hh] = m_new
            p = jnp.exp2(s - m_new).astype(_BF)
            acc_scr[hh] = jnp.exp2(m - m_new) * acc_scr[hh] + _dot(vt_ref[hh, c], p)

    scores(s0_scr, 0)

    def body(t, _):
        scores(s1_scr, 2 * t + 1)
        accumulate(s0_scr, 2 * t)
        scores(s0_scr, 2 * t + 2)
        accumulate(s1_scr, 2 * t + 1)
        return 0

    lax.fori_loop(0, nk // 2 - 1, body, 0)
    scores(s1_scr, nk - 1)
    accumulate(s0_scr, nk - 2)
    accumulate(s1_scr, nk - 1)
    _attn_finish([acc_scr[0], acc_scr[1]], o_ref)


def _attn_call(qt, mrow, k, vt, km, vtm, *, bounded):
    bsz, _, nq, _, tq = qt.shape
    nk, tk = vt.shape[2], vt.shape[4]
    seq = k.shape[2]
    assert nk % 2 == 0 and nk >= 4
    q_spec = pl.BlockSpec((None, 2, None, HEAD_PAD, tq), lambda b, hp, i: (b, hp, i, 0, 0))
    kv_specs = [pl.BlockSpec((None, 2, seq, HEAD_PAD), lambda b, hp, i: (b, hp, 0, 0)),
                pl.BlockSpec((None, 2, nk, HEAD_PAD, tk), lambda b, hp, i: (b, hp, 0, 0, 0)),
                pl.BlockSpec((2, LANES, HEAD_PAD), lambda b, hp, i: (hp, 0, 0)),
                pl.BlockSpec((2, HEAD_PAD, LANES), lambda b, hp, i: (hp, 0, 0))]
    if bounded:
        body = functools.partial(_attn_bounded_kernel, nk=nk, tk=tk)
        in_specs = [q_spec, pl.BlockSpec((None, 2, None, 1, tq), lambda b, hp, i: (b, hp, i, 0, 0))] + kv_specs
        args, scratch = (qt, mrow, k, vt, km, vtm), []
    else:
        body = functools.partial(_attn_online_kernel, nk=nk, tk=tk)
        in_specs = [q_spec] + kv_specs
        args = (qt, k, vt, km, vtm)
        scratch = [pltpu.VMEM((2, tk, tq), _F32), pltpu.VMEM((2, tk, tq), _F32),
                   pltpu.VMEM((2, 1, tq), _F32), pltpu.VMEM((2, HEAD_PAD, tq), _F32)]
    return pl.pallas_call(
        body,
        grid=(bsz, N_HEADS // 2, nq),
        in_specs=in_specs,
        out_specs=pl.BlockSpec((None, tq, LANES), lambda b, hp, i: (b, i, hp)),
        out_shape=jax.ShapeDtypeStruct((bsz, nq * tq, ATTN_WIDTH), _F32),
        scratch_shapes=scratch,
        compiler_params=pltpu.CompilerParams(
            dimension_semantics=("parallel", "parallel", "arbitrary"), vmem_limit_bytes=VMEM_LIMIT),
        name="attn_bounded" if bounded else "attn_online",
    )(*args)


def _cmul_add(ar, ai, xr, xi, sr, si):
    return ar * xr - ai * xi + sr, ar * xi + ai * xr + si


def _s5_kernel(a_ref, um_ref, pin_ref, pout_ref, lag_ref, w_ref, v_ref, t_ref, y_ref, sup_scr, ent_scr,
               *, nslab, bsz):
    rows = nslab * bsz
    half = STATE_W // 2
    a = jnp.concatenate([a_ref[pl.ds(i, rows, stride=SSM_GROUP), :] for i in range(SSM_GROUP)], axis=1)
    ap = _dot(a.astype(_BF), pin_ref[...]).astype(_BF)
    uc = [ap[:, c * CHUNK_W:(c + 1) * CHUNK_W] for c in range(SLAB_CHUNKS)]
    w = w_ref[...]
    s = [_dot(u, w) for u in uc]
    sr = [x[:, :half] for x in s]
    si = [x[:, half:] for x in s]
    t = t_ref[...]
    trow = lambda r: t[r:r + 1, :]
    sup_r = sup_i = None
    for c in range(SLAB_CHUNKS):
        cr, ci = trow(c), trow(SLAB_CHUNKS + c)
        pr = cr * sr[c] - ci * si[c]
        pi = cr * si[c] + ci * sr[c]
        sup_r = pr if sup_r is None else sup_r + pr
        sup_i = pi if sup_i is None else sup_i + pi
    sup_scr[:, :half] = sup_r
    sup_scr[:, half:] = sup_i

    lane = lax.broadcasted_iota(jnp.int32, (bsz, half), 1)
    fwd = lane < SSM_STATE
    sm = _dot(um_ref[...], w)
    xr = jnp.where(fwd, sm[:bsz, :half], 0.0)
    xi = jnp.where(fwd, sm[:bsz, half:], 0.0)
    a_slab_r, a_slab_i = trow(2 * SLAB_CHUNKS), trow(2 * SLAB_CHUNKS + 1)
    for j in range(nslab):
        rf = j * bsz
        rb = (nslab - 1 - j) * bsz
        ent_scr[rf:rf + bsz, 0:SSM_STATE] = xr[:, 0:SSM_STATE]
        ent_scr[rb:rb + bsz, SSM_STATE:half] = xr[:, SSM_STATE:half]
        ent_scr[rf:rf + bsz, half:half + SSM_STATE] = xi[:, 0:SSM_STATE]
        ent_scr[rb:rb + bsz, half + SSM_STATE:STATE_W] = xi[:, SSM_STATE:half]
        s_r = jnp.where(fwd, sup_scr[rf:rf + bsz, :half], sup_scr[rb:rb + bsz, :half])
        s_i = jnp.where(fwd, sup_scr[rf:rf + bsz, half:], sup_scr[rb:rb + bsz, half:])
        xr, xi = _cmul_add(a_slab_r, a_slab_i, xr, xi, s_r, s_i)

    ent = ent_scr[...]
    a_r, a_i = trow(2 * SLAB_CHUNKS + 2), trow(2 * SLAB_CHUNKS + 3)
    xf = [(ent[:, :half], ent[:, half:])]
    for c in range(1, SLAB_CHUNKS):
        xf.append(_cmul_add(a_r, a_i, xf[-1][0], xf[-1][1], sr[c - 1], si[c - 1]))
    xb = [(ent[:, :half], ent[:, half:])]
    for c in range(SLAB_CHUNKS - 2, -1, -1):
        xb.insert(0, _cmul_add(a_r, a_i, xb[0][0], xb[0][1], sr[c + 1], si[c + 1]))
    fwd_rows = lax.broadcasted_iota(jnp.int32, (rows, half), 1) < SSM_STATE
    m = jnp.concatenate(
        [pltpu.roll(jnp.broadcast_to(lag_ref[i:i + 1, :], (SSM_CHUNK, 2 * CHUNK_W)), 0, 1,
                    stride=SSM_GROUP, stride_axis=0)[:, :CHUNK_W] for i in range(SSM_GROUP)], axis=0).astype(_BF)
    v = v_ref[...]
    ys = []
    for c in range(SLAB_CHUNKS):
        xin = jnp.concatenate([jnp.where(fwd_rows, xf[c][0], xb[c][0]),
                               jnp.where(fwd_rows, xf[c][1], xb[c][1])], axis=1).astype(_BF)
        ys.append((_dot(uc[c], m) + _dot(xin, v)).astype(_BF))
    yp = _dot(jnp.concatenate(ys, axis=1), pout_ref[...])
    for o in range(SSM_GROUP):
        y_ref[pl.ds(o, rows, stride=SSM_GROUP), :] = yp[:, o * SLAB_T:(o + 1) * SLAB_T]


def _s5_call(a, um, pin, pout, lags, w_mat, v_mat, tab, *, nslab, bsz):
    n = nslab * bsz * SSM_GROUP
    rows = nslab * bsz
    g_spec = lambda *shape: pl.BlockSpec((None,) + shape, lambda g: (g,) + (0,) * len(shape))
    perm_spec = pl.BlockSpec((SLAB_W, SLAB_W), lambda g: (0, 0), pipeline_mode=pl.Buffered(1))
    return pl.pallas_call(
        functools.partial(_s5_kernel, nslab=nslab, bsz=bsz),
        grid=(SSM_GROUPS,),
        in_specs=[g_spec(n, SLAB_T), g_spec(SUBLANES, CHUNK_W), perm_spec, perm_spec,
                  g_spec(SSM_GROUP, 2 * CHUNK_W), g_spec(CHUNK_W, STATE_W), g_spec(STATE_W, CHUNK_W),
                  g_spec(2 * SUBLANES, STATE_W // 2)],
        out_specs=g_spec(n, SLAB_T),
        out_shape=jax.ShapeDtypeStruct((SSM_GROUPS, n, SLAB_T), _F32),
        scratch_shapes=[pltpu.VMEM((rows, STATE_W), _F32), pltpu.VMEM((rows, STATE_W), _F32)],
        compiler_params=pltpu.CompilerParams(
            dimension_semantics=("parallel",), vmem_limit_bytes=VMEM_LIMIT),
        name="s5",
    )(a, um, pin, pout, lags, w_mat, v_mat, tab)


def _s5_matrices(a_re, a_im, log_dt, b_re, b_im, c_re, c_im, d_skip):
    tc = SSM_CHUNK
    lam = lax.complex(jnp.minimum(a_re.astype(_F32), -1e-4), a_im.astype(_F32))
    dt = jnp.exp(log_dt.astype(_F32))[..., None]
    lam_dt = lam * dt
    lam_bar = jnp.exp(lam_dt)
    b_bar = ((lam_bar - 1.0) / lam)[..., None] * lax.complex(b_re.astype(_F32), b_im.astype(_F32))
    c_c = lax.complex(c_re.astype(_F32), c_im.astype(_F32))
    k_idx = jnp.arange(tc + 1, dtype=_F32)
    pw = jnp.exp(lam_dt[:, :, None, :] * k_idx[None, None, :, None])
    kern = jnp.real(jnp.einsum('dgop,dgkp,dgpi->dgkoi', c_c, pw[:, :, :tc], b_bar))
    d_g = d_skip.astype(_F32).reshape(SSM_GROUPS, SSM_GROUP)
    center = kern[0][:, :1] + kern[1][:, :1] + (jnp.eye(SSM_GROUP, dtype=_F32)[None] * d_g[:, :, None])[:, None]
    lags = jnp.concatenate([center, kern[0][:, 1:], jnp.zeros_like(center), kern[1][:, :0:-1]], axis=1)
    lags = lags.transpose(0, 3, 1, 2).reshape(SSM_GROUPS, SSM_GROUP, 2 * CHUNK_W)
    wf = b_bar[0].transpose(0, 2, 1)[:, :, None, :] * pw[0][:, tc - 1::-1][:, None, :, :]
    wb = b_bar[1].transpose(0, 2, 1)[:, :, None, :] * pw[1][:, :tc][:, None, :, :]
    w_mat = jnp.concatenate([jnp.real(wf), jnp.real(wb), jnp.imag(wf), jnp.imag(wb)], axis=-1)
    w_mat = w_mat.reshape(SSM_GROUPS, CHUNK_W, STATE_W)
    gf = pw[0][:, 1:tc + 1][:, :, None, :] * c_c[0][:, None, :, :]
    gb = pw[1][:, tc:0:-1][:, :, None, :] * c_c[1][:, None, :, :]
    v_mat = jnp.concatenate([jnp.real(gf), jnp.real(gb), -jnp.imag(gf), -jnp.imag(gb)], axis=-1)
    v_mat = v_mat.reshape(SSM_GROUPS, CHUNK_W, STATE_W).transpose(0, 2, 1)
    n_idx = jnp.arange(SLAB_CHUNKS + 1, dtype=_F32) * tc
    pc = jnp.exp(lam_dt[:, :, None, :] * n_idx[None, None, :, None])
    coef = jnp.concatenate([pc[0][:, SLAB_CHUNKS - 1::-1], pc[1][:, :SLAB_CHUNKS]], axis=-1)
    both = lambda n: jnp.concatenate([pc[0][:, n], pc[1][:, n]], axis=-1)[:, None, :]
    a_slab, a_chunk = both(SLAB_CHUNKS), both(1)
    tab = jnp.concatenate([jnp.real(coef), jnp.imag(coef), jnp.real(a_slab), jnp.imag(a_slab),
                           jnp.real(a_chunk), jnp.imag(a_chunk)], axis=1)
    tab = jnp.pad(tab, ((0, 0), (0, 2 * SUBLANES - tab.shape[1]), (0, 0)))
    return lags, w_mat, v_mat, tab


def _slab_permutations():
    r = lax.broadcasted_iota(jnp.int32, (SLAB_W, SLAB_W), 0)
    c = lax.broadcasted_iota(jnp.int32, (SLAB_W, SLAB_W), 1)
    ch, tok = r // SLAB_T, r % SLAB_T
    pin = c == (tok // SSM_CHUNK) * CHUNK_W + ch * SSM_CHUNK + tok % SSM_CHUNK
    ch, tok = c // SLAB_T, c % SLAB_T
    pout = r == (tok // SSM_CHUNK) * CHUNK_W + (tok % SSM_CHUNK) * SSM_GROUP + ch
    return pin.astype(_BF), pout.astype(_BF)


def _post_kernel(x_ref, attn_ref, y_ref, wglu_ref, gmix_ref, wout_ref, gpm_ref, gpre_ref,
                 wup_ref, wdn_ref, gpost_ref, o_ref):
    nslab = y_ref.shape[1]
    yt = jnp.concatenate([jnp.concatenate([y_ref[g, c] for c in range(nslab)], axis=1)
                          for g in range(SSM_GROUPS)], axis=0)
    y = yt.T
    gy = 0.5 * y * (1.0 + jnp.tanh(math.sqrt(2.0 / math.pi) * (y + 0.044715 * (y * y * y))))
    z = _dot(gy.astype(_BF), wglu_ref[...])
    ssm = z[:, :SSM_WIDTH] * (1.0 / (1.0 + jnp.exp(-z[:, SSM_WIDTH:])))
    gmix = gmix_ref[...]
    mix = jnp.concatenate([_rms(attn_ref[...], gmix[:, :ATTN_WIDTH]),
                           _rms(ssm, gmix[:, ATTN_WIDTH:])], axis=-1).astype(_BF)
    h1 = x_ref[...] + _rms(_dot(mix, wout_ref[...]), gpm_ref[...])
    hn = _rms(h1, gpre_ref[...]).astype(_BF)
    acc = None
    for c in range(D_FF // FF_TILE):
        up = _dot(hn, wup_ref[:, c * FF_TILE:(c + 1) * FF_TILE])
        up = jnp.maximum(up, 0.0)
        part = _dot((up * up).astype(_BF), wdn_ref[c * FF_TILE:(c + 1) * FF_TILE, :])
        acc = part if acc is None else acc + part
    o_ref[...] = h1 + _rms(acc, gpost_ref[...])


def _post_call(x, attn, y, wglu, gmix, wout, gpm, gpre, wup, wdn, gpost, *, tile):
    bsz, seq, _ = x.shape
    row_spec = lambda w: pl.BlockSpec((None, tile, w), lambda b, i: (b, i, 0))
    wspec = lambda shape: pl.BlockSpec(shape, lambda b, i: (0, 0), pipeline_mode=pl.Buffered(1))
    y_spec = pl.BlockSpec((SSM_GROUPS, tile // SLAB_T, SSM_GROUP, SLAB_T), lambda b, i: (0, i, b, 0))
    return pl.pallas_call(
        _post_kernel,
        grid=(bsz, seq // tile),
        in_specs=[row_spec(D_MODEL), row_spec(ATTN_WIDTH), y_spec,
                  wspec((SSM_WIDTH, 2 * SSM_WIDTH)), wspec((1, D_MODEL)), wspec((D_MODEL, D_MODEL)),
                  wspec((1, D_MODEL)), wspec((1, D_MODEL)), wspec((D_MODEL, D_FF)),
                  wspec((D_FF, D_MODEL)), wspec((1, D_MODEL))],
        out_specs=row_spec(D_MODEL),
        out_shape=jax.ShapeDtypeStruct((bsz, seq, D_MODEL), _F32),
        compiler_params=pltpu.CompilerParams(
            dimension_semantics=("parallel", "parallel"), vmem_limit_bytes=VMEM_LIMIT),
        name="post",
    )(x, attn, y, wglu, gmix, wout, gpm, gpre, wup, wdn, gpost)


def _rope_partner(w):
    half = QK_ROPE_DIM // 2
    return jnp.concatenate([-w[..., half:], w[..., :half]], axis=-1)


def _rope_tables(pos, tile):
    half = QK_ROPE_DIM // 2
    inv = 1.0 / (ROPE_BASE ** (jnp.arange(0, QK_ROPE_DIM, 2, dtype=_F32) / QK_ROPE_DIM))
    ang = pos.astype(_F32)[..., None] * inv
    cos, sin = lax.optimization_barrier((jnp.cos(ang), jnp.sin(ang)))
    bsz, seq = pos.shape
    rope_k = jnp.concatenate([jnp.zeros((bsz, seq, QK_NOPE_DIM), _F32), cos, cos, sin, sin], axis=-1)
    rope_q = jnp.stack([cos, sin], axis=1).reshape(bsz, 2, seq // tile, tile, half).transpose(0, 2, 1, 4, 3)
    return rope_k, rope_q


def _prep_weights(w_in, w_uq, w_ukv):
    scale = QK_HEAD_DIM ** -0.5 * math.log2(math.e)
    w_kr = w_in[:, OFF_KR:OFF_U]
    win = jnp.concatenate([w_in[:, :OFF_KR], jnp.zeros((D_MODEL, QK_NOPE_DIM), _F32),
                           w_kr, _rope_partner(w_kr)], axis=1).astype(_BF)
    wu_t = w_in[:, OFF_U:].T.astype(_BF)
    wq_t = (w_uq * scale).T.astype(_BF)
    wkv3 = w_ukv.reshape(KV_LORA_RANK, N_HEADS, QK_NOPE_DIM + V_HEAD_DIM)
    wk = jnp.concatenate([wkv3[..., :QK_NOPE_DIM],
                          jnp.zeros((KV_LORA_RANK, N_HEADS, HEAD_PAD - QK_NOPE_DIM), _F32)], axis=-1)
    wk = wk.reshape(KV_LORA_RANK, N_HEADS * HEAD_PAD).astype(_BF)
    wv_t = wkv3[..., QK_NOPE_DIM:].reshape(KV_LORA_RANK, N_HEADS * V_HEAD_DIM).T.astype(_BF)
    return win, wu_t, wq_t, wk, wv_t


def kernel(x, positions, meta_tokens, g_pre_mix, w_in, g_q_lat, w_uq, g_kv_lat, w_ukv,
           ssm_A_re, ssm_A_im, ssm_log_dt, ssm_B_re, ssm_B_im, ssm_C_re, ssm_C_im, ssm_D,
           w_glu, g_mix_out, w_out, g_post_mix, g_pre_mlp, w_mlp_up, w_mlp_down, g_post_mlp):
    bsz, seq, _ = x.shape
    assert seq % ROW_TILE == 0 and ROW_TILE % SLAB_T == 0 and bsz <= SUBLANES
    assert N_META <= SSM_CHUNK
    row = lambda g: g.reshape(1, -1).astype(_F32)

    win, wu_t, wq_t, wk, wv_t = _prep_weights(w_in[0], w_uq[0], w_ukv[0])
    rope_k, rope_q = _rope_tables(positions.astype(jnp.int32) + N_META, ROW_TILE)
    rope_k_m, _ = _rope_tables(jnp.arange(N_META, dtype=jnp.int32)[None], N_META)
    weights = (row(g_pre_mix[0]), win, wu_t, row(g_q_lat[0]), wq_t, row(g_kv_lat[0]), wk, wv_t)

    qt, k, vt, u, q_norm, k_sq = _proj_call(x, rope_k, rope_q, *weights, tile=ROW_TILE, meta=False)
    k_m, vt_m, u_m = _proj_call(meta_tokens.astype(x.dtype)[None], rope_k_m, None, *weights,
                                tile=N_META, meta=True)
    km = jnp.pad(k_m[0], ((0, 0), (0, LANES - N_META), (0, 0)))
    vtm = jnp.pad(vt_m[0, :, 0], ((0, 0), (0, 0), (0, LANES - N_META)))
    k_m_sq = jnp.max(jnp.sum(jnp.square(k_m[0].astype(_F32)), axis=-1), axis=-1)
    k_max = jnp.sqrt(jnp.maximum(jnp.max(k_sq, axis=(2, 3, 4)), k_m_sq[None]))
    mrow = q_norm * (k_max * BOUND_SLACK)[:, :, None, None, None]
    attn = lax.cond(jnp.max(mrow) <= BOUND_LIMIT,
                    lambda: _attn_call(qt, mrow, k, vt, km, vtm, bounded=True),
                    lambda: _attn_call(qt, mrow, k, vt, km, vtm, bounded=False))

    um = u_m[0].astype(_BF).reshape(N_META, SSM_GROUPS, SSM_GROUP).transpose(1, 2, 0)
    um = jnp.pad(um, ((0, 0), (0, 0), (SSM_CHUNK - N_META, 0))).reshape(SSM_GROUPS, 1, CHUNK_W)
    um = jnp.broadcast_to(um, (SSM_GROUPS, SUBLANES, CHUNK_W))
    lags, w_mat, v_mat, tab = _s5_matrices(ssm_A_re[0], ssm_A_im[0], ssm_log_dt[0], ssm_B_re[0],
                                            ssm_B_im[0], ssm_C_re[0], ssm_C_im[0], ssm_D[0])
    pin, pout = _slab_permutations()
    nslab = seq // SLAB_T
    yg = _s5_call(u.reshape(SSM_GROUPS, nslab * bsz * SSM_GROUP, SLAB_T), um, pin, pout,
                  lags, w_mat.astype(_BF), v_mat.astype(_BF), tab, nslab=nslab, bsz=bsz)
    y = yg.reshape(SSM_GROUPS, nslab, bsz * SSM_GROUP, SLAB_T)

    return _post_call(x, attn, y, w_glu[0].astype(_BF), row(g_mix_out[0]), w_out[0].astype(_BF),
                      row(g_post_mix[0]), row(g_pre_mlp[0]), w_mlp_up[0].astype(_BF),
                      w_mlp_down[0].astype(_BF), row(g_post_mlp[0]), tile=ROW_TILE)
```

```python
import functools
import math

import jax
import jax.numpy as jnp
from jax import lax
from jax.experimental import pallas as pl
from jax.experimental.pallas import tpu as pltpu

D_MODEL = 1024
N_META = 16
ATTN_WIDTH = 512
SSM_WIDTH = 512
N_HEADS = 8
V_HEAD_DIM = 64
QK_NOPE_DIM = 64
QK_ROPE_DIM = 32
QK_HEAD_DIM = QK_NOPE_DIM + QK_ROPE_DIM
Q_LORA_RANK = 384
KV_LORA_RANK = 256
ROPE_BASE = 10000.0
SSM_GROUP = 16
SSM_GROUPS = 32
SSM_STATE = 64
D_FF = 4 * D_MODEL
EPS = 1e-6
OFF_KV = Q_LORA_RANK
OFF_KR = OFF_KV + KV_LORA_RANK
OFF_U = OFF_KR + QK_ROPE_DIM

LANES = 128
SUBLANES = 8
HEAD_PAD = LANES
ONES_LANE = V_HEAD_DIM
PW_Q = 0
PW_KV = PW_Q + Q_LORA_RANK
PW_END = PW_KV + KV_LORA_RANK

SSM_CHUNK = 32
CHUNK_W = SSM_CHUNK * SSM_GROUP
STATE_W = 4 * SSM_STATE
SLAB_T = LANES
SLAB_CHUNKS = SLAB_T // SSM_CHUNK
SLAB_W = SSM_GROUP * SLAB_T

ROW_TILE = 512
FF_TILE = 1024
BOUND_SLACK = 1.0 + 2.0 ** -6
BOUND_LIMIT = 60.0
VMEM_LIMIT = 56 * 1024 * 1024

_BF = jnp.bfloat16
_F32 = jnp.float32


def _dot(a, b):
    return jnp.dot(a, b, preferred_element_type=_F32)


def _rms(x, g):
    return x * lax.rsqrt(jnp.mean(x * x, axis=-1, keepdims=True) + EPS) * g


_NT = (((1,), (1,)), ((), ()))


def _rotate(x1, x2, cos_t, sin_t):
    return x1 * cos_t - x2 * sin_t, x1 * sin_t + x2 * cos_t


def _proj_kernel(x_ref, rope_ref, *refs, meta):
    if meta:
        gpre_ref, win_ref, wkr_ref, wu_ref, gkv_ref, wk_ref, wv_ref, k_ref, v_ref, u_ref = refs
    else:
        (gpre_ref, win_ref, wkr_ref, wu_ref, gq_ref, wq_ref, gkv_ref, wk_ref, wv_ref,
         q_ref, k_ref, v_ref, u_ref, qn_ref, kmx_ref) = refs
    tile = x_ref.shape[0]
    half = QK_ROPE_DIM // 2
    cos_t, sin_t = rope_ref[0], rope_ref[1]
    xn = _rms(x_ref[...], gpre_ref[...]).astype(_BF)
    proj = _dot(xn, win_ref[...])
    kvn = _rms(proj[:, PW_KV:PW_END], gkv_ref[...]).astype(_BF)
    krt = lax.dot_general(wkr_ref[...], xn, _NT, preferred_element_type=_F32)
    r1, r2 = _rotate(krt[:half], krt[half:], cos_t, sin_t)
    kr = jnp.concatenate([jnp.zeros((QK_NOPE_DIM, tile), _F32), r1, r2,
                          jnp.zeros((HEAD_PAD - QK_HEAD_DIM, tile), _F32)], axis=0).T
    kk = _dot(kvn, wk_ref[...])
    vt = lax.dot_general(wv_ref[...], kvn, _NT, preferred_element_type=_F32)
    ones_tail = (lax.broadcasted_iota(jnp.int32, (HEAD_PAD - V_HEAD_DIM, tile), 0) == 0).astype(_F32)
    if meta:
        u_ref[...] = lax.dot_general(xn, wu_ref[...], _NT, preferred_element_type=_F32)
    else:
        ut = lax.dot_general(wu_ref[...], xn, _NT, preferred_element_type=_F32)
        for g in range(SSM_GROUPS):
            for c in range(tile // SLAB_T):
                u_ref[g, c] = ut[g * SSM_GROUP:(g + 1) * SSM_GROUP, c * SLAB_T:(c + 1) * SLAB_T]
        qn = _rms(proj[:, PW_Q:PW_KV], gq_ref[...]).astype(_BF)
        qt = lax.dot_general(wq_ref[...], qn, _NT, preferred_element_type=_F32)
        zero_rows = jnp.zeros((HEAD_PAD - QK_HEAD_DIM, tile), _F32)
    for h in range(N_HEADS):
        k_h = (kk[:, h * HEAD_PAD:(h + 1) * HEAD_PAD] + kr).astype(_BF)
        k_ref[h] = k_h
        v_ref[h] = jnp.concatenate([vt[h * V_HEAD_DIM:(h + 1) * V_HEAD_DIM], ones_tail], axis=0).astype(_BF)
        if not meta:
            blk = qt[h * QK_HEAD_DIM:(h + 1) * QK_HEAD_DIM]
            r1, r2 = _rotate(blk[QK_NOPE_DIM:QK_NOPE_DIM + half], blk[QK_NOPE_DIM + half:], cos_t, sin_t)
            qt_h = jnp.concatenate([blk[:QK_NOPE_DIM], r1, r2, zero_rows], axis=0).astype(_BF)
            q_ref[h] = qt_h
            qt_f = qt_h.astype(_F32)
            qn_ref[h] = jnp.sqrt(jnp.sum(qt_f * qt_f, axis=0, keepdims=True))
            k_f = k_h.astype(_F32)
            kmx_ref[h] = jnp.broadcast_to(jnp.max(jnp.sum(k_f * k_f, axis=1, keepdims=True), axis=0, keepdims=True),
                                          (1, LANES))


def _const_spec(shape):
    nd = len(shape)
    return pl.BlockSpec(shape, lambda *_: (0,) * nd)


def _proj_call(x, rope, gpre, win, wkr_t, wu_t, gq, wq_t, gkv, wk, wv_t, *, tile, meta):
    bsz, seq, _ = x.shape
    nt = seq // tile
    row_spec = lambda w: pl.BlockSpec((None, tile, w), lambda b, i: (b, i, 0))
    rope_spec = pl.BlockSpec((None, None, 2, QK_ROPE_DIM // 2, tile), lambda b, i: (b, i, 0, 0, 0))
    k_spec = pl.BlockSpec((None, N_HEADS, tile, HEAD_PAD), lambda b, i: (b, 0, i, 0))
    t_spec = pl.BlockSpec((None, N_HEADS, None, HEAD_PAD, tile), lambda b, i: (b, 0, i, 0, 0))
    k_shape = jax.ShapeDtypeStruct((bsz, N_HEADS, seq, HEAD_PAD), _BF)
    t_shape = jax.ShapeDtypeStruct((bsz, N_HEADS, nt, HEAD_PAD, tile), _BF)
    w_specs = lambda *ws: [_const_spec(w.shape) for w in ws]
    if meta:
        args = (x, rope, gpre, win, wkr_t, wu_t, gkv, wk, wv_t)
        in_specs = [row_spec(D_MODEL), rope_spec] + w_specs(*args[2:])
        out_specs = [k_spec, t_spec, row_spec(SSM_WIDTH)]
        out_shape = [k_shape, t_shape, jax.ShapeDtypeStruct((bsz, seq, SSM_WIDTH), _F32)]
    else:
        args = (x, rope, gpre, win, wkr_t, wu_t, gq, wq_t, gkv, wk, wv_t)
        in_specs = [row_spec(D_MODEL), rope_spec] + w_specs(*args[2:])
        norm_spec = lambda w: pl.BlockSpec((None, N_HEADS, None, 1, w), lambda b, i: (b, 0, i, 0, 0))
        u_spec = pl.BlockSpec((SSM_GROUPS, tile // SLAB_T, SSM_GROUP, SLAB_T), lambda b, i: (0, i, b, 0))
        out_specs = [t_spec, k_spec, t_spec, u_spec, norm_spec(tile), norm_spec(LANES)]
        out_shape = [t_shape, k_shape, t_shape,
                     jax.ShapeDtypeStruct((SSM_GROUPS, seq // SLAB_T, bsz * SSM_GROUP, SLAB_T), _F32),
                     jax.ShapeDtypeStruct((bsz, N_HEADS, nt, 1, tile), _F32),
                     jax.ShapeDtypeStruct((bsz, N_HEADS, nt, 1, LANES), _F32)]
    return pl.pallas_call(
        functools.partial(_proj_kernel, meta=meta),
        grid=(bsz, nt),
        in_specs=in_specs,
        out_specs=out_specs,
        out_shape=out_shape,
        compiler_params=pltpu.CompilerParams(
            dimension_semantics=("parallel", "parallel"), vmem_limit_bytes=VMEM_LIMIT),
        name="proj_meta" if meta else "proj",
    )(*args)


def _attn_finish(accs, o_ref):
    halves = [(acc * (1.0 / acc[ONES_LANE:ONES_LANE + 1, :]))[:V_HEAD_DIM] for acc in accs]
    o_ref[...] = jnp.concatenate(halves, axis=0).T


def _attn_bounded_kernel(qt_ref, mrow_ref, k_ref, vt_ref, km_ref, vtm_ref, o_ref, *, nk, tk):
    tq = qt_ref.shape[2]
    key_row = lax.broadcasted_iota(jnp.int32, (LANES, tq), 0)
    accs = []
    for hh in range(2):
        qt = qt_ref[hh]
        mrow = mrow_ref[hh]
        s0 = jnp.where(key_row < N_META, _dot(km_ref[hh], qt), -jnp.inf)
        acc = _dot(vtm_ref[hh], jnp.exp2(s0 - mrow).astype(_BF))
        scores = lambda c: _dot(k_ref[hh, c * tk:(c + 1) * tk, :], qt)
        s_next = scores(0)
        for c in range(nk):
            s = s_next
            if c + 1 < nk:
                s_next = scores(c + 1)
            acc = acc + _dot(vt_ref[hh, c], jnp.exp2(s - mrow).astype(_BF))
        accs.append(acc)
    _attn_finish(accs, o_ref)


def _attn_online_kernel(qt_ref, k_ref, vt_ref, km_ref, vtm_ref, o_ref, s0_scr, s1_scr, m_scr, acc_scr, *, nk, tk):
    tq = qt_ref.shape[2]
    key_row = lax.broadcasted_iota(jnp.int32, (LANES, tq), 0)
    for hh in range(2):
        s0 = jnp.where(key_row < N_META, _dot(km_ref[hh], qt_ref[hh]), -jnp.inf)
        m0 = jnp.max(s0, axis=0, keepdims=True)
        m_scr[hh] = m0
        acc_scr[hh] = _dot(vtm_ref[hh], jnp.exp2(s0 - m0).astype(_BF))

    def scores(buf, c):
        off = pl.multiple_of(c * tk, tk)
        for hh in range(2):
            buf[hh] = _dot(k_ref[hh, pl.ds(off, tk), :], qt_ref[hh])

    def accumulate(buf, c):
        for hh in range(2):
            s = buf[hh]
            m = m_scr[hh]
            m_new = jnp.maximum(m, jnp.max(s, axis=0, keepdims=True))
            m_scr[hh] = m_new
            p = jnp.exp2(s - m_new).astype(_BF)
            acc_scr[hh] = jnp.exp2(m - m_new) * acc_scr[hh] + _dot(vt_ref[hh, c], p)

    scores(s0_scr, 0)

    def body(t, _):
        scores(s1_scr, 2 * t + 1)
        accumulate(s0_scr, 2 * t)
        scores(s0_scr, 2 * t + 2)
        accumulate(s1_scr, 2 * t + 1)
        return 0

    lax.fori_loop(0, nk // 2 - 1, body, 0)
    scores(s1_scr, nk - 1)
    accumulate(s0_scr, nk - 2)
    accumulate(s1_scr, nk - 1)
    _attn_finish([acc_scr[0], acc_scr[1]], o_ref)


def _attn_call(qt, mrow, k, vt, km, vtm, *, bounded):
    bsz, _, nq, _, tq = qt.shape
    nk, tk = vt.shape[2], vt.shape[4]
    seq = k.shape[2]
    assert nk % 2 == 0 and nk >= 4
    q_spec = pl.BlockSpec((None, 2, None, HEAD_PAD, tq), lambda b, hp, i: (b, hp, i, 0, 0))
    kv_specs = [pl.BlockSpec((None, 2, seq, HEAD_PAD), lambda b, hp, i: (b, hp, 0, 0)),
                pl.BlockSpec((None, 2, nk, HEAD_PAD, tk), lambda b, hp, i: (b, hp, 0, 0, 0)),
                pl.BlockSpec((2, LANES, HEAD_PAD), lambda b, hp, i: (hp, 0, 0)),
                pl.BlockSpec((2, HEAD_PAD, LANES), lambda b, hp, i: (hp, 0, 0))]
    if bounded:
        body = functools.partial(_attn_bounded_kernel, nk=nk, tk=tk)
        in_specs = [q_spec, pl.BlockSpec((None, 2, None, 1, tq), lambda b, hp, i: (b, hp, i, 0, 0))] + kv_specs
        args, scratch = (qt, mrow, k, vt, km, vtm), []
    else:
        body = functools.partial(_attn_online_kernel, nk=nk, tk=tk)
        in_specs = [q_spec] + kv_specs
        args = (qt, k, vt, km, vtm)
        scratch = [pltpu.VMEM((2, tk, tq), _F32), pltpu.VMEM((2, tk, tq), _F32),
                   pltpu.VMEM((2, 1, tq), _F32), pltpu.VMEM((2, HEAD_PAD, tq), _F32)]
    return pl.pallas_call(
        body,
        grid=(bsz, N_HEADS // 2, nq),
        in_specs=in_specs,
        out_specs=pl.BlockSpec((None, tq, LANES), lambda b, hp, i: (b, i, hp)),
        out_shape=jax.ShapeDtypeStruct((bsz, nq * tq, ATTN_WIDTH), _F32),
        scratch_shapes=scratch,
        compiler_params=pltpu.CompilerParams(
            dimension_semantics=("parallel", "parallel", "arbitrary"), vmem_limit_bytes=VMEM_LIMIT),
        name="attn_bounded" if bounded else "attn_online",
    )(*args)


def _cmul_add(ar, ai, xr, xi, sr, si):
    return ar * xr - ai * xi + sr, ar * xi + ai * xr + si


def _s5_kernel(a_ref, um_ref, pin_ref, pout_ref, lag_ref, w_ref, v_ref, t_ref, y_ref, sup_scr, ent_scr,
               *, nslab, bsz):
    rows = nslab * bsz
    half = STATE_W // 2
    a = jnp.concatenate([a_ref[pl.ds(i, rows, stride=SSM_GROUP), :] for i in range(SSM_GROUP)], axis=1)
    ap = _dot(a.astype(_BF), pin_ref[...]).astype(_BF)
    uc = [ap[:, c * CHUNK_W:(c + 1) * CHUNK_W] for c in range(SLAB_CHUNKS)]
    w = w_ref[...]
    s = [_dot(u, w) for u in uc]
    sr = [x[:, :half] for x in s]
    si = [x[:, half:] for x in s]
    t = t_ref[...]
    trow = lambda r: t[r:r + 1, :]
    sup_r = sup_i = None
    for c in range(SLAB_CHUNKS):
        cr, ci = trow(c), trow(SLAB_CHUNKS + c)
        pr = cr * sr[c] - ci * si[c]
        pi = cr * si[c] + ci * sr[c]
        sup_r = pr if sup_r is None else sup_r + pr
        sup_i = pi if sup_i is None else sup_i + pi
    sup_scr[:, :half] = sup_r
    sup_scr[:, half:] = sup_i

    lane = lax.broadcasted_iota(jnp.int32, (bsz, half), 1)
    fwd = lane < SSM_STATE
    sm = _dot(um_ref[...], w)
    xr = jnp.where(fwd, sm[:bsz, :half], 0.0)
    xi = jnp.where(fwd, sm[:bsz, half:], 0.0)
    a_slab_r, a_slab_i = trow(2 * SLAB_CHUNKS), trow(2 * SLAB_CHUNKS + 1)
    for j in range(nslab):
        rf = j * bsz
        rb = (nslab - 1 - j) * bsz
        ent_scr[rf:rf + bsz, 0:SSM_STATE] = xr[:, 0:SSM_STATE]
        ent_scr[rb:rb + bsz, SSM_STATE:half] = xr[:, SSM_STATE:half]
        ent_scr[rf:rf + bsz, half:half + SSM_STATE] = xi[:, 0:SSM_STATE]
        ent_scr[rb:rb + bsz, half + SSM_STATE:STATE_W] = xi[:, SSM_STATE:half]
        s_r = jnp.where(fwd, sup_scr[rf:rf + bsz, :half], sup_scr[rb:rb + bsz, :half])
        s_i = jnp.where(fwd, sup_scr[rf:rf + bsz, half:], sup_scr[rb:rb + bsz, half:])
        xr, xi = _cmul_add(a_slab_r, a_slab_i, xr, xi, s_r, s_i)

    ent = ent_scr[...]
    a_r, a_i = trow(2 * SLAB_CHUNKS + 2), trow(2 * SLAB_CHUNKS + 3)
    xf = [(ent[:, :half], ent[:, half:])]
    for c in range(1, SLAB_CHUNKS):
        xf.append(_cmul_add(a_r, a_i, xf[-1][0], xf[-1][1], sr[c - 1], si[c - 1]))
    xb = [(ent[:, :half], ent[:, half:])]
    for c in range(SLAB_CHUNKS - 2, -1, -1):
        xb.insert(0, _cmul_add(a_r, a_i, xb[0][0], xb[0][1], sr[c + 1], si[c + 1]))
    fwd_rows = lax.broadcasted_iota(jnp.int32, (rows, half), 1) < SSM_STATE
    m = jnp.concatenate(
        [pltpu.roll(jnp.broadcast_to(lag_ref[i:i + 1, :], (SSM_CHUNK, 2 * CHUNK_W)), 0, 1,
                    stride=SSM_GROUP, stride_axis=0)[:, :CHUNK_W] for i in range(SSM_GROUP)], axis=0).astype(_BF)
    v = v_ref[...]
    ys = []
    for c in range(SLAB_CHUNKS):
        xin = jnp.concatenate([jnp.where(fwd_rows, xf[c][0], xb[c][0]),
                               jnp.where(fwd_rows, xf[c][1], xb[c][1])], axis=1).astype(_BF)
        ys.append((_dot(uc[c], m) + _dot(xin, v)).astype(_BF))
    yp = _dot(jnp.concatenate(ys, axis=1), pout_ref[...])
    for o in range(SSM_GROUP):
        y_ref[pl.ds(o, rows, stride=SSM_GROUP), :] = yp[:, o * SLAB_T:(o + 1) * SLAB_T]


def _s5_call(a, um, pin, pout, lags, w_mat, v_mat, tab, *, nslab, bsz):
    n = nslab * bsz * SSM_GROUP
    rows = nslab * bsz
    g_spec = lambda *shape: pl.BlockSpec((None,) + shape, lambda g: (g,) + (0,) * len(shape))
    perm_spec = pl.BlockSpec((SLAB_W, SLAB_W), lambda g: (0, 0), pipeline_mode=pl.Buffered(1))
    return pl.pallas_call(
        functools.partial(_s5_kernel, nslab=nslab, bsz=bsz),
        grid=(SSM_GROUPS,),
        in_specs=[g_spec(n, SLAB_T), g_spec(SUBLANES, CHUNK_W), perm_spec, perm_spec,
                  g_spec(SSM_GROUP, 2 * CHUNK_W), g_spec(CHUNK_W, STATE_W), g_spec(STATE_W, CHUNK_W),
                  g_spec(2 * SUBLANES, STATE_W // 2)],
        out_specs=g_spec(n, SLAB_T),
        out_shape=jax.ShapeDtypeStruct((SSM_GROUPS, n, SLAB_T), _F32),
        scratch_shapes=[pltpu.VMEM((rows, STATE_W), _F32), pltpu.VMEM((rows, STATE_W), _F32)],
        compiler_params=pltpu.CompilerParams(
            dimension_semantics=("parallel",), vmem_limit_bytes=VMEM_LIMIT),
        name="s5",
    )(a, um, pin, pout, lags, w_mat, v_mat, tab)


def _s5_matrices(a_re, a_im, log_dt, b_re, b_im, c_re, c_im, d_skip):
    tc = SSM_CHUNK
    lam = lax.complex(jnp.minimum(a_re.astype(_F32), -1e-4), a_im.astype(_F32))
    dt = jnp.exp(log_dt.astype(_F32))[..., None]
    lam_dt = lam * dt
    lam_bar = jnp.exp(lam_dt)
    b_bar = ((lam_bar - 1.0) / lam)[..., None] * lax.complex(b_re.astype(_F32), b_im.astype(_F32))
    c_c = lax.complex(c_re.astype(_F32), c_im.astype(_F32))
    k_idx = jnp.arange(tc + 1, dtype=_F32)
    pw = jnp.exp(lam_dt[:, :, None, :] * k_idx[None, None, :, None])
    kern = jnp.real(jnp.einsum('dgop,dgkp,dgpi->dgkoi', c_c, pw[:, :, :tc], b_bar))
    d_g = d_skip.astype(_F32).reshape(SSM_GROUPS, SSM_GROUP)
    center = kern[0][:, :1] + kern[1][:, :1] + (jnp.eye(SSM_GROUP, dtype=_F32)[None] * d_g[:, :, None])[:, None]
    lags = jnp.concatenate([center, kern[0][:, 1:], jnp.zeros_like(center), kern[1][:, :0:-1]], axis=1)
    lags = lags.transpose(0, 3, 1, 2).reshape(SSM_GROUPS, SSM_GROUP, 2 * CHUNK_W)
    wf = b_bar[0].transpose(0, 2, 1)[:, :, None, :] * pw[0][:, tc - 1::-1][:, None, :, :]
    wb = b_bar[1].transpose(0, 2, 1)[:, :, None, :] * pw[1][:, :tc][:, None, :, :]
    w_mat = jnp.concatenate([jnp.real(wf), jnp.real(wb), jnp.imag(wf), jnp.imag(wb)], axis=-1)
    w_mat = w_mat.reshape(SSM_GROUPS, CHUNK_W, STATE_W)
    gf = pw[0][:, 1:tc + 1][:, :, None, :] * c_c[0][:, None, :, :]
    gb = pw[1][:, tc:0:-1][:, :, None, :] * c_c[1][:, None, :, :]
    v_mat = jnp.concatenate([jnp.real(gf), jnp.real(gb), -jnp.imag(gf), -jnp.imag(gb)], axis=-1)
    v_mat = v_mat.reshape(SSM_GROUPS, CHUNK_W, STATE_W).transpose(0, 2, 1)
    n_idx = jnp.arange(SLAB_CHUNKS + 1, dtype=_F32) * tc
    pc = jnp.exp(lam_dt[:, :, None, :] * n_idx[None, None, :, None])
    coef = jnp.concatenate([pc[0][:, SLAB_CHUNKS - 1::-1], pc[1][:, :SLAB_CHUNKS]], axis=-1)
    both = lambda n: jnp.concatenate([pc[0][:, n], pc[1][:, n]], axis=-1)[:, None, :]
    a_slab, a_chunk = both(SLAB_CHUNKS), both(1)
    tab = jnp.concatenate([jnp.real(coef), jnp.imag(coef), jnp.real(a_slab), jnp.imag(a_slab),
                           jnp.real(a_chunk), jnp.imag(a_chunk)], axis=1)
    tab = jnp.pad(tab, ((0, 0), (0, 2 * SUBLANES - tab.shape[1]), (0, 0)))
    return lags, w_mat, v_mat, tab


def _slab_permutations():
    r = lax.broadcasted_iota(jnp.int32, (SLAB_W, SLAB_W), 0)
    c = lax.broadcasted_iota(jnp.int32, (SLAB_W, SLAB_W), 1)
    ch, tok = r // SLAB_T, r % SLAB_T
    pin = c == (tok // SSM_CHUNK) * CHUNK_W + ch * SSM_CHUNK + tok % SSM_CHUNK
    ch, tok = c // SLAB_T, c % SLAB_T
    pout = r == (tok // SSM_CHUNK) * CHUNK_W + (tok % SSM_CHUNK) * SSM_GROUP + ch
    return pin.astype(_BF), pout.astype(_BF)


def _post_kernel(x_ref, attn_ref, y_ref, wglu_ref, gmix_ref, wout_ref, gpm_ref, gpre_ref,
                 wup_ref, wdn_ref, gpost_ref, o_ref):
    nslab = y_ref.shape[1]
    yt = jnp.concatenate([jnp.concatenate([y_ref[g, c] for c in range(nslab)], axis=1)
                          for g in range(SSM_GROUPS)], axis=0)
    y = yt.T
    gy = 0.5 * y * (1.0 + jnp.tanh(math.sqrt(2.0 / math.pi) * (y + 0.044715 * (y * y * y))))
    z = _dot(gy.astype(_BF), wglu_ref[...])
    ssm = z[:, :SSM_WIDTH] * (1.0 / (1.0 + jnp.exp(-z[:, SSM_WIDTH:])))
    gmix = gmix_ref[...]
    mix = jnp.concatenate([_rms(attn_ref[...], gmix[:, :ATTN_WIDTH]),
                           _rms(ssm, gmix[:, ATTN_WIDTH:])], axis=-1).astype(_BF)
    h1 = x_ref[...] + _rms(_dot(mix, wout_ref[...]), gpm_ref[...])
    hn = _rms(h1, gpre_ref[...]).astype(_BF)
    acc = None
    for c in range(D_FF // FF_TILE):
        up = _dot(hn, wup_ref[:, c * FF_TILE:(c + 1) * FF_TILE])
        up = jnp.maximum(up, 0.0)
        part = _dot((up * up).astype(_BF), wdn_ref[c * FF_TILE:(c + 1) * FF_TILE, :])
        acc = part if acc is None else acc + part
    o_ref[...] = h1 + _rms(acc, gpost_ref[...])


def _post_call(x, attn, y, wglu, gmix, wout, gpm, gpre, wup, wdn, gpost, *, tile):
    bsz, seq, _ = x.shape
    row_spec = lambda w: pl.BlockSpec((None, tile, w), lambda b, i: (b, i, 0))
    wspec = lambda shape: pl.BlockSpec(shape, lambda b, i: (0, 0), pipeline_mode=pl.Buffered(1))
    y_spec = pl.BlockSpec((SSM_GROUPS, tile // SLAB_T, SSM_GROUP, SLAB_T), lambda b, i: (0, i, b, 0))
    return pl.pallas_call(
        _post_kernel,
        grid=(bsz, seq // tile),
        in_specs=[row_spec(D_MODEL), row_spec(ATTN_WIDTH), y_spec,
                  wspec((SSM_WIDTH, 2 * SSM_WIDTH)), wspec((1, D_MODEL)), wspec((D_MODEL, D_MODEL)),
                  wspec((1, D_MODEL)), wspec((1, D_MODEL)), wspec((D_MODEL, D_FF)),
                  wspec((D_FF, D_MODEL)), wspec((1, D_MODEL))],
        out_specs=row_spec(D_MODEL),
        out_shape=jax.ShapeDtypeStruct((bsz, seq, D_MODEL), _F32),
        compiler_params=pltpu.CompilerParams(
            dimension_semantics=("parallel", "parallel"), vmem_limit_bytes=VMEM_LIMIT),
        name="post",
    )(x, attn, y, wglu, gmix, wout, gpm, gpre, wup, wdn, gpost)


def _rope_tables(pos, tile):
    half = QK_ROPE_DIM // 2
    inv = 1.0 / (ROPE_BASE ** (jnp.arange(0, QK_ROPE_DIM, 2, dtype=_F32) / QK_ROPE_DIM))
    ang = pos.astype(_F32)[:, None, :] * inv[None, :, None]
    bsz, seq = pos.shape
    rope = jnp.stack([jnp.cos(ang), jnp.sin(ang)], axis=1)
    return rope.reshape(bsz, 2, half, seq // tile, tile).transpose(0, 3, 1, 2, 4)


def _prep_weights(w_in, w_uq, w_ukv):
    scale = QK_HEAD_DIM ** -0.5 * math.log2(math.e)
    win = w_in[:, :OFF_KR].astype(_BF)
    wkr_t = w_in[:, OFF_KR:OFF_U].T.astype(_BF)
    wu_t = w_in[:, OFF_U:].T.astype(_BF)
    wq_t = (w_uq * scale).T.astype(_BF)
    wkv3 = w_ukv.reshape(KV_LORA_RANK, N_HEADS, QK_NOPE_DIM + V_HEAD_DIM)
    wk = jnp.concatenate([wkv3[..., :QK_NOPE_DIM],
                          jnp.zeros((KV_LORA_RANK, N_HEADS, HEAD_PAD - QK_NOPE_DIM), _F32)], axis=-1)
    wk = wk.reshape(KV_LORA_RANK, N_HEADS * HEAD_PAD).astype(_BF)
    wv_t = wkv3[..., QK_NOPE_DIM:].reshape(KV_LORA_RANK, N_HEADS * V_HEAD_DIM).T.astype(_BF)
    return win, wkr_t, wu_t, wq_t, wk, wv_t


def kernel(x, positions, meta_tokens, g_pre_mix, w_in, g_q_lat, w_uq, g_kv_lat, w_ukv,
           ssm_A_re, ssm_A_im, ssm_log_dt, ssm_B_re, ssm_B_im, ssm_C_re, ssm_C_im, ssm_D,
           w_glu, g_mix_out, w_out, g_post_mix, g_pre_mlp, w_mlp_up, w_mlp_down, g_post_mlp):
    bsz, seq, _ = x.shape
    assert seq % ROW_TILE == 0 and ROW_TILE % SLAB_T == 0 and bsz <= SUBLANES
    assert N_META <= SSM_CHUNK
    row = lambda g: g.reshape(1, -1).astype(_F32)

    win, wkr_t, wu_t, wq_t, wk, wv_t = _prep_weights(w_in[0], w_uq[0], w_ukv[0])
    weights = (row(g_pre_mix[0]), win, wkr_t, wu_t, row(g_q_lat[0]), wq_t, row(g_kv_lat[0]), wk, wv_t)
    rope = _rope_tables(positions.astype(jnp.int32) + N_META, ROW_TILE)
    qt, k, vt, u, q_norm, k_sq = _proj_call(x, rope, *weights, tile=ROW_TILE, meta=False)
    meta_x = jnp.pad(meta_tokens.astype(x.dtype), ((0, LANES - N_META), (0, 0)))[None]
    rope_m = _rope_tables(jnp.arange(LANES, dtype=jnp.int32)[None], LANES)
    k_m, vt_m, u_m = _proj_call(meta_x, rope_m, *weights, tile=LANES, meta=True)
    km, vtm, u_m = k_m[0], vt_m[0, :, 0], u_m[:, :N_META]
    k_m_sq = jnp.max(jnp.sum(jnp.square(k_m[0].astype(_F32)), axis=-1), axis=-1)
    k_max = jnp.sqrt(jnp.maximum(jnp.max(k_sq, axis=(2, 3, 4)), k_m_sq[None]))
    mrow = q_norm * (k_max * BOUND_SLACK)[:, :, None, None, None]
    attn = lax.cond(jnp.max(mrow) <= BOUND_LIMIT,
                    lambda: _attn_call(qt, mrow, k, vt, km, vtm, bounded=True),
                    lambda: _attn_call(qt, mrow, k, vt, km, vtm, bounded=False))

    um = u_m[0].astype(_BF).reshape(N_META, SSM_GROUPS, SSM_GROUP).transpose(1, 2, 0)
    um = jnp.pad(um, ((0, 0), (0, 0), (SSM_CHUNK - N_META, 0))).reshape(SSM_GROUPS, 1, CHUNK_W)
    um = jnp.broadcast_to(um, (SSM_GROUPS, SUBLANES, CHUNK_W))
    lags, w_mat, v_mat, tab = _s5_matrices(ssm_A_re[0], ssm_A_im[0], ssm_log_dt[0], ssm_B_re[0],
                                            ssm_B_im[0], ssm_C_re[0], ssm_C_im[0], ssm_D[0])
    pin, pout = _slab_permutations()
    nslab = seq // SLAB_T
    yg = _s5_call(u.reshape(SSM_GROUPS, nslab * bsz * SSM_GROUP, SLAB_T), um, pin, pout,
                  lags, w_mat.astype(_BF), v_mat.astype(_BF), tab, nslab=nslab, bsz=bsz)
    y = yg.reshape(SSM_GROUPS, nslab, bsz * SSM_GROUP, SLAB_T)

    return _post_call(x, attn, y, w_glu[0].astype(_BF), row(g_mix_out[0]), w_out[0].astype(_BF),
                      row(g_post_mix[0]), row(g_pre_mlp[0]), w_mlp_up[0].astype(_BF),
                      w_mlp_down[0].astype(_BF), row(g_post_mlp[0]), tile=ROW_TILE)
```

```python
import functools
import math

import jax
import jax.numpy as jnp
from jax import lax
from jax.experimental import pallas as pl
from jax.experimental.pallas import tpu as pltpu

D_MODEL = 1024
N_META = 16
ATTN_WIDTH = 512
SSM_WIDTH = 512
N_HEADS = 8
V_HEAD_DIM = 64
QK_NOPE_DIM = 64
QK_ROPE_DIM = 32
QK_HEAD_DIM = QK_NOPE_DIM + QK_ROPE_DIM
Q_LORA_RANK = 384
KV_LORA_RANK = 256
ROPE_BASE = 10000.0
SSM_GROUP = 16
SSM_GROUPS = 32
SSM_STATE = 64
D_FF = 4 * D_MODEL
EPS = 1e-6
OFF_KV = Q_LORA_RANK
OFF_KR = OFF_KV + KV_LORA_RANK
OFF_U = OFF_KR + QK_ROPE_DIM

LANES = 128
SUBLANES = 8
HEAD_PAD = LANES
ONES_LANE = V_HEAD_DIM
ATTN_TQ = 1024
ATTN_TK = 256
V_ROWS = 80
PW_Q = 0
PW_KV = PW_Q + Q_LORA_RANK
PW_END = PW_KV + KV_LORA_RANK

SSM_CHUNK = 32
CHUNK_W = SSM_CHUNK * SSM_GROUP
STATE_W = 4 * SSM_STATE
SLAB_T = LANES
SLAB_CHUNKS = SLAB_T // SSM_CHUNK
SLAB_W = SSM_GROUP * SLAB_T

ROW_TILE = 512
FF_TILE = 1024
BOUND_SLACK = 1.0 + 2.0 ** -6
BOUND_LIMIT = 60.0
VMEM_LIMIT = 56 * 1024 * 1024

_BF = jnp.bfloat16
_F32 = jnp.float32


def _dot(a, b):
    return jnp.dot(a, b, preferred_element_type=_F32)


def _rms(x, g):
    return x * lax.rsqrt(jnp.mean(x * x, axis=-1, keepdims=True) + EPS) * g


_NT = (((1,), (1,)), ((), ()))


def _rotate(x1, x2, cos_t, sin_t):
    return x1 * cos_t - x2 * sin_t, x1 * sin_t + x2 * cos_t


def _proj_kernel(x_ref, rope_ref, *refs, meta):
    if meta:
        gpre_ref, win_ref, wkr_ref, wu_ref, gkv_ref, wk_ref, wv_ref, k_ref, v_ref, u_ref = refs
    else:
        (gpre_ref, win_ref, wkr_ref, wu_ref, gq_ref, wq_ref, gkv_ref, wk_ref, wv_ref,
         q_ref, k_ref, v_ref, u_ref, qn_ref, kmx_ref) = refs
    tile = x_ref.shape[0]
    half = QK_ROPE_DIM // 2
    cos_t, sin_t = rope_ref[0], rope_ref[1]
    xn = _rms(x_ref[...], gpre_ref[...]).astype(_BF)
    proj = _dot(xn, win_ref[...])
    kvn = _rms(proj[:, PW_KV:PW_END], gkv_ref[...]).astype(_BF)
    krt = lax.dot_general(wkr_ref[...], xn, _NT, preferred_element_type=_F32)
    r1, r2 = _rotate(krt[:half], krt[half:], cos_t, sin_t)
    kr = jnp.concatenate([jnp.zeros((QK_NOPE_DIM, tile), _F32), r1, r2,
                          jnp.zeros((HEAD_PAD - QK_HEAD_DIM, tile), _F32)], axis=0).T
    kk = _dot(kvn, wk_ref[...])
    vt = lax.dot_general(wv_ref[...], kvn, _NT, preferred_element_type=_F32)
    ones_tail = (lax.broadcasted_iota(jnp.int32, (HEAD_PAD - V_HEAD_DIM, tile), 0) == 0).astype(_F32)
    if meta:
        u_ref[...] = lax.dot_general(xn, wu_ref[...], _NT, preferred_element_type=_F32)
    else:
        ut = lax.dot_general(wu_ref[...], xn, _NT, preferred_element_type=_F32)
        for g in range(SSM_GROUPS):
            for c in range(tile // SLAB_T):
                u_ref[g, c] = ut[g * SSM_GROUP:(g + 1) * SSM_GROUP, c * SLAB_T:(c + 1) * SLAB_T]
        qn = _rms(proj[:, PW_Q:PW_KV], gq_ref[...]).astype(_BF)
        qt = lax.dot_general(wq_ref[...], qn, _NT, preferred_element_type=_F32)
        zero_rows = jnp.zeros((HEAD_PAD - QK_HEAD_DIM, tile), _F32)
    for h in range(N_HEADS):
        k_h = (kk[:, h * HEAD_PAD:(h + 1) * HEAD_PAD] + kr).astype(_BF)
        k_ref[h] = k_h
        v_ref[h] = jnp.concatenate([vt[h * V_HEAD_DIM:(h + 1) * V_HEAD_DIM], ones_tail], axis=0).astype(_BF)
        if not meta:
            blk = qt[h * QK_HEAD_DIM:(h + 1) * QK_HEAD_DIM]
            r1, r2 = _rotate(blk[QK_NOPE_DIM:QK_NOPE_DIM + half], blk[QK_NOPE_DIM + half:], cos_t, sin_t)
            qt_h = jnp.concatenate([blk[:QK_NOPE_DIM], r1, r2, zero_rows], axis=0).astype(_BF)
            q_ref[h] = qt_h
            qt_f = qt_h.astype(_F32)
            qn_ref[h] = jnp.sqrt(jnp.sum(qt_f * qt_f, axis=0, keepdims=True))
            k_f = k_h.astype(_F32)
            kmx_ref[h] = jnp.broadcast_to(jnp.max(jnp.sum(k_f * k_f, axis=1, keepdims=True), axis=0, keepdims=True),
                                          (1, LANES))


def _const_spec(shape):
    nd = len(shape)
    return pl.BlockSpec(shape, lambda *_: (0,) * nd)


def _proj_call(x, rope, gpre, win, wkr_t, wu_t, gq, wq_t, gkv, wk, wv_t, *, tile, meta):
    bsz, seq, _ = x.shape
    nt = seq // tile
    row_spec = lambda w: pl.BlockSpec((None, tile, w), lambda b, i: (b, i, 0))
    rope_spec = pl.BlockSpec((None, None, 2, QK_ROPE_DIM // 2, tile), lambda b, i: (b, i, 0, 0, 0))
    k_spec = pl.BlockSpec((None, N_HEADS, tile, HEAD_PAD), lambda b, i: (b, 0, i, 0))
    t_spec = pl.BlockSpec((None, N_HEADS, None, HEAD_PAD, tile), lambda b, i: (b, 0, i, 0, 0))
    k_shape = jax.ShapeDtypeStruct((bsz, N_HEADS, seq, HEAD_PAD), _BF)
    t_shape = jax.ShapeDtypeStruct((bsz, N_HEADS, nt, HEAD_PAD, tile), _BF)
    w_specs = lambda *ws: [_const_spec(w.shape) for w in ws]
    if meta:
        args = (x, rope, gpre, win, wkr_t, wu_t, gkv, wk, wv_t)
        in_specs = [row_spec(D_MODEL), rope_spec] + w_specs(*args[2:])
        out_specs = [k_spec, t_spec, row_spec(SSM_WIDTH)]
        out_shape = [k_shape, t_shape, jax.ShapeDtypeStruct((bsz, seq, SSM_WIDTH), _F32)]
    else:
        args = (x, rope, gpre, win, wkr_t, wu_t, gq, wq_t, gkv, wk, wv_t)
        in_specs = [row_spec(D_MODEL), rope_spec] + w_specs(*args[2:])
        norm_spec = lambda w: pl.BlockSpec((None, N_HEADS, None, 1, w), lambda b, i: (b, 0, i, 0, 0))
        u_spec = pl.BlockSpec((SSM_GROUPS, tile // SLAB_T, SSM_GROUP, SLAB_T), lambda b, i: (0, i, b, 0))
        out_specs = [t_spec, k_spec, t_spec, u_spec, norm_spec(tile), norm_spec(LANES)]
        out_shape = [t_shape, k_shape, t_shape,
                     jax.ShapeDtypeStruct((SSM_GROUPS, seq // SLAB_T, bsz * SSM_GROUP, SLAB_T), _F32),
                     jax.ShapeDtypeStruct((bsz, N_HEADS, nt, 1, tile), _F32),
                     jax.ShapeDtypeStruct((bsz, N_HEADS, nt, 1, LANES), _F32)]
    return pl.pallas_call(
        functools.partial(_proj_kernel, meta=meta),
        grid=(bsz, nt),
        in_specs=in_specs,
        out_specs=out_specs,
        out_shape=out_shape,
        compiler_params=pltpu.CompilerParams(
            dimension_semantics=("parallel", "parallel"), vmem_limit_bytes=VMEM_LIMIT),
        name="proj_meta" if meta else "proj",
    )(*args)


def _attn_finish(accs, o_ref):
    halves = [(acc * (1.0 / acc[ONES_LANE:ONES_LANE + 1, :]))[:V_HEAD_DIM] for acc in accs]
    o_ref[...] = jnp.concatenate(halves, axis=0).T


def _lane_concat(ref, hh):
    return jnp.concatenate([ref[hh, j] for j in range(ref.shape[1])], axis=1)


def _attn_bounded_kernel(qt_ref, mrow_ref, k_ref, vt_ref, km_ref, vtm_ref, o_ref, *, nk, tk):
    tq = o_ref.shape[0]
    per_slab = vt_ref.shape[3] // tk
    key_row = lax.broadcasted_iota(jnp.int32, (LANES, tq), 0)
    accs = []
    for hh in range(2):
        qt = _lane_concat(qt_ref, hh)
        mrow = _lane_concat(mrow_ref, hh)
        s0 = jnp.where(key_row < N_META, _dot(km_ref[hh], qt), -jnp.inf)
        acc = _dot(vtm_ref[hh, :V_ROWS, :], jnp.exp2(s0 - mrow).astype(_BF))
        scores = lambda c: _dot(k_ref[hh, c * tk:(c + 1) * tk, :], qt)
        s_next = scores(0)
        for c in range(nk):
            s = s_next
            if c + 1 < nk:
                s_next = scores(c + 1)
            vt_c = vt_ref[hh, c // per_slab, :V_ROWS, (c % per_slab) * tk:(c % per_slab + 1) * tk]
            acc = acc + _dot(vt_c, jnp.exp2(s - mrow).astype(_BF))
        accs.append(acc)
    _attn_finish(accs, o_ref)


def _attn_online_kernel(qt_ref, k_ref, vt_ref, km_ref, vtm_ref, o_ref, s0_scr, s1_scr, m_scr, acc_scr, *, nk, tk):
    tq = o_ref.shape[0]
    key_row = lax.broadcasted_iota(jnp.int32, (LANES, tq), 0)
    for hh in range(2):
        s0 = jnp.where(key_row < N_META, _dot(km_ref[hh], _lane_concat(qt_ref, hh)), -jnp.inf)
        m0 = jnp.max(s0, axis=0, keepdims=True)
        m_scr[hh] = m0
        acc_scr[hh] = _dot(vtm_ref[hh, :V_ROWS, :], jnp.exp2(s0 - m0).astype(_BF))

    def scores(buf, c):
        off = pl.multiple_of(c * tk, tk)
        for hh in range(2):
            buf[hh] = _dot(k_ref[hh, pl.ds(off, tk), :], _lane_concat(qt_ref, hh))

    def accumulate(buf, c):
        for hh in range(2):
            s = buf[hh]
            m = m_scr[hh]
            m_new = jnp.maximum(m, jnp.max(s, axis=0, keepdims=True))
            m_scr[hh] = m_new
            p = jnp.exp2(s - m_new).astype(_BF)
            acc_scr[hh] = jnp.exp2(m - m_new) * acc_scr[hh] + _dot(vt_ref[hh, c, :V_ROWS, :], p)

    scores(s0_scr, 0)

    def body(t, _):
        scores(s1_scr, 2 * t + 1)
        accumulate(s0_scr, 2 * t)
        scores(s0_scr, 2 * t + 2)
        accumulate(s1_scr, 2 * t + 1)
        return 0

    lax.fori_loop(0, nk // 2 - 1, body, 0)
    scores(s1_scr, nk - 1)
    accumulate(s0_scr, nk - 2)
    accumulate(s1_scr, nk - 1)
    _attn_finish([acc_scr[0], acc_scr[1]], o_ref)


def _attn_call(qt, mrow, k, vt, km, vtm, *, bounded):
    bsz, _, nslab, _, slab = qt.shape
    seq = k.shape[2]
    tq = min(ATTN_TQ, seq)
    tk = min(ATTN_TK, slab)
    qs = tq // slab
    assert tq % slab == 0 and seq % tq == 0 and slab % tk == 0
    q_spec = lambda rows: pl.BlockSpec((None, 2, qs, rows, slab), lambda b, hp, i: (b, hp, i, 0, 0))
    kv_specs = [pl.BlockSpec((None, 2, seq, HEAD_PAD), lambda b, hp, i: (b, hp, 0, 0)),
                pl.BlockSpec((None, 2, nslab, HEAD_PAD, slab), lambda b, hp, i: (b, hp, 0, 0, 0)),
                pl.BlockSpec((2, LANES, HEAD_PAD), lambda b, hp, i: (hp, 0, 0)),
                pl.BlockSpec((2, HEAD_PAD, LANES), lambda b, hp, i: (hp, 0, 0))]
    if bounded:
        body = functools.partial(_attn_bounded_kernel, nk=seq // tk, tk=tk)
        in_specs = [q_spec(HEAD_PAD), q_spec(1)] + kv_specs
        args, scratch = (qt, mrow, k, vt, km, vtm), []
    else:
        assert nslab % 2 == 0 and nslab >= 4
        body = functools.partial(_attn_online_kernel, nk=nslab, tk=slab)
        in_specs = [q_spec(HEAD_PAD)] + kv_specs
        args = (qt, k, vt, km, vtm)
        scratch = [pltpu.VMEM((2, slab, tq), _F32), pltpu.VMEM((2, slab, tq), _F32),
                   pltpu.VMEM((2, 1, tq), _F32), pltpu.VMEM((2, V_ROWS, tq), _F32)]
    return pl.pallas_call(
        body,
        grid=(bsz, N_HEADS // 2, seq // tq),
        in_specs=in_specs,
        out_specs=pl.BlockSpec((None, tq, LANES), lambda b, hp, i: (b, i, hp)),
        out_shape=jax.ShapeDtypeStruct((bsz, seq, ATTN_WIDTH), _F32),
        scratch_shapes=scratch,
        compiler_params=pltpu.CompilerParams(
            dimension_semantics=("parallel", "parallel", "arbitrary"), vmem_limit_bytes=VMEM_LIMIT),
        name="attn_bounded" if bounded else "attn_online",
    )(*args)


def _cmul_add(ar, ai, xr, xi, sr, si):
    return ar * xr - ai * xi + sr, ar * xi + ai * xr + si


def _s5_kernel(a_ref, um_ref, pin_ref, pout_ref, lag_ref, w_ref, v_ref, t_ref, y_ref, sup_scr, ent_scr,
               *, nslab, bsz):
    rows = nslab * bsz
    half = STATE_W // 2
    a = jnp.concatenate([a_ref[pl.ds(i, rows, stride=SSM_GROUP), :] for i in range(SSM_GROUP)], axis=1)
    ap = _dot(a.astype(_BF), pin_ref[...]).astype(_BF)
    uc = [ap[:, c * CHUNK_W:(c + 1) * CHUNK_W] for c in range(SLAB_CHUNKS)]
    w = w_ref[...]
    s = [_dot(u, w) for u in uc]
    sr = [x[:, :half] for x in s]
    si = [x[:, half:] for x in s]
    t = t_ref[...]
    trow = lambda r: t[r:r + 1, :]
    sup_r = sup_i = None
    for c in range(SLAB_CHUNKS):
        cr, ci = trow(c), trow(SLAB_CHUNKS + c)
        pr = cr * sr[c] - ci * si[c]
        pi = cr * si[c] + ci * sr[c]
        sup_r = pr if sup_r is None else sup_r + pr
        sup_i = pi if sup_i is None else sup_i + pi
    sup_scr[:, :half] = sup_r
    sup_scr[:, half:] = sup_i

    lane = lax.broadcasted_iota(jnp.int32, (bsz, half), 1)
    fwd = lane < SSM_STATE
    sm = _dot(um_ref[...], w)
    xr = jnp.where(fwd, sm[:bsz, :half], 0.0)
    xi = jnp.where(fwd, sm[:bsz, half:], 0.0)
    a_slab_r, a_slab_i = trow(2 * SLAB_CHUNKS), trow(2 * SLAB_CHUNKS + 1)
    for j in range(nslab):
        rf = j * bsz
        rb = (nslab - 1 - j) * bsz
        ent_scr[rf:rf + bsz, 0:SSM_STATE] = xr[:, 0:SSM_STATE]
        ent_scr[rb:rb + bsz, SSM_STATE:half] = xr[:, SSM_STATE:half]
        ent_scr[rf:rf + bsz, half:half + SSM_STATE] = xi[:, 0:SSM_STATE]
        ent_scr[rb:rb + bsz, half + SSM_STATE:STATE_W] = xi[:, SSM_STATE:half]
        s_r = jnp.where(fwd, sup_scr[rf:rf + bsz, :half], sup_scr[rb:rb + bsz, :half])
        s_i = jnp.where(fwd, sup_scr[rf:rf + bsz, half:], sup_scr[rb:rb + bsz, half:])
        xr, xi = _cmul_add(a_slab_r, a_slab_i, xr, xi, s_r, s_i)

    ent = ent_scr[...]
    a_r, a_i = trow(2 * SLAB_CHUNKS + 2), trow(2 * SLAB_CHUNKS + 3)
    xf = [(ent[:, :half], ent[:, half:])]
    for c in range(1, SLAB_CHUNKS):
        xf.append(_cmul_add(a_r, a_i, xf[-1][0], xf[-1][1], sr[c - 1], si[c - 1]))
    xb = [(ent[:, :half], ent[:, half:])]
    for c in range(SLAB_CHUNKS - 2, -1, -1):
        xb.insert(0, _cmul_add(a_r, a_i, xb[0][0], xb[0][1], sr[c + 1], si[c + 1]))
    fwd_rows = lax.broadcasted_iota(jnp.int32, (rows, half), 1) < SSM_STATE
    m = jnp.concatenate(
        [pltpu.roll(jnp.broadcast_to(lag_ref[i:i + 1, :], (SSM_CHUNK, 2 * CHUNK_W)), 0, 1,
                    stride=SSM_GROUP, stride_axis=0)[:, :CHUNK_W] for i in range(SSM_GROUP)], axis=0).astype(_BF)
    v = v_ref[...]
    ys = []
    for c in range(SLAB_CHUNKS):
        xin = jnp.concatenate([jnp.where(fwd_rows, xf[c][0], xb[c][0]),
                               jnp.where(fwd_rows, xf[c][1], xb[c][1])], axis=1).astype(_BF)
        ys.append((_dot(uc[c], m) + _dot(xin, v)).astype(_BF))
    yp = _dot(jnp.concatenate(ys, axis=1), pout_ref[...])
    for o in range(SSM_GROUP):
        y_ref[pl.ds(o, rows, stride=SSM_GROUP), :] = yp[:, o * SLAB_T:(o + 1) * SLAB_T]


def _s5_call(a, um, pin, pout, lags, w_mat, v_mat, tab, *, nslab, bsz):
    n = nslab * bsz * SSM_GROUP
    rows = nslab * bsz
    g_spec = lambda *shape: pl.BlockSpec((None,) + shape, lambda g: (g,) + (0,) * len(shape))
    perm_spec = pl.BlockSpec((SLAB_W, SLAB_W), lambda g: (0, 0), pipeline_mode=pl.Buffered(1))
    return pl.pallas_call(
        functools.partial(_s5_kernel, nslab=nslab, bsz=bsz),
        grid=(SSM_GROUPS,),
        in_specs=[g_spec(n, SLAB_T), g_spec(SUBLANES, CHUNK_W), perm_spec, perm_spec,
                  g_spec(SSM_GROUP, 2 * CHUNK_W), g_spec(CHUNK_W, STATE_W), g_spec(STATE_W, CHUNK_W),
                  g_spec(2 * SUBLANES, STATE_W // 2)],
        out_specs=g_spec(n, SLAB_T),
        out_shape=jax.ShapeDtypeStruct((SSM_GROUPS, n, SLAB_T), _F32),
        scratch_shapes=[pltpu.VMEM((rows, STATE_W), _F32), pltpu.VMEM((rows, STATE_W), _F32)],
        compiler_params=pltpu.CompilerParams(
            dimension_semantics=("parallel",), vmem_limit_bytes=VMEM_LIMIT),
        name="s5",
    )(a, um, pin, pout, lags, w_mat, v_mat, tab)


def _s5_matrices(a_re, a_im, log_dt, b_re, b_im, c_re, c_im, d_skip):
    tc = SSM_CHUNK
    lam = lax.complex(jnp.minimum(a_re.astype(_F32), -1e-4), a_im.astype(_F32))
    dt = jnp.exp(log_dt.astype(_F32))[..., None]
    lam_dt = lam * dt
    lam_bar = jnp.exp(lam_dt)
    b_bar = ((lam_bar - 1.0) / lam)[..., None] * lax.complex(b_re.astype(_F32), b_im.astype(_F32))
    c_c = lax.complex(c_re.astype(_F32), c_im.astype(_F32))
    k_idx = jnp.arange(tc + 1, dtype=_F32)
    pw = jnp.exp(lam_dt[:, :, None, :] * k_idx[None, None, :, None])
    kern = jnp.real(jnp.einsum('dgop,dgkp,dgpi->dgkoi', c_c, pw[:, :, :tc], b_bar))
    d_g = d_skip.astype(_F32).reshape(SSM_GROUPS, SSM_GROUP)
    center = kern[0][:, :1] + kern[1][:, :1] + (jnp.eye(SSM_GROUP, dtype=_F32)[None] * d_g[:, :, None])[:, None]
    lags = jnp.concatenate([center, kern[0][:, 1:], jnp.zeros_like(center), kern[1][:, :0:-1]], axis=1)
    lags = lags.transpose(0, 3, 1, 2).reshape(SSM_GROUPS, SSM_GROUP, 2 * CHUNK_W)
    wf = b_bar[0].transpose(0, 2, 1)[:, :, None, :] * pw[0][:, tc - 1::-1][:, None, :, :]
    wb = b_bar[1].transpose(0, 2, 1)[:, :, None, :] * pw[1][:, :tc][:, None, :, :]
    w_mat = jnp.concatenate([jnp.real(wf), jnp.real(wb), jnp.imag(wf), jnp.imag(wb)], axis=-1)
    w_mat = w_mat.reshape(SSM_GROUPS, CHUNK_W, STATE_W)
    gf = pw[0][:, 1:tc + 1][:, :, None, :] * c_c[0][:, None, :, :]
    gb = pw[1][:, tc:0:-1][:, :, None, :] * c_c[1][:, None, :, :]
    v_mat = jnp.concatenate([jnp.real(gf), jnp.real(gb), -jnp.imag(gf), -jnp.imag(gb)], axis=-1)
    v_mat = v_mat.reshape(SSM_GROUPS, CHUNK_W, STATE_W).transpose(0, 2, 1)
    n_idx = jnp.arange(SLAB_CHUNKS + 1, dtype=_F32) * tc
    pc = jnp.exp(lam_dt[:, :, None, :] * n_idx[None, None, :, None])
    coef = jnp.concatenate([pc[0][:, SLAB_CHUNKS - 1::-1], pc[1][:, :SLAB_CHUNKS]], axis=-1)
    both = lambda n: jnp.concatenate([pc[0][:, n], pc[1][:, n]], axis=-1)[:, None, :]
    a_slab, a_chunk = both(SLAB_CHUNKS), both(1)
    tab = jnp.concatenate([jnp.real(coef), jnp.imag(coef), jnp.real(a_slab), jnp.imag(a_slab),
                           jnp.real(a_chunk), jnp.imag(a_chunk)], axis=1)
    tab = jnp.pad(tab, ((0, 0), (0, 2 * SUBLANES - tab.shape[1]), (0, 0)))
    return lags, w_mat, v_mat, tab


def _slab_permutations():
    r = lax.broadcasted_iota(jnp.int32, (SLAB_W, SLAB_W), 0)
    c = lax.broadcasted_iota(jnp.int32, (SLAB_W, SLAB_W), 1)
    ch, tok = r // SLAB_T, r % SLAB_T
    pin = c == (tok // SSM_CHUNK) * CHUNK_W + ch * SSM_CHUNK + tok % SSM_CHUNK
    ch, tok = c // SLAB_T, c % SLAB_T
    pout = r == (tok // SSM_CHUNK) * CHUNK_W + (tok % SSM_CHUNK) * SSM_GROUP + ch
    return pin.astype(_BF), pout.astype(_BF)


def _post_kernel(x_ref, attn_ref, y_ref, wglu_ref, gmix_ref, wout_ref, gpm_ref, gpre_ref,
                 wup_ref, wdn_ref, gpost_ref, o_ref):
    nslab = y_ref.shape[1]
    yt = jnp.concatenate([jnp.concatenate([y_ref[g, c] for c in range(nslab)], axis=1)
                          for g in range(SSM_GROUPS)], axis=0)
    y = yt.T
    gy = 0.5 * y * (1.0 + jnp.tanh(math.sqrt(2.0 / math.pi) * (y + 0.044715 * (y * y * y))))
    z = _dot(gy.astype(_BF), wglu_ref[...])
    ssm = z[:, :SSM_WIDTH] * (1.0 / (1.0 + jnp.exp(-z[:, SSM_WIDTH:])))
    gmix = gmix_ref[...]
    mix = jnp.concatenate([_rms(attn_ref[...], gmix[:, :ATTN_WIDTH]),
                           _rms(ssm, gmix[:, ATTN_WIDTH:])], axis=-1).astype(_BF)
    h1 = x_ref[...] + _rms(_dot(mix, wout_ref[...]), gpm_ref[...])
    hn = _rms(h1, gpre_ref[...]).astype(_BF)
    acc = None
    for c in range(D_FF // FF_TILE):
        up = _dot(hn, wup_ref[:, c * FF_TILE:(c + 1) * FF_TILE])
        up = jnp.maximum(up, 0.0)
        part = _dot((up * up).astype(_BF), wdn_ref[c * FF_TILE:(c + 1) * FF_TILE, :])
        acc = part if acc is None else acc + part
    o_ref[...] = h1 + _rms(acc, gpost_ref[...])


def _post_call(x, attn, y, wglu, gmix, wout, gpm, gpre, wup, wdn, gpost, *, tile):
    bsz, seq, _ = x.shape
    row_spec = lambda w: pl.BlockSpec((None, tile, w), lambda b, i: (b, i, 0))
    wspec = lambda shape: pl.BlockSpec(shape, lambda b, i: (0, 0), pipeline_mode=pl.Buffered(1))
    y_spec = pl.BlockSpec((SSM_GROUPS, tile // SLAB_T, SSM_GROUP, SLAB_T), lambda b, i: (0, i, b, 0))
    return pl.pallas_call(
        _post_kernel,
        grid=(bsz, seq // tile),
        in_specs=[row_spec(D_MODEL), row_spec(ATTN_WIDTH), y_spec,
                  wspec((SSM_WIDTH, 2 * SSM_WIDTH)), wspec((1, D_MODEL)), wspec((D_MODEL, D_MODEL)),
                  wspec((1, D_MODEL)), wspec((1, D_MODEL)), wspec((D_MODEL, D_FF)),
                  wspec((D_FF, D_MODEL)), wspec((1, D_MODEL))],
        out_specs=row_spec(D_MODEL),
        out_shape=jax.ShapeDtypeStruct((bsz, seq, D_MODEL), _F32),
        compiler_params=pltpu.CompilerParams(
            dimension_semantics=("parallel", "parallel"), vmem_limit_bytes=VMEM_LIMIT),
        name="post",
    )(x, attn, y, wglu, gmix, wout, gpm, gpre, wup, wdn, gpost)


def _rope_tables(pos, tile):
    half = QK_ROPE_DIM // 2
    inv = 1.0 / (ROPE_BASE ** (jnp.arange(0, QK_ROPE_DIM, 2, dtype=_F32) / QK_ROPE_DIM))
    ang = pos.astype(_F32)[:, None, :] * inv[None, :, None]
    bsz, seq = pos.shape
    rope = jnp.stack([jnp.cos(ang), jnp.sin(ang)], axis=1)
    return rope.reshape(bsz, 2, half, seq // tile, tile).transpose(0, 3, 1, 2, 4)


def _prep_weights(w_in, w_uq, w_ukv):
    scale = QK_HEAD_DIM ** -0.5 * math.log2(math.e)
    win = w_in[:, :OFF_KR].astype(_BF)
    wkr_t = w_in[:, OFF_KR:OFF_U].T.astype(_BF)
    wu_t = w_in[:, OFF_U:].T.astype(_BF)
    wq_t = (w_uq * scale).T.astype(_BF)
    wkv3 = w_ukv.reshape(KV_LORA_RANK, N_HEADS, QK_NOPE_DIM + V_HEAD_DIM)
    wk = jnp.concatenate([wkv3[..., :QK_NOPE_DIM],
                          jnp.zeros((KV_LORA_RANK, N_HEADS, HEAD_PAD - QK_NOPE_DIM), _F32)], axis=-1)
    wk = wk.reshape(KV_LORA_RANK, N_HEADS * HEAD_PAD).astype(_BF)
    wv_t = wkv3[..., QK_NOPE_DIM:].reshape(KV_LORA_RANK, N_HEADS * V_HEAD_DIM).T.astype(_BF)
    return win, wkr_t, wu_t, wq_t, wk, wv_t


def kernel(x, positions, meta_tokens, g_pre_mix, w_in, g_q_lat, w_uq, g_kv_lat, w_ukv,
           ssm_A_re, ssm_A_im, ssm_log_dt, ssm_B_re, ssm_B_im, ssm_C_re, ssm_C_im, ssm_D,
           w_glu, g_mix_out, w_out, g_post_mix, g_pre_mlp, w_mlp_up, w_mlp_down, g_post_mlp):
    bsz, seq, _ = x.shape
    assert seq % ROW_TILE == 0 and ROW_TILE % SLAB_T == 0 and bsz <= SUBLANES
    assert N_META <= SSM_CHUNK
    row = lambda g: g.reshape(1, -1).astype(_F32)

    win, wkr_t, wu_t, wq_t, wk, wv_t = _prep_weights(w_in[0], w_uq[0], w_ukv[0])
    weights = (row(g_pre_mix[0]), win, wkr_t, wu_t, row(g_q_lat[0]), wq_t, row(g_kv_lat[0]), wk, wv_t)
    rope = _rope_tables(positions.astype(jnp.int32) + N_META, ROW_TILE)
    qt, k, vt, u, q_norm, k_sq = _proj_call(x, rope, *weights, tile=ROW_TILE, meta=False)
    meta_x = jnp.pad(meta_tokens.astype(x.dtype), ((0, LANES - N_META), (0, 0)))[None]
    rope_m = _rope_tables(jnp.arange(LANES, dtype=jnp.int32)[None], LANES)
    k_m, vt_m, u_m = _proj_call(meta_x, rope_m, *weights, tile=LANES, meta=True)
    km, vtm, u_m = k_m[0], vt_m[0, :, 0], u_m[:, :N_META]
    k_m_sq = jnp.max(jnp.sum(jnp.square(k_m[0].astype(_F32)), axis=-1), axis=-1)
    k_max = jnp.sqrt(jnp.maximum(jnp.max(k_sq, axis=(2, 3, 4)), k_m_sq[None]))
    mrow = q_norm * (k_max * BOUND_SLACK)[:, :, None, None, None]
    attn = lax.cond(jnp.max(mrow) <= BOUND_LIMIT,
                    lambda: _attn_call(qt, mrow, k, vt, km, vtm, bounded=True),
                    lambda: _attn_call(qt, mrow, k, vt, km, vtm, bounded=False))

    um = u_m[0].astype(_BF).reshape(N_META, SSM_GROUPS, SSM_GROUP).transpose(1, 2, 0)
    um = jnp.pad(um, ((0, 0), (0, 0), (SSM_CHUNK - N_META, 0))).reshape(SSM_GROUPS, 1, CHUNK_W)
    um = jnp.broadcast_to(um, (SSM_GROUPS, SUBLANES, CHUNK_W))
    lags, w_mat, v_mat, tab = _s5_matrices(ssm_A_re[0], ssm_A_im[0], ssm_log_dt[0], ssm_B_re[0],
                                            ssm_B_im[0], ssm_C_re[0], ssm_C_im[0], ssm_D[0])
    pin, pout = _slab_permutations()
    nslab = seq // SLAB_T
    yg = _s5_call(u.reshape(SSM_GROUPS, nslab * bsz * SSM_GROUP, SLAB_T), um, pin, pout,
                  lags, w_mat.astype(_BF), v_mat.astype(_BF), tab, nslab=nslab, bsz=bsz)
    y = yg.reshape(SSM_GROUPS, nslab, bsz * SSM_GROUP, SLAB_T)

    return _post_call(x, attn, y, w_glu[0].astype(_BF), row(g_mix_out[0]), w_out[0].astype(_BF),
                      row(g_post_mix[0]), row(g_pre_mlp[0]), w_mlp_up[0].astype(_BF),
                      w_mlp_down[0].astype(_BF), row(g_post_mlp[0]), tile=ROW_TILE)
```

```python
import functools
import math

import jax
import jax.numpy as jnp
from jax import lax
from jax.experimental import pallas as pl
from jax.experimental.pallas import tpu as pltpu

D_MODEL = 1024
N_META = 16
ATTN_WIDTH = 512
SSM_WIDTH = 512
N_HEADS = 8
V_HEAD_DIM = 64
QK_NOPE_DIM = 64
QK_ROPE_DIM = 32
QK_HEAD_DIM = QK_NOPE_DIM + QK_ROPE_DIM
Q_LORA_RANK = 384
KV_LORA_RANK = 256
ROPE_BASE = 10000.0
SSM_GROUP = 16
SSM_GROUPS = 32
SSM_STATE = 64
D_FF = 4 * D_MODEL
EPS = 1e-6
OFF_KV = Q_LORA_RANK
OFF_KR = OFF_KV + KV_LORA_RANK
OFF_U = OFF_KR + QK_ROPE_DIM

LANES = 128
SUBLANES = 8
HEAD_PAD = LANES
ONES_LANE = V_HEAD_DIM
ATTN_TQ = 512
ATTN_TK = 512
V_ROWS = 80
PW_Q = 0
PW_KV = PW_Q + Q_LORA_RANK
PW_END = PW_KV + KV_LORA_RANK

SSM_CHUNK = 32
CHUNK_W = SSM_CHUNK * SSM_GROUP
STATE_W = 4 * SSM_STATE
SLAB_T = LANES
SLAB_CHUNKS = SLAB_T // SSM_CHUNK
SLAB_W = SSM_GROUP * SLAB_T

ROW_TILE = 512
FF_TILE = 1024
BOUND_SLACK = 1.0 + 2.0 ** -6
BOUND_LIMIT = 60.0
VMEM_LIMIT = 56 * 1024 * 1024

_BF = jnp.bfloat16
_F32 = jnp.float32


def _dot(a, b):
    return jnp.dot(a, b, preferred_element_type=_F32)


def _rms(x, g):
    return x * lax.rsqrt(jnp.mean(x * x, axis=-1, keepdims=True) + EPS) * g


_NT = (((1,), (1,)), ((), ()))


def _rotate(x1, x2, cos_t, sin_t):
    return x1 * cos_t - x2 * sin_t, x1 * sin_t + x2 * cos_t


def _proj_kernel(x_ref, rope_ref, *refs, meta):
    if meta:
        gpre_ref, win_ref, wkr_ref, wu_ref, gkv_ref, wk_ref, wv_ref, k_ref, v_ref, u_ref = refs
    else:
        (gpre_ref, win_ref, wkr_ref, wu_ref, gq_ref, wq_ref, gkv_ref, wk_ref, wv_ref,
         q_ref, k_ref, v_ref, u_ref, qn_ref, kmx_ref) = refs
    tile = x_ref.shape[0]
    half = QK_ROPE_DIM // 2
    cos_t, sin_t = rope_ref[0], rope_ref[1]
    xn = _rms(x_ref[...], gpre_ref[...]).astype(_BF)
    proj = _dot(xn, win_ref[...])
    kvn = _rms(proj[:, PW_KV:PW_END], gkv_ref[...]).astype(_BF)
    krt = lax.dot_general(wkr_ref[...], xn, _NT, preferred_element_type=_F32)
    r1, r2 = _rotate(krt[:half], krt[half:], cos_t, sin_t)
    kr = jnp.concatenate([jnp.zeros((QK_NOPE_DIM, tile), _F32), r1, r2,
                          jnp.zeros((HEAD_PAD - QK_HEAD_DIM, tile), _F32)], axis=0).T
    kk = _dot(kvn, wk_ref[...])
    vt = lax.dot_general(wv_ref[...], kvn, _NT, preferred_element_type=_F32)
    ones_tail = (lax.broadcasted_iota(jnp.int32, (HEAD_PAD - V_HEAD_DIM, tile), 0) == 0).astype(_F32)
    if meta:
        u_ref[...] = lax.dot_general(xn, wu_ref[...], _NT, preferred_element_type=_F32)
    else:
        ut = lax.dot_general(wu_ref[...], xn, _NT, preferred_element_type=_F32)
        for g in range(SSM_GROUPS):
            for c in range(tile // SLAB_T):
                u_ref[g, c] = ut[g * SSM_GROUP:(g + 1) * SSM_GROUP, c * SLAB_T:(c + 1) * SLAB_T]
        qn = _rms(proj[:, PW_Q:PW_KV], gq_ref[...]).astype(_BF)
        qt = lax.dot_general(wq_ref[...], qn, _NT, preferred_element_type=_F32)
        zero_rows = jnp.zeros((HEAD_PAD - QK_HEAD_DIM, tile), _F32)
    for h in range(N_HEADS):
        k_h = (kk[:, h * HEAD_PAD:(h + 1) * HEAD_PAD] + kr).astype(_BF)
        k_ref[h] = k_h
        v_ref[h] = jnp.concatenate([vt[h * V_HEAD_DIM:(h + 1) * V_HEAD_DIM], ones_tail], axis=0).astype(_BF)
        if not meta:
            blk = qt[h * QK_HEAD_DIM:(h + 1) * QK_HEAD_DIM]
            r1, r2 = _rotate(blk[QK_NOPE_DIM:QK_NOPE_DIM + half], blk[QK_NOPE_DIM + half:], cos_t, sin_t)
            qt_h = jnp.concatenate([blk[:QK_NOPE_DIM], r1, r2, zero_rows], axis=0).astype(_BF)
            q_ref[h] = qt_h
            qt_f = qt_h.astype(_F32)
            qn_ref[h] = jnp.sqrt(jnp.sum(qt_f * qt_f, axis=0, keepdims=True))
            k_f = k_h.astype(_F32)
            kmx_ref[h] = jnp.broadcast_to(jnp.max(jnp.sum(k_f * k_f, axis=1, keepdims=True), axis=0, keepdims=True),
                                          (1, LANES))


def _const_spec(shape):
    nd = len(shape)
    return pl.BlockSpec(shape, lambda *_: (0,) * nd)


def _proj_call(x, rope, gpre, win, wkr_t, wu_t, gq, wq_t, gkv, wk, wv_t, *, tile, meta):
    bsz, seq, _ = x.shape
    nt = seq // tile
    row_spec = lambda w: pl.BlockSpec((None, tile, w), lambda b, i: (b, i, 0))
    rope_spec = pl.BlockSpec((None, None, 2, QK_ROPE_DIM // 2, tile), lambda b, i: (b, i, 0, 0, 0))
    k_spec = pl.BlockSpec((None, N_HEADS, tile, HEAD_PAD), lambda b, i: (b, 0, i, 0))
    t_spec = pl.BlockSpec((None, N_HEADS, None, HEAD_PAD, tile), lambda b, i: (b, 0, i, 0, 0))
    k_shape = jax.ShapeDtypeStruct((bsz, N_HEADS, seq, HEAD_PAD), _BF)
    t_shape = jax.ShapeDtypeStruct((bsz, N_HEADS, nt, HEAD_PAD, tile), _BF)
    w_specs = lambda *ws: [_const_spec(w.shape) for w in ws]
    if meta:
        args = (x, rope, gpre, win, wkr_t, wu_t, gkv, wk, wv_t)
        in_specs = [row_spec(D_MODEL), rope_spec] + w_specs(*args[2:])
        out_specs = [k_spec, t_spec, row_spec(SSM_WIDTH)]
        out_shape = [k_shape, t_shape, jax.ShapeDtypeStruct((bsz, seq, SSM_WIDTH), _F32)]
    else:
        args = (x, rope, gpre, win, wkr_t, wu_t, gq, wq_t, gkv, wk, wv_t)
        in_specs = [row_spec(D_MODEL), rope_spec] + w_specs(*args[2:])
        norm_spec = lambda w: pl.BlockSpec((None, N_HEADS, None, 1, w), lambda b, i: (b, 0, i, 0, 0))
        u_spec = pl.BlockSpec((SSM_GROUPS, tile // SLAB_T, SSM_GROUP, SLAB_T), lambda b, i: (0, i, b, 0))
        out_specs = [t_spec, k_spec, t_spec, u_spec, norm_spec(tile), norm_spec(LANES)]
        out_shape = [t_shape, k_shape, t_shape,
                     jax.ShapeDtypeStruct((SSM_GROUPS, seq // SLAB_T, bsz * SSM_GROUP, SLAB_T), _F32),
                     jax.ShapeDtypeStruct((bsz, N_HEADS, nt, 1, tile), _F32),
                     jax.ShapeDtypeStruct((bsz, N_HEADS, nt, 1, LANES), _F32)]
    return pl.pallas_call(
        functools.partial(_proj_kernel, meta=meta),
        grid=(bsz, nt),
        in_specs=in_specs,
        out_specs=out_specs,
        out_shape=out_shape,
        compiler_params=pltpu.CompilerParams(
            dimension_semantics=("parallel", "parallel"), vmem_limit_bytes=VMEM_LIMIT),
        name="proj_meta" if meta else "proj",
    )(*args)


def _attn_finish(accs, o_ref):
    halves = [(acc * (1.0 / acc[ONES_LANE:ONES_LANE + 1, :]))[:V_HEAD_DIM] for acc in accs]
    o_ref[...] = jnp.concatenate(halves, axis=0).T


def _lane_concat(ref, hh):
    return jnp.concatenate([ref[hh, j] for j in range(ref.shape[1])], axis=1)


def _attn_bounded_kernel(qt_ref, mrow_ref, k_ref, vt_ref, km_ref, vtm_ref, o_ref, *, nk, tk):
    tq = o_ref.shape[0]
    per_slab = vt_ref.shape[3] // tk
    key_row = lax.broadcasted_iota(jnp.int32, (LANES, tq), 0)
    accs = []
    for hh in range(2):
        qt = _lane_concat(qt_ref, hh)
        mrow = _lane_concat(mrow_ref, hh)
        s0 = jnp.where(key_row < N_META, _dot(km_ref[hh], qt), -jnp.inf)
        acc = _dot(vtm_ref[hh, :V_ROWS, :], jnp.exp2(s0 - mrow).astype(_BF))
        scores = lambda c: _dot(k_ref[hh, c * tk:(c + 1) * tk, :], qt)
        s_next = scores(0)
        for c in range(nk):
            s = s_next
            if c + 1 < nk:
                s_next = scores(c + 1)
            vt_c = vt_ref[hh, c // per_slab, :V_ROWS, (c % per_slab) * tk:(c % per_slab + 1) * tk]
            acc = acc + _dot(vt_c, jnp.exp2(s - mrow).astype(_BF))
        accs.append(acc)
    _attn_finish(accs, o_ref)


def _attn_online_kernel(qt_ref, k_ref, vt_ref, km_ref, vtm_ref, o_ref, s0_scr, s1_scr, m_scr, acc_scr, *, nk, tk):
    tq = o_ref.shape[0]
    key_row = lax.broadcasted_iota(jnp.int32, (LANES, tq), 0)
    for hh in range(2):
        s0 = jnp.where(key_row < N_META, _dot(km_ref[hh], _lane_concat(qt_ref, hh)), -jnp.inf)
        m0 = jnp.max(s0, axis=0, keepdims=True)
        m_scr[hh] = m0
        acc_scr[hh] = _dot(vtm_ref[hh, :V_ROWS, :], jnp.exp2(s0 - m0).astype(_BF))

    def scores(buf, c):
        off = pl.multiple_of(c * tk, tk)
        for hh in range(2):
            buf[hh] = _dot(k_ref[hh, pl.ds(off, tk), :], _lane_concat(qt_ref, hh))

    def accumulate(buf, c):
        for hh in range(2):
            s = buf[hh]
            m = m_scr[hh]
            m_new = jnp.maximum(m, jnp.max(s, axis=0, keepdims=True))
            m_scr[hh] = m_new
            p = jnp.exp2(s - m_new).astype(_BF)
            acc_scr[hh] = jnp.exp2(m - m_new) * acc_scr[hh] + _dot(vt_ref[hh, c, :V_ROWS, :], p)

    scores(s0_scr, 0)

    def body(t, _):
        scores(s1_scr, 2 * t + 1)
        accumulate(s0_scr, 2 * t)
        scores(s0_scr, 2 * t + 2)
        accumulate(s1_scr, 2 * t + 1)
        return 0

    lax.fori_loop(0, nk // 2 - 1, body, 0)
    scores(s1_scr, nk - 1)
    accumulate(s0_scr, nk - 2)
    accumulate(s1_scr, nk - 1)
    _attn_finish([acc_scr[0], acc_scr[1]], o_ref)


def _attn_call(qt, mrow, k, vt, km, vtm, *, bounded):
    bsz, _, nslab, _, slab = qt.shape
    seq = k.shape[2]
    tq = min(ATTN_TQ, seq)
    tk = min(ATTN_TK, slab)
    qs = tq // slab
    assert tq % slab == 0 and seq % tq == 0 and slab % tk == 0
    q_spec = lambda rows: pl.BlockSpec((None, 2, qs, rows, slab), lambda b, hp, i: (b, hp, i, 0, 0))
    kv_specs = [pl.BlockSpec((None, 2, seq, HEAD_PAD), lambda b, hp, i: (b, hp, 0, 0)),
                pl.BlockSpec((None, 2, nslab, HEAD_PAD, slab), lambda b, hp, i: (b, hp, 0, 0, 0)),
                pl.BlockSpec((2, LANES, HEAD_PAD), lambda b, hp, i: (hp, 0, 0)),
                pl.BlockSpec((2, HEAD_PAD, LANES), lambda b, hp, i: (hp, 0, 0))]
    if bounded:
        body = functools.partial(_attn_bounded_kernel, nk=seq // tk, tk=tk)
        in_specs = [q_spec(HEAD_PAD), q_spec(1)] + kv_specs
        args, scratch = (qt, mrow, k, vt, km, vtm), []
    else:
        assert nslab % 2 == 0 and nslab >= 4
        body = functools.partial(_attn_online_kernel, nk=nslab, tk=slab)
        in_specs = [q_spec(HEAD_PAD)] + kv_specs
        args = (qt, k, vt, km, vtm)
        scratch = [pltpu.VMEM((2, slab, tq), _F32), pltpu.VMEM((2, slab, tq), _F32),
                   pltpu.VMEM((2, 1, tq), _F32), pltpu.VMEM((2, V_ROWS, tq), _F32)]
    return pl.pallas_call(
        body,
        grid=(bsz, N_HEADS // 2, seq // tq),
        in_specs=in_specs,
        out_specs=pl.BlockSpec((None, tq, LANES), lambda b, hp, i: (b, i, hp)),
        out_shape=jax.ShapeDtypeStruct((bsz, seq, ATTN_WIDTH), _F32),
        scratch_shapes=scratch,
        compiler_params=pltpu.CompilerParams(
            dimension_semantics=("parallel", "parallel", "arbitrary"), vmem_limit_bytes=VMEM_LIMIT),
        name="attn_bounded" if bounded else "attn_online",
    )(*args)


def _cmul_add(ar, ai, xr, xi, sr, si):
    return ar * xr - ai * xi + sr, ar * xi + ai * xr + si


def _s5_kernel(a_ref, um_ref, pin_ref, pout_ref, lag_ref, w_ref, v_ref, t_ref, y_ref, sup_scr, ent_scr,
               *, nslab, bsz):
    rows = nslab * bsz
    half = STATE_W // 2
    a = jnp.concatenate([a_ref[pl.ds(i, rows, stride=SSM_GROUP), :] for i in range(SSM_GROUP)], axis=1)
    ap = _dot(a.astype(_BF), pin_ref[...]).astype(_BF)
    uc = [ap[:, c * CHUNK_W:(c + 1) * CHUNK_W] for c in range(SLAB_CHUNKS)]
    w = w_ref[...]
    s = [_dot(u, w) for u in uc]
    sr = [x[:, :half] for x in s]
    si = [x[:, half:] for x in s]
    t = t_ref[...]
    trow = lambda r: t[r:r + 1, :]
    sup_r = sup_i = None
    for c in range(SLAB_CHUNKS):
        cr, ci = trow(c), trow(SLAB_CHUNKS + c)
        pr = cr * sr[c] - ci * si[c]
        pi = cr * si[c] + ci * sr[c]
        sup_r = pr if sup_r is None else sup_r + pr
        sup_i = pi if sup_i is None else sup_i + pi
    sup_scr[:, :half] = sup_r
    sup_scr[:, half:] = sup_i

    lane = lax.broadcasted_iota(jnp.int32, (bsz, half), 1)
    fwd = lane < SSM_STATE
    sm = _dot(um_ref[...], w)
    xr = jnp.where(fwd, sm[:bsz, :half], 0.0)
    xi = jnp.where(fwd, sm[:bsz, half:], 0.0)
    a_slab_r, a_slab_i = trow(2 * SLAB_CHUNKS), trow(2 * SLAB_CHUNKS + 1)
    for j in range(nslab):
        rf = j * bsz
        rb = (nslab - 1 - j) * bsz
        ent_scr[rf:rf + bsz, 0:SSM_STATE] = xr[:, 0:SSM_STATE]
        ent_scr[rb:rb + bsz, SSM_STATE:half] = xr[:, SSM_STATE:half]
        ent_scr[rf:rf + bsz, half:half + SSM_STATE] = xi[:, 0:SSM_STATE]
        ent_scr[rb:rb + bsz, half + SSM_STATE:STATE_W] = xi[:, SSM_STATE:half]
        s_r = jnp.where(fwd, sup_scr[rf:rf + bsz, :half], sup_scr[rb:rb + bsz, :half])
        s_i = jnp.where(fwd, sup_scr[rf:rf + bsz, half:], sup_scr[rb:rb + bsz, half:])
        xr, xi = _cmul_add(a_slab_r, a_slab_i, xr, xi, s_r, s_i)

    ent = ent_scr[...]
    a_r, a_i = trow(2 * SLAB_CHUNKS + 2), trow(2 * SLAB_CHUNKS + 3)
    xf = [(ent[:, :half], ent[:, half:])]
    for c in range(1, SLAB_CHUNKS):
        xf.append(_cmul_add(a_r, a_i, xf[-1][0], xf[-1][1], sr[c - 1], si[c - 1]))
    xb = [(ent[:, :half], ent[:, half:])]
    for c in range(SLAB_CHUNKS - 2, -1, -1):
        xb.insert(0, _cmul_add(a_r, a_i, xb[0][0], xb[0][1], sr[c + 1], si[c + 1]))
    fwd_rows = lax.broadcasted_iota(jnp.int32, (rows, half), 1) < SSM_STATE
    m = jnp.concatenate(
        [pltpu.roll(jnp.broadcast_to(lag_ref[i:i + 1, :], (SSM_CHUNK, 2 * CHUNK_W)), 0, 1,
                    stride=SSM_GROUP, stride_axis=0)[:, :CHUNK_W] for i in range(SSM_GROUP)], axis=0).astype(_BF)
    v = v_ref[...]
    ys = []
    for c in range(SLAB_CHUNKS):
        xin = jnp.concatenate([jnp.where(fwd_rows, xf[c][0], xb[c][0]),
                               jnp.where(fwd_rows, xf[c][1], xb[c][1])], axis=1).astype(_BF)
        ys.append((_dot(uc[c], m) + _dot(xin, v)).astype(_BF))
    yp = _dot(jnp.concatenate(ys, axis=1), pout_ref[...])
    for o in range(SSM_GROUP):
        y_ref[pl.ds(o, rows, stride=SSM_GROUP), :] = yp[:, o * SLAB_T:(o + 1) * SLAB_T]


def _s5_call(a, um, pin, pout, lags, w_mat, v_mat, tab, *, nslab, bsz):
    n = nslab * bsz * SSM_GROUP
    rows = nslab * bsz
    g_spec = lambda *shape: pl.BlockSpec((None,) + shape, lambda g: (g,) + (0,) * len(shape))
    perm_spec = pl.BlockSpec((SLAB_W, SLAB_W), lambda g: (0, 0), pipeline_mode=pl.Buffered(1))
    return pl.pallas_call(
        functools.partial(_s5_kernel, nslab=nslab, bsz=bsz),
        grid=(SSM_GROUPS,),
        in_specs=[g_spec(n, SLAB_T), g_spec(SUBLANES, CHUNK_W), perm_spec, perm_spec,
                  g_spec(SSM_GROUP, 2 * CHUNK_W), g_spec(CHUNK_W, STATE_W), g_spec(STATE_W, CHUNK_W),
                  g_spec(2 * SUBLANES, STATE_W // 2)],
        out_specs=g_spec(n, SLAB_T),
        out_shape=jax.ShapeDtypeStruct((SSM_GROUPS, n, SLAB_T), _F32),
        scratch_shapes=[pltpu.VMEM((rows, STATE_W), _F32), pltpu.VMEM((rows, STATE_W), _F32)],
        compiler_params=pltpu.CompilerParams(
            dimension_semantics=("parallel",), vmem_limit_bytes=VMEM_LIMIT),
        name="s5",
    )(a, um, pin, pout, lags, w_mat, v_mat, tab)


def _s5_matrices(a_re, a_im, log_dt, b_re, b_im, c_re, c_im, d_skip):
    tc = SSM_CHUNK
    lam = lax.complex(jnp.minimum(a_re.astype(_F32), -1e-4), a_im.astype(_F32))
    dt = jnp.exp(log_dt.astype(_F32))[..., None]
    lam_dt = lam * dt
    lam_bar = jnp.exp(lam_dt)
    b_bar = ((lam_bar - 1.0) / lam)[..., None] * lax.complex(b_re.astype(_F32), b_im.astype(_F32))
    c_c = lax.complex(c_re.astype(_F32), c_im.astype(_F32))
    k_idx = jnp.arange(tc + 1, dtype=_F32)
    pw = jnp.exp(lam_dt[:, :, None, :] * k_idx[None, None, :, None])
    kern = jnp.real(jnp.einsum('dgop,dgkp,dgpi->dgkoi', c_c, pw[:, :, :tc], b_bar))
    d_g = d_skip.astype(_F32).reshape(SSM_GROUPS, SSM_GROUP)
    center = kern[0][:, :1] + kern[1][:, :1] + (jnp.eye(SSM_GROUP, dtype=_F32)[None] * d_g[:, :, None])[:, None]
    lags = jnp.concatenate([center, kern[0][:, 1:], jnp.zeros_like(center), kern[1][:, :0:-1]], axis=1)
    lags = lags.transpose(0, 3, 1, 2).reshape(SSM_GROUPS, SSM_GROUP, 2 * CHUNK_W)
    wf = b_bar[0].transpose(0, 2, 1)[:, :, None, :] * pw[0][:, tc - 1::-1][:, None, :, :]
    wb = b_bar[1].transpose(0, 2, 1)[:, :, None, :] * pw[1][:, :tc][:, None, :, :]
    w_mat = jnp.concatenate([jnp.real(wf), jnp.real(wb), jnp.imag(wf), jnp.imag(wb)], axis=-1)
    w_mat = w_mat.reshape(SSM_GROUPS, CHUNK_W, STATE_W)
    gf = pw[0][:, 1:tc + 1][:, :, None, :] * c_c[0][:, None, :, :]
    gb = pw[1][:, tc:0:-1][:, :, None, :] * c_c[1][:, None, :, :]
    v_mat = jnp.concatenate([jnp.real(gf), jnp.real(gb), -jnp.imag(gf), -jnp.imag(gb)], axis=-1)
    v_mat = v_mat.reshape(SSM_GROUPS, CHUNK_W, STATE_W).transpose(0, 2, 1)
    n_idx = jnp.arange(SLAB_CHUNKS + 1, dtype=_F32) * tc
    pc = jnp.exp(lam_dt[:, :, None, :] * n_idx[None, None, :, None])
    coef = jnp.concatenate([pc[0][:, SLAB_CHUNKS - 1::-1], pc[1][:, :SLAB_CHUNKS]], axis=-1)
    both = lambda n: jnp.concatenate([pc[0][:, n], pc[1][:, n]], axis=-1)[:, None, :]
    a_slab, a_chunk = both(SLAB_CHUNKS), both(1)
    tab = jnp.concatenate([jnp.real(coef), jnp.imag(coef), jnp.real(a_slab), jnp.imag(a_slab),
                           jnp.real(a_chunk), jnp.imag(a_chunk)], axis=1)
    tab = jnp.pad(tab, ((0, 0), (0, 2 * SUBLANES - tab.shape[1]), (0, 0)))
    return lags, w_mat, v_mat, tab


def _slab_permutations():
    r = lax.broadcasted_iota(jnp.int32, (SLAB_W, SLAB_W), 0)
    c = lax.broadcasted_iota(jnp.int32, (SLAB_W, SLAB_W), 1)
    ch, tok = r // SLAB_T, r % SLAB_T
    pin = c == (tok // SSM_CHUNK) * CHUNK_W + ch * SSM_CHUNK + tok % SSM_CHUNK
    ch, tok = c // SLAB_T, c % SLAB_T
    pout = r == (tok // SSM_CHUNK) * CHUNK_W + (tok % SSM_CHUNK) * SSM_GROUP + ch
    return pin.astype(_BF), pout.astype(_BF)


def _post_kernel(x_ref, attn_ref, y_ref, wglu_ref, gmix_ref, wout_ref, gpm_ref, gpre_ref,
                 wup_ref, wdn_ref, gpost_ref, o_ref):
    nslab = y_ref.shape[1]
    yt = jnp.concatenate([jnp.concatenate([y_ref[g, c] for c in range(nslab)], axis=1)
                          for g in range(SSM_GROUPS)], axis=0)
    y = yt.T
    gy = 0.5 * y * (1.0 + jnp.tanh(math.sqrt(2.0 / math.pi) * (y + 0.044715 * (y * y * y))))
    z = _dot(gy.astype(_BF), wglu_ref[...])
    ssm = z[:, :SSM_WIDTH] * (1.0 / (1.0 + jnp.exp(-z[:, SSM_WIDTH:])))
    gmix = gmix_ref[...]
    mix = jnp.concatenate([_rms(attn_ref[...], gmix[:, :ATTN_WIDTH]),
                           _rms(ssm, gmix[:, ATTN_WIDTH:])], axis=-1).astype(_BF)
    h1 = x_ref[...] + _rms(_dot(mix, wout_ref[...]), gpm_ref[...])
    hn = _rms(h1, gpre_ref[...]).astype(_BF)
    acc = None
    for c in range(D_FF // FF_TILE):
        up = _dot(hn, wup_ref[:, c * FF_TILE:(c + 1) * FF_TILE])
        up = jnp.maximum(up, 0.0)
        part = _dot((up * up).astype(_BF), wdn_ref[c * FF_TILE:(c + 1) * FF_TILE, :])
        acc = part if acc is None else acc + part
    o_ref[...] = h1 + _rms(acc, gpost_ref[...])


def _post_call(x, attn, y, wglu, gmix, wout, gpm, gpre, wup, wdn, gpost, *, tile):
    bsz, seq, _ = x.shape
    row_spec = lambda w: pl.BlockSpec((None, tile, w), lambda b, i: (b, i, 0))
    wspec = lambda shape: pl.BlockSpec(shape, lambda b, i: (0, 0), pipeline_mode=pl.Buffered(1))
    y_spec = pl.BlockSpec((SSM_GROUPS, tile // SLAB_T, SSM_GROUP, SLAB_T), lambda b, i: (0, i, b, 0))
    return pl.pallas_call(
        _post_kernel,
        grid=(bsz, seq // tile),
        in_specs=[row_spec(D_MODEL), row_spec(ATTN_WIDTH), y_spec,
                  wspec((SSM_WIDTH, 2 * SSM_WIDTH)), wspec((1, D_MODEL)), wspec((D_MODEL, D_MODEL)),
                  wspec((1, D_MODEL)), wspec((1, D_MODEL)), wspec((D_MODEL, D_FF)),
                  wspec((D_FF, D_MODEL)), wspec((1, D_MODEL))],
        out_specs=row_spec(D_MODEL),
        out_shape=jax.ShapeDtypeStruct((bsz, seq, D_MODEL), _F32),
        compiler_params=pltpu.CompilerParams(
            dimension_semantics=("parallel", "parallel"), vmem_limit_bytes=VMEM_LIMIT),
        name="post",
    )(x, attn, y, wglu, gmix, wout, gpm, gpre, wup, wdn, gpost)


def _rope_tables(pos, tile):
    half = QK_ROPE_DIM // 2
    inv = 1.0 / (ROPE_BASE ** (jnp.arange(0, QK_ROPE_DIM, 2, dtype=_F32) / QK_ROPE_DIM))
    ang = pos.astype(_F32)[:, None, :] * inv[None, :, None]
    bsz, seq = pos.shape
    rope = jnp.stack([jnp.cos(ang), jnp.sin(ang)], axis=1)
    return rope.reshape(bsz, 2, half, seq // tile, tile).transpose(0, 3, 1, 2, 4)


def _prep_weights(w_in, w_uq, w_ukv):
    scale = QK_HEAD_DIM ** -0.5 * math.log2(math.e)
    win = w_in[:, :OFF_KR].astype(_BF)
    wkr_t = w_in[:, OFF_KR:OFF_U].T.astype(_BF)
    wu_t = w_in[:, OFF_U:].T.astype(_BF)
    wq_t = (w_uq * scale).T.astype(_BF)
    wkv3 = w_ukv.reshape(KV_LORA_RANK, N_HEADS, QK_NOPE_DIM + V_HEAD_DIM)
    wk = jnp.concatenate([wkv3[..., :QK_NOPE_DIM],
                          jnp.zeros((KV_LORA_RANK, N_HEADS, HEAD_PAD - QK_NOPE_DIM), _F32)], axis=-1)
    wk = wk.reshape(KV_LORA_RANK, N_HEADS * HEAD_PAD).astype(_BF)
    wv_t = wkv3[..., QK_NOPE_DIM:].reshape(KV_LORA_RANK, N_HEADS * V_HEAD_DIM).T.astype(_BF)
    return win, wkr_t, wu_t, wq_t, wk, wv_t


def kernel(x, positions, meta_tokens, g_pre_mix, w_in, g_q_lat, w_uq, g_kv_lat, w_ukv,
           ssm_A_re, ssm_A_im, ssm_log_dt, ssm_B_re, ssm_B_im, ssm_C_re, ssm_C_im, ssm_D,
           w_glu, g_mix_out, w_out, g_post_mix, g_pre_mlp, w_mlp_up, w_mlp_down, g_post_mlp):
    bsz, seq, _ = x.shape
    assert seq % ROW_TILE == 0 and ROW_TILE % SLAB_T == 0 and bsz <= SUBLANES
    assert N_META <= SSM_CHUNK
    row = lambda g: g.reshape(1, -1).astype(_F32)

    win, wkr_t, wu_t, wq_t, wk, wv_t = _prep_weights(w_in[0], w_uq[0], w_ukv[0])
    weights = (row(g_pre_mix[0]), win, wkr_t, wu_t, row(g_q_lat[0]), wq_t, row(g_kv_lat[0]), wk, wv_t)
    rope = _rope_tables(positions.astype(jnp.int32) + N_META, ROW_TILE)
    qt, k, vt, u, q_norm, k_sq = _proj_call(x, rope, *weights, tile=ROW_TILE, meta=False)
    meta_x = jnp.pad(meta_tokens.astype(x.dtype), ((0, LANES - N_META), (0, 0)))[None]
    rope_m = _rope_tables(jnp.arange(LANES, dtype=jnp.int32)[None], LANES)
    k_m, vt_m, u_m = _proj_call(meta_x, rope_m, *weights, tile=LANES, meta=True)
    km, vtm, u_m = k_m[0], vt_m[0, :, 0], u_m[:, :N_META]
    k_m_sq = jnp.max(jnp.sum(jnp.square(k_m[0].astype(_F32)), axis=-1), axis=-1)
    k_max = jnp.sqrt(jnp.maximum(jnp.max(k_sq, axis=(2, 3, 4)), k_m_sq[None]))
    mrow = q_norm * (k_max * BOUND_SLACK)[:, :, None, None, None]
    attn = lax.cond(jnp.max(mrow) <= BOUND_LIMIT,
                    lambda: _attn_call(qt, mrow, k, vt, km, vtm, bounded=True),
                    lambda: _attn_call(qt, mrow, k, vt, km, vtm, bounded=False))

    um = u_m[0].astype(_BF).reshape(N_META, SSM_GROUPS, SSM_GROUP).transpose(1, 2, 0)
    um = jnp.pad(um, ((0, 0), (0, 0), (SSM_CHUNK - N_META, 0))).reshape(SSM_GROUPS, 1, CHUNK_W)
    um = jnp.broadcast_to(um, (SSM_GROUPS, SUBLANES, CHUNK_W))
    lags, w_mat, v_mat, tab = _s5_matrices(ssm_A_re[0], ssm_A_im[0], ssm_log_dt[0], ssm_B_re[0],
                                            ssm_B_im[0], ssm_C_re[0], ssm_C_im[0], ssm_D[0])
    pin, pout = _slab_permutations()
    nslab = seq // SLAB_T
    yg = _s5_call(u.reshape(SSM_GROUPS, nslab * bsz * SSM_GROUP, SLAB_T), um, pin, pout,
                  lags, w_mat.astype(_BF), v_mat.astype(_BF), tab, nslab=nslab, bsz=bsz)
    y = yg.reshape(SSM_GROUPS, nslab, bsz * SSM_GROUP, SLAB_T)

    return _post_call(x, attn, y, w_glu[0].astype(_BF), row(g_mix_out[0]), w_out[0].astype(_BF),
                      row(g_post_mix[0]), row(g_pre_mlp[0]), w_mlp_up[0].astype(_BF),
                      w_mlp_down[0].astype(_BF), row(g_post_mlp[0]), tile=ROW_TILE)
```

```python
import functools
import math

import jax
import jax.numpy as jnp
from jax import lax
from jax.experimental import pallas as pl
from jax.experimental.pallas import tpu as pltpu

D_MODEL = 1024
N_META = 16
ATTN_WIDTH = 512
SSM_WIDTH = 512
N_HEADS = 8
V_HEAD_DIM = 64
QK_NOPE_DIM = 64
QK_ROPE_DIM = 32
QK_HEAD_DIM = QK_NOPE_DIM + QK_ROPE_DIM
Q_LORA_RANK = 384
KV_LORA_RANK = 256
ROPE_BASE = 10000.0
SSM_GROUP = 16
SSM_GROUPS = 32
SSM_STATE = 64
D_FF = 4 * D_MODEL
EPS = 1e-6
OFF_KV = Q_LORA_RANK
OFF_KR = OFF_KV + KV_LORA_RANK
OFF_U = OFF_KR + QK_ROPE_DIM

LANES = 128
SUBLANES = 8
HEAD_PAD = LANES
ONES_LANE = V_HEAD_DIM
ATTN_TQ = 1024
ATTN_TK = 256
V_ROWS = HEAD_PAD
PW_Q = 0
PW_KV = PW_Q + Q_LORA_RANK
PW_END = PW_KV + KV_LORA_RANK

SSM_CHUNK = 32
CHUNK_W = SSM_CHUNK * SSM_GROUP
STATE_W = 4 * SSM_STATE
SLAB_T = LANES
SLAB_CHUNKS = SLAB_T // SSM_CHUNK
SLAB_W = SSM_GROUP * SLAB_T

ROW_TILE = 512
FF_TILE = 1024
BOUND_SLACK = 1.0 + 2.0 ** -6
BOUND_LIMIT = 60.0
VMEM_LIMIT = 56 * 1024 * 1024

_BF = jnp.bfloat16
_F32 = jnp.float32


def _dot(a, b):
    return jnp.dot(a, b, preferred_element_type=_F32)


def _rms(x, g):
    return x * lax.rsqrt(jnp.mean(x * x, axis=-1, keepdims=True) + EPS) * g


_NT = (((1,), (1,)), ((), ()))


def _rotate(x1, x2, cos_t, sin_t):
    return x1 * cos_t - x2 * sin_t, x1 * sin_t + x2 * cos_t


def _proj_kernel(x_ref, rope_ref, *refs, meta):
    if meta:
        gpre_ref, win_ref, wkr_ref, wu_ref, gkv_ref, wk_ref, wv_ref, k_ref, v_ref, u_ref = refs
    else:
        (gpre_ref, win_ref, wkr_ref, wu_ref, gq_ref, wq_ref, gkv_ref, wk_ref, wv_ref,
         q_ref, k_ref, v_ref, u_ref, qn_ref, kmx_ref) = refs
    tile = x_ref.shape[0]
    half = QK_ROPE_DIM // 2
    cos_t, sin_t = rope_ref[0], rope_ref[1]
    xn = _rms(x_ref[...], gpre_ref[...]).astype(_BF)
    proj = _dot(xn, win_ref[...])
    kvn = _rms(proj[:, PW_KV:PW_END], gkv_ref[...]).astype(_BF)
    krt = lax.dot_general(wkr_ref[...], xn, _NT, preferred_element_type=_F32)
    r1, r2 = _rotate(krt[:half], krt[half:], cos_t, sin_t)
    kr = jnp.concatenate([jnp.zeros((QK_NOPE_DIM, tile), _F32), r1, r2,
                          jnp.zeros((HEAD_PAD - QK_HEAD_DIM, tile), _F32)], axis=0).T
    kk = _dot(kvn, wk_ref[...])
    vt = lax.dot_general(wv_ref[...], kvn, _NT, preferred_element_type=_F32)
    ones_tail = (lax.broadcasted_iota(jnp.int32, (HEAD_PAD - V_HEAD_DIM, tile), 0) == 0).astype(_F32)
    if meta:
        u_ref[...] = lax.dot_general(xn, wu_ref[...], _NT, preferred_element_type=_F32)
    else:
        ut = lax.dot_general(wu_ref[...], xn, _NT, preferred_element_type=_F32)
        for g in range(SSM_GROUPS):
            for c in range(tile // SLAB_T):
                u_ref[g, c] = ut[g * SSM_GROUP:(g + 1) * SSM_GROUP, c * SLAB_T:(c + 1) * SLAB_T]
        qn = _rms(proj[:, PW_Q:PW_KV], gq_ref[...]).astype(_BF)
        qt = lax.dot_general(wq_ref[...], qn, _NT, preferred_element_type=_F32)
        zero_rows = jnp.zeros((HEAD_PAD - QK_HEAD_DIM, tile), _F32)
    for h in range(N_HEADS):
        k_h = (kk[:, h * HEAD_PAD:(h + 1) * HEAD_PAD] + kr).astype(_BF)
        k_ref[h] = k_h
        v_ref[h] = jnp.concatenate([vt[h * V_HEAD_DIM:(h + 1) * V_HEAD_DIM], ones_tail], axis=0).astype(_BF)
        if not meta:
            blk = qt[h * QK_HEAD_DIM:(h + 1) * QK_HEAD_DIM]
            r1, r2 = _rotate(blk[QK_NOPE_DIM:QK_NOPE_DIM + half], blk[QK_NOPE_DIM + half:], cos_t, sin_t)
            qt_h = jnp.concatenate([blk[:QK_NOPE_DIM], r1, r2, zero_rows], axis=0).astype(_BF)
            q_ref[h] = qt_h
            qt_f = qt_h.astype(_F32)
            qn_ref[h] = jnp.sqrt(jnp.sum(qt_f * qt_f, axis=0, keepdims=True))
            k_f = k_h.astype(_F32)
            kmx_ref[h] = jnp.broadcast_to(jnp.max(jnp.sum(k_f * k_f, axis=1, keepdims=True), axis=0, keepdims=True),
                                          (1, LANES))


def _const_spec(shape):
    nd = len(shape)
    return pl.BlockSpec(shape, lambda *_: (0,) * nd)


def _proj_call(x, rope, gpre, win, wkr_t, wu_t, gq, wq_t, gkv, wk, wv_t, *, tile, meta):
    bsz, seq, _ = x.shape
    nt = seq // tile
    row_spec = lambda w: pl.BlockSpec((None, tile, w), lambda b, i: (b, i, 0))
    rope_spec = pl.BlockSpec((None, None, 2, QK_ROPE_DIM // 2, tile), lambda b, i: (b, i, 0, 0, 0))
    k_spec = pl.BlockSpec((None, N_HEADS, tile, HEAD_PAD), lambda b, i: (b, 0, i, 0))
    t_spec = pl.BlockSpec((None, N_HEADS, None, HEAD_PAD, tile), lambda b, i: (b, 0, i, 0, 0))
    k_shape = jax.ShapeDtypeStruct((bsz, N_HEADS, seq, HEAD_PAD), _BF)
    t_shape = jax.ShapeDtypeStruct((bsz, N_HEADS, nt, HEAD_PAD, tile), _BF)
    w_specs = lambda *ws: [_const_spec(w.shape) for w in ws]
    if meta:
        args = (x, rope, gpre, win, wkr_t, wu_t, gkv, wk, wv_t)
        in_specs = [row_spec(D_MODEL), rope_spec] + w_specs(*args[2:])
        out_specs = [k_spec, t_spec, row_spec(SSM_WIDTH)]
        out_shape = [k_shape, t_shape, jax.ShapeDtypeStruct((bsz, seq, SSM_WIDTH), _F32)]
    else:
        args = (x, rope, gpre, win, wkr_t, wu_t, gq, wq_t, gkv, wk, wv_t)
        in_specs = [row_spec(D_MODEL), rope_spec] + w_specs(*args[2:])
        norm_spec = lambda w: pl.BlockSpec((None, N_HEADS, None, 1, w), lambda b, i: (b, 0, i, 0, 0))
        u_spec = pl.BlockSpec((SSM_GROUPS, tile // SLAB_T, SSM_GROUP, SLAB_T), lambda b, i: (0, i, b, 0))
        out_specs = [t_spec, k_spec, t_spec, u_spec, norm_spec(tile), norm_spec(LANES)]
        out_shape = [t_shape, k_shape, t_shape,
                     jax.ShapeDtypeStruct((SSM_GROUPS, seq // SLAB_T, bsz * SSM_GROUP, SLAB_T), _F32),
                     jax.ShapeDtypeStruct((bsz, N_HEADS, nt, 1, tile), _F32),
                     jax.ShapeDtypeStruct((bsz, N_HEADS, nt, 1, LANES), _F32)]
    return pl.pallas_call(
        functools.partial(_proj_kernel, meta=meta),
        grid=(bsz, nt),
        in_specs=in_specs,
        out_specs=out_specs,
        out_shape=out_shape,
        compiler_params=pltpu.CompilerParams(
            dimension_semantics=("parallel", "parallel"), vmem_limit_bytes=VMEM_LIMIT),
        name="proj_meta" if meta else "proj",
    )(*args)


def _attn_finish(accs, o_ref):
    halves = [(acc * (1.0 / acc[ONES_LANE:ONES_LANE + 1, :]))[:V_HEAD_DIM] for acc in accs]
    o_ref[...] = jnp.concatenate(halves, axis=0).T


def _lane_concat(ref, hh):
    return jnp.concatenate([ref[hh, j] for j in range(ref.shape[1])], axis=1)


def _attn_bounded_kernel(qt_ref, mrow_ref, k_ref, vt_ref, km_ref, vtm_ref, o_ref, *, nk, tk):
    tq = o_ref.shape[0]
    per_slab = vt_ref.shape[3] // tk
    key_row = lax.broadcasted_iota(jnp.int32, (LANES, tq), 0)
    accs = []
    for hh in range(2):
        qt = _lane_concat(qt_ref, hh)
        mrow = _lane_concat(mrow_ref, hh)
        s0 = jnp.where(key_row < N_META, _dot(km_ref[hh], qt), -jnp.inf)
        acc = _dot(vtm_ref[hh, :V_ROWS, :], jnp.exp2(s0 - mrow).astype(_BF))
        scores = lambda c: _dot(k_ref[hh, c * tk:(c + 1) * tk, :], qt)
        s_next = scores(0)
        for c in range(nk):
            s = s_next
            if c + 1 < nk:
                s_next = scores(c + 1)
            vt_c = vt_ref[hh, c // per_slab, :V_ROWS, (c % per_slab) * tk:(c % per_slab + 1) * tk]
            acc = acc + _dot(vt_c, jnp.exp2(s - mrow).astype(_BF))
        accs.append(acc)
    _attn_finish(accs, o_ref)


def _attn_online_kernel(qt_ref, k_ref, vt_ref, km_ref, vtm_ref, o_ref, s0_scr, s1_scr, m_scr, acc_scr, *, nk, tk):
    tq = o_ref.shape[0]
    key_row = lax.broadcasted_iota(jnp.int32, (LANES, tq), 0)
    for hh in range(2):
        s0 = jnp.where(key_row < N_META, _dot(km_ref[hh], _lane_concat(qt_ref, hh)), -jnp.inf)
        m0 = jnp.max(s0, axis=0, keepdims=True)
        m_scr[hh] = m0
        acc_scr[hh] = _dot(vtm_ref[hh, :V_ROWS, :], jnp.exp2(s0 - m0).astype(_BF))

    def scores(buf, c):
        off = pl.multiple_of(c * tk, tk)
        for hh in range(2):
            buf[hh] = _dot(k_ref[hh, pl.ds(off, tk), :], _lane_concat(qt_ref, hh))

    def accumulate(buf, c):
        for hh in range(2):
            s = buf[hh]
            m = m_scr[hh]
            m_new = jnp.maximum(m, jnp.max(s, axis=0, keepdims=True))
            m_scr[hh] = m_new
            p = jnp.exp2(s - m_new).astype(_BF)
            acc_scr[hh] = jnp.exp2(m - m_new) * acc_scr[hh] + _dot(vt_ref[hh, c, :V_ROWS, :], p)

    scores(s0_scr, 0)

    def body(t, _):
        scores(s1_scr, 2 * t + 1)
        accumulate(s0_scr, 2 * t)
        scores(s0_scr, 2 * t + 2)
        accumulate(s1_scr, 2 * t + 1)
        return 0

    lax.fori_loop(0, nk // 2 - 1, body, 0)
    scores(s1_scr, nk - 1)
    accumulate(s0_scr, nk - 2)
    accumulate(s1_scr, nk - 1)
    _attn_finish([acc_scr[0], acc_scr[1]], o_ref)


def _attn_call(qt, mrow, k, vt, km, vtm, *, bounded):
    bsz, _, nslab, _, slab = qt.shape
    seq = k.shape[2]
    tq = min(ATTN_TQ, seq)
    tk = min(ATTN_TK, slab)
    qs = tq // slab
    assert tq % slab == 0 and seq % tq == 0 and slab % tk == 0
    q_spec = lambda rows: pl.BlockSpec((None, 2, qs, rows, slab), lambda b, hp, i: (b, hp, i, 0, 0))
    kv_specs = [pl.BlockSpec((None, 2, seq, HEAD_PAD), lambda b, hp, i: (b, hp, 0, 0)),
                pl.BlockSpec((None, 2, nslab, HEAD_PAD, slab), lambda b, hp, i: (b, hp, 0, 0, 0)),
                pl.BlockSpec((2, LANES, HEAD_PAD), lambda b, hp, i: (hp, 0, 0)),
                pl.BlockSpec((2, HEAD_PAD, LANES), lambda b, hp, i: (hp, 0, 0))]
    if bounded:
        body = functools.partial(_attn_bounded_kernel, nk=seq // tk, tk=tk)
        in_specs = [q_spec(HEAD_PAD), q_spec(1)] + kv_specs
        args, scratch = (qt, mrow, k, vt, km, vtm), []
    else:
        assert nslab % 2 == 0 and nslab >= 4
        body = functools.partial(_attn_online_kernel, nk=nslab, tk=slab)
        in_specs = [q_spec(HEAD_PAD)] + kv_specs
        args = (qt, k, vt, km, vtm)
        scratch = [pltpu.VMEM((2, slab, tq), _F32), pltpu.VMEM((2, slab, tq), _F32),
                   pltpu.VMEM((2, 1, tq), _F32), pltpu.VMEM((2, V_ROWS, tq), _F32)]
    return pl.pallas_call(
        body,
        grid=(bsz, N_HEADS // 2, seq // tq),
        in_specs=in_specs,
        out_specs=pl.BlockSpec((None, tq, LANES), lambda b, hp, i: (b, i, hp)),
        out_shape=jax.ShapeDtypeStruct((bsz, seq, ATTN_WIDTH), _F32),
        scratch_shapes=scratch,
        compiler_params=pltpu.CompilerParams(
            dimension_semantics=("parallel", "parallel", "arbitrary"), vmem_limit_bytes=VMEM_LIMIT),
        name="attn_bounded" if bounded else "attn_online",
    )(*args)


def _cmul_add(ar, ai, xr, xi, sr, si):
    return ar * xr - ai * xi + sr, ar * xi + ai * xr + si


def _s5_kernel(a_ref, um_ref, pin_ref, pout_ref, lag_ref, w_ref, v_ref, t_ref, y_ref, sup_scr, ent_scr,
               *, nslab, bsz):
    rows = nslab * bsz
    half = STATE_W // 2
    a = jnp.concatenate([a_ref[pl.ds(i, rows, stride=SSM_GROUP), :] for i in range(SSM_GROUP)], axis=1)
    ap = _dot(a.astype(_BF), pin_ref[...]).astype(_BF)
    uc = [ap[:, c * CHUNK_W:(c + 1) * CHUNK_W] for c in range(SLAB_CHUNKS)]
    w = w_ref[...]
    s = [_dot(u, w) for u in uc]
    sr = [x[:, :half] for x in s]
    si = [x[:, half:] for x in s]
    t = t_ref[...]
    trow = lambda r: t[r:r + 1, :]
    sup_r = sup_i = None
    for c in range(SLAB_CHUNKS):
        cr, ci = trow(c), trow(SLAB_CHUNKS + c)
        pr = cr * sr[c] - ci * si[c]
        pi = cr * si[c] + ci * sr[c]
        sup_r = pr if sup_r is None else sup_r + pr
        sup_i = pi if sup_i is None else sup_i + pi
    sup_scr[:, :half] = sup_r
    sup_scr[:, half:] = sup_i

    lane = lax.broadcasted_iota(jnp.int32, (bsz, half), 1)
    fwd = lane < SSM_STATE
    sm = _dot(um_ref[...], w)
    xr = jnp.where(fwd, sm[:bsz, :half], 0.0)
    xi = jnp.where(fwd, sm[:bsz, half:], 0.0)
    a_slab_r, a_slab_i = trow(2 * SLAB_CHUNKS), trow(2 * SLAB_CHUNKS + 1)
    for j in range(nslab):
        rf = j * bsz
        rb = (nslab - 1 - j) * bsz
        ent_scr[rf:rf + bsz, 0:SSM_STATE] = xr[:, 0:SSM_STATE]
        ent_scr[rb:rb + bsz, SSM_STATE:half] = xr[:, SSM_STATE:half]
        ent_scr[rf:rf + bsz, half:half + SSM_STATE] = xi[:, 0:SSM_STATE]
        ent_scr[rb:rb + bsz, half + SSM_STATE:STATE_W] = xi[:, SSM_STATE:half]
        s_r = jnp.where(fwd, sup_scr[rf:rf + bsz, :half], sup_scr[rb:rb + bsz, :half])
        s_i = jnp.where(fwd, sup_scr[rf:rf + bsz, half:], sup_scr[rb:rb + bsz, half:])
        xr, xi = _cmul_add(a_slab_r, a_slab_i, xr, xi, s_r, s_i)

    ent = ent_scr[...]
    a_r, a_i = trow(2 * SLAB_CHUNKS + 2), trow(2 * SLAB_CHUNKS + 3)
    xf = [(ent[:, :half], ent[:, half:])]
    for c in range(1, SLAB_CHUNKS):
        xf.append(_cmul_add(a_r, a_i, xf[-1][0], xf[-1][1], sr[c - 1], si[c - 1]))
    xb = [(ent[:, :half], ent[:, half:])]
    for c in range(SLAB_CHUNKS - 2, -1, -1):
        xb.insert(0, _cmul_add(a_r, a_i, xb[0][0], xb[0][1], sr[c + 1], si[c + 1]))
    fwd_rows = lax.broadcasted_iota(jnp.int32, (rows, half), 1) < SSM_STATE
    m = jnp.concatenate(
        [pltpu.roll(jnp.broadcast_to(lag_ref[i:i + 1, :], (SSM_CHUNK, 2 * CHUNK_W)), 0, 1,
                    stride=SSM_GROUP, stride_axis=0)[:, :CHUNK_W] for i in range(SSM_GROUP)], axis=0).astype(_BF)
    v = v_ref[...]
    ys = []
    for c in range(SLAB_CHUNKS):
        xin = jnp.concatenate([jnp.where(fwd_rows, xf[c][0], xb[c][0]),
                               jnp.where(fwd_rows, xf[c][1], xb[c][1])], axis=1).astype(_BF)
        ys.append((_dot(uc[c], m) + _dot(xin, v)).astype(_BF))
    yp = _dot(jnp.concatenate(ys, axis=1), pout_ref[...])
    for o in range(SSM_GROUP):
        y_ref[pl.ds(o, rows, stride=SSM_GROUP), :] = yp[:, o * SLAB_T:(o + 1) * SLAB_T]


def _s5_call(a, um, pin, pout, lags, w_mat, v_mat, tab, *, nslab, bsz):
    n = nslab * bsz * SSM_GROUP
    rows = nslab * bsz
    g_spec = lambda *shape: pl.BlockSpec((None,) + shape, lambda g: (g,) + (0,) * len(shape))
    perm_spec = pl.BlockSpec((SLAB_W, SLAB_W), lambda g: (0, 0), pipeline_mode=pl.Buffered(1))
    return pl.pallas_call(
        functools.partial(_s5_kernel, nslab=nslab, bsz=bsz),
        grid=(SSM_GROUPS,),
        in_specs=[g_spec(n, SLAB_T), g_spec(SUBLANES, CHUNK_W), perm_spec, perm_spec,
                  g_spec(SSM_GROUP, 2 * CHUNK_W), g_spec(CHUNK_W, STATE_W), g_spec(STATE_W, CHUNK_W),
                  g_spec(2 * SUBLANES, STATE_W // 2)],
        out_specs=g_spec(n, SLAB_T),
        out_shape=jax.ShapeDtypeStruct((SSM_GROUPS, n, SLAB_T), _F32),
        scratch_shapes=[pltpu.VMEM((rows, STATE_W), _F32), pltpu.VMEM((rows, STATE_W), _F32)],
        compiler_params=pltpu.CompilerParams(
            dimension_semantics=("parallel",), vmem_limit_bytes=VMEM_LIMIT),
        name="s5",
    )(a, um, pin, pout, lags, w_mat, v_mat, tab)


def _s5_matrices(a_re, a_im, log_dt, b_re, b_im, c_re, c_im, d_skip):
    tc = SSM_CHUNK
    lam = lax.complex(jnp.minimum(a_re.astype(_F32), -1e-4), a_im.astype(_F32))
    dt = jnp.exp(log_dt.astype(_F32))[..., None]
    lam_dt = lam * dt
    lam_bar = jnp.exp(lam_dt)
    b_bar = ((lam_bar - 1.0) / lam)[..., None] * lax.complex(b_re.astype(_F32), b_im.astype(_F32))
    c_c = lax.complex(c_re.astype(_F32), c_im.astype(_F32))
    k_idx = jnp.arange(tc + 1, dtype=_F32)
    pw = jnp.exp(lam_dt[:, :, None, :] * k_idx[None, None, :, None])
    kern = jnp.real(jnp.einsum('dgop,dgkp,dgpi->dgkoi', c_c, pw[:, :, :tc], b_bar))
    d_g = d_skip.astype(_F32).reshape(SSM_GROUPS, SSM_GROUP)
    center = kern[0][:, :1] + kern[1][:, :1] + (jnp.eye(SSM_GROUP, dtype=_F32)[None] * d_g[:, :, None])[:, None]
    lags = jnp.concatenate([center, kern[0][:, 1:], jnp.zeros_like(center), kern[1][:, :0:-1]], axis=1)
    lags = lags.transpose(0, 3, 1, 2).reshape(SSM_GROUPS, SSM_GROUP, 2 * CHUNK_W)
    wf = b_bar[0].transpose(0, 2, 1)[:, :, None, :] * pw[0][:, tc - 1::-1][:, None, :, :]
    wb = b_bar[1].transpose(0, 2, 1)[:, :, None, :] * pw[1][:, :tc][:, None, :, :]
    w_mat = jnp.concatenate([jnp.real(wf), jnp.real(wb), jnp.imag(wf), jnp.imag(wb)], axis=-1)
    w_mat = w_mat.reshape(SSM_GROUPS, CHUNK_W, STATE_W)
    gf = pw[0][:, 1:tc + 1][:, :, None, :] * c_c[0][:, None, :, :]
    gb = pw[1][:, tc:0:-1][:, :, None, :] * c_c[1][:, None, :, :]
    v_mat = jnp.concatenate([jnp.real(gf), jnp.real(gb), -jnp.imag(gf), -jnp.imag(gb)], axis=-1)
    v_mat = v_mat.reshape(SSM_GROUPS, CHUNK_W, STATE_W).transpose(0, 2, 1)
    n_idx = jnp.arange(SLAB_CHUNKS + 1, dtype=_F32) * tc
    pc = jnp.exp(lam_dt[:, :, None, :] * n_idx[None, None, :, None])
    coef = jnp.concatenate([pc[0][:, SLAB_CHUNKS - 1::-1], pc[1][:, :SLAB_CHUNKS]], axis=-1)
    both = lambda n: jnp.concatenate([pc[0][:, n], pc[1][:, n]], axis=-1)[:, None, :]
    a_slab, a_chunk = both(SLAB_CHUNKS), both(1)
    tab = jnp.concatenate([jnp.real(coef), jnp.imag(coef), jnp.real(a_slab), jnp.imag(a_slab),
                           jnp.real(a_chunk), jnp.imag(a_chunk)], axis=1)
    tab = jnp.pad(tab, ((0, 0), (0, 2 * SUBLANES - tab.shape[1]), (0, 0)))
    return lags, w_mat, v_mat, tab


def _slab_permutations():
    r = lax.broadcasted_iota(jnp.int32, (SLAB_W, SLAB_W), 0)
    c = lax.broadcasted_iota(jnp.int32, (SLAB_W, SLAB_W), 1)
    ch, tok = r // SLAB_T, r % SLAB_T
    pin = c == (tok // SSM_CHUNK) * CHUNK_W + ch * SSM_CHUNK + tok % SSM_CHUNK
    ch, tok = c // SLAB_T, c % SLAB_T
    pout = r == (tok // SSM_CHUNK) * CHUNK_W + (tok % SSM_CHUNK) * SSM_GROUP + ch
    return pin.astype(_BF), pout.astype(_BF)


def _post_kernel(x_ref, attn_ref, y_ref, wglu_ref, gmix_ref, wout_ref, gpm_ref, gpre_ref,
                 wup_ref, wdn_ref, gpost_ref, o_ref):
    nslab = y_ref.shape[1]
    yt = jnp.concatenate([jnp.concatenate([y_ref[g, c] for c in range(nslab)], axis=1)
                          for g in range(SSM_GROUPS)], axis=0)
    y = yt.T
    gy = 0.5 * y * (1.0 + jnp.tanh(math.sqrt(2.0 / math.pi) * (y + 0.044715 * (y * y * y))))
    z = _dot(gy.astype(_BF), wglu_ref[...])
    ssm = z[:, :SSM_WIDTH] * (1.0 / (1.0 + jnp.exp(-z[:, SSM_WIDTH:])))
    gmix = gmix_ref[...]
    mix = jnp.concatenate([_rms(attn_ref[...], gmix[:, :ATTN_WIDTH]),
                           _rms(ssm, gmix[:, ATTN_WIDTH:])], axis=-1).astype(_BF)
    h1 = x_ref[...] + _rms(_dot(mix, wout_ref[...]), gpm_ref[...])
    hn = _rms(h1, gpre_ref[...]).astype(_BF)
    acc = None
    for c in range(D_FF // FF_TILE):
        up = _dot(hn, wup_ref[:, c * FF_TILE:(c + 1) * FF_TILE])
        up = jnp.maximum(up, 0.0)
        part = _dot((up * up).astype(_BF), wdn_ref[c * FF_TILE:(c + 1) * FF_TILE, :])
        acc = part if acc is None else acc + part
    o_ref[...] = h1 + _rms(acc, gpost_ref[...])


def _post_call(x, attn, y, wglu, gmix, wout, gpm, gpre, wup, wdn, gpost, *, tile):
    bsz, seq, _ = x.shape
    row_spec = lambda w: pl.BlockSpec((None, tile, w), lambda b, i: (b, i, 0))
    wspec = lambda shape: pl.BlockSpec(shape, lambda b, i: (0, 0), pipeline_mode=pl.Buffered(1))
    y_spec = pl.BlockSpec((SSM_GROUPS, tile // SLAB_T, SSM_GROUP, SLAB_T), lambda b, i: (0, i, b, 0))
    return pl.pallas_call(
        _post_kernel,
        grid=(bsz, seq // tile),
        in_specs=[row_spec(D_MODEL), row_spec(ATTN_WIDTH), y_spec,
                  wspec((SSM_WIDTH, 2 * SSM_WIDTH)), wspec((1, D_MODEL)), wspec((D_MODEL, D_MODEL)),
                  wspec((1, D_MODEL)), wspec((1, D_MODEL)), wspec((D_MODEL, D_FF)),
                  wspec((D_FF, D_MODEL)), wspec((1, D_MODEL))],
        out_specs=row_spec(D_MODEL),
        out_shape=jax.ShapeDtypeStruct((bsz, seq, D_MODEL), _F32),
        compiler_params=pltpu.CompilerParams(
            dimension_semantics=("parallel", "parallel"), vmem_limit_bytes=VMEM_LIMIT),
        name="post",
    )(x, attn, y, wglu, gmix, wout, gpm, gpre, wup, wdn, gpost)


def _rope_tables(pos, tile):
    half = QK_ROPE_DIM // 2
    inv = 1.0 / (ROPE_BASE ** (jnp.arange(0, QK_ROPE_DIM, 2, dtype=_F32) / QK_ROPE_DIM))
    ang = pos.astype(_F32)[:, None, :] * inv[None, :, None]
    bsz, seq = pos.shape
    rope = jnp.stack([jnp.cos(ang), jnp.sin(ang)], axis=1)
    return rope.reshape(bsz, 2, half, seq // tile, tile).transpose(0, 3, 1, 2, 4)


def _prep_weights(w_in, w_uq, w_ukv):
    scale = QK_HEAD_DIM ** -0.5 * math.log2(math.e)
    win = w_in[:, :OFF_KR].astype(_BF)
    wkr_t = w_in[:, OFF_KR:OFF_U].T.astype(_BF)
    wu_t = w_in[:, OFF_U:].T.astype(_BF)
    wq_t = (w_uq * scale).T.astype(_BF)
    wkv3 = w_ukv.reshape(KV_LORA_RANK, N_HEADS, QK_NOPE_DIM + V_HEAD_DIM)
    wk = jnp.concatenate([wkv3[..., :QK_NOPE_DIM],
                          jnp.zeros((KV_LORA_RANK, N_HEADS, HEAD_PAD - QK_NOPE_DIM), _F32)], axis=-1)
    wk = wk.reshape(KV_LORA_RANK, N_HEADS * HEAD_PAD).astype(_BF)
    wv_t = wkv3[..., QK_NOPE_DIM:].reshape(KV_LORA_RANK, N_HEADS * V_HEAD_DIM).T.astype(_BF)
    return win, wkr_t, wu_t, wq_t, wk, wv_t


def kernel(x, positions, meta_tokens, g_pre_mix, w_in, g_q_lat, w_uq, g_kv_lat, w_ukv,
           ssm_A_re, ssm_A_im, ssm_log_dt, ssm_B_re, ssm_B_im, ssm_C_re, ssm_C_im, ssm_D,
           w_glu, g_mix_out, w_out, g_post_mix, g_pre_mlp, w_mlp_up, w_mlp_down, g_post_mlp):
    bsz, seq, _ = x.shape
    assert seq % ROW_TILE == 0 and ROW_TILE % SLAB_T == 0 and bsz <= SUBLANES
    assert N_META <= SSM_CHUNK
    row = lambda g: g.reshape(1, -1).astype(_F32)

    win, wkr_t, wu_t, wq_t, wk, wv_t = _prep_weights(w_in[0], w_uq[0], w_ukv[0])
    weights = (row(g_pre_mix[0]), win, wkr_t, wu_t, row(g_q_lat[0]), wq_t, row(g_kv_lat[0]), wk, wv_t)
    rope = _rope_tables(positions.astype(jnp.int32) + N_META, ROW_TILE)
    qt, k, vt, u, q_norm, k_sq = _proj_call(x, rope, *weights, tile=ROW_TILE, meta=False)
    meta_x = jnp.pad(meta_tokens.astype(x.dtype), ((0, LANES - N_META), (0, 0)))[None]
    rope_m = _rope_tables(jnp.arange(LANES, dtype=jnp.int32)[None], LANES)
    k_m, vt_m, u_m = _proj_call(meta_x, rope_m, *weights, tile=LANES, meta=True)
    km, vtm, u_m = k_m[0], vt_m[0, :, 0], u_m[:, :N_META]
    k_m_sq = jnp.max(jnp.sum(jnp.square(k_m[0].astype(_F32)), axis=-1), axis=-1)
    k_max = jnp.sqrt(jnp.maximum(jnp.max(k_sq, axis=(2, 3, 4)), k_m_sq[None]))
    mrow = q_norm * (k_max * BOUND_SLACK)[:, :, None, None, None]
    attn = lax.cond(jnp.max(mrow) <= BOUND_LIMIT,
                    lambda: _attn_call(qt, mrow, k, vt, km, vtm, bounded=True),
                    lambda: _attn_call(qt, mrow, k, vt, km, vtm, bounded=False))

    um = u_m[0].astype(_BF).reshape(N_META, SSM_GROUPS, SSM_GROUP).transpose(1, 2, 0)
    um = jnp.pad(um, ((0, 0), (0, 0), (SSM_CHUNK - N_META, 0))).reshape(SSM_GROUPS, 1, CHUNK_W)
    um = jnp.broadcast_to(um, (SSM_GROUPS, SUBLANES, CHUNK_W))
    lags, w_mat, v_mat, tab = _s5_matrices(ssm_A_re[0], ssm_A_im[0], ssm_log_dt[0], ssm_B_re[0],
                                            ssm_B_im[0], ssm_C_re[0], ssm_C_im[0], ssm_D[0])
    pin, pout = _slab_permutations()
    nslab = seq // SLAB_T
    yg = _s5_call(u.reshape(SSM_GROUPS, nslab * bsz * SSM_GROUP, SLAB_T), um, pin, pout,
                  lags, w_mat.astype(_BF), v_mat.astype(_BF), tab, nslab=nslab, bsz=bsz)
    y = yg.reshape(SSM_GROUPS, nslab, bsz * SSM_GROUP, SLAB_T)

    return _post_call(x, attn, y, w_glu[0].astype(_BF), row(g_mix_out[0]), w_out[0].astype(_BF),
                      row(g_post_mix[0]), row(g_pre_mlp[0]), w_mlp_up[0].astype(_BF),
                      w_mlp_down[0].astype(_BF), row(g_post_mlp[0]), tile=ROW_TILE)
```

```python
import functools
import math

import jax
import jax.numpy as jnp
from jax import lax
from jax.experimental import pallas as pl
from jax.experimental.pallas import tpu as pltpu

D_MODEL = 1024
N_META = 16
ATTN_WIDTH = 512
SSM_WIDTH = 512
N_HEADS = 8
V_HEAD_DIM = 64
QK_NOPE_DIM = 64
QK_ROPE_DIM = 32
QK_HEAD_DIM = QK_NOPE_DIM + QK_ROPE_DIM
Q_LORA_RANK = 384
KV_LORA_RANK = 256
ROPE_BASE = 10000.0
SSM_GROUP = 16
SSM_GROUPS = 32
SSM_STATE = 64
D_FF = 4 * D_MODEL
EPS = 1e-6
OFF_KV = Q_LORA_RANK
OFF_KR = OFF_KV + KV_LORA_RANK
OFF_U = OFF_KR + QK_ROPE_DIM

LANES = 128
SUBLANES = 8
HEAD_PAD = LANES
ONES_LANE = V_HEAD_DIM
ATTN_TQ = 1024
ATTN_TK = 256
V_ROWS = HEAD_PAD
PW_Q = 0
PW_KV = PW_Q + Q_LORA_RANK
PW_END = PW_KV + KV_LORA_RANK

SSM_CHUNK = 32
CHUNK_W = SSM_CHUNK * SSM_GROUP
STATE_W = 4 * SSM_STATE
SLAB_T = LANES
SLAB_CHUNKS = SLAB_T // SSM_CHUNK
SLAB_W = SSM_GROUP * SLAB_T

ROW_TILE = 512
FF_TILE = 1024
POST_SLABS = 2
BOUND_SLACK = 1.0 + 2.0 ** -6
BOUND_LIMIT = 60.0
VMEM_LIMIT = 56 * 1024 * 1024

_BF = jnp.bfloat16
_F32 = jnp.float32


def _dot(a, b):
    return jnp.dot(a, b, preferred_element_type=_F32)


def _rms(x, g):
    return x * lax.rsqrt(jnp.mean(x * x, axis=-1, keepdims=True) + EPS) * g


_NT = (((1,), (1,)), ((), ()))


def _rotate(x1, x2, cos_t, sin_t):
    return x1 * cos_t - x2 * sin_t, x1 * sin_t + x2 * cos_t


def _proj_kernel(x_ref, rope_ref, *refs, meta):
    if meta:
        gpre_ref, win_ref, wkr_ref, wu_ref, gkv_ref, wk_ref, wv_ref, k_ref, v_ref, u_ref = refs
    else:
        (gpre_ref, win_ref, wkr_ref, wu_ref, gq_ref, wq_ref, gkv_ref, wk_ref, wv_ref,
         q_ref, k_ref, v_ref, u_ref, qn_ref, kmx_ref) = refs
    tile = x_ref.shape[0]
    half = QK_ROPE_DIM // 2
    cos_t, sin_t = rope_ref[0], rope_ref[1]
    xn = _rms(x_ref[...], gpre_ref[...]).astype(_BF)
    proj = _dot(xn, win_ref[...])
    kvn = _rms(proj[:, PW_KV:PW_END], gkv_ref[...]).astype(_BF)
    krt = lax.dot_general(wkr_ref[...], xn, _NT, preferred_element_type=_F32)
    r1, r2 = _rotate(krt[:half], krt[half:], cos_t, sin_t)
    kr = jnp.concatenate([jnp.zeros((QK_NOPE_DIM, tile), _F32), r1, r2,
                          jnp.zeros((HEAD_PAD - QK_HEAD_DIM, tile), _F32)], axis=0).T
    kk = _dot(kvn, wk_ref[...])
    vt = lax.dot_general(wv_ref[...], kvn, _NT, preferred_element_type=_F32)
    ones_tail = (lax.broadcasted_iota(jnp.int32, (HEAD_PAD - V_HEAD_DIM, tile), 0) == 0).astype(_F32)
    if meta:
        u_ref[...] = lax.dot_general(xn, wu_ref[...], _NT, preferred_element_type=_F32)
    else:
        ut = lax.dot_general(wu_ref[...], xn, _NT, preferred_element_type=_F32)
        for g in range(SSM_GROUPS):
            for c in range(tile // SLAB_T):
                u_ref[g, c] = ut[g * SSM_GROUP:(g + 1) * SSM_GROUP, c * SLAB_T:(c + 1) * SLAB_T]
        qn = _rms(proj[:, PW_Q:PW_KV], gq_ref[...]).astype(_BF)
        qt = lax.dot_general(wq_ref[...], qn, _NT, preferred_element_type=_F32)
        zero_rows = jnp.zeros((HEAD_PAD - QK_HEAD_DIM, tile), _F32)
    for h in range(N_HEADS):
        k_h = (kk[:, h * HEAD_PAD:(h + 1) * HEAD_PAD] + kr).astype(_BF)
        k_ref[h] = k_h
        v_ref[h] = jnp.concatenate([vt[h * V_HEAD_DIM:(h + 1) * V_HEAD_DIM], ones_tail], axis=0).astype(_BF)
        if not meta:
            blk = qt[h * QK_HEAD_DIM:(h + 1) * QK_HEAD_DIM]
            r1, r2 = _rotate(blk[QK_NOPE_DIM:QK_NOPE_DIM + half], blk[QK_NOPE_DIM + half:], cos_t, sin_t)
            qt_h = jnp.concatenate([blk[:QK_NOPE_DIM], r1, r2, zero_rows], axis=0).astype(_BF)
            q_ref[h] = qt_h
            qt_f = qt_h.astype(_F32)
            qn_ref[h] = jnp.sqrt(jnp.sum(qt_f * qt_f, axis=0, keepdims=True))
            k_f = k_h.astype(_F32)
            kmx_ref[h] = jnp.broadcast_to(jnp.max(jnp.sum(k_f * k_f, axis=1, keepdims=True), axis=0, keepdims=True),
                                          (1, LANES))


def _const_spec(shape):
    nd = len(shape)
    return pl.BlockSpec(shape, lambda *_: (0,) * nd)


def _proj_call(x, rope, gpre, win, wkr_t, wu_t, gq, wq_t, gkv, wk, wv_t, *, tile, meta):
    bsz, seq, _ = x.shape
    nt = seq // tile
    row_spec = lambda w: pl.BlockSpec((None, tile, w), lambda b, i: (b, i, 0))
    rope_spec = pl.BlockSpec((None, None, 2, QK_ROPE_DIM // 2, tile), lambda b, i: (b, i, 0, 0, 0))
    k_spec = pl.BlockSpec((None, N_HEADS, tile, HEAD_PAD), lambda b, i: (b, 0, i, 0))
    t_spec = pl.BlockSpec((None, N_HEADS, None, HEAD_PAD, tile), lambda b, i: (b, 0, i, 0, 0))
    k_shape = jax.ShapeDtypeStruct((bsz, N_HEADS, seq, HEAD_PAD), _BF)
    t_shape = jax.ShapeDtypeStruct((bsz, N_HEADS, nt, HEAD_PAD, tile), _BF)
    w_specs = lambda *ws: [_const_spec(w.shape) for w in ws]
    if meta:
        args = (x, rope, gpre, win, wkr_t, wu_t, gkv, wk, wv_t)
        in_specs = [row_spec(D_MODEL), rope_spec] + w_specs(*args[2:])
        out_specs = [k_spec, t_spec, row_spec(SSM_WIDTH)]
        out_shape = [k_shape, t_shape, jax.ShapeDtypeStruct((bsz, seq, SSM_WIDTH), _F32)]
    else:
        args = (x, rope, gpre, win, wkr_t, wu_t, gq, wq_t, gkv, wk, wv_t)
        in_specs = [row_spec(D_MODEL), rope_spec] + w_specs(*args[2:])
        norm_spec = lambda w: pl.BlockSpec((None, N_HEADS, None, 1, w), lambda b, i: (b, 0, i, 0, 0))
        u_spec = pl.BlockSpec((SSM_GROUPS, tile // SLAB_T, SSM_GROUP, SLAB_T), lambda b, i: (0, i, b, 0))
        out_specs = [t_spec, k_spec, t_spec, u_spec, norm_spec(tile), norm_spec(LANES)]
        out_shape = [t_shape, k_shape, t_shape,
                     jax.ShapeDtypeStruct((SSM_GROUPS, seq // SLAB_T, bsz * SSM_GROUP, SLAB_T), _F32),
                     jax.ShapeDtypeStruct((bsz, N_HEADS, nt, 1, tile), _F32),
                     jax.ShapeDtypeStruct((bsz, N_HEADS, nt, 1, LANES), _F32)]
    return pl.pallas_call(
        functools.partial(_proj_kernel, meta=meta),
        grid=(bsz, nt),
        in_specs=in_specs,
        out_specs=out_specs,
        out_shape=out_shape,
        compiler_params=pltpu.CompilerParams(
            dimension_semantics=("parallel", "parallel"), vmem_limit_bytes=VMEM_LIMIT),
        name="proj_meta" if meta else "proj",
    )(*args)


def _attn_finish(accs, o_ref):
    halves = [(acc * (1.0 / acc[ONES_LANE:ONES_LANE + 1, :]))[:V_HEAD_DIM] for acc in accs]
    o_ref[...] = jnp.concatenate(halves, axis=0).T


def _lane_concat(ref, hh):
    return jnp.concatenate([ref[hh, j] for j in range(ref.shape[1])], axis=1)


def _attn_bounded_kernel(qt_ref, mrow_ref, k_ref, vt_ref, km_ref, vtm_ref, o_ref, *, nk, tk):
    tq = o_ref.shape[0]
    per_slab = vt_ref.shape[3] // tk
    key_row = lax.broadcasted_iota(jnp.int32, (LANES, tq), 0)
    accs = []
    for hh in range(2):
        qt = _lane_concat(qt_ref, hh)
        mrow = _lane_concat(mrow_ref, hh)
        s0 = jnp.where(key_row < N_META, _dot(km_ref[hh], qt), -jnp.inf)
        acc = _dot(vtm_ref[hh, :V_ROWS, :], jnp.exp2(s0 - mrow).astype(_BF))
        scores = lambda c: _dot(k_ref[hh, c * tk:(c + 1) * tk, :], qt)
        s_next = scores(0)
        for c in range(nk):
            s = s_next
            if c + 1 < nk:
                s_next = scores(c + 1)
            vt_c = vt_ref[hh, c // per_slab, :V_ROWS, (c % per_slab) * tk:(c % per_slab + 1) * tk]
            acc = acc + _dot(vt_c, jnp.exp2(s - mrow).astype(_BF))
        accs.append(acc)
    _attn_finish(accs, o_ref)


def _attn_online_kernel(qt_ref, k_ref, vt_ref, km_ref, vtm_ref, o_ref, s0_scr, s1_scr, m_scr, acc_scr, *, nk, tk):
    tq = o_ref.shape[0]
    key_row = lax.broadcasted_iota(jnp.int32, (LANES, tq), 0)
    for hh in range(2):
        s0 = jnp.where(key_row < N_META, _dot(km_ref[hh], _lane_concat(qt_ref, hh)), -jnp.inf)
        m0 = jnp.max(s0, axis=0, keepdims=True)
        m_scr[hh] = m0
        acc_scr[hh] = _dot(vtm_ref[hh, :V_ROWS, :], jnp.exp2(s0 - m0).astype(_BF))

    def scores(buf, c):
        off = pl.multiple_of(c * tk, tk)
        for hh in range(2):
            buf[hh] = _dot(k_ref[hh, pl.ds(off, tk), :], _lane_concat(qt_ref, hh))

    def accumulate(buf, c):
        for hh in range(2):
            s = buf[hh]
            m = m_scr[hh]
            m_new = jnp.maximum(m, jnp.max(s, axis=0, keepdims=True))
            m_scr[hh] = m_new
            p = jnp.exp2(s - m_new).astype(_BF)
            acc_scr[hh] = jnp.exp2(m - m_new) * acc_scr[hh] + _dot(vt_ref[hh, c, :V_ROWS, :], p)

    scores(s0_scr, 0)

    def body(t, _):
        scores(s1_scr, 2 * t + 1)
        accumulate(s0_scr, 2 * t)
        scores(s0_scr, 2 * t + 2)
        accumulate(s1_scr, 2 * t + 1)
        return 0

    lax.fori_loop(0, nk // 2 - 1, body, 0)
    scores(s1_scr, nk - 1)
    accumulate(s0_scr, nk - 2)
    accumulate(s1_scr, nk - 1)
    _attn_finish([acc_scr[0], acc_scr[1]], o_ref)


def _attn_call(qt, mrow, k, vt, km, vtm, *, bounded):
    bsz, _, nslab, _, slab = qt.shape
    seq = k.shape[2]
    tq = min(ATTN_TQ, seq)
    tk = min(ATTN_TK, slab)
    qs = tq // slab
    assert tq % slab == 0 and seq % tq == 0 and slab % tk == 0
    q_spec = lambda rows: pl.BlockSpec((None, 2, qs, rows, slab), lambda b, hp, i: (b, hp, i, 0, 0))
    kv_specs = [pl.BlockSpec((None, 2, seq, HEAD_PAD), lambda b, hp, i: (b, hp, 0, 0)),
                pl.BlockSpec((None, 2, nslab, HEAD_PAD, slab), lambda b, hp, i: (b, hp, 0, 0, 0)),
                pl.BlockSpec((2, LANES, HEAD_PAD), lambda b, hp, i: (hp, 0, 0)),
                pl.BlockSpec((2, HEAD_PAD, LANES), lambda b, hp, i: (hp, 0, 0))]
    if bounded:
        body = functools.partial(_attn_bounded_kernel, nk=seq // tk, tk=tk)
        in_specs = [q_spec(HEAD_PAD), q_spec(1)] + kv_specs
        args, scratch = (qt, mrow, k, vt, km, vtm), []
    else:
        assert nslab % 2 == 0 and nslab >= 4
        body = functools.partial(_attn_online_kernel, nk=nslab, tk=slab)
        in_specs = [q_spec(HEAD_PAD)] + kv_specs
        args = (qt, k, vt, km, vtm)
        scratch = [pltpu.VMEM((2, slab, tq), _F32), pltpu.VMEM((2, slab, tq), _F32),
                   pltpu.VMEM((2, 1, tq), _F32), pltpu.VMEM((2, V_ROWS, tq), _F32)]
    return pl.pallas_call(
        body,
        grid=(bsz, N_HEADS // 2, seq // tq),
        in_specs=in_specs,
        out_specs=pl.BlockSpec((None, tq, LANES), lambda b, hp, i: (b, i, hp)),
        out_shape=jax.ShapeDtypeStruct((bsz, seq, ATTN_WIDTH), _F32),
        scratch_shapes=scratch,
        compiler_params=pltpu.CompilerParams(
            dimension_semantics=("parallel", "parallel", "arbitrary"), vmem_limit_bytes=VMEM_LIMIT),
        name="attn_bounded" if bounded else "attn_online",
    )(*args)


def _cmul_add(ar, ai, xr, xi, sr, si):
    return ar * xr - ai * xi + sr, ar * xi + ai * xr + si


def _s5_kernel(a_ref, um_ref, pin_ref, pout_ref, lag_ref, w_ref, v_ref, t_ref, y_ref, sup_scr, ent_scr,
               *, nslab, bsz):
    rows = nslab * bsz
    half = STATE_W // 2
    a = jnp.concatenate([a_ref[pl.ds(i, rows, stride=SSM_GROUP), :] for i in range(SSM_GROUP)], axis=1)
    ap = _dot(a.astype(_BF), pin_ref[...]).astype(_BF)
    uc = [ap[:, c * CHUNK_W:(c + 1) * CHUNK_W] for c in range(SLAB_CHUNKS)]
    w = w_ref[...]
    s = [_dot(u, w) for u in uc]
    sr = [x[:, :half] for x in s]
    si = [x[:, half:] for x in s]
    t = t_ref[...]
    trow = lambda r: t[r:r + 1, :]
    sup_r = sup_i = None
    for c in range(SLAB_CHUNKS):
        cr, ci = trow(c), trow(SLAB_CHUNKS + c)
        pr = cr * sr[c] - ci * si[c]
        pi = cr * si[c] + ci * sr[c]
        sup_r = pr if sup_r is None else sup_r + pr
        sup_i = pi if sup_i is None else sup_i + pi
    sup_scr[:, :half] = sup_r
    sup_scr[:, half:] = sup_i

    lane = lax.broadcasted_iota(jnp.int32, (bsz, half), 1)
    fwd = lane < SSM_STATE
    sm = _dot(um_ref[...], w)
    xr = jnp.where(fwd, sm[:bsz, :half], 0.0)
    xi = jnp.where(fwd, sm[:bsz, half:], 0.0)
    a_slab_r, a_slab_i = trow(2 * SLAB_CHUNKS), trow(2 * SLAB_CHUNKS + 1)
    for j in range(nslab):
        rf = j * bsz
        rb = (nslab - 1 - j) * bsz
        ent_scr[rf:rf + bsz, 0:SSM_STATE] = xr[:, 0:SSM_STATE]
        ent_scr[rb:rb + bsz, SSM_STATE:half] = xr[:, SSM_STATE:half]
        ent_scr[rf:rf + bsz, half:half + SSM_STATE] = xi[:, 0:SSM_STATE]
        ent_scr[rb:rb + bsz, half + SSM_STATE:STATE_W] = xi[:, SSM_STATE:half]
        s_r = jnp.where(fwd, sup_scr[rf:rf + bsz, :half], sup_scr[rb:rb + bsz, :half])
        s_i = jnp.where(fwd, sup_scr[rf:rf + bsz, half:], sup_scr[rb:rb + bsz, half:])
        xr, xi = _cmul_add(a_slab_r, a_slab_i, xr, xi, s_r, s_i)

    ent = ent_scr[...]
    a_r, a_i = trow(2 * SLAB_CHUNKS + 2), trow(2 * SLAB_CHUNKS + 3)
    xf = [(ent[:, :half], ent[:, half:])]
    for c in range(1, SLAB_CHUNKS):
        xf.append(_cmul_add(a_r, a_i, xf[-1][0], xf[-1][1], sr[c - 1], si[c - 1]))
    xb = [(ent[:, :half], ent[:, half:])]
    for c in range(SLAB_CHUNKS - 2, -1, -1):
        xb.insert(0, _cmul_add(a_r, a_i, xb[0][0], xb[0][1], sr[c + 1], si[c + 1]))
    fwd_rows = lax.broadcasted_iota(jnp.int32, (rows, half), 1) < SSM_STATE
    m = jnp.concatenate(
        [pltpu.roll(jnp.broadcast_to(lag_ref[i:i + 1, :], (SSM_CHUNK, 2 * CHUNK_W)), 0, 1,
                    stride=SSM_GROUP, stride_axis=0)[:, :CHUNK_W] for i in range(SSM_GROUP)], axis=0).astype(_BF)
    v = v_ref[...]
    ys = []
    for c in range(SLAB_CHUNKS):
        xin = jnp.concatenate([jnp.where(fwd_rows, xf[c][0], xb[c][0]),
                               jnp.where(fwd_rows, xf[c][1], xb[c][1])], axis=1).astype(_BF)
        ys.append((_dot(uc[c], m) + _dot(xin, v)).astype(_BF))
    yp = _dot(jnp.concatenate(ys, axis=1), pout_ref[...])
    for o in range(SSM_GROUP):
        y_ref[pl.ds(o, rows, stride=SSM_GROUP), :] = yp[:, o * SLAB_T:(o + 1) * SLAB_T]


def _s5_call(a, um, pin, pout, lags, w_mat, v_mat, tab, *, nslab, bsz):
    n = nslab * bsz * SSM_GROUP
    rows = nslab * bsz
    g_spec = lambda *shape: pl.BlockSpec((None,) + shape, lambda g: (g,) + (0,) * len(shape))
    perm_spec = pl.BlockSpec((SLAB_W, SLAB_W), lambda g: (0, 0), pipeline_mode=pl.Buffered(1))
    return pl.pallas_call(
        functools.partial(_s5_kernel, nslab=nslab, bsz=bsz),
        grid=(SSM_GROUPS,),
        in_specs=[g_spec(n, SLAB_T), g_spec(SUBLANES, CHUNK_W), perm_spec, perm_spec,
                  g_spec(SSM_GROUP, 2 * CHUNK_W), g_spec(CHUNK_W, STATE_W), g_spec(STATE_W, CHUNK_W),
                  g_spec(2 * SUBLANES, STATE_W // 2)],
        out_specs=g_spec(n, SLAB_T),
        out_shape=jax.ShapeDtypeStruct((SSM_GROUPS, n, SLAB_T), _F32),
        scratch_shapes=[pltpu.VMEM((rows, STATE_W), _F32), pltpu.VMEM((rows, STATE_W), _F32)],
        compiler_params=pltpu.CompilerParams(
            dimension_semantics=("parallel",), vmem_limit_bytes=VMEM_LIMIT),
        name="s5",
    )(a, um, pin, pout, lags, w_mat, v_mat, tab)


def _s5_matrices(a_re, a_im, log_dt, b_re, b_im, c_re, c_im, d_skip):
    tc = SSM_CHUNK
    lam = lax.complex(jnp.minimum(a_re.astype(_F32), -1e-4), a_im.astype(_F32))
    dt = jnp.exp(log_dt.astype(_F32))[..., None]
    lam_dt = lam * dt
    lam_bar = jnp.exp(lam_dt)
    b_bar = ((lam_bar - 1.0) / lam)[..., None] * lax.complex(b_re.astype(_F32), b_im.astype(_F32))
    c_c = lax.complex(c_re.astype(_F32), c_im.astype(_F32))
    k_idx = jnp.arange(tc + 1, dtype=_F32)
    pw = jnp.exp(lam_dt[:, :, None, :] * k_idx[None, None, :, None])
    kern = jnp.real(jnp.einsum('dgop,dgkp,dgpi->dgkoi', c_c, pw[:, :, :tc], b_bar))
    d_g = d_skip.astype(_F32).reshape(SSM_GROUPS, SSM_GROUP)
    center = kern[0][:, :1] + kern[1][:, :1] + (jnp.eye(SSM_GROUP, dtype=_F32)[None] * d_g[:, :, None])[:, None]
    lags = jnp.concatenate([center, kern[0][:, 1:], jnp.zeros_like(center), kern[1][:, :0:-1]], axis=1)
    lags = lags.transpose(0, 3, 1, 2).reshape(SSM_GROUPS, SSM_GROUP, 2 * CHUNK_W)
    wf = b_bar[0].transpose(0, 2, 1)[:, :, None, :] * pw[0][:, tc - 1::-1][:, None, :, :]
    wb = b_bar[1].transpose(0, 2, 1)[:, :, None, :] * pw[1][:, :tc][:, None, :, :]
    w_mat = jnp.concatenate([jnp.real(wf), jnp.real(wb), jnp.imag(wf), jnp.imag(wb)], axis=-1)
    w_mat = w_mat.reshape(SSM_GROUPS, CHUNK_W, STATE_W)
    gf = pw[0][:, 1:tc + 1][:, :, None, :] * c_c[0][:, None, :, :]
    gb = pw[1][:, tc:0:-1][:, :, None, :] * c_c[1][:, None, :, :]
    v_mat = jnp.concatenate([jnp.real(gf), jnp.real(gb), -jnp.imag(gf), -jnp.imag(gb)], axis=-1)
    v_mat = v_mat.reshape(SSM_GROUPS, CHUNK_W, STATE_W).transpose(0, 2, 1)
    n_idx = jnp.arange(SLAB_CHUNKS + 1, dtype=_F32) * tc
    pc = jnp.exp(lam_dt[:, :, None, :] * n_idx[None, None, :, None])
    coef = jnp.concatenate([pc[0][:, SLAB_CHUNKS - 1::-1], pc[1][:, :SLAB_CHUNKS]], axis=-1)
    both = lambda n: jnp.concatenate([pc[0][:, n], pc[1][:, n]], axis=-1)[:, None, :]
    a_slab, a_chunk = both(SLAB_CHUNKS), both(1)
    tab = jnp.concatenate([jnp.real(coef), jnp.imag(coef), jnp.real(a_slab), jnp.imag(a_slab),
                           jnp.real(a_chunk), jnp.imag(a_chunk)], axis=1)
    tab = jnp.pad(tab, ((0, 0), (0, 2 * SUBLANES - tab.shape[1]), (0, 0)))
    return lags, w_mat, v_mat, tab


def _slab_permutations():
    r = lax.broadcasted_iota(jnp.int32, (SLAB_W, SLAB_W), 0)
    c = lax.broadcasted_iota(jnp.int32, (SLAB_W, SLAB_W), 1)
    ch, tok = r // SLAB_T, r % SLAB_T
    pin = c == (tok // SSM_CHUNK) * CHUNK_W + ch * SSM_CHUNK + tok % SSM_CHUNK
    ch, tok = c // SLAB_T, c % SLAB_T
    pout = r == (tok // SSM_CHUNK) * CHUNK_W + (tok % SSM_CHUNK) * SSM_GROUP + ch
    return pin.astype(_BF), pout.astype(_BF)


def _post_kernel(x_ref, attn_ref, y_ref, wglu_ref, gmix_ref, wout_ref, gpm_ref, gpre_ref,
                 wup_ref, wdn_ref, gpost_ref, o_ref):
    nslab = y_ref.shape[1]
    gmix = gmix_ref[...]
    per = min(POST_SLABS, nslab)
    blocks = range(nslab // per)
    rows = [pl.ds(r * per * SLAB_T, per * SLAB_T) for r in blocks]
    gy = []
    for r in blocks:
        yt = jnp.concatenate([jnp.concatenate([y_ref[g, r * per + c] for c in range(per)], axis=1)
                              for g in range(SSM_GROUPS)], axis=0)
        y = yt.T
        gy.append((0.5 * y * (1.0 + jnp.tanh(math.sqrt(2.0 / math.pi) * (y + 0.044715 * (y * y * y))))).astype(_BF))
    z = [_dot(gy[r], wglu_ref[...]) for r in blocks]
    mix = []
    for r in blocks:
        ssm = z[r][:, :SSM_WIDTH] * (1.0 / (1.0 + jnp.exp(-z[r][:, SSM_WIDTH:])))
        mix.append(jnp.concatenate([_rms(attn_ref[rows[r], :], gmix[:, :ATTN_WIDTH]),
                                    _rms(ssm, gmix[:, ATTN_WIDTH:])], axis=-1).astype(_BF))
    mixed = [_dot(mix[r], wout_ref[...]) for r in blocks]
    h1 = [x_ref[rows[r], :] + _rms(mixed[r], gpm_ref[...]) for r in blocks]
    hn = [_rms(h1[r], gpre_ref[...]).astype(_BF) for r in blocks]
    acc = [None for _ in blocks]
    for c in range(D_FF // FF_TILE):
        for r in blocks:
            up = jnp.maximum(_dot(hn[r], wup_ref[:, c * FF_TILE:(c + 1) * FF_TILE]), 0.0)
            part = _dot((up * up).astype(_BF), wdn_ref[c * FF_TILE:(c + 1) * FF_TILE, :])
            acc[r] = part if acc[r] is None else acc[r] + part
    for r in blocks:
        o_ref[rows[r], :] = h1[r] + _rms(acc[r], gpost_ref[...])


def _post_call(x, attn, y, wglu, gmix, wout, gpm, gpre, wup, wdn, gpost, *, tile):
    bsz, seq, _ = x.shape
    row_spec = lambda w: pl.BlockSpec((None, tile, w), lambda b, i: (b, i, 0))
    wspec = lambda shape: pl.BlockSpec(shape, lambda b, i: (0, 0), pipeline_mode=pl.Buffered(1))
    y_spec = pl.BlockSpec((SSM_GROUPS, tile // SLAB_T, SSM_GROUP, SLAB_T), lambda b, i: (0, i, b, 0))
    return pl.pallas_call(
        _post_kernel,
        grid=(bsz, seq // tile),
        in_specs=[row_spec(D_MODEL), row_spec(ATTN_WIDTH), y_spec,
                  wspec((SSM_WIDTH, 2 * SSM_WIDTH)), wspec((1, D_MODEL)), wspec((D_MODEL, D_MODEL)),
                  wspec((1, D_MODEL)), wspec((1, D_MODEL)), wspec((D_MODEL, D_FF)),
                  wspec((D_FF, D_MODEL)), wspec((1, D_MODEL))],
        out_specs=row_spec(D_MODEL),
        out_shape=jax.ShapeDtypeStruct((bsz, seq, D_MODEL), _F32),
        compiler_params=pltpu.CompilerParams(
            dimension_semantics=("parallel", "parallel"), vmem_limit_bytes=VMEM_LIMIT),
        name="post",
    )(x, attn, y, wglu, gmix, wout, gpm, gpre, wup, wdn, gpost)


def _rope_tables(pos, tile):
    half = QK_ROPE_DIM // 2
    inv = 1.0 / (ROPE_BASE ** (jnp.arange(0, QK_ROPE_DIM, 2, dtype=_F32) / QK_ROPE_DIM))
    ang = pos.astype(_F32)[:, None, :] * inv[None, :, None]
    bsz, seq = pos.shape
    rope = jnp.stack([jnp.cos(ang), jnp.sin(ang)], axis=1)
    return rope.reshape(bsz, 2, half, seq // tile, tile).transpose(0, 3, 1, 2, 4)


def _prep_weights(w_in, w_uq, w_ukv):
    scale = QK_HEAD_DIM ** -0.5 * math.log2(math.e)
    win = w_in[:, :OFF_KR].astype(_BF)
    wkr_t = w_in[:, OFF_KR:OFF_U].T.astype(_BF)
    wu_t = w_in[:, OFF_U:].T.astype(_BF)
    wq_t = (w_uq * scale).T.astype(_BF)
    wkv3 = w_ukv.reshape(KV_LORA_RANK, N_HEADS, QK_NOPE_DIM + V_HEAD_DIM)
    wk = jnp.concatenate([wkv3[..., :QK_NOPE_DIM],
                          jnp.zeros((KV_LORA_RANK, N_HEADS, HEAD_PAD - QK_NOPE_DIM), _F32)], axis=-1)
    wk = wk.reshape(KV_LORA_RANK, N_HEADS * HEAD_PAD).astype(_BF)
    wv_t = wkv3[..., QK_NOPE_DIM:].reshape(KV_LORA_RANK, N_HEADS * V_HEAD_DIM).T.astype(_BF)
    return win, wkr_t, wu_t, wq_t, wk, wv_t


def kernel(x, positions, meta_tokens, g_pre_mix, w_in, g_q_lat, w_uq, g_kv_lat, w_ukv,
           ssm_A_re, ssm_A_im, ssm_log_dt, ssm_B_re, ssm_B_im, ssm_C_re, ssm_C_im, ssm_D,
           w_glu, g_mix_out, w_out, g_post_mix, g_pre_mlp, w_mlp_up, w_mlp_down, g_post_mlp):
    bsz, seq, _ = x.shape
    assert seq % ROW_TILE == 0 and ROW_TILE % SLAB_T == 0 and bsz <= SUBLANES
    assert N_META <= SSM_CHUNK
    row = lambda g: g.reshape(1, -1).astype(_F32)

    win, wkr_t, wu_t, wq_t, wk, wv_t = _prep_weights(w_in[0], w_uq[0], w_ukv[0])
    weights = (row(g_pre_mix[0]), win, wkr_t, wu_t, row(g_q_lat[0]), wq_t, row(g_kv_lat[0]), wk, wv_t)
    rope = _rope_tables(positions.astype(jnp.int32) + N_META, ROW_TILE)
    qt, k, vt, u, q_norm, k_sq = _proj_call(x, rope, *weights, tile=ROW_TILE, meta=False)
    meta_x = jnp.pad(meta_tokens.astype(x.dtype), ((0, LANES - N_META), (0, 0)))[None]
    rope_m = _rope_tables(jnp.arange(LANES, dtype=jnp.int32)[None], LANES)
    k_m, vt_m, u_m = _proj_call(meta_x, rope_m, *weights, tile=LANES, meta=True)
    km, vtm, u_m = k_m[0], vt_m[0, :, 0], u_m[:, :N_META]
    k_m_sq = jnp.max(jnp.sum(jnp.square(k_m[0].astype(_F32)), axis=-1), axis=-1)
    k_max = jnp.sqrt(jnp.maximum(jnp.max(k_sq, axis=(2, 3, 4)), k_m_sq[None]))
    mrow = q_norm * (k_max * BOUND_SLACK)[:, :, None, None, None]
    attn = lax.cond(jnp.max(mrow) <= BOUND_LIMIT,
                    lambda: _attn_call(qt, mrow, k, vt, km, vtm, bounded=True),
                    lambda: _attn_call(qt, mrow, k, vt, km, vtm, bounded=False))

    um = u_m[0].astype(_BF).reshape(N_META, SSM_GROUPS, SSM_GROUP).transpose(1, 2, 0)
    um = jnp.pad(um, ((0, 0), (0, 0), (SSM_CHUNK - N_META, 0))).reshape(SSM_GROUPS, 1, CHUNK_W)
    um = jnp.broadcast_to(um, (SSM_GROUPS, SUBLANES, CHUNK_W))
    lags, w_mat, v_mat, tab = _s5_matrices(ssm_A_re[0], ssm_A_im[0], ssm_log_dt[0], ssm_B_re[0],
                                            ssm_B_im[0], ssm_C_re[0], ssm_C_im[0], ssm_D[0])
    pin, pout = _slab_permutations()
    nslab = seq // SLAB_T
    yg = _s5_call(u.reshape(SSM_GROUPS, nslab * bsz * SSM_GROUP, SLAB_T), um, pin, pout,
                  lags, w_mat.astype(_BF), v_mat.astype(_BF), tab, nslab=nslab, bsz=bsz)
    y = yg.reshape(SSM_GROUPS, nslab, bsz * SSM_GROUP, SLAB_T)

    return _post_call(x, attn, y, w_glu[0].astype(_BF), row(g_mix_out[0]), w_out[0].astype(_BF),
                      row(g_post_mix[0]), row(g_pre_mlp[0]), w_mlp_up[0].astype(_BF),
                      w_mlp_down[0].astype(_BF), row(g_post_mlp[0]), tile=ROW_TILE)
```

```python
import functools
import math

import jax
import jax.numpy as jnp
from jax import lax
from jax.experimental import pallas as pl
from jax.experimental.pallas import tpu as pltpu

D_MODEL = 1024
N_META = 16
ATTN_WIDTH = 512
SSM_WIDTH = 512
N_HEADS = 8
V_HEAD_DIM = 64
QK_NOPE_DIM = 64
QK_ROPE_DIM = 32
QK_HEAD_DIM = QK_NOPE_DIM + QK_ROPE_DIM
Q_LORA_RANK = 384
KV_LORA_RANK = 256
ROPE_BASE = 10000.0
SSM_GROUP = 16
SSM_GROUPS = 32
SSM_STATE = 64
D_FF = 4 * D_MODEL
EPS = 1e-6
OFF_KV = Q_LORA_RANK
OFF_KR = OFF_KV + KV_LORA_RANK
OFF_U = OFF_KR + QK_ROPE_DIM

LANES = 128
SUBLANES = 8
HEAD_PAD = LANES
ONES_LANE = V_HEAD_DIM
ATTN_TQ = 1024
ATTN_TK = 256
V_ROWS = HEAD_PAD
PW_Q = 0
PW_KV = PW_Q + Q_LORA_RANK
PW_END = PW_KV + KV_LORA_RANK

SSM_CHUNK = 32
CHUNK_W = SSM_CHUNK * SSM_GROUP
STATE_W = 4 * SSM_STATE
SLAB_T = LANES
SLAB_CHUNKS = SLAB_T // SSM_CHUNK
SLAB_W = SSM_GROUP * SLAB_T

ROW_TILE = 512
FF_TILE = 1024
POST_SLABS = 2
BOUND_SLACK = 1.0 + 2.0 ** -6
BOUND_LIMIT = 60.0
VMEM_LIMIT = 56 * 1024 * 1024

_BF = jnp.bfloat16
_F32 = jnp.float32


def _dot(a, b):
    return jnp.dot(a, b, preferred_element_type=_F32)


def _rms(x, g):
    return x * lax.rsqrt(jnp.mean(x * x, axis=-1, keepdims=True) + EPS) * g


_NT = (((1,), (1,)), ((), ()))


def _rotate(x1, x2, cos_t, sin_t):
    return x1 * cos_t - x2 * sin_t, x1 * sin_t + x2 * cos_t


def _proj_kernel(x_ref, rope_ref, *refs, meta):
    if meta:
        gpre_ref, win_ref, wkr_ref, wu_ref, gkv_ref, wk_ref, wv_ref, k_ref, v_ref, u_ref = refs
    else:
        (gpre_ref, win_ref, wkr_ref, wu_ref, gq_ref, wq_ref, gkv_ref, wk_ref, wv_ref,
         q_ref, k_ref, v_ref, u_ref, qn_ref, kmx_ref) = refs
    tile = x_ref.shape[0]
    half = QK_ROPE_DIM // 2
    cos_t, sin_t = rope_ref[0], rope_ref[1]
    xn = _rms(x_ref[...], gpre_ref[...]).astype(_BF)
    proj = _dot(xn, win_ref[...])
    kvn = _rms(proj[:, PW_KV:PW_END], gkv_ref[...]).astype(_BF)
    krt = lax.dot_general(wkr_ref[...], xn, _NT, preferred_element_type=_F32)
    r1, r2 = _rotate(krt[:half], krt[half:], cos_t, sin_t)
    kr = jnp.concatenate([jnp.zeros((QK_NOPE_DIM, tile), _F32), r1, r2,
                          jnp.zeros((HEAD_PAD - QK_HEAD_DIM, tile), _F32)], axis=0).T
    kk = _dot(kvn, wk_ref[...])
    vt = lax.dot_general(wv_ref[...], kvn, _NT, preferred_element_type=_F32)
    ones_tail = (lax.broadcasted_iota(jnp.int32, (HEAD_PAD - V_HEAD_DIM, tile), 0) == 0).astype(_F32)
    if meta:
        u_ref[...] = lax.dot_general(xn, wu_ref[...], _NT, preferred_element_type=_F32)
    else:
        ut = lax.dot_general(wu_ref[...], xn, _NT, preferred_element_type=_F32)
        for g in range(SSM_GROUPS):
            for c in range(tile // SLAB_T):
                u_ref[g, c] = ut[g * SSM_GROUP:(g + 1) * SSM_GROUP, c * SLAB_T:(c + 1) * SLAB_T]
        qn = _rms(proj[:, PW_Q:PW_KV], gq_ref[...]).astype(_BF)
        qt = lax.dot_general(wq_ref[...], qn, _NT, preferred_element_type=_F32)
        zero_rows = jnp.zeros((HEAD_PAD - QK_HEAD_DIM, tile), _F32)
    for h in range(N_HEADS):
        k_h = (kk[:, h * HEAD_PAD:(h + 1) * HEAD_PAD] + kr).astype(_BF)
        k_ref[h] = k_h
        v_ref[h] = jnp.concatenate([vt[h * V_HEAD_DIM:(h + 1) * V_HEAD_DIM], ones_tail], axis=0).astype(_BF)
        if not meta:
            blk = qt[h * QK_HEAD_DIM:(h + 1) * QK_HEAD_DIM]
            r1, r2 = _rotate(blk[QK_NOPE_DIM:QK_NOPE_DIM + half], blk[QK_NOPE_DIM + half:], cos_t, sin_t)
            qt_h = jnp.concatenate([blk[:QK_NOPE_DIM], r1, r2, zero_rows], axis=0).astype(_BF)
            q_ref[h] = qt_h
            qt_f = qt_h.astype(_F32)
            qn_ref[h] = jnp.sqrt(jnp.sum(qt_f * qt_f, axis=0, keepdims=True))
            k_f = k_h.astype(_F32)
            kmx_ref[h] = jnp.broadcast_to(jnp.max(jnp.sum(k_f * k_f, axis=1, keepdims=True), axis=0, keepdims=True),
                                          (1, LANES))


def _const_spec(shape):
    nd = len(shape)
    return pl.BlockSpec(shape, lambda *_: (0,) * nd)


def _proj_call(x, rope, gpre, win, wkr_t, wu_t, gq, wq_t, gkv, wk, wv_t, *, tile, meta):
    bsz, seq, _ = x.shape
    nt = seq // tile
    row_spec = lambda w: pl.BlockSpec((None, tile, w), lambda b, i: (b, i, 0))
    rope_spec = pl.BlockSpec((None, None, 2, QK_ROPE_DIM // 2, tile), lambda b, i: (b, i, 0, 0, 0))
    k_spec = pl.BlockSpec((None, N_HEADS, tile, HEAD_PAD), lambda b, i: (b, 0, i, 0))
    t_spec = pl.BlockSpec((None, N_HEADS, None, HEAD_PAD, tile), lambda b, i: (b, 0, i, 0, 0))
    k_shape = jax.ShapeDtypeStruct((bsz, N_HEADS, seq, HEAD_PAD), _BF)
    t_shape = jax.ShapeDtypeStruct((bsz, N_HEADS, nt, HEAD_PAD, tile), _BF)
    w_specs = lambda *ws: [_const_spec(w.shape) for w in ws]
    if meta:
        args = (x, rope, gpre, win, wkr_t, wu_t, gkv, wk, wv_t)
        in_specs = [row_spec(D_MODEL), rope_spec] + w_specs(*args[2:])
        out_specs = [k_spec, t_spec, row_spec(SSM_WIDTH)]
        out_shape = [k_shape, t_shape, jax.ShapeDtypeStruct((bsz, seq, SSM_WIDTH), _F32)]
    else:
        args = (x, rope, gpre, win, wkr_t, wu_t, gq, wq_t, gkv, wk, wv_t)
        in_specs = [row_spec(D_MODEL), rope_spec] + w_specs(*args[2:])
        norm_spec = lambda w: pl.BlockSpec((None, N_HEADS, None, 1, w), lambda b, i: (b, 0, i, 0, 0))
        u_spec = pl.BlockSpec((SSM_GROUPS, tile // SLAB_T, SSM_GROUP, SLAB_T), lambda b, i: (0, i, b, 0))
        out_specs = [t_spec, k_spec, t_spec, u_spec, norm_spec(tile), norm_spec(LANES)]
        out_shape = [t_shape, k_shape, t_shape,
                     jax.ShapeDtypeStruct((SSM_GROUPS, seq // SLAB_T, bsz * SSM_GROUP, SLAB_T), _F32),
                     jax.ShapeDtypeStruct((bsz, N_HEADS, nt, 1, tile), _F32),
                     jax.ShapeDtypeStruct((bsz, N_HEADS, nt, 1, LANES), _F32)]
    return pl.pallas_call(
        functools.partial(_proj_kernel, meta=meta),
        grid=(bsz, nt),
        in_specs=in_specs,
        out_specs=out_specs,
        out_shape=out_shape,
        compiler_params=pltpu.CompilerParams(
            dimension_semantics=("parallel", "parallel"), vmem_limit_bytes=VMEM_LIMIT),
        name="proj_meta" if meta else "proj",
    )(*args)


def _attn_finish(accs, o_ref):
    halves = [(acc * (1.0 / acc[ONES_LANE:ONES_LANE + 1, :]))[:V_HEAD_DIM] for acc in accs]
    o_ref[...] = jnp.concatenate(halves, axis=0).T


def _lane_concat(ref, hh):
    return jnp.concatenate([ref[hh, j] for j in range(ref.shape[1])], axis=1)


def _attn_bounded_kernel(qt_ref, mrow_ref, k_ref, vt_ref, km_ref, vtm_ref, o_ref, *, nk, tk):
    tq = o_ref.shape[0]
    per_slab = vt_ref.shape[3] // tk
    key_row = lax.broadcasted_iota(jnp.int32, (LANES, tq), 0)
    accs = []
    for hh in range(2):
        qt = _lane_concat(qt_ref, hh)
        mrow = _lane_concat(mrow_ref, hh)
        s0 = jnp.where(key_row < N_META, _dot(km_ref[hh], qt), -jnp.inf)
        acc = _dot(vtm_ref[hh, :V_ROWS, :], jnp.exp2(s0 - mrow).astype(_BF))
        scores = lambda c: _dot(k_ref[hh, c * tk:(c + 1) * tk, :], qt)
        s_next = scores(0)
        for c in range(nk):
            s = s_next
            if c + 1 < nk:
                s_next = scores(c + 1)
            vt_c = vt_ref[hh, c // per_slab, :V_ROWS, (c % per_slab) * tk:(c % per_slab + 1) * tk]
            acc = acc + _dot(vt_c, jnp.exp2(s - mrow).astype(_BF))
        accs.append(acc)
    _attn_finish(accs, o_ref)


def _attn_online_kernel(qt_ref, k_ref, vt_ref, km_ref, vtm_ref, o_ref, s0_scr, s1_scr, m_scr, acc_scr, *, nk, tk):
    tq = o_ref.shape[0]
    key_row = lax.broadcasted_iota(jnp.int32, (LANES, tq), 0)
    for hh in range(2):
        s0 = jnp.where(key_row < N_META, _dot(km_ref[hh], _lane_concat(qt_ref, hh)), -jnp.inf)
        m0 = jnp.max(s0, axis=0, keepdims=True)
        m_scr[hh] = m0
        acc_scr[hh] = _dot(vtm_ref[hh, :V_ROWS, :], jnp.exp2(s0 - m0).astype(_BF))

    def scores(buf, c):
        off = pl.multiple_of(c * tk, tk)
        for hh in range(2):
            buf[hh] = _dot(k_ref[hh, pl.ds(off, tk), :], _lane_concat(qt_ref, hh))

    def accumulate(buf, c):
        for hh in range(2):
            s = buf[hh]
            m = m_scr[hh]
            m_new = jnp.maximum(m, jnp.max(s, axis=0, keepdims=True))
            m_scr[hh] = m_new
            p = jnp.exp2(s - m_new).astype(_BF)
            acc_scr[hh] = jnp.exp2(m - m_new) * acc_scr[hh] + _dot(vt_ref[hh, c, :V_ROWS, :], p)

    scores(s0_scr, 0)

    def body(t, _):
        scores(s1_scr, 2 * t + 1)
        accumulate(s0_scr, 2 * t)
        scores(s0_scr, 2 * t + 2)
        accumulate(s1_scr, 2 * t + 1)
        return 0

    lax.fori_loop(0, nk // 2 - 1, body, 0)
    scores(s1_scr, nk - 1)
    accumulate(s0_scr, nk - 2)
    accumulate(s1_scr, nk - 1)
    _attn_finish([acc_scr[0], acc_scr[1]], o_ref)


def _attn_call(qt, mrow, k, vt, km, vtm, *, bounded):
    bsz, _, nslab, _, slab = qt.shape
    seq = k.shape[2]
    tq = min(ATTN_TQ, seq)
    tk = min(ATTN_TK, slab)
    qs = tq // slab
    assert tq % slab == 0 and seq % tq == 0 and slab % tk == 0
    q_spec = lambda rows: pl.BlockSpec((None, 2, qs, rows, slab), lambda b, hp, i: (b, hp, i, 0, 0))
    kv_specs = [pl.BlockSpec((None, 2, seq, HEAD_PAD), lambda b, hp, i: (b, hp, 0, 0)),
                pl.BlockSpec((None, 2, nslab, HEAD_PAD, slab), lambda b, hp, i: (b, hp, 0, 0, 0)),
                pl.BlockSpec((2, LANES, HEAD_PAD), lambda b, hp, i: (hp, 0, 0)),
                pl.BlockSpec((2, HEAD_PAD, LANES), lambda b, hp, i: (hp, 0, 0))]
    if bounded:
        body = functools.partial(_attn_bounded_kernel, nk=seq // tk, tk=tk)
        in_specs = [q_spec(HEAD_PAD), q_spec(1)] + kv_specs
        args, scratch = (qt, mrow, k, vt, km, vtm), []
    else:
        assert nslab % 2 == 0 and nslab >= 4
        body = functools.partial(_attn_online_kernel, nk=nslab, tk=slab)
        in_specs = [q_spec(HEAD_PAD)] + kv_specs
        args = (qt, k, vt, km, vtm)
        scratch = [pltpu.VMEM((2, slab, tq), _F32), pltpu.VMEM((2, slab, tq), _F32),
                   pltpu.VMEM((2, 1, tq), _F32), pltpu.VMEM((2, V_ROWS, tq), _F32)]
    return pl.pallas_call(
        body,
        grid=(bsz, N_HEADS // 2, seq // tq),
        in_specs=in_specs,
        out_specs=pl.BlockSpec((None, tq, LANES), lambda b, hp, i: (b, i, hp)),
        out_shape=jax.ShapeDtypeStruct((bsz, seq, ATTN_WIDTH), _F32),
        scratch_shapes=scratch,
        compiler_params=pltpu.CompilerParams(
            dimension_semantics=("parallel", "parallel", "arbitrary"), vmem_limit_bytes=VMEM_LIMIT),
        name="attn_bounded" if bounded else "attn_online",
    )(*args)


def _cmul_add(ar, ai, xr, xi, sr, si):
    return ar * xr - ai * xi + sr, ar * xi + ai * xr + si


def _s5_kernel(a_ref, um_ref, pout_ref, lag_ref, w_ref, v_ref, t_ref, y_ref, sup_scr, ent_scr,
               *, nslab, bsz):
    rows = nslab * bsz
    half = STATE_W // 2
    a = [a_ref[pl.ds(i, rows, stride=SSM_GROUP), :] for i in range(SSM_GROUP)]
    uc = [jnp.concatenate([x[:, c * SSM_CHUNK:(c + 1) * SSM_CHUNK] for x in a], axis=1).astype(_BF)
          for c in range(SLAB_CHUNKS)]
    w = w_ref[...]
    s = [_dot(u, w) for u in uc]
    sr = [x[:, :half] for x in s]
    si = [x[:, half:] for x in s]
    t = t_ref[...]
    trow = lambda r: t[r:r + 1, :]
    sup_r = sup_i = None
    for c in range(SLAB_CHUNKS):
        cr, ci = trow(c), trow(SLAB_CHUNKS + c)
        pr = cr * sr[c] - ci * si[c]
        pi = cr * si[c] + ci * sr[c]
        sup_r = pr if sup_r is None else sup_r + pr
        sup_i = pi if sup_i is None else sup_i + pi
    sup_scr[:, :half] = sup_r
    sup_scr[:, half:] = sup_i

    lane = lax.broadcasted_iota(jnp.int32, (bsz, half), 1)
    fwd = lane < SSM_STATE
    sm = _dot(um_ref[...], w)
    xr = jnp.where(fwd, sm[:bsz, :half], 0.0)
    xi = jnp.where(fwd, sm[:bsz, half:], 0.0)
    a_slab_r, a_slab_i = trow(2 * SLAB_CHUNKS), trow(2 * SLAB_CHUNKS + 1)
    for j in range(nslab):
        rf = j * bsz
        rb = (nslab - 1 - j) * bsz
        ent_scr[rf:rf + bsz, 0:SSM_STATE] = xr[:, 0:SSM_STATE]
        ent_scr[rb:rb + bsz, SSM_STATE:half] = xr[:, SSM_STATE:half]
        ent_scr[rf:rf + bsz, half:half + SSM_STATE] = xi[:, 0:SSM_STATE]
        ent_scr[rb:rb + bsz, half + SSM_STATE:STATE_W] = xi[:, SSM_STATE:half]
        s_r = jnp.where(fwd, sup_scr[rf:rf + bsz, :half], sup_scr[rb:rb + bsz, :half])
        s_i = jnp.where(fwd, sup_scr[rf:rf + bsz, half:], sup_scr[rb:rb + bsz, half:])
        xr, xi = _cmul_add(a_slab_r, a_slab_i, xr, xi, s_r, s_i)

    ent = ent_scr[...]
    a_r, a_i = trow(2 * SLAB_CHUNKS + 2), trow(2 * SLAB_CHUNKS + 3)
    xf = [(ent[:, :half], ent[:, half:])]
    for c in range(1, SLAB_CHUNKS):
        xf.append(_cmul_add(a_r, a_i, xf[-1][0], xf[-1][1], sr[c - 1], si[c - 1]))
    xb = [(ent[:, :half], ent[:, half:])]
    for c in range(SLAB_CHUNKS - 2, -1, -1):
        xb.insert(0, _cmul_add(a_r, a_i, xb[0][0], xb[0][1], sr[c + 1], si[c + 1]))
    fwd_rows = lax.broadcasted_iota(jnp.int32, (rows, half), 1) < SSM_STATE
    m = jnp.concatenate(
        [pltpu.roll(jnp.broadcast_to(lag_ref[i:i + 1, :], (SSM_CHUNK, 2 * CHUNK_W)), 0, 1,
                    stride=SSM_GROUP, stride_axis=0)[:, :CHUNK_W] for i in range(SSM_GROUP)], axis=0).astype(_BF)
    v = v_ref[...]
    ys = []
    for c in range(SLAB_CHUNKS):
        xin = jnp.concatenate([jnp.where(fwd_rows, xf[c][0], xb[c][0]),
                               jnp.where(fwd_rows, xf[c][1], xb[c][1])], axis=1).astype(_BF)
        ys.append((_dot(uc[c], m) + _dot(xin, v)).astype(_BF))
    yp = _dot(jnp.concatenate(ys, axis=1), pout_ref[...])
    for o in range(SSM_GROUP):
        y_ref[pl.ds(o, rows, stride=SSM_GROUP), :] = yp[:, o * SLAB_T:(o + 1) * SLAB_T]


def _s5_call(a, um, pout, lags, w_mat, v_mat, tab, *, nslab, bsz):
    n = nslab * bsz * SSM_GROUP
    rows = nslab * bsz
    g_spec = lambda *shape: pl.BlockSpec((None,) + shape, lambda g: (g,) + (0,) * len(shape))
    perm_spec = pl.BlockSpec((SLAB_W, SLAB_W), lambda g: (0, 0), pipeline_mode=pl.Buffered(1))
    return pl.pallas_call(
        functools.partial(_s5_kernel, nslab=nslab, bsz=bsz),
        grid=(SSM_GROUPS,),
        in_specs=[g_spec(n, SLAB_T), g_spec(SUBLANES, CHUNK_W), perm_spec,
                  g_spec(SSM_GROUP, 2 * CHUNK_W), g_spec(CHUNK_W, STATE_W), g_spec(STATE_W, CHUNK_W),
                  g_spec(2 * SUBLANES, STATE_W // 2)],
        out_specs=g_spec(n, SLAB_T),
        out_shape=jax.ShapeDtypeStruct((SSM_GROUPS, n, SLAB_T), _F32),
        scratch_shapes=[pltpu.VMEM((rows, STATE_W), _F32), pltpu.VMEM((rows, STATE_W), _F32)],
        compiler_params=pltpu.CompilerParams(
            dimension_semantics=("parallel",), vmem_limit_bytes=VMEM_LIMIT),
        name="s5",
    )(a, um, pout, lags, w_mat, v_mat, tab)


def _s5_matrices(a_re, a_im, log_dt, b_re, b_im, c_re, c_im, d_skip):
    tc = SSM_CHUNK
    lam = lax.complex(jnp.minimum(a_re.astype(_F32), -1e-4), a_im.astype(_F32))
    dt = jnp.exp(log_dt.astype(_F32))[..., None]
    lam_dt = lam * dt
    lam_bar = jnp.exp(lam_dt)
    b_bar = ((lam_bar - 1.0) / lam)[..., None] * lax.complex(b_re.astype(_F32), b_im.astype(_F32))
    c_c = lax.complex(c_re.astype(_F32), c_im.astype(_F32))
    k_idx = jnp.arange(tc + 1, dtype=_F32)
    pw = jnp.exp(lam_dt[:, :, None, :] * k_idx[None, None, :, None])
    kern = jnp.real(jnp.einsum('dgop,dgkp,dgpi->dgkoi', c_c, pw[:, :, :tc], b_bar))
    d_g = d_skip.astype(_F32).reshape(SSM_GROUPS, SSM_GROUP)
    center = kern[0][:, :1] + kern[1][:, :1] + (jnp.eye(SSM_GROUP, dtype=_F32)[None] * d_g[:, :, None])[:, None]
    lags = jnp.concatenate([center, kern[0][:, 1:], jnp.zeros_like(center), kern[1][:, :0:-1]], axis=1)
    lags = lags.transpose(0, 3, 1, 2).reshape(SSM_GROUPS, SSM_GROUP, 2 * CHUNK_W)
    wf = b_bar[0].transpose(0, 2, 1)[:, :, None, :] * pw[0][:, tc - 1::-1][:, None, :, :]
    wb = b_bar[1].transpose(0, 2, 1)[:, :, None, :] * pw[1][:, :tc][:, None, :, :]
    w_mat = jnp.concatenate([jnp.real(wf), jnp.real(wb), jnp.imag(wf), jnp.imag(wb)], axis=-1)
    w_mat = w_mat.reshape(SSM_GROUPS, CHUNK_W, STATE_W)
    gf = pw[0][:, 1:tc + 1][:, :, None, :] * c_c[0][:, None, :, :]
    gb = pw[1][:, tc:0:-1][:, :, None, :] * c_c[1][:, None, :, :]
    v_mat = jnp.concatenate([jnp.real(gf), jnp.real(gb), -jnp.imag(gf), -jnp.imag(gb)], axis=-1)
    v_mat = v_mat.reshape(SSM_GROUPS, CHUNK_W, STATE_W).transpose(0, 2, 1)
    n_idx = jnp.arange(SLAB_CHUNKS + 1, dtype=_F32) * tc
    pc = jnp.exp(lam_dt[:, :, None, :] * n_idx[None, None, :, None])
    coef = jnp.concatenate([pc[0][:, SLAB_CHUNKS - 1::-1], pc[1][:, :SLAB_CHUNKS]], axis=-1)
    both = lambda n: jnp.concatenate([pc[0][:, n], pc[1][:, n]], axis=-1)[:, None, :]
    a_slab, a_chunk = both(SLAB_CHUNKS), both(1)
    tab = jnp.concatenate([jnp.real(coef), jnp.imag(coef), jnp.real(a_slab), jnp.imag(a_slab),
                           jnp.real(a_chunk), jnp.imag(a_chunk)], axis=1)
    tab = jnp.pad(tab, ((0, 0), (0, 2 * SUBLANES - tab.shape[1]), (0, 0)))
    return lags, w_mat, v_mat, tab


def _slab_permutation():
    r = lax.broadcasted_iota(jnp.int32, (SLAB_W, SLAB_W), 0)
    c = lax.broadcasted_iota(jnp.int32, (SLAB_W, SLAB_W), 1)
    ch, tok = c // SLAB_T, c % SLAB_T
    return (r == (tok // SSM_CHUNK) * CHUNK_W + (tok % SSM_CHUNK) * SSM_GROUP + ch).astype(_BF)


def _post_kernel(x_ref, attn_ref, y_ref, wglu_ref, gmix_ref, wout_ref, gpm_ref, gpre_ref,
                 wup_ref, wdn_ref, gpost_ref, o_ref):
    nslab = y_ref.shape[1]
    gmix = gmix_ref[...]
    per = min(POST_SLABS, nslab)
    blocks = range(nslab // per)
    rows = [pl.ds(r * per * SLAB_T, per * SLAB_T) for r in blocks]
    gy = []
    for r in blocks:
        yt = jnp.concatenate([jnp.concatenate([y_ref[g, r * per + c] for c in range(per)], axis=1)
                              for g in range(SSM_GROUPS)], axis=0)
        y = yt.T
        gy.append((0.5 * y * (1.0 + jnp.tanh(math.sqrt(2.0 / math.pi) * (y + 0.044715 * (y * y * y))))).astype(_BF))
    z = [_dot(gy[r], wglu_ref[...]) for r in blocks]
    mix = []
    for r in blocks:
        ssm = z[r][:, :SSM_WIDTH] * (1.0 / (1.0 + jnp.exp(-z[r][:, SSM_WIDTH:])))
        mix.append(jnp.concatenate([_rms(attn_ref[rows[r], :], gmix[:, :ATTN_WIDTH]),
                                    _rms(ssm, gmix[:, ATTN_WIDTH:])], axis=-1).astype(_BF))
    mixed = [_dot(mix[r], wout_ref[...]) for r in blocks]
    h1 = [x_ref[rows[r], :] + _rms(mixed[r], gpm_ref[...]) for r in blocks]
    hn = [_rms(h1[r], gpre_ref[...]).astype(_BF) for r in blocks]
    acc = [None for _ in blocks]
    for c in range(D_FF // FF_TILE):
        for r in blocks:
            up = jnp.maximum(_dot(hn[r], wup_ref[:, c * FF_TILE:(c + 1) * FF_TILE]), 0.0)
            part = _dot((up * up).astype(_BF), wdn_ref[c * FF_TILE:(c + 1) * FF_TILE, :])
            acc[r] = part if acc[r] is None else acc[r] + part
    for r in blocks:
        o_ref[rows[r], :] = h1[r] + _rms(acc[r], gpost_ref[...])


def _post_call(x, attn, y, wglu, gmix, wout, gpm, gpre, wup, wdn, gpost, *, tile):
    bsz, seq, _ = x.shape
    row_spec = lambda w: pl.BlockSpec((None, tile, w), lambda b, i: (b, i, 0))
    wspec = lambda shape: pl.BlockSpec(shape, lambda b, i: (0, 0), pipeline_mode=pl.Buffered(1))
    y_spec = pl.BlockSpec((SSM_GROUPS, tile // SLAB_T, SSM_GROUP, SLAB_T), lambda b, i: (0, i, b, 0))
    return pl.pallas_call(
        _post_kernel,
        grid=(bsz, seq // tile),
        in_specs=[row_spec(D_MODEL), row_spec(ATTN_WIDTH), y_spec,
                  wspec((SSM_WIDTH, 2 * SSM_WIDTH)), wspec((1, D_MODEL)), wspec((D_MODEL, D_MODEL)),
                  wspec((1, D_MODEL)), wspec((1, D_MODEL)), wspec((D_MODEL, D_FF)),
                  wspec((D_FF, D_MODEL)), wspec((1, D_MODEL))],
        out_specs=row_spec(D_MODEL),
        out_shape=jax.ShapeDtypeStruct((bsz, seq, D_MODEL), _F32),
        compiler_params=pltpu.CompilerParams(
            dimension_semantics=("parallel", "parallel"), vmem_limit_bytes=VMEM_LIMIT),
        name="post",
    )(x, attn, y, wglu, gmix, wout, gpm, gpre, wup, wdn, gpost)


def _rope_tables(pos, tile):
    half = QK_ROPE_DIM // 2
    inv = 1.0 / (ROPE_BASE ** (jnp.arange(0, QK_ROPE_DIM, 2, dtype=_F32) / QK_ROPE_DIM))
    ang = pos.astype(_F32)[:, None, :] * inv[None, :, None]
    bsz, seq = pos.shape
    rope = jnp.stack([jnp.cos(ang), jnp.sin(ang)], axis=1)
    return rope.reshape(bsz, 2, half, seq // tile, tile).transpose(0, 3, 1, 2, 4)


def _prep_weights(w_in, w_uq, w_ukv):
    scale = QK_HEAD_DIM ** -0.5 * math.log2(math.e)
    win = w_in[:, :OFF_KR].astype(_BF)
    wkr_t = w_in[:, OFF_KR:OFF_U].T.astype(_BF)
    wu_t = w_in[:, OFF_U:].T.astype(_BF)
    wq_t = (w_uq * scale).T.astype(_BF)
    wkv3 = w_ukv.reshape(KV_LORA_RANK, N_HEADS, QK_NOPE_DIM + V_HEAD_DIM)
    wk = jnp.concatenate([wkv3[..., :QK_NOPE_DIM],
                          jnp.zeros((KV_LORA_RANK, N_HEADS, HEAD_PAD - QK_NOPE_DIM), _F32)], axis=-1)
    wk = wk.reshape(KV_LORA_RANK, N_HEADS * HEAD_PAD).astype(_BF)
    wv_t = wkv3[..., QK_NOPE_DIM:].reshape(KV_LORA_RANK, N_HEADS * V_HEAD_DIM).T.astype(_BF)
    return win, wkr_t, wu_t, wq_t, wk, wv_t


def kernel(x, positions, meta_tokens, g_pre_mix, w_in, g_q_lat, w_uq, g_kv_lat, w_ukv,
           ssm_A_re, ssm_A_im, ssm_log_dt, ssm_B_re, ssm_B_im, ssm_C_re, ssm_C_im, ssm_D,
           w_glu, g_mix_out, w_out, g_post_mix, g_pre_mlp, w_mlp_up, w_mlp_down, g_post_mlp):
    bsz, seq, _ = x.shape
    assert seq % ROW_TILE == 0 and ROW_TILE % SLAB_T == 0 and bsz <= SUBLANES
    assert N_META <= SSM_CHUNK
    row = lambda g: g.reshape(1, -1).astype(_F32)

    win, wkr_t, wu_t, wq_t, wk, wv_t = _prep_weights(w_in[0], w_uq[0], w_ukv[0])
    weights = (row(g_pre_mix[0]), win, wkr_t, wu_t, row(g_q_lat[0]), wq_t, row(g_kv_lat[0]), wk, wv_t)
    rope = _rope_tables(positions.astype(jnp.int32) + N_META, ROW_TILE)
    qt, k, vt, u, q_norm, k_sq = _proj_call(x, rope, *weights, tile=ROW_TILE, meta=False)
    meta_x = jnp.pad(meta_tokens.astype(x.dtype), ((0, LANES - N_META), (0, 0)))[None]
    rope_m = _rope_tables(jnp.arange(LANES, dtype=jnp.int32)[None], LANES)
    k_m, vt_m, u_m = _proj_call(meta_x, rope_m, *weights, tile=LANES, meta=True)
    km, vtm, u_m = k_m[0], vt_m[0, :, 0], u_m[:, :N_META]
    k_m_sq = jnp.max(jnp.sum(jnp.square(k_m[0].astype(_F32)), axis=-1), axis=-1)
    k_max = jnp.sqrt(jnp.maximum(jnp.max(k_sq, axis=(2, 3, 4)), k_m_sq[None]))
    mrow = q_norm * (k_max * BOUND_SLACK)[:, :, None, None, None]
    attn = lax.cond(jnp.max(mrow) <= BOUND_LIMIT,
                    lambda: _attn_call(qt, mrow, k, vt, km, vtm, bounded=True),
                    lambda: _attn_call(qt, mrow, k, vt, km, vtm, bounded=False))

    um = u_m[0].astype(_BF).reshape(N_META, SSM_GROUPS, SSM_GROUP).transpose(1, 2, 0)
    um = jnp.pad(um, ((0, 0), (0, 0), (SSM_CHUNK - N_META, 0))).reshape(SSM_GROUPS, 1, CHUNK_W)
    um = jnp.broadcast_to(um, (SSM_GROUPS, SUBLANES, CHUNK_W))
    lags, w_mat, v_mat, tab = _s5_matrices(ssm_A_re[0], ssm_A_im[0], ssm_log_dt[0], ssm_B_re[0],
                                            ssm_B_im[0], ssm_C_re[0], ssm_C_im[0], ssm_D[0])
    pout = _slab_permutation()
    nslab = seq // SLAB_T
    yg = _s5_call(u.reshape(SSM_GROUPS, nslab * bsz * SSM_GROUP, SLAB_T), um, pout,
                  lags, w_mat.astype(_BF), v_mat.astype(_BF), tab, nslab=nslab, bsz=bsz)
    y = yg.reshape(SSM_GROUPS, nslab, bsz * SSM_GROUP, SLAB_T)

    return _post_call(x, attn, y, w_glu[0].astype(_BF), row(g_mix_out[0]), w_out[0].astype(_BF),
                      row(g_post_mix[0]), row(g_pre_mlp[0]), w_mlp_up[0].astype(_BF),
                      w_mlp_down[0].astype(_BF), row(g_post_mlp[0]), tile=ROW_TILE)
```

```python
import functools
import math

import jax
import jax.numpy as jnp
from jax import lax
from jax.experimental import pallas as pl
from jax.experimental.pallas import tpu as pltpu

D_MODEL = 1024
N_META = 16
ATTN_WIDTH = 512
SSM_WIDTH = 512
N_HEADS = 8
V_HEAD_DIM = 64
QK_NOPE_DIM = 64
QK_ROPE_DIM = 32
QK_HEAD_DIM = QK_NOPE_DIM + QK_ROPE_DIM
Q_LORA_RANK = 384
KV_LORA_RANK = 256
ROPE_BASE = 10000.0
SSM_GROUP = 16
SSM_GROUPS = 32
SSM_STATE = 64
D_FF = 4 * D_MODEL
EPS = 1e-6
OFF_KV = Q_LORA_RANK
OFF_KR = OFF_KV + KV_LORA_RANK
OFF_U = OFF_KR + QK_ROPE_DIM

LANES = 128
SUBLANES = 8
HEAD_PAD = LANES
ONES_LANE = V_HEAD_DIM
ATTN_TQ = 1024
ATTN_TK = 256
V_ROWS = HEAD_PAD
PW_Q = 0
PW_KV = PW_Q + Q_LORA_RANK
PW_END = PW_KV + KV_LORA_RANK

SSM_CHUNK = 32
CHUNK_W = SSM_CHUNK * SSM_GROUP
STATE_W = 4 * SSM_STATE
SLAB_T = LANES
SLAB_CHUNKS = SLAB_T // SSM_CHUNK
SLAB_W = SSM_GROUP * SLAB_T

ROW_TILE = 512
FF_TILE = 1024
POST_SLABS = 2
BOUND_SLACK = 1.0 + 2.0 ** -6
BOUND_LIMIT = 60.0
VMEM_LIMIT = 56 * 1024 * 1024

_BF = jnp.bfloat16
_F32 = jnp.float32


def _dot(a, b):
    return jnp.dot(a, b, preferred_element_type=_F32)


def _rms(x, g):
    return x * lax.rsqrt(jnp.mean(x * x, axis=-1, keepdims=True) + EPS) * g


_NT = (((1,), (1,)), ((), ()))


def _rotate(x1, x2, cos_t, sin_t):
    return x1 * cos_t - x2 * sin_t, x1 * sin_t + x2 * cos_t


def _proj_kernel(x_ref, rope_ref, *refs, meta):
    if meta:
        gpre_ref, win_ref, wkr_ref, wu_ref, gkv_ref, wk_ref, wv_ref, k_ref, v_ref, u_ref = refs
    else:
        (gpre_ref, win_ref, wkr_ref, wu_ref, gq_ref, wq_ref, gkv_ref, wk_ref, wv_ref,
         q_ref, k_ref, v_ref, u_ref, qn_ref, kmx_ref) = refs
    tile = x_ref.shape[0]
    half = QK_ROPE_DIM // 2
    cos_t, sin_t = rope_ref[0], rope_ref[1]
    xn = _rms(x_ref[...], gpre_ref[...]).astype(_BF)
    proj = _dot(xn, win_ref[...])
    kvn = _rms(proj[:, PW_KV:PW_END], gkv_ref[...]).astype(_BF)
    krt = lax.dot_general(wkr_ref[...], xn, _NT, preferred_element_type=_F32)
    r1, r2 = _rotate(krt[:half], krt[half:], cos_t, sin_t)
    kr = jnp.concatenate([jnp.zeros((QK_NOPE_DIM, tile), _F32), r1, r2,
                          jnp.zeros((HEAD_PAD - QK_HEAD_DIM, tile), _F32)], axis=0).T
    kk = _dot(kvn, wk_ref[...])
    vt = lax.dot_general(wv_ref[...], kvn, _NT, preferred_element_type=_F32)
    ones_tail = (lax.broadcasted_iota(jnp.int32, (HEAD_PAD - V_HEAD_DIM, tile), 0) == 0).astype(_F32)
    if meta:
        u_ref[...] = lax.dot_general(xn, wu_ref[...], _NT, preferred_element_type=_F32)
    else:
        ut = lax.dot_general(wu_ref[...], xn, _NT, preferred_element_type=_F32)
        for g in range(SSM_GROUPS):
            for c in range(tile // SLAB_T):
                u_ref[g, c] = ut[g * SSM_GROUP:(g + 1) * SSM_GROUP, c * SLAB_T:(c + 1) * SLAB_T]
        qn = _rms(proj[:, PW_Q:PW_KV], gq_ref[...]).astype(_BF)
        qt = lax.dot_general(wq_ref[...], qn, _NT, preferred_element_type=_F32)
        zero_rows = jnp.zeros((HEAD_PAD - QK_HEAD_DIM, tile), _F32)
    for h in range(N_HEADS):
        k_h = (kk[:, h * HEAD_PAD:(h + 1) * HEAD_PAD] + kr).astype(_BF)
        k_ref[h] = k_h
        v_ref[h] = jnp.concatenate([vt[h * V_HEAD_DIM:(h + 1) * V_HEAD_DIM], ones_tail], axis=0).astype(_BF)
        if not meta:
            blk = qt[h * QK_HEAD_DIM:(h + 1) * QK_HEAD_DIM]
            r1, r2 = _rotate(blk[QK_NOPE_DIM:QK_NOPE_DIM + half], blk[QK_NOPE_DIM + half:], cos_t, sin_t)
            qt_h = jnp.concatenate([blk[:QK_NOPE_DIM], r1, r2, zero_rows], axis=0).astype(_BF)
            q_ref[h] = qt_h
            qt_f = qt_h.astype(_F32)
            qn_ref[h] = jnp.sqrt(jnp.sum(qt_f * qt_f, axis=0, keepdims=True))
            k_f = k_h.astype(_F32)
            kmx_ref[h] = jnp.broadcast_to(jnp.max(jnp.sum(k_f * k_f, axis=1, keepdims=True), axis=0, keepdims=True),
                                          (1, LANES))


def _const_spec(shape):
    nd = len(shape)
    return pl.BlockSpec(shape, lambda *_: (0,) * nd)


def _proj_call(x, rope, gpre, win, wkr_t, wu_t, gq, wq_t, gkv, wk, wv_t, *, tile, meta):
    bsz, seq, _ = x.shape
    nt = seq // tile
    row_spec = lambda w: pl.BlockSpec((None, tile, w), lambda b, i: (b, i, 0))
    rope_spec = pl.BlockSpec((None, None, 2, QK_ROPE_DIM // 2, tile), lambda b, i: (b, i, 0, 0, 0))
    k_spec = pl.BlockSpec((None, N_HEADS, tile, HEAD_PAD), lambda b, i: (b, 0, i, 0))
    t_spec = pl.BlockSpec((None, N_HEADS, None, HEAD_PAD, tile), lambda b, i: (b, 0, i, 0, 0))
    k_shape = jax.ShapeDtypeStruct((bsz, N_HEADS, seq, HEAD_PAD), _BF)
    t_shape = jax.ShapeDtypeStruct((bsz, N_HEADS, nt, HEAD_PAD, tile), _BF)
    w_specs = lambda *ws: [_const_spec(w.shape) for w in ws]
    if meta:
        args = (x, rope, gpre, win, wkr_t, wu_t, gkv, wk, wv_t)
        in_specs = [row_spec(D_MODEL), rope_spec] + w_specs(*args[2:])
        out_specs = [k_spec, t_spec, row_spec(SSM_WIDTH)]
        out_shape = [k_shape, t_shape, jax.ShapeDtypeStruct((bsz, seq, SSM_WIDTH), _F32)]
    else:
        args = (x, rope, gpre, win, wkr_t, wu_t, gq, wq_t, gkv, wk, wv_t)
        in_specs = [row_spec(D_MODEL), rope_spec] + w_specs(*args[2:])
        norm_spec = lambda w: pl.BlockSpec((None, N_HEADS, None, 1, w), lambda b, i: (b, 0, i, 0, 0))
        u_spec = pl.BlockSpec((SSM_GROUPS, tile // SLAB_T, SSM_GROUP, SLAB_T), lambda b, i: (0, i, b, 0))
        out_specs = [t_spec, k_spec, t_spec, u_spec, norm_spec(tile), norm_spec(LANES)]
        out_shape = [t_shape, k_shape, t_shape,
                     jax.ShapeDtypeStruct((SSM_GROUPS, seq // SLAB_T, bsz * SSM_GROUP, SLAB_T), _F32),
                     jax.ShapeDtypeStruct((bsz, N_HEADS, nt, 1, tile), _F32),
                     jax.ShapeDtypeStruct((bsz, N_HEADS, nt, 1, LANES), _F32)]
    return pl.pallas_call(
        functools.partial(_proj_kernel, meta=meta),
        grid=(bsz, nt),
        in_specs=in_specs,
        out_specs=out_specs,
        out_shape=out_shape,
        compiler_params=pltpu.CompilerParams(
            dimension_semantics=("parallel", "parallel"), vmem_limit_bytes=VMEM_LIMIT),
        name="proj_meta" if meta else "proj",
    )(*args)


def _attn_finish(accs, o_ref):
    halves = [(acc * (1.0 / acc[ONES_LANE:ONES_LANE + 1, :]))[:V_HEAD_DIM] for acc in accs]
    o_ref[...] = jnp.concatenate(halves, axis=0).T


def _lane_concat(ref, hh):
    return jnp.concatenate([ref[hh, j] for j in range(ref.shape[1])], axis=1)


def _attn_bounded_kernel(qt_ref, mrow_ref, k_ref, vt_ref, km_ref, vtm_ref, o_ref, *, nk, tk):
    tq = o_ref.shape[0]
    per_slab = vt_ref.shape[3] // tk
    key_row = lax.broadcasted_iota(jnp.int32, (LANES, tq), 0)
    accs = []
    for hh in range(2):
        qt = _lane_concat(qt_ref, hh)
        mrow = _lane_concat(mrow_ref, hh)
        s0 = jnp.where(key_row < N_META, _dot(km_ref[hh], qt), -jnp.inf)
        acc = _dot(vtm_ref[hh, :V_ROWS, :], jnp.exp2(s0 - mrow).astype(_BF))
        scores = lambda c: _dot(k_ref[hh, c * tk:(c + 1) * tk, :], qt)
        s_next = scores(0)
        for c in range(nk):
            s = s_next
            if c + 1 < nk:
                s_next = scores(c + 1)
            vt_c = vt_ref[hh, c // per_slab, :V_ROWS, (c % per_slab) * tk:(c % per_slab + 1) * tk]
            acc = acc + _dot(vt_c, jnp.exp2(s - mrow).astype(_BF))
        accs.append(acc)
    _attn_finish(accs, o_ref)


def _attn_online_kernel(qt_ref, k_ref, vt_ref, km_ref, vtm_ref, o_ref, s0_scr, s1_scr, m_scr, acc_scr, *, nk, tk):
    tq = o_ref.shape[0]
    key_row = lax.broadcasted_iota(jnp.int32, (LANES, tq), 0)
    for hh in range(2):
        s0 = jnp.where(key_row < N_META, _dot(km_ref[hh], _lane_concat(qt_ref, hh)), -jnp.inf)
        m0 = jnp.max(s0, axis=0, keepdims=True)
        m_scr[hh] = m0
        acc_scr[hh] = _dot(vtm_ref[hh, :V_ROWS, :], jnp.exp2(s0 - m0).astype(_BF))

    def scores(buf, c):
        off = pl.multiple_of(c * tk, tk)
        for hh in range(2):
            buf[hh] = _dot(k_ref[hh, pl.ds(off, tk), :], _lane_concat(qt_ref, hh))

    def accumulate(buf, c):
        for hh in range(2):
            s = buf[hh]
            m = m_scr[hh]
            m_new = jnp.maximum(m, jnp.max(s, axis=0, keepdims=True))
            m_scr[hh] = m_new
            p = jnp.exp2(s - m_new).astype(_BF)
            acc_scr[hh] = jnp.exp2(m - m_new) * acc_scr[hh] + _dot(vt_ref[hh, c, :V_ROWS, :], p)

    scores(s0_scr, 0)

    def body(t, _):
        scores(s1_scr, 2 * t + 1)
        accumulate(s0_scr, 2 * t)
        scores(s0_scr, 2 * t + 2)
        accumulate(s1_scr, 2 * t + 1)
        return 0

    lax.fori_loop(0, nk // 2 - 1, body, 0)
    scores(s1_scr, nk - 1)
    accumulate(s0_scr, nk - 2)
    accumulate(s1_scr, nk - 1)
    _attn_finish([acc_scr[0], acc_scr[1]], o_ref)


def _attn_call(qt, mrow, k, vt, km, vtm, *, bounded):
    bsz, _, nslab, _, slab = qt.shape
    seq = k.shape[2]
    tq = min(ATTN_TQ, seq)
    tk = min(ATTN_TK, slab)
    qs = tq // slab
    assert tq % slab == 0 and seq % tq == 0 and slab % tk == 0
    q_spec = lambda rows: pl.BlockSpec((None, 2, qs, rows, slab), lambda b, hp, i: (b, hp, i, 0, 0))
    kv_specs = [pl.BlockSpec((None, 2, seq, HEAD_PAD), lambda b, hp, i: (b, hp, 0, 0)),
                pl.BlockSpec((None, 2, nslab, HEAD_PAD, slab), lambda b, hp, i: (b, hp, 0, 0, 0)),
                pl.BlockSpec((2, LANES, HEAD_PAD), lambda b, hp, i: (hp, 0, 0)),
                pl.BlockSpec((2, HEAD_PAD, LANES), lambda b, hp, i: (hp, 0, 0))]
    if bounded:
        body = functools.partial(_attn_bounded_kernel, nk=seq // tk, tk=tk)
        in_specs = [q_spec(HEAD_PAD), q_spec(1)] + kv_specs
        args, scratch = (qt, mrow, k, vt, km, vtm), []
    else:
        assert nslab % 2 == 0 and nslab >= 4
        body = functools.partial(_attn_online_kernel, nk=nslab, tk=slab)
        in_specs = [q_spec(HEAD_PAD)] + kv_specs
        args = (qt, k, vt, km, vtm)
        scratch = [pltpu.VMEM((2, slab, tq), _F32), pltpu.VMEM((2, slab, tq), _F32),
                   pltpu.VMEM((2, 1, tq), _F32), pltpu.VMEM((2, V_ROWS, tq), _F32)]
    return pl.pallas_call(
        body,
        grid=(bsz, N_HEADS // 2, seq // tq),
        in_specs=in_specs,
        out_specs=pl.BlockSpec((None, tq, LANES), lambda b, hp, i: (b, i, hp)),
        out_shape=jax.ShapeDtypeStruct((bsz, seq, ATTN_WIDTH), _F32),
        scratch_shapes=scratch,
        compiler_params=pltpu.CompilerParams(
            dimension_semantics=("parallel", "parallel", "arbitrary"), vmem_limit_bytes=VMEM_LIMIT),
        name="attn_bounded" if bounded else "attn_online",
    )(*args)


def _cmul_add(ar, ai, xr, xi, sr, si):
    return ar * xr - ai * xi + sr, ar * xi + ai * xr + si


def _s5_kernel(a_ref, um_ref, q_ref, lag_ref, w_ref, v_ref, t_ref, y_ref, sup_scr, ent_scr,
               *, nslab, bsz):
    rows = nslab * bsz
    half = STATE_W // 2
    a = [a_ref[pl.ds(i, rows, stride=SSM_GROUP), :] for i in range(SSM_GROUP)]
    uc = [jnp.concatenate([x[:, c * SSM_CHUNK:(c + 1) * SSM_CHUNK] for x in a], axis=1).astype(_BF)
          for c in range(SLAB_CHUNKS)]
    w = w_ref[...]
    s = [_dot(u, w) for u in uc]
    sr = [x[:, :half] for x in s]
    si = [x[:, half:] for x in s]
    t = t_ref[...]
    trow = lambda r: t[r:r + 1, :]
    sup_r = sup_i = None
    for c in range(SLAB_CHUNKS):
        cr, ci = trow(c), trow(SLAB_CHUNKS + c)
        pr = cr * sr[c] - ci * si[c]
        pi = cr * si[c] + ci * sr[c]
        sup_r = pr if sup_r is None else sup_r + pr
        sup_i = pi if sup_i is None else sup_i + pi
    sup_scr[:, :half] = sup_r
    sup_scr[:, half:] = sup_i

    lane = lax.broadcasted_iota(jnp.int32, (bsz, half), 1)
    fwd = lane < SSM_STATE
    sm = _dot(um_ref[...], w)
    xr = jnp.where(fwd, sm[:bsz, :half], 0.0)
    xi = jnp.where(fwd, sm[:bsz, half:], 0.0)
    a_slab_r, a_slab_i = trow(2 * SLAB_CHUNKS), trow(2 * SLAB_CHUNKS + 1)
    for j in range(nslab):
        rf = j * bsz
        rb = (nslab - 1 - j) * bsz
        ent_scr[rf:rf + bsz, 0:SSM_STATE] = xr[:, 0:SSM_STATE]
        ent_scr[rb:rb + bsz, SSM_STATE:half] = xr[:, SSM_STATE:half]
        ent_scr[rf:rf + bsz, half:half + SSM_STATE] = xi[:, 0:SSM_STATE]
        ent_scr[rb:rb + bsz, half + SSM_STATE:STATE_W] = xi[:, SSM_STATE:half]
        s_r = jnp.where(fwd, sup_scr[rf:rf + bsz, :half], sup_scr[rb:rb + bsz, :half])
        s_i = jnp.where(fwd, sup_scr[rf:rf + bsz, half:], sup_scr[rb:rb + bsz, half:])
        xr, xi = _cmul_add(a_slab_r, a_slab_i, xr, xi, s_r, s_i)

    ent = ent_scr[...]
    a_r, a_i = trow(2 * SLAB_CHUNKS + 2), trow(2 * SLAB_CHUNKS + 3)
    xf = [(ent[:, :half], ent[:, half:])]
    for c in range(1, SLAB_CHUNKS):
        xf.append(_cmul_add(a_r, a_i, xf[-1][0], xf[-1][1], sr[c - 1], si[c - 1]))
    xb = [(ent[:, :half], ent[:, half:])]
    for c in range(SLAB_CHUNKS - 2, -1, -1):
        xb.insert(0, _cmul_add(a_r, a_i, xb[0][0], xb[0][1], sr[c + 1], si[c + 1]))
    fwd_rows = lax.broadcasted_iota(jnp.int32, (rows, half), 1) < SSM_STATE
    m = jnp.concatenate(
        [pltpu.roll(jnp.broadcast_to(lag_ref[i:i + 1, :], (SSM_CHUNK, 2 * CHUNK_W)), 0, 1,
                    stride=SSM_GROUP, stride_axis=0)[:, :CHUNK_W] for i in range(SSM_GROUP)], axis=0).astype(_BF)
    q = q_ref[...]
    m = _dot(m, q).astype(_BF)
    v = _dot(v_ref[...], q).astype(_BF)
    ys = []
    for c in range(SLAB_CHUNKS):
        xin = jnp.concatenate([jnp.where(fwd_rows, xf[c][0], xb[c][0]),
                               jnp.where(fwd_rows, xf[c][1], xb[c][1])], axis=1).astype(_BF)
        ys.append(_dot(uc[c], m) + _dot(xin, v))
    for o in range(SSM_GROUP):
        y_ref[pl.ds(o, rows, stride=SSM_GROUP), :] = jnp.concatenate(
            [y[:, o * SSM_CHUNK:(o + 1) * SSM_CHUNK] for y in ys], axis=1)


def _s5_call(a, um, perm, lags, w_mat, v_mat, tab, *, nslab, bsz):
    n = nslab * bsz * SSM_GROUP
    rows = nslab * bsz
    g_spec = lambda *shape: pl.BlockSpec((None,) + shape, lambda g: (g,) + (0,) * len(shape))
    perm_spec = pl.BlockSpec((CHUNK_W, CHUNK_W), lambda g: (0, 0), pipeline_mode=pl.Buffered(1))
    return pl.pallas_call(
        functools.partial(_s5_kernel, nslab=nslab, bsz=bsz),
        grid=(SSM_GROUPS,),
        in_specs=[g_spec(n, SLAB_T), g_spec(SUBLANES, CHUNK_W), perm_spec,
                  g_spec(SSM_GROUP, 2 * CHUNK_W), g_spec(CHUNK_W, STATE_W), g_spec(STATE_W, CHUNK_W),
                  g_spec(2 * SUBLANES, STATE_W // 2)],
        out_specs=g_spec(n, SLAB_T),
        out_shape=jax.ShapeDtypeStruct((SSM_GROUPS, n, SLAB_T), _F32),
        scratch_shapes=[pltpu.VMEM((rows, STATE_W), _F32), pltpu.VMEM((rows, STATE_W), _F32)],
        compiler_params=pltpu.CompilerParams(
            dimension_semantics=("parallel",), vmem_limit_bytes=VMEM_LIMIT),
        name="s5",
    )(a, um, perm, lags, w_mat, v_mat, tab)


def _s5_matrices(a_re, a_im, log_dt, b_re, b_im, c_re, c_im, d_skip):
    tc = SSM_CHUNK
    lam = lax.complex(jnp.minimum(a_re.astype(_F32), -1e-4), a_im.astype(_F32))
    dt = jnp.exp(log_dt.astype(_F32))[..., None]
    lam_dt = lam * dt
    lam_bar = jnp.exp(lam_dt)
    b_bar = ((lam_bar - 1.0) / lam)[..., None] * lax.complex(b_re.astype(_F32), b_im.astype(_F32))
    c_c = lax.complex(c_re.astype(_F32), c_im.astype(_F32))
    k_idx = jnp.arange(tc + 1, dtype=_F32)
    pw = jnp.exp(lam_dt[:, :, None, :] * k_idx[None, None, :, None])
    kern = jnp.real(jnp.einsum('dgop,dgkp,dgpi->dgkoi', c_c, pw[:, :, :tc], b_bar))
    d_g = d_skip.astype(_F32).reshape(SSM_GROUPS, SSM_GROUP)
    center = kern[0][:, :1] + kern[1][:, :1] + (jnp.eye(SSM_GROUP, dtype=_F32)[None] * d_g[:, :, None])[:, None]
    lags = jnp.concatenate([center, kern[0][:, 1:], jnp.zeros_like(center), kern[1][:, :0:-1]], axis=1)
    lags = lags.transpose(0, 3, 1, 2).reshape(SSM_GROUPS, SSM_GROUP, 2 * CHUNK_W)
    wf = b_bar[0].transpose(0, 2, 1)[:, :, None, :] * pw[0][:, tc - 1::-1][:, None, :, :]
    wb = b_bar[1].transpose(0, 2, 1)[:, :, None, :] * pw[1][:, :tc][:, None, :, :]
    w_mat = jnp.concatenate([jnp.real(wf), jnp.real(wb), jnp.imag(wf), jnp.imag(wb)], axis=-1)
    w_mat = w_mat.reshape(SSM_GROUPS, CHUNK_W, STATE_W)
    gf = pw[0][:, 1:tc + 1][:, :, None, :] * c_c[0][:, None, :, :]
    gb = pw[1][:, tc:0:-1][:, :, None, :] * c_c[1][:, None, :, :]
    v_mat = jnp.concatenate([jnp.real(gf), jnp.real(gb), -jnp.imag(gf), -jnp.imag(gb)], axis=-1)
    v_mat = v_mat.reshape(SSM_GROUPS, CHUNK_W, STATE_W).transpose(0, 2, 1)
    n_idx = jnp.arange(SLAB_CHUNKS + 1, dtype=_F32) * tc
    pc = jnp.exp(lam_dt[:, :, None, :] * n_idx[None, None, :, None])
    coef = jnp.concatenate([pc[0][:, SLAB_CHUNKS - 1::-1], pc[1][:, :SLAB_CHUNKS]], axis=-1)
    both = lambda n: jnp.concatenate([pc[0][:, n], pc[1][:, n]], axis=-1)[:, None, :]
    a_slab, a_chunk = both(SLAB_CHUNKS), both(1)
    tab = jnp.concatenate([jnp.real(coef), jnp.imag(coef), jnp.real(a_slab), jnp.imag(a_slab),
                           jnp.real(a_chunk), jnp.imag(a_chunk)], axis=1)
    tab = jnp.pad(tab, ((0, 0), (0, 2 * SUBLANES - tab.shape[1]), (0, 0)))
    return lags, w_mat, v_mat, tab


def _chunk_permutation():
    r = lax.broadcasted_iota(jnp.int32, (CHUNK_W, CHUNK_W), 0)
    c = lax.broadcasted_iota(jnp.int32, (CHUNK_W, CHUNK_W), 1)
    return (r == (c % SSM_CHUNK) * SSM_GROUP + c // SSM_CHUNK).astype(_BF)


def _post_kernel(x_ref, attn_ref, y_ref, wglu_ref, gmix_ref, wout_ref, gpm_ref, gpre_ref,
                 wup_ref, wdn_ref, gpost_ref, o_ref):
    nslab = y_ref.shape[1]
    gmix = gmix_ref[...]
    per = min(POST_SLABS, nslab)
    blocks = range(nslab // per)
    rows = [pl.ds(r * per * SLAB_T, per * SLAB_T) for r in blocks]
    gy = []
    for r in blocks:
        yt = jnp.concatenate([jnp.concatenate([y_ref[g, r * per + c] for c in range(per)], axis=1)
                              for g in range(SSM_GROUPS)], axis=0)
        y = yt.T
        gy.append((0.5 * y * (1.0 + jnp.tanh(math.sqrt(2.0 / math.pi) * (y + 0.044715 * (y * y * y))))).astype(_BF))
    z = [_dot(gy[r], wglu_ref[...]) for r in blocks]
    mix = []
    for r in blocks:
        ssm = z[r][:, :SSM_WIDTH] * (1.0 / (1.0 + jnp.exp(-z[r][:, SSM_WIDTH:])))
        mix.append(jnp.concatenate([_rms(attn_ref[rows[r], :], gmix[:, :ATTN_WIDTH]),
                                    _rms(ssm, gmix[:, ATTN_WIDTH:])], axis=-1).astype(_BF))
    mixed = [_dot(mix[r], wout_ref[...]) for r in blocks]
    h1 = [x_ref[rows[r], :] + _rms(mixed[r], gpm_ref[...]) for r in blocks]
    hn = [_rms(h1[r], gpre_ref[...]).astype(_BF) for r in blocks]
    acc = [None for _ in blocks]
    for c in range(D_FF // FF_TILE):
        for r in blocks:
            up = jnp.maximum(_dot(hn[r], wup_ref[:, c * FF_TILE:(c + 1) * FF_TILE]), 0.0)
            part = _dot((up * up).astype(_BF), wdn_ref[c * FF_TILE:(c + 1) * FF_TILE, :])
            acc[r] = part if acc[r] is None else acc[r] + part
    for r in blocks:
        o_ref[rows[r], :] = h1[r] + _rms(acc[r], gpost_ref[...])


def _post_call(x, attn, y, wglu, gmix, wout, gpm, gpre, wup, wdn, gpost, *, tile):
    bsz, seq, _ = x.shape
    row_spec = lambda w: pl.BlockSpec((None, tile, w), lambda b, i: (b, i, 0))
    wspec = lambda shape: pl.BlockSpec(shape, lambda b, i: (0, 0), pipeline_mode=pl.Buffered(1))
    y_spec = pl.BlockSpec((SSM_GROUPS, tile // SLAB_T, SSM_GROUP, SLAB_T), lambda b, i: (0, i, b, 0))
    return pl.pallas_call(
        _post_kernel,
        grid=(bsz, seq // tile),
        in_specs=[row_spec(D_MODEL), row_spec(ATTN_WIDTH), y_spec,
                  wspec((SSM_WIDTH, 2 * SSM_WIDTH)), wspec((1, D_MODEL)), wspec((D_MODEL, D_MODEL)),
                  wspec((1, D_MODEL)), wspec((1, D_MODEL)), wspec((D_MODEL, D_FF)),
                  wspec((D_FF, D_MODEL)), wspec((1, D_MODEL))],
        out_specs=row_spec(D_MODEL),
        out_shape=jax.ShapeDtypeStruct((bsz, seq, D_MODEL), _F32),
        compiler_params=pltpu.CompilerParams(
            dimension_semantics=("parallel", "parallel"), vmem_limit_bytes=VMEM_LIMIT),
        name="post",
    )(x, attn, y, wglu, gmix, wout, gpm, gpre, wup, wdn, gpost)


def _rope_tables(pos, tile):
    half = QK_ROPE_DIM // 2
    inv = 1.0 / (ROPE_BASE ** (jnp.arange(0, QK_ROPE_DIM, 2, dtype=_F32) / QK_ROPE_DIM))
    ang = pos.astype(_F32)[:, None, :] * inv[None, :, None]
    bsz, seq = pos.shape
    rope = jnp.stack([jnp.cos(ang), jnp.sin(ang)], axis=1)
    return rope.reshape(bsz, 2, half, seq // tile, tile).transpose(0, 3, 1, 2, 4)


def _prep_weights(w_in, w_uq, w_ukv):
    scale = QK_HEAD_DIM ** -0.5 * math.log2(math.e)
    win = w_in[:, :OFF_KR].astype(_BF)
    wkr_t = w_in[:, OFF_KR:OFF_U].T.astype(_BF)
    wu_t = w_in[:, OFF_U:].T.astype(_BF)
    wq_t = (w_uq * scale).T.astype(_BF)
    wkv3 = w_ukv.reshape(KV_LORA_RANK, N_HEADS, QK_NOPE_DIM + V_HEAD_DIM)
    wk = jnp.concatenate([wkv3[..., :QK_NOPE_DIM],
                          jnp.zeros((KV_LORA_RANK, N_HEADS, HEAD_PAD - QK_NOPE_DIM), _F32)], axis=-1)
    wk = wk.reshape(KV_LORA_RANK, N_HEADS * HEAD_PAD).astype(_BF)
    wv_t = wkv3[..., QK_NOPE_DIM:].reshape(KV_LORA_RANK, N_HEADS * V_HEAD_DIM).T.astype(_BF)
    return win, wkr_t, wu_t, wq_t, wk, wv_t


def kernel(x, positions, meta_tokens, g_pre_mix, w_in, g_q_lat, w_uq, g_kv_lat, w_ukv,
           ssm_A_re, ssm_A_im, ssm_log_dt, ssm_B_re, ssm_B_im, ssm_C_re, ssm_C_im, ssm_D,
           w_glu, g_mix_out, w_out, g_post_mix, g_pre_mlp, w_mlp_up, w_mlp_down, g_post_mlp):
    bsz, seq, _ = x.shape
    assert seq % ROW_TILE == 0 and ROW_TILE % SLAB_T == 0 and bsz <= SUBLANES
    assert N_META <= SSM_CHUNK
    row = lambda g: g.reshape(1, -1).astype(_F32)

    win, wkr_t, wu_t, wq_t, wk, wv_t = _prep_weights(w_in[0], w_uq[0], w_ukv[0])
    weights = (row(g_pre_mix[0]), win, wkr_t, wu_t, row(g_q_lat[0]), wq_t, row(g_kv_lat[0]), wk, wv_t)
    rope = _rope_tables(positions.astype(jnp.int32) + N_META, ROW_TILE)
    qt, k, vt, u, q_norm, k_sq = _proj_call(x, rope, *weights, tile=ROW_TILE, meta=False)
    meta_x = jnp.pad(meta_tokens.astype(x.dtype), ((0, LANES - N_META), (0, 0)))[None]
    rope_m = _rope_tables(jnp.arange(LANES, dtype=jnp.int32)[None], LANES)
    k_m, vt_m, u_m = _proj_call(meta_x, rope_m, *weights, tile=LANES, meta=True)
    km, vtm, u_m = k_m[0], vt_m[0, :, 0], u_m[:, :N_META]
    k_m_sq = jnp.max(jnp.sum(jnp.square(k_m[0].astype(_F32)), axis=-1), axis=-1)
    k_max = jnp.sqrt(jnp.maximum(jnp.max(k_sq, axis=(2, 3, 4)), k_m_sq[None]))
    mrow = q_norm * (k_max * BOUND_SLACK)[:, :, None, None, None]
    attn = lax.cond(jnp.max(mrow) <= BOUND_LIMIT,
                    lambda: _attn_call(qt, mrow, k, vt, km, vtm, bounded=True),
                    lambda: _attn_call(qt, mrow, k, vt, km, vtm, bounded=False))

    um = u_m[0].astype(_BF).reshape(N_META, SSM_GROUPS, SSM_GROUP).transpose(1, 2, 0)
    um = jnp.pad(um, ((0, 0), (0, 0), (SSM_CHUNK - N_META, 0))).reshape(SSM_GROUPS, 1, CHUNK_W)
    um = jnp.broadcast_to(um, (SSM_GROUPS, SUBLANES, CHUNK_W))
    lags, w_mat, v_mat, tab = _s5_matrices(ssm_A_re[0], ssm_A_im[0], ssm_log_dt[0], ssm_B_re[0],
                                            ssm_B_im[0], ssm_C_re[0], ssm_C_im[0], ssm_D[0])
    nslab = seq // SLAB_T
    yg = _s5_call(u.reshape(SSM_GROUPS, nslab * bsz * SSM_GROUP, SLAB_T), um, _chunk_permutation(),
                  lags, w_mat.astype(_BF), v_mat.astype(_BF), tab, nslab=nslab, bsz=bsz)
    y = yg.reshape(SSM_GROUPS, nslab, bsz * SSM_GROUP, SLAB_T)

    return _post_call(x, attn, y, w_glu[0].astype(_BF), row(g_mix_out[0]), w_out[0].astype(_BF),
                      row(g_post_mix[0]), row(g_pre_mlp[0]), w_mlp_up[0].astype(_BF),
                      w_mlp_down[0].astype(_BF), row(g_post_mlp[0]), tile=ROW_TILE)
```

```python
import functools
import math

import jax
import jax.numpy as jnp
from jax import lax
from jax.experimental import pallas as pl
from jax.experimental.pallas import tpu as pltpu

D_MODEL = 1024
N_META = 16
ATTN_WIDTH = 512
SSM_WIDTH = 512
N_HEADS = 8
V_HEAD_DIM = 64
QK_NOPE_DIM = 64
QK_ROPE_DIM = 32
QK_HEAD_DIM = QK_NOPE_DIM + QK_ROPE_DIM
Q_LORA_RANK = 384
KV_LORA_RANK = 256
ROPE_BASE = 10000.0
SSM_GROUP = 16
SSM_GROUPS = 32
SSM_STATE = 64
D_FF = 4 * D_MODEL
EPS = 1e-6
OFF_KV = Q_LORA_RANK
OFF_KR = OFF_KV + KV_LORA_RANK
OFF_U = OFF_KR + QK_ROPE_DIM

LANES = 128
SUBLANES = 8
HEAD_PAD = LANES
ONES_LANE = V_HEAD_DIM
ATTN_TQ = 1024
ATTN_TK = 256
V_ROWS = HEAD_PAD
PW_Q = 0
PW_KV = PW_Q + Q_LORA_RANK
PW_END = PW_KV + KV_LORA_RANK

SSM_CHUNK = 32
CHUNK_W = SSM_CHUNK * SSM_GROUP
STATE_W = 4 * SSM_STATE
SLAB_T = LANES
SLAB_CHUNKS = SLAB_T // SSM_CHUNK
SLAB_W = SSM_GROUP * SLAB_T

ROW_TILE = 512
PROJ_TILE = 1024
FF_TILE = 1024
POST_SLABS = 2
BOUND_SLACK = 1.0 + 2.0 ** -6
BOUND_LIMIT = 60.0
VMEM_LIMIT = 56 * 1024 * 1024

_BF = jnp.bfloat16
_F32 = jnp.float32


def _dot(a, b):
    return jnp.dot(a, b, preferred_element_type=_F32)


def _rms(x, g):
    return x * lax.rsqrt(jnp.mean(x * x, axis=-1, keepdims=True) + EPS) * g


_NT = (((1,), (1,)), ((), ()))


def _rotate(x1, x2, cos_t, sin_t):
    return x1 * cos_t - x2 * sin_t, x1 * sin_t + x2 * cos_t


def _proj_kernel(x_ref, rope_ref, *refs, meta):
    if meta:
        gpre_ref, win_ref, wkr_ref, wu_ref, gkv_ref, wk_ref, wv_ref, k_ref, v_ref, u_ref = refs
    else:
        (gpre_ref, win_ref, wkr_ref, wu_ref, gq_ref, wq_ref, gkv_ref, wk_ref, wv_ref,
         q_ref, k_ref, v_ref, u_ref, qn_ref, kmx_ref) = refs
    tile = x_ref.shape[0]
    half = QK_ROPE_DIM // 2
    cos_t, sin_t = rope_ref[0], rope_ref[1]
    xn = _rms(x_ref[...], gpre_ref[...]).astype(_BF)
    proj = _dot(xn, win_ref[...])
    kvn = _rms(proj[:, PW_KV:PW_END], gkv_ref[...]).astype(_BF)
    krt = lax.dot_general(wkr_ref[...], xn, _NT, preferred_element_type=_F32)
    r1, r2 = _rotate(krt[:half], krt[half:], cos_t, sin_t)
    kr = jnp.concatenate([jnp.zeros((QK_NOPE_DIM, tile), _F32), r1, r2,
                          jnp.zeros((HEAD_PAD - QK_HEAD_DIM, tile), _F32)], axis=0).T
    kk = _dot(kvn, wk_ref[...])
    vt = lax.dot_general(wv_ref[...], kvn, _NT, preferred_element_type=_F32)
    ones_tail = (lax.broadcasted_iota(jnp.int32, (HEAD_PAD - V_HEAD_DIM, tile), 0) == 0).astype(_F32)
    if meta:
        u_ref[...] = lax.dot_general(xn, wu_ref[...], _NT, preferred_element_type=_F32)
    else:
        ut = lax.dot_general(wu_ref[...], xn, _NT, preferred_element_type=_F32)
        for g in range(SSM_GROUPS):
            for c in range(tile // SLAB_T):
                u_ref[g, c] = ut[g * SSM_GROUP:(g + 1) * SSM_GROUP, c * SLAB_T:(c + 1) * SLAB_T]
        qn = _rms(proj[:, PW_Q:PW_KV], gq_ref[...]).astype(_BF)
        qt = lax.dot_general(wq_ref[...], qn, _NT, preferred_element_type=_F32)
        zero_rows = jnp.zeros((HEAD_PAD - QK_HEAD_DIM, tile), _F32)
    for h in range(N_HEADS):
        k_h = (kk[:, h * HEAD_PAD:(h + 1) * HEAD_PAD] + kr).astype(_BF)
        k_ref[h] = k_h
        v_ref[h] = jnp.concatenate([vt[h * V_HEAD_DIM:(h + 1) * V_HEAD_DIM], ones_tail], axis=0).astype(_BF)
        if not meta:
            blk = qt[h * QK_HEAD_DIM:(h + 1) * QK_HEAD_DIM]
            r1, r2 = _rotate(blk[QK_NOPE_DIM:QK_NOPE_DIM + half], blk[QK_NOPE_DIM + half:], cos_t, sin_t)
            qt_h = jnp.concatenate([blk[:QK_NOPE_DIM], r1, r2, zero_rows], axis=0).astype(_BF)
            q_ref[h] = qt_h
            qt_f = qt_h.astype(_F32)
            qn_ref[h] = jnp.sqrt(jnp.sum(qt_f * qt_f, axis=0, keepdims=True))
            k_f = k_h.astype(_F32)
            kmx_ref[h] = jnp.broadcast_to(jnp.max(jnp.sum(k_f * k_f, axis=1, keepdims=True), axis=0, keepdims=True),
                                          (1, LANES))


def _const_spec(shape):
    nd = len(shape)
    return pl.BlockSpec(shape, lambda *_: (0,) * nd)


def _proj_call(x, rope, gpre, win, wkr_t, wu_t, gq, wq_t, gkv, wk, wv_t, *, tile, meta):
    bsz, seq, _ = x.shape
    nt = seq // tile
    row_spec = lambda w: pl.BlockSpec((None, tile, w), lambda b, i: (b, i, 0))
    rope_spec = pl.BlockSpec((None, None, 2, QK_ROPE_DIM // 2, tile), lambda b, i: (b, i, 0, 0, 0))
    k_spec = pl.BlockSpec((None, N_HEADS, tile, HEAD_PAD), lambda b, i: (b, 0, i, 0))
    t_spec = pl.BlockSpec((None, N_HEADS, None, HEAD_PAD, tile), lambda b, i: (b, 0, i, 0, 0))
    k_shape = jax.ShapeDtypeStruct((bsz, N_HEADS, seq, HEAD_PAD), _BF)
    t_shape = jax.ShapeDtypeStruct((bsz, N_HEADS, nt, HEAD_PAD, tile), _BF)
    w_specs = lambda *ws: [_const_spec(w.shape) for w in ws]
    if meta:
        args = (x, rope, gpre, win, wkr_t, wu_t, gkv, wk, wv_t)
        in_specs = [row_spec(D_MODEL), rope_spec] + w_specs(*args[2:])
        out_specs = [k_spec, t_spec, row_spec(SSM_WIDTH)]
        out_shape = [k_shape, t_shape, jax.ShapeDtypeStruct((bsz, seq, SSM_WIDTH), _F32)]
    else:
        args = (x, rope, gpre, win, wkr_t, wu_t, gq, wq_t, gkv, wk, wv_t)
        in_specs = [row_spec(D_MODEL), rope_spec] + w_specs(*args[2:])
        norm_spec = lambda w: pl.BlockSpec((None, N_HEADS, None, 1, w), lambda b, i: (b, 0, i, 0, 0))
        u_spec = pl.BlockSpec((SSM_GROUPS, tile // SLAB_T, SSM_GROUP, SLAB_T), lambda b, i: (0, i, b, 0))
        out_specs = [t_spec, k_spec, t_spec, u_spec, norm_spec(tile), norm_spec(LANES)]
        out_shape = [t_shape, k_shape, t_shape,
                     jax.ShapeDtypeStruct((SSM_GROUPS, seq // SLAB_T, bsz * SSM_GROUP, SLAB_T), _F32),
                     jax.ShapeDtypeStruct((bsz, N_HEADS, nt, 1, tile), _F32),
                     jax.ShapeDtypeStruct((bsz, N_HEADS, nt, 1, LANES), _F32)]
    return pl.pallas_call(
        functools.partial(_proj_kernel, meta=meta),
        grid=(bsz, nt),
        in_specs=in_specs,
        out_specs=out_specs,
        out_shape=out_shape,
        compiler_params=pltpu.CompilerParams(
            dimension_semantics=("parallel", "parallel"), vmem_limit_bytes=VMEM_LIMIT),
        name="proj_meta" if meta else "proj",
    )(*args)


def _attn_finish(accs, o_ref):
    halves = [(acc * (1.0 / acc[ONES_LANE:ONES_LANE + 1, :]))[:V_HEAD_DIM] for acc in accs]
    o_ref[...] = jnp.concatenate(halves, axis=0).T


def _lane_concat(ref, hh):
    return jnp.concatenate([ref[hh, j] for j in range(ref.shape[1])], axis=1)


def _attn_bounded_kernel(qt_ref, mrow_ref, k_ref, vt_ref, km_ref, vtm_ref, o_ref, *, nk, tk):
    tq = o_ref.shape[0]
    per_slab = vt_ref.shape[3] // tk
    key_row = lax.broadcasted_iota(jnp.int32, (LANES, tq), 0)
    accs = []
    for hh in range(2):
        qt = _lane_concat(qt_ref, hh)
        mrow = _lane_concat(mrow_ref, hh)
        s0 = jnp.where(key_row < N_META, _dot(km_ref[hh], qt), -jnp.inf)
        acc = _dot(vtm_ref[hh, :V_ROWS, :], jnp.exp2(s0 - mrow).astype(_BF))
        scores = lambda c: _dot(k_ref[hh, c * tk:(c + 1) * tk, :], qt)
        s_next = scores(0)
        for c in range(nk):
            s = s_next
            if c + 1 < nk:
                s_next = scores(c + 1)
            vt_c = vt_ref[hh, c // per_slab, :V_ROWS, (c % per_slab) * tk:(c % per_slab + 1) * tk]
            acc = acc + _dot(vt_c, jnp.exp2(s - mrow).astype(_BF))
        accs.append(acc)
    _attn_finish(accs, o_ref)


def _attn_online_kernel(qt_ref, k_ref, vt_ref, km_ref, vtm_ref, o_ref, s0_scr, s1_scr, m_scr, acc_scr, *, nk, tk):
    tq = o_ref.shape[0]
    key_row = lax.broadcasted_iota(jnp.int32, (LANES, tq), 0)
    for hh in range(2):
        s0 = jnp.where(key_row < N_META, _dot(km_ref[hh], _lane_concat(qt_ref, hh)), -jnp.inf)
        m0 = jnp.max(s0, axis=0, keepdims=True)
        m_scr[hh] = m0
        acc_scr[hh] = _dot(vtm_ref[hh, :V_ROWS, :], jnp.exp2(s0 - m0).astype(_BF))

    def scores(buf, c):
        off = pl.multiple_of(c * tk, tk)
        for hh in range(2):
            buf[hh] = _dot(k_ref[hh, pl.ds(off, tk), :], _lane_concat(qt_ref, hh))

    def accumulate(buf, c):
        for hh in range(2):
            s = buf[hh]
            m = m_scr[hh]
            m_new = jnp.maximum(m, jnp.max(s, axis=0, keepdims=True))
            m_scr[hh] = m_new
            p = jnp.exp2(s - m_new).astype(_BF)
            acc_scr[hh] = jnp.exp2(m - m_new) * acc_scr[hh] + _dot(vt_ref[hh, c, :V_ROWS, :], p)

    scores(s0_scr, 0)

    def body(t, _):
        scores(s1_scr, 2 * t + 1)
        accumulate(s0_scr, 2 * t)
        scores(s0_scr, 2 * t + 2)
        accumulate(s1_scr, 2 * t + 1)
        return 0

    lax.fori_loop(0, nk // 2 - 1, body, 0)
    scores(s1_scr, nk - 1)
    accumulate(s0_scr, nk - 2)
    accumulate(s1_scr, nk - 1)
    _attn_finish([acc_scr[0], acc_scr[1]], o_ref)


def _attn_call(qt, mrow, k, vt, km, vtm, *, bounded):
    bsz, _, nslab, _, slab = qt.shape
    seq = k.shape[2]
    tq = min(ATTN_TQ, seq)
    tk = min(ATTN_TK, slab)
    qs = tq // slab
    assert tq % slab == 0 and seq % tq == 0 and slab % tk == 0
    q_spec = lambda rows: pl.BlockSpec((None, 2, qs, rows, slab), lambda b, hp, i: (b, hp, i, 0, 0))
    kv_specs = [pl.BlockSpec((None, 2, seq, HEAD_PAD), lambda b, hp, i: (b, hp, 0, 0)),
                pl.BlockSpec((None, 2, nslab, HEAD_PAD, slab), lambda b, hp, i: (b, hp, 0, 0, 0)),
                pl.BlockSpec((2, LANES, HEAD_PAD), lambda b, hp, i: (hp, 0, 0)),
                pl.BlockSpec((2, HEAD_PAD, LANES), lambda b, hp, i: (hp, 0, 0))]
    if bounded:
        body = functools.partial(_attn_bounded_kernel, nk=seq // tk, tk=tk)
        in_specs = [q_spec(HEAD_PAD), q_spec(1)] + kv_specs
        args, scratch = (qt, mrow, k, vt, km, vtm), []
    else:
        assert nslab % 2 == 0 and nslab >= 4
        body = functools.partial(_attn_online_kernel, nk=nslab, tk=slab)
        in_specs = [q_spec(HEAD_PAD)] + kv_specs
        args = (qt, k, vt, km, vtm)
        scratch = [pltpu.VMEM((2, slab, tq), _F32), pltpu.VMEM((2, slab, tq), _F32),
                   pltpu.VMEM((2, 1, tq), _F32), pltpu.VMEM((2, V_ROWS, tq), _F32)]
    return pl.pallas_call(
        body,
        grid=(bsz, N_HEADS // 2, seq // tq),
        in_specs=in_specs,
        out_specs=pl.BlockSpec((None, tq, LANES), lambda b, hp, i: (b, i, hp)),
        out_shape=jax.ShapeDtypeStruct((bsz, seq, ATTN_WIDTH), _F32),
        scratch_shapes=scratch,
        compiler_params=pltpu.CompilerParams(
            dimension_semantics=("parallel", "parallel", "arbitrary"), vmem_limit_bytes=VMEM_LIMIT),
        name="attn_bounded" if bounded else "attn_online",
    )(*args)


def _cmul_add(ar, ai, xr, xi, sr, si):
    return ar * xr - ai * xi + sr, ar * xi + ai * xr + si


def _s5_kernel(a_ref, um_ref, q_ref, lag_ref, w_ref, v_ref, t_ref, y_ref, sup_scr, ent_scr,
               *, nslab, bsz):
    rows = nslab * bsz
    half = STATE_W // 2
    a = [a_ref[pl.ds(i, rows, stride=SSM_GROUP), :] for i in range(SSM_GROUP)]
    uc = [jnp.concatenate([x[:, c * SSM_CHUNK:(c + 1) * SSM_CHUNK] for x in a], axis=1).astype(_BF)
          for c in range(SLAB_CHUNKS)]
    w = w_ref[...]
    s = [_dot(u, w) for u in uc]
    sr = [x[:, :half] for x in s]
    si = [x[:, half:] for x in s]
    t = t_ref[...]
    trow = lambda r: t[r:r + 1, :]
    sup_r = sup_i = None
    for c in range(SLAB_CHUNKS):
        cr, ci = trow(c), trow(SLAB_CHUNKS + c)
        pr = cr * sr[c] - ci * si[c]
        pi = cr * si[c] + ci * sr[c]
        sup_r = pr if sup_r is None else sup_r + pr
        sup_i = pi if sup_i is None else sup_i + pi
    sup_scr[:, :half] = sup_r
    sup_scr[:, half:] = sup_i

    lane = lax.broadcasted_iota(jnp.int32, (bsz, half), 1)
    fwd = lane < SSM_STATE
    sm = _dot(um_ref[...], w)
    xr = jnp.where(fwd, sm[:bsz, :half], 0.0)
    xi = jnp.where(fwd, sm[:bsz, half:], 0.0)
    a_slab_r, a_slab_i = trow(2 * SLAB_CHUNKS), trow(2 * SLAB_CHUNKS + 1)
    for j in range(nslab):
        rf = j * bsz
        rb = (nslab - 1 - j) * bsz
        ent_scr[rf:rf + bsz, 0:SSM_STATE] = xr[:, 0:SSM_STATE]
        ent_scr[rb:rb + bsz, SSM_STATE:half] = xr[:, SSM_STATE:half]
        ent_scr[rf:rf + bsz, half:half + SSM_STATE] = xi[:, 0:SSM_STATE]
        ent_scr[rb:rb + bsz, half + SSM_STATE:STATE_W] = xi[:, SSM_STATE:half]
        s_r = jnp.where(fwd, sup_scr[rf:rf + bsz, :half], sup_scr[rb:rb + bsz, :half])
        s_i = jnp.where(fwd, sup_scr[rf:rf + bsz, half:], sup_scr[rb:rb + bsz, half:])
        xr, xi = _cmul_add(a_slab_r, a_slab_i, xr, xi, s_r, s_i)

    ent = ent_scr[...]
    a_r, a_i = trow(2 * SLAB_CHUNKS + 2), trow(2 * SLAB_CHUNKS + 3)
    xf = [(ent[:, :half], ent[:, half:])]
    for c in range(1, SLAB_CHUNKS):
        xf.append(_cmul_add(a_r, a_i, xf[-1][0], xf[-1][1], sr[c - 1], si[c - 1]))
    xb = [(ent[:, :half], ent[:, half:])]
    for c in range(SLAB_CHUNKS - 2, -1, -1):
        xb.insert(0, _cmul_add(a_r, a_i, xb[0][0], xb[0][1], sr[c + 1], si[c + 1]))
    fwd_rows = lax.broadcasted_iota(jnp.int32, (rows, half), 1) < SSM_STATE
    m = jnp.concatenate(
        [pltpu.roll(jnp.broadcast_to(lag_ref[i:i + 1, :], (SSM_CHUNK, 2 * CHUNK_W)), 0, 1,
                    stride=SSM_GROUP, stride_axis=0)[:, :CHUNK_W] for i in range(SSM_GROUP)], axis=0).astype(_BF)
    q = q_ref[...]
    m = _dot(m, q).astype(_BF)
    v = _dot(v_ref[...], q).astype(_BF)
    ys = []
    for c in range(SLAB_CHUNKS):
        xin = jnp.concatenate([jnp.where(fwd_rows, xf[c][0], xb[c][0]),
                               jnp.where(fwd_rows, xf[c][1], xb[c][1])], axis=1).astype(_BF)
        ys.append(_dot(uc[c], m) + _dot(xin, v))
    for o in range(SSM_GROUP):
        y_ref[pl.ds(o, rows, stride=SSM_GROUP), :] = jnp.concatenate(
            [y[:, o * SSM_CHUNK:(o + 1) * SSM_CHUNK] for y in ys], axis=1)


def _s5_call(a, um, perm, lags, w_mat, v_mat, tab, *, nslab, bsz):
    n = nslab * bsz * SSM_GROUP
    rows = nslab * bsz
    g_spec = lambda *shape: pl.BlockSpec((None,) + shape, lambda g: (g,) + (0,) * len(shape))
    perm_spec = pl.BlockSpec((CHUNK_W, CHUNK_W), lambda g: (0, 0), pipeline_mode=pl.Buffered(1))
    return pl.pallas_call(
        functools.partial(_s5_kernel, nslab=nslab, bsz=bsz),
        grid=(SSM_GROUPS,),
        in_specs=[g_spec(n, SLAB_T), g_spec(SUBLANES, CHUNK_W), perm_spec,
                  g_spec(SSM_GROUP, 2 * CHUNK_W), g_spec(CHUNK_W, STATE_W), g_spec(STATE_W, CHUNK_W),
                  g_spec(2 * SUBLANES, STATE_W // 2)],
        out_specs=g_spec(n, SLAB_T),
        out_shape=jax.ShapeDtypeStruct((SSM_GROUPS, n, SLAB_T), _F32),
        scratch_shapes=[pltpu.VMEM((rows, STATE_W), _F32), pltpu.VMEM((rows, STATE_W), _F32)],
        compiler_params=pltpu.CompilerParams(
            dimension_semantics=("parallel",), vmem_limit_bytes=VMEM_LIMIT),
        name="s5",
    )(a, um, perm, lags, w_mat, v_mat, tab)


def _s5_matrices(a_re, a_im, log_dt, b_re, b_im, c_re, c_im, d_skip):
    tc = SSM_CHUNK
    lam = lax.complex(jnp.minimum(a_re.astype(_F32), -1e-4), a_im.astype(_F32))
    dt = jnp.exp(log_dt.astype(_F32))[..., None]
    lam_dt = lam * dt
    lam_bar = jnp.exp(lam_dt)
    b_bar = ((lam_bar - 1.0) / lam)[..., None] * lax.complex(b_re.astype(_F32), b_im.astype(_F32))
    c_c = lax.complex(c_re.astype(_F32), c_im.astype(_F32))
    k_idx = jnp.arange(tc + 1, dtype=_F32)
    pw = jnp.exp(lam_dt[:, :, None, :] * k_idx[None, None, :, None])
    kern = jnp.real(jnp.einsum('dgop,dgkp,dgpi->dgkoi', c_c, pw[:, :, :tc], b_bar))
    d_g = d_skip.astype(_F32).reshape(SSM_GROUPS, SSM_GROUP)
    center = kern[0][:, :1] + kern[1][:, :1] + (jnp.eye(SSM_GROUP, dtype=_F32)[None] * d_g[:, :, None])[:, None]
    lags = jnp.concatenate([center, kern[0][:, 1:], jnp.zeros_like(center), kern[1][:, :0:-1]], axis=1)
    lags = lags.transpose(0, 3, 1, 2).reshape(SSM_GROUPS, SSM_GROUP, 2 * CHUNK_W)
    wf = b_bar[0].transpose(0, 2, 1)[:, :, None, :] * pw[0][:, tc - 1::-1][:, None, :, :]
    wb = b_bar[1].transpose(0, 2, 1)[:, :, None, :] * pw[1][:, :tc][:, None, :, :]
    w_mat = jnp.concatenate([jnp.real(wf), jnp.real(wb), jnp.imag(wf), jnp.imag(wb)], axis=-1)
    w_mat = w_mat.reshape(SSM_GROUPS, CHUNK_W, STATE_W)
    gf = pw[0][:, 1:tc + 1][:, :, None, :] * c_c[0][:, None, :, :]
    gb = pw[1][:, tc:0:-1][:, :, None, :] * c_c[1][:, None, :, :]
    v_mat = jnp.concatenate([jnp.real(gf), jnp.real(gb), -jnp.imag(gf), -jnp.imag(gb)], axis=-1)
    v_mat = v_mat.reshape(SSM_GROUPS, CHUNK_W, STATE_W).transpose(0, 2, 1)
    n_idx = jnp.arange(SLAB_CHUNKS + 1, dtype=_F32) * tc
    pc = jnp.exp(lam_dt[:, :, None, :] * n_idx[None, None, :, None])
    coef = jnp.concatenate([pc[0][:, SLAB_CHUNKS - 1::-1], pc[1][:, :SLAB_CHUNKS]], axis=-1)
    both = lambda n: jnp.concatenate([pc[0][:, n], pc[1][:, n]], axis=-1)[:, None, :]
    a_slab, a_chunk = both(SLAB_CHUNKS), both(1)
    tab = jnp.concatenate([jnp.real(coef), jnp.imag(coef), jnp.real(a_slab), jnp.imag(a_slab),
                           jnp.real(a_chunk), jnp.imag(a_chunk)], axis=1)
    tab = jnp.pad(tab, ((0, 0), (0, 2 * SUBLANES - tab.shape[1]), (0, 0)))
    return lags, w_mat, v_mat, tab


def _chunk_permutation():
    r = lax.broadcasted_iota(jnp.int32, (CHUNK_W, CHUNK_W), 0)
    c = lax.broadcasted_iota(jnp.int32, (CHUNK_W, CHUNK_W), 1)
    return (r == (c % SSM_CHUNK) * SSM_GROUP + c // SSM_CHUNK).astype(_BF)


def _post_kernel(x_ref, attn_ref, y_ref, wglu_ref, gmix_ref, wout_ref, gpm_ref, gpre_ref,
                 wup_ref, wdn_ref, gpost_ref, o_ref):
    nslab = y_ref.shape[1]
    gmix = gmix_ref[...]
    per = min(POST_SLABS, nslab)
    blocks = range(nslab // per)
    rows = [pl.ds(r * per * SLAB_T, per * SLAB_T) for r in blocks]
    gy = []
    for r in blocks:
        yt = jnp.concatenate([jnp.concatenate([y_ref[g, r * per + c] for c in range(per)], axis=1)
                              for g in range(SSM_GROUPS)], axis=0)
        y = yt.T
        gy.append((0.5 * y * (1.0 + jnp.tanh(math.sqrt(2.0 / math.pi) * (y + 0.044715 * (y * y * y))))).astype(_BF))
    z = [_dot(gy[r], wglu_ref[...]) for r in blocks]
    mix = []
    for r in blocks:
        ssm = z[r][:, :SSM_WIDTH] * (1.0 / (1.0 + jnp.exp(-z[r][:, SSM_WIDTH:])))
        mix.append(jnp.concatenate([_rms(attn_ref[rows[r], :], gmix[:, :ATTN_WIDTH]),
                                    _rms(ssm, gmix[:, ATTN_WIDTH:])], axis=-1).astype(_BF))
    mixed = [_dot(mix[r], wout_ref[...]) for r in blocks]
    h1 = [x_ref[rows[r], :] + _rms(mixed[r], gpm_ref[...]) for r in blocks]
    hn = [_rms(h1[r], gpre_ref[...]).astype(_BF) for r in blocks]
    acc = [None for _ in blocks]
    for c in range(D_FF // FF_TILE):
        for r in blocks:
            up = jnp.maximum(_dot(hn[r], wup_ref[:, c * FF_TILE:(c + 1) * FF_TILE]), 0.0)
            part = _dot((up * up).astype(_BF), wdn_ref[c * FF_TILE:(c + 1) * FF_TILE, :])
            acc[r] = part if acc[r] is None else acc[r] + part
    for r in blocks:
        o_ref[rows[r], :] = h1[r] + _rms(acc[r], gpost_ref[...])


def _post_call(x, attn, y, wglu, gmix, wout, gpm, gpre, wup, wdn, gpost, *, tile):
    bsz, seq, _ = x.shape
    row_spec = lambda w: pl.BlockSpec((None, tile, w), lambda b, i: (b, i, 0))
    wspec = lambda shape: pl.BlockSpec(shape, lambda b, i: (0, 0), pipeline_mode=pl.Buffered(1))
    y_spec = pl.BlockSpec((SSM_GROUPS, tile // SLAB_T, SSM_GROUP, SLAB_T), lambda b, i: (0, i, b, 0))
    return pl.pallas_call(
        _post_kernel,
        grid=(bsz, seq // tile),
        in_specs=[row_spec(D_MODEL), row_spec(ATTN_WIDTH), y_spec,
                  wspec((SSM_WIDTH, 2 * SSM_WIDTH)), wspec((1, D_MODEL)), wspec((D_MODEL, D_MODEL)),
                  wspec((1, D_MODEL)), wspec((1, D_MODEL)), wspec((D_MODEL, D_FF)),
                  wspec((D_FF, D_MODEL)), wspec((1, D_MODEL))],
        out_specs=row_spec(D_MODEL),
        out_shape=jax.ShapeDtypeStruct((bsz, seq, D_MODEL), _F32),
        compiler_params=pltpu.CompilerParams(
            dimension_semantics=("parallel", "parallel"), vmem_limit_bytes=VMEM_LIMIT),
        name="post",
    )(x, attn, y, wglu, gmix, wout, gpm, gpre, wup, wdn, gpost)


def _rope_tables(pos, tile):
    half = QK_ROPE_DIM // 2
    inv = 1.0 / (ROPE_BASE ** (jnp.arange(0, QK_ROPE_DIM, 2, dtype=_F32) / QK_ROPE_DIM))
    ang = pos.astype(_F32)[:, None, :] * inv[None, :, None]
    bsz, seq = pos.shape
    rope = jnp.stack([jnp.cos(ang), jnp.sin(ang)], axis=1)
    return rope.reshape(bsz, 2, half, seq // tile, tile).transpose(0, 3, 1, 2, 4)


def _prep_weights(w_in, w_uq, w_ukv):
    scale = QK_HEAD_DIM ** -0.5 * math.log2(math.e)
    win = w_in[:, :OFF_KR].astype(_BF)
    wkr_t = w_in[:, OFF_KR:OFF_U].T.astype(_BF)
    wu_t = w_in[:, OFF_U:].T.astype(_BF)
    wq_t = (w_uq * scale).T.astype(_BF)
    wkv3 = w_ukv.reshape(KV_LORA_RANK, N_HEADS, QK_NOPE_DIM + V_HEAD_DIM)
    wk = jnp.concatenate([wkv3[..., :QK_NOPE_DIM],
                          jnp.zeros((KV_LORA_RANK, N_HEADS, HEAD_PAD - QK_NOPE_DIM), _F32)], axis=-1)
    wk = wk.reshape(KV_LORA_RANK, N_HEADS * HEAD_PAD).astype(_BF)
    wv_t = wkv3[..., QK_NOPE_DIM:].reshape(KV_LORA_RANK, N_HEADS * V_HEAD_DIM).T.astype(_BF)
    return win, wkr_t, wu_t, wq_t, wk, wv_t


def kernel(x, positions, meta_tokens, g_pre_mix, w_in, g_q_lat, w_uq, g_kv_lat, w_ukv,
           ssm_A_re, ssm_A_im, ssm_log_dt, ssm_B_re, ssm_B_im, ssm_C_re, ssm_C_im, ssm_D,
           w_glu, g_mix_out, w_out, g_post_mix, g_pre_mlp, w_mlp_up, w_mlp_down, g_post_mlp):
    bsz, seq, _ = x.shape
    assert seq % ROW_TILE == 0 and ROW_TILE % SLAB_T == 0 and bsz <= SUBLANES
    assert seq % PROJ_TILE == 0 and PROJ_TILE % SLAB_T == 0
    assert N_META <= SSM_CHUNK
    row = lambda g: g.reshape(1, -1).astype(_F32)

    win, wkr_t, wu_t, wq_t, wk, wv_t = _prep_weights(w_in[0], w_uq[0], w_ukv[0])
    weights = (row(g_pre_mix[0]), win, wkr_t, wu_t, row(g_q_lat[0]), wq_t, row(g_kv_lat[0]), wk, wv_t)
    rope = _rope_tables(positions.astype(jnp.int32) + N_META, PROJ_TILE)
    qt, k, vt, u, q_norm, k_sq = _proj_call(x, rope, *weights, tile=PROJ_TILE, meta=False)
    meta_x = jnp.pad(meta_tokens.astype(x.dtype), ((0, LANES - N_META), (0, 0)))[None]
    rope_m = _rope_tables(jnp.arange(LANES, dtype=jnp.int32)[None], LANES)
    k_m, vt_m, u_m = _proj_call(meta_x, rope_m, *weights, tile=LANES, meta=True)
    km, vtm, u_m = k_m[0], vt_m[0, :, 0], u_m[:, :N_META]
    k_m_sq = jnp.max(jnp.sum(jnp.square(k_m[0].astype(_F32)), axis=-1), axis=-1)
    k_max = jnp.sqrt(jnp.maximum(jnp.max(k_sq, axis=(2, 3, 4)), k_m_sq[None]))
    mrow = q_norm * (k_max * BOUND_SLACK)[:, :, None, None, None]
    attn = lax.cond(jnp.max(mrow) <= BOUND_LIMIT,
                    lambda: _attn_call(qt, mrow, k, vt, km, vtm, bounded=True),
                    lambda: _attn_call(qt, mrow, k, vt, km, vtm, bounded=False))

    um = u_m[0].astype(_BF).reshape(N_META, SSM_GROUPS, SSM_GROUP).transpose(1, 2, 0)
    um = jnp.pad(um, ((0, 0), (0, 0), (SSM_CHUNK - N_META, 0))).reshape(SSM_GROUPS, 1, CHUNK_W)
    um = jnp.broadcast_to(um, (SSM_GROUPS, SUBLANES, CHUNK_W))
    lags, w_mat, v_mat, tab = _s5_matrices(ssm_A_re[0], ssm_A_im[0], ssm_log_dt[0], ssm_B_re[0],
                                            ssm_B_im[0], ssm_C_re[0], ssm_C_im[0], ssm_D[0])
    nslab = seq // SLAB_T
    yg = _s5_call(u.reshape(SSM_GROUPS, nslab * bsz * SSM_GROUP, SLAB_T), um, _chunk_permutation(),
                  lags, w_mat.astype(_BF), v_mat.astype(_BF), tab, nslab=nslab, bsz=bsz)
    y = yg.reshape(SSM_GROUPS, nslab, bsz * SSM_GROUP, SLAB_T)

    return _post_call(x, attn, y, w_glu[0].astype(_BF), row(g_mix_out[0]), w_out[0].astype(_BF),
                      row(g_post_mix[0]), row(g_pre_mlp[0]), w_mlp_up[0].astype(_BF),
                      w_mlp_down[0].astype(_BF), row(g_post_mlp[0]), tile=ROW_TILE)
```

```python
import functools
import math

import jax
import jax.numpy as jnp
from jax import lax
from jax.experimental import pallas as pl
from jax.experimental.pallas import tpu as pltpu

D_MODEL = 1024
N_META = 16
ATTN_WIDTH = 512
SSM_WIDTH = 512
N_HEADS = 8
V_HEAD_DIM = 64
QK_NOPE_DIM = 64
QK_ROPE_DIM = 32
QK_HEAD_DIM = QK_NOPE_DIM + QK_ROPE_DIM
Q_LORA_RANK = 384
KV_LORA_RANK = 256
ROPE_BASE = 10000.0
SSM_GROUP = 16
SSM_GROUPS = 32
SSM_STATE = 64
D_FF = 4 * D_MODEL
EPS = 1e-6
OFF_KV = Q_LORA_RANK
OFF_KR = OFF_KV + KV_LORA_RANK
OFF_U = OFF_KR + QK_ROPE_DIM

LANES = 128
SUBLANES = 8
HEAD_PAD = LANES
ONES_LANE = V_HEAD_DIM
ATTN_TQ = 1024
ATTN_TK = 256
V_ROWS = HEAD_PAD
PW_Q = 0
PW_KV = PW_Q + Q_LORA_RANK
PW_END = PW_KV + KV_LORA_RANK

SSM_CHUNK = 32
CHUNK_W = SSM_CHUNK * SSM_GROUP
STATE_W = 4 * SSM_STATE
SLAB_T = LANES
SLAB_CHUNKS = SLAB_T // SSM_CHUNK
SLAB_W = SSM_GROUP * SLAB_T

ROW_TILE = 512
PROJ_TILE = 1024
FF_TILE = 1024
POST_SLABS = 2
BOUND_SLACK = 1.0 + 2.0 ** -6
BOUND_LIMIT = -1.0
VMEM_LIMIT = 56 * 1024 * 1024

_BF = jnp.bfloat16
_F32 = jnp.float32


def _dot(a, b):
    return jnp.dot(a, b, preferred_element_type=_F32)


def _rms(x, g):
    return x * lax.rsqrt(jnp.mean(x * x, axis=-1, keepdims=True) + EPS) * g


_NT = (((1,), (1,)), ((), ()))


def _rotate(x1, x2, cos_t, sin_t):
    return x1 * cos_t - x2 * sin_t, x1 * sin_t + x2 * cos_t


def _proj_kernel(x_ref, rope_ref, *refs, meta):
    if meta:
        gpre_ref, win_ref, wkr_ref, wu_ref, gkv_ref, wk_ref, wv_ref, k_ref, v_ref, u_ref = refs
    else:
        (gpre_ref, win_ref, wkr_ref, wu_ref, gq_ref, wq_ref, gkv_ref, wk_ref, wv_ref,
         q_ref, k_ref, v_ref, u_ref, qn_ref, kmx_ref) = refs
    tile = x_ref.shape[0]
    half = QK_ROPE_DIM // 2
    cos_t, sin_t = rope_ref[0], rope_ref[1]
    xn = _rms(x_ref[...], gpre_ref[...]).astype(_BF)
    proj = _dot(xn, win_ref[...])
    kvn = _rms(proj[:, PW_KV:PW_END], gkv_ref[...]).astype(_BF)
    krt = lax.dot_general(wkr_ref[...], xn, _NT, preferred_element_type=_F32)
    r1, r2 = _rotate(krt[:half], krt[half:], cos_t, sin_t)
    kr = jnp.concatenate([jnp.zeros((QK_NOPE_DIM, tile), _F32), r1, r2,
                          jnp.zeros((HEAD_PAD - QK_HEAD_DIM, tile), _F32)], axis=0).T
    kk = _dot(kvn, wk_ref[...])
    vt = lax.dot_general(wv_ref[...], kvn, _NT, preferred_element_type=_F32)
    ones_tail = (lax.broadcasted_iota(jnp.int32, (HEAD_PAD - V_HEAD_DIM, tile), 0) == 0).astype(_F32)
    if meta:
        u_ref[...] = lax.dot_general(xn, wu_ref[...], _NT, preferred_element_type=_F32)
    else:
        ut = lax.dot_general(wu_ref[...], xn, _NT, preferred_element_type=_F32)
        for g in range(SSM_GROUPS):
            for c in range(tile // SLAB_T):
                u_ref[g, c] = ut[g * SSM_GROUP:(g + 1) * SSM_GROUP, c * SLAB_T:(c + 1) * SLAB_T]
        qn = _rms(proj[:, PW_Q:PW_KV], gq_ref[...]).astype(_BF)
        qt = lax.dot_general(wq_ref[...], qn, _NT, preferred_element_type=_F32)
        zero_rows = jnp.zeros((HEAD_PAD - QK_HEAD_DIM, tile), _F32)
    for h in range(N_HEADS):
        k_h = (kk[:, h * HEAD_PAD:(h + 1) * HEAD_PAD] + kr).astype(_BF)
        k_ref[h] = k_h
        v_ref[h] = jnp.concatenate([vt[h * V_HEAD_DIM:(h + 1) * V_HEAD_DIM], ones_tail], axis=0).astype(_BF)
        if not meta:
            blk = qt[h * QK_HEAD_DIM:(h + 1) * QK_HEAD_DIM]
            r1, r2 = _rotate(blk[QK_NOPE_DIM:QK_NOPE_DIM + half], blk[QK_NOPE_DIM + half:], cos_t, sin_t)
            qt_h = jnp.concatenate([blk[:QK_NOPE_DIM], r1, r2, zero_rows], axis=0).astype(_BF)
            q_ref[h] = qt_h
            qt_f = qt_h.astype(_F32)
            qn_ref[h] = jnp.sqrt(jnp.sum(qt_f * qt_f, axis=0, keepdims=True))
            k_f = k_h.astype(_F32)
            kmx_ref[h] = jnp.broadcast_to(jnp.max(jnp.sum(k_f * k_f, axis=1, keepdims=True), axis=0, keepdims=True),
                                          (1, LANES))


def _const_spec(shape):
    nd = len(shape)
    return pl.BlockSpec(shape, lambda *_: (0,) * nd)


def _proj_call(x, rope, gpre, win, wkr_t, wu_t, gq, wq_t, gkv, wk, wv_t, *, tile, meta):
    bsz, seq, _ = x.shape
    nt = seq // tile
    row_spec = lambda w: pl.BlockSpec((None, tile, w), lambda b, i: (b, i, 0))
    rope_spec = pl.BlockSpec((None, None, 2, QK_ROPE_DIM // 2, tile), lambda b, i: (b, i, 0, 0, 0))
    k_spec = pl.BlockSpec((None, N_HEADS, tile, HEAD_PAD), lambda b, i: (b, 0, i, 0))
    t_spec = pl.BlockSpec((None, N_HEADS, None, HEAD_PAD, tile), lambda b, i: (b, 0, i, 0, 0))
    k_shape = jax.ShapeDtypeStruct((bsz, N_HEADS, seq, HEAD_PAD), _BF)
    t_shape = jax.ShapeDtypeStruct((bsz, N_HEADS, nt, HEAD_PAD, tile), _BF)
    w_specs = lambda *ws: [_const_spec(w.shape) for w in ws]
    if meta:
        args = (x, rope, gpre, win, wkr_t, wu_t, gkv, wk, wv_t)
        in_specs = [row_spec(D_MODEL), rope_spec] + w_specs(*args[2:])
        out_specs = [k_spec, t_spec, row_spec(SSM_WIDTH)]
        out_shape = [k_shape, t_shape, jax.ShapeDtypeStruct((bsz, seq, SSM_WIDTH), _F32)]
    else:
        args = (x, rope, gpre, win, wkr_t, wu_t, gq, wq_t, gkv, wk, wv_t)
        in_specs = [row_spec(D_MODEL), rope_spec] + w_specs(*args[2:])
        norm_spec = lambda w: pl.BlockSpec((None, N_HEADS, None, 1, w), lambda b, i: (b, 0, i, 0, 0))
        u_spec = pl.BlockSpec((SSM_GROUPS, tile // SLAB_T, SSM_GROUP, SLAB_T), lambda b, i: (0, i, b, 0))
        out_specs = [t_spec, k_spec, t_spec, u_spec, norm_spec(tile), norm_spec(LANES)]
        out_shape = [t_shape, k_shape, t_shape,
                     jax.ShapeDtypeStruct((SSM_GROUPS, seq // SLAB_T, bsz * SSM_GROUP, SLAB_T), _F32),
                     jax.ShapeDtypeStruct((bsz, N_HEADS, nt, 1, tile), _F32),
                     jax.ShapeDtypeStruct((bsz, N_HEADS, nt, 1, LANES), _F32)]
    return pl.pallas_call(
        functools.partial(_proj_kernel, meta=meta),
        grid=(bsz, nt),
        in_specs=in_specs,
        out_specs=out_specs,
        out_shape=out_shape,
        compiler_params=pltpu.CompilerParams(
            dimension_semantics=("parallel", "parallel"), vmem_limit_bytes=VMEM_LIMIT),
        name="proj_meta" if meta else "proj",
    )(*args)


def _attn_finish(accs, o_ref):
    halves = [(acc * (1.0 / acc[ONES_LANE:ONES_LANE + 1, :]))[:V_HEAD_DIM] for acc in accs]
    o_ref[...] = jnp.concatenate(halves, axis=0).T


def _lane_concat(ref, hh):
    return jnp.concatenate([ref[hh, j] for j in range(ref.shape[1])], axis=1)


def _attn_bounded_kernel(qt_ref, mrow_ref, k_ref, vt_ref, km_ref, vtm_ref, o_ref, *, nk, tk):
    tq = o_ref.shape[0]
    per_slab = vt_ref.shape[3] // tk
    key_row = lax.broadcasted_iota(jnp.int32, (LANES, tq), 0)
    accs = []
    for hh in range(2):
        qt = _lane_concat(qt_ref, hh)
        mrow = _lane_concat(mrow_ref, hh)
        s0 = jnp.where(key_row < N_META, _dot(km_ref[hh], qt), -jnp.inf)
        acc = _dot(vtm_ref[hh, :V_ROWS, :], jnp.exp2(s0 - mrow).astype(_BF))
        scores = lambda c: _dot(k_ref[hh, c * tk:(c + 1) * tk, :], qt)
        s_next = scores(0)
        for c in range(nk):
            s = s_next
            if c + 1 < nk:
                s_next = scores(c + 1)
            vt_c = vt_ref[hh, c // per_slab, :V_ROWS, (c % per_slab) * tk:(c % per_slab + 1) * tk]
            acc = acc + _dot(vt_c, jnp.exp2(s - mrow).astype(_BF))
        accs.append(acc)
    _attn_finish(accs, o_ref)


def _attn_online_kernel(qt_ref, k_ref, vt_ref, km_ref, vtm_ref, o_ref, s0_scr, s1_scr, m_scr, acc_scr, *, nk, tk):
    tq = o_ref.shape[0]
    key_row = lax.broadcasted_iota(jnp.int32, (LANES, tq), 0)
    for hh in range(2):
        s0 = jnp.where(key_row < N_META, _dot(km_ref[hh], _lane_concat(qt_ref, hh)), -jnp.inf)
        m0 = jnp.max(s0, axis=0, keepdims=True)
        m_scr[hh] = m0
        acc_scr[hh] = _dot(vtm_ref[hh, :V_ROWS, :], jnp.exp2(s0 - m0).astype(_BF))

    def scores(buf, c):
        off = pl.multiple_of(c * tk, tk)
        for hh in range(2):
            buf[hh] = _dot(k_ref[hh, pl.ds(off, tk), :], _lane_concat(qt_ref, hh))

    def accumulate(buf, c):
        for hh in range(2):
            s = buf[hh]
            m = m_scr[hh]
            m_new = jnp.maximum(m, jnp.max(s, axis=0, keepdims=True))
            m_scr[hh] = m_new
            p = jnp.exp2(s - m_new).astype(_BF)
            acc_scr[hh] = jnp.exp2(m - m_new) * acc_scr[hh] + _dot(vt_ref[hh, c, :V_ROWS, :], p)

    scores(s0_scr, 0)

    def body(t, _):
        scores(s1_scr, 2 * t + 1)
        accumulate(s0_scr, 2 * t)
        scores(s0_scr, 2 * t + 2)
        accumulate(s1_scr, 2 * t + 1)
        return 0

    lax.fori_loop(0, nk // 2 - 1, body, 0)
    scores(s1_scr, nk - 1)
    accumulate(s0_scr, nk - 2)
    accumulate(s1_scr, nk - 1)
    _attn_finish([acc_scr[0], acc_scr[1]], o_ref)


def _attn_call(qt, mrow, k, vt, km, vtm, *, bounded):
    bsz, _, nslab, _, slab = qt.shape
    seq = k.shape[2]
    tq = min(ATTN_TQ, seq)
    tk = min(ATTN_TK, slab)
    qs = tq // slab
    assert tq % slab == 0 and seq % tq == 0 and slab % tk == 0
    q_spec = lambda rows: pl.BlockSpec((None, 2, qs, rows, slab), lambda b, hp, i: (b, hp, i, 0, 0))
    kv_specs = [pl.BlockSpec((None, 2, seq, HEAD_PAD), lambda b, hp, i: (b, hp, 0, 0)),
                pl.BlockSpec((None, 2, nslab, HEAD_PAD, slab), lambda b, hp, i: (b, hp, 0, 0, 0)),
                pl.BlockSpec((2, LANES, HEAD_PAD), lambda b, hp, i: (hp, 0, 0)),
                pl.BlockSpec((2, HEAD_PAD, LANES), lambda b, hp, i: (hp, 0, 0))]
    if bounded:
        body = functools.partial(_attn_bounded_kernel, nk=seq // tk, tk=tk)
        in_specs = [q_spec(HEAD_PAD), q_spec(1)] + kv_specs
        args, scratch = (qt, mrow, k, vt, km, vtm), []
    else:
        assert nslab % 2 == 0 and nslab >= 4
        body = functools.partial(_attn_online_kernel, nk=nslab, tk=slab)
        in_specs = [q_spec(HEAD_PAD)] + kv_specs
        args = (qt, k, vt, km, vtm)
        scratch = [pltpu.VMEM((2, slab, tq), _F32), pltpu.VMEM((2, slab, tq), _F32),
                   pltpu.VMEM((2, 1, tq), _F32), pltpu.VMEM((2, V_ROWS, tq), _F32)]
    return pl.pallas_call(
        body,
        grid=(bsz, N_HEADS // 2, seq // tq),
        in_specs=in_specs,
        out_specs=pl.BlockSpec((None, tq, LANES), lambda b, hp, i: (b, i, hp)),
        out_shape=jax.ShapeDtypeStruct((bsz, seq, ATTN_WIDTH), _F32),
        scratch_shapes=scratch,
        compiler_params=pltpu.CompilerParams(
            dimension_semantics=("parallel", "parallel", "arbitrary"), vmem_limit_bytes=VMEM_LIMIT),
        name="attn_bounded" if bounded else "attn_online",
    )(*args)


def _cmul_add(ar, ai, xr, xi, sr, si):
    return ar * xr - ai * xi + sr, ar * xi + ai * xr + si


def _s5_kernel(a_ref, um_ref, q_ref, lag_ref, w_ref, v_ref, t_ref, y_ref, sup_scr, ent_scr,
               *, nslab, bsz):
    rows = nslab * bsz
    half = STATE_W // 2
    a = [a_ref[pl.ds(i, rows, stride=SSM_GROUP), :] for i in range(SSM_GROUP)]
    uc = [jnp.concatenate([x[:, c * SSM_CHUNK:(c + 1) * SSM_CHUNK] for x in a], axis=1).astype(_BF)
          for c in range(SLAB_CHUNKS)]
    w = w_ref[...]
    s = [_dot(u, w) for u in uc]
    sr = [x[:, :half] for x in s]
    si = [x[:, half:] for x in s]
    t = t_ref[...]
    trow = lambda r: t[r:r + 1, :]
    sup_r = sup_i = None
    for c in range(SLAB_CHUNKS):
        cr, ci = trow(c), trow(SLAB_CHUNKS + c)
        pr = cr * sr[c] - ci * si[c]
        pi = cr * si[c] + ci * sr[c]
        sup_r = pr if sup_r is None else sup_r + pr
        sup_i = pi if sup_i is None else sup_i + pi
    sup_scr[:, :half] = sup_r
    sup_scr[:, half:] = sup_i

    lane = lax.broadcasted_iota(jnp.int32, (bsz, half), 1)
    fwd = lane < SSM_STATE
    sm = _dot(um_ref[...], w)
    xr = jnp.where(fwd, sm[:bsz, :half], 0.0)
    xi = jnp.where(fwd, sm[:bsz, half:], 0.0)
    a_slab_r, a_slab_i = trow(2 * SLAB_CHUNKS), trow(2 * SLAB_CHUNKS + 1)
    for j in range(nslab):
        rf = j * bsz
        rb = (nslab - 1 - j) * bsz
        ent_scr[rf:rf + bsz, 0:SSM_STATE] = xr[:, 0:SSM_STATE]
        ent_scr[rb:rb + bsz, SSM_STATE:half] = xr[:, SSM_STATE:half]
        ent_scr[rf:rf + bsz, half:half + SSM_STATE] = xi[:, 0:SSM_STATE]
        ent_scr[rb:rb + bsz, half + SSM_STATE:STATE_W] = xi[:, SSM_STATE:half]
        s_r = jnp.where(fwd, sup_scr[rf:rf + bsz, :half], sup_scr[rb:rb + bsz, :half])
        s_i = jnp.where(fwd, sup_scr[rf:rf + bsz, half:], sup_scr[rb:rb + bsz, half:])
        xr, xi = _cmul_add(a_slab_r, a_slab_i, xr, xi, s_r, s_i)

    ent = ent_scr[...]
    a_r, a_i = trow(2 * SLAB_CHUNKS + 2), trow(2 * SLAB_CHUNKS + 3)
    xf = [(ent[:, :half], ent[:, half:])]
    for c in range(1, SLAB_CHUNKS):
        xf.append(_cmul_add(a_r, a_i, xf[-1][0], xf[-1][1], sr[c - 1], si[c - 1]))
    xb = [(ent[:, :half], ent[:, half:])]
    for c in range(SLAB_CHUNKS - 2, -1, -1):
        xb.insert(0, _cmul_add(a_r, a_i, xb[0][0], xb[0][1], sr[c + 1], si[c + 1]))
    fwd_rows = lax.broadcasted_iota(jnp.int32, (rows, half), 1) < SSM_STATE
    m = jnp.concatenate(
        [pltpu.roll(jnp.broadcast_to(lag_ref[i:i + 1, :], (SSM_CHUNK, 2 * CHUNK_W)), 0, 1,
                    stride=SSM_GROUP, stride_axis=0)[:, :CHUNK_W] for i in range(SSM_GROUP)], axis=0).astype(_BF)
    q = q_ref[...]
    m = _dot(m, q).astype(_BF)
    v = _dot(v_ref[...], q).astype(_BF)
    ys = []
    for c in range(SLAB_CHUNKS):
        xin = jnp.concatenate([jnp.where(fwd_rows, xf[c][0], xb[c][0]),
                               jnp.where(fwd_rows, xf[c][1], xb[c][1])], axis=1).astype(_BF)
        ys.append(_dot(uc[c], m) + _dot(xin, v))
    for o in range(SSM_GROUP):
        y_ref[pl.ds(o, rows, stride=SSM_GROUP), :] = jnp.concatenate(
            [y[:, o * SSM_CHUNK:(o + 1) * SSM_CHUNK] for y in ys], axis=1)


def _s5_call(a, um, perm, lags, w_mat, v_mat, tab, *, nslab, bsz):
    n = nslab * bsz * SSM_GROUP
    rows = nslab * bsz
    g_spec = lambda *shape: pl.BlockSpec((None,) + shape, lambda g: (g,) + (0,) * len(shape))
    perm_spec = pl.BlockSpec((CHUNK_W, CHUNK_W), lambda g: (0, 0), pipeline_mode=pl.Buffered(1))
    return pl.pallas_call(
        functools.partial(_s5_kernel, nslab=nslab, bsz=bsz),
        grid=(SSM_GROUPS,),
        in_specs=[g_spec(n, SLAB_T), g_spec(SUBLANES, CHUNK_W), perm_spec,
                  g_spec(SSM_GROUP, 2 * CHUNK_W), g_spec(CHUNK_W, STATE_W), g_spec(STATE_W, CHUNK_W),
                  g_spec(2 * SUBLANES, STATE_W // 2)],
        out_specs=g_spec(n, SLAB_T),
        out_shape=jax.ShapeDtypeStruct((SSM_GROUPS, n, SLAB_T), _F32),
        scratch_shapes=[pltpu.VMEM((rows, STATE_W), _F32), pltpu.VMEM((rows, STATE_W), _F32)],
        compiler_params=pltpu.CompilerParams(
            dimension_semantics=("parallel",), vmem_limit_bytes=VMEM_LIMIT),
        name="s5",
    )(a, um, perm, lags, w_mat, v_mat, tab)


def _s5_matrices(a_re, a_im, log_dt, b_re, b_im, c_re, c_im, d_skip):
    tc = SSM_CHUNK
    lam = lax.complex(jnp.minimum(a_re.astype(_F32), -1e-4), a_im.astype(_F32))
    dt = jnp.exp(log_dt.astype(_F32))[..., None]
    lam_dt = lam * dt
    lam_bar = jnp.exp(lam_dt)
    b_bar = ((lam_bar - 1.0) / lam)[..., None] * lax.complex(b_re.astype(_F32), b_im.astype(_F32))
    c_c = lax.complex(c_re.astype(_F32), c_im.astype(_F32))
    k_idx = jnp.arange(tc + 1, dtype=_F32)
    pw = jnp.exp(lam_dt[:, :, None, :] * k_idx[None, None, :, None])
    kern = jnp.real(jnp.einsum('dgop,dgkp,dgpi->dgkoi', c_c, pw[:, :, :tc], b_bar))
    d_g = d_skip.astype(_F32).reshape(SSM_GROUPS, SSM_GROUP)
    center = kern[0][:, :1] + kern[1][:, :1] + (jnp.eye(SSM_GROUP, dtype=_F32)[None] * d_g[:, :, None])[:, None]
    lags = jnp.concatenate([center, kern[0][:, 1:], jnp.zeros_like(center), kern[1][:, :0:-1]], axis=1)
    lags = lags.transpose(0, 3, 1, 2).reshape(SSM_GROUPS, SSM_GROUP, 2 * CHUNK_W)
    wf = b_bar[0].transpose(0, 2, 1)[:, :, None, :] * pw[0][:, tc - 1::-1][:, None, :, :]
    wb = b_bar[1].transpose(0, 2, 1)[:, :, None, :] * pw[1][:, :tc][:, None, :, :]
    w_mat = jnp.concatenate([jnp.real(wf), jnp.real(wb), jnp.imag(wf), jnp.imag(wb)], axis=-1)
    w_mat = w_mat.reshape(SSM_GROUPS, CHUNK_W, STATE_W)
    gf = pw[0][:, 1:tc + 1][:, :, None, :] * c_c[0][:, None, :, :]
    gb = pw[1][:, tc:0:-1][:, :, None, :] * c_c[1][:, None, :, :]
    v_mat = jnp.concatenate([jnp.real(gf), jnp.real(gb), -jnp.imag(gf), -jnp.imag(gb)], axis=-1)
    v_mat = v_mat.reshape(SSM_GROUPS, CHUNK_W, STATE_W).transpose(0, 2, 1)
    n_idx = jnp.arange(SLAB_CHUNKS + 1, dtype=_F32) * tc
    pc = jnp.exp(lam_dt[:, :, None, :] * n_idx[None, None, :, None])
    coef = jnp.concatenate([pc[0][:, SLAB_CHUNKS - 1::-1], pc[1][:, :SLAB_CHUNKS]], axis=-1)
    both = lambda n: jnp.concatenate([pc[0][:, n], pc[1][:, n]], axis=-1)[:, None, :]
    a_slab, a_chunk = both(SLAB_CHUNKS), both(1)
    tab = jnp.concatenate([jnp.real(coef), jnp.imag(coef), jnp.real(a_slab), jnp.imag(a_slab),
                           jnp.real(a_chunk), jnp.imag(a_chunk)], axis=1)
    tab = jnp.pad(tab, ((0, 0), (0, 2 * SUBLANES - tab.shape[1]), (0, 0)))
    return lags, w_mat, v_mat, tab


def _chunk_permutation():
    r = lax.broadcasted_iota(jnp.int32, (CHUNK_W, CHUNK_W), 0)
    c = lax.broadcasted_iota(jnp.int32, (CHUNK_W, CHUNK_W), 1)
    return (r == (c % SSM_CHUNK) * SSM_GROUP + c // SSM_CHUNK).astype(_BF)


def _post_kernel(x_ref, attn_ref, y_ref, wglu_ref, gmix_ref, wout_ref, gpm_ref, gpre_ref,
                 wup_ref, wdn_ref, gpost_ref, o_ref):
    nslab = y_ref.shape[1]
    gmix = gmix_ref[...]
    per = min(POST_SLABS, nslab)
    blocks = range(nslab // per)
    rows = [pl.ds(r * per * SLAB_T, per * SLAB_T) for r in blocks]
    gy = []
    for r in blocks:
        yt = jnp.concatenate([jnp.concatenate([y_ref[g, r * per + c] for c in range(per)], axis=1)
                              for g in range(SSM_GROUPS)], axis=0)
        y = yt.T
        gy.append((0.5 * y * (1.0 + jnp.tanh(math.sqrt(2.0 / math.pi) * (y + 0.044715 * (y * y * y))))).astype(_BF))
    z = [_dot(gy[r], wglu_ref[...]) for r in blocks]
    mix = []
    for r in blocks:
        ssm = z[r][:, :SSM_WIDTH] * (1.0 / (1.0 + jnp.exp(-z[r][:, SSM_WIDTH:])))
        mix.append(jnp.concatenate([_rms(attn_ref[rows[r], :], gmix[:, :ATTN_WIDTH]),
                                    _rms(ssm, gmix[:, ATTN_WIDTH:])], axis=-1).astype(_BF))
    mixed = [_dot(mix[r], wout_ref[...]) for r in blocks]
    h1 = [x_ref[rows[r], :] + _rms(mixed[r], gpm_ref[...]) for r in blocks]
    hn = [_rms(h1[r], gpre_ref[...]).astype(_BF) for r in blocks]
    acc = [None for _ in blocks]
    for c in range(D_FF // FF_TILE):
        for r in blocks:
            up = jnp.maximum(_dot(hn[r], wup_ref[:, c * FF_TILE:(c + 1) * FF_TILE]), 0.0)
            part = _dot((up * up).astype(_BF), wdn_ref[c * FF_TILE:(c + 1) * FF_TILE, :])
            acc[r] = part if acc[r] is None else acc[r] + part
    for r in blocks:
        o_ref[rows[r], :] = h1[r] + _rms(acc[r], gpost_ref[...])


def _post_call(x, attn, y, wglu, gmix, wout, gpm, gpre, wup, wdn, gpost, *, tile):
    bsz, seq, _ = x.shape
    row_spec = lambda w: pl.BlockSpec((None, tile, w), lambda b, i: (b, i, 0))
    wspec = lambda shape: pl.BlockSpec(shape, lambda b, i: (0, 0), pipeline_mode=pl.Buffered(1))
    y_spec = pl.BlockSpec((SSM_GROUPS, tile // SLAB_T, SSM_GROUP, SLAB_T), lambda b, i: (0, i, b, 0))
    return pl.pallas_call(
        _post_kernel,
        grid=(bsz, seq // tile),
        in_specs=[row_spec(D_MODEL), row_spec(ATTN_WIDTH), y_spec,
                  wspec((SSM_WIDTH, 2 * SSM_WIDTH)), wspec((1, D_MODEL)), wspec((D_MODEL, D_MODEL)),
                  wspec((1, D_MODEL)), wspec((1, D_MODEL)), wspec((D_MODEL, D_FF)),
                  wspec((D_FF, D_MODEL)), wspec((1, D_MODEL))],
        out_specs=row_spec(D_MODEL),
        out_shape=jax.ShapeDtypeStruct((bsz, seq, D_MODEL), _F32),
        compiler_params=pltpu.CompilerParams(
            dimension_semantics=("parallel", "parallel"), vmem_limit_bytes=VMEM_LIMIT),
        name="post",
    )(x, attn, y, wglu, gmix, wout, gpm, gpre, wup, wdn, gpost)


def _rope_tables(pos, tile):
    half = QK_ROPE_DIM // 2
    inv = 1.0 / (ROPE_BASE ** (jnp.arange(0, QK_ROPE_DIM, 2, dtype=_F32) / QK_ROPE_DIM))
    ang = pos.astype(_F32)[:, None, :] * inv[None, :, None]
    bsz, seq = pos.shape
    rope = jnp.stack([jnp.cos(ang), jnp.sin(ang)], axis=1)
    return rope.reshape(bsz, 2, half, seq // tile, tile).transpose(0, 3, 1, 2, 4)


def _prep_weights(w_in, w_uq, w_ukv):
    scale = QK_HEAD_DIM ** -0.5 * math.log2(math.e)
    win = w_in[:, :OFF_KR].astype(_BF)
    wkr_t = w_in[:, OFF_KR:OFF_U].T.astype(_BF)
    wu_t = w_in[:, OFF_U:].T.astype(_BF)
    wq_t = (w_uq * scale).T.astype(_BF)
    wkv3 = w_ukv.reshape(KV_LORA_RANK, N_HEADS, QK_NOPE_DIM + V_HEAD_DIM)
    wk = jnp.concatenate([wkv3[..., :QK_NOPE_DIM],
                          jnp.zeros((KV_LORA_RANK, N_HEADS, HEAD_PAD - QK_NOPE_DIM), _F32)], axis=-1)
    wk = wk.reshape(KV_LORA_RANK, N_HEADS * HEAD_PAD).astype(_BF)
    wv_t = wkv3[..., QK_NOPE_DIM:].reshape(KV_LORA_RANK, N_HEADS * V_HEAD_DIM).T.astype(_BF)
    return win, wkr_t, wu_t, wq_t, wk, wv_t


def kernel(x, positions, meta_tokens, g_pre_mix, w_in, g_q_lat, w_uq, g_kv_lat, w_ukv,
           ssm_A_re, ssm_A_im, ssm_log_dt, ssm_B_re, ssm_B_im, ssm_C_re, ssm_C_im, ssm_D,
           w_glu, g_mix_out, w_out, g_post_mix, g_pre_mlp, w_mlp_up, w_mlp_down, g_post_mlp):
    bsz, seq, _ = x.shape
    assert seq % ROW_TILE == 0 and ROW_TILE % SLAB_T == 0 and bsz <= SUBLANES
    assert seq % PROJ_TILE == 0 and PROJ_TILE % SLAB_T == 0
    assert N_META <= SSM_CHUNK
    row = lambda g: g.reshape(1, -1).astype(_F32)

    win, wkr_t, wu_t, wq_t, wk, wv_t = _prep_weights(w_in[0], w_uq[0], w_ukv[0])
    weights = (row(g_pre_mix[0]), win, wkr_t, wu_t, row(g_q_lat[0]), wq_t, row(g_kv_lat[0]), wk, wv_t)
    rope = _rope_tables(positions.astype(jnp.int32) + N_META, PROJ_TILE)
    qt, k, vt, u, q_norm, k_sq = _proj_call(x, rope, *weights, tile=PROJ_TILE, meta=False)
    meta_x = jnp.pad(meta_tokens.astype(x.dtype), ((0, LANES - N_META), (0, 0)))[None]
    rope_m = _rope_tables(jnp.arange(LANES, dtype=jnp.int32)[None], LANES)
    k_m, vt_m, u_m = _proj_call(meta_x, rope_m, *weights, tile=LANES, meta=True)
    km, vtm, u_m = k_m[0], vt_m[0, :, 0], u_m[:, :N_META]
    k_m_sq = jnp.max(jnp.sum(jnp.square(k_m[0].astype(_F32)), axis=-1), axis=-1)
    k_max = jnp.sqrt(jnp.maximum(jnp.max(k_sq, axis=(2, 3, 4)), k_m_sq[None]))
    mrow = q_norm * (k_max * BOUND_SLACK)[:, :, None, None, None]
    attn = lax.cond(jnp.max(mrow) <= BOUND_LIMIT,
                    lambda: _attn_call(qt, mrow, k, vt, km, vtm, bounded=True),
                    lambda: _attn_call(qt, mrow, k, vt, km, vtm, bounded=False))

    um = u_m[0].astype(_BF).reshape(N_META, SSM_GROUPS, SSM_GROUP).transpose(1, 2, 0)
    um = jnp.pad(um, ((0, 0), (0, 0), (SSM_CHUNK - N_META, 0))).reshape(SSM_GROUPS, 1, CHUNK_W)
    um = jnp.broadcast_to(um, (SSM_GROUPS, SUBLANES, CHUNK_W))
    lags, w_mat, v_mat, tab = _s5_matrices(ssm_A_re[0], ssm_A_im[0], ssm_log_dt[0], ssm_B_re[0],
                                            ssm_B_im[0], ssm_C_re[0], ssm_C_im[0], ssm_D[0])
    nslab = seq // SLAB_T
    yg = _s5_call(u.reshape(SSM_GROUPS, nslab * bsz * SSM_GROUP, SLAB_T), um, _chunk_permutation(),
                  lags, w_mat.astype(_BF), v_mat.astype(_BF), tab, nslab=nslab, bsz=bsz)
    y = yg.reshape(SSM_GROUPS, nslab, bsz * SSM_GROUP, SLAB_T)

    return _post_call(x, attn, y, w_glu[0].astype(_BF), row(g_mix_out[0]), w_out[0].astype(_BF),
                      row(g_post_mix[0]), row(g_pre_mlp[0]), w_mlp_up[0].astype(_BF),
                      w_mlp_down[0].astype(_BF), row(g_post_mlp[0]), tile=ROW_TILE)
```

```python
import functools
import math

import jax
import jax.numpy as jnp
from jax import lax
from jax.experimental import pallas as pl
from jax.experimental.pallas import tpu as pltpu

D_MODEL = 1024
N_META = 16
ATTN_WIDTH = 512
SSM_WIDTH = 512
N_HEADS = 8
V_HEAD_DIM = 64
QK_NOPE_DIM = 64
QK_ROPE_DIM = 32
QK_HEAD_DIM = QK_NOPE_DIM + QK_ROPE_DIM
Q_LORA_RANK = 384
KV_LORA_RANK = 256
ROPE_BASE = 10000.0
SSM_GROUP = 16
SSM_GROUPS = 32
SSM_STATE = 64
D_FF = 4 * D_MODEL
EPS = 1e-6
OFF_KV = Q_LORA_RANK
OFF_KR = OFF_KV + KV_LORA_RANK
OFF_U = OFF_KR + QK_ROPE_DIM

LANES = 128
SUBLANES = 8
HEAD_PAD = LANES
ONES_LANE = V_HEAD_DIM
ATTN_TQ = 1024
ATTN_TK = 256
V_ROWS = 112
PW_Q = 0
PW_KV = PW_Q + Q_LORA_RANK
PW_END = PW_KV + KV_LORA_RANK

SSM_CHUNK = 32
CHUNK_W = SSM_CHUNK * SSM_GROUP
STATE_W = 4 * SSM_STATE
SLAB_T = LANES
SLAB_CHUNKS = SLAB_T // SSM_CHUNK
SLAB_W = SSM_GROUP * SLAB_T

ROW_TILE = 512
PROJ_TILE = 1024
FF_TILE = 1024
POST_SLABS = 2
BOUND_SLACK = 1.0 + 2.0 ** -6
BOUND_LIMIT = 60.0
VMEM_LIMIT = 56 * 1024 * 1024

_BF = jnp.bfloat16
_F32 = jnp.float32


def _dot(a, b):
    return jnp.dot(a, b, preferred_element_type=_F32)


def _rms(x, g):
    return x * lax.rsqrt(jnp.mean(x * x, axis=-1, keepdims=True) + EPS) * g


_NT = (((1,), (1,)), ((), ()))


def _rotate(x1, x2, cos_t, sin_t):
    return x1 * cos_t - x2 * sin_t, x1 * sin_t + x2 * cos_t


def _proj_kernel(x_ref, rope_ref, *refs, meta):
    if meta:
        gpre_ref, win_ref, wkr_ref, wu_ref, gkv_ref, wk_ref, wv_ref, k_ref, v_ref, u_ref = refs
    else:
        (gpre_ref, win_ref, wkr_ref, wu_ref, gq_ref, wq_ref, gkv_ref, wk_ref, wv_ref,
         q_ref, k_ref, v_ref, u_ref, qn_ref, kmx_ref) = refs
    tile = x_ref.shape[0]
    half = QK_ROPE_DIM // 2
    cos_t, sin_t = rope_ref[0], rope_ref[1]
    xn = _rms(x_ref[...], gpre_ref[...]).astype(_BF)
    proj = _dot(xn, win_ref[...])
    kvn = _rms(proj[:, PW_KV:PW_END], gkv_ref[...]).astype(_BF)
    krt = lax.dot_general(wkr_ref[...], xn, _NT, preferred_element_type=_F32)
    r1, r2 = _rotate(krt[:half], krt[half:], cos_t, sin_t)
    kr = jnp.concatenate([jnp.zeros((QK_NOPE_DIM, tile), _F32), r1, r2,
                          jnp.zeros((HEAD_PAD - QK_HEAD_DIM, tile), _F32)], axis=0).T
    kk = _dot(kvn, wk_ref[...])
    vt = lax.dot_general(wv_ref[...], kvn, _NT, preferred_element_type=_F32)
    ones_tail = (lax.broadcasted_iota(jnp.int32, (HEAD_PAD - V_HEAD_DIM, tile), 0) == 0).astype(_F32)
    if meta:
        u_ref[...] = lax.dot_general(xn, wu_ref[...], _NT, preferred_element_type=_F32)
    else:
        ut = lax.dot_general(wu_ref[...], xn, _NT, preferred_element_type=_F32)
        for g in range(SSM_GROUPS):
            for c in range(tile // SLAB_T):
                u_ref[g, c] = ut[g * SSM_GROUP:(g + 1) * SSM_GROUP, c * SLAB_T:(c + 1) * SLAB_T]
        qn = _rms(proj[:, PW_Q:PW_KV], gq_ref[...]).astype(_BF)
        qt = lax.dot_general(wq_ref[...], qn, _NT, preferred_element_type=_F32)
        zero_rows = jnp.zeros((HEAD_PAD - QK_HEAD_DIM, tile), _F32)
    for h in range(N_HEADS):
        k_h = (kk[:, h * HEAD_PAD:(h + 1) * HEAD_PAD] + kr).astype(_BF)
        k_ref[h] = k_h
        v_ref[h] = jnp.concatenate([vt[h * V_HEAD_DIM:(h + 1) * V_HEAD_DIM], ones_tail], axis=0).astype(_BF)
        if not meta:
            blk = qt[h * QK_HEAD_DIM:(h + 1) * QK_HEAD_DIM]
            r1, r2 = _rotate(blk[QK_NOPE_DIM:QK_NOPE_DIM + half], blk[QK_NOPE_DIM + half:], cos_t, sin_t)
            qt_h = jnp.concatenate([blk[:QK_NOPE_DIM], r1, r2, zero_rows], axis=0).astype(_BF)
            q_ref[h] = qt_h
            qt_f = qt_h.astype(_F32)
            qn_ref[h] = jnp.sqrt(jnp.sum(qt_f * qt_f, axis=0, keepdims=True))
            k_f = k_h.astype(_F32)
            kmx_ref[h] = jnp.broadcast_to(jnp.max(jnp.sum(k_f * k_f, axis=1, keepdims=True), axis=0, keepdims=True),
                                          (1, LANES))


def _const_spec(shape):
    nd = len(shape)
    return pl.BlockSpec(shape, lambda *_: (0,) * nd)


def _proj_call(x, rope, gpre, win, wkr_t, wu_t, gq, wq_t, gkv, wk, wv_t, *, tile, meta):
    bsz, seq, _ = x.shape
    nt = seq // tile
    row_spec = lambda w: pl.BlockSpec((None, tile, w), lambda b, i: (b, i, 0))
    rope_spec = pl.BlockSpec((None, None, 2, QK_ROPE_DIM // 2, tile), lambda b, i: (b, i, 0, 0, 0))
    k_spec = pl.BlockSpec((None, N_HEADS, tile, HEAD_PAD), lambda b, i: (b, 0, i, 0))
    t_spec = pl.BlockSpec((None, N_HEADS, None, HEAD_PAD, tile), lambda b, i: (b, 0, i, 0, 0))
    k_shape = jax.ShapeDtypeStruct((bsz, N_HEADS, seq, HEAD_PAD), _BF)
    t_shape = jax.ShapeDtypeStruct((bsz, N_HEADS, nt, HEAD_PAD, tile), _BF)
    w_specs = lambda *ws: [_const_spec(w.shape) for w in ws]
    if meta:
        args = (x, rope, gpre, win, wkr_t, wu_t, gkv, wk, wv_t)
        in_specs = [row_spec(D_MODEL), rope_spec] + w_specs(*args[2:])
        out_specs = [k_spec, t_spec, row_spec(SSM_WIDTH)]
        out_shape = [k_shape, t_shape, jax.ShapeDtypeStruct((bsz, seq, SSM_WIDTH), _F32)]
    else:
        args = (x, rope, gpre, win, wkr_t, wu_t, gq, wq_t, gkv, wk, wv_t)
        in_specs = [row_spec(D_MODEL), rope_spec] + w_specs(*args[2:])
        norm_spec = lambda w: pl.BlockSpec((None, N_HEADS, None, 1, w), lambda b, i: (b, 0, i, 0, 0))
        u_spec = pl.BlockSpec((SSM_GROUPS, tile // SLAB_T, SSM_GROUP, SLAB_T), lambda b, i: (0, i, b, 0))
        out_specs = [t_spec, k_spec, t_spec, u_spec, norm_spec(tile), norm_spec(LANES)]
        out_shape = [t_shape, k_shape, t_shape,
                     jax.ShapeDtypeStruct((SSM_GROUPS, seq // SLAB_T, bsz * SSM_GROUP, SLAB_T), _F32),
                     jax.ShapeDtypeStruct((bsz, N_HEADS, nt, 1, tile), _F32),
                     jax.ShapeDtypeStruct((bsz, N_HEADS, nt, 1, LANES), _F32)]
    return pl.pallas_call(
        functools.partial(_proj_kernel, meta=meta),
        grid=(bsz, nt),
        in_specs=in_specs,
        out_specs=out_specs,
        out_shape=out_shape,
        compiler_params=pltpu.CompilerParams(
            dimension_semantics=("parallel", "parallel"), vmem_limit_bytes=VMEM_LIMIT),
        name="proj_meta" if meta else "proj",
    )(*args)


def _attn_finish(accs, o_ref):
    halves = [(acc * (1.0 / acc[ONES_LANE:ONES_LANE + 1, :]))[:V_HEAD_DIM] for acc in accs]
    o_ref[...] = jnp.concatenate(halves, axis=0).T


def _lane_concat(ref, hh):
    return jnp.concatenate([ref[hh, j] for j in range(ref.shape[1])], axis=1)


def _attn_bounded_kernel(qt_ref, mrow_ref, k_ref, vt_ref, km_ref, vtm_ref, o_ref, *, nk, tk):
    tq = o_ref.shape[0]
    per_slab = vt_ref.shape[3] // tk
    key_row = lax.broadcasted_iota(jnp.int32, (LANES, tq), 0)
    accs = []
    for hh in range(2):
        qt = _lane_concat(qt_ref, hh)
        mrow = _lane_concat(mrow_ref, hh)
        s0 = jnp.where(key_row < N_META, _dot(km_ref[hh], qt), -jnp.inf)
        acc = _dot(vtm_ref[hh, :V_ROWS, :], jnp.exp2(s0 - mrow).astype(_BF))
        scores = lambda c: _dot(k_ref[hh, c * tk:(c + 1) * tk, :], qt)
        s_next = scores(0)
        for c in range(nk):
            s = s_next
            if c + 1 < nk:
                s_next = scores(c + 1)
            vt_c = vt_ref[hh, c // per_slab, :V_ROWS, (c % per_slab) * tk:(c % per_slab + 1) * tk]
            acc = acc + _dot(vt_c, jnp.exp2(s - mrow).astype(_BF))
        accs.append(acc)
    _attn_finish(accs, o_ref)


def _attn_online_kernel(qt_ref, k_ref, vt_ref, km_ref, vtm_ref, o_ref, s0_scr, s1_scr, m_scr, acc_scr, *, nk, tk):
    tq = o_ref.shape[0]
    key_row = lax.broadcasted_iota(jnp.int32, (LANES, tq), 0)
    for hh in range(2):
        s0 = jnp.where(key_row < N_META, _dot(km_ref[hh], _lane_concat(qt_ref, hh)), -jnp.inf)
        m0 = jnp.max(s0, axis=0, keepdims=True)
        m_scr[hh] = m0
        acc_scr[hh] = _dot(vtm_ref[hh, :V_ROWS, :], jnp.exp2(s0 - m0).astype(_BF))

    def scores(buf, c):
        off = pl.multiple_of(c * tk, tk)
        for hh in range(2):
            buf[hh] = _dot(k_ref[hh, pl.ds(off, tk), :], _lane_concat(qt_ref, hh))

    def accumulate(buf, c):
        for hh in range(2):
            s = buf[hh]
            m = m_scr[hh]
            m_new = jnp.maximum(m, jnp.max(s, axis=0, keepdims=True))
            m_scr[hh] = m_new
            p = jnp.exp2(s - m_new).astype(_BF)
            acc_scr[hh] = jnp.exp2(m - m_new) * acc_scr[hh] + _dot(vt_ref[hh, c, :V_ROWS, :], p)

    scores(s0_scr, 0)

    def body(t, _):
        scores(s1_scr, 2 * t + 1)
        accumulate(s0_scr, 2 * t)
        scores(s0_scr, 2 * t + 2)
        accumulate(s1_scr, 2 * t + 1)
        return 0

    lax.fori_loop(0, nk // 2 - 1, body, 0)
    scores(s1_scr, nk - 1)
    accumulate(s0_scr, nk - 2)
    accumulate(s1_scr, nk - 1)
    _attn_finish([acc_scr[0], acc_scr[1]], o_ref)


def _attn_call(qt, mrow, k, vt, km, vtm, *, bounded):
    bsz, _, nslab, _, slab = qt.shape
    seq = k.shape[2]
    tq = min(ATTN_TQ, seq)
    tk = min(ATTN_TK, slab)
    qs = tq // slab
    assert tq % slab == 0 and seq % tq == 0 and slab % tk == 0
    q_spec = lambda rows: pl.BlockSpec((None, 2, qs, rows, slab), lambda b, hp, i: (b, hp, i, 0, 0))
    kv_specs = [pl.BlockSpec((None, 2, seq, HEAD_PAD), lambda b, hp, i: (b, hp, 0, 0)),
                pl.BlockSpec((None, 2, nslab, HEAD_PAD, slab), lambda b, hp, i: (b, hp, 0, 0, 0)),
                pl.BlockSpec((2, LANES, HEAD_PAD), lambda b, hp, i: (hp, 0, 0)),
                pl.BlockSpec((2, HEAD_PAD, LANES), lambda b, hp, i: (hp, 0, 0))]
    if bounded:
        body = functools.partial(_attn_bounded_kernel, nk=seq // tk, tk=tk)
        in_specs = [q_spec(HEAD_PAD), q_spec(1)] + kv_specs
        args, scratch = (qt, mrow, k, vt, km, vtm), []
    else:
        assert nslab % 2 == 0 and nslab >= 4
        body = functools.partial(_attn_online_kernel, nk=nslab, tk=slab)
        in_specs = [q_spec(HEAD_PAD)] + kv_specs
        args = (qt, k, vt, km, vtm)
        scratch = [pltpu.VMEM((2, slab, tq), _F32), pltpu.VMEM((2, slab, tq), _F32),
                   pltpu.VMEM((2, 1, tq), _F32), pltpu.VMEM((2, V_ROWS, tq), _F32)]
    return pl.pallas_call(
        body,
        grid=(bsz, N_HEADS // 2, seq // tq),
        in_specs=in_specs,
        out_specs=pl.BlockSpec((None, tq, LANES), lambda b, hp, i: (b, i, hp)),
        out_shape=jax.ShapeDtypeStruct((bsz, seq, ATTN_WIDTH), _F32),
        scratch_shapes=scratch,
        compiler_params=pltpu.CompilerParams(
            dimension_semantics=("parallel", "parallel", "arbitrary"), vmem_limit_bytes=VMEM_LIMIT),
        name="attn_bounded" if bounded else "attn_online",
    )(*args)


def _cmul_add(ar, ai, xr, xi, sr, si):
    return ar * xr - ai * xi + sr, ar * xi + ai * xr + si


def _s5_kernel(a_ref, um_ref, q_ref, lag_ref, w_ref, v_ref, t_ref, y_ref, sup_scr, ent_scr,
               *, nslab, bsz):
    rows = nslab * bsz
    half = STATE_W // 2
    a = [a_ref[pl.ds(i, rows, stride=SSM_GROUP), :] for i in range(SSM_GROUP)]
    uc = [jnp.concatenate([x[:, c * SSM_CHUNK:(c + 1) * SSM_CHUNK] for x in a], axis=1).astype(_BF)
          for c in range(SLAB_CHUNKS)]
    w = w_ref[...]
    s = [_dot(u, w) for u in uc]
    sr = [x[:, :half] for x in s]
    si = [x[:, half:] for x in s]
    t = t_ref[...]
    trow = lambda r: t[r:r + 1, :]
    sup_r = sup_i = None
    for c in range(SLAB_CHUNKS):
        cr, ci = trow(c), trow(SLAB_CHUNKS + c)
        pr = cr * sr[c] - ci * si[c]
        pi = cr * si[c] + ci * sr[c]
        sup_r = pr if sup_r is None else sup_r + pr
        sup_i = pi if sup_i is None else sup_i + pi
    sup_scr[:, :half] = sup_r
    sup_scr[:, half:] = sup_i

    lane = lax.broadcasted_iota(jnp.int32, (bsz, half), 1)
    fwd = lane < SSM_STATE
    sm = _dot(um_ref[...], w)
    xr = jnp.where(fwd, sm[:bsz, :half], 0.0)
    xi = jnp.where(fwd, sm[:bsz, half:], 0.0)
    a_slab_r, a_slab_i = trow(2 * SLAB_CHUNKS), trow(2 * SLAB_CHUNKS + 1)
    for j in range(nslab):
        rf = j * bsz
        rb = (nslab - 1 - j) * bsz
        ent_scr[rf:rf + bsz, 0:SSM_STATE] = xr[:, 0:SSM_STATE]
        ent_scr[rb:rb + bsz, SSM_STATE:half] = xr[:, SSM_STATE:half]
        ent_scr[rf:rf + bsz, half:half + SSM_STATE] = xi[:, 0:SSM_STATE]
        ent_scr[rb:rb + bsz, half + SSM_STATE:STATE_W] = xi[:, SSM_STATE:half]
        s_r = jnp.where(fwd, sup_scr[rf:rf + bsz, :half], sup_scr[rb:rb + bsz, :half])
        s_i = jnp.where(fwd, sup_scr[rf:rf + bsz, half:], sup_scr[rb:rb + bsz, half:])
        xr, xi = _cmul_add(a_slab_r, a_slab_i, xr, xi, s_r, s_i)

    ent = ent_scr[...]
    a_r, a_i = trow(2 * SLAB_CHUNKS + 2), trow(2 * SLAB_CHUNKS + 3)
    xf = [(ent[:, :half], ent[:, half:])]
    for c in range(1, SLAB_CHUNKS):
        xf.append(_cmul_add(a_r, a_i, xf[-1][0], xf[-1][1], sr[c - 1], si[c - 1]))
    xb = [(ent[:, :half], ent[:, half:])]
    for c in range(SLAB_CHUNKS - 2, -1, -1):
        xb.insert(0, _cmul_add(a_r, a_i, xb[0][0], xb[0][1], sr[c + 1], si[c + 1]))
    fwd_rows = lax.broadcasted_iota(jnp.int32, (rows, half), 1) < SSM_STATE
    m = jnp.concatenate(
        [pltpu.roll(jnp.broadcast_to(lag_ref[i:i + 1, :], (SSM_CHUNK, 2 * CHUNK_W)), 0, 1,
                    stride=SSM_GROUP, stride_axis=0)[:, :CHUNK_W] for i in range(SSM_GROUP)], axis=0).astype(_BF)
    q = q_ref[...]
    m = _dot(m, q).astype(_BF)
    v = _dot(v_ref[...], q).astype(_BF)
    ys = []
    for c in range(SLAB_CHUNKS):
        xin = jnp.concatenate([jnp.where(fwd_rows, xf[c][0], xb[c][0]),
                               jnp.where(fwd_rows, xf[c][1], xb[c][1])], axis=1).astype(_BF)
        ys.append(_dot(uc[c], m) + _dot(xin, v))
    for o in range(SSM_GROUP):
        y_ref[pl.ds(o, rows, stride=SSM_GROUP), :] = jnp.concatenate(
            [y[:, o * SSM_CHUNK:(o + 1) * SSM_CHUNK] for y in ys], axis=1)


def _s5_call(a, um, perm, lags, w_mat, v_mat, tab, *, nslab, bsz):
    n = nslab * bsz * SSM_GROUP
    rows = nslab * bsz
    g_spec = lambda *shape: pl.BlockSpec((None,) + shape, lambda g: (g,) + (0,) * len(shape))
    perm_spec = pl.BlockSpec((CHUNK_W, CHUNK_W), lambda g: (0, 0), pipeline_mode=pl.Buffered(1))
    return pl.pallas_call(
        functools.partial(_s5_kernel, nslab=nslab, bsz=bsz),
        grid=(SSM_GROUPS,),
        in_specs=[g_spec(n, SLAB_T), g_spec(SUBLANES, CHUNK_W), perm_spec,
                  g_spec(SSM_GROUP, 2 * CHUNK_W), g_spec(CHUNK_W, STATE_W), g_spec(STATE_W, CHUNK_W),
                  g_spec(2 * SUBLANES, STATE_W // 2)],
        out_specs=g_spec(n, SLAB_T),
        out_shape=jax.ShapeDtypeStruct((SSM_GROUPS, n, SLAB_T), _F32),
        scratch_shapes=[pltpu.VMEM((rows, STATE_W), _F32), pltpu.VMEM((rows, STATE_W), _F32)],
        compiler_params=pltpu.CompilerParams(
            dimension_semantics=("parallel",), vmem_limit_bytes=VMEM_LIMIT),
        name="s5",
    )(a, um, perm, lags, w_mat, v_mat, tab)


def _s5_matrices(a_re, a_im, log_dt, b_re, b_im, c_re, c_im, d_skip):
    tc = SSM_CHUNK
    lam = lax.complex(jnp.minimum(a_re.astype(_F32), -1e-4), a_im.astype(_F32))
    dt = jnp.exp(log_dt.astype(_F32))[..., None]
    lam_dt = lam * dt
    lam_bar = jnp.exp(lam_dt)
    b_bar = ((lam_bar - 1.0) / lam)[..., None] * lax.complex(b_re.astype(_F32), b_im.astype(_F32))
    c_c = lax.complex(c_re.astype(_F32), c_im.astype(_F32))
    k_idx = jnp.arange(tc + 1, dtype=_F32)
    pw = jnp.exp(lam_dt[:, :, None, :] * k_idx[None, None, :, None])
    kern = jnp.real(jnp.einsum('dgop,dgkp,dgpi->dgkoi', c_c, pw[:, :, :tc], b_bar))
    d_g = d_skip.astype(_F32).reshape(SSM_GROUPS, SSM_GROUP)
    center = kern[0][:, :1] + kern[1][:, :1] + (jnp.eye(SSM_GROUP, dtype=_F32)[None] * d_g[:, :, None])[:, None]
    lags = jnp.concatenate([center, kern[0][:, 1:], jnp.zeros_like(center), kern[1][:, :0:-1]], axis=1)
    lags = lags.transpose(0, 3, 1, 2).reshape(SSM_GROUPS, SSM_GROUP, 2 * CHUNK_W)
    wf = b_bar[0].transpose(0, 2, 1)[:, :, None, :] * pw[0][:, tc - 1::-1][:, None, :, :]
    wb = b_bar[1].transpose(0, 2, 1)[:, :, None, :] * pw[1][:, :tc][:, None, :, :]
    w_mat = jnp.concatenate([jnp.real(wf), jnp.real(wb), jnp.imag(wf), jnp.imag(wb)], axis=-1)
    w_mat = w_mat.reshape(SSM_GROUPS, CHUNK_W, STATE_W)
    gf = pw[0][:, 1:tc + 1][:, :, None, :] * c_c[0][:, None, :, :]
    gb = pw[1][:, tc:0:-1][:, :, None, :] * c_c[1][:, None, :, :]
    v_mat = jnp.concatenate([jnp.real(gf), jnp.real(gb), -jnp.imag(gf), -jnp.imag(gb)], axis=-1)
    v_mat = v_mat.reshape(SSM_GROUPS, CHUNK_W, STATE_W).transpose(0, 2, 1)
    n_idx = jnp.arange(SLAB_CHUNKS + 1, dtype=_F32) * tc
    pc = jnp.exp(lam_dt[:, :, None, :] * n_idx[None, None, :, None])
    coef = jnp.concatenate([pc[0][:, SLAB_CHUNKS - 1::-1], pc[1][:, :SLAB_CHUNKS]], axis=-1)
    both = lambda n: jnp.concatenate([pc[0][:, n], pc[1][:, n]], axis=-1)[:, None, :]
    a_slab, a_chunk = both(SLAB_CHUNKS), both(1)
    tab = jnp.concatenate([jnp.real(coef), jnp.imag(coef), jnp.real(a_slab), jnp.imag(a_slab),
                           jnp.real(a_chunk), jnp.imag(a_chunk)], axis=1)
    tab = jnp.pad(tab, ((0, 0), (0, 2 * SUBLANES - tab.shape[1]), (0, 0)))
    return lags, w_mat, v_mat, tab


def _chunk_permutation():
    r = lax.broadcasted_iota(jnp.int32, (CHUNK_W, CHUNK_W), 0)
    c = lax.broadcasted_iota(jnp.int32, (CHUNK_W, CHUNK_W), 1)
    return (r == (c % SSM_CHUNK) * SSM_GROUP + c // SSM_CHUNK).astype(_BF)


def _post_kernel(x_ref, attn_ref, y_ref, wglu_ref, gmix_ref, wout_ref, gpm_ref, gpre_ref,
                 wup_ref, wdn_ref, gpost_ref, o_ref):
    nslab = y_ref.shape[1]
    gmix = gmix_ref[...]
    per = min(POST_SLABS, nslab)
    blocks = range(nslab // per)
    rows = [pl.ds(r * per * SLAB_T, per * SLAB_T) for r in blocks]
    gy = []
    for r in blocks:
        yt = jnp.concatenate([jnp.concatenate([y_ref[g, r * per + c] for c in range(per)], axis=1)
                              for g in range(SSM_GROUPS)], axis=0)
        y = yt.T
        gy.append((0.5 * y * (1.0 + jnp.tanh(math.sqrt(2.0 / math.pi) * (y + 0.044715 * (y * y * y))))).astype(_BF))
    z = [_dot(gy[r], wglu_ref[...]) for r in blocks]
    mix = []
    for r in blocks:
        ssm = z[r][:, :SSM_WIDTH] * (1.0 / (1.0 + jnp.exp(-z[r][:, SSM_WIDTH:])))
        mix.append(jnp.concatenate([_rms(attn_ref[rows[r], :], gmix[:, :ATTN_WIDTH]),
                                    _rms(ssm, gmix[:, ATTN_WIDTH:])], axis=-1).astype(_BF))
    mixed = [_dot(mix[r], wout_ref[...]) for r in blocks]
    h1 = [x_ref[rows[r], :] + _rms(mixed[r], gpm_ref[...]) for r in blocks]
    hn = [_rms(h1[r], gpre_ref[...]).astype(_BF) for r in blocks]
    acc = [None for _ in blocks]
    for c in range(D_FF // FF_TILE):
        for r in blocks:
            up = jnp.maximum(_dot(hn[r], wup_ref[:, c * FF_TILE:(c + 1) * FF_TILE]), 0.0)
            part = _dot((up * up).astype(_BF), wdn_ref[c * FF_TILE:(c + 1) * FF_TILE, :])
            acc[r] = part if acc[r] is None else acc[r] + part
    for r in blocks:
        o_ref[rows[r], :] = h1[r] + _rms(acc[r], gpost_ref[...])


def _post_call(x, attn, y, wglu, gmix, wout, gpm, gpre, wup, wdn, gpost, *, tile):
    bsz, seq, _ = x.shape
    row_spec = lambda w: pl.BlockSpec((None, tile, w), lambda b, i: (b, i, 0))
    wspec = lambda shape: pl.BlockSpec(shape, lambda b, i: (0, 0), pipeline_mode=pl.Buffered(1))
    y_spec = pl.BlockSpec((SSM_GROUPS, tile // SLAB_T, SSM_GROUP, SLAB_T), lambda b, i: (0, i, b, 0))
    return pl.pallas_call(
        _post_kernel,
        grid=(bsz, seq // tile),
        in_specs=[row_spec(D_MODEL), row_spec(ATTN_WIDTH), y_spec,
                  wspec((SSM_WIDTH, 2 * SSM_WIDTH)), wspec((1, D_MODEL)), wspec((D_MODEL, D_MODEL)),
                  wspec((1, D_MODEL)), wspec((1, D_MODEL)), wspec((D_MODEL, D_FF)),
                  wspec((D_FF, D_MODEL)), wspec((1, D_MODEL))],
        out_specs=row_spec(D_MODEL),
        out_shape=jax.ShapeDtypeStruct((bsz, seq, D_MODEL), _F32),
        compiler_params=pltpu.CompilerParams(
            dimension_semantics=("parallel", "parallel"), vmem_limit_bytes=VMEM_LIMIT),
        name="post",
    )(x, attn, y, wglu, gmix, wout, gpm, gpre, wup, wdn, gpost)


def _rope_tables(pos, tile):
    half = QK_ROPE_DIM // 2
    inv = 1.0 / (ROPE_BASE ** (jnp.arange(0, QK_ROPE_DIM, 2, dtype=_F32) / QK_ROPE_DIM))
    ang = pos.astype(_F32)[:, None, :] * inv[None, :, None]
    bsz, seq = pos.shape
    rope = jnp.stack([jnp.cos(ang), jnp.sin(ang)], axis=1)
    return rope.reshape(bsz, 2, half, seq // tile, tile).transpose(0, 3, 1, 2, 4)


def _prep_weights(w_in, w_uq, w_ukv):
    scale = QK_HEAD_DIM ** -0.5 * math.log2(math.e)
    win = w_in[:, :OFF_KR].astype(_BF)
    wkr_t = w_in[:, OFF_KR:OFF_U].T.astype(_BF)
    wu_t = w_in[:, OFF_U:].T.astype(_BF)
    wq_t = (w_uq * scale).T.astype(_BF)
    wkv3 = w_ukv.reshape(KV_LORA_RANK, N_HEADS, QK_NOPE_DIM + V_HEAD_DIM)
    wk = jnp.concatenate([wkv3[..., :QK_NOPE_DIM],
                          jnp.zeros((KV_LORA_RANK, N_HEADS, HEAD_PAD - QK_NOPE_DIM), _F32)], axis=-1)
    wk = wk.reshape(KV_LORA_RANK, N_HEADS * HEAD_PAD).astype(_BF)
    wv_t = wkv3[..., QK_NOPE_DIM:].reshape(KV_LORA_RANK, N_HEADS * V_HEAD_DIM).T.astype(_BF)
    return win, wkr_t, wu_t, wq_t, wk, wv_t


def kernel(x, positions, meta_tokens, g_pre_mix, w_in, g_q_lat, w_uq, g_kv_lat, w_ukv,
           ssm_A_re, ssm_A_im, ssm_log_dt, ssm_B_re, ssm_B_im, ssm_C_re, ssm_C_im, ssm_D,
           w_glu, g_mix_out, w_out, g_post_mix, g_pre_mlp, w_mlp_up, w_mlp_down, g_post_mlp):
    bsz, seq, _ = x.shape
    assert seq % ROW_TILE == 0 and ROW_TILE % SLAB_T == 0 and bsz <= SUBLANES
    assert seq % PROJ_TILE == 0 and PROJ_TILE % SLAB_T == 0
    assert N_META <= SSM_CHUNK
    row = lambda g: g.reshape(1, -1).astype(_F32)

    win, wkr_t, wu_t, wq_t, wk, wv_t = _prep_weights(w_in[0], w_uq[0], w_ukv[0])
    weights = (row(g_pre_mix[0]), win, wkr_t, wu_t, row(g_q_lat[0]), wq_t, row(g_kv_lat[0]), wk, wv_t)
    rope = _rope_tables(positions.astype(jnp.int32) + N_META, PROJ_TILE)
    qt, k, vt, u, q_norm, k_sq = _proj_call(x, rope, *weights, tile=PROJ_TILE, meta=False)
    meta_x = jnp.pad(meta_tokens.astype(x.dtype), ((0, LANES - N_META), (0, 0)))[None]
    rope_m = _rope_tables(jnp.arange(LANES, dtype=jnp.int32)[None], LANES)
    k_m, vt_m, u_m = _proj_call(meta_x, rope_m, *weights, tile=LANES, meta=True)
    km, vtm, u_m = k_m[0], vt_m[0, :, 0], u_m[:, :N_META]
    k_m_sq = jnp.max(jnp.sum(jnp.square(k_m[0].astype(_F32)), axis=-1), axis=-1)
    k_max = jnp.sqrt(jnp.maximum(jnp.max(k_sq, axis=(2, 3, 4)), k_m_sq[None]))
    mrow = q_norm * (k_max * BOUND_SLACK)[:, :, None, None, None]
    attn = lax.cond(jnp.max(mrow) <= BOUND_LIMIT,
                    lambda: _attn_call(qt, mrow, k, vt, km, vtm, bounded=True),
                    lambda: _attn_call(qt, mrow, k, vt, km, vtm, bounded=False))

    um = u_m[0].astype(_BF).reshape(N_META, SSM_GROUPS, SSM_GROUP).transpose(1, 2, 0)
    um = jnp.pad(um, ((0, 0), (0, 0), (SSM_CHUNK - N_META, 0))).reshape(SSM_GROUPS, 1, CHUNK_W)
    um = jnp.broadcast_to(um, (SSM_GROUPS, SUBLANES, CHUNK_W))
    lags, w_mat, v_mat, tab = _s5_matrices(ssm_A_re[0], ssm_A_im[0], ssm_log_dt[0], ssm_B_re[0],
                                            ssm_B_im[0], ssm_C_re[0], ssm_C_im[0], ssm_D[0])
    nslab = seq // SLAB_T
    yg = _s5_call(u.reshape(SSM_GROUPS, nslab * bsz * SSM_GROUP, SLAB_T), um, _chunk_permutation(),
                  lags, w_mat.astype(_BF), v_mat.astype(_BF), tab, nslab=nslab, bsz=bsz)
    y = yg.reshape(SSM_GROUPS, nslab, bsz * SSM_GROUP, SLAB_T)

    return _post_call(x, attn, y, w_glu[0].astype(_BF), row(g_mix_out[0]), w_out[0].astype(_BF),
                      row(g_post_mix[0]), row(g_pre_mlp[0]), w_mlp_up[0].astype(_BF),
                      w_mlp_down[0].astype(_BF), row(g_post_mlp[0]), tile=ROW_TILE)
```

```python
import functools
import math

import jax
import jax.numpy as jnp
from jax import lax
from jax.experimental import pallas as pl
from jax.experimental.pallas import tpu as pltpu

D_MODEL = 1024
N_META = 16
ATTN_WIDTH = 512
SSM_WIDTH = 512
N_HEADS = 8
V_HEAD_DIM = 64
QK_NOPE_DIM = 64
QK_ROPE_DIM = 32
QK_HEAD_DIM = QK_NOPE_DIM + QK_ROPE_DIM
Q_LORA_RANK = 384
KV_LORA_RANK = 256
ROPE_BASE = 10000.0
SSM_GROUP = 16
SSM_GROUPS = 32
SSM_STATE = 64
D_FF = 4 * D_MODEL
EPS = 1e-6
OFF_KR = Q_LORA_RANK + KV_LORA_RANK
OFF_U = OFF_KR + QK_ROPE_DIM

LANES = 128
SUBLANES = 8
HEAD_PAD = LANES
ONES_LANE = V_HEAD_DIM
ATTN_TQ = 1024
ATTN_TK = 256
V_ROWS = 112
PW_Q = 0
PW_KV = PW_Q + Q_LORA_RANK
PW_END = PW_KV + KV_LORA_RANK

SSM_CHUNK = 32
CHUNK_W = SSM_CHUNK * SSM_GROUP
STATE_W = 4 * SSM_STATE
SLAB_T = LANES
SLAB_CHUNKS = SLAB_T // SSM_CHUNK

ROW_TILE = 512
PROJ_TILE = 1024
FF_TILE = 1024
POST_SLABS = 2
BOUND_SLACK = 1.0 + 2.0 ** -6
BOUND_LIMIT = 60.0
V7X_VMEM_BYTES = 64 * 1024 * 1024
VMEM_LIMIT = V7X_VMEM_BYTES - 8 * 1024 * 1024

_BF = jnp.bfloat16
_F32 = jnp.float32


def _dot(a, b):
    return jnp.dot(a, b, preferred_element_type=_F32)


def _rms(x, g):
    return x * lax.rsqrt(jnp.mean(x * x, axis=-1, keepdims=True) + EPS) * g


_NT = (((1,), (1,)), ((), ()))


def _rotate(x1, x2, cos_t, sin_t):
    return x1 * cos_t - x2 * sin_t, x1 * sin_t + x2 * cos_t


def _proj_kernel(x_ref, rope_ref, *refs, meta):
    if meta:
        gpre_ref, win_ref, wkr_ref, wu_ref, gkv_ref, wk_ref, wv_ref, k_ref, v_ref, u_ref = refs
    else:
        (gpre_ref, win_ref, wkr_ref, wu_ref, gq_ref, wq_ref, gkv_ref, wk_ref, wv_ref,
         q_ref, k_ref, v_ref, u_ref, qn_ref, kmx_ref) = refs
    tile = x_ref.shape[0]
    half = QK_ROPE_DIM // 2
    cos_t, sin_t = rope_ref[0], rope_ref[1]
    xn = _rms(x_ref[...], gpre_ref[...]).astype(_BF)
    proj = _dot(xn, win_ref[...])
    kvn = _rms(proj[:, PW_KV:PW_END], gkv_ref[...]).astype(_BF)
    krt = lax.dot_general(wkr_ref[...], xn, _NT, preferred_element_type=_F32)
    r1, r2 = _rotate(krt[:half], krt[half:], cos_t, sin_t)
    kr = jnp.concatenate([jnp.zeros((QK_NOPE_DIM, tile), _F32), r1, r2,
                          jnp.zeros((HEAD_PAD - QK_HEAD_DIM, tile), _F32)], axis=0).T
    kk = _dot(kvn, wk_ref[...])
    vt = lax.dot_general(wv_ref[...], kvn, _NT, preferred_element_type=_F32)
    ones_tail = (lax.broadcasted_iota(jnp.int32, (HEAD_PAD - V_HEAD_DIM, tile), 0) == 0).astype(_F32)
    if meta:
        u_ref[...] = lax.dot_general(xn, wu_ref[...], _NT, preferred_element_type=_F32)
    else:
        ut = lax.dot_general(wu_ref[...], xn, _NT, preferred_element_type=_F32)
        for g in range(SSM_GROUPS):
            for c in range(tile // SLAB_T):
                u_ref[g, c] = ut[g * SSM_GROUP:(g + 1) * SSM_GROUP, c * SLAB_T:(c + 1) * SLAB_T]
        qn = _rms(proj[:, PW_Q:PW_KV], gq_ref[...]).astype(_BF)
        qt = lax.dot_general(wq_ref[...], qn, _NT, preferred_element_type=_F32)
        zero_rows = jnp.zeros((HEAD_PAD - QK_HEAD_DIM, tile), _F32)
    for h in range(N_HEADS):
        k_h = (kk[:, h * HEAD_PAD:(h + 1) * HEAD_PAD] + kr).astype(_BF)
        k_ref[h] = k_h
        v_ref[h] = jnp.concatenate([vt[h * V_HEAD_DIM:(h + 1) * V_HEAD_DIM], ones_tail], axis=0).astype(_BF)
        if not meta:
            blk = qt[h * QK_HEAD_DIM:(h + 1) * QK_HEAD_DIM]
            r1, r2 = _rotate(blk[QK_NOPE_DIM:QK_NOPE_DIM + half], blk[QK_NOPE_DIM + half:], cos_t, sin_t)
            qt_h = jnp.concatenate([blk[:QK_NOPE_DIM], r1, r2, zero_rows], axis=0).astype(_BF)
            q_ref[h] = qt_h
            qt_f = qt_h.astype(_F32)
            qn_ref[h] = jnp.sqrt(jnp.sum(qt_f * qt_f, axis=0, keepdims=True))
            k_f = k_h.astype(_F32)
            kmx_ref[h] = jnp.broadcast_to(jnp.max(jnp.sum(k_f * k_f, axis=1, keepdims=True), axis=0, keepdims=True),
                                          (1, LANES))


def _const_spec(shape):
    nd = len(shape)
    return pl.BlockSpec(shape, lambda *_: (0,) * nd)


def _proj_call(x, rope, gpre, win, wkr_t, wu_t, gq, wq_t, gkv, wk, wv_t, *, tile, meta):
    bsz, seq, _ = x.shape
    nt = seq // tile
    row_spec = lambda w: pl.BlockSpec((None, tile, w), lambda b, i: (b, i, 0))
    rope_spec = pl.BlockSpec((None, None, 2, QK_ROPE_DIM // 2, tile), lambda b, i: (b, i, 0, 0, 0))
    k_spec = pl.BlockSpec((None, N_HEADS, tile, HEAD_PAD), lambda b, i: (b, 0, i, 0))
    t_spec = pl.BlockSpec((None, N_HEADS, None, HEAD_PAD, tile), lambda b, i: (b, 0, i, 0, 0))
    k_shape = jax.ShapeDtypeStruct((bsz, N_HEADS, seq, HEAD_PAD), _BF)
    t_shape = jax.ShapeDtypeStruct((bsz, N_HEADS, nt, HEAD_PAD, tile), _BF)
    w_specs = lambda *ws: [_const_spec(w.shape) for w in ws]
    if meta:
        args = (x, rope, gpre, win, wkr_t, wu_t, gkv, wk, wv_t)
        in_specs = [row_spec(D_MODEL), rope_spec] + w_specs(*args[2:])
        out_specs = [k_spec, t_spec, row_spec(SSM_WIDTH)]
        out_shape = [k_shape, t_shape, jax.ShapeDtypeStruct((bsz, seq, SSM_WIDTH), _F32)]
    else:
        args = (x, rope, gpre, win, wkr_t, wu_t, gq, wq_t, gkv, wk, wv_t)
        in_specs = [row_spec(D_MODEL), rope_spec] + w_specs(*args[2:])
        norm_spec = lambda w: pl.BlockSpec((None, N_HEADS, None, 1, w), lambda b, i: (b, 0, i, 0, 0))
        u_spec = pl.BlockSpec((SSM_GROUPS, tile // SLAB_T, SSM_GROUP, SLAB_T), lambda b, i: (0, i, b, 0))
        out_specs = [t_spec, k_spec, t_spec, u_spec, norm_spec(tile), norm_spec(LANES)]
        out_shape = [t_shape, k_shape, t_shape,
                     jax.ShapeDtypeStruct((SSM_GROUPS, seq // SLAB_T, bsz * SSM_GROUP, SLAB_T), _F32),
                     jax.ShapeDtypeStruct((bsz, N_HEADS, nt, 1, tile), _F32),
                     jax.ShapeDtypeStruct((bsz, N_HEADS, nt, 1, LANES), _F32)]
    return pl.pallas_call(
        functools.partial(_proj_kernel, meta=meta),
        grid=(bsz, nt),
        in_specs=in_specs,
        out_specs=out_specs,
        out_shape=out_shape,
        compiler_params=pltpu.CompilerParams(
            dimension_semantics=("parallel", "parallel"), vmem_limit_bytes=VMEM_LIMIT),
        name="proj_meta" if meta else "proj",
    )(*args)


def _attn_finish(accs, o_ref):
    halves = [(acc * (1.0 / acc[ONES_LANE:ONES_LANE + 1, :]))[:V_HEAD_DIM] for acc in accs]
    o_ref[...] = jnp.concatenate(halves, axis=0).T


def _lane_concat(ref, hh):
    return jnp.concatenate([ref[hh, j] for j in range(ref.shape[1])], axis=1)


def _attn_bounded_kernel(qt_ref, mrow_ref, k_ref, vt_ref, km_ref, vtm_ref, o_ref, *, nk, tk):
    tq = o_ref.shape[0]
    per_slab = vt_ref.shape[3] // tk
    key_row = lax.broadcasted_iota(jnp.int32, (LANES, tq), 0)
    accs = []
    for hh in range(2):
        qt = _lane_concat(qt_ref, hh)
        mrow = _lane_concat(mrow_ref, hh)
        s0 = jnp.where(key_row < N_META, _dot(km_ref[hh], qt), -jnp.inf)
        acc = _dot(vtm_ref[hh, :V_ROWS, :], jnp.exp2(s0 - mrow).astype(_BF))
        scores = lambda c: _dot(k_ref[hh, c * tk:(c + 1) * tk, :], qt)
        s_next = scores(0)
        for c in range(nk):
            s = s_next
            if c + 1 < nk:
                s_next = scores(c + 1)
            vt_c = vt_ref[hh, c // per_slab, :V_ROWS, (c % per_slab) * tk:(c % per_slab + 1) * tk]
            acc = acc + _dot(vt_c, jnp.exp2(s - mrow).astype(_BF))
        accs.append(acc)
    _attn_finish(accs, o_ref)


def _attn_online_kernel(qt_ref, k_ref, vt_ref, km_ref, vtm_ref, o_ref, s0_scr, s1_scr, m_scr, acc_scr, *, nk, tk):
    tq = o_ref.shape[0]
    key_row = lax.broadcasted_iota(jnp.int32, (LANES, tq), 0)
    for hh in range(2):
        s0 = jnp.where(key_row < N_META, _dot(km_ref[hh], _lane_concat(qt_ref, hh)), -jnp.inf)
        m0 = jnp.max(s0, axis=0, keepdims=True)
        m_scr[hh] = m0
        acc_scr[hh] = _dot(vtm_ref[hh, :V_ROWS, :], jnp.exp2(s0 - m0).astype(_BF))

    def scores(buf, c):
        off = pl.multiple_of(c * tk, tk)
        for hh in range(2):
            buf[hh] = _dot(k_ref[hh, pl.ds(off, tk), :], _lane_concat(qt_ref, hh))

    def accumulate(buf, c):
        for hh in range(2):
            s = buf[hh]
            m = m_scr[hh]
            m_new = jnp.maximum(m, jnp.max(s, axis=0, keepdims=True))
            m_scr[hh] = m_new
            p = jnp.exp2(s - m_new).astype(_BF)
            acc_scr[hh] = jnp.exp2(m - m_new) * acc_scr[hh] + _dot(vt_ref[hh, c, :V_ROWS, :], p)

    scores(s0_scr, 0)

    def body(t, _):
        scores(s1_scr, 2 * t + 1)
        accumulate(s0_scr, 2 * t)
        scores(s0_scr, 2 * t + 2)
        accumulate(s1_scr, 2 * t + 1)
        return 0

    lax.fori_loop(0, nk // 2 - 1, body, 0)
    scores(s1_scr, nk - 1)
    accumulate(s0_scr, nk - 2)
    accumulate(s1_scr, nk - 1)
    _attn_finish([acc_scr[0], acc_scr[1]], o_ref)


def _attn_call(qt, mrow, k, vt, km, vtm, *, bounded):
    bsz, _, nslab, _, slab = qt.shape
    seq = k.shape[2]
    tq = min(ATTN_TQ, seq)
    tk = min(ATTN_TK, slab)
    qs = tq // slab
    assert tq % slab == 0 and seq % tq == 0 and slab % tk == 0
    q_spec = lambda rows: pl.BlockSpec((None, 2, qs, rows, slab), lambda b, hp, i: (b, hp, i, 0, 0))
    kv_specs = [pl.BlockSpec((None, 2, seq, HEAD_PAD), lambda b, hp, i: (b, hp, 0, 0)),
                pl.BlockSpec((None, 2, nslab, HEAD_PAD, slab), lambda b, hp, i: (b, hp, 0, 0, 0)),
                pl.BlockSpec((2, LANES, HEAD_PAD), lambda b, hp, i: (hp, 0, 0)),
                pl.BlockSpec((2, HEAD_PAD, LANES), lambda b, hp, i: (hp, 0, 0))]
    if bounded:
        body = functools.partial(_attn_bounded_kernel, nk=seq // tk, tk=tk)
        in_specs = [q_spec(HEAD_PAD), q_spec(1)] + kv_specs
        args, scratch = (qt, mrow, k, vt, km, vtm), []
    else:
        assert nslab % 2 == 0 and nslab >= 4
        body = functools.partial(_attn_online_kernel, nk=nslab, tk=slab)
        in_specs = [q_spec(HEAD_PAD)] + kv_specs
        args = (qt, k, vt, km, vtm)
        scratch = [pltpu.VMEM((2, slab, tq), _F32), pltpu.VMEM((2, slab, tq), _F32),
                   pltpu.VMEM((2, 1, tq), _F32), pltpu.VMEM((2, V_ROWS, tq), _F32)]
    return pl.pallas_call(
        body,
        grid=(bsz, N_HEADS // 2, seq // tq),
        in_specs=in_specs,
        out_specs=pl.BlockSpec((None, tq, LANES), lambda b, hp, i: (b, i, hp)),
        out_shape=jax.ShapeDtypeStruct((bsz, seq, ATTN_WIDTH), _F32),
        scratch_shapes=scratch,
        compiler_params=pltpu.CompilerParams(
            dimension_semantics=("parallel", "parallel", "arbitrary"), vmem_limit_bytes=VMEM_LIMIT),
        name="attn_bounded" if bounded else "attn_online",
    )(*args)


def _cmul_add(ar, ai, xr, xi, sr, si):
    return ar * xr - ai * xi + sr, ar * xi + ai * xr + si


def _s5_kernel(a_ref, um_ref, q_ref, lag_ref, w_ref, v_ref, t_ref, y_ref, sup_scr, ent_scr,
               *, nslab, bsz):
    rows = nslab * bsz
    half = STATE_W // 2
    a = [a_ref[pl.ds(i, rows, stride=SSM_GROUP), :] for i in range(SSM_GROUP)]
    uc = [jnp.concatenate([x[:, c * SSM_CHUNK:(c + 1) * SSM_CHUNK] for x in a], axis=1).astype(_BF)
          for c in range(SLAB_CHUNKS)]
    w = w_ref[...]
    s = [_dot(u, w) for u in uc]
    sr = [x[:, :half] for x in s]
    si = [x[:, half:] for x in s]
    t = t_ref[...]
    trow = lambda r: t[r:r + 1, :]
    sup_r = sup_i = None
    for c in range(SLAB_CHUNKS):
        cr, ci = trow(c), trow(SLAB_CHUNKS + c)
        pr = cr * sr[c] - ci * si[c]
        pi = cr * si[c] + ci * sr[c]
        sup_r = pr if sup_r is None else sup_r + pr
        sup_i = pi if sup_i is None else sup_i + pi
    sup_scr[:, :half] = sup_r
    sup_scr[:, half:] = sup_i

    lane = lax.broadcasted_iota(jnp.int32, (bsz, half), 1)
    fwd = lane < SSM_STATE
    sm = _dot(um_ref[...], w)
    xr = jnp.where(fwd, sm[:bsz, :half], 0.0)
    xi = jnp.where(fwd, sm[:bsz, half:], 0.0)
    a_slab_r, a_slab_i = trow(2 * SLAB_CHUNKS), trow(2 * SLAB_CHUNKS + 1)
    for j in range(nslab):
        rf = j * bsz
        rb = (nslab - 1 - j) * bsz
        ent_scr[rf:rf + bsz, 0:SSM_STATE] = xr[:, 0:SSM_STATE]
        ent_scr[rb:rb + bsz, SSM_STATE:half] = xr[:, SSM_STATE:half]
        ent_scr[rf:rf + bsz, half:half + SSM_STATE] = xi[:, 0:SSM_STATE]
        ent_scr[rb:rb + bsz, half + SSM_STATE:STATE_W] = xi[:, SSM_STATE:half]
        s_r = jnp.where(fwd, sup_scr[rf:rf + bsz, :half], sup_scr[rb:rb + bsz, :half])
        s_i = jnp.where(fwd, sup_scr[rf:rf + bsz, half:], sup_scr[rb:rb + bsz, half:])
        xr, xi = _cmul_add(a_slab_r, a_slab_i, xr, xi, s_r, s_i)

    ent = ent_scr[...]
    a_r, a_i = trow(2 * SLAB_CHUNKS + 2), trow(2 * SLAB_CHUNKS + 3)
    xf = [(ent[:, :half], ent[:, half:])]
    for c in range(1, SLAB_CHUNKS):
        xf.append(_cmul_add(a_r, a_i, xf[-1][0], xf[-1][1], sr[c - 1], si[c - 1]))
    xb = [(ent[:, :half], ent[:, half:])]
    for c in range(SLAB_CHUNKS - 2, -1, -1):
        xb.insert(0, _cmul_add(a_r, a_i, xb[0][0], xb[0][1], sr[c + 1], si[c + 1]))
    fwd_rows = lax.broadcasted_iota(jnp.int32, (rows, half), 1) < SSM_STATE
    m = jnp.concatenate(
        [pltpu.roll(jnp.broadcast_to(lag_ref[i:i + 1, :], (SSM_CHUNK, 2 * CHUNK_W)), 0, 1,
                    stride=SSM_GROUP, stride_axis=0)[:, :CHUNK_W] for i in range(SSM_GROUP)], axis=0).astype(_BF)
    q = q_ref[...]
    m = _dot(m, q).astype(_BF)
    v = _dot(v_ref[...], q).astype(_BF)
    ys = []
    for c in range(SLAB_CHUNKS):
        xin = jnp.concatenate([jnp.where(fwd_rows, xf[c][0], xb[c][0]),
                               jnp.where(fwd_rows, xf[c][1], xb[c][1])], axis=1).astype(_BF)
        ys.append(_dot(uc[c], m) + _dot(xin, v))
    for o in range(SSM_GROUP):
        y_ref[pl.ds(o, rows, stride=SSM_GROUP), :] = jnp.concatenate(
            [y[:, o * SSM_CHUNK:(o + 1) * SSM_CHUNK] for y in ys], axis=1)


def _s5_call(a, um, perm, lags, w_mat, v_mat, tab, *, nslab, bsz):
    n = nslab * bsz * SSM_GROUP
    rows = nslab * bsz
    g_spec = lambda *shape: pl.BlockSpec((None,) + shape, lambda g: (g,) + (0,) * len(shape))
    perm_spec = pl.BlockSpec((CHUNK_W, CHUNK_W), lambda g: (0, 0), pipeline_mode=pl.Buffered(1))
    return pl.pallas_call(
        functools.partial(_s5_kernel, nslab=nslab, bsz=bsz),
        grid=(SSM_GROUPS,),
        in_specs=[g_spec(n, SLAB_T), g_spec(SUBLANES, CHUNK_W), perm_spec,
                  g_spec(SSM_GROUP, 2 * CHUNK_W), g_spec(CHUNK_W, STATE_W), g_spec(STATE_W, CHUNK_W),
                  g_spec(2 * SUBLANES, STATE_W // 2)],
        out_specs=g_spec(n, SLAB_T),
        out_shape=jax.ShapeDtypeStruct((SSM_GROUPS, n, SLAB_T), _F32),
        scratch_shapes=[pltpu.VMEM((rows, STATE_W), _F32), pltpu.VMEM((rows, STATE_W), _F32)],
        compiler_params=pltpu.CompilerParams(
            dimension_semantics=("parallel",), vmem_limit_bytes=VMEM_LIMIT),
        name="s5",
    )(a, um, perm, lags, w_mat, v_mat, tab)


def _s5_matrices(a_re, a_im, log_dt, b_re, b_im, c_re, c_im, d_skip):
    tc = SSM_CHUNK
    lam = lax.complex(jnp.minimum(a_re.astype(_F32), -1e-4), a_im.astype(_F32))
    dt = jnp.exp(log_dt.astype(_F32))[..., None]
    lam_dt = lam * dt
    lam_bar = jnp.exp(lam_dt)
    b_bar = ((lam_bar - 1.0) / lam)[..., None] * lax.complex(b_re.astype(_F32), b_im.astype(_F32))
    c_c = lax.complex(c_re.astype(_F32), c_im.astype(_F32))
    k_idx = jnp.arange(tc + 1, dtype=_F32)
    pw = jnp.exp(lam_dt[:, :, None, :] * k_idx[None, None, :, None])
    kern = jnp.real(jnp.einsum('dgop,dgkp,dgpi->dgkoi', c_c, pw[:, :, :tc], b_bar))
    d_g = d_skip.astype(_F32).reshape(SSM_GROUPS, SSM_GROUP)
    center = kern[0][:, :1] + kern[1][:, :1] + (jnp.eye(SSM_GROUP, dtype=_F32)[None] * d_g[:, :, None])[:, None]
    lags = jnp.concatenate([center, kern[0][:, 1:], jnp.zeros_like(center), kern[1][:, :0:-1]], axis=1)
    lags = lags.transpose(0, 3, 1, 2).reshape(SSM_GROUPS, SSM_GROUP, 2 * CHUNK_W)
    wf = b_bar[0].transpose(0, 2, 1)[:, :, None, :] * pw[0][:, tc - 1::-1][:, None, :, :]
    wb = b_bar[1].transpose(0, 2, 1)[:, :, None, :] * pw[1][:, :tc][:, None, :, :]
    w_mat = jnp.concatenate([jnp.real(wf), jnp.real(wb), jnp.imag(wf), jnp.imag(wb)], axis=-1)
    w_mat = w_mat.reshape(SSM_GROUPS, CHUNK_W, STATE_W)
    gf = pw[0][:, 1:tc + 1][:, :, None, :] * c_c[0][:, None, :, :]
    gb = pw[1][:, tc:0:-1][:, :, None, :] * c_c[1][:, None, :, :]
    v_mat = jnp.concatenate([jnp.real(gf), jnp.real(gb), -jnp.imag(gf), -jnp.imag(gb)], axis=-1)
    v_mat = v_mat.reshape(SSM_GROUPS, CHUNK_W, STATE_W).transpose(0, 2, 1)
    n_idx = jnp.arange(SLAB_CHUNKS + 1, dtype=_F32) * tc
    pc = jnp.exp(lam_dt[:, :, None, :] * n_idx[None, None, :, None])
    coef = jnp.concatenate([pc[0][:, SLAB_CHUNKS - 1::-1], pc[1][:, :SLAB_CHUNKS]], axis=-1)
    both = lambda n: jnp.concatenate([pc[0][:, n], pc[1][:, n]], axis=-1)[:, None, :]
    a_slab, a_chunk = both(SLAB_CHUNKS), both(1)
    tab = jnp.concatenate([jnp.real(coef), jnp.imag(coef), jnp.real(a_slab), jnp.imag(a_slab),
                           jnp.real(a_chunk), jnp.imag(a_chunk)], axis=1)
    tab = jnp.pad(tab, ((0, 0), (0, 2 * SUBLANES - tab.shape[1]), (0, 0)))
    return lags, w_mat, v_mat, tab


def _chunk_permutation():
    r = lax.broadcasted_iota(jnp.int32, (CHUNK_W, CHUNK_W), 0)
    c = lax.broadcasted_iota(jnp.int32, (CHUNK_W, CHUNK_W), 1)
    return (r == (c % SSM_CHUNK) * SSM_GROUP + c // SSM_CHUNK).astype(_BF)


def _post_kernel(x_ref, attn_ref, y_ref, wglu_ref, gmix_ref, wout_ref, gpm_ref, gpre_ref,
                 wup_ref, wdn_ref, gpost_ref, o_ref):
    nslab = y_ref.shape[1]
    gmix = gmix_ref[...]
    per = min(POST_SLABS, nslab)
    blocks = range(nslab // per)
    rows = [pl.ds(r * per * SLAB_T, per * SLAB_T) for r in blocks]
    gy = []
    for r in blocks:
        yt = jnp.concatenate([jnp.concatenate([y_ref[g, r * per + c] for c in range(per)], axis=1)
                              for g in range(SSM_GROUPS)], axis=0)
        y = yt.T
        gy.append((0.5 * y * (1.0 + jnp.tanh(math.sqrt(2.0 / math.pi) * (y + 0.044715 * (y * y * y))))).astype(_BF))
    z = [_dot(gy[r], wglu_ref[...]) for r in blocks]
    mix = []
    for r in blocks:
        ssm = z[r][:, :SSM_WIDTH] * (1.0 / (1.0 + jnp.exp(-z[r][:, SSM_WIDTH:])))
        mix.append(jnp.concatenate([_rms(attn_ref[rows[r], :], gmix[:, :ATTN_WIDTH]),
                                    _rms(ssm, gmix[:, ATTN_WIDTH:])], axis=-1).astype(_BF))
    mixed = [_dot(mix[r], wout_ref[...]) for r in blocks]
    h1 = [x_ref[rows[r], :] + _rms(mixed[r], gpm_ref[...]) for r in blocks]
    hn = [_rms(h1[r], gpre_ref[...]).astype(_BF) for r in blocks]
    acc = [None for _ in blocks]
    for c in range(D_FF // FF_TILE):
        for r in blocks:
            up = jnp.maximum(_dot(hn[r], wup_ref[:, c * FF_TILE:(c + 1) * FF_TILE]), 0.0)
            part = _dot((up * up).astype(_BF), wdn_ref[c * FF_TILE:(c + 1) * FF_TILE, :])
            acc[r] = part if acc[r] is None else acc[r] + part
    for r in blocks:
        o_ref[rows[r], :] = h1[r] + _rms(acc[r], gpost_ref[...])


def _post_call(x, attn, y, wglu, gmix, wout, gpm, gpre, wup, wdn, gpost, *, tile):
    bsz, seq, _ = x.shape
    row_spec = lambda w: pl.BlockSpec((None, tile, w), lambda b, i: (b, i, 0))
    wspec = lambda shape: pl.BlockSpec(shape, lambda b, i: (0, 0), pipeline_mode=pl.Buffered(1))
    y_spec = pl.BlockSpec((SSM_GROUPS, tile // SLAB_T, SSM_GROUP, SLAB_T), lambda b, i: (0, i, b, 0))
    return pl.pallas_call(
        _post_kernel,
        grid=(bsz, seq // tile),
        in_specs=[row_spec(D_MODEL), row_spec(ATTN_WIDTH), y_spec,
                  wspec((SSM_WIDTH, 2 * SSM_WIDTH)), wspec((1, D_MODEL)), wspec((D_MODEL, D_MODEL)),
                  wspec((1, D_MODEL)), wspec((1, D_MODEL)), wspec((D_MODEL, D_FF)),
                  wspec((D_FF, D_MODEL)), wspec((1, D_MODEL))],
        out_specs=row_spec(D_MODEL),
        out_shape=jax.ShapeDtypeStruct((bsz, seq, D_MODEL), _F32),
        compiler_params=pltpu.CompilerParams(
            dimension_semantics=("parallel", "parallel"), vmem_limit_bytes=VMEM_LIMIT),
        name="post",
    )(x, attn, y, wglu, gmix, wout, gpm, gpre, wup, wdn, gpost)


def _rope_tables(pos, tile):
    half = QK_ROPE_DIM // 2
    inv = 1.0 / (ROPE_BASE ** (jnp.arange(0, QK_ROPE_DIM, 2, dtype=_F32) / QK_ROPE_DIM))
    ang = pos.astype(_F32)[:, None, :] * inv[None, :, None]
    bsz, seq = pos.shape
    rope = jnp.stack([jnp.cos(ang), jnp.sin(ang)], axis=1)
    return rope.reshape(bsz, 2, half, seq // tile, tile).transpose(0, 3, 1, 2, 4)


def _prep_weights(w_in, w_uq, w_ukv):
    scale = QK_HEAD_DIM ** -0.5 * math.log2(math.e)
    win = w_in[:, :OFF_KR].astype(_BF)
    wkr_t = w_in[:, OFF_KR:OFF_U].T.astype(_BF)
    wu_t = w_in[:, OFF_U:].T.astype(_BF)
    wq_t = (w_uq * scale).T.astype(_BF)
    wkv3 = w_ukv.reshape(KV_LORA_RANK, N_HEADS, QK_NOPE_DIM + V_HEAD_DIM)
    wk = jnp.concatenate([wkv3[..., :QK_NOPE_DIM],
                          jnp.zeros((KV_LORA_RANK, N_HEADS, HEAD_PAD - QK_NOPE_DIM), _F32)], axis=-1)
    wk = wk.reshape(KV_LORA_RANK, N_HEADS * HEAD_PAD).astype(_BF)
    wv_t = wkv3[..., QK_NOPE_DIM:].reshape(KV_LORA_RANK, N_HEADS * V_HEAD_DIM).T.astype(_BF)
    return win, wkr_t, wu_t, wq_t, wk, wv_t


def kernel(x, positions, meta_tokens, g_pre_mix, w_in, g_q_lat, w_uq, g_kv_lat, w_ukv,
           ssm_A_re, ssm_A_im, ssm_log_dt, ssm_B_re, ssm_B_im, ssm_C_re, ssm_C_im, ssm_D,
           w_glu, g_mix_out, w_out, g_post_mix, g_pre_mlp, w_mlp_up, w_mlp_down, g_post_mlp):
    bsz, seq, _ = x.shape
    assert seq % ROW_TILE == 0 and ROW_TILE % SLAB_T == 0 and bsz <= SUBLANES
    assert seq % PROJ_TILE == 0 and PROJ_TILE % SLAB_T == 0
    assert N_META <= SSM_CHUNK
    row = lambda g: g.reshape(1, -1).astype(_F32)

    win, wkr_t, wu_t, wq_t, wk, wv_t = _prep_weights(w_in[0], w_uq[0], w_ukv[0])
    weights = (row(g_pre_mix[0]), win, wkr_t, wu_t, row(g_q_lat[0]), wq_t, row(g_kv_lat[0]), wk, wv_t)
    rope = _rope_tables(positions.astype(jnp.int32) + N_META, PROJ_TILE)
    qt, k, vt, u, q_norm, k_sq = _proj_call(x, rope, *weights, tile=PROJ_TILE, meta=False)
    meta_x = jnp.pad(meta_tokens.astype(x.dtype), ((0, LANES - N_META), (0, 0)))[None]
    rope_m = _rope_tables(jnp.arange(LANES, dtype=jnp.int32)[None], LANES)
    k_m, vt_m, u_m = _proj_call(meta_x, rope_m, *weights, tile=LANES, meta=True)
    km, vtm, u_m = k_m[0], vt_m[0, :, 0], u_m[:, :N_META]
    k_m_sq = jnp.max(jnp.sum(jnp.square(k_m[0].astype(_F32)), axis=-1), axis=-1)
    k_max = jnp.sqrt(jnp.maximum(jnp.max(k_sq, axis=(2, 3, 4)), k_m_sq[None]))
    mrow = q_norm * (k_max * BOUND_SLACK)[:, :, None, None, None]
    attn = lax.cond(jnp.max(mrow) <= BOUND_LIMIT,
                    lambda: _attn_call(qt, mrow, k, vt, km, vtm, bounded=True),
                    lambda: _attn_call(qt, mrow, k, vt, km, vtm, bounded=False))

    um = u_m[0].astype(_BF).reshape(N_META, SSM_GROUPS, SSM_GROUP).transpose(1, 2, 0)
    um = jnp.pad(um, ((0, 0), (0, 0), (SSM_CHUNK - N_META, 0))).reshape(SSM_GROUPS, 1, CHUNK_W)
    um = jnp.broadcast_to(um, (SSM_GROUPS, SUBLANES, CHUNK_W))
    lags, w_mat, v_mat, tab = _s5_matrices(ssm_A_re[0], ssm_A_im[0], ssm_log_dt[0], ssm_B_re[0],
                                            ssm_B_im[0], ssm_C_re[0], ssm_C_im[0], ssm_D[0])
    nslab = seq // SLAB_T
    yg = _s5_call(u.reshape(SSM_GROUPS, nslab * bsz * SSM_GROUP, SLAB_T), um, _chunk_permutation(),
                  lags, w_mat.astype(_BF), v_mat.astype(_BF), tab, nslab=nslab, bsz=bsz)
    y = yg.reshape(SSM_GROUPS, nslab, bsz * SSM_GROUP, SLAB_T)

    return _post_call(x, attn, y, w_glu[0].astype(_BF), row(g_mix_out[0]), w_out[0].astype(_BF),
                      row(g_post_mix[0]), row(g_pre_mlp[0]), w_mlp_up[0].astype(_BF),
                      w_mlp_down[0].astype(_BF), row(g_post_mlp[0]), tile=ROW_TILE)
```

```python
import functools
import math

import jax
import jax.numpy as jnp
from jax import lax
from jax.experimental import pallas as pl
from jax.experimental.pallas import tpu as pltpu

D_MODEL = 1024
N_META = 16
ATTN_WIDTH = 512
SSM_WIDTH = 512
N_HEADS = 8
V_HEAD_DIM = 64
QK_NOPE_DIM = 64
QK_ROPE_DIM = 32
QK_HEAD_DIM = QK_NOPE_DIM + QK_ROPE_DIM
Q_LORA_RANK = 384
KV_LORA_RANK = 256
ROPE_BASE = 10000.0
SSM_GROUP = 16
SSM_GROUPS = 32
SSM_STATE = 64
D_FF = 4 * D_MODEL
EPS = 1e-6
OFF_KR = Q_LORA_RANK + KV_LORA_RANK
OFF_U = OFF_KR + QK_ROPE_DIM

LANES = 128
SUBLANES = 8
HEAD_PAD = LANES
ONES_LANE = V_HEAD_DIM
ATTN_TQ = 1024
ATTN_TK = 256
V_ROWS = 96
PW_Q = 0
PW_KV = PW_Q + Q_LORA_RANK
PW_END = PW_KV + KV_LORA_RANK

SSM_CHUNK = 32
CHUNK_W = SSM_CHUNK * SSM_GROUP
STATE_W = 4 * SSM_STATE
SLAB_T = LANES
SLAB_CHUNKS = SLAB_T // SSM_CHUNK

ROW_TILE = 512
PROJ_TILE = 1024
FF_TILE = 1024
POST_SLABS = 2
BOUND_SLACK = 1.0 + 2.0 ** -6
BOUND_LIMIT = 60.0
V7X_VMEM_BYTES = 64 * 1024 * 1024
VMEM_LIMIT = V7X_VMEM_BYTES - 8 * 1024 * 1024

_BF = jnp.bfloat16
_F32 = jnp.float32


def _dot(a, b):
    return jnp.dot(a, b, preferred_element_type=_F32)


def _rms(x, g):
    return x * lax.rsqrt(jnp.mean(x * x, axis=-1, keepdims=True) + EPS) * g


_NT = (((1,), (1,)), ((), ()))


def _rotate(x1, x2, cos_t, sin_t):
    return x1 * cos_t - x2 * sin_t, x1 * sin_t + x2 * cos_t


def _proj_kernel(x_ref, rope_ref, *refs, meta):
    if meta:
        gpre_ref, win_ref, wkr_ref, wu_ref, gkv_ref, wk_ref, wv_ref, k_ref, v_ref, u_ref = refs
    else:
        (gpre_ref, win_ref, wkr_ref, wu_ref, gq_ref, wq_ref, gkv_ref, wk_ref, wv_ref,
         q_ref, k_ref, v_ref, u_ref, qn_ref, kmx_ref) = refs
    tile = x_ref.shape[0]
    half = QK_ROPE_DIM // 2
    cos_t, sin_t = rope_ref[0], rope_ref[1]
    xn = _rms(x_ref[...], gpre_ref[...]).astype(_BF)
    proj = _dot(xn, win_ref[...])
    kvn = _rms(proj[:, PW_KV:PW_END], gkv_ref[...]).astype(_BF)
    krt = lax.dot_general(wkr_ref[...], xn, _NT, preferred_element_type=_F32)
    r1, r2 = _rotate(krt[:half], krt[half:], cos_t, sin_t)
    kr = jnp.concatenate([jnp.zeros((QK_NOPE_DIM, tile), _F32), r1, r2,
                          jnp.zeros((HEAD_PAD - QK_HEAD_DIM, tile), _F32)], axis=0).T
    kk = _dot(kvn, wk_ref[...])
    vt = lax.dot_general(wv_ref[...], kvn, _NT, preferred_element_type=_F32)
    ones_tail = (lax.broadcasted_iota(jnp.int32, (HEAD_PAD - V_HEAD_DIM, tile), 0) == 0).astype(_F32)
    if meta:
        u_ref[...] = lax.dot_general(xn, wu_ref[...], _NT, preferred_element_type=_F32)
    else:
        ut = lax.dot_general(wu_ref[...], xn, _NT, preferred_element_type=_F32)
        for g in range(SSM_GROUPS):
            for c in range(tile // SLAB_T):
                u_ref[g, c] = ut[g * SSM_GROUP:(g + 1) * SSM_GROUP, c * SLAB_T:(c + 1) * SLAB_T]
        qn = _rms(proj[:, PW_Q:PW_KV], gq_ref[...]).astype(_BF)
        qt = lax.dot_general(wq_ref[...], qn, _NT, preferred_element_type=_F32)
        zero_rows = jnp.zeros((HEAD_PAD - QK_HEAD_DIM, tile), _F32)
    for h in range(N_HEADS):
        k_h = (kk[:, h * HEAD_PAD:(h + 1) * HEAD_PAD] + kr).astype(_BF)
        k_ref[h] = k_h
        v_ref[h] = jnp.concatenate([vt[h * V_HEAD_DIM:(h + 1) * V_HEAD_DIM], ones_tail], axis=0).astype(_BF)
        if not meta:
            blk = qt[h * QK_HEAD_DIM:(h + 1) * QK_HEAD_DIM]
            r1, r2 = _rotate(blk[QK_NOPE_DIM:QK_NOPE_DIM + half], blk[QK_NOPE_DIM + half:], cos_t, sin_t)
            qt_h = jnp.concatenate([blk[:QK_NOPE_DIM], r1, r2, zero_rows], axis=0).astype(_BF)
            q_ref[h] = qt_h
            qt_f = qt_h.astype(_F32)
            qn_ref[h] = jnp.sqrt(jnp.sum(qt_f * qt_f, axis=0, keepdims=True))
            k_f = k_h.astype(_F32)
            kmx_ref[h] = jnp.broadcast_to(jnp.max(jnp.sum(k_f * k_f, axis=1, keepdims=True), axis=0, keepdims=True),
                                          (1, LANES))


def _const_spec(shape):
    nd = len(shape)
    return pl.BlockSpec(shape, lambda *_: (0,) * nd)


def _proj_call(x, rope, gpre, win, wkr_t, wu_t, gq, wq_t, gkv, wk, wv_t, *, tile, meta):
    bsz, seq, _ = x.shape
    nt = seq // tile
    row_spec = lambda w: pl.BlockSpec((None, tile, w), lambda b, i: (b, i, 0))
    rope_spec = pl.BlockSpec((None, None, 2, QK_ROPE_DIM // 2, tile), lambda b, i: (b, i, 0, 0, 0))
    k_spec = pl.BlockSpec((None, N_HEADS, tile, HEAD_PAD), lambda b, i: (b, 0, i, 0))
    t_spec = pl.BlockSpec((None, N_HEADS, None, HEAD_PAD, tile), lambda b, i: (b, 0, i, 0, 0))
    k_shape = jax.ShapeDtypeStruct((bsz, N_HEADS, seq, HEAD_PAD), _BF)
    t_shape = jax.ShapeDtypeStruct((bsz, N_HEADS, nt, HEAD_PAD, tile), _BF)
    w_specs = lambda *ws: [_const_spec(w.shape) for w in ws]
    if meta:
        args = (x, rope, gpre, win, wkr_t, wu_t, gkv, wk, wv_t)
        in_specs = [row_spec(D_MODEL), rope_spec] + w_specs(*args[2:])
        out_specs = [k_spec, t_spec, row_spec(SSM_WIDTH)]
        out_shape = [k_shape, t_shape, jax.ShapeDtypeStruct((bsz, seq, SSM_WIDTH), _F32)]
    else:
        args = (x, rope, gpre, win, wkr_t, wu_t, gq, wq_t, gkv, wk, wv_t)
        in_specs = [row_spec(D_MODEL), rope_spec] + w_specs(*args[2:])
        norm_spec = lambda w: pl.BlockSpec((None, N_HEADS, None, 1, w), lambda b, i: (b, 0, i, 0, 0))
        u_spec = pl.BlockSpec((SSM_GROUPS, tile // SLAB_T, SSM_GROUP, SLAB_T), lambda b, i: (0, i, b, 0))
        out_specs = [t_spec, k_spec, t_spec, u_spec, norm_spec(tile), norm_spec(LANES)]
        out_shape = [t_shape, k_shape, t_shape,
                     jax.ShapeDtypeStruct((SSM_GROUPS, seq // SLAB_T, bsz * SSM_GROUP, SLAB_T), _F32),
                     jax.ShapeDtypeStruct((bsz, N_HEADS, nt, 1, tile), _F32),
                     jax.ShapeDtypeStruct((bsz, N_HEADS, nt, 1, LANES), _F32)]
    return pl.pallas_call(
        functools.partial(_proj_kernel, meta=meta),
        grid=(bsz, nt),
        in_specs=in_specs,
        out_specs=out_specs,
        out_shape=out_shape,
        compiler_params=pltpu.CompilerParams(
            dimension_semantics=("parallel", "parallel"), vmem_limit_bytes=VMEM_LIMIT),
        name="proj_meta" if meta else "proj",
    )(*args)


def _attn_finish(accs, o_ref):
    halves = [(acc * (1.0 / acc[ONES_LANE:ONES_LANE + 1, :]))[:V_HEAD_DIM] for acc in accs]
    o_ref[...] = jnp.concatenate(halves, axis=0).T


def _lane_concat(ref, hh):
    return jnp.concatenate([ref[hh, j] for j in range(ref.shape[1])], axis=1)


def _attn_bounded_kernel(qt_ref, mrow_ref, k_ref, vt_ref, km_ref, vtm_ref, o_ref, *, nk, tk):
    tq = o_ref.shape[0]
    per_slab = vt_ref.shape[3] // tk
    key_row = lax.broadcasted_iota(jnp.int32, (LANES, tq), 0)
    accs = []
    for hh in range(2):
        qt = _lane_concat(qt_ref, hh)
        mrow = _lane_concat(mrow_ref, hh)
        s0 = jnp.where(key_row < N_META, _dot(km_ref[hh], qt), -jnp.inf)
        acc = _dot(vtm_ref[hh, :V_ROWS, :], jnp.exp2(s0 - mrow).astype(_BF))
        scores = lambda c: _dot(k_ref[hh, c * tk:(c + 1) * tk, :], qt)
        s_next = scores(0)
        for c in range(nk):
            s = s_next
            if c + 1 < nk:
                s_next = scores(c + 1)
            vt_c = vt_ref[hh, c // per_slab, :V_ROWS, (c % per_slab) * tk:(c % per_slab + 1) * tk]
            acc = acc + _dot(vt_c, jnp.exp2(s - mrow).astype(_BF))
        accs.append(acc)
    _attn_finish(accs, o_ref)


def _attn_online_kernel(qt_ref, k_ref, vt_ref, km_ref, vtm_ref, o_ref, s0_scr, s1_scr, m_scr, acc_scr, *, nk, tk):
    tq = o_ref.shape[0]
    key_row = lax.broadcasted_iota(jnp.int32, (LANES, tq), 0)
    for hh in range(2):
        s0 = jnp.where(key_row < N_META, _dot(km_ref[hh], _lane_concat(qt_ref, hh)), -jnp.inf)
        m0 = jnp.max(s0, axis=0, keepdims=True)
        m_scr[hh] = m0
        acc_scr[hh] = _dot(vtm_ref[hh, :V_ROWS, :], jnp.exp2(s0 - m0).astype(_BF))

    def scores(buf, c):
        off = pl.multiple_of(c * tk, tk)
        for hh in range(2):
            buf[hh] = _dot(k_ref[hh, pl.ds(off, tk), :], _lane_concat(qt_ref, hh))

    def accumulate(buf, c):
        for hh in range(2):
            s = buf[hh]
            m = m_scr[hh]
            m_new = jnp.maximum(m, jnp.max(s, axis=0, keepdims=True))
            m_scr[hh] = m_new
            p = jnp.exp2(s - m_new).astype(_BF)
            acc_scr[hh] = jnp.exp2(m - m_new) * acc_scr[hh] + _dot(vt_ref[hh, c, :V_ROWS, :], p)

    scores(s0_scr, 0)

    def body(t, _):
        scores(s1_scr, 2 * t + 1)
        accumulate(s0_scr, 2 * t)
        scores(s0_scr, 2 * t + 2)
        accumulate(s1_scr, 2 * t + 1)
        return 0

    lax.fori_loop(0, nk // 2 - 1, body, 0)
    scores(s1_scr, nk - 1)
    accumulate(s0_scr, nk - 2)
    accumulate(s1_scr, nk - 1)
    _attn_finish([acc_scr[0], acc_scr[1]], o_ref)


def _attn_call(qt, mrow, k, vt, km, vtm, *, bounded):
    bsz, _, nslab, _, slab = qt.shape
    seq = k.shape[2]
    tq = min(ATTN_TQ, seq)
    tk = min(ATTN_TK, slab)
    qs = tq // slab
    assert tq % slab == 0 and seq % tq == 0 and slab % tk == 0
    q_spec = lambda rows: pl.BlockSpec((None, 2, qs, rows, slab), lambda b, hp, i: (b, hp, i, 0, 0))
    kv_specs = [pl.BlockSpec((None, 2, seq, HEAD_PAD), lambda b, hp, i: (b, hp, 0, 0)),
                pl.BlockSpec((None, 2, nslab, HEAD_PAD, slab), lambda b, hp, i: (b, hp, 0, 0, 0)),
                pl.BlockSpec((2, LANES, HEAD_PAD), lambda b, hp, i: (hp, 0, 0)),
                pl.BlockSpec((2, HEAD_PAD, LANES), lambda b, hp, i: (hp, 0, 0))]
    if bounded:
        body = functools.partial(_attn_bounded_kernel, nk=seq // tk, tk=tk)
        in_specs = [q_spec(HEAD_PAD), q_spec(1)] + kv_specs
        args, scratch = (qt, mrow, k, vt, km, vtm), []
    else:
        assert nslab % 2 == 0 and nslab >= 4
        body = functools.partial(_attn_online_kernel, nk=nslab, tk=slab)
        in_specs = [q_spec(HEAD_PAD)] + kv_specs
        args = (qt, k, vt, km, vtm)
        scratch = [pltpu.VMEM((2, slab, tq), _F32), pltpu.VMEM((2, slab, tq), _F32),
                   pltpu.VMEM((2, 1, tq), _F32), pltpu.VMEM((2, V_ROWS, tq), _F32)]
    return pl.pallas_call(
        body,
        grid=(bsz, N_HEADS // 2, seq // tq),
        in_specs=in_specs,
        out_specs=pl.BlockSpec((None, tq, LANES), lambda b, hp, i: (b, i, hp)),
        out_shape=jax.ShapeDtypeStruct((bsz, seq, ATTN_WIDTH), _F32),
        scratch_shapes=scratch,
        compiler_params=pltpu.CompilerParams(
            dimension_semantics=("parallel", "parallel", "arbitrary"), vmem_limit_bytes=VMEM_LIMIT),
        name="attn_bounded" if bounded else "attn_online",
    )(*args)


def _cmul_add(ar, ai, xr, xi, sr, si):
    return ar * xr - ai * xi + sr, ar * xi + ai * xr + si


def _s5_kernel(a_ref, um_ref, q_ref, lag_ref, w_ref, v_ref, t_ref, y_ref, sup_scr, ent_scr,
               *, nslab, bsz):
    rows = nslab * bsz
    half = STATE_W // 2
    a = [a_ref[pl.ds(i, rows, stride=SSM_GROUP), :] for i in range(SSM_GROUP)]
    uc = [jnp.concatenate([x[:, c * SSM_CHUNK:(c + 1) * SSM_CHUNK] for x in a], axis=1).astype(_BF)
          for c in range(SLAB_CHUNKS)]
    w = w_ref[...]
    s = [_dot(u, w) for u in uc]
    sr = [x[:, :half] for x in s]
    si = [x[:, half:] for x in s]
    t = t_ref[...]
    trow = lambda r: t[r:r + 1, :]
    sup_r = sup_i = None
    for c in range(SLAB_CHUNKS):
        cr, ci = trow(c), trow(SLAB_CHUNKS + c)
        pr = cr * sr[c] - ci * si[c]
        pi = cr * si[c] + ci * sr[c]
        sup_r = pr if sup_r is None else sup_r + pr
        sup_i = pi if sup_i is None else sup_i + pi
    sup_scr[:, :half] = sup_r
    sup_scr[:, half:] = sup_i

    lane = lax.broadcasted_iota(jnp.int32, (bsz, half), 1)
    fwd = lane < SSM_STATE
    sm = _dot(um_ref[...], w)
    xr = jnp.where(fwd, sm[:bsz, :half], 0.0)
    xi = jnp.where(fwd, sm[:bsz, half:], 0.0)
    a_slab_r, a_slab_i = trow(2 * SLAB_CHUNKS), trow(2 * SLAB_CHUNKS + 1)
    for j in range(nslab):
        rf = j * bsz
        rb = (nslab - 1 - j) * bsz
        ent_scr[rf:rf + bsz, 0:SSM_STATE] = xr[:, 0:SSM_STATE]
        ent_scr[rb:rb + bsz, SSM_STATE:half] = xr[:, SSM_STATE:half]
        ent_scr[rf:rf + bsz, half:half + SSM_STATE] = xi[:, 0:SSM_STATE]
        ent_scr[rb:rb + bsz, half + SSM_STATE:STATE_W] = xi[:, SSM_STATE:half]
        s_r = jnp.where(fwd, sup_scr[rf:rf + bsz, :half], sup_scr[rb:rb + bsz, :half])
        s_i = jnp.where(fwd, sup_scr[rf:rf + bsz, half:], sup_scr[rb:rb + bsz, half:])
        xr, xi = _cmul_add(a_slab_r, a_slab_i, xr, xi, s_r, s_i)

    ent = ent_scr[...]
    a_r, a_i = trow(2 * SLAB_CHUNKS + 2), trow(2 * SLAB_CHUNKS + 3)
    xf = [(ent[:, :half], ent[:, half:])]
    for c in range(1, SLAB_CHUNKS):
        xf.append(_cmul_add(a_r, a_i, xf[-1][0], xf[-1][1], sr[c - 1], si[c - 1]))
    xb = [(ent[:, :half], ent[:, half:])]
    for c in range(SLAB_CHUNKS - 2, -1, -1):
        xb.insert(0, _cmul_add(a_r, a_i, xb[0][0], xb[0][1], sr[c + 1], si[c + 1]))
    fwd_rows = lax.broadcasted_iota(jnp.int32, (rows, half), 1) < SSM_STATE
    m = jnp.concatenate(
        [pltpu.roll(jnp.broadcast_to(lag_ref[i:i + 1, :], (SSM_CHUNK, 2 * CHUNK_W)), 0, 1,
                    stride=SSM_GROUP, stride_axis=0)[:, :CHUNK_W] for i in range(SSM_GROUP)], axis=0).astype(_BF)
    q = q_ref[...]
    m = _dot(m, q).astype(_BF)
    v = _dot(v_ref[...], q).astype(_BF)
    ys = []
    for c in range(SLAB_CHUNKS):
        xin = jnp.concatenate([jnp.where(fwd_rows, xf[c][0], xb[c][0]),
                               jnp.where(fwd_rows, xf[c][1], xb[c][1])], axis=1).astype(_BF)
        ys.append(_dot(uc[c], m) + _dot(xin, v))
    for o in range(SSM_GROUP):
        y_ref[pl.ds(o, rows, stride=SSM_GROUP), :] = jnp.concatenate(
            [y[:, o * SSM_CHUNK:(o + 1) * SSM_CHUNK] for y in ys], axis=1)


def _s5_call(a, um, perm, lags, w_mat, v_mat, tab, *, nslab, bsz):
    n = nslab * bsz * SSM_GROUP
    rows = nslab * bsz
    g_spec = lambda *shape: pl.BlockSpec((None,) + shape, lambda g: (g,) + (0,) * len(shape))
    perm_spec = pl.BlockSpec((CHUNK_W, CHUNK_W), lambda g: (0, 0), pipeline_mode=pl.Buffered(1))
    return pl.pallas_call(
        functools.partial(_s5_kernel, nslab=nslab, bsz=bsz),
        grid=(SSM_GROUPS,),
        in_specs=[g_spec(n, SLAB_T), g_spec(SUBLANES, CHUNK_W), perm_spec,
                  g_spec(SSM_GROUP, 2 * CHUNK_W), g_spec(CHUNK_W, STATE_W), g_spec(STATE_W, CHUNK_W),
                  g_spec(2 * SUBLANES, STATE_W // 2)],
        out_specs=g_spec(n, SLAB_T),
        out_shape=jax.ShapeDtypeStruct((SSM_GROUPS, n, SLAB_T), _F32),
        scratch_shapes=[pltpu.VMEM((rows, STATE_W), _F32), pltpu.VMEM((rows, STATE_W), _F32)],
        compiler_params=pltpu.CompilerParams(
            dimension_semantics=("parallel",), vmem_limit_bytes=VMEM_LIMIT),
        name="s5",
    )(a, um, perm, lags, w_mat, v_mat, tab)


def _s5_matrices(a_re, a_im, log_dt, b_re, b_im, c_re, c_im, d_skip):
    tc = SSM_CHUNK
    lam = lax.complex(jnp.minimum(a_re.astype(_F32), -1e-4), a_im.astype(_F32))
    dt = jnp.exp(log_dt.astype(_F32))[..., None]
    lam_dt = lam * dt
    lam_bar = jnp.exp(lam_dt)
    b_bar = ((lam_bar - 1.0) / lam)[..., None] * lax.complex(b_re.astype(_F32), b_im.astype(_F32))
    c_c = lax.complex(c_re.astype(_F32), c_im.astype(_F32))
    k_idx = jnp.arange(tc + 1, dtype=_F32)
    pw = jnp.exp(lam_dt[:, :, None, :] * k_idx[None, None, :, None])
    kern = jnp.real(jnp.einsum('dgop,dgkp,dgpi->dgkoi', c_c, pw[:, :, :tc], b_bar))
    d_g = d_skip.astype(_F32).reshape(SSM_GROUPS, SSM_GROUP)
    center = kern[0][:, :1] + kern[1][:, :1] + (jnp.eye(SSM_GROUP, dtype=_F32)[None] * d_g[:, :, None])[:, None]
    lags = jnp.concatenate([center, kern[0][:, 1:], jnp.zeros_like(center), kern[1][:, :0:-1]], axis=1)
    lags = lags.transpose(0, 3, 1, 2).reshape(SSM_GROUPS, SSM_GROUP, 2 * CHUNK_W)
    wf = b_bar[0].transpose(0, 2, 1)[:, :, None, :] * pw[0][:, tc - 1::-1][:, None, :, :]
    wb = b_bar[1].transpose(0, 2, 1)[:, :, None, :] * pw[1][:, :tc][:, None, :, :]
    w_mat = jnp.concatenate([jnp.real(wf), jnp.real(wb), jnp.imag(wf), jnp.imag(wb)], axis=-1)
    w_mat = w_mat.reshape(SSM_GROUPS, CHUNK_W, STATE_W)
    gf = pw[0][:, 1:tc + 1][:, :, None, :] * c_c[0][:, None, :, :]
    gb = pw[1][:, tc:0:-1][:, :, None, :] * c_c[1][:, None, :, :]
    v_mat = jnp.concatenate([jnp.real(gf), jnp.real(gb), -jnp.imag(gf), -jnp.imag(gb)], axis=-1)
    v_mat = v_mat.reshape(SSM_GROUPS, CHUNK_W, STATE_W).transpose(0, 2, 1)
    n_idx = jnp.arange(SLAB_CHUNKS + 1, dtype=_F32) * tc
    pc = jnp.exp(lam_dt[:, :, None, :] * n_idx[None, None, :, None])
    coef = jnp.concatenate([pc[0][:, SLAB_CHUNKS - 1::-1], pc[1][:, :SLAB_CHUNKS]], axis=-1)
    both = lambda n: jnp.concatenate([pc[0][:, n], pc[1][:, n]], axis=-1)[:, None, :]
    a_slab, a_chunk = both(SLAB_CHUNKS), both(1)
    tab = jnp.concatenate([jnp.real(coef), jnp.imag(coef), jnp.real(a_slab), jnp.imag(a_slab),
                           jnp.real(a_chunk), jnp.imag(a_chunk)], axis=1)
    tab = jnp.pad(tab, ((0, 0), (0, 2 * SUBLANES - tab.shape[1]), (0, 0)))
    return lags, w_mat, v_mat, tab


def _chunk_permutation():
    r = lax.broadcasted_iota(jnp.int32, (CHUNK_W, CHUNK_W), 0)
    c = lax.broadcasted_iota(jnp.int32, (CHUNK_W, CHUNK_W), 1)
    return (r == (c % SSM_CHUNK) * SSM_GROUP + c // SSM_CHUNK).astype(_BF)


def _post_kernel(x_ref, attn_ref, y_ref, wglu_ref, gmix_ref, wout_ref, gpm_ref, gpre_ref,
                 wup_ref, wdn_ref, gpost_ref, o_ref):
    nslab = y_ref.shape[1]
    gmix = gmix_ref[...]
    per = min(POST_SLABS, nslab)
    blocks = range(nslab // per)
    rows = [pl.ds(r * per * SLAB_T, per * SLAB_T) for r in blocks]
    gy = []
    for r in blocks:
        yt = jnp.concatenate([jnp.concatenate([y_ref[g, r * per + c] for c in range(per)], axis=1)
                              for g in range(SSM_GROUPS)], axis=0)
        y = yt.T
        gy.append((0.5 * y * (1.0 + jnp.tanh(math.sqrt(2.0 / math.pi) * (y + 0.044715 * (y * y * y))))).astype(_BF))
    z = [_dot(gy[r], wglu_ref[...]) for r in blocks]
    mix = []
    for r in blocks:
        ssm = z[r][:, :SSM_WIDTH] * (1.0 / (1.0 + jnp.exp(-z[r][:, SSM_WIDTH:])))
        mix.append(jnp.concatenate([_rms(attn_ref[rows[r], :], gmix[:, :ATTN_WIDTH]),
                                    _rms(ssm, gmix[:, ATTN_WIDTH:])], axis=-1).astype(_BF))
    mixed = [_dot(mix[r], wout_ref[...]) for r in blocks]
    h1 = [x_ref[rows[r], :] + _rms(mixed[r], gpm_ref[...]) for r in blocks]
    hn = [_rms(h1[r], gpre_ref[...]).astype(_BF) for r in blocks]
    acc = [None for _ in blocks]
    for c in range(D_FF // FF_TILE):
        for r in blocks:
            up = jnp.maximum(_dot(hn[r], wup_ref[:, c * FF_TILE:(c + 1) * FF_TILE]), 0.0)
            part = _dot((up * up).astype(_BF), wdn_ref[c * FF_TILE:(c + 1) * FF_TILE, :])
            acc[r] = part if acc[r] is None else acc[r] + part
    for r in blocks:
        o_ref[rows[r], :] = h1[r] + _rms(acc[r], gpost_ref[...])


def _post_call(x, attn, y, wglu, gmix, wout, gpm, gpre, wup, wdn, gpost, *, tile):
    bsz, seq, _ = x.shape
    row_spec = lambda w: pl.BlockSpec((None, tile, w), lambda b, i: (b, i, 0))
    wspec = lambda shape: pl.BlockSpec(shape, lambda b, i: (0, 0), pipeline_mode=pl.Buffered(1))
    y_spec = pl.BlockSpec((SSM_GROUPS, tile // SLAB_T, SSM_GROUP, SLAB_T), lambda b, i: (0, i, b, 0))
    return pl.pallas_call(
        _post_kernel,
        grid=(bsz, seq // tile),
        in_specs=[row_spec(D_MODEL), row_spec(ATTN_WIDTH), y_spec,
                  wspec((SSM_WIDTH, 2 * SSM_WIDTH)), wspec((1, D_MODEL)), wspec((D_MODEL, D_MODEL)),
                  wspec((1, D_MODEL)), wspec((1, D_MODEL)), wspec((D_MODEL, D_FF)),
                  wspec((D_FF, D_MODEL)), wspec((1, D_MODEL))],
        out_specs=row_spec(D_MODEL),
        out_shape=jax.ShapeDtypeStruct((bsz, seq, D_MODEL), _F32),
        compiler_params=pltpu.CompilerParams(
            dimension_semantics=("parallel", "parallel"), vmem_limit_bytes=VMEM_LIMIT),
        name="post",
    )(x, attn, y, wglu, gmix, wout, gpm, gpre, wup, wdn, gpost)


def _rope_tables(pos, tile):
    half = QK_ROPE_DIM // 2
    inv = 1.0 / (ROPE_BASE ** (jnp.arange(0, QK_ROPE_DIM, 2, dtype=_F32) / QK_ROPE_DIM))
    ang = pos.astype(_F32)[:, None, :] * inv[None, :, None]
    bsz, seq = pos.shape
    rope = jnp.stack([jnp.cos(ang), jnp.sin(ang)], axis=1)
    return rope.reshape(bsz, 2, half, seq // tile, tile).transpose(0, 3, 1, 2, 4)


def _prep_weights(w_in, w_uq, w_ukv):
    scale = QK_HEAD_DIM ** -0.5 * math.log2(math.e)
    win = w_in[:, :OFF_KR].astype(_BF)
    wkr_t = w_in[:, OFF_KR:OFF_U].T.astype(_BF)
    wu_t = w_in[:, OFF_U:].T.astype(_BF)
    wq_t = (w_uq * scale).T.astype(_BF)
    wkv3 = w_ukv.reshape(KV_LORA_RANK, N_HEADS, QK_NOPE_DIM + V_HEAD_DIM)
    wk = jnp.concatenate([wkv3[..., :QK_NOPE_DIM],
                          jnp.zeros((KV_LORA_RANK, N_HEADS, HEAD_PAD - QK_NOPE_DIM), _F32)], axis=-1)
    wk = wk.reshape(KV_LORA_RANK, N_HEADS * HEAD_PAD).astype(_BF)
    wv_t = wkv3[..., QK_NOPE_DIM:].reshape(KV_LORA_RANK, N_HEADS * V_HEAD_DIM).T.astype(_BF)
    return win, wkr_t, wu_t, wq_t, wk, wv_t


def kernel(x, positions, meta_tokens, g_pre_mix, w_in, g_q_lat, w_uq, g_kv_lat, w_ukv,
           ssm_A_re, ssm_A_im, ssm_log_dt, ssm_B_re, ssm_B_im, ssm_C_re, ssm_C_im, ssm_D,
           w_glu, g_mix_out, w_out, g_post_mix, g_pre_mlp, w_mlp_up, w_mlp_down, g_post_mlp):
    bsz, seq, _ = x.shape
    assert seq % ROW_TILE == 0 and ROW_TILE % SLAB_T == 0 and bsz <= SUBLANES
    assert seq % PROJ_TILE == 0 and PROJ_TILE % SLAB_T == 0
    assert N_META <= SSM_CHUNK
    row = lambda g: g.reshape(1, -1).astype(_F32)

    win, wkr_t, wu_t, wq_t, wk, wv_t = _prep_weights(w_in[0], w_uq[0], w_ukv[0])
    weights = (row(g_pre_mix[0]), win, wkr_t, wu_t, row(g_q_lat[0]), wq_t, row(g_kv_lat[0]), wk, wv_t)
    rope = _rope_tables(positions.astype(jnp.int32) + N_META, PROJ_TILE)
    qt, k, vt, u, q_norm, k_sq = _proj_call(x, rope, *weights, tile=PROJ_TILE, meta=False)
    meta_x = jnp.pad(meta_tokens.astype(x.dtype), ((0, LANES - N_META), (0, 0)))[None]
    rope_m = _rope_tables(jnp.arange(LANES, dtype=jnp.int32)[None], LANES)
    k_m, vt_m, u_m = _proj_call(meta_x, rope_m, *weights, tile=LANES, meta=True)
    km, vtm, u_m = k_m[0], vt_m[0, :, 0], u_m[:, :N_META]
    k_m_sq = jnp.max(jnp.sum(jnp.square(k_m[0].astype(_F32)), axis=-1), axis=-1)
    k_max = jnp.sqrt(jnp.maximum(jnp.max(k_sq, axis=(2, 3, 4)), k_m_sq[None]))
    mrow = q_norm * (k_max * BOUND_SLACK)[:, :, None, None, None]
    attn = lax.cond(jnp.max(mrow) <= BOUND_LIMIT,
                    lambda: _attn_call(qt, mrow, k, vt, km, vtm, bounded=True),
                    lambda: _attn_call(qt, mrow, k, vt, km, vtm, bounded=False))

    um = u_m[0].astype(_BF).reshape(N_META, SSM_GROUPS, SSM_GROUP).transpose(1, 2, 0)
    um = jnp.pad(um, ((0, 0), (0, 0), (SSM_CHUNK - N_META, 0))).reshape(SSM_GROUPS, 1, CHUNK_W)
    um = jnp.broadcast_to(um, (SSM_GROUPS, SUBLANES, CHUNK_W))
    lags, w_mat, v_mat, tab = _s5_matrices(ssm_A_re[0], ssm_A_im[0], ssm_log_dt[0], ssm_B_re[0],
                                            ssm_B_im[0], ssm_C_re[0], ssm_C_im[0], ssm_D[0])
    nslab = seq // SLAB_T
    yg = _s5_call(u.reshape(SSM_GROUPS, nslab * bsz * SSM_GROUP, SLAB_T), um, _chunk_permutation(),
                  lags, w_mat.astype(_BF), v_mat.astype(_BF), tab, nslab=nslab, bsz=bsz)
    y = yg.reshape(SSM_GROUPS, nslab, bsz * SSM_GROUP, SLAB_T)

    return _post_call(x, attn, y, w_glu[0].astype(_BF), row(g_mix_out[0]), w_out[0].astype(_BF),
                      row(g_post_mix[0]), row(g_pre_mlp[0]), w_mlp_up[0].astype(_BF),
                      w_mlp_down[0].astype(_BF), row(g_post_mlp[0]), tile=ROW_TILE)
```

```python
import functools
import math

import jax
import jax.numpy as jnp
from jax import lax
from jax.experimental import pallas as pl
from jax.experimental.pallas import tpu as pltpu

D_MODEL = 1024
N_META = 16
ATTN_WIDTH = 512
SSM_WIDTH = 512
N_HEADS = 8
V_HEAD_DIM = 64
QK_NOPE_DIM = 64
QK_ROPE_DIM = 32
QK_HEAD_DIM = QK_NOPE_DIM + QK_ROPE_DIM
Q_LORA_RANK = 384
KV_LORA_RANK = 256
ROPE_BASE = 10000.0
SSM_GROUP = 16
SSM_GROUPS = 32
SSM_STATE = 64
D_FF = 4 * D_MODEL
EPS = 1e-6
OFF_KR = Q_LORA_RANK + KV_LORA_RANK
OFF_U = OFF_KR + QK_ROPE_DIM

LANES = 128
SUBLANES = 8
HEAD_PAD = LANES
ONES_LANE = V_HEAD_DIM
ATTN_TQ = 1024
ATTN_TK = 256
V_ROWS = 112
PW_Q = 0
PW_KV = PW_Q + Q_LORA_RANK
PW_END = PW_KV + KV_LORA_RANK

SSM_CHUNK = 32
CHUNK_W = SSM_CHUNK * SSM_GROUP
STATE_W = 4 * SSM_STATE
SLAB_T = LANES
SLAB_CHUNKS = SLAB_T // SSM_CHUNK

ROW_TILE = 512
PROJ_TILE = 1024
FF_TILE = 1024
POST_SLABS = 2
BOUND_SLACK = 1.0 + 2.0 ** -6
BOUND_LIMIT = 40.0
V7X_VMEM_BYTES = 64 * 1024 * 1024
VMEM_LIMIT = V7X_VMEM_BYTES - 8 * 1024 * 1024

_BF = jnp.bfloat16
_F32 = jnp.float32


def _dot(a, b):
    return jnp.dot(a, b, preferred_element_type=_F32)


def _rms(x, g):
    return x * lax.rsqrt(jnp.mean(x * x, axis=-1, keepdims=True) + EPS) * g


_NT = (((1,), (1,)), ((), ()))


def _rotate(x1, x2, cos_t, sin_t):
    return x1 * cos_t - x2 * sin_t, x1 * sin_t + x2 * cos_t


def _proj_kernel(x_ref, rope_ref, *refs, meta):
    if meta:
        gpre_ref, win_ref, wkr_ref, wu_ref, gkv_ref, wk_ref, wv_ref, k_ref, v_ref, u_ref = refs
    else:
        (gpre_ref, win_ref, wkr_ref, wu_ref, gq_ref, wq_ref, gkv_ref, wk_ref, wv_ref,
         q_ref, k_ref, v_ref, u_ref, qn_ref, kmx_ref) = refs
    tile = x_ref.shape[0]
    half = QK_ROPE_DIM // 2
    cos_t, sin_t = rope_ref[0], rope_ref[1]
    xn = _rms(x_ref[...], gpre_ref[...]).astype(_BF)
    proj = _dot(xn, win_ref[...])
    kvn = _rms(proj[:, PW_KV:PW_END], gkv_ref[...]).astype(_BF)
    krt = lax.dot_general(wkr_ref[...], xn, _NT, preferred_element_type=_F32)
    r1, r2 = _rotate(krt[:half], krt[half:], cos_t, sin_t)
    kr = jnp.concatenate([jnp.zeros((QK_NOPE_DIM, tile), _F32), r1, r2,
                          jnp.zeros((HEAD_PAD - QK_HEAD_DIM, tile), _F32)], axis=0).T
    kk = _dot(kvn, wk_ref[...])
    vt = lax.dot_general(wv_ref[...], kvn, _NT, preferred_element_type=_F32)
    ones_tail = (lax.broadcasted_iota(jnp.int32, (HEAD_PAD - V_HEAD_DIM, tile), 0) == 0).astype(_F32)
    if meta:
        u_ref[...] = lax.dot_general(xn, wu_ref[...], _NT, preferred_element_type=_F32)
    else:
        ut = lax.dot_general(wu_ref[...], xn, _NT, preferred_element_type=_F32)
        for g in range(SSM_GROUPS):
            for c in range(tile // SLAB_T):
                u_ref[g, c] = ut[g * SSM_GROUP:(g + 1) * SSM_GROUP, c * SLAB_T:(c + 1) * SLAB_T]
        qn = _rms(proj[:, PW_Q:PW_KV], gq_ref[...]).astype(_BF)
        qt = lax.dot_general(wq_ref[...], qn, _NT, preferred_element_type=_F32)
        zero_rows = jnp.zeros((HEAD_PAD - QK_HEAD_DIM, tile), _F32)
    for h in range(N_HEADS):
        k_h = (kk[:, h * HEAD_PAD:(h + 1) * HEAD_PAD] + kr).astype(_BF)
        k_ref[h] = k_h
        v_ref[h] = jnp.concatenate([vt[h * V_HEAD_DIM:(h + 1) * V_HEAD_DIM], ones_tail], axis=0).astype(_BF)
        if not meta:
            blk = qt[h * QK_HEAD_DIM:(h + 1) * QK_HEAD_DIM]
            r1, r2 = _rotate(blk[QK_NOPE_DIM:QK_NOPE_DIM + half], blk[QK_NOPE_DIM + half:], cos_t, sin_t)
            qt_h = jnp.concatenate([blk[:QK_NOPE_DIM], r1, r2, zero_rows], axis=0).astype(_BF)
            q_ref[h] = qt_h
            qt_f = qt_h.astype(_F32)
            qn_ref[h] = jnp.sqrt(jnp.sum(qt_f * qt_f, axis=0, keepdims=True))
            k_f = k_h.astype(_F32)
            kmx_ref[h] = jnp.broadcast_to(jnp.max(jnp.sum(k_f * k_f, axis=1, keepdims=True), axis=0, keepdims=True),
                                          (1, LANES))


def _const_spec(shape):
    nd = len(shape)
    return pl.BlockSpec(shape, lambda *_: (0,) * nd)


def _proj_call(x, rope, gpre, win, wkr_t, wu_t, gq, wq_t, gkv, wk, wv_t, *, tile, meta):
    bsz, seq, _ = x.shape
    nt = seq // tile
    row_spec = lambda w: pl.BlockSpec((None, tile, w), lambda b, i: (b, i, 0))
    rope_spec = pl.BlockSpec((None, None, 2, QK_ROPE_DIM // 2, tile), lambda b, i: (b, i, 0, 0, 0))
    k_spec = pl.BlockSpec((None, N_HEADS, tile, HEAD_PAD), lambda b, i: (b, 0, i, 0))
    t_spec = pl.BlockSpec((None, N_HEADS, None, HEAD_PAD, tile), lambda b, i: (b, 0, i, 0, 0))
    k_shape = jax.ShapeDtypeStruct((bsz, N_HEADS, seq, HEAD_PAD), _BF)
    t_shape = jax.ShapeDtypeStruct((bsz, N_HEADS, nt, HEAD_PAD, tile), _BF)
    w_specs = lambda *ws: [_const_spec(w.shape) for w in ws]
    if meta:
        args = (x, rope, gpre, win, wkr_t, wu_t, gkv, wk, wv_t)
        in_specs = [row_spec(D_MODEL), rope_spec] + w_specs(*args[2:])
        out_specs = [k_spec, t_spec, row_spec(SSM_WIDTH)]
        out_shape = [k_shape, t_shape, jax.ShapeDtypeStruct((bsz, seq, SSM_WIDTH), _F32)]
    else:
        args = (x, rope, gpre, win, wkr_t, wu_t, gq, wq_t, gkv, wk, wv_t)
        in_specs = [row_spec(D_MODEL), rope_spec] + w_specs(*args[2:])
        norm_spec = lambda w: pl.BlockSpec((None, N_HEADS, None, 1, w), lambda b, i: (b, 0, i, 0, 0))
        u_spec = pl.BlockSpec((SSM_GROUPS, tile // SLAB_T, SSM_GROUP, SLAB_T), lambda b, i: (0, i, b, 0))
        out_specs = [t_spec, k_spec, t_spec, u_spec, norm_spec(tile), norm_spec(LANES)]
        out_shape = [t_shape, k_shape, t_shape,
                     jax.ShapeDtypeStruct((SSM_GROUPS, seq // SLAB_T, bsz * SSM_GROUP, SLAB_T), _F32),
                     jax.ShapeDtypeStruct((bsz, N_HEADS, nt, 1, tile), _F32),
                     jax.ShapeDtypeStruct((bsz, N_HEADS, nt, 1, LANES), _F32)]
    return pl.pallas_call(
        functools.partial(_proj_kernel, meta=meta),
        grid=(bsz, nt),
        in_specs=in_specs,
        out_specs=out_specs,
        out_shape=out_shape,
        compiler_params=pltpu.CompilerParams(
            dimension_semantics=("parallel", "parallel"), vmem_limit_bytes=VMEM_LIMIT),
        name="proj_meta" if meta else "proj",
    )(*args)


def _attn_finish(accs, o_ref):
    halves = [(acc * (1.0 / acc[ONES_LANE:ONES_LANE + 1, :]))[:V_HEAD_DIM] for acc in accs]
    o_ref[...] = jnp.concatenate(halves, axis=0).T


def _lane_concat(ref, hh):
    return jnp.concatenate([ref[hh, j] for j in range(ref.shape[1])], axis=1)


def _attn_bounded_kernel(qt_ref, mrow_ref, k_ref, vt_ref, km_ref, vtm_ref, o_ref, *, nk, tk):
    tq = o_ref.shape[0]
    per_slab = vt_ref.shape[3] // tk
    key_row = lax.broadcasted_iota(jnp.int32, (LANES, tq), 0)
    accs = []
    for hh in range(2):
        qt = _lane_concat(qt_ref, hh)
        mrow = _lane_concat(mrow_ref, hh)
        s0 = jnp.where(key_row < N_META, _dot(km_ref[hh], qt), -jnp.inf)
        acc = _dot(vtm_ref[hh, :V_ROWS, :], jnp.exp2(s0 - mrow).astype(_BF))
        scores = lambda c: _dot(k_ref[hh, c * tk:(c + 1) * tk, :], qt)
        s_next = scores(0)
        for c in range(nk):
            s = s_next
            if c + 1 < nk:
                s_next = scores(c + 1)
            vt_c = vt_ref[hh, c // per_slab, :V_ROWS, (c % per_slab) * tk:(c % per_slab + 1) * tk]
            acc = acc + _dot(vt_c, jnp.exp2(s - mrow).astype(_BF))
        accs.append(acc)
    _attn_finish(accs, o_ref)


def _attn_online_kernel(qt_ref, k_ref, vt_ref, km_ref, vtm_ref, o_ref, s0_scr, s1_scr, m_scr, acc_scr, *, nk, tk):
    tq = o_ref.shape[0]
    key_row = lax.broadcasted_iota(jnp.int32, (LANES, tq), 0)
    for hh in range(2):
        s0 = jnp.where(key_row < N_META, _dot(km_ref[hh], _lane_concat(qt_ref, hh)), -jnp.inf)
        m0 = jnp.max(s0, axis=0, keepdims=True)
        m_scr[hh] = m0
        acc_scr[hh] = _dot(vtm_ref[hh, :V_ROWS, :], jnp.exp2(s0 - m0).astype(_BF))

    def scores(buf, c):
        off = pl.multiple_of(c * tk, tk)
        for hh in range(2):
            buf[hh] = _dot(k_ref[hh, pl.ds(off, tk), :], _lane_concat(qt_ref, hh))

    def accumulate(buf, c):
        for hh in range(2):
            s = buf[hh]
            m = m_scr[hh]
            m_new = jnp.maximum(m, jnp.max(s, axis=0, keepdims=True))
            m_scr[hh] = m_new
            p = jnp.exp2(s - m_new).astype(_BF)
            acc_scr[hh] = jnp.exp2(m - m_new) * acc_scr[hh] + _dot(vt_ref[hh, c, :V_ROWS, :], p)

    scores(s0_scr, 0)

    def body(t, _):
        scores(s1_scr, 2 * t + 1)
        accumulate(s0_scr, 2 * t)
        scores(s0_scr, 2 * t + 2)
        accumulate(s1_scr, 2 * t + 1)
        return 0

    lax.fori_loop(0, nk // 2 - 1, body, 0)
    scores(s1_scr, nk - 1)
    accumulate(s0_scr, nk - 2)
    accumulate(s1_scr, nk - 1)
    _attn_finish([acc_scr[0], acc_scr[1]], o_ref)


def _attn_call(qt, mrow, k, vt, km, vtm, *, bounded):
    bsz, _, nslab, _, slab = qt.shape
    seq = k.shape[2]
    tq = min(ATTN_TQ, seq)
    tk = min(ATTN_TK, slab)
    qs = tq // slab
    assert tq % slab == 0 and seq % tq == 0 and slab % tk == 0
    q_spec = lambda rows: pl.BlockSpec((None, 2, qs, rows, slab), lambda b, hp, i: (b, hp, i, 0, 0))
    kv_specs = [pl.BlockSpec((None, 2, seq, HEAD_PAD), lambda b, hp, i: (b, hp, 0, 0)),
                pl.BlockSpec((None, 2, nslab, HEAD_PAD, slab), lambda b, hp, i: (b, hp, 0, 0, 0)),
                pl.BlockSpec((2, LANES, HEAD_PAD), lambda b, hp, i: (hp, 0, 0)),
                pl.BlockSpec((2, HEAD_PAD, LANES), lambda b, hp, i: (hp, 0, 0))]
    if bounded:
        body = functools.partial(_attn_bounded_kernel, nk=seq // tk, tk=tk)
        in_specs = [q_spec(HEAD_PAD), q_spec(1)] + kv_specs
        args, scratch = (qt, mrow, k, vt, km, vtm), []
    else:
        assert nslab % 2 == 0 and nslab >= 4
        body = functools.partial(_attn_online_kernel, nk=nslab, tk=slab)
        in_specs = [q_spec(HEAD_PAD)] + kv_specs
        args = (qt, k, vt, km, vtm)
        scratch = [pltpu.VMEM((2, slab, tq), _F32), pltpu.VMEM((2, slab, tq), _F32),
                   pltpu.VMEM((2, 1, tq), _F32), pltpu.VMEM((2, V_ROWS, tq), _F32)]
    return pl.pallas_call(
        body,
        grid=(bsz, N_HEADS // 2, seq // tq),
        in_specs=in_specs,
        out_specs=pl.BlockSpec((None, tq, LANES), lambda b, hp, i: (b, i, hp)),
        out_shape=jax.ShapeDtypeStruct((bsz, seq, ATTN_WIDTH), _F32),
        scratch_shapes=scratch,
        compiler_params=pltpu.CompilerParams(
            dimension_semantics=("parallel", "parallel", "arbitrary"), vmem_limit_bytes=VMEM_LIMIT),
        name="attn_bounded" if bounded else "attn_online",
    )(*args)


def _cmul_add(ar, ai, xr, xi, sr, si):
    return ar * xr - ai * xi + sr, ar * xi + ai * xr + si


def _s5_kernel(a_ref, um_ref, q_ref, lag_ref, w_ref, v_ref, t_ref, y_ref, sup_scr, ent_scr,
               *, nslab, bsz):
    rows = nslab * bsz
    half = STATE_W // 2
    a = [a_ref[pl.ds(i, rows, stride=SSM_GROUP), :] for i in range(SSM_GROUP)]
    uc = [jnp.concatenate([x[:, c * SSM_CHUNK:(c + 1) * SSM_CHUNK] for x in a], axis=1).astype(_BF)
          for c in range(SLAB_CHUNKS)]
    w = w_ref[...]
    s = [_dot(u, w) for u in uc]
    sr = [x[:, :half] for x in s]
    si = [x[:, half:] for x in s]
    t = t_ref[...]
    trow = lambda r: t[r:r + 1, :]
    sup_r = sup_i = None
    for c in range(SLAB_CHUNKS):
        cr, ci = trow(c), trow(SLAB_CHUNKS + c)
        pr = cr * sr[c] - ci * si[c]
        pi = cr * si[c] + ci * sr[c]
        sup_r = pr if sup_r is None else sup_r + pr
        sup_i = pi if sup_i is None else sup_i + pi
    sup_scr[:, :half] = sup_r
    sup_scr[:, half:] = sup_i

    lane = lax.broadcasted_iota(jnp.int32, (bsz, half), 1)
    fwd = lane < SSM_STATE
    sm = _dot(um_ref[...], w)
    xr = jnp.where(fwd, sm[:bsz, :half], 0.0)
    xi = jnp.where(fwd, sm[:bsz, half:], 0.0)
    a_slab_r, a_slab_i = trow(2 * SLAB_CHUNKS), trow(2 * SLAB_CHUNKS + 1)
    for j in range(nslab):
        rf = j * bsz
        rb = (nslab - 1 - j) * bsz
        ent_scr[rf:rf + bsz, 0:SSM_STATE] = xr[:, 0:SSM_STATE]
        ent_scr[rb:rb + bsz, SSM_STATE:half] = xr[:, SSM_STATE:half]
        ent_scr[rf:rf + bsz, half:half + SSM_STATE] = xi[:, 0:SSM_STATE]
        ent_scr[rb:rb + bsz, half + SSM_STATE:STATE_W] = xi[:, SSM_STATE:half]
        s_r = jnp.where(fwd, sup_scr[rf:rf + bsz, :half], sup_scr[rb:rb + bsz, :half])
        s_i = jnp.where(fwd, sup_scr[rf:rf + bsz, half:], sup_scr[rb:rb + bsz, half:])
        xr, xi = _cmul_add(a_slab_r, a_slab_i, xr, xi, s_r, s_i)

    ent = ent_scr[...]
    a_r, a_i = trow(2 * SLAB_CHUNKS + 2), trow(2 * SLAB_CHUNKS + 3)
    xf = [(ent[:, :half], ent[:, half:])]
    for c in range(1, SLAB_CHUNKS):
        xf.append(_cmul_add(a_r, a_i, xf[-1][0], xf[-1][1], sr[c - 1], si[c - 1]))
    xb = [(ent[:, :half], ent[:, half:])]
    for c in range(SLAB_CHUNKS - 2, -1, -1):
        xb.insert(0, _cmul_add(a_r, a_i, xb[0][0], xb[0][1], sr[c + 1], si[c + 1]))
    fwd_rows = lax.broadcasted_iota(jnp.int32, (rows, half), 1) < SSM_STATE
    m = jnp.concatenate(
        [pltpu.roll(jnp.broadcast_to(lag_ref[i:i + 1, :], (SSM_CHUNK, 2 * CHUNK_W)), 0, 1,
                    stride=SSM_GROUP, stride_axis=0)[:, :CHUNK_W] for i in range(SSM_GROUP)], axis=0).astype(_BF)
    q = q_ref[...]
    m = _dot(m, q).astype(_BF)
    v = _dot(v_ref[...], q).astype(_BF)
    ys = []
    for c in range(SLAB_CHUNKS):
        xin = jnp.concatenate([jnp.where(fwd_rows, xf[c][0], xb[c][0]),
                               jnp.where(fwd_rows, xf[c][1], xb[c][1])], axis=1).astype(_BF)
        ys.append(_dot(uc[c], m) + _dot(xin, v))
    for o in range(SSM_GROUP):
        y_ref[pl.ds(o, rows, stride=SSM_GROUP), :] = jnp.concatenate(
            [y[:, o * SSM_CHUNK:(o + 1) * SSM_CHUNK] for y in ys], axis=1)


def _s5_call(a, um, perm, lags, w_mat, v_mat, tab, *, nslab, bsz):
    n = nslab * bsz * SSM_GROUP
    rows = nslab * bsz
    g_spec = lambda *shape: pl.BlockSpec((None,) + shape, lambda g: (g,) + (0,) * len(shape))
    perm_spec = pl.BlockSpec((CHUNK_W, CHUNK_W), lambda g: (0, 0), pipeline_mode=pl.Buffered(1))
    return pl.pallas_call(
        functools.partial(_s5_kernel, nslab=nslab, bsz=bsz),
        grid=(SSM_GROUPS,),
        in_specs=[g_spec(n, SLAB_T), g_spec(SUBLANES, CHUNK_W), perm_spec,
                  g_spec(SSM_GROUP, 2 * CHUNK_W), g_spec(CHUNK_W, STATE_W), g_spec(STATE_W, CHUNK_W),
                  g_spec(2 * SUBLANES, STATE_W // 2)],
        out_specs=g_spec(n, SLAB_T),
        out_shape=jax.ShapeDtypeStruct((SSM_GROUPS, n, SLAB_T), _F32),
        scratch_shapes=[pltpu.VMEM((rows, STATE_W), _F32), pltpu.VMEM((rows, STATE_W), _F32)],
        compiler_params=pltpu.CompilerParams(
            dimension_semantics=("parallel",), vmem_limit_bytes=VMEM_LIMIT),
        name="s5",
    )(a, um, perm, lags, w_mat, v_mat, tab)


def _s5_matrices(a_re, a_im, log_dt, b_re, b_im, c_re, c_im, d_skip):
    tc = SSM_CHUNK
    lam = lax.complex(jnp.minimum(a_re.astype(_F32), -1e-4), a_im.astype(_F32))
    dt = jnp.exp(log_dt.astype(_F32))[..., None]
    lam_dt = lam * dt
    lam_bar = jnp.exp(lam_dt)
    b_bar = ((lam_bar - 1.0) / lam)[..., None] * lax.complex(b_re.astype(_F32), b_im.astype(_F32))
    c_c = lax.complex(c_re.astype(_F32), c_im.astype(_F32))
    k_idx = jnp.arange(tc + 1, dtype=_F32)
    pw = jnp.exp(lam_dt[:, :, None, :] * k_idx[None, None, :, None])
    kern = jnp.real(jnp.einsum('dgop,dgkp,dgpi->dgkoi', c_c, pw[:, :, :tc], b_bar))
    d_g = d_skip.astype(_F32).reshape(SSM_GROUPS, SSM_GROUP)
    center = kern[0][:, :1] + kern[1][:, :1] + (jnp.eye(SSM_GROUP, dtype=_F32)[None] * d_g[:, :, None])[:, None]
    lags = jnp.concatenate([center, kern[0][:, 1:], jnp.zeros_like(center), kern[1][:, :0:-1]], axis=1)
    lags = lags.transpose(0, 3, 1, 2).reshape(SSM_GROUPS, SSM_GROUP, 2 * CHUNK_W)
    wf = b_bar[0].transpose(0, 2, 1)[:, :, None, :] * pw[0][:, tc - 1::-1][:, None, :, :]
    wb = b_bar[1].transpose(0, 2, 1)[:, :, None, :] * pw[1][:, :tc][:, None, :, :]
    w_mat = jnp.concatenate([jnp.real(wf), jnp.real(wb), jnp.imag(wf), jnp.imag(wb)], axis=-1)
    w_mat = w_mat.reshape(SSM_GROUPS, CHUNK_W, STATE_W)
    gf = pw[0][:, 1:tc + 1][:, :, None, :] * c_c[0][:, None, :, :]
    gb = pw[1][:, tc:0:-1][:, :, None, :] * c_c[1][:, None, :, :]
    v_mat = jnp.concatenate([jnp.real(gf), jnp.real(gb), -jnp.imag(gf), -jnp.imag(gb)], axis=-1)
    v_mat = v_mat.reshape(SSM_GROUPS, CHUNK_W, STATE_W).transpose(0, 2, 1)
    n_idx = jnp.arange(SLAB_CHUNKS + 1, dtype=_F32) * tc
    pc = jnp.exp(lam_dt[:, :, None, :] * n_idx[None, None, :, None])
    coef = jnp.concatenate([pc[0][:, SLAB_CHUNKS - 1::-1], pc[1][:, :SLAB_CHUNKS]], axis=-1)
    both = lambda n: jnp.concatenate([pc[0][:, n], pc[1][:, n]], axis=-1)[:, None, :]
    a_slab, a_chunk = both(SLAB_CHUNKS), both(1)
    tab = jnp.concatenate([jnp.real(coef), jnp.imag(coef), jnp.real(a_slab), jnp.imag(a_slab),
                           jnp.real(a_chunk), jnp.imag(a_chunk)], axis=1)
    tab = jnp.pad(tab, ((0, 0), (0, 2 * SUBLANES - tab.shape[1]), (0, 0)))
    return lags, w_mat, v_mat, tab


def _chunk_permutation():
    r = lax.broadcasted_iota(jnp.int32, (CHUNK_W, CHUNK_W), 0)
    c = lax.broadcasted_iota(jnp.int32, (CHUNK_W, CHUNK_W), 1)
    return (r == (c % SSM_CHUNK) * SSM_GROUP + c // SSM_CHUNK).astype(_BF)


def _post_kernel(x_ref, attn_ref, y_ref, wglu_ref, gmix_ref, wout_ref, gpm_ref, gpre_ref,
                 wup_ref, wdn_ref, gpost_ref, o_ref):
    nslab = y_ref.shape[1]
    gmix = gmix_ref[...]
    per = min(POST_SLABS, nslab)
    blocks = range(nslab // per)
    rows = [pl.ds(r * per * SLAB_T, per * SLAB_T) for r in blocks]
    gy = []
    for r in blocks:
        yt = jnp.concatenate([jnp.concatenate([y_ref[g, r * per + c] for c in range(per)], axis=1)
                              for g in range(SSM_GROUPS)], axis=0)
        y = yt.T
        gy.append((0.5 * y * (1.0 + jnp.tanh(math.sqrt(2.0 / math.pi) * (y + 0.044715 * (y * y * y))))).astype(_BF))
    z = [_dot(gy[r], wglu_ref[...]) for r in blocks]
    mix = []
    for r in blocks:
        ssm = z[r][:, :SSM_WIDTH] * (1.0 / (1.0 + jnp.exp(-z[r][:, SSM_WIDTH:])))
        mix.append(jnp.concatenate([_rms(attn_ref[rows[r], :], gmix[:, :ATTN_WIDTH]),
                                    _rms(ssm, gmix[:, ATTN_WIDTH:])], axis=-1).astype(_BF))
    mixed = [_dot(mix[r], wout_ref[...]) for r in blocks]
    h1 = [x_ref[rows[r], :] + _rms(mixed[r], gpm_ref[...]) for r in blocks]
    hn = [_rms(h1[r], gpre_ref[...]).astype(_BF) for r in blocks]
    acc = [None for _ in blocks]
    for c in range(D_FF // FF_TILE):
        for r in blocks:
            up = jnp.maximum(_dot(hn[r], wup_ref[:, c * FF_TILE:(c + 1) * FF_TILE]), 0.0)
            part = _dot((up * up).astype(_BF), wdn_ref[c * FF_TILE:(c + 1) * FF_TILE, :])
            acc[r] = part if acc[r] is None else acc[r] + part
    for r in blocks:
        o_ref[rows[r], :] = h1[r] + _rms(acc[r], gpost_ref[...])


def _post_call(x, attn, y, wglu, gmix, wout, gpm, gpre, wup, wdn, gpost, *, tile):
    bsz, seq, _ = x.shape
    row_spec = lambda w: pl.BlockSpec((None, tile, w), lambda b, i: (b, i, 0))
    wspec = lambda shape: pl.BlockSpec(shape, lambda b, i: (0, 0), pipeline_mode=pl.Buffered(1))
    y_spec = pl.BlockSpec((SSM_GROUPS, tile // SLAB_T, SSM_GROUP, SLAB_T), lambda b, i: (0, i, b, 0))
    return pl.pallas_call(
        _post_kernel,
        grid=(bsz, seq // tile),
        in_specs=[row_spec(D_MODEL), row_spec(ATTN_WIDTH), y_spec,
                  wspec((SSM_WIDTH, 2 * SSM_WIDTH)), wspec((1, D_MODEL)), wspec((D_MODEL, D_MODEL)),
                  wspec((1, D_MODEL)), wspec((1, D_MODEL)), wspec((D_MODEL, D_FF)),
                  wspec((D_FF, D_MODEL)), wspec((1, D_MODEL))],
        out_specs=row_spec(D_MODEL),
        out_shape=jax.ShapeDtypeStruct((bsz, seq, D_MODEL), _F32),
        compiler_params=pltpu.CompilerParams(
            dimension_semantics=("parallel", "parallel"), vmem_limit_bytes=VMEM_LIMIT),
        name="post",
    )(x, attn, y, wglu, gmix, wout, gpm, gpre, wup, wdn, gpost)


def _rope_tables(pos, tile):
    half = QK_ROPE_DIM // 2
    inv = 1.0 / (ROPE_BASE ** (jnp.arange(0, QK_ROPE_DIM, 2, dtype=_F32) / QK_ROPE_DIM))
    ang = pos.astype(_F32)[:, None, :] * inv[None, :, None]
    bsz, seq = pos.shape
    rope = jnp.stack([jnp.cos(ang), jnp.sin(ang)], axis=1)
    return rope.reshape(bsz, 2, half, seq // tile, tile).transpose(0, 3, 1, 2, 4)


def _prep_weights(w_in, w_uq, w_ukv):
    scale = QK_HEAD_DIM ** -0.5 * math.log2(math.e)
    win = w_in[:, :OFF_KR].astype(_BF)
    wkr_t = w_in[:, OFF_KR:OFF_U].T.astype(_BF)
    wu_t = w_in[:, OFF_U:].T.astype(_BF)
    wq_t = (w_uq * scale).T.astype(_BF)
    wkv3 = w_ukv.reshape(KV_LORA_RANK, N_HEADS, QK_NOPE_DIM + V_HEAD_DIM)
    wk = jnp.concatenate([wkv3[..., :QK_NOPE_DIM],
                          jnp.zeros((KV_LORA_RANK, N_HEADS, HEAD_PAD - QK_NOPE_DIM), _F32)], axis=-1)
    wk = wk.reshape(KV_LORA_RANK, N_HEADS * HEAD_PAD).astype(_BF)
    wv_t = wkv3[..., QK_NOPE_DIM:].reshape(KV_LORA_RANK, N_HEADS * V_HEAD_DIM).T.astype(_BF)
    return win, wkr_t, wu_t, wq_t, wk, wv_t


def kernel(x, positions, meta_tokens, g_pre_mix, w_in, g_q_lat, w_uq, g_kv_lat, w_ukv,
           ssm_A_re, ssm_A_im, ssm_log_dt, ssm_B_re, ssm_B_im, ssm_C_re, ssm_C_im, ssm_D,
           w_glu, g_mix_out, w_out, g_post_mix, g_pre_mlp, w_mlp_up, w_mlp_down, g_post_mlp):
    bsz, seq, _ = x.shape
    assert seq % ROW_TILE == 0 and ROW_TILE % SLAB_T == 0 and bsz <= SUBLANES
    assert seq % PROJ_TILE == 0 and PROJ_TILE % SLAB_T == 0
    assert N_META <= SSM_CHUNK
    row = lambda g: g.reshape(1, -1).astype(_F32)

    win, wkr_t, wu_t, wq_t, wk, wv_t = _prep_weights(w_in[0], w_uq[0], w_ukv[0])
    weights = (row(g_pre_mix[0]), win, wkr_t, wu_t, row(g_q_lat[0]), wq_t, row(g_kv_lat[0]), wk, wv_t)
    rope = _rope_tables(positions.astype(jnp.int32) + N_META, PROJ_TILE)
    qt, k, vt, u, q_norm, k_sq = _proj_call(x, rope, *weights, tile=PROJ_TILE, meta=False)
    meta_x = jnp.pad(meta_tokens.astype(x.dtype), ((0, LANES - N_META), (0, 0)))[None]
    rope_m = _rope_tables(jnp.arange(LANES, dtype=jnp.int32)[None], LANES)
    k_m, vt_m, u_m = _proj_call(meta_x, rope_m, *weights, tile=LANES, meta=True)
    km, vtm, u_m = k_m[0], vt_m[0, :, 0], u_m[:, :N_META]
    k_m_sq = jnp.max(jnp.sum(jnp.square(k_m[0].astype(_F32)), axis=-1), axis=-1)
    k_max = jnp.sqrt(jnp.maximum(jnp.max(k_sq, axis=(2, 3, 4)), k_m_sq[None]))
    mrow = q_norm * (k_max * BOUND_SLACK)[:, :, None, None, None]
    attn = lax.cond(jnp.max(mrow) <= BOUND_LIMIT,
                    lambda: _attn_call(qt, mrow, k, vt, km, vtm, bounded=True),
                    lambda: _attn_call(qt, mrow, k, vt, km, vtm, bounded=False))

    um = u_m[0].astype(_BF).reshape(N_META, SSM_GROUPS, SSM_GROUP).transpose(1, 2, 0)
    um = jnp.pad(um, ((0, 0), (0, 0), (SSM_CHUNK - N_META, 0))).reshape(SSM_GROUPS, 1, CHUNK_W)
    um = jnp.broadcast_to(um, (SSM_GROUPS, SUBLANES, CHUNK_W))
    lags, w_mat, v_mat, tab = _s5_matrices(ssm_A_re[0], ssm_A_im[0], ssm_log_dt[0], ssm_B_re[0],
                                            ssm_B_im[0], ssm_C_re[0], ssm_C_im[0], ssm_D[0])
    nslab = seq // SLAB_T
    yg = _s5_call(u.reshape(SSM_GROUPS, nslab * bsz * SSM_GROUP, SLAB_T), um, _chunk_permutation(),
                  lags, w_mat.astype(_BF), v_mat.astype(_BF), tab, nslab=nslab, bsz=bsz)
    y = yg.reshape(SSM_GROUPS, nslab, bsz * SSM_GROUP, SLAB_T)

    return _post_call(x, attn, y, w_glu[0].astype(_BF), row(g_mix_out[0]), w_out[0].astype(_BF),
                      row(g_post_mix[0]), row(g_pre_mlp[0]), w_mlp_up[0].astype(_BF),
                      w_mlp_down[0].astype(_BF), row(g_post_mlp[0]), tile=ROW_TILE)
```

```python
import functools
import math

import jax
import jax.numpy as jnp
from jax import lax
from jax.experimental import pallas as pl
from jax.experimental.pallas import tpu as pltpu

D_MODEL = 1024
N_META = 16
ATTN_WIDTH = 512
SSM_WIDTH = 512
N_HEADS = 8
V_HEAD_DIM = 64
QK_NOPE_DIM = 64
QK_ROPE_DIM = 32
QK_HEAD_DIM = QK_NOPE_DIM + QK_ROPE_DIM
Q_LORA_RANK = 384
KV_LORA_RANK = 256
ROPE_BASE = 10000.0
SSM_GROUP = 16
SSM_GROUPS = 32
SSM_STATE = 64
D_FF = 4 * D_MODEL
EPS = 1e-6
OFF_KR = Q_LORA_RANK + KV_LORA_RANK
OFF_U = OFF_KR + QK_ROPE_DIM

LANES = 128
SUBLANES = 8
HEAD_PAD = LANES
ONES_LANE = V_HEAD_DIM
ATTN_TQ = 1024
ATTN_TK = 256
V_ROWS = 112
PW_Q = 0
PW_KV = PW_Q + Q_LORA_RANK
PW_END = PW_KV + KV_LORA_RANK

SSM_CHUNK = 32
CHUNK_W = SSM_CHUNK * SSM_GROUP
STATE_W = 4 * SSM_STATE
SLAB_T = LANES
SLAB_CHUNKS = SLAB_T // SSM_CHUNK

ROW_TILE = 512
PROJ_TILE = 1024
FF_TILE = 1024
POST_SLABS = 2
BOUND_SLACK = 1.0 + 2.0 ** -6
BOUND_LIMIT = 40.0
V7X_VMEM_BYTES = 64 * 1024 * 1024
VMEM_LIMIT = V7X_VMEM_BYTES - 8 * 1024 * 1024

_BF = jnp.bfloat16
_F32 = jnp.float32


def _dot(a, b):
    return jnp.dot(a, b, preferred_element_type=_F32)


def _rms(x, g):
    return x * lax.rsqrt(jnp.mean(x * x, axis=-1, keepdims=True) + EPS) * g


_NT = (((1,), (1,)), ((), ()))


def _rotate(x1, x2, cos_t, sin_t):
    return x1 * cos_t - x2 * sin_t, x1 * sin_t + x2 * cos_t


def _proj_kernel(x_ref, rope_ref, *refs, meta):
    if meta:
        gpre_ref, win_ref, wkr_ref, wu_ref, gkv_ref, wk_ref, wv_ref, k_ref, v_ref, u_ref = refs
    else:
        (gpre_ref, win_ref, wkr_ref, wu_ref, gq_ref, wq_ref, gkv_ref, wk_ref, wv_ref,
         q_ref, k_ref, v_ref, u_ref, qn_ref, kmx_ref) = refs
    tile = x_ref.shape[0]
    half = QK_ROPE_DIM // 2
    cos_t, sin_t = rope_ref[0], rope_ref[1]
    xn = _rms(x_ref[...], gpre_ref[...]).astype(_BF)
    proj = _dot(xn, win_ref[...])
    kvn = _rms(proj[:, PW_KV:PW_END], gkv_ref[...]).astype(_BF)
    krt = lax.dot_general(wkr_ref[...], xn, _NT, preferred_element_type=_F32)
    r1, r2 = _rotate(krt[:half], krt[half:], cos_t, sin_t)
    kr = jnp.concatenate([jnp.zeros((QK_NOPE_DIM, tile), _F32), r1, r2,
                          jnp.zeros((HEAD_PAD - QK_HEAD_DIM, tile), _F32)], axis=0).T
    kk = _dot(kvn, wk_ref[...])
    vt = lax.dot_general(wv_ref[...], kvn, _NT, preferred_element_type=_F32)
    ones_tail = (lax.broadcasted_iota(jnp.int32, (HEAD_PAD - V_HEAD_DIM, tile), 0) == 0).astype(_F32)
    if meta:
        u_ref[...] = lax.dot_general(xn, wu_ref[...], _NT, preferred_element_type=_F32)
    else:
        ut = lax.dot_general(wu_ref[...], xn, _NT, preferred_element_type=_F32)
        for g in range(SSM_GROUPS):
            for c in range(tile // SLAB_T):
                u_ref[g, c] = ut[g * SSM_GROUP:(g + 1) * SSM_GROUP, c * SLAB_T:(c + 1) * SLAB_T]
        qn = _rms(proj[:, PW_Q:PW_KV], gq_ref[...]).astype(_BF)
        qt = lax.dot_general(wq_ref[...], qn, _NT, preferred_element_type=_F32)
        zero_rows = jnp.zeros((HEAD_PAD - QK_HEAD_DIM, tile), _F32)
    for h in range(N_HEADS):
        k_h = (kk[:, h * HEAD_PAD:(h + 1) * HEAD_PAD] + kr).astype(_BF)
        k_ref[h] = k_h
        v_ref[h] = jnp.concatenate([vt[h * V_HEAD_DIM:(h + 1) * V_HEAD_DIM], ones_tail], axis=0).astype(_BF)
        if not meta:
            blk = qt[h * QK_HEAD_DIM:(h + 1) * QK_HEAD_DIM]
            r1, r2 = _rotate(blk[QK_NOPE_DIM:QK_NOPE_DIM + half], blk[QK_NOPE_DIM + half:], cos_t, sin_t)
            qt_h = jnp.concatenate([blk[:QK_NOPE_DIM], r1, r2, zero_rows], axis=0).astype(_BF)
            q_ref[h] = qt_h
            qt_f = qt_h.astype(_F32)
            qn_ref[h] = jnp.sqrt(jnp.sum(qt_f * qt_f, axis=0, keepdims=True))
            k_f = k_h.astype(_F32)
            kmx_ref[h] = jnp.broadcast_to(jnp.max(jnp.sum(k_f * k_f, axis=1, keepdims=True), axis=0, keepdims=True),
                                          (1, LANES))


def _const_spec(shape):
    nd = len(shape)
    return pl.BlockSpec(shape, lambda *_: (0,) * nd)


def _proj_call(x, rope, gpre, win, wkr_t, wu_t, gq, wq_t, gkv, wk, wv_t, *, tile, meta):
    bsz, seq, _ = x.shape
    nt = seq // tile
    row_spec = lambda w: pl.BlockSpec((None, tile, w), lambda b, i: (b, i, 0))
    rope_spec = pl.BlockSpec((None, None, 2, QK_ROPE_DIM // 2, tile), lambda b, i: (b, i, 0, 0, 0))
    k_spec = pl.BlockSpec((None, N_HEADS, tile, HEAD_PAD), lambda b, i: (b, 0, i, 0))
    t_spec = pl.BlockSpec((None, N_HEADS, None, HEAD_PAD, tile), lambda b, i: (b, 0, i, 0, 0))
    k_shape = jax.ShapeDtypeStruct((bsz, N_HEADS, seq, HEAD_PAD), _BF)
    t_shape = jax.ShapeDtypeStruct((bsz, N_HEADS, nt, HEAD_PAD, tile), _BF)
    w_specs = lambda *ws: [_const_spec(w.shape) for w in ws]
    if meta:
        args = (x, rope, gpre, win, wkr_t, wu_t, gkv, wk, wv_t)
        in_specs = [row_spec(D_MODEL), rope_spec] + w_specs(*args[2:])
        out_specs = [k_spec, t_spec, row_spec(SSM_WIDTH)]
        out_shape = [k_shape, t_shape, jax.ShapeDtypeStruct((bsz, seq, SSM_WIDTH), _F32)]
    else:
        args = (x, rope, gpre, win, wkr_t, wu_t, gq, wq_t, gkv, wk, wv_t)
        in_specs = [row_spec(D_MODEL), rope_spec] + w_specs(*args[2:])
        norm_spec = lambda w: pl.BlockSpec((None, N_HEADS, None, 1, w), lambda b, i: (b, 0, i, 0, 0))
        u_spec = pl.BlockSpec((SSM_GROUPS, tile // SLAB_T, SSM_GROUP, SLAB_T), lambda b, i: (0, i, b, 0))
        out_specs = [t_spec, k_spec, t_spec, u_spec, norm_spec(tile), norm_spec(LANES)]
        out_shape = [t_shape, k_shape, t_shape,
                     jax.ShapeDtypeStruct((SSM_GROUPS, seq // SLAB_T, bsz * SSM_GROUP, SLAB_T), _F32),
                     jax.ShapeDtypeStruct((bsz, N_HEADS, nt, 1, tile), _F32),
                     jax.ShapeDtypeStruct((bsz, N_HEADS, nt, 1, LANES), _F32)]
    return pl.pallas_call(
        functools.partial(_proj_kernel, meta=meta),
        grid=(bsz, nt),
        in_specs=in_specs,
        out_specs=out_specs,
        out_shape=out_shape,
        compiler_params=pltpu.CompilerParams(
            dimension_semantics=("parallel", "parallel"), vmem_limit_bytes=VMEM_LIMIT,
            allow_input_fusion=[False] + [True] * (len(args) - 1)),
        name="proj_meta" if meta else "proj",
    )(*args)


def _attn_finish(accs, o_ref):
    halves = [(acc * (1.0 / acc[ONES_LANE:ONES_LANE + 1, :]))[:V_HEAD_DIM] for acc in accs]
    o_ref[...] = jnp.concatenate(halves, axis=0).T


def _lane_concat(ref, hh):
    return jnp.concatenate([ref[hh, j] for j in range(ref.shape[1])], axis=1)


def _attn_bounded_kernel(qt_ref, mrow_ref, k_ref, vt_ref, km_ref, vtm_ref, o_ref, *, nk, tk):
    tq = o_ref.shape[0]
    per_slab = vt_ref.shape[3] // tk
    key_row = lax.broadcasted_iota(jnp.int32, (LANES, tq), 0)
    accs = []
    for hh in range(2):
        qt = _lane_concat(qt_ref, hh)
        mrow = _lane_concat(mrow_ref, hh)
        s0 = jnp.where(key_row < N_META, _dot(km_ref[hh], qt), -jnp.inf)
        acc = _dot(vtm_ref[hh, :V_ROWS, :], jnp.exp2(s0 - mrow).astype(_BF))
        scores = lambda c: _dot(k_ref[hh, c * tk:(c + 1) * tk, :], qt)
        s_next = scores(0)
        for c in range(nk):
            s = s_next
            if c + 1 < nk:
                s_next = scores(c + 1)
            vt_c = vt_ref[hh, c // per_slab, :V_ROWS, (c % per_slab) * tk:(c % per_slab + 1) * tk]
            acc = acc + _dot(vt_c, jnp.exp2(s - mrow).astype(_BF))
        accs.append(acc)
    _attn_finish(accs, o_ref)


def _attn_online_kernel(qt_ref, k_ref, vt_ref, km_ref, vtm_ref, o_ref, s0_scr, s1_scr, m_scr, acc_scr, *, nk, tk):
    tq = o_ref.shape[0]
    key_row = lax.broadcasted_iota(jnp.int32, (LANES, tq), 0)
    for hh in range(2):
        s0 = jnp.where(key_row < N_META, _dot(km_ref[hh], _lane_concat(qt_ref, hh)), -jnp.inf)
        m0 = jnp.max(s0, axis=0, keepdims=True)
        m_scr[hh] = m0
        acc_scr[hh] = _dot(vtm_ref[hh, :V_ROWS, :], jnp.exp2(s0 - m0).astype(_BF))

    def scores(buf, c):
        off = pl.multiple_of(c * tk, tk)
        for hh in range(2):
            buf[hh] = _dot(k_ref[hh, pl.ds(off, tk), :], _lane_concat(qt_ref, hh))

    def accumulate(buf, c):
        for hh in range(2):
            s = buf[hh]
            m = m_scr[hh]
            m_new = jnp.maximum(m, jnp.max(s, axis=0, keepdims=True))
            m_scr[hh] = m_new
            p = jnp.exp2(s - m_new).astype(_BF)
            acc_scr[hh] = jnp.exp2(m - m_new) * acc_scr[hh] + _dot(vt_ref[hh, c, :V_ROWS, :], p)

    scores(s0_scr, 0)

    def body(t, _):
        scores(s1_scr, 2 * t + 1)
        accumulate(s0_scr, 2 * t)
        scores(s0_scr, 2 * t + 2)
        accumulate(s1_scr, 2 * t + 1)
        return 0

    lax.fori_loop(0, nk // 2 - 1, body, 0)
    scores(s1_scr, nk - 1)
    accumulate(s0_scr, nk - 2)
    accumulate(s1_scr, nk - 1)
    _attn_finish([acc_scr[0], acc_scr[1]], o_ref)


def _attn_call(qt, mrow, k, vt, km, vtm, *, bounded):
    bsz, _, nslab, _, slab = qt.shape
    seq = k.shape[2]
    tq = min(ATTN_TQ, seq)
    tk = min(ATTN_TK, slab)
    qs = tq // slab
    assert tq % slab == 0 and seq % tq == 0 and slab % tk == 0
    q_spec = lambda rows: pl.BlockSpec((None, 2, qs, rows, slab), lambda b, hp, i: (b, hp, i, 0, 0))
    kv_specs = [pl.BlockSpec((None, 2, seq, HEAD_PAD), lambda b, hp, i: (b, hp, 0, 0)),
                pl.BlockSpec((None, 2, nslab, HEAD_PAD, slab), lambda b, hp, i: (b, hp, 0, 0, 0)),
                pl.BlockSpec((2, LANES, HEAD_PAD), lambda b, hp, i: (hp, 0, 0)),
                pl.BlockSpec((2, HEAD_PAD, LANES), lambda b, hp, i: (hp, 0, 0))]
    if bounded:
        body = functools.partial(_attn_bounded_kernel, nk=seq // tk, tk=tk)
        in_specs = [q_spec(HEAD_PAD), q_spec(1)] + kv_specs
        args, scratch = (qt, mrow, k, vt, km, vtm), []
    else:
        assert nslab % 2 == 0 and nslab >= 4
        body = functools.partial(_attn_online_kernel, nk=nslab, tk=slab)
        in_specs = [q_spec(HEAD_PAD)] + kv_specs
        args = (qt, k, vt, km, vtm)
        scratch = [pltpu.VMEM((2, slab, tq), _F32), pltpu.VMEM((2, slab, tq), _F32),
                   pltpu.VMEM((2, 1, tq), _F32), pltpu.VMEM((2, V_ROWS, tq), _F32)]
    return pl.pallas_call(
        body,
        grid=(bsz, N_HEADS // 2, seq // tq),
        in_specs=in_specs,
        out_specs=pl.BlockSpec((None, tq, LANES), lambda b, hp, i: (b, i, hp)),
        out_shape=jax.ShapeDtypeStruct((bsz, seq, ATTN_WIDTH), _F32),
        scratch_shapes=scratch,
        compiler_params=pltpu.CompilerParams(
            dimension_semantics=("parallel", "parallel", "arbitrary"), vmem_limit_bytes=VMEM_LIMIT),
        name="attn_bounded" if bounded else "attn_online",
    )(*args)


def _cmul_add(ar, ai, xr, xi, sr, si):
    return ar * xr - ai * xi + sr, ar * xi + ai * xr + si


def _s5_kernel(a_ref, um_ref, q_ref, lag_ref, w_ref, v_ref, t_ref, y_ref, sup_scr, ent_scr,
               *, nslab, bsz):
    rows = nslab * bsz
    half = STATE_W // 2
    a = [a_ref[pl.ds(i, rows, stride=SSM_GROUP), :] for i in range(SSM_GROUP)]
    uc = [jnp.concatenate([x[:, c * SSM_CHUNK:(c + 1) * SSM_CHUNK] for x in a], axis=1).astype(_BF)
          for c in range(SLAB_CHUNKS)]
    w = w_ref[...]
    s = [_dot(u, w) for u in uc]
    sr = [x[:, :half] for x in s]
    si = [x[:, half:] for x in s]
    t = t_ref[...]
    trow = lambda r: t[r:r + 1, :]
    sup_r = sup_i = None
    for c in range(SLAB_CHUNKS):
        cr, ci = trow(c), trow(SLAB_CHUNKS + c)
        pr = cr * sr[c] - ci * si[c]
        pi = cr * si[c] + ci * sr[c]
        sup_r = pr if sup_r is None else sup_r + pr
        sup_i = pi if sup_i is None else sup_i + pi
    sup_scr[:, :half] = sup_r
    sup_scr[:, half:] = sup_i

    lane = lax.broadcasted_iota(jnp.int32, (bsz, half), 1)
    fwd = lane < SSM_STATE
    sm = _dot(um_ref[...], w)
    xr = jnp.where(fwd, sm[:bsz, :half], 0.0)
    xi = jnp.where(fwd, sm[:bsz, half:], 0.0)
    a_slab_r, a_slab_i = trow(2 * SLAB_CHUNKS), trow(2 * SLAB_CHUNKS + 1)
    for j in range(nslab):
        rf = j * bsz
        rb = (nslab - 1 - j) * bsz
        ent_scr[rf:rf + bsz, 0:SSM_STATE] = xr[:, 0:SSM_STATE]
        ent_scr[rb:rb + bsz, SSM_STATE:half] = xr[:, SSM_STATE:half]
        ent_scr[rf:rf + bsz, half:half + SSM_STATE] = xi[:, 0:SSM_STATE]
        ent_scr[rb:rb + bsz, half + SSM_STATE:STATE_W] = xi[:, SSM_STATE:half]
        s_r = jnp.where(fwd, sup_scr[rf:rf + bsz, :half], sup_scr[rb:rb + bsz, :half])
        s_i = jnp.where(fwd, sup_scr[rf:rf + bsz, half:], sup_scr[rb:rb + bsz, half:])
        xr, xi = _cmul_add(a_slab_r, a_slab_i, xr, xi, s_r, s_i)

    ent = ent_scr[...]
    a_r, a_i = trow(2 * SLAB_CHUNKS + 2), trow(2 * SLAB_CHUNKS + 3)
    xf = [(ent[:, :half], ent[:, half:])]
    for c in range(1, SLAB_CHUNKS):
        xf.append(_cmul_add(a_r, a_i, xf[-1][0], xf[-1][1], sr[c - 1], si[c - 1]))
    xb = [(ent[:, :half], ent[:, half:])]
    for c in range(SLAB_CHUNKS - 2, -1, -1):
        xb.insert(0, _cmul_add(a_r, a_i, xb[0][0], xb[0][1], sr[c + 1], si[c + 1]))
    fwd_rows = lax.broadcasted_iota(jnp.int32, (rows, half), 1) < SSM_STATE
    m = jnp.concatenate(
        [pltpu.roll(jnp.broadcast_to(lag_ref[i:i + 1, :], (SSM_CHUNK, 2 * CHUNK_W)), 0, 1,
                    stride=SSM_GROUP, stride_axis=0)[:, :CHUNK_W] for i in range(SSM_GROUP)], axis=0).astype(_BF)
    q = q_ref[...]
    m = _dot(m, q).astype(_BF)
    v = _dot(v_ref[...], q).astype(_BF)
    ys = []
    for c in range(SLAB_CHUNKS):
        xin = jnp.concatenate([jnp.where(fwd_rows, xf[c][0], xb[c][0]),
                               jnp.where(fwd_rows, xf[c][1], xb[c][1])], axis=1).astype(_BF)
        ys.append(_dot(uc[c], m) + _dot(xin, v))
    for o in range(SSM_GROUP):
        y_ref[pl.ds(o, rows, stride=SSM_GROUP), :] = jnp.concatenate(
            [y[:, o * SSM_CHUNK:(o + 1) * SSM_CHUNK] for y in ys], axis=1)


def _s5_call(a, um, perm, lags, w_mat, v_mat, tab, *, nslab, bsz):
    n = nslab * bsz * SSM_GROUP
    rows = nslab * bsz
    g_spec = lambda *shape: pl.BlockSpec((None,) + shape, lambda g: (g,) + (0,) * len(shape))
    perm_spec = pl.BlockSpec((CHUNK_W, CHUNK_W), lambda g: (0, 0), pipeline_mode=pl.Buffered(1))
    return pl.pallas_call(
        functools.partial(_s5_kernel, nslab=nslab, bsz=bsz),
        grid=(SSM_GROUPS,),
        in_specs=[g_spec(n, SLAB_T), g_spec(SUBLANES, CHUNK_W), perm_spec,
                  g_spec(SSM_GROUP, 2 * CHUNK_W), g_spec(CHUNK_W, STATE_W), g_spec(STATE_W, CHUNK_W),
                  g_spec(2 * SUBLANES, STATE_W // 2)],
        out_specs=g_spec(n, SLAB_T),
        out_shape=jax.ShapeDtypeStruct((SSM_GROUPS, n, SLAB_T), _F32),
        scratch_shapes=[pltpu.VMEM((rows, STATE_W), _F32), pltpu.VMEM((rows, STATE_W), _F32)],
        compiler_params=pltpu.CompilerParams(
            dimension_semantics=("parallel",), vmem_limit_bytes=VMEM_LIMIT,
            allow_input_fusion=[False] + [True] * 6),
        name="s5",
    )(a, um, perm, lags, w_mat, v_mat, tab)


def _s5_matrices(a_re, a_im, log_dt, b_re, b_im, c_re, c_im, d_skip):
    tc = SSM_CHUNK
    lam = lax.complex(jnp.minimum(a_re.astype(_F32), -1e-4), a_im.astype(_F32))
    dt = jnp.exp(log_dt.astype(_F32))[..., None]
    lam_dt = lam * dt
    lam_bar = jnp.exp(lam_dt)
    b_bar = ((lam_bar - 1.0) / lam)[..., None] * lax.complex(b_re.astype(_F32), b_im.astype(_F32))
    c_c = lax.complex(c_re.astype(_F32), c_im.astype(_F32))
    k_idx = jnp.arange(tc + 1, dtype=_F32)
    pw = jnp.exp(lam_dt[:, :, None, :] * k_idx[None, None, :, None])
    kern = jnp.real(jnp.einsum('dgop,dgkp,dgpi->dgkoi', c_c, pw[:, :, :tc], b_bar))
    d_g = d_skip.astype(_F32).reshape(SSM_GROUPS, SSM_GROUP)
    center = kern[0][:, :1] + kern[1][:, :1] + (jnp.eye(SSM_GROUP, dtype=_F32)[None] * d_g[:, :, None])[:, None]
    lags = jnp.concatenate([center, kern[0][:, 1:], jnp.zeros_like(center), kern[1][:, :0:-1]], axis=1)
    lags = lags.transpose(0, 3, 1, 2).reshape(SSM_GROUPS, SSM_GROUP, 2 * CHUNK_W)
    wf = b_bar[0].transpose(0, 2, 1)[:, :, None, :] * pw[0][:, tc - 1::-1][:, None, :, :]
    wb = b_bar[1].transpose(0, 2, 1)[:, :, None, :] * pw[1][:, :tc][:, None, :, :]
    w_mat = jnp.concatenate([jnp.real(wf), jnp.real(wb), jnp.imag(wf), jnp.imag(wb)], axis=-1)
    w_mat = w_mat.reshape(SSM_GROUPS, CHUNK_W, STATE_W)
    gf = pw[0][:, 1:tc + 1][:, :, None, :] * c_c[0][:, None, :, :]
    gb = pw[1][:, tc:0:-1][:, :, None, :] * c_c[1][:, None, :, :]
    v_mat = jnp.concatenate([jnp.real(gf), jnp.real(gb), -jnp.imag(gf), -jnp.imag(gb)], axis=-1)
    v_mat = v_mat.reshape(SSM_GROUPS, CHUNK_W, STATE_W).transpose(0, 2, 1)
    n_idx = jnp.arange(SLAB_CHUNKS + 1, dtype=_F32) * tc
    pc = jnp.exp(lam_dt[:, :, None, :] * n_idx[None, None, :, None])
    coef = jnp.concatenate([pc[0][:, SLAB_CHUNKS - 1::-1], pc[1][:, :SLAB_CHUNKS]], axis=-1)
    both = lambda n: jnp.concatenate([pc[0][:, n], pc[1][:, n]], axis=-1)[:, None, :]
    a_slab, a_chunk = both(SLAB_CHUNKS), both(1)
    tab = jnp.concatenate([jnp.real(coef), jnp.imag(coef), jnp.real(a_slab), jnp.imag(a_slab),
                           jnp.real(a_chunk), jnp.imag(a_chunk)], axis=1)
    tab = jnp.pad(tab, ((0, 0), (0, 2 * SUBLANES - tab.shape[1]), (0, 0)))
    return lags, w_mat, v_mat, tab


def _chunk_permutation():
    r = lax.broadcasted_iota(jnp.int32, (CHUNK_W, CHUNK_W), 0)
    c = lax.broadcasted_iota(jnp.int32, (CHUNK_W, CHUNK_W), 1)
    return (r == (c % SSM_CHUNK) * SSM_GROUP + c // SSM_CHUNK).astype(_BF)


def _post_kernel(x_ref, attn_ref, y_ref, wglu_ref, gmix_ref, wout_ref, gpm_ref, gpre_ref,
                 wup_ref, wdn_ref, gpost_ref, o_ref):
    nslab = y_ref.shape[1]
    gmix = gmix_ref[...]
    per = min(POST_SLABS, nslab)
    blocks = range(nslab // per)
    rows = [pl.ds(r * per * SLAB_T, per * SLAB_T) for r in blocks]
    gy = []
    for r in blocks:
        yt = jnp.concatenate([jnp.concatenate([y_ref[g, r * per + c] for c in range(per)], axis=1)
                              for g in range(SSM_GROUPS)], axis=0)
        y = yt.T
        gy.append((0.5 * y * (1.0 + jnp.tanh(math.sqrt(2.0 / math.pi) * (y + 0.044715 * (y * y * y))))).astype(_BF))
    z = [_dot(gy[r], wglu_ref[...]) for r in blocks]
    mix = []
    for r in blocks:
        ssm = z[r][:, :SSM_WIDTH] * (1.0 / (1.0 + jnp.exp(-z[r][:, SSM_WIDTH:])))
        mix.append(jnp.concatenate([_rms(attn_ref[rows[r], :], gmix[:, :ATTN_WIDTH]),
                                    _rms(ssm, gmix[:, ATTN_WIDTH:])], axis=-1).astype(_BF))
    mixed = [_dot(mix[r], wout_ref[...]) for r in blocks]
    h1 = [x_ref[rows[r], :] + _rms(mixed[r], gpm_ref[...]) for r in blocks]
    hn = [_rms(h1[r], gpre_ref[...]).astype(_BF) for r in blocks]
    acc = [None for _ in blocks]
    for c in range(D_FF // FF_TILE):
        for r in blocks:
            up = jnp.maximum(_dot(hn[r], wup_ref[:, c * FF_TILE:(c + 1) * FF_TILE]), 0.0)
            part = _dot((up * up).astype(_BF), wdn_ref[c * FF_TILE:(c + 1) * FF_TILE, :])
            acc[r] = part if acc[r] is None else acc[r] + part
    for r in blocks:
        o_ref[rows[r], :] = h1[r] + _rms(acc[r], gpost_ref[...])


def _post_call(x, attn, y, wglu, gmix, wout, gpm, gpre, wup, wdn, gpost, *, tile):
    bsz, seq, _ = x.shape
    row_spec = lambda w: pl.BlockSpec((None, tile, w), lambda b, i: (b, i, 0))
    wspec = lambda shape: pl.BlockSpec(shape, lambda b, i: (0, 0), pipeline_mode=pl.Buffered(1))
    y_spec = pl.BlockSpec((SSM_GROUPS, tile // SLAB_T, SSM_GROUP, SLAB_T), lambda b, i: (0, i, b, 0))
    return pl.pallas_call(
        _post_kernel,
        grid=(bsz, seq // tile),
        in_specs=[row_spec(D_MODEL), row_spec(ATTN_WIDTH), y_spec,
                  wspec((SSM_WIDTH, 2 * SSM_WIDTH)), wspec((1, D_MODEL)), wspec((D_MODEL, D_MODEL)),
                  wspec((1, D_MODEL)), wspec((1, D_MODEL)), wspec((D_MODEL, D_FF)),
                  wspec((D_FF, D_MODEL)), wspec((1, D_MODEL))],
        out_specs=row_spec(D_MODEL),
        out_shape=jax.ShapeDtypeStruct((bsz, seq, D_MODEL), _F32),
        compiler_params=pltpu.CompilerParams(
            dimension_semantics=("parallel", "parallel"), vmem_limit_bytes=VMEM_LIMIT),
        name="post",
    )(x, attn, y, wglu, gmix, wout, gpm, gpre, wup, wdn, gpost)


def _rope_tables(pos, tile):
    half = QK_ROPE_DIM // 2
    inv = 1.0 / (ROPE_BASE ** (jnp.arange(0, QK_ROPE_DIM, 2, dtype=_F32) / QK_ROPE_DIM))
    ang = pos.astype(_F32)[:, None, :] * inv[None, :, None]
    bsz, seq = pos.shape
    rope = jnp.stack([jnp.cos(ang), jnp.sin(ang)], axis=1)
    return rope.reshape(bsz, 2, half, seq // tile, tile).transpose(0, 3, 1, 2, 4)


def _prep_weights(w_in, w_uq, w_ukv):
    scale = QK_HEAD_DIM ** -0.5 * math.log2(math.e)
    win = w_in[:, :OFF_KR].astype(_BF)
    wkr_t = w_in[:, OFF_KR:OFF_U].T.astype(_BF)
    wu_t = w_in[:, OFF_U:].T.astype(_BF)
    wq_t = (w_uq * scale).T.astype(_BF)
    wkv3 = w_ukv.reshape(KV_LORA_RANK, N_HEADS, QK_NOPE_DIM + V_HEAD_DIM)
    wk = jnp.concatenate([wkv3[..., :QK_NOPE_DIM],
                          jnp.zeros((KV_LORA_RANK, N_HEADS, HEAD_PAD - QK_NOPE_DIM), _F32)], axis=-1)
    wk = wk.reshape(KV_LORA_RANK, N_HEADS * HEAD_PAD).astype(_BF)
    wv_t = wkv3[..., QK_NOPE_DIM:].reshape(KV_LORA_RANK, N_HEADS * V_HEAD_DIM).T.astype(_BF)
    return win, wkr_t, wu_t, wq_t, wk, wv_t


def kernel(x, positions, meta_tokens, g_pre_mix, w_in, g_q_lat, w_uq, g_kv_lat, w_ukv,
           ssm_A_re, ssm_A_im, ssm_log_dt, ssm_B_re, ssm_B_im, ssm_C_re, ssm_C_im, ssm_D,
           w_glu, g_mix_out, w_out, g_post_mix, g_pre_mlp, w_mlp_up, w_mlp_down, g_post_mlp):
    bsz, seq, _ = x.shape
    assert seq % ROW_TILE == 0 and ROW_TILE % SLAB_T == 0 and bsz <= SUBLANES
    assert seq % PROJ_TILE == 0 and PROJ_TILE % SLAB_T == 0
    assert N_META <= SSM_CHUNK
    row = lambda g: g.reshape(1, -1).astype(_F32)

    win, wkr_t, wu_t, wq_t, wk, wv_t = _prep_weights(w_in[0], w_uq[0], w_ukv[0])
    weights = (row(g_pre_mix[0]), win, wkr_t, wu_t, row(g_q_lat[0]), wq_t, row(g_kv_lat[0]), wk, wv_t)
    rope = _rope_tables(positions.astype(jnp.int32) + N_META, PROJ_TILE)
    qt, k, vt, u, q_norm, k_sq = _proj_call(x, rope, *weights, tile=PROJ_TILE, meta=False)
    meta_x = jnp.pad(meta_tokens.astype(x.dtype), ((0, LANES - N_META), (0, 0)))[None]
    rope_m = _rope_tables(jnp.arange(LANES, dtype=jnp.int32)[None], LANES)
    k_m, vt_m, u_m = _proj_call(meta_x, rope_m, *weights, tile=LANES, meta=True)
    km, vtm, u_m = k_m[0], vt_m[0, :, 0], u_m[:, :N_META]
    k_m_sq = jnp.max(jnp.sum(jnp.square(k_m[0].astype(_F32)), axis=-1), axis=-1)
    k_max = jnp.sqrt(jnp.maximum(jnp.max(k_sq, axis=(2, 3, 4)), k_m_sq[None]))
    mrow = q_norm * (k_max * BOUND_SLACK)[:, :, None, None, None]
    attn = lax.cond(jnp.max(mrow) <= BOUND_LIMIT,
                    lambda: _attn_call(qt, mrow, k, vt, km, vtm, bounded=True),
                    lambda: _attn_call(qt, mrow, k, vt, km, vtm, bounded=False))

    um = u_m[0].astype(_BF).reshape(N_META, SSM_GROUPS, SSM_GROUP).transpose(1, 2, 0)
    um = jnp.pad(um, ((0, 0), (0, 0), (SSM_CHUNK - N_META, 0))).reshape(SSM_GROUPS, 1, CHUNK_W)
    um = jnp.broadcast_to(um, (SSM_GROUPS, SUBLANES, CHUNK_W))
    lags, w_mat, v_mat, tab = _s5_matrices(ssm_A_re[0], ssm_A_im[0], ssm_log_dt[0], ssm_B_re[0],
                                            ssm_B_im[0], ssm_C_re[0], ssm_C_im[0], ssm_D[0])
    nslab = seq // SLAB_T
    yg = _s5_call(u.reshape(SSM_GROUPS, nslab * bsz * SSM_GROUP, SLAB_T), um, _chunk_permutation(),
                  lags, w_mat.astype(_BF), v_mat.astype(_BF), tab, nslab=nslab, bsz=bsz)
    y = yg.reshape(SSM_GROUPS, nslab, bsz * SSM_GROUP, SLAB_T)

    return _post_call(x, attn, y, w_glu[0].astype(_BF), row(g_mix_out[0]), w_out[0].astype(_BF),
                      row(g_post_mix[0]), row(g_pre_mlp[0]), w_mlp_up[0].astype(_BF),
                      w_mlp_down[0].astype(_BF), row(g_post_mlp[0]), tile=ROW_TILE)
```

```python
import functools
import math

import jax
import jax.numpy as jnp
from jax import lax
from jax.experimental import pallas as pl
from jax.experimental.pallas import tpu as pltpu

D_MODEL = 1024
N_META = 16
ATTN_WIDTH = 512
SSM_WIDTH = 512
N_HEADS = 8
V_HEAD_DIM = 64
QK_NOPE_DIM = 64
QK_ROPE_DIM = 32
QK_HEAD_DIM = QK_NOPE_DIM + QK_ROPE_DIM
Q_LORA_RANK = 384
KV_LORA_RANK = 256
ROPE_BASE = 10000.0
SSM_GROUP = 16
SSM_GROUPS = 32
SSM_STATE = 64
D_FF = 4 * D_MODEL
EPS = 1e-6
OFF_KR = Q_LORA_RANK + KV_LORA_RANK
OFF_U = OFF_KR + QK_ROPE_DIM

LANES = 128
SUBLANES = 8
HEAD_PAD = LANES
ONES_LANE = V_HEAD_DIM
ATTN_TQ = 1024
ATTN_TK = 256
V_ROWS = 112
PW_Q = 0
PW_KV = PW_Q + Q_LORA_RANK
PW_END = PW_KV + KV_LORA_RANK

SSM_CHUNK = 32
CHUNK_W = SSM_CHUNK * SSM_GROUP
STATE_W = 4 * SSM_STATE
SLAB_T = LANES
SLAB_CHUNKS = SLAB_T // SSM_CHUNK

ROW_TILE = 512
PROJ_TILE = 1024
FF_TILE = 1024
POST_SLABS = 2
BOUND_SLACK = 1.0 + 2.0 ** -6
BOUND_LIMIT = 40.0
V7X_VMEM_BYTES = 64 * 1024 * 1024
VMEM_LIMIT = V7X_VMEM_BYTES - 8 * 1024 * 1024

_BF = jnp.bfloat16
_F32 = jnp.float32


def _dot(a, b):
    return jnp.dot(a, b, preferred_element_type=_F32)


def _rms(x, g):
    return x * lax.rsqrt(jnp.mean(x * x, axis=-1, keepdims=True) + EPS) * g


_NT = (((1,), (1,)), ((), ()))


def _rotate(x1, x2, cos_t, sin_t):
    return x1 * cos_t - x2 * sin_t, x1 * sin_t + x2 * cos_t


def _proj_kernel(x_ref, rope_ref, *refs, meta):
    if meta:
        gpre_ref, win_ref, wkr_ref, wu_ref, gkv_ref, wk_ref, wv_ref, k_ref, v_ref, u_ref = refs
    else:
        (gpre_ref, win_ref, wkr_ref, wu_ref, gq_ref, wq_ref, gkv_ref, wk_ref, wv_ref,
         q_ref, k_ref, v_ref, u_ref, qn_ref, kmx_ref) = refs
    tile = x_ref.shape[0]
    half = QK_ROPE_DIM // 2
    cos_t, sin_t = rope_ref[0], rope_ref[1]
    xn = _rms(x_ref[...], gpre_ref[...]).astype(_BF)
    proj = _dot(xn, win_ref[...])
    kvn = _rms(proj[:, PW_KV:PW_END], gkv_ref[...]).astype(_BF)
    krt = lax.dot_general(wkr_ref[...], xn, _NT, preferred_element_type=_F32)
    r1, r2 = _rotate(krt[:half], krt[half:], cos_t, sin_t)
    kr = jnp.concatenate([jnp.zeros((QK_NOPE_DIM, tile), _F32), r1, r2,
                          jnp.zeros((HEAD_PAD - QK_HEAD_DIM, tile), _F32)], axis=0).T
    kk = _dot(kvn, wk_ref[...])
    vt = lax.dot_general(wv_ref[...], kvn, _NT, preferred_element_type=_F32)
    ones_tail = (lax.broadcasted_iota(jnp.int32, (HEAD_PAD - V_HEAD_DIM, tile), 0) == 0).astype(_F32)
    if meta:
        u_ref[...] = lax.dot_general(xn, wu_ref[...], _NT, preferred_element_type=_F32)
    else:
        ut = lax.dot_general(wu_ref[...], xn, _NT, preferred_element_type=_F32)
        for g in range(SSM_GROUPS):
            for c in range(tile // SLAB_T):
                u_ref[g, c] = ut[g * SSM_GROUP:(g + 1) * SSM_GROUP, c * SLAB_T:(c + 1) * SLAB_T]
        qn = _rms(proj[:, PW_Q:PW_KV], gq_ref[...]).astype(_BF)
        qt = lax.dot_general(wq_ref[...], qn, _NT, preferred_element_type=_F32)
        zero_rows = jnp.zeros((HEAD_PAD - QK_HEAD_DIM, tile), _F32)
    q_norms, k_maxes = [], []
    for h in range(N_HEADS):
        k_h = (kk[:, h * HEAD_PAD:(h + 1) * HEAD_PAD] + kr).astype(_BF)
        k_ref[h] = k_h
        v_ref[h] = jnp.concatenate([vt[h * V_HEAD_DIM:(h + 1) * V_HEAD_DIM], ones_tail], axis=0).astype(_BF)
        if not meta:
            blk = qt[h * QK_HEAD_DIM:(h + 1) * QK_HEAD_DIM]
            r1, r2 = _rotate(blk[QK_NOPE_DIM:QK_NOPE_DIM + half], blk[QK_NOPE_DIM + half:], cos_t, sin_t)
            qt_h = jnp.concatenate([blk[:QK_NOPE_DIM], r1, r2, zero_rows], axis=0).astype(_BF)
            q_ref[h] = qt_h
            qt_f = qt_h.astype(_F32)
            q_norms.append(jnp.sqrt(jnp.sum(qt_f * qt_f, axis=0, keepdims=True)))
            k_f = k_h.astype(_F32)
            k_maxes.append(jnp.broadcast_to(
                jnp.max(jnp.sum(k_f * k_f, axis=1, keepdims=True), axis=0, keepdims=True), (1, LANES)))
    if not meta:
        qn_ref[...] = jnp.concatenate(q_norms, axis=0)
        kmx_ref[...] = jnp.concatenate(k_maxes, axis=0)


def _const_spec(shape):
    nd = len(shape)
    return pl.BlockSpec(shape, lambda *_: (0,) * nd)


def _proj_call(x, rope, gpre, win, wkr_t, wu_t, gq, wq_t, gkv, wk, wv_t, *, tile, meta):
    bsz, seq, _ = x.shape
    nt = seq // tile
    row_spec = lambda w: pl.BlockSpec((None, tile, w), lambda b, i: (b, i, 0))
    rope_spec = pl.BlockSpec((None, None, 2, QK_ROPE_DIM // 2, tile), lambda b, i: (b, i, 0, 0, 0))
    k_spec = pl.BlockSpec((None, N_HEADS, tile, HEAD_PAD), lambda b, i: (b, 0, i, 0))
    t_spec = pl.BlockSpec((None, N_HEADS, None, HEAD_PAD, tile), lambda b, i: (b, 0, i, 0, 0))
    k_shape = jax.ShapeDtypeStruct((bsz, N_HEADS, seq, HEAD_PAD), _BF)
    t_shape = jax.ShapeDtypeStruct((bsz, N_HEADS, nt, HEAD_PAD, tile), _BF)
    w_specs = lambda *ws: [_const_spec(w.shape) for w in ws]
    if meta:
        args = (x, rope, gpre, win, wkr_t, wu_t, gkv, wk, wv_t)
        in_specs = [row_spec(D_MODEL), rope_spec] + w_specs(*args[2:])
        out_specs = [k_spec, t_spec, row_spec(SSM_WIDTH)]
        out_shape = [k_shape, t_shape, jax.ShapeDtypeStruct((bsz, seq, SSM_WIDTH), _F32)]
    else:
        args = (x, rope, gpre, win, wkr_t, wu_t, gq, wq_t, gkv, wk, wv_t)
        in_specs = [row_spec(D_MODEL), rope_spec] + w_specs(*args[2:])
        norm_spec = lambda w: pl.BlockSpec((None, None, N_HEADS, w), lambda b, i: (b, i, 0, 0))
        u_spec = pl.BlockSpec((SSM_GROUPS, tile // SLAB_T, SSM_GROUP, SLAB_T), lambda b, i: (0, i, b, 0))
        out_specs = [t_spec, k_spec, t_spec, u_spec, norm_spec(tile), norm_spec(LANES)]
        out_shape = [t_shape, k_shape, t_shape,
                     jax.ShapeDtypeStruct((SSM_GROUPS, seq // SLAB_T, bsz * SSM_GROUP, SLAB_T), _F32),
                     jax.ShapeDtypeStruct((bsz, nt, N_HEADS, tile), _F32),
                     jax.ShapeDtypeStruct((bsz, nt, N_HEADS, LANES), _F32)]
    return pl.pallas_call(
        functools.partial(_proj_kernel, meta=meta),
        grid=(bsz, nt),
        in_specs=in_specs,
        out_specs=out_specs,
        out_shape=out_shape,
        compiler_params=pltpu.CompilerParams(
            dimension_semantics=("parallel", "parallel"), vmem_limit_bytes=VMEM_LIMIT),
        name="proj_meta" if meta else "proj",
    )(*args)


def _attn_finish(accs, o_ref):
    halves = [(acc * (1.0 / acc[ONES_LANE:ONES_LANE + 1, :]))[:V_HEAD_DIM] for acc in accs]
    o_ref[...] = jnp.concatenate(halves, axis=0).T


def _lane_concat(ref, hh):
    return jnp.concatenate([ref[hh, j] for j in range(ref.shape[1])], axis=1)


def _attn_bounded_kernel(qt_ref, qn_ref, kb_ref, k_ref, vt_ref, km_ref, vtm_ref, o_ref, *, nk, tk):
    tq = o_ref.shape[0]
    per_slab = vt_ref.shape[3] // tk
    key_row = lax.broadcasted_iota(jnp.int32, (LANES, tq), 0)
    accs = []
    for hh in range(2):
        qt = _lane_concat(qt_ref, hh)
        head = pl.ds(2 * pl.program_id(1) + hh, 1)
        mrow = (jnp.concatenate([qn_ref[j, head, :] for j in range(qn_ref.shape[0])], axis=1)
                * kb_ref[head, 0:1])
        s0 = jnp.where(key_row < N_META, _dot(km_ref[hh], qt), -jnp.inf)
        acc = _dot(vtm_ref[hh, :V_ROWS, :], jnp.exp2(s0 - mrow).astype(_BF))
        scores = lambda c: _dot(k_ref[hh, c * tk:(c + 1) * tk, :], qt)
        s_next = scores(0)
        for c in range(nk):
            s = s_next
            if c + 1 < nk:
                s_next = scores(c + 1)
            vt_c = vt_ref[hh, c // per_slab, :V_ROWS, (c % per_slab) * tk:(c % per_slab + 1) * tk]
            acc = acc + _dot(vt_c, jnp.exp2(s - mrow).astype(_BF))
        accs.append(acc)
    _attn_finish(accs, o_ref)


def _attn_online_kernel(qt_ref, k_ref, vt_ref, km_ref, vtm_ref, o_ref, s0_scr, s1_scr, m_scr, acc_scr, *, nk, tk):
    tq = o_ref.shape[0]
    key_row = lax.broadcasted_iota(jnp.int32, (LANES, tq), 0)
    for hh in range(2):
        s0 = jnp.where(key_row < N_META, _dot(km_ref[hh], _lane_concat(qt_ref, hh)), -jnp.inf)
        m0 = jnp.max(s0, axis=0, keepdims=True)
        m_scr[hh] = m0
        acc_scr[hh] = _dot(vtm_ref[hh, :V_ROWS, :], jnp.exp2(s0 - m0).astype(_BF))

    def scores(buf, c):
        off = pl.multiple_of(c * tk, tk)
        for hh in range(2):
            buf[hh] = _dot(k_ref[hh, pl.ds(off, tk), :], _lane_concat(qt_ref, hh))

    def accumulate(buf, c):
        for hh in range(2):
            s = buf[hh]
            m = m_scr[hh]
            m_new = jnp.maximum(m, jnp.max(s, axis=0, keepdims=True))
            m_scr[hh] = m_new
            p = jnp.exp2(s - m_new).astype(_BF)
            acc_scr[hh] = jnp.exp2(m - m_new) * acc_scr[hh] + _dot(vt_ref[hh, c, :V_ROWS, :], p)

    scores(s0_scr, 0)

    def body(t, _):
        scores(s1_scr, 2 * t + 1)
        accumulate(s0_scr, 2 * t)
        scores(s0_scr, 2 * t + 2)
        accumulate(s1_scr, 2 * t + 1)
        return 0

    lax.fori_loop(0, nk // 2 - 1, body, 0)
    scores(s1_scr, nk - 1)
    accumulate(s0_scr, nk - 2)
    accumulate(s1_scr, nk - 1)
    _attn_finish([acc_scr[0], acc_scr[1]], o_ref)


def _attn_call(qt, q_norm, k_bound, k, vt, km, vtm, *, bounded):
    bsz, _, nslab, _, slab = qt.shape
    seq = k.shape[2]
    tq = min(ATTN_TQ, seq)
    tk = min(ATTN_TK, slab)
    qs = tq // slab
    assert tq % slab == 0 and seq % tq == 0 and slab % tk == 0
    q_spec = lambda rows: pl.BlockSpec((None, 2, qs, rows, slab), lambda b, hp, i: (b, hp, i, 0, 0))
    kv_specs = [pl.BlockSpec((None, 2, seq, HEAD_PAD), lambda b, hp, i: (b, hp, 0, 0)),
                pl.BlockSpec((None, 2, nslab, HEAD_PAD, slab), lambda b, hp, i: (b, hp, 0, 0, 0)),
                pl.BlockSpec((2, LANES, HEAD_PAD), lambda b, hp, i: (hp, 0, 0)),
                pl.BlockSpec((2, HEAD_PAD, LANES), lambda b, hp, i: (hp, 0, 0))]
    if bounded:
        body = functools.partial(_attn_bounded_kernel, nk=seq // tk, tk=tk)
        in_specs = [q_spec(HEAD_PAD),
                    pl.BlockSpec((None, qs, N_HEADS, slab), lambda b, hp, i: (b, i, 0, 0)),
                    pl.BlockSpec((None, N_HEADS, LANES), lambda b, hp, i: (b, 0, 0))] + kv_specs
        args, scratch = (qt, q_norm, k_bound, k, vt, km, vtm), []
    else:
        assert nslab % 2 == 0 and nslab >= 4
        body = functools.partial(_attn_online_kernel, nk=nslab, tk=slab)
        in_specs = [q_spec(HEAD_PAD)] + kv_specs
        args = (qt, k, vt, km, vtm)
        scratch = [pltpu.VMEM((2, slab, tq), _F32), pltpu.VMEM((2, slab, tq), _F32),
                   pltpu.VMEM((2, 1, tq), _F32), pltpu.VMEM((2, V_ROWS, tq), _F32)]
    return pl.pallas_call(
        body,
        grid=(bsz, N_HEADS // 2, seq // tq),
        in_specs=in_specs,
        out_specs=pl.BlockSpec((None, tq, LANES), lambda b, hp, i: (b, i, hp)),
        out_shape=jax.ShapeDtypeStruct((bsz, seq, ATTN_WIDTH), _F32),
        scratch_shapes=scratch,
        compiler_params=pltpu.CompilerParams(
            dimension_semantics=("parallel", "parallel", "arbitrary"), vmem_limit_bytes=VMEM_LIMIT),
        name="attn_bounded" if bounded else "attn_online",
    )(*args)


def _cmul_add(ar, ai, xr, xi, sr, si):
    return ar * xr - ai * xi + sr, ar * xi + ai * xr + si


def _s5_kernel(a_ref, um_ref, q_ref, lag_ref, w_ref, v_ref, t_ref, y_ref, sup_scr, ent_scr,
               *, nslab, bsz):
    rows = nslab * bsz
    half = STATE_W // 2
    a = [a_ref[pl.ds(i, rows, stride=SSM_GROUP), :] for i in range(SSM_GROUP)]
    uc = [jnp.concatenate([x[:, c * SSM_CHUNK:(c + 1) * SSM_CHUNK] for x in a], axis=1).astype(_BF)
          for c in range(SLAB_CHUNKS)]
    w = w_ref[...]
    s = [_dot(u, w) for u in uc]
    sr = [x[:, :half] for x in s]
    si = [x[:, half:] for x in s]
    t = t_ref[...]
    trow = lambda r: t[r:r + 1, :]
    sup_r = sup_i = None
    for c in range(SLAB_CHUNKS):
        cr, ci = trow(c), trow(SLAB_CHUNKS + c)
        pr = cr * sr[c] - ci * si[c]
        pi = cr * si[c] + ci * sr[c]
        sup_r = pr if sup_r is None else sup_r + pr
        sup_i = pi if sup_i is None else sup_i + pi
    sup_scr[:, :half] = sup_r
    sup_scr[:, half:] = sup_i

    lane = lax.broadcasted_iota(jnp.int32, (bsz, half), 1)
    fwd = lane < SSM_STATE
    sm = _dot(um_ref[...], w)
    xr = jnp.where(fwd, sm[:bsz, :half], 0.0)
    xi = jnp.where(fwd, sm[:bsz, half:], 0.0)
    a_slab_r, a_slab_i = trow(2 * SLAB_CHUNKS), trow(2 * SLAB_CHUNKS + 1)
    for j in range(nslab):
        rf = j * bsz
        rb = (nslab - 1 - j) * bsz
        ent_scr[rf:rf + bsz, 0:SSM_STATE] = xr[:, 0:SSM_STATE]
        ent_scr[rb:rb + bsz, SSM_STATE:half] = xr[:, SSM_STATE:half]
        ent_scr[rf:rf + bsz, half:half + SSM_STATE] = xi[:, 0:SSM_STATE]
        ent_scr[rb:rb + bsz, half + SSM_STATE:STATE_W] = xi[:, SSM_STATE:half]
        s_r = jnp.where(fwd, sup_scr[rf:rf + bsz, :half], sup_scr[rb:rb + bsz, :half])
        s_i = jnp.where(fwd, sup_scr[rf:rf + bsz, half:], sup_scr[rb:rb + bsz, half:])
        xr, xi = _cmul_add(a_slab_r, a_slab_i, xr, xi, s_r, s_i)

    ent = ent_scr[...]
    a_r, a_i = trow(2 * SLAB_CHUNKS + 2), trow(2 * SLAB_CHUNKS + 3)
    xf = [(ent[:, :half], ent[:, half:])]
    for c in range(1, SLAB_CHUNKS):
        xf.append(_cmul_add(a_r, a_i, xf[-1][0], xf[-1][1], sr[c - 1], si[c - 1]))
    xb = [(ent[:, :half], ent[:, half:])]
    for c in range(SLAB_CHUNKS - 2, -1, -1):
        xb.insert(0, _cmul_add(a_r, a_i, xb[0][0], xb[0][1], sr[c + 1], si[c + 1]))
    fwd_rows = lax.broadcasted_iota(jnp.int32, (rows, half), 1) < SSM_STATE
    m = jnp.concatenate(
        [pltpu.roll(jnp.broadcast_to(lag_ref[i:i + 1, :], (SSM_CHUNK, 2 * CHUNK_W)), 0, 1,
                    stride=SSM_GROUP, stride_axis=0)[:, :CHUNK_W] for i in range(SSM_GROUP)], axis=0).astype(_BF)
    q = q_ref[...]
    m = _dot(m, q).astype(_BF)
    v = _dot(v_ref[...], q).astype(_BF)
    ys = []
    for c in range(SLAB_CHUNKS):
        xin = jnp.concatenate([jnp.where(fwd_rows, xf[c][0], xb[c][0]),
                               jnp.where(fwd_rows, xf[c][1], xb[c][1])], axis=1).astype(_BF)
        ys.append(_dot(uc[c], m) + _dot(xin, v))
    for o in range(SSM_GROUP):
        y_ref[pl.ds(o, rows, stride=SSM_GROUP), :] = jnp.concatenate(
            [y[:, o * SSM_CHUNK:(o + 1) * SSM_CHUNK] for y in ys], axis=1)


def _s5_call(a, um, perm, lags, w_mat, v_mat, tab, *, nslab, bsz):
    n = nslab * bsz * SSM_GROUP
    rows = nslab * bsz
    g_spec = lambda *shape: pl.BlockSpec((None,) + shape, lambda g: (g,) + (0,) * len(shape))
    perm_spec = pl.BlockSpec((CHUNK_W, CHUNK_W), lambda g: (0, 0), pipeline_mode=pl.Buffered(1))
    return pl.pallas_call(
        functools.partial(_s5_kernel, nslab=nslab, bsz=bsz),
        grid=(SSM_GROUPS,),
        in_specs=[g_spec(n, SLAB_T), g_spec(SUBLANES, CHUNK_W), perm_spec,
                  g_spec(SSM_GROUP, 2 * CHUNK_W), g_spec(CHUNK_W, STATE_W), g_spec(STATE_W, CHUNK_W),
                  g_spec(2 * SUBLANES, STATE_W // 2)],
        out_specs=g_spec(n, SLAB_T),
        out_shape=jax.ShapeDtypeStruct((SSM_GROUPS, n, SLAB_T), _F32),
        scratch_shapes=[pltpu.VMEM((rows, STATE_W), _F32), pltpu.VMEM((rows, STATE_W), _F32)],
        compiler_params=pltpu.CompilerParams(
            dimension_semantics=("parallel",), vmem_limit_bytes=VMEM_LIMIT),
        name="s5",
    )(a, um, perm, lags, w_mat, v_mat, tab)


def _s5_matrices(a_re, a_im, log_dt, b_re, b_im, c_re, c_im, d_skip):
    tc = SSM_CHUNK
    lam = lax.complex(jnp.minimum(a_re.astype(_F32), -1e-4), a_im.astype(_F32))
    dt = jnp.exp(log_dt.astype(_F32))[..., None]
    lam_dt = lam * dt
    lam_bar = jnp.exp(lam_dt)
    b_bar = ((lam_bar - 1.0) / lam)[..., None] * lax.complex(b_re.astype(_F32), b_im.astype(_F32))
    c_c = lax.complex(c_re.astype(_F32), c_im.astype(_F32))
    k_idx = jnp.arange(tc + 1, dtype=_F32)
    pw = jnp.exp(lam_dt[:, :, None, :] * k_idx[None, None, :, None])
    kern = jnp.real(jnp.einsum('dgop,dgkp,dgpi->dgkoi', c_c, pw[:, :, :tc], b_bar))
    d_g = d_skip.astype(_F32).reshape(SSM_GROUPS, SSM_GROUP)
    center = kern[0][:, :1] + kern[1][:, :1] + (jnp.eye(SSM_GROUP, dtype=_F32)[None] * d_g[:, :, None])[:, None]
    lags = jnp.concatenate([center, kern[0][:, 1:], jnp.zeros_like(center), kern[1][:, :0:-1]], axis=1)
    lags = lags.transpose(0, 3, 1, 2).reshape(SSM_GROUPS, SSM_GROUP, 2 * CHUNK_W)
    wf = b_bar[0].transpose(0, 2, 1)[:, :, None, :] * pw[0][:, tc - 1::-1][:, None, :, :]
    wb = b_bar[1].transpose(0, 2, 1)[:, :, None, :] * pw[1][:, :tc][:, None, :, :]
    w_mat = jnp.concatenate([jnp.real(wf), jnp.real(wb), jnp.imag(wf), jnp.imag(wb)], axis=-1)
    w_mat = w_mat.reshape(SSM_GROUPS, CHUNK_W, STATE_W)
    gf = pw[0][:, 1:tc + 1][:, :, None, :] * c_c[0][:, None, :, :]
    gb = pw[1][:, tc:0:-1][:, :, None, :] * c_c[1][:, None, :, :]
    v_mat = jnp.concatenate([jnp.real(gf), jnp.real(gb), -jnp.imag(gf), -jnp.imag(gb)], axis=-1)
    v_mat = v_mat.reshape(SSM_GROUPS, CHUNK_W, STATE_W).transpose(0, 2, 1)
    n_idx = jnp.arange(SLAB_CHUNKS + 1, dtype=_F32) * tc
    pc = jnp.exp(lam_dt[:, :, None, :] * n_idx[None, None, :, None])
    coef = jnp.concatenate([pc[0][:, SLAB_CHUNKS - 1::-1], pc[1][:, :SLAB_CHUNKS]], axis=-1)
    both = lambda n: jnp.concatenate([pc[0][:, n], pc[1][:, n]], axis=-1)[:, None, :]
    a_slab, a_chunk = both(SLAB_CHUNKS), both(1)
    tab = jnp.concatenate([jnp.real(coef), jnp.imag(coef), jnp.real(a_slab), jnp.imag(a_slab),
                           jnp.real(a_chunk), jnp.imag(a_chunk)], axis=1)
    tab = jnp.pad(tab, ((0, 0), (0, 2 * SUBLANES - tab.shape[1]), (0, 0)))
    return lags, w_mat, v_mat, tab


def _chunk_permutation():
    r = lax.broadcasted_iota(jnp.int32, (CHUNK_W, CHUNK_W), 0)
    c = lax.broadcasted_iota(jnp.int32, (CHUNK_W, CHUNK_W), 1)
    return (r == (c % SSM_CHUNK) * SSM_GROUP + c // SSM_CHUNK).astype(_BF)


def _post_kernel(x_ref, attn_ref, y_ref, wglu_ref, gmix_ref, wout_ref, gpm_ref, gpre_ref,
                 wup_ref, wdn_ref, gpost_ref, o_ref):
    nslab = y_ref.shape[1]
    gmix = gmix_ref[...]
    per = min(POST_SLABS, nslab)
    blocks = range(nslab // per)
    rows = [pl.ds(r * per * SLAB_T, per * SLAB_T) for r in blocks]
    gy = []
    for r in blocks:
        yt = jnp.concatenate([jnp.concatenate([y_ref[g, r * per + c] for c in range(per)], axis=1)
                              for g in range(SSM_GROUPS)], axis=0)
        y = yt.T
        gy.append((0.5 * y * (1.0 + jnp.tanh(math.sqrt(2.0 / math.pi) * (y + 0.044715 * (y * y * y))))).astype(_BF))
    z = [_dot(gy[r], wglu_ref[...]) for r in blocks]
    mix = []
    for r in blocks:
        ssm = z[r][:, :SSM_WIDTH] * (1.0 / (1.0 + jnp.exp(-z[r][:, SSM_WIDTH:])))
        mix.append(jnp.concatenate([_rms(attn_ref[rows[r], :], gmix[:, :ATTN_WIDTH]),
                                    _rms(ssm, gmix[:, ATTN_WIDTH:])], axis=-1).astype(_BF))
    mixed = [_dot(mix[r], wout_ref[...]) for r in blocks]
    h1 = [x_ref[rows[r], :] + _rms(mixed[r], gpm_ref[...]) for r in blocks]
    hn = [_rms(h1[r], gpre_ref[...]).astype(_BF) for r in blocks]
    acc = [None for _ in blocks]
    for c in range(D_FF // FF_TILE):
        for r in blocks:
            up = jnp.maximum(_dot(hn[r], wup_ref[:, c * FF_TILE:(c + 1) * FF_TILE]), 0.0)
            part = _dot((up * up).astype(_BF), wdn_ref[c * FF_TILE:(c + 1) * FF_TILE, :])
            acc[r] = part if acc[r] is None else acc[r] + part
    for r in blocks:
        o_ref[rows[r], :] = h1[r] + _rms(acc[r], gpost_ref[...])


def _post_call(x, attn, y, wglu, gmix, wout, gpm, gpre, wup, wdn, gpost, *, tile):
    bsz, seq, _ = x.shape
    row_spec = lambda w: pl.BlockSpec((None, tile, w), lambda b, i: (b, i, 0))
    wspec = lambda shape: pl.BlockSpec(shape, lambda b, i: (0, 0), pipeline_mode=pl.Buffered(1))
    y_spec = pl.BlockSpec((SSM_GROUPS, tile // SLAB_T, SSM_GROUP, SLAB_T), lambda b, i: (0, i, b, 0))
    return pl.pallas_call(
        _post_kernel,
        grid=(bsz, seq // tile),
        in_specs=[row_spec(D_MODEL), row_spec(ATTN_WIDTH), y_spec,
                  wspec((SSM_WIDTH, 2 * SSM_WIDTH)), wspec((1, D_MODEL)), wspec((D_MODEL, D_MODEL)),
                  wspec((1, D_MODEL)), wspec((1, D_MODEL)), wspec((D_MODEL, D_FF)),
                  wspec((D_FF, D_MODEL)), wspec((1, D_MODEL))],
        out_specs=row_spec(D_MODEL),
        out_shape=jax.ShapeDtypeStruct((bsz, seq, D_MODEL), _F32),
        compiler_params=pltpu.CompilerParams(
            dimension_semantics=("parallel", "parallel"), vmem_limit_bytes=VMEM_LIMIT),
        name="post",
    )(x, attn, y, wglu, gmix, wout, gpm, gpre, wup, wdn, gpost)


def _rope_tables(pos, tile):
    half = QK_ROPE_DIM // 2
    inv = 1.0 / (ROPE_BASE ** (jnp.arange(0, QK_ROPE_DIM, 2, dtype=_F32) / QK_ROPE_DIM))
    ang = pos.astype(_F32)[:, None, :] * inv[None, :, None]
    bsz, seq = pos.shape
    rope = jnp.stack([jnp.cos(ang), jnp.sin(ang)], axis=1)
    return rope.reshape(bsz, 2, half, seq // tile, tile).transpose(0, 3, 1, 2, 4)


def _prep_weights(w_in, w_uq, w_ukv):
    scale = QK_HEAD_DIM ** -0.5 * math.log2(math.e)
    win = w_in[:, :OFF_KR].astype(_BF)
    wkr_t = w_in[:, OFF_KR:OFF_U].T.astype(_BF)
    wu_t = w_in[:, OFF_U:].T.astype(_BF)
    wq_t = (w_uq * scale).T.astype(_BF)
    wkv3 = w_ukv.reshape(KV_LORA_RANK, N_HEADS, QK_NOPE_DIM + V_HEAD_DIM)
    wk = jnp.concatenate([wkv3[..., :QK_NOPE_DIM],
                          jnp.zeros((KV_LORA_RANK, N_HEADS, HEAD_PAD - QK_NOPE_DIM), _F32)], axis=-1)
    wk = wk.reshape(KV_LORA_RANK, N_HEADS * HEAD_PAD).astype(_BF)
    wv_t = wkv3[..., QK_NOPE_DIM:].reshape(KV_LORA_RANK, N_HEADS * V_HEAD_DIM).T.astype(_BF)
    return win, wkr_t, wu_t, wq_t, wk, wv_t


def kernel(x, positions, meta_tokens, g_pre_mix, w_in, g_q_lat, w_uq, g_kv_lat, w_ukv,
           ssm_A_re, ssm_A_im, ssm_log_dt, ssm_B_re, ssm_B_im, ssm_C_re, ssm_C_im, ssm_D,
           w_glu, g_mix_out, w_out, g_post_mix, g_pre_mlp, w_mlp_up, w_mlp_down, g_post_mlp):
    bsz, seq, _ = x.shape
    assert seq % ROW_TILE == 0 and ROW_TILE % SLAB_T == 0 and bsz <= SUBLANES
    assert seq % PROJ_TILE == 0 and PROJ_TILE % SLAB_T == 0
    assert N_META <= SSM_CHUNK
    row = lambda g: g.reshape(1, -1).astype(_F32)

    win, wkr_t, wu_t, wq_t, wk, wv_t = _prep_weights(w_in[0], w_uq[0], w_ukv[0])
    weights = (row(g_pre_mix[0]), win, wkr_t, wu_t, row(g_q_lat[0]), wq_t, row(g_kv_lat[0]), wk, wv_t)
    rope = _rope_tables(positions.astype(jnp.int32) + N_META, PROJ_TILE)
    qt, k, vt, u, q_norm, k_sq = _proj_call(x, rope, *weights, tile=PROJ_TILE, meta=False)
    meta_x = jnp.pad(meta_tokens.astype(x.dtype), ((0, LANES - N_META), (0, 0)))[None]
    rope_m = _rope_tables(jnp.arange(LANES, dtype=jnp.int32)[None], LANES)
    k_m, vt_m, u_m = _proj_call(meta_x, rope_m, *weights, tile=LANES, meta=True)
    km, vtm, u_m = k_m[0], vt_m[0, :, 0], u_m[:, :N_META]
    k_m_sq = jnp.max(jnp.sum(jnp.square(k_m[0].astype(_F32)), axis=-1), axis=-1)
    k_bound = jnp.sqrt(jnp.maximum(jnp.max(k_sq, axis=(1, 3)), k_m_sq[None])) * BOUND_SLACK
    safe = jnp.max(jnp.max(q_norm, axis=(1, 3)) * k_bound) <= BOUND_LIMIT
    k_bound = jnp.broadcast_to(k_bound[:, :, None], k_bound.shape + (LANES,))
    attn = lax.cond(safe,
                    lambda: _attn_call(qt, q_norm, k_bound, k, vt, km, vtm, bounded=True),
                    lambda: _attn_call(qt, q_norm, k_bound, k, vt, km, vtm, bounded=False))

    um = u_m[0].astype(_BF).reshape(N_META, SSM_GROUPS, SSM_GROUP).transpose(1, 2, 0)
    um = jnp.pad(um, ((0, 0), (0, 0), (SSM_CHUNK - N_META, 0))).reshape(SSM_GROUPS, 1, CHUNK_W)
    um = jnp.broadcast_to(um, (SSM_GROUPS, SUBLANES, CHUNK_W))
    lags, w_mat, v_mat, tab = _s5_matrices(ssm_A_re[0], ssm_A_im[0], ssm_log_dt[0], ssm_B_re[0],
                                            ssm_B_im[0], ssm_C_re[0], ssm_C_im[0], ssm_D[0])
    nslab = seq // SLAB_T
    yg = _s5_call(u.reshape(SSM_GROUPS, nslab * bsz * SSM_GROUP, SLAB_T), um, _chunk_permutation(),
                  lags, w_mat.astype(_BF), v_mat.astype(_BF), tab, nslab=nslab, bsz=bsz)
    y = yg.reshape(SSM_GROUPS, nslab, bsz * SSM_GROUP, SLAB_T)

    return _post_call(x, attn, y, w_glu[0].astype(_BF), row(g_mix_out[0]), w_out[0].astype(_BF),
                      row(g_post_mix[0]), row(g_pre_mlp[0]), w_mlp_up[0].astype(_BF),
                      w_mlp_down[0].astype(_BF), row(g_post_mlp[0]), tile=ROW_TILE)
```

```python
import functools
import math

import jax
import jax.numpy as jnp
from jax import lax
from jax.experimental import pallas as pl
from jax.experimental.pallas import tpu as pltpu

D_MODEL = 1024
N_META = 16
ATTN_WIDTH = 512
SSM_WIDTH = 512
N_HEADS = 8
V_HEAD_DIM = 64
QK_NOPE_DIM = 64
QK_ROPE_DIM = 32
QK_HEAD_DIM = QK_NOPE_DIM + QK_ROPE_DIM
Q_LORA_RANK = 384
KV_LORA_RANK = 256
ROPE_BASE = 10000.0
SSM_GROUP = 16
SSM_GROUPS = 32
SSM_STATE = 64
D_FF = 4 * D_MODEL
EPS = 1e-6
OFF_KR = Q_LORA_RANK + KV_LORA_RANK
OFF_U = OFF_KR + QK_ROPE_DIM

LANES = 128
SUBLANES = 8
HEAD_PAD = LANES
ONES_LANE = V_HEAD_DIM
ATTN_HEADS = 4
ATTN_TQ = 1024
ATTN_TK = 256
V_ROWS = 112
PW_Q = 0
PW_KV = PW_Q + Q_LORA_RANK
PW_END = PW_KV + KV_LORA_RANK

SSM_CHUNK = 32
CHUNK_W = SSM_CHUNK * SSM_GROUP
STATE_W = 4 * SSM_STATE
SLAB_T = LANES
SLAB_CHUNKS = SLAB_T // SSM_CHUNK

ROW_TILE = 512
PROJ_TILE = 1024
FF_TILE = 1024
POST_SLABS = 2
BOUND_SLACK = 1.0 + 2.0 ** -6
BOUND_LIMIT = 40.0
V7X_VMEM_BYTES = 64 * 1024 * 1024
VMEM_LIMIT = V7X_VMEM_BYTES - 8 * 1024 * 1024

_BF = jnp.bfloat16
_F32 = jnp.float32


def _dot(a, b):
    return jnp.dot(a, b, preferred_element_type=_F32)


def _rms(x, g):
    return x * lax.rsqrt(jnp.mean(x * x, axis=-1, keepdims=True) + EPS) * g


_NT = (((1,), (1,)), ((), ()))


def _rotate(x1, x2, cos_t, sin_t):
    return x1 * cos_t - x2 * sin_t, x1 * sin_t + x2 * cos_t


def _proj_kernel(x_ref, rope_ref, *refs, meta):
    if meta:
        gpre_ref, win_ref, wkr_ref, wu_ref, gkv_ref, wk_ref, wv_ref, k_ref, v_ref, u_ref = refs
    else:
        (gpre_ref, win_ref, wkr_ref, wu_ref, gq_ref, wq_ref, gkv_ref, wk_ref, wv_ref,
         q_ref, k_ref, v_ref, u_ref, qn_ref, kmx_ref) = refs
    tile = x_ref.shape[0]
    half = QK_ROPE_DIM // 2
    cos_t, sin_t = rope_ref[0], rope_ref[1]
    xn = _rms(x_ref[...], gpre_ref[...]).astype(_BF)
    proj = _dot(xn, win_ref[...])
    kvn = _rms(proj[:, PW_KV:PW_END], gkv_ref[...]).astype(_BF)
    krt = lax.dot_general(wkr_ref[...], xn, _NT, preferred_element_type=_F32)
    r1, r2 = _rotate(krt[:half], krt[half:], cos_t, sin_t)
    kr = jnp.concatenate([jnp.zeros((QK_NOPE_DIM, tile), _F32), r1, r2,
                          jnp.zeros((HEAD_PAD - QK_HEAD_DIM, tile), _F32)], axis=0).T
    kk = _dot(kvn, wk_ref[...])
    vt = lax.dot_general(wv_ref[...], kvn, _NT, preferred_element_type=_F32)
    ones_tail = (lax.broadcasted_iota(jnp.int32, (HEAD_PAD - V_HEAD_DIM, tile), 0) == 0).astype(_F32)
    if meta:
        u_ref[...] = lax.dot_general(xn, wu_ref[...], _NT, preferred_element_type=_F32)
    else:
        ut = lax.dot_general(wu_ref[...], xn, _NT, preferred_element_type=_F32)
        for g in range(SSM_GROUPS):
            for c in range(tile // SLAB_T):
                u_ref[g, c] = ut[g * SSM_GROUP:(g + 1) * SSM_GROUP, c * SLAB_T:(c + 1) * SLAB_T]
        qn = _rms(proj[:, PW_Q:PW_KV], gq_ref[...]).astype(_BF)
        qt = lax.dot_general(wq_ref[...], qn, _NT, preferred_element_type=_F32)
        zero_rows = jnp.zeros((HEAD_PAD - QK_HEAD_DIM, tile), _F32)
    for h in range(N_HEADS):
        k_h = (kk[:, h * HEAD_PAD:(h + 1) * HEAD_PAD] + kr).astype(_BF)
        k_ref[h] = k_h
        v_ref[h] = jnp.concatenate([vt[h * V_HEAD_DIM:(h + 1) * V_HEAD_DIM], ones_tail], axis=0).astype(_BF)
        if not meta:
            blk = qt[h * QK_HEAD_DIM:(h + 1) * QK_HEAD_DIM]
            r1, r2 = _rotate(blk[QK_NOPE_DIM:QK_NOPE_DIM + half], blk[QK_NOPE_DIM + half:], cos_t, sin_t)
            qt_h = jnp.concatenate([blk[:QK_NOPE_DIM], r1, r2, zero_rows], axis=0).astype(_BF)
            q_ref[h] = qt_h
            qt_f = qt_h.astype(_F32)
            qn_ref[h] = jnp.sqrt(jnp.sum(qt_f * qt_f, axis=0, keepdims=True))
            k_f = k_h.astype(_F32)
            kmx_ref[h] = jnp.broadcast_to(jnp.max(jnp.sum(k_f * k_f, axis=1, keepdims=True), axis=0, keepdims=True),
                                          (1, LANES))


def _const_spec(shape):
    nd = len(shape)
    return pl.BlockSpec(shape, lambda *_: (0,) * nd)


def _proj_call(x, rope, gpre, win, wkr_t, wu_t, gq, wq_t, gkv, wk, wv_t, *, tile, meta):
    bsz, seq, _ = x.shape
    nt = seq // tile
    row_spec = lambda w: pl.BlockSpec((None, tile, w), lambda b, i: (b, i, 0))
    rope_spec = pl.BlockSpec((None, None, 2, QK_ROPE_DIM // 2, tile), lambda b, i: (b, i, 0, 0, 0))
    k_spec = pl.BlockSpec((None, N_HEADS, tile, HEAD_PAD), lambda b, i: (b, 0, i, 0))
    t_spec = pl.BlockSpec((None, N_HEADS, None, HEAD_PAD, tile), lambda b, i: (b, 0, i, 0, 0))
    k_shape = jax.ShapeDtypeStruct((bsz, N_HEADS, seq, HEAD_PAD), _BF)
    t_shape = jax.ShapeDtypeStruct((bsz, N_HEADS, nt, HEAD_PAD, tile), _BF)
    w_specs = lambda *ws: [_const_spec(w.shape) for w in ws]
    if meta:
        args = (x, rope, gpre, win, wkr_t, wu_t, gkv, wk, wv_t)
        in_specs = [row_spec(D_MODEL), rope_spec] + w_specs(*args[2:])
        out_specs = [k_spec, t_spec, row_spec(SSM_WIDTH)]
        out_shape = [k_shape, t_shape, jax.ShapeDtypeStruct((bsz, seq, SSM_WIDTH), _F32)]
    else:
        args = (x, rope, gpre, win, wkr_t, wu_t, gq, wq_t, gkv, wk, wv_t)
        in_specs = [row_spec(D_MODEL), rope_spec] + w_specs(*args[2:])
        norm_spec = lambda w: pl.BlockSpec((None, N_HEADS, None, 1, w), lambda b, i: (b, 0, i, 0, 0))
        u_spec = pl.BlockSpec((SSM_GROUPS, tile // SLAB_T, SSM_GROUP, SLAB_T), lambda b, i: (0, i, b, 0))
        out_specs = [t_spec, k_spec, t_spec, u_spec, norm_spec(tile), norm_spec(LANES)]
        out_shape = [t_shape, k_shape, t_shape,
                     jax.ShapeDtypeStruct((SSM_GROUPS, seq // SLAB_T, bsz * SSM_GROUP, SLAB_T), _F32),
                     jax.ShapeDtypeStruct((bsz, N_HEADS, nt, 1, tile), _F32),
                     jax.ShapeDtypeStruct((bsz, N_HEADS, nt, 1, LANES), _F32)]
    return pl.pallas_call(
        functools.partial(_proj_kernel, meta=meta),
        grid=(bsz, nt),
        in_specs=in_specs,
        out_specs=out_specs,
        out_shape=out_shape,
        compiler_params=pltpu.CompilerParams(
            dimension_semantics=("parallel", "parallel"), vmem_limit_bytes=VMEM_LIMIT),
        name="proj_meta" if meta else "proj",
    )(*args)


def _attn_finish(accs, o_ref):
    halves = [(acc * (1.0 / acc[ONES_LANE:ONES_LANE + 1, :]))[:V_HEAD_DIM] for acc in accs]
    for p in range(len(accs) // 2):
        o_ref[:, p * LANES:(p + 1) * LANES] = jnp.concatenate(halves[2 * p:2 * p + 2], axis=0).T


def _lane_concat(ref, hh):
    return jnp.concatenate([ref[hh, j] for j in range(ref.shape[1])], axis=1)


def _attn_bounded_kernel(qt_ref, mrow_ref, k_ref, vt_ref, km_ref, vtm_ref, o_ref, *, nk, tk):
    tq = o_ref.shape[0]
    per_slab = vt_ref.shape[3] // tk
    key_row = lax.broadcasted_iota(jnp.int32, (LANES, tq), 0)
    accs = []
    for hh in range(qt_ref.shape[0]):
        qt = _lane_concat(qt_ref, hh)
        mrow = _lane_concat(mrow_ref, hh)
        s0 = jnp.where(key_row < N_META, _dot(km_ref[hh], qt), -jnp.inf)
        acc = _dot(vtm_ref[hh, :V_ROWS, :], jnp.exp2(s0 - mrow).astype(_BF))
        scores = lambda c: _dot(k_ref[hh, c * tk:(c + 1) * tk, :], qt)
        s_next = scores(0)
        for c in range(nk):
            s = s_next
            if c + 1 < nk:
                s_next = scores(c + 1)
            vt_c = vt_ref[hh, c // per_slab, :V_ROWS, (c % per_slab) * tk:(c % per_slab + 1) * tk]
            acc = acc + _dot(vt_c, jnp.exp2(s - mrow).astype(_BF))
        accs.append(acc)
    _attn_finish(accs, o_ref)


def _attn_online_kernel(qt_ref, k_ref, vt_ref, km_ref, vtm_ref, o_ref, s0_scr, s1_scr, m_scr, acc_scr, *, nk, tk):
    tq = o_ref.shape[0]
    key_row = lax.broadcasted_iota(jnp.int32, (LANES, tq), 0)
    for hh in range(2):
        s0 = jnp.where(key_row < N_META, _dot(km_ref[hh], _lane_concat(qt_ref, hh)), -jnp.inf)
        m0 = jnp.max(s0, axis=0, keepdims=True)
        m_scr[hh] = m0
        acc_scr[hh] = _dot(vtm_ref[hh, :V_ROWS, :], jnp.exp2(s0 - m0).astype(_BF))

    def scores(buf, c):
        off = pl.multiple_of(c * tk, tk)
        for hh in range(2):
            buf[hh] = _dot(k_ref[hh, pl.ds(off, tk), :], _lane_concat(qt_ref, hh))

    def accumulate(buf, c):
        for hh in range(2):
            s = buf[hh]
            m = m_scr[hh]
            m_new = jnp.maximum(m, jnp.max(s, axis=0, keepdims=True))
            m_scr[hh] = m_new
            p = jnp.exp2(s - m_new).astype(_BF)
            acc_scr[hh] = jnp.exp2(m - m_new) * acc_scr[hh] + _dot(vt_ref[hh, c, :V_ROWS, :], p)

    scores(s0_scr, 0)

    def body(t, _):
        scores(s1_scr, 2 * t + 1)
        accumulate(s0_scr, 2 * t)
        scores(s0_scr, 2 * t + 2)
        accumulate(s1_scr, 2 * t + 1)
        return 0

    lax.fori_loop(0, nk // 2 - 1, body, 0)
    scores(s1_scr, nk - 1)
    accumulate(s0_scr, nk - 2)
    accumulate(s1_scr, nk - 1)
    _attn_finish([acc_scr[0], acc_scr[1]], o_ref)


def _attn_call(qt, mrow, k, vt, km, vtm, *, bounded):
    bsz, _, nslab, _, slab = qt.shape
    seq = k.shape[2]
    tq = min(ATTN_TQ, seq)
    tk = min(ATTN_TK, slab)
    qs = tq // slab
    assert tq % slab == 0 and seq % tq == 0 and slab % tk == 0
    hs = ATTN_HEADS if bounded else 2
    q_spec = lambda rows: pl.BlockSpec((None, hs, qs, rows, slab), lambda b, hp, i: (b, hp, i, 0, 0))
    kv_specs = [pl.BlockSpec((None, hs, seq, HEAD_PAD), lambda b, hp, i: (b, hp, 0, 0)),
                pl.BlockSpec((None, hs, nslab, HEAD_PAD, slab), lambda b, hp, i: (b, hp, 0, 0, 0)),
                pl.BlockSpec((hs, LANES, HEAD_PAD), lambda b, hp, i: (hp, 0, 0)),
                pl.BlockSpec((hs, HEAD_PAD, LANES), lambda b, hp, i: (hp, 0, 0))]
    if bounded:
        body = functools.partial(_attn_bounded_kernel, nk=seq // tk, tk=tk)
        in_specs = [q_spec(HEAD_PAD), q_spec(1)] + kv_specs
        args, scratch = (qt, mrow, k, vt, km, vtm), []
    else:
        assert nslab % 2 == 0 and nslab >= 4
        body = functools.partial(_attn_online_kernel, nk=nslab, tk=slab)
        in_specs = [q_spec(HEAD_PAD)] + kv_specs
        args = (qt, k, vt, km, vtm)
        scratch = [pltpu.VMEM((2, slab, tq), _F32), pltpu.VMEM((2, slab, tq), _F32),
                   pltpu.VMEM((2, 1, tq), _F32), pltpu.VMEM((2, V_ROWS, tq), _F32)]
    return pl.pallas_call(
        body,
        grid=(bsz, N_HEADS // hs, seq // tq),
        in_specs=in_specs,
        out_specs=pl.BlockSpec((None, tq, LANES * hs // 2), lambda b, hp, i: (b, i, hp)),
        out_shape=jax.ShapeDtypeStruct((bsz, seq, ATTN_WIDTH), _F32),
        scratch_shapes=scratch,
        compiler_params=pltpu.CompilerParams(
            dimension_semantics=("parallel", "parallel", "arbitrary"), vmem_limit_bytes=VMEM_LIMIT),
        name="attn_bounded" if bounded else "attn_online",
    )(*args)


def _cmul_add(ar, ai, xr, xi, sr, si):
    return ar * xr - ai * xi + sr, ar * xi + ai * xr + si


def _s5_kernel(a_ref, um_ref, q_ref, lag_ref, w_ref, v_ref, t_ref, y_ref, sup_scr, ent_scr,
               *, nslab, bsz):
    rows = nslab * bsz
    half = STATE_W // 2
    a = [a_ref[pl.ds(i, rows, stride=SSM_GROUP), :] for i in range(SSM_GROUP)]
    uc = [jnp.concatenate([x[:, c * SSM_CHUNK:(c + 1) * SSM_CHUNK] for x in a], axis=1).astype(_BF)
          for c in range(SLAB_CHUNKS)]
    w = w_ref[...]
    s = [_dot(u, w) for u in uc]
    sr = [x[:, :half] for x in s]
    si = [x[:, half:] for x in s]
    t = t_ref[...]
    trow = lambda r: t[r:r + 1, :]
    sup_r = sup_i = None
    for c in range(SLAB_CHUNKS):
        cr, ci = trow(c), trow(SLAB_CHUNKS + c)
        pr = cr * sr[c] - ci * si[c]
        pi = cr * si[c] + ci * sr[c]
        sup_r = pr if sup_r is None else sup_r + pr
        sup_i = pi if sup_i is None else sup_i + pi
    sup_scr[:, :half] = sup_r
    sup_scr[:, half:] = sup_i

    lane = lax.broadcasted_iota(jnp.int32, (bsz, half), 1)
    fwd = lane < SSM_STATE
    sm = _dot(um_ref[...], w)
    xr = jnp.where(fwd, sm[:bsz, :half], 0.0)
    xi = jnp.where(fwd, sm[:bsz, half:], 0.0)
    a_slab_r, a_slab_i = trow(2 * SLAB_CHUNKS), trow(2 * SLAB_CHUNKS + 1)
    for j in range(nslab):
        rf = j * bsz
        rb = (nslab - 1 - j) * bsz
        ent_scr[rf:rf + bsz, 0:SSM_STATE] = xr[:, 0:SSM_STATE]
        ent_scr[rb:rb + bsz, SSM_STATE:half] = xr[:, SSM_STATE:half]
        ent_scr[rf:rf + bsz, half:half + SSM_STATE] = xi[:, 0:SSM_STATE]
        ent_scr[rb:rb + bsz, half + SSM_STATE:STATE_W] = xi[:, SSM_STATE:half]
        s_r = jnp.where(fwd, sup_scr[rf:rf + bsz, :half], sup_scr[rb:rb + bsz, :half])
        s_i = jnp.where(fwd, sup_scr[rf:rf + bsz, half:], sup_scr[rb:rb + bsz, half:])
        xr, xi = _cmul_add(a_slab_r, a_slab_i, xr, xi, s_r, s_i)

    ent = ent_scr[...]
    a_r, a_i = trow(2 * SLAB_CHUNKS + 2), trow(2 * SLAB_CHUNKS + 3)
    xf = [(ent[:, :half], ent[:, half:])]
    for c in range(1, SLAB_CHUNKS):
        xf.append(_cmul_add(a_r, a_i, xf[-1][0], xf[-1][1], sr[c - 1], si[c - 1]))
    xb = [(ent[:, :half], ent[:, half:])]
    for c in range(SLAB_CHUNKS - 2, -1, -1):
        xb.insert(0, _cmul_add(a_r, a_i, xb[0][0], xb[0][1], sr[c + 1], si[c + 1]))
    fwd_rows = lax.broadcasted_iota(jnp.int32, (rows, half), 1) < SSM_STATE
    m = jnp.concatenate(
        [pltpu.roll(jnp.broadcast_to(lag_ref[i:i + 1, :], (SSM_CHUNK, 2 * CHUNK_W)), 0, 1,
                    stride=SSM_GROUP, stride_axis=0)[:, :CHUNK_W] for i in range(SSM_GROUP)], axis=0).astype(_BF)
    q = q_ref[...]
    m = _dot(m, q).astype(_BF)
    v = _dot(v_ref[...], q).astype(_BF)
    ys = []
    for c in range(SLAB_CHUNKS):
        xin = jnp.concatenate([jnp.where(fwd_rows, xf[c][0], xb[c][0]),
                               jnp.where(fwd_rows, xf[c][1], xb[c][1])], axis=1).astype(_BF)
        ys.append(_dot(uc[c], m) + _dot(xin, v))
    for o in range(SSM_GROUP):
        y_ref[pl.ds(o, rows, stride=SSM_GROUP), :] = jnp.concatenate(
            [y[:, o * SSM_CHUNK:(o + 1) * SSM_CHUNK] for y in ys], axis=1)


def _s5_call(a, um, perm, lags, w_mat, v_mat, tab, *, nslab, bsz):
    n = nslab * bsz * SSM_GROUP
    rows = nslab * bsz
    g_spec = lambda *shape: pl.BlockSpec((None,) + shape, lambda g: (g,) + (0,) * len(shape))
    perm_spec = pl.BlockSpec((CHUNK_W, CHUNK_W), lambda g: (0, 0), pipeline_mode=pl.Buffered(1))
    return pl.pallas_call(
        functools.partial(_s5_kernel, nslab=nslab, bsz=bsz),
        grid=(SSM_GROUPS,),
        in_specs=[g_spec(n, SLAB_T), g_spec(SUBLANES, CHUNK_W), perm_spec,
                  g_spec(SSM_GROUP, 2 * CHUNK_W), g_spec(CHUNK_W, STATE_W), g_spec(STATE_W, CHUNK_W),
                  g_spec(2 * SUBLANES, STATE_W // 2)],
        out_specs=g_spec(n, SLAB_T),
        out_shape=jax.ShapeDtypeStruct((SSM_GROUPS, n, SLAB_T), _F32),
        scratch_shapes=[pltpu.VMEM((rows, STATE_W), _F32), pltpu.VMEM((rows, STATE_W), _F32)],
        compiler_params=pltpu.CompilerParams(
            dimension_semantics=("parallel",), vmem_limit_bytes=VMEM_LIMIT),
        name="s5",
    )(a, um, perm, lags, w_mat, v_mat, tab)


def _s5_matrices(a_re, a_im, log_dt, b_re, b_im, c_re, c_im, d_skip):
    tc = SSM_CHUNK
    lam = lax.complex(jnp.minimum(a_re.astype(_F32), -1e-4), a_im.astype(_F32))
    dt = jnp.exp(log_dt.astype(_F32))[..., None]
    lam_dt = lam * dt
    lam_bar = jnp.exp(lam_dt)
    b_bar = ((lam_bar - 1.0) / lam)[..., None] * lax.complex(b_re.astype(_F32), b_im.astype(_F32))
    c_c = lax.complex(c_re.astype(_F32), c_im.astype(_F32))
    k_idx = jnp.arange(tc + 1, dtype=_F32)
    pw = jnp.exp(lam_dt[:, :, None, :] * k_idx[None, None, :, None])
    kern = jnp.real(jnp.einsum('dgop,dgkp,dgpi->dgkoi', c_c, pw[:, :, :tc], b_bar))
    d_g = d_skip.astype(_F32).reshape(SSM_GROUPS, SSM_GROUP)
    center = kern[0][:, :1] + kern[1][:, :1] + (jnp.eye(SSM_GROUP, dtype=_F32)[None] * d_g[:, :, None])[:, None]
    lags = jnp.concatenate([center, kern[0][:, 1:], jnp.zeros_like(center), kern[1][:, :0:-1]], axis=1)
    lags = lags.transpose(0, 3, 1, 2).reshape(SSM_GROUPS, SSM_GROUP, 2 * CHUNK_W)
    wf = b_bar[0].transpose(0, 2, 1)[:, :, None, :] * pw[0][:, tc - 1::-1][:, None, :, :]
    wb = b_bar[1].transpose(0, 2, 1)[:, :, None, :] * pw[1][:, :tc][:, None, :, :]
    w_mat = jnp.concatenate([jnp.real(wf), jnp.real(wb), jnp.imag(wf), jnp.imag(wb)], axis=-1)
    w_mat = w_mat.reshape(SSM_GROUPS, CHUNK_W, STATE_W)
    gf = pw[0][:, 1:tc + 1][:, :, None, :] * c_c[0][:, None, :, :]
    gb = pw[1][:, tc:0:-1][:, :, None, :] * c_c[1][:, None, :, :]
    v_mat = jnp.concatenate([jnp.real(gf), jnp.real(gb), -jnp.imag(gf), -jnp.imag(gb)], axis=-1)
    v_mat = v_mat.reshape(SSM_GROUPS, CHUNK_W, STATE_W).transpose(0, 2, 1)
    n_idx = jnp.arange(SLAB_CHUNKS + 1, dtype=_F32) * tc
    pc = jnp.exp(lam_dt[:, :, None, :] * n_idx[None, None, :, None])
    coef = jnp.concatenate([pc[0][:, SLAB_CHUNKS - 1::-1], pc[1][:, :SLAB_CHUNKS]], axis=-1)
    both = lambda n: jnp.concatenate([pc[0][:, n], pc[1][:, n]], axis=-1)[:, None, :]
    a_slab, a_chunk = both(SLAB_CHUNKS), both(1)
    tab = jnp.concatenate([jnp.real(coef), jnp.imag(coef), jnp.real(a_slab), jnp.imag(a_slab),
                           jnp.real(a_chunk), jnp.imag(a_chunk)], axis=1)
    tab = jnp.pad(tab, ((0, 0), (0, 2 * SUBLANES - tab.shape[1]), (0, 0)))
    return lags, w_mat, v_mat, tab


def _chunk_permutation():
    r = lax.broadcasted_iota(jnp.int32, (CHUNK_W, CHUNK_W), 0)
    c = lax.broadcasted_iota(jnp.int32, (CHUNK_W, CHUNK_W), 1)
    return (r == (c % SSM_CHUNK) * SSM_GROUP + c // SSM_CHUNK).astype(_BF)


def _post_kernel(x_ref, attn_ref, y_ref, wglu_ref, gmix_ref, wout_ref, gpm_ref, gpre_ref,
                 wup_ref, wdn_ref, gpost_ref, o_ref):
    nslab = y_ref.shape[1]
    gmix = gmix_ref[...]
    per = min(POST_SLABS, nslab)
    blocks = range(nslab // per)
    rows = [pl.ds(r * per * SLAB_T, per * SLAB_T) for r in blocks]
    gy = []
    for r in blocks:
        yt = jnp.concatenate([jnp.concatenate([y_ref[g, r * per + c] for c in range(per)], axis=1)
                              for g in range(SSM_GROUPS)], axis=0)
        y = yt.T
        gy.append((0.5 * y * (1.0 + jnp.tanh(math.sqrt(2.0 / math.pi) * (y + 0.044715 * (y * y * y))))).astype(_BF))
    z = [_dot(gy[r], wglu_ref[...]) for r in blocks]
    mix = []
    for r in blocks:
        ssm = z[r][:, :SSM_WIDTH] * (1.0 / (1.0 + jnp.exp(-z[r][:, SSM_WIDTH:])))
        mix.append(jnp.concatenate([_rms(attn_ref[rows[r], :], gmix[:, :ATTN_WIDTH]),
                                    _rms(ssm, gmix[:, ATTN_WIDTH:])], axis=-1).astype(_BF))
    mixed = [_dot(mix[r], wout_ref[...]) for r in blocks]
    h1 = [x_ref[rows[r], :] + _rms(mixed[r], gpm_ref[...]) for r in blocks]
    hn = [_rms(h1[r], gpre_ref[...]).astype(_BF) for r in blocks]
    acc = [None for _ in blocks]
    for c in range(D_FF // FF_TILE):
        for r in blocks:
            up = jnp.maximum(_dot(hn[r], wup_ref[:, c * FF_TILE:(c + 1) * FF_TILE]), 0.0)
            part = _dot((up * up).astype(_BF), wdn_ref[c * FF_TILE:(c + 1) * FF_TILE, :])
            acc[r] = part if acc[r] is None else acc[r] + part
    for r in blocks:
        o_ref[rows[r], :] = h1[r] + _rms(acc[r], gpost_ref[...])


def _post_call(x, attn, y, wglu, gmix, wout, gpm, gpre, wup, wdn, gpost, *, tile):
    bsz, seq, _ = x.shape
    row_spec = lambda w: pl.BlockSpec((None, tile, w), lambda b, i: (b, i, 0))
    wspec = lambda shape: pl.BlockSpec(shape, lambda b, i: (0, 0), pipeline_mode=pl.Buffered(1))
    y_spec = pl.BlockSpec((SSM_GROUPS, tile // SLAB_T, SSM_GROUP, SLAB_T), lambda b, i: (0, i, b, 0))
    return pl.pallas_call(
        _post_kernel,
        grid=(bsz, seq // tile),
        in_specs=[row_spec(D_MODEL), row_spec(ATTN_WIDTH), y_spec,
                  wspec((SSM_WIDTH, 2 * SSM_WIDTH)), wspec((1, D_MODEL)), wspec((D_MODEL, D_MODEL)),
                  wspec((1, D_MODEL)), wspec((1, D_MODEL)), wspec((D_MODEL, D_FF)),
                  wspec((D_FF, D_MODEL)), wspec((1, D_MODEL))],
        out_specs=row_spec(D_MODEL),
        out_shape=jax.ShapeDtypeStruct((bsz, seq, D_MODEL), _F32),
        compiler_params=pltpu.CompilerParams(
            dimension_semantics=("parallel", "parallel"), vmem_limit_bytes=VMEM_LIMIT),
        name="post",
    )(x, attn, y, wglu, gmix, wout, gpm, gpre, wup, wdn, gpost)


def _rope_tables(pos, tile):
    half = QK_ROPE_DIM // 2
    inv = 1.0 / (ROPE_BASE ** (jnp.arange(0, QK_ROPE_DIM, 2, dtype=_F32) / QK_ROPE_DIM))
    ang = pos.astype(_F32)[:, None, :] * inv[None, :, None]
    bsz, seq = pos.shape
    rope = jnp.stack([jnp.cos(ang), jnp.sin(ang)], axis=1)
    return rope.reshape(bsz, 2, half, seq // tile, tile).transpose(0, 3, 1, 2, 4)


def _prep_weights(w_in, w_uq, w_ukv):
    scale = QK_HEAD_DIM ** -0.5 * math.log2(math.e)
    win = w_in[:, :OFF_KR].astype(_BF)
    wkr_t = w_in[:, OFF_KR:OFF_U].T.astype(_BF)
    wu_t = w_in[:, OFF_U:].T.astype(_BF)
    wq_t = (w_uq * scale).T.astype(_BF)
    wkv3 = w_ukv.reshape(KV_LORA_RANK, N_HEADS, QK_NOPE_DIM + V_HEAD_DIM)
    wk = jnp.concatenate([wkv3[..., :QK_NOPE_DIM],
                          jnp.zeros((KV_LORA_RANK, N_HEADS, HEAD_PAD - QK_NOPE_DIM), _F32)], axis=-1)
    wk = wk.reshape(KV_LORA_RANK, N_HEADS * HEAD_PAD).astype(_BF)
    wv_t = wkv3[..., QK_NOPE_DIM:].reshape(KV_LORA_RANK, N_HEADS * V_HEAD_DIM).T.astype(_BF)
    return win, wkr_t, wu_t, wq_t, wk, wv_t


def kernel(x, positions, meta_tokens, g_pre_mix, w_in, g_q_lat, w_uq, g_kv_lat, w_ukv,
           ssm_A_re, ssm_A_im, ssm_log_dt, ssm_B_re, ssm_B_im, ssm_C_re, ssm_C_im, ssm_D,
           w_glu, g_mix_out, w_out, g_post_mix, g_pre_mlp, w_mlp_up, w_mlp_down, g_post_mlp):
    bsz, seq, _ = x.shape
    assert seq % ROW_TILE == 0 and ROW_TILE % SLAB_T == 0 and bsz <= SUBLANES
    assert seq % PROJ_TILE == 0 and PROJ_TILE % SLAB_T == 0
    assert N_META <= SSM_CHUNK
    row = lambda g: g.reshape(1, -1).astype(_F32)

    win, wkr_t, wu_t, wq_t, wk, wv_t = _prep_weights(w_in[0], w_uq[0], w_ukv[0])
    weights = (row(g_pre_mix[0]), win, wkr_t, wu_t, row(g_q_lat[0]), wq_t, row(g_kv_lat[0]), wk, wv_t)
    rope = _rope_tables(positions.astype(jnp.int32) + N_META, PROJ_TILE)
    qt, k, vt, u, q_norm, k_sq = _proj_call(x, rope, *weights, tile=PROJ_TILE, meta=False)
    meta_x = jnp.pad(meta_tokens.astype(x.dtype), ((0, LANES - N_META), (0, 0)))[None]
    rope_m = _rope_tables(jnp.arange(LANES, dtype=jnp.int32)[None], LANES)
    k_m, vt_m, u_m = _proj_call(meta_x, rope_m, *weights, tile=LANES, meta=True)
    km, vtm, u_m = k_m[0], vt_m[0, :, 0], u_m[:, :N_META]
    k_m_sq = jnp.max(jnp.sum(jnp.square(k_m[0].astype(_F32)), axis=-1), axis=-1)
    k_max = jnp.sqrt(jnp.maximum(jnp.max(k_sq, axis=(2, 3, 4)), k_m_sq[None]))
    mrow = q_norm * (k_max * BOUND_SLACK)[:, :, None, None, None]
    attn = lax.cond(jnp.max(mrow) <= BOUND_LIMIT,
                    lambda: _attn_call(qt, mrow, k, vt, km, vtm, bounded=True),
                    lambda: _attn_call(qt, mrow, k, vt, km, vtm, bounded=False))

    um = u_m[0].astype(_BF).reshape(N_META, SSM_GROUPS, SSM_GROUP).transpose(1, 2, 0)
    um = jnp.pad(um, ((0, 0), (0, 0), (SSM_CHUNK - N_META, 0))).reshape(SSM_GROUPS, 1, CHUNK_W)
    um = jnp.broadcast_to(um, (SSM_GROUPS, SUBLANES, CHUNK_W))
    lags, w_mat, v_mat, tab = _s5_matrices(ssm_A_re[0], ssm_A_im[0], ssm_log_dt[0], ssm_B_re[0],
                                            ssm_B_im[0], ssm_C_re[0], ssm_C_im[0], ssm_D[0])
    nslab = seq // SLAB_T
    yg = _s5_call(u.reshape(SSM_GROUPS, nslab * bsz * SSM_GROUP, SLAB_T), um, _chunk_permutation(),
                  lags, w_mat.astype(_BF), v_mat.astype(_BF), tab, nslab=nslab, bsz=bsz)
    y = yg.reshape(SSM_GROUPS, nslab, bsz * SSM_GROUP, SLAB_T)

    return _post_call(x, attn, y, w_glu[0].astype(_BF), row(g_mix_out[0]), w_out[0].astype(_BF),
                      row(g_post_mix[0]), row(g_pre_mlp[0]), w_mlp_up[0].astype(_BF),
                      w_mlp_down[0].astype(_BF), row(g_post_mlp[0]), tile=ROW_TILE)
```

```python
import functools
import math

import jax
import jax.numpy as jnp
from jax import lax
from jax.experimental import pallas as pl
from jax.experimental.pallas import tpu as pltpu

D_MODEL = 1024
N_META = 16
ATTN_WIDTH = 512
SSM_WIDTH = 512
N_HEADS = 8
V_HEAD_DIM = 64
QK_NOPE_DIM = 64
QK_ROPE_DIM = 32
QK_HEAD_DIM = QK_NOPE_DIM + QK_ROPE_DIM
Q_LORA_RANK = 384
KV_LORA_RANK = 256
ROPE_BASE = 10000.0
SSM_GROUP = 16
SSM_GROUPS = 32
SSM_STATE = 64
D_FF = 4 * D_MODEL
EPS = 1e-6
OFF_KR = Q_LORA_RANK + KV_LORA_RANK
OFF_U = OFF_KR + QK_ROPE_DIM

LANES = 128
SUBLANES = 8
HEAD_PAD = LANES
ONES_LANE = V_HEAD_DIM
ATTN_TQ = 1024
ATTN_TK = 256
V_ROWS = 112
PW_Q = 0
PW_KV = PW_Q + Q_LORA_RANK
PW_END = PW_KV + KV_LORA_RANK

SSM_CHUNK = 32
CHUNK_W = SSM_CHUNK * SSM_GROUP
STATE_W = 4 * SSM_STATE
SLAB_T = LANES
SLAB_CHUNKS = SLAB_T // SSM_CHUNK

ROW_TILE = 512
PROJ_TILE = 1024
FF_TILE = 1024
POST_SLABS = 2
BOUND_SLACK = 1.0 + 2.0 ** -6
BOUND_LIMIT = 40.0
V7X_VMEM_BYTES = 64 * 1024 * 1024
VMEM_LIMIT = V7X_VMEM_BYTES - 8 * 1024 * 1024

_BF = jnp.bfloat16
_F32 = jnp.float32


def _dot(a, b):
    return jnp.dot(a, b, preferred_element_type=_F32)


def _rms(x, g):
    return x * lax.rsqrt(jnp.mean(x * x, axis=-1, keepdims=True) + EPS) * g


_NT = (((1,), (1,)), ((), ()))


def _rotate(x1, x2, cos_t, sin_t):
    return x1 * cos_t - x2 * sin_t, x1 * sin_t + x2 * cos_t


def _proj_kernel(x_ref, rope_ref, *refs, meta):
    if meta:
        gpre_ref, win_ref, wkr_ref, wu_ref, gkv_ref, wk_ref, wv_ref, k_ref, v_ref, u_ref = refs
    else:
        (gpre_ref, win_ref, wkr_ref, wu_ref, gq_ref, wq_ref, gkv_ref, wk_ref, wv_ref,
         q_ref, k_ref, v_ref, u_ref, qn_ref, kmx_ref) = refs
    tile = x_ref.shape[0]
    half = QK_ROPE_DIM // 2
    cos_t, sin_t = rope_ref[0], rope_ref[1]
    xn = _rms(x_ref[...], gpre_ref[...]).astype(_BF)
    proj = _dot(xn, win_ref[...])
    kvn = _rms(proj[:, PW_KV:PW_END], gkv_ref[...]).astype(_BF)
    krt = lax.dot_general(wkr_ref[...], xn, _NT, preferred_element_type=_F32)
    r1, r2 = _rotate(krt[:half], krt[half:], cos_t, sin_t)
    kr = jnp.concatenate([jnp.zeros((QK_NOPE_DIM, tile), _F32), r1, r2,
                          jnp.zeros((HEAD_PAD - QK_HEAD_DIM, tile), _F32)], axis=0).T
    kk = _dot(kvn, wk_ref[...])
    vt = lax.dot_general(wv_ref[...], kvn, _NT, preferred_element_type=_F32)
    ones_tail = (lax.broadcasted_iota(jnp.int32, (HEAD_PAD - V_HEAD_DIM, tile), 0) == 0).astype(_F32)
    if meta:
        u_ref[...] = lax.dot_general(xn, wu_ref[...], _NT, preferred_element_type=_F32)
    else:
        ut = lax.dot_general(wu_ref[...], xn, _NT, preferred_element_type=_F32)
        for g in range(SSM_GROUPS):
            for c in range(tile // SLAB_T):
                u_ref[g, c] = ut[g * SSM_GROUP:(g + 1) * SSM_GROUP, c * SLAB_T:(c + 1) * SLAB_T]
        qn = _rms(proj[:, PW_Q:PW_KV], gq_ref[...]).astype(_BF)
        qt = lax.dot_general(wq_ref[...], qn, _NT, preferred_element_type=_F32)
        zero_rows = jnp.zeros((HEAD_PAD - QK_HEAD_DIM, tile), _F32)
    for h in range(N_HEADS):
        k_h = (kk[:, h * HEAD_PAD:(h + 1) * HEAD_PAD] + kr).astype(_BF)
        k_ref[h] = k_h
        v_ref[h] = jnp.concatenate([vt[h * V_HEAD_DIM:(h + 1) * V_HEAD_DIM], ones_tail], axis=0).astype(_BF)
        if not meta:
            blk = qt[h * QK_HEAD_DIM:(h + 1) * QK_HEAD_DIM]
            r1, r2 = _rotate(blk[QK_NOPE_DIM:QK_NOPE_DIM + half], blk[QK_NOPE_DIM + half:], cos_t, sin_t)
            qt_h = jnp.concatenate([blk[:QK_NOPE_DIM], r1, r2, zero_rows], axis=0).astype(_BF)
            q_ref[h] = qt_h
            qt_f = qt_h.astype(_F32)
            qn_ref[h] = jnp.sqrt(jnp.sum(qt_f * qt_f, axis=0, keepdims=True))
            k_f = k_h.astype(_F32)
            kmx_ref[h] = jnp.broadcast_to(jnp.max(jnp.sum(k_f * k_f, axis=1, keepdims=True), axis=0, keepdims=True),
                                          (1, LANES))


def _const_spec(shape):
    nd = len(shape)
    return pl.BlockSpec(shape, lambda *_: (0,) * nd)


def _proj_call(x, rope, gpre, win, wkr_t, wu_t, gq, wq_t, gkv, wk, wv_t, *, tile, meta):
    bsz, seq, _ = x.shape
    nt = seq // tile
    row_spec = lambda w: pl.BlockSpec((None, tile, w), lambda b, i: (b, i, 0))
    rope_spec = pl.BlockSpec((None, None, 2, QK_ROPE_DIM // 2, tile), lambda b, i: (b, i, 0, 0, 0))
    k_spec = pl.BlockSpec((None, N_HEADS, tile, HEAD_PAD), lambda b, i: (b, 0, i, 0))
    t_spec = pl.BlockSpec((None, N_HEADS, None, HEAD_PAD, tile), lambda b, i: (b, 0, i, 0, 0))
    k_shape = jax.ShapeDtypeStruct((bsz, N_HEADS, seq, HEAD_PAD), _BF)
    t_shape = jax.ShapeDtypeStruct((bsz, N_HEADS, nt, HEAD_PAD, tile), _BF)
    w_specs = lambda *ws: [_const_spec(w.shape) for w in ws]
    if meta:
        args = (x, rope, gpre, win, wkr_t, wu_t, gkv, wk, wv_t)
        in_specs = [row_spec(D_MODEL), rope_spec] + w_specs(*args[2:])
        out_specs = [k_spec, t_spec, row_spec(SSM_WIDTH)]
        out_shape = [k_shape, t_shape, jax.ShapeDtypeStruct((bsz, seq, SSM_WIDTH), _F32)]
    else:
        args = (x, rope, gpre, win, wkr_t, wu_t, gq, wq_t, gkv, wk, wv_t)
        in_specs = [row_spec(D_MODEL), rope_spec] + w_specs(*args[2:])
        norm_spec = lambda w: pl.BlockSpec((None, N_HEADS, None, 1, w), lambda b, i: (b, 0, i, 0, 0))
        u_spec = pl.BlockSpec((SSM_GROUPS, tile // SLAB_T, SSM_GROUP, SLAB_T), lambda b, i: (0, i, b, 0))
        out_specs = [t_spec, k_spec, t_spec, u_spec, norm_spec(tile), norm_spec(LANES)]
        out_shape = [t_shape, k_shape, t_shape,
                     jax.ShapeDtypeStruct((SSM_GROUPS, seq // SLAB_T, bsz * SSM_GROUP, SLAB_T), _F32),
                     jax.ShapeDtypeStruct((bsz, N_HEADS, nt, 1, tile), _F32),
                     jax.ShapeDtypeStruct((bsz, N_HEADS, nt, 1, LANES), _F32)]
    return pl.pallas_call(
        functools.partial(_proj_kernel, meta=meta),
        grid=(bsz, nt),
        in_specs=in_specs,
        out_specs=out_specs,
        out_shape=out_shape,
        compiler_params=pltpu.CompilerParams(
            dimension_semantics=("parallel", "parallel"), vmem_limit_bytes=VMEM_LIMIT),
        name="proj_meta" if meta else "proj",
    )(*args)


def _attn_finish(accs, o_ref):
    halves = [(acc * (1.0 / acc[ONES_LANE:ONES_LANE + 1, :]))[:V_HEAD_DIM] for acc in accs]
    o_ref[...] = jnp.concatenate(halves, axis=0).T


def _lane_concat(ref, hh):
    return jnp.concatenate([ref[hh, j] for j in range(ref.shape[1])], axis=1)


def _attn_bounded_kernel(qt_ref, mrow_ref, k_ref, vt_ref, km_ref, vtm_ref, o_ref, *, nk, tk):
    tq = o_ref.shape[0]
    per_slab = vt_ref.shape[3] // tk
    key_row = lax.broadcasted_iota(jnp.int32, (LANES, tq), 0)
    accs = []
    for hh in range(2):
        qt = _lane_concat(qt_ref, hh)
        mrow = _lane_concat(mrow_ref, hh)
        s0 = jnp.where(key_row < N_META, _dot(km_ref[hh], qt), -jnp.inf)
        acc = _dot(vtm_ref[hh, :V_ROWS, :], jnp.exp2(s0 - mrow).astype(_BF))
        scores = lambda c: _dot(k_ref[hh, c * tk:(c + 1) * tk, :], qt)
        s_next = scores(0)
        for c in range(nk):
            s = s_next
            if c + 1 < nk:
                s_next = scores(c + 1)
            vt_c = vt_ref[hh, c // per_slab, :V_ROWS, (c % per_slab) * tk:(c % per_slab + 1) * tk]
            acc = acc + _dot(vt_c, jnp.exp2(s - mrow).astype(_BF))
        accs.append(acc)
    _attn_finish(accs, o_ref)


def _attn_online_kernel(qt_ref, k_ref, vt_ref, km_ref, vtm_ref, o_ref, s0_scr, s1_scr, m_scr, acc_scr, *, nk, tk):
    tq = o_ref.shape[0]
    key_row = lax.broadcasted_iota(jnp.int32, (LANES, tq), 0)
    for hh in range(2):
        s0 = jnp.where(key_row < N_META, _dot(km_ref[hh], _lane_concat(qt_ref, hh)), -jnp.inf)
        m0 = jnp.max(s0, axis=0, keepdims=True)
        m_scr[hh] = m0
        acc_scr[hh] = _dot(vtm_ref[hh, :V_ROWS, :], jnp.exp2(s0 - m0).astype(_BF))

    def scores(buf, c):
        off = pl.multiple_of(c * tk, tk)
        for hh in range(2):
            buf[hh] = _dot(k_ref[hh, pl.ds(off, tk), :], _lane_concat(qt_ref, hh))

    def accumulate(buf, c):
        for hh in range(2):
            s = buf[hh]
            m = m_scr[hh]
            m_new = jnp.maximum(m, jnp.max(s, axis=0, keepdims=True))
            m_scr[hh] = m_new
            p = jnp.exp2(s - m_new).astype(_BF)
            acc_scr[hh] = jnp.exp2(m - m_new) * acc_scr[hh] + _dot(vt_ref[hh, c, :V_ROWS, :], p)

    scores(s0_scr, 0)

    def body(t, _):
        scores(s1_scr, 2 * t + 1)
        accumulate(s0_scr, 2 * t)
        scores(s0_scr, 2 * t + 2)
        accumulate(s1_scr, 2 * t + 1)
        return 0

    lax.fori_loop(0, nk // 2 - 1, body, 0)
    scores(s1_scr, nk - 1)
    accumulate(s0_scr, nk - 2)
    accumulate(s1_scr, nk - 1)
    _attn_finish([acc_scr[0], acc_scr[1]], o_ref)


def _attn_call(qt, mrow, k, vt, km, vtm, *, bounded):
    bsz, _, nslab, _, slab = qt.shape
    seq = k.shape[2]
    tq = min(ATTN_TQ, seq)
    tk = min(ATTN_TK, slab)
    qs = tq // slab
    assert tq % slab == 0 and seq % tq == 0 and slab % tk == 0
    q_spec = lambda rows: pl.BlockSpec((None, 2, qs, rows, slab), lambda b, hp, i: (b, hp, i, 0, 0))
    kv_specs = [pl.BlockSpec((None, 2, seq, HEAD_PAD), lambda b, hp, i: (b, hp, 0, 0)),
                pl.BlockSpec((None, 2, nslab, HEAD_PAD, slab), lambda b, hp, i: (b, hp, 0, 0, 0)),
                pl.BlockSpec((2, LANES, HEAD_PAD), lambda b, hp, i: (hp, 0, 0)),
                pl.BlockSpec((2, HEAD_PAD, LANES), lambda b, hp, i: (hp, 0, 0))]
    if bounded:
        body = functools.partial(_attn_bounded_kernel, nk=seq // tk, tk=tk)
        in_specs = [q_spec(HEAD_PAD), q_spec(1)] + kv_specs
        args, scratch = (qt, mrow, k, vt, km, vtm), []
    else:
        assert nslab % 2 == 0 and nslab >= 4
        body = functools.partial(_attn_online_kernel, nk=nslab, tk=slab)
        in_specs = [q_spec(HEAD_PAD)] + kv_specs
        args = (qt, k, vt, km, vtm)
        scratch = [pltpu.VMEM((2, slab, tq), _F32), pltpu.VMEM((2, slab, tq), _F32),
                   pltpu.VMEM((2, 1, tq), _F32), pltpu.VMEM((2, V_ROWS, tq), _F32)]
    return pl.pallas_call(
        body,
        grid=(bsz, N_HEADS // 2, seq // tq),
        in_specs=in_specs,
        out_specs=pl.BlockSpec((None, tq, LANES), lambda b, hp, i: (b, i, hp)),
        out_shape=jax.ShapeDtypeStruct((bsz, seq, ATTN_WIDTH), _F32),
        scratch_shapes=scratch,
        compiler_params=pltpu.CompilerParams(
            dimension_semantics=("parallel", "parallel", "arbitrary"), vmem_limit_bytes=VMEM_LIMIT),
        name="attn_bounded" if bounded else "attn_online",
    )(*args)


def _cmul_add(ar, ai, xr, xi, sr, si):
    return ar * xr - ai * xi + sr, ar * xi + ai * xr + si


def _s5_kernel(a_ref, um_ref, q_ref, lag_ref, w_ref, v_ref, t_ref, y_ref, sup_scr, ent_scr,
               *, nslab, bsz):
    rows = nslab * bsz
    half = STATE_W // 2
    a = [a_ref[pl.ds(i, rows, stride=SSM_GROUP), :] for i in range(SSM_GROUP)]
    uc = [jnp.concatenate([x[:, c * SSM_CHUNK:(c + 1) * SSM_CHUNK] for x in a], axis=1).astype(_BF)
          for c in range(SLAB_CHUNKS)]
    w = w_ref[...]
    s = [_dot(u, w) for u in uc]
    sr = [x[:, :half] for x in s]
    si = [x[:, half:] for x in s]
    t = t_ref[...]
    trow = lambda r: t[r:r + 1, :]
    sup_r = sup_i = None
    for c in range(SLAB_CHUNKS):
        cr, ci = trow(c), trow(SLAB_CHUNKS + c)
        pr = cr * sr[c] - ci * si[c]
        pi = cr * si[c] + ci * sr[c]
        sup_r = pr if sup_r is None else sup_r + pr
        sup_i = pi if sup_i is None else sup_i + pi
    sup_scr[:, :half] = sup_r
    sup_scr[:, half:] = sup_i

    lane = lax.broadcasted_iota(jnp.int32, (bsz, half), 1)
    fwd = lane < SSM_STATE
    sm = _dot(um_ref[...], w)
    xr = jnp.where(fwd, sm[:bsz, :half], 0.0)
    xi = jnp.where(fwd, sm[:bsz, half:], 0.0)
    a_slab_r, a_slab_i = trow(2 * SLAB_CHUNKS), trow(2 * SLAB_CHUNKS + 1)
    for j in range(nslab):
        rf = j * bsz
        rb = (nslab - 1 - j) * bsz
        ent_scr[rf:rf + bsz, 0:SSM_STATE] = xr[:, 0:SSM_STATE]
        ent_scr[rb:rb + bsz, SSM_STATE:half] = xr[:, SSM_STATE:half]
        ent_scr[rf:rf + bsz, half:half + SSM_STATE] = xi[:, 0:SSM_STATE]
        ent_scr[rb:rb + bsz, half + SSM_STATE:STATE_W] = xi[:, SSM_STATE:half]
        s_r = jnp.where(fwd, sup_scr[rf:rf + bsz, :half], sup_scr[rb:rb + bsz, :half])
        s_i = jnp.where(fwd, sup_scr[rf:rf + bsz, half:], sup_scr[rb:rb + bsz, half:])
        xr, xi = _cmul_add(a_slab_r, a_slab_i, xr, xi, s_r, s_i)

    ent = ent_scr[...]
    a_r, a_i = trow(2 * SLAB_CHUNKS + 2), trow(2 * SLAB_CHUNKS + 3)
    xf = [(ent[:, :half], ent[:, half:])]
    for c in range(1, SLAB_CHUNKS):
        xf.append(_cmul_add(a_r, a_i, xf[-1][0], xf[-1][1], sr[c - 1], si[c - 1]))
    xb = [(ent[:, :half], ent[:, half:])]
    for c in range(SLAB_CHUNKS - 2, -1, -1):
        xb.insert(0, _cmul_add(a_r, a_i, xb[0][0], xb[0][1], sr[c + 1], si[c + 1]))
    fwd_rows = lax.broadcasted_iota(jnp.int32, (rows, half), 1) < SSM_STATE
    m = jnp.concatenate(
        [pltpu.roll(jnp.broadcast_to(lag_ref[i:i + 1, :], (SSM_CHUNK, 2 * CHUNK_W)), 0, 1,
                    stride=SSM_GROUP, stride_axis=0)[:, :CHUNK_W] for i in range(SSM_GROUP)], axis=0).astype(_BF)
    q = q_ref[...]
    m = _dot(m, q).astype(_BF)
    v = _dot(v_ref[...], q).astype(_BF)
    ys = []
    for c in range(SLAB_CHUNKS):
        xin = jnp.concatenate([jnp.where(fwd_rows, xf[c][0], xb[c][0]),
                               jnp.where(fwd_rows, xf[c][1], xb[c][1])], axis=1).astype(_BF)
        ys.append(_dot(uc[c], m) + _dot(xin, v))
    for o in range(SSM_GROUP):
        y_ref[pl.ds(o, rows, stride=SSM_GROUP), :] = jnp.concatenate(
            [y[:, o * SSM_CHUNK:(o + 1) * SSM_CHUNK] for y in ys], axis=1)


def _s5_call(a, um, perm, lags, w_mat, v_mat, tab, *, nslab, bsz):
    n = nslab * bsz * SSM_GROUP
    rows = nslab * bsz
    g_spec = lambda *shape: pl.BlockSpec((None,) + shape, lambda g: (g,) + (0,) * len(shape))
    perm_spec = pl.BlockSpec((CHUNK_W, CHUNK_W), lambda g: (0, 0), pipeline_mode=pl.Buffered(1))
    return pl.pallas_call(
        functools.partial(_s5_kernel, nslab=nslab, bsz=bsz),
        grid=(SSM_GROUPS,),
        in_specs=[g_spec(n, SLAB_T), g_spec(SUBLANES, CHUNK_W), perm_spec,
                  g_spec(SSM_GROUP, 2 * CHUNK_W), g_spec(CHUNK_W, STATE_W), g_spec(STATE_W, CHUNK_W),
                  g_spec(2 * SUBLANES, STATE_W // 2)],
        out_specs=g_spec(n, SLAB_T),
        out_shape=jax.ShapeDtypeStruct((SSM_GROUPS, n, SLAB_T), _F32),
        scratch_shapes=[pltpu.VMEM((rows, STATE_W), _F32), pltpu.VMEM((rows, STATE_W), _F32)],
        compiler_params=pltpu.CompilerParams(
            dimension_semantics=("parallel",), vmem_limit_bytes=VMEM_LIMIT),
        name="s5",
    )(a, um, perm, lags, w_mat, v_mat, tab)


def _s5_matrices(a_re, a_im, log_dt, b_re, b_im, c_re, c_im, d_skip):
    tc, nc = SSM_CHUNK, SLAB_CHUNKS
    f32 = lambda t: t.astype(_F32)
    hi = lax.Precision.HIGHEST
    re, im = jnp.minimum(f32(a_re), -1e-4), f32(a_im)
    dt = jnp.exp(f32(log_dt))[..., None]
    ks = jnp.concatenate([jnp.arange(tc + 1, dtype=_F32), tc * jnp.arange(2, nc + 1, dtype=_F32)])
    mag = jnp.exp((re * dt)[:, :, None, :] * ks[None, None, :, None])
    ang = (im * dt)[:, :, None, :] * ks[None, None, :, None]
    pr, pi = lax.optimization_barrier((mag * jnp.cos(ang), mag * jnp.sin(ang)))
    l1r, l1i, den = pr[:, :, 1] - 1.0, pi[:, :, 1], re * re + im * im
    fr, fi = ((l1r * re + l1i * im) / den)[..., None], ((l1i * re - l1r * im) / den)[..., None]
    br, bi = f32(b_re), f32(b_im)
    bbr, bbi = fr * br - fi * bi, fr * bi + fi * br
    prt, pit = pr.transpose(0, 1, 3, 2), pi.transpose(0, 1, 3, 2)
    cr, ci = f32(c_re), f32(c_im)
    qr, qi = prt[:, :, :, :tc, None], pit[:, :, :, :tc, None]
    tr = qr * bbr[:, :, :, None, :] - qi * bbi[:, :, :, None, :]
    ti = qr * bbi[:, :, :, None, :] + qi * bbr[:, :, :, None, :]
    tt = jnp.concatenate([tr, ti], axis=2).reshape(2, SSM_GROUPS, 2 * SSM_STATE, tc * SSM_GROUP)
    kern = jnp.einsum('dgop,dgpn->dgon', jnp.concatenate([cr, -ci], axis=-1), tt, precision=hi)
    kern = kern.reshape(2, SSM_GROUPS, SSM_GROUP, tc, SSM_GROUP).transpose(0, 1, 3, 2, 4)
    d_g = f32(d_skip).reshape(SSM_GROUPS, SSM_GROUP)
    center = kern[0][:, :1] + kern[1][:, :1] + (jnp.eye(SSM_GROUP, dtype=_F32)[None] * d_g[:, :, None])[:, None]
    lags = jnp.concatenate([center, kern[0][:, 1:], jnp.zeros_like(center), kern[1][:, :0:-1]], axis=1)
    lags = lags.transpose(0, 3, 1, 2).reshape(SSM_GROUPS, SSM_GROUP, 2 * CHUNK_W)
    wr = jnp.stack([pr[0][:, tc - 1::-1], pr[1][:, :tc]])[:, :, None]
    wi = jnp.stack([pi[0][:, tc - 1::-1], pi[1][:, :tc]])[:, :, None]
    btr, bti = bbr.transpose(0, 1, 3, 2)[:, :, :, None], bbi.transpose(0, 1, 3, 2)[:, :, :, None]
    w_re, w_im = btr * wr - bti * wi, btr * wi + bti * wr
    w_mat = jnp.concatenate([w_re[0], w_re[1], w_im[0], w_im[1]], axis=-1).reshape(SSM_GROUPS, CHUNK_W, STATE_W)
    crt, cit = cr.transpose(0, 1, 3, 2)[:, :, :, None, :], ci.transpose(0, 1, 3, 2)[:, :, :, None, :]
    qr, qi = prt[:, :, :, :tc + 1, None], pit[:, :, :, :tc + 1, None]
    gr, gi = crt * qr - cit * qi, crt * qi + cit * qr
    v_mat = jnp.concatenate([gr[0][:, :, 1:tc + 1], gr[1][:, :, tc:0:-1], -gi[0][:, :, 1:tc + 1], -gi[1][:, :, tc:0:-1]],
                            axis=1).reshape(SSM_GROUPS, STATE_W, CHUNK_W)
    at = [0, tc] + [tc - 1 + n for n in range(2, nc + 1)]
    both = lambda part, nf, nb: jnp.concatenate([part[0][:, at[nf]], part[1][:, at[nb]]], axis=-1)
    rows = ([both(pr, nc - 1 - c, c) for c in range(nc)] + [both(pi, nc - 1 - c, c) for c in range(nc)]
            + [both(pr, nc, nc), both(pi, nc, nc), both(pr, 1, 1), both(pi, 1, 1)])
    tab = jnp.stack(rows, axis=1)
    tab = jnp.pad(tab, ((0, 0), (0, 2 * SUBLANES - tab.shape[1]), (0, 0)))
    return lags, w_mat, v_mat, tab


def _chunk_permutation():
    r = lax.broadcasted_iota(jnp.int32, (CHUNK_W, CHUNK_W), 0)
    c = lax.broadcasted_iota(jnp.int32, (CHUNK_W, CHUNK_W), 1)
    return (r == (c % SSM_CHUNK) * SSM_GROUP + c // SSM_CHUNK).astype(_BF)


def _post_kernel(x_ref, attn_ref, y_ref, wglu_ref, gmix_ref, wout_ref, gpm_ref, gpre_ref,
                 wup_ref, wdn_ref, gpost_ref, o_ref):
    nslab = y_ref.shape[1]
    gmix = gmix_ref[...]
    per = min(POST_SLABS, nslab)
    blocks = range(nslab // per)
    rows = [pl.ds(r * per * SLAB_T, per * SLAB_T) for r in blocks]
    gy = []
    for r in blocks:
        yt = jnp.concatenate([jnp.concatenate([y_ref[g, r * per + c] for c in range(per)], axis=1)
                              for g in range(SSM_GROUPS)], axis=0)
        y = yt.T
        gy.append((0.5 * y * (1.0 + jnp.tanh(math.sqrt(2.0 / math.pi) * (y + 0.044715 * (y * y * y))))).astype(_BF))
    z = [_dot(gy[r], wglu_ref[...]) for r in blocks]
    mix = []
    for r in blocks:
        ssm = z[r][:, :SSM_WIDTH] * (1.0 / (1.0 + jnp.exp(-z[r][:, SSM_WIDTH:])))
        mix.append(jnp.concatenate([_rms(attn_ref[rows[r], :], gmix[:, :ATTN_WIDTH]),
                                    _rms(ssm, gmix[:, ATTN_WIDTH:])], axis=-1).astype(_BF))
    mixed = [_dot(mix[r], wout_ref[...]) for r in blocks]
    h1 = [x_ref[rows[r], :] + _rms(mixed[r], gpm_ref[...]) for r in blocks]
    hn = [_rms(h1[r], gpre_ref[...]).astype(_BF) for r in blocks]
    acc = [None for _ in blocks]
    for c in range(D_FF // FF_TILE):
        for r in blocks:
            up = jnp.maximum(_dot(hn[r], wup_ref[:, c * FF_TILE:(c + 1) * FF_TILE]), 0.0)
            part = _dot((up * up).astype(_BF), wdn_ref[c * FF_TILE:(c + 1) * FF_TILE, :])
            acc[r] = part if acc[r] is None else acc[r] + part
    for r in blocks:
        o_ref[rows[r], :] = h1[r] + _rms(acc[r], gpost_ref[...])


def _post_call(x, attn, y, wglu, gmix, wout, gpm, gpre, wup, wdn, gpost, *, tile):
    bsz, seq, _ = x.shape
    row_spec = lambda w: pl.BlockSpec((None, tile, w), lambda b, i: (b, i, 0))
    wspec = lambda shape: pl.BlockSpec(shape, lambda b, i: (0, 0), pipeline_mode=pl.Buffered(1))
    y_spec = pl.BlockSpec((SSM_GROUPS, tile // SLAB_T, SSM_GROUP, SLAB_T), lambda b, i: (0, i, b, 0))
    return pl.pallas_call(
        _post_kernel,
        grid=(bsz, seq // tile),
        in_specs=[row_spec(D_MODEL), row_spec(ATTN_WIDTH), y_spec,
                  wspec((SSM_WIDTH, 2 * SSM_WIDTH)), wspec((1, D_MODEL)), wspec((D_MODEL, D_MODEL)),
                  wspec((1, D_MODEL)), wspec((1, D_MODEL)), wspec((D_MODEL, D_FF)),
                  wspec((D_FF, D_MODEL)), wspec((1, D_MODEL))],
        out_specs=row_spec(D_MODEL),
        out_shape=jax.ShapeDtypeStruct((bsz, seq, D_MODEL), _F32),
        compiler_params=pltpu.CompilerParams(
            dimension_semantics=("parallel", "parallel"), vmem_limit_bytes=VMEM_LIMIT),
        name="post",
    )(x, attn, y, wglu, gmix, wout, gpm, gpre, wup, wdn, gpost)


def _rope_tables(pos, tile):
    half = QK_ROPE_DIM // 2
    inv = 1.0 / (ROPE_BASE ** (jnp.arange(0, QK_ROPE_DIM, 2, dtype=_F32) / QK_ROPE_DIM))
    ang = pos.astype(_F32)[:, None, :] * inv[None, :, None]
    bsz, seq = pos.shape
    rope = jnp.stack([jnp.cos(ang), jnp.sin(ang)], axis=1)
    return rope.reshape(bsz, 2, half, seq // tile, tile).transpose(0, 3, 1, 2, 4)


def _prep_weights(w_in, w_uq, w_ukv):
    scale = QK_HEAD_DIM ** -0.5 * math.log2(math.e)
    win = w_in[:, :OFF_KR].astype(_BF)
    wkr_t = w_in[:, OFF_KR:OFF_U].T.astype(_BF)
    wu_t = w_in[:, OFF_U:].T.astype(_BF)
    wq_t = (w_uq * scale).T.astype(_BF)
    wkv3 = w_ukv.reshape(KV_LORA_RANK, N_HEADS, QK_NOPE_DIM + V_HEAD_DIM)
    wk = jnp.concatenate([wkv3[..., :QK_NOPE_DIM],
                          jnp.zeros((KV_LORA_RANK, N_HEADS, HEAD_PAD - QK_NOPE_DIM), _F32)], axis=-1)
    wk = wk.reshape(KV_LORA_RANK, N_HEADS * HEAD_PAD).astype(_BF)
    wv_t = wkv3[..., QK_NOPE_DIM:].reshape(KV_LORA_RANK, N_HEADS * V_HEAD_DIM).T.astype(_BF)
    return win, wkr_t, wu_t, wq_t, wk, wv_t


def kernel(x, positions, meta_tokens, g_pre_mix, w_in, g_q_lat, w_uq, g_kv_lat, w_ukv,
           ssm_A_re, ssm_A_im, ssm_log_dt, ssm_B_re, ssm_B_im, ssm_C_re, ssm_C_im, ssm_D,
           w_glu, g_mix_out, w_out, g_post_mix, g_pre_mlp, w_mlp_up, w_mlp_down, g_post_mlp):
    bsz, seq, _ = x.shape
    assert seq % ROW_TILE == 0 and ROW_TILE % SLAB_T == 0 and bsz <= SUBLANES
    assert seq % PROJ_TILE == 0 and PROJ_TILE % SLAB_T == 0
    assert N_META <= SSM_CHUNK
    row = lambda g: g.reshape(1, -1).astype(_F32)

    win, wkr_t, wu_t, wq_t, wk, wv_t = _prep_weights(w_in[0], w_uq[0], w_ukv[0])
    weights = (row(g_pre_mix[0]), win, wkr_t, wu_t, row(g_q_lat[0]), wq_t, row(g_kv_lat[0]), wk, wv_t)
    rope = _rope_tables(positions.astype(jnp.int32) + N_META, PROJ_TILE)
    qt, k, vt, u, q_norm, k_sq = _proj_call(x, rope, *weights, tile=PROJ_TILE, meta=False)
    meta_x = jnp.pad(meta_tokens.astype(x.dtype), ((0, LANES - N_META), (0, 0)))[None]
    rope_m = _rope_tables(jnp.arange(LANES, dtype=jnp.int32)[None], LANES)
    k_m, vt_m, u_m = _proj_call(meta_x, rope_m, *weights, tile=LANES, meta=True)
    km, vtm, u_m = k_m[0], vt_m[0, :, 0], u_m[:, :N_META]
    k_m_sq = jnp.max(jnp.sum(jnp.square(k_m[0].astype(_F32)), axis=-1), axis=-1)
    k_max = jnp.sqrt(jnp.maximum(jnp.max(k_sq, axis=(2, 3, 4)), k_m_sq[None]))
    mrow = q_norm * (k_max * BOUND_SLACK)[:, :, None, None, None]
    attn = lax.cond(jnp.max(mrow) <= BOUND_LIMIT,
                    lambda: _attn_call(qt, mrow, k, vt, km, vtm, bounded=True),
                    lambda: _attn_call(qt, mrow, k, vt, km, vtm, bounded=False))

    um = u_m[0].astype(_BF).reshape(N_META, SSM_GROUPS, SSM_GROUP).transpose(1, 2, 0)
    um = jnp.pad(um, ((0, 0), (0, 0), (SSM_CHUNK - N_META, 0))).reshape(SSM_GROUPS, 1, CHUNK_W)
    um = jnp.broadcast_to(um, (SSM_GROUPS, SUBLANES, CHUNK_W))
    lags, w_mat, v_mat, tab = _s5_matrices(ssm_A_re[0], ssm_A_im[0], ssm_log_dt[0], ssm_B_re[0],
                                            ssm_B_im[0], ssm_C_re[0], ssm_C_im[0], ssm_D[0])
    nslab = seq // SLAB_T
    yg = _s5_call(u.reshape(SSM_GROUPS, nslab * bsz * SSM_GROUP, SLAB_T), um, _chunk_permutation(),
                  lags, w_mat.astype(_BF), v_mat.astype(_BF), tab, nslab=nslab, bsz=bsz)
    y = yg.reshape(SSM_GROUPS, nslab, bsz * SSM_GROUP, SLAB_T)

    return _post_call(x, attn, y, w_glu[0].astype(_BF), row(g_mix_out[0]), w_out[0].astype(_BF),
                      row(g_post_mix[0]), row(g_pre_mlp[0]), w_mlp_up[0].astype(_BF),
                      w_mlp_down[0].astype(_BF), row(g_post_mlp[0]), tile=ROW_TILE)
```

```python
import functools
import math

import jax
import jax.numpy as jnp
from jax import lax
from jax.experimental import pallas as pl
from jax.experimental.pallas import tpu as pltpu

D_MODEL = 1024
N_META = 16
ATTN_WIDTH = 512
SSM_WIDTH = 512
N_HEADS = 8
V_HEAD_DIM = 64
QK_NOPE_DIM = 64
QK_ROPE_DIM = 32
QK_HEAD_DIM = QK_NOPE_DIM + QK_ROPE_DIM
Q_LORA_RANK = 384
KV_LORA_RANK = 256
ROPE_BASE = 10000.0
SSM_GROUP = 16
SSM_GROUPS = 32
SSM_STATE = 64
D_FF = 4 * D_MODEL
EPS = 1e-6
OFF_KR = Q_LORA_RANK + KV_LORA_RANK
OFF_U = OFF_KR + QK_ROPE_DIM

LANES = 128
SUBLANES = 8
HEAD_PAD = LANES
ONES_LANE = V_HEAD_DIM
ATTN_TQ = 1024
ATTN_TK = 256
V_ROWS = 112
PW_Q = 0
PW_KV = PW_Q + Q_LORA_RANK
PW_END = PW_KV + KV_LORA_RANK

SSM_CHUNK = 32
CHUNK_W = SSM_CHUNK * SSM_GROUP
STATE_W = 4 * SSM_STATE
SLAB_T = LANES
SLAB_CHUNKS = SLAB_T // SSM_CHUNK

ROW_TILE = 512
PROJ_TILE = 1024
FF_TILE = 1024
POST_SLABS = 2
BOUND_SLACK = 1.0 + 2.0 ** -6
BOUND_LIMIT = 40.0
V7X_VMEM_BYTES = 64 * 1024 * 1024
VMEM_LIMIT = V7X_VMEM_BYTES - 8 * 1024 * 1024

_BF = jnp.bfloat16
_F32 = jnp.float32


def _dot(a, b):
    return jnp.dot(a, b, preferred_element_type=_F32)


def _rms(x, g):
    return x * lax.rsqrt(jnp.mean(x * x, axis=-1, keepdims=True) + EPS) * g


_NT = (((1,), (1,)), ((), ()))


def _rotate(x1, x2, cos_t, sin_t):
    return x1 * cos_t - x2 * sin_t, x1 * sin_t + x2 * cos_t


def _proj_kernel(x_ref, rope_ref, *refs, meta):
    if meta:
        gpre_ref, win_ref, wkr_ref, wu_ref, gkv_ref, wk_ref, wv_ref, k_ref, v_ref, u_ref = refs
    else:
        (gpre_ref, win_ref, wkr_ref, wu_ref, gq_ref, wq_ref, gkv_ref, wk_ref, wv_ref,
         q_ref, k_ref, v_ref, u_ref, qn_ref, kmx_ref) = refs
    tile = x_ref.shape[0]
    half = QK_ROPE_DIM // 2
    cos_t, sin_t = rope_ref[0], rope_ref[1]
    xn = _rms(x_ref[...], gpre_ref[...]).astype(_BF)
    proj = _dot(xn, win_ref[...])
    kvn = _rms(proj[:, PW_KV:PW_END], gkv_ref[...]).astype(_BF)
    krt = lax.dot_general(wkr_ref[...], xn, _NT, preferred_element_type=_F32)
    r1, r2 = _rotate(krt[:half], krt[half:], cos_t, sin_t)
    kr = jnp.concatenate([jnp.zeros((QK_NOPE_DIM, tile), _F32), r1, r2,
                          jnp.zeros((HEAD_PAD - QK_HEAD_DIM, tile), _F32)], axis=0).T
    kk = _dot(kvn, wk_ref[...])
    vt = lax.dot_general(wv_ref[...], kvn, _NT, preferred_element_type=_F32)
    ones_tail = (lax.broadcasted_iota(jnp.int32, (HEAD_PAD - V_HEAD_DIM, tile), 0) == 0).astype(_F32)
    if meta:
        u_ref[...] = lax.dot_general(xn, wu_ref[...], _NT, preferred_element_type=_F32)
    else:
        ut = lax.dot_general(wu_ref[...], xn, _NT, preferred_element_type=_F32)
        for g in range(SSM_GROUPS):
            for c in range(tile // SLAB_T):
                u_ref[g, c] = ut[g * SSM_GROUP:(g + 1) * SSM_GROUP, c * SLAB_T:(c + 1) * SLAB_T]
        qn = _rms(proj[:, PW_Q:PW_KV], gq_ref[...]).astype(_BF)
        qt = lax.dot_general(wq_ref[...], qn, _NT, preferred_element_type=_F32)
        zero_rows = jnp.zeros((HEAD_PAD - QK_HEAD_DIM, tile), _F32)
    for h in range(N_HEADS):
        k_h = (kk[:, h * HEAD_PAD:(h + 1) * HEAD_PAD] + kr).astype(_BF)
        k_ref[h] = k_h
        v_ref[h] = jnp.concatenate([vt[h * V_HEAD_DIM:(h + 1) * V_HEAD_DIM], ones_tail], axis=0).astype(_BF)
        if not meta:
            blk = qt[h * QK_HEAD_DIM:(h + 1) * QK_HEAD_DIM]
            r1, r2 = _rotate(blk[QK_NOPE_DIM:QK_NOPE_DIM + half], blk[QK_NOPE_DIM + half:], cos_t, sin_t)
            qt_h = jnp.concatenate([blk[:QK_NOPE_DIM], r1, r2, zero_rows], axis=0).astype(_BF)
            q_ref[h] = qt_h
            qt_f = qt_h.astype(_F32)
            qn_ref[h] = jnp.sqrt(jnp.sum(qt_f * qt_f, axis=0, keepdims=True))
            k_f = k_h.astype(_F32)
            kmx_ref[h] = jnp.broadcast_to(jnp.max(jnp.sum(k_f * k_f, axis=1, keepdims=True), axis=0, keepdims=True),
                                          (1, LANES))


def _const_spec(shape):
    nd = len(shape)
    return pl.BlockSpec(shape, lambda *_: (0,) * nd)


def _proj_call(x, rope, gpre, win, wkr_t, wu_t, gq, wq_t, gkv, wk, wv_t, *, tile, meta):
    bsz, seq, _ = x.shape
    nt = seq // tile
    row_spec = lambda w: pl.BlockSpec((None, tile, w), lambda b, i: (b, i, 0))
    rope_spec = pl.BlockSpec((None, None, 2, QK_ROPE_DIM // 2, tile), lambda b, i: (b, i, 0, 0, 0))
    k_spec = pl.BlockSpec((None, N_HEADS, tile, HEAD_PAD), lambda b, i: (b, 0, i, 0))
    t_spec = pl.BlockSpec((None, N_HEADS, None, HEAD_PAD, tile), lambda b, i: (b, 0, i, 0, 0))
    k_shape = jax.ShapeDtypeStruct((bsz, N_HEADS, seq, HEAD_PAD), _BF)
    t_shape = jax.ShapeDtypeStruct((bsz, N_HEADS, nt, HEAD_PAD, tile), _BF)
    w_specs = lambda *ws: [_const_spec(w.shape) for w in ws]
    if meta:
        args = (x, rope, gpre, win, wkr_t, wu_t, gkv, wk, wv_t)
        in_specs = [row_spec(D_MODEL), rope_spec] + w_specs(*args[2:])
        out_specs = [k_spec, t_spec, row_spec(SSM_WIDTH)]
        out_shape = [k_shape, t_shape, jax.ShapeDtypeStruct((bsz, seq, SSM_WIDTH), _F32)]
    else:
        args = (x, rope, gpre, win, wkr_t, wu_t, gq, wq_t, gkv, wk, wv_t)
        in_specs = [row_spec(D_MODEL), rope_spec] + w_specs(*args[2:])
        norm_spec = lambda w: pl.BlockSpec((None, N_HEADS, None, 1, w), lambda b, i: (b, 0, i, 0, 0))
        u_spec = pl.BlockSpec((SSM_GROUPS, tile // SLAB_T, SSM_GROUP, SLAB_T), lambda b, i: (0, i, b, 0))
        out_specs = [t_spec, k_spec, t_spec, u_spec, norm_spec(tile), norm_spec(LANES)]
        out_shape = [t_shape, k_shape, t_shape,
                     jax.ShapeDtypeStruct((SSM_GROUPS, seq // SLAB_T, bsz * SSM_GROUP, SLAB_T), _F32),
                     jax.ShapeDtypeStruct((bsz, N_HEADS, nt, 1, tile), _F32),
                     jax.ShapeDtypeStruct((bsz, N_HEADS, nt, 1, LANES), _F32)]
    return pl.pallas_call(
        functools.partial(_proj_kernel, meta=meta),
        grid=(bsz, nt),
        in_specs=in_specs,
        out_specs=out_specs,
        out_shape=out_shape,
        compiler_params=pltpu.CompilerParams(
            dimension_semantics=("parallel", "parallel"), vmem_limit_bytes=VMEM_LIMIT),
        name="proj_meta" if meta else "proj",
    )(*args)


def _attn_finish(accs, o_ref):
    halves = [(acc * (1.0 / acc[ONES_LANE:ONES_LANE + 1, :]))[:V_HEAD_DIM] for acc in accs]
    o_ref[...] = jnp.concatenate(halves, axis=0).T


def _lane_concat(ref, hh):
    return jnp.concatenate([ref[hh, j] for j in range(ref.shape[1])], axis=1)


def _attn_bounded_kernel(qt_ref, mrow_ref, k_ref, vt_ref, km_ref, vtm_ref, o_ref, *, nk, tk):
    tq = o_ref.shape[0]
    per_slab = vt_ref.shape[3] // tk
    key_row = lax.broadcasted_iota(jnp.int32, (LANES, tq), 0)
    accs = []
    for hh in range(2):
        qt = _lane_concat(qt_ref, hh)
        mrow = _lane_concat(mrow_ref, hh)
        s0 = jnp.where(key_row < N_META, _dot(km_ref[hh], qt), -jnp.inf)
        acc = _dot(vtm_ref[hh, :V_ROWS, :], jnp.exp2(s0 - mrow).astype(_BF))
        scores = lambda c: _dot(k_ref[hh, c * tk:(c + 1) * tk, :], qt)
        s_next = scores(0)
        for c in range(nk):
            s = s_next
            if c + 1 < nk:
                s_next = scores(c + 1)
            vt_c = vt_ref[hh, c // per_slab, :V_ROWS, (c % per_slab) * tk:(c % per_slab + 1) * tk]
            acc = acc + _dot(vt_c, jnp.exp2(s - mrow).astype(_BF))
        accs.append(acc)
    _attn_finish(accs, o_ref)


def _attn_online_kernel(qt_ref, k_ref, vt_ref, km_ref, vtm_ref, o_ref, s0_scr, s1_scr, m_scr, acc_scr, *, nk, tk):
    tq = o_ref.shape[0]
    key_row = lax.broadcasted_iota(jnp.int32, (LANES, tq), 0)
    for hh in range(2):
        s0 = jnp.where(key_row < N_META, _dot(km_ref[hh], _lane_concat(qt_ref, hh)), -jnp.inf)
        m0 = jnp.max(s0, axis=0, keepdims=True)
        m_scr[hh] = m0
        acc_scr[hh] = _dot(vtm_ref[hh, :V_ROWS, :], jnp.exp2(s0 - m0).astype(_BF))

    def scores(buf, c):
        off = pl.multiple_of(c * tk, tk)
        for hh in range(2):
            buf[hh] = _dot(k_ref[hh, pl.ds(off, tk), :], _lane_concat(qt_ref, hh))

    def accumulate(buf, c):
        for hh in range(2):
            s = buf[hh]
            m = m_scr[hh]
            m_new = jnp.maximum(m, jnp.max(s, axis=0, keepdims=True))
            m_scr[hh] = m_new
            p = jnp.exp2(s - m_new).astype(_BF)
            acc_scr[hh] = jnp.exp2(m - m_new) * acc_scr[hh] + _dot(vt_ref[hh, c, :V_ROWS, :], p)

    scores(s0_scr, 0)

    def body(t, _):
        scores(s1_scr, 2 * t + 1)
        accumulate(s0_scr, 2 * t)
        scores(s0_scr, 2 * t + 2)
        accumulate(s1_scr, 2 * t + 1)
        return 0

    lax.fori_loop(0, nk // 2 - 1, body, 0)
    scores(s1_scr, nk - 1)
    accumulate(s0_scr, nk - 2)
    accumulate(s1_scr, nk - 1)
    _attn_finish([acc_scr[0], acc_scr[1]], o_ref)


def _attn_call(qt, mrow, k, vt, km, vtm, *, bounded):
    bsz, _, nslab, _, slab = qt.shape
    seq = k.shape[2]
    tq = min(ATTN_TQ, seq)
    tk = min(ATTN_TK, slab)
    qs = tq // slab
    assert tq % slab == 0 and seq % tq == 0 and slab % tk == 0
    q_spec = lambda rows: pl.BlockSpec((None, 2, qs, rows, slab), lambda b, hp, i: (b, hp, i, 0, 0))
    kv_specs = [pl.BlockSpec((None, 2, seq, HEAD_PAD), lambda b, hp, i: (b, hp, 0, 0)),
                pl.BlockSpec((None, 2, nslab, HEAD_PAD, slab), lambda b, hp, i: (b, hp, 0, 0, 0)),
                pl.BlockSpec((2, LANES, HEAD_PAD), lambda b, hp, i: (hp, 0, 0)),
                pl.BlockSpec((2, HEAD_PAD, LANES), lambda b, hp, i: (hp, 0, 0))]
    if bounded:
        body = functools.partial(_attn_bounded_kernel, nk=seq // tk, tk=tk)
        in_specs = [q_spec(HEAD_PAD), q_spec(1)] + kv_specs
        args, scratch = (qt, mrow, k, vt, km, vtm), []
    else:
        assert nslab % 2 == 0 and nslab >= 4
        body = functools.partial(_attn_online_kernel, nk=nslab, tk=slab)
        in_specs = [q_spec(HEAD_PAD)] + kv_specs
        args = (qt, k, vt, km, vtm)
        scratch = [pltpu.VMEM((2, slab, tq), _F32), pltpu.VMEM((2, slab, tq), _F32),
                   pltpu.VMEM((2, 1, tq), _F32), pltpu.VMEM((2, V_ROWS, tq), _F32)]
    return pl.pallas_call(
        body,
        grid=(bsz, N_HEADS // 2, seq // tq),
        in_specs=in_specs,
        out_specs=pl.BlockSpec((None, tq, LANES), lambda b, hp, i: (b, i, hp)),
        out_shape=jax.ShapeDtypeStruct((bsz, seq, ATTN_WIDTH), _F32),
        scratch_shapes=scratch,
        compiler_params=pltpu.CompilerParams(
            dimension_semantics=("parallel", "parallel", "arbitrary"), vmem_limit_bytes=VMEM_LIMIT),
        name="attn_bounded" if bounded else "attn_online",
    )(*args)


def _cmul_add(ar, ai, xr, xi, sr, si):
    return ar * xr - ai * xi + sr, ar * xi + ai * xr + si


def _s5_kernel(a_ref, um_ref, q_ref, bt_ref, c0_ref, d_ref, w_ref, v_ref, t_ref, y_ref, sup_scr, ent_scr,
               *, nslab, bsz):
    rows = nslab * bsz
    half = STATE_W // 2
    a = [a_ref[pl.ds(i, rows, stride=SSM_GROUP), :] for i in range(SSM_GROUP)]
    uc = [jnp.concatenate([x[:, c * SSM_CHUNK:(c + 1) * SSM_CHUNK] for x in a], axis=1).astype(_BF)
          for c in range(SLAB_CHUNKS)]
    w = w_ref[...]
    s = [_dot(u, w) for u in uc]
    sr = [x[:, :half] for x in s]
    si = [x[:, half:] for x in s]
    t = t_ref[...]
    trow = lambda r: t[r:r + 1, :]
    sup_r = sup_i = None
    for c in range(SLAB_CHUNKS):
        cr, ci = trow(c), trow(SLAB_CHUNKS + c)
        pr = cr * sr[c] - ci * si[c]
        pi = cr * si[c] + ci * sr[c]
        sup_r = pr if sup_r is None else sup_r + pr
        sup_i = pi if sup_i is None else sup_i + pi
    sup_scr[:, :half] = sup_r
    sup_scr[:, half:] = sup_i

    lane = lax.broadcasted_iota(jnp.int32, (bsz, half), 1)
    fwd = lane < SSM_STATE
    sm = _dot(um_ref[...], w)
    xr = jnp.where(fwd, sm[:bsz, :half], 0.0)
    xi = jnp.where(fwd, sm[:bsz, half:], 0.0)
    a_slab_r, a_slab_i = trow(2 * SLAB_CHUNKS), trow(2 * SLAB_CHUNKS + 1)
    for j in range(nslab):
        rf = j * bsz
        rb = (nslab - 1 - j) * bsz
        ent_scr[rf:rf + bsz, 0:SSM_STATE] = xr[:, 0:SSM_STATE]
        ent_scr[rb:rb + bsz, SSM_STATE:half] = xr[:, SSM_STATE:half]
        ent_scr[rf:rf + bsz, half:half + SSM_STATE] = xi[:, 0:SSM_STATE]
        ent_scr[rb:rb + bsz, half + SSM_STATE:STATE_W] = xi[:, SSM_STATE:half]
        s_r = jnp.where(fwd, sup_scr[rf:rf + bsz, :half], sup_scr[rb:rb + bsz, :half])
        s_i = jnp.where(fwd, sup_scr[rf:rf + bsz, half:], sup_scr[rb:rb + bsz, half:])
        xr, xi = _cmul_add(a_slab_r, a_slab_i, xr, xi, s_r, s_i)

    ent = ent_scr[...]
    a_r, a_i = trow(2 * SLAB_CHUNKS + 2), trow(2 * SLAB_CHUNKS + 3)
    xf = [(ent[:, :half], ent[:, half:])]
    for c in range(1, SLAB_CHUNKS):
        xf.append(_cmul_add(a_r, a_i, xf[-1][0], xf[-1][1], sr[c - 1], si[c - 1]))
    xb = [(ent[:, :half], ent[:, half:])]
    for c in range(SLAB_CHUNKS - 2, -1, -1):
        xb.insert(0, _cmul_add(a_r, a_i, xb[0][0], xb[0][1], sr[c + 1], si[c + 1]))
    fwd_rows = lax.broadcasted_iota(jnp.int32, (rows, half), 1) < SSM_STATE
    hi = lax.Precision.HIGHEST
    vf = v_ref[...]
    dirs = [jnp.concatenate([vf[d * SSM_STATE:(d + 1) * SSM_STATE],
                             vf[half + d * SSM_STATE:half + (d + 1) * SSM_STATE]], axis=0) for d in range(2)]
    lag_f = jnp.dot(bt_ref[0], dirs[0], precision=hi, preferred_element_type=_F32)
    lag_b = jnp.dot(bt_ref[1], dirs[1], precision=hi, preferred_element_type=_F32)
    center = (jnp.dot(bt_ref[0], c0_ref[0], precision=hi, preferred_element_type=_F32)
              + jnp.dot(bt_ref[1], c0_ref[1], precision=hi, preferred_element_type=_F32) + d_ref[...])
    lane = lax.broadcasted_iota(jnp.int32, (SSM_GROUP, CHUNK_W), 1)
    lag = jnp.concatenate([center, lag_f[:, :CHUNK_W - SSM_GROUP], jnp.where(lane < SSM_GROUP, 0.0, lag_b)], axis=1)
    m = jnp.concatenate(
        [pltpu.roll(jnp.broadcast_to(lag[i:i + 1, :], (SSM_CHUNK, 2 * CHUNK_W)), 0, 1,
                    stride=SSM_GROUP, stride_axis=0)[:, :CHUNK_W] for i in range(SSM_GROUP)], axis=0).astype(_BF)
    q = q_ref[...]
    m = _dot(m, q).astype(_BF)
    v = _dot(vf.astype(_BF), q).astype(_BF)
    ys = []
    for c in range(SLAB_CHUNKS):
        xin = jnp.concatenate([jnp.where(fwd_rows, xf[c][0], xb[c][0]),
                               jnp.where(fwd_rows, xf[c][1], xb[c][1])], axis=1).astype(_BF)
        ys.append(_dot(uc[c], m) + _dot(xin, v))
    for o in range(SSM_GROUP):
        y_ref[pl.ds(o, rows, stride=SSM_GROUP), :] = jnp.concatenate(
            [y[:, o * SSM_CHUNK:(o + 1) * SSM_CHUNK] for y in ys], axis=1)


def _s5_call(a, um, perm, bt, c0, dmat, w_mat, v_mat, tab, *, nslab, bsz):
    n = nslab * bsz * SSM_GROUP
    rows = nslab * bsz
    g_spec = lambda *shape: pl.BlockSpec((None,) + shape, lambda g: (g,) + (0,) * len(shape))
    perm_spec = pl.BlockSpec((CHUNK_W, CHUNK_W), lambda g: (0, 0), pipeline_mode=pl.Buffered(1))
    return pl.pallas_call(
        functools.partial(_s5_kernel, nslab=nslab, bsz=bsz),
        grid=(SSM_GROUPS,),
        in_specs=[g_spec(n, SLAB_T), g_spec(SUBLANES, CHUNK_W), perm_spec,
                  g_spec(2, SSM_GROUP, 2 * SSM_STATE), g_spec(2, 2 * SSM_STATE, SSM_GROUP), g_spec(SSM_GROUP, SSM_GROUP),
                  g_spec(CHUNK_W, STATE_W), g_spec(STATE_W, CHUNK_W),
                  g_spec(2 * SUBLANES, STATE_W // 2)],
        out_specs=g_spec(n, SLAB_T),
        out_shape=jax.ShapeDtypeStruct((SSM_GROUPS, n, SLAB_T), _F32),
        scratch_shapes=[pltpu.VMEM((rows, STATE_W), _F32), pltpu.VMEM((rows, STATE_W), _F32)],
        compiler_params=pltpu.CompilerParams(
            dimension_semantics=("parallel",), vmem_limit_bytes=VMEM_LIMIT),
        name="s5",
    )(a, um, perm, bt, c0, dmat, w_mat, v_mat, tab)


def _s5_matrices(a_re, a_im, log_dt, b_re, b_im, c_re, c_im, d_skip):
    tc = SSM_CHUNK
    lam = lax.complex(jnp.minimum(a_re.astype(_F32), -1e-4), a_im.astype(_F32))
    dt = jnp.exp(log_dt.astype(_F32))[..., None]
    lam_dt = lam * dt
    lam_bar = jnp.exp(lam_dt)
    b_bar = ((lam_bar - 1.0) / lam)[..., None] * lax.complex(b_re.astype(_F32), b_im.astype(_F32))
    c_c = lax.complex(c_re.astype(_F32), c_im.astype(_F32))
    k_idx = jnp.arange(tc + 1, dtype=_F32)
    pw = jnp.exp(lam_dt[:, :, None, :] * k_idx[None, None, :, None])
    bt = jnp.concatenate([jnp.real(b_bar), jnp.imag(b_bar)], axis=2).transpose(1, 0, 3, 2)
    c0 = jnp.concatenate([jnp.real(c_c), -jnp.imag(c_c)], axis=3).transpose(1, 0, 3, 2)
    dmat = jnp.eye(SSM_GROUP, dtype=_F32)[None] * d_skip.astype(_F32).reshape(SSM_GROUPS, 1, SSM_GROUP)
    wf = b_bar[0].transpose(0, 2, 1)[:, :, None, :] * pw[0][:, tc - 1::-1][:, None, :, :]
    wb = b_bar[1].transpose(0, 2, 1)[:, :, None, :] * pw[1][:, :tc][:, None, :, :]
    w_mat = jnp.concatenate([jnp.real(wf), jnp.real(wb), jnp.imag(wf), jnp.imag(wb)], axis=-1)
    w_mat = w_mat.reshape(SSM_GROUPS, CHUNK_W, STATE_W)
    gf = pw[0][:, 1:tc + 1][:, :, None, :] * c_c[0][:, None, :, :]
    gb = pw[1][:, tc:0:-1][:, :, None, :] * c_c[1][:, None, :, :]
    v_mat = jnp.concatenate([jnp.real(gf), jnp.real(gb), -jnp.imag(gf), -jnp.imag(gb)], axis=-1)
    v_mat = v_mat.reshape(SSM_GROUPS, CHUNK_W, STATE_W).transpose(0, 2, 1)
    n_idx = jnp.arange(SLAB_CHUNKS + 1, dtype=_F32) * tc
    pc = jnp.exp(lam_dt[:, :, None, :] * n_idx[None, None, :, None])
    coef = jnp.concatenate([pc[0][:, SLAB_CHUNKS - 1::-1], pc[1][:, :SLAB_CHUNKS]], axis=-1)
    both = lambda n: jnp.concatenate([pc[0][:, n], pc[1][:, n]], axis=-1)[:, None, :]
    a_slab, a_chunk = both(SLAB_CHUNKS), both(1)
    tab = jnp.concatenate([jnp.real(coef), jnp.imag(coef), jnp.real(a_slab), jnp.imag(a_slab),
                           jnp.real(a_chunk), jnp.imag(a_chunk)], axis=1)
    tab = jnp.pad(tab, ((0, 0), (0, 2 * SUBLANES - tab.shape[1]), (0, 0)))
    return bt, c0, dmat, w_mat, v_mat, tab


def _chunk_permutation():
    r = lax.broadcasted_iota(jnp.int32, (CHUNK_W, CHUNK_W), 0)
    c = lax.broadcasted_iota(jnp.int32, (CHUNK_W, CHUNK_W), 1)
    return (r == (c % SSM_CHUNK) * SSM_GROUP + c // SSM_CHUNK).astype(_BF)


def _post_kernel(x_ref, attn_ref, y_ref, wglu_ref, gmix_ref, wout_ref, gpm_ref, gpre_ref,
                 wup_ref, wdn_ref, gpost_ref, o_ref):
    nslab = y_ref.shape[1]
    gmix = gmix_ref[...]
    per = min(POST_SLABS, nslab)
    blocks = range(nslab // per)
    rows = [pl.ds(r * per * SLAB_T, per * SLAB_T) for r in blocks]
    gy = []
    for r in blocks:
        yt = jnp.concatenate([jnp.concatenate([y_ref[g, r * per + c] for c in range(per)], axis=1)
                              for g in range(SSM_GROUPS)], axis=0)
        y = yt.T
        gy.append((0.5 * y * (1.0 + jnp.tanh(math.sqrt(2.0 / math.pi) * (y + 0.044715 * (y * y * y))))).astype(_BF))
    z = [_dot(gy[r], wglu_ref[...]) for r in blocks]
    mix = []
    for r in blocks:
        ssm = z[r][:, :SSM_WIDTH] * (1.0 / (1.0 + jnp.exp(-z[r][:, SSM_WIDTH:])))
        mix.append(jnp.concatenate([_rms(attn_ref[rows[r], :], gmix[:, :ATTN_WIDTH]),
                                    _rms(ssm, gmix[:, ATTN_WIDTH:])], axis=-1).astype(_BF))
    mixed = [_dot(mix[r], wout_ref[...]) for r in blocks]
    h1 = [x_ref[rows[r], :] + _rms(mixed[r], gpm_ref[...]) for r in blocks]
    hn = [_rms(h1[r], gpre_ref[...]).astype(_BF) for r in blocks]
    acc = [None for _ in blocks]
    for c in range(D_FF // FF_TILE):
        for r in blocks:
            up = jnp.maximum(_dot(hn[r], wup_ref[:, c * FF_TILE:(c + 1) * FF_TILE]), 0.0)
            part = _dot((up * up).astype(_BF), wdn_ref[c * FF_TILE:(c + 1) * FF_TILE, :])
            acc[r] = part if acc[r] is None else acc[r] + part
    for r in blocks:
        o_ref[rows[r], :] = h1[r] + _rms(acc[r], gpost_ref[...])


def _post_call(x, attn, y, wglu, gmix, wout, gpm, gpre, wup, wdn, gpost, *, tile):
    bsz, seq, _ = x.shape
    row_spec = lambda w: pl.BlockSpec((None, tile, w), lambda b, i: (b, i, 0))
    wspec = lambda shape: pl.BlockSpec(shape, lambda b, i: (0, 0), pipeline_mode=pl.Buffered(1))
    y_spec = pl.BlockSpec((SSM_GROUPS, tile // SLAB_T, SSM_GROUP, SLAB_T), lambda b, i: (0, i, b, 0))
    return pl.pallas_call(
        _post_kernel,
        grid=(bsz, seq // tile),
        in_specs=[row_spec(D_MODEL), row_spec(ATTN_WIDTH), y_spec,
                  wspec((SSM_WIDTH, 2 * SSM_WIDTH)), wspec((1, D_MODEL)), wspec((D_MODEL, D_MODEL)),
                  wspec((1, D_MODEL)), wspec((1, D_MODEL)), wspec((D_MODEL, D_FF)),
                  wspec((D_FF, D_MODEL)), wspec((1, D_MODEL))],
        out_specs=row_spec(D_MODEL),
        out_shape=jax.ShapeDtypeStruct((bsz, seq, D_MODEL), _F32),
        compiler_params=pltpu.CompilerParams(
            dimension_semantics=("parallel", "parallel"), vmem_limit_bytes=VMEM_LIMIT),
        name="post",
    )(x, attn, y, wglu, gmix, wout, gpm, gpre, wup, wdn, gpost)


def _rope_tables(pos, tile):
    half = QK_ROPE_DIM // 2
    inv = 1.0 / (ROPE_BASE ** (jnp.arange(0, QK_ROPE_DIM, 2, dtype=_F32) / QK_ROPE_DIM))
    ang = pos.astype(_F32)[:, None, :] * inv[None, :, None]
    bsz, seq = pos.shape
    rope = jnp.stack([jnp.cos(ang), jnp.sin(ang)], axis=1)
    return rope.reshape(bsz, 2, half, seq // tile, tile).transpose(0, 3, 1, 2, 4)


def _prep_weights(w_in, w_uq, w_ukv):
    scale = QK_HEAD_DIM ** -0.5 * math.log2(math.e)
    win = w_in[:, :OFF_KR].astype(_BF)
    wkr_t = w_in[:, OFF_KR:OFF_U].T.astype(_BF)
    wu_t = w_in[:, OFF_U:].T.astype(_BF)
    wq_t = (w_uq * scale).T.astype(_BF)
    wkv3 = w_ukv.reshape(KV_LORA_RANK, N_HEADS, QK_NOPE_DIM + V_HEAD_DIM)
    wk = jnp.concatenate([wkv3[..., :QK_NOPE_DIM],
                          jnp.zeros((KV_LORA_RANK, N_HEADS, HEAD_PAD - QK_NOPE_DIM), _F32)], axis=-1)
    wk = wk.reshape(KV_LORA_RANK, N_HEADS * HEAD_PAD).astype(_BF)
    wv_t = wkv3[..., QK_NOPE_DIM:].reshape(KV_LORA_RANK, N_HEADS * V_HEAD_DIM).T.astype(_BF)
    return win, wkr_t, wu_t, wq_t, wk, wv_t


def kernel(x, positions, meta_tokens, g_pre_mix, w_in, g_q_lat, w_uq, g_kv_lat, w_ukv,
           ssm_A_re, ssm_A_im, ssm_log_dt, ssm_B_re, ssm_B_im, ssm_C_re, ssm_C_im, ssm_D,
           w_glu, g_mix_out, w_out, g_post_mix, g_pre_mlp, w_mlp_up, w_mlp_down, g_post_mlp):
    bsz, seq, _ = x.shape
    assert seq % ROW_TILE == 0 and ROW_TILE % SLAB_T == 0 and bsz <= SUBLANES
    assert seq % PROJ_TILE == 0 and PROJ_TILE % SLAB_T == 0
    assert N_META <= SSM_CHUNK
    row = lambda g: g.reshape(1, -1).astype(_F32)

    win, wkr_t, wu_t, wq_t, wk, wv_t = _prep_weights(w_in[0], w_uq[0], w_ukv[0])
    weights = (row(g_pre_mix[0]), win, wkr_t, wu_t, row(g_q_lat[0]), wq_t, row(g_kv_lat[0]), wk, wv_t)
    rope = _rope_tables(positions.astype(jnp.int32) + N_META, PROJ_TILE)
    qt, k, vt, u, q_norm, k_sq = _proj_call(x, rope, *weights, tile=PROJ_TILE, meta=False)
    meta_x = jnp.pad(meta_tokens.astype(x.dtype), ((0, LANES - N_META), (0, 0)))[None]
    rope_m = _rope_tables(jnp.arange(LANES, dtype=jnp.int32)[None], LANES)
    k_m, vt_m, u_m = _proj_call(meta_x, rope_m, *weights, tile=LANES, meta=True)
    km, vtm, u_m = k_m[0], vt_m[0, :, 0], u_m[:, :N_META]
    k_m_sq = jnp.max(jnp.sum(jnp.square(k_m[0].astype(_F32)), axis=-1), axis=-1)
    k_max = jnp.sqrt(jnp.maximum(jnp.max(k_sq, axis=(2, 3, 4)), k_m_sq[None]))
    mrow = q_norm * (k_max * BOUND_SLACK)[:, :, None, None, None]
    attn = lax.cond(jnp.max(mrow) <= BOUND_LIMIT,
                    lambda: _attn_call(qt, mrow, k, vt, km, vtm, bounded=True),
                    lambda: _attn_call(qt, mrow, k, vt, km, vtm, bounded=False))

    um = u_m[0].astype(_BF).reshape(N_META, SSM_GROUPS, SSM_GROUP).transpose(1, 2, 0)
    um = jnp.pad(um, ((0, 0), (0, 0), (SSM_CHUNK - N_META, 0))).reshape(SSM_GROUPS, 1, CHUNK_W)
    um = jnp.broadcast_to(um, (SSM_GROUPS, SUBLANES, CHUNK_W))
    bt, c0, dmat, w_mat, v_mat, tab = _s5_matrices(ssm_A_re[0], ssm_A_im[0], ssm_log_dt[0], ssm_B_re[0],
                                            ssm_B_im[0], ssm_C_re[0], ssm_C_im[0], ssm_D[0])
    nslab = seq // SLAB_T
    yg = _s5_call(u.reshape(SSM_GROUPS, nslab * bsz * SSM_GROUP, SLAB_T), um, _chunk_permutation(),
                  bt, c0, dmat, w_mat.astype(_BF), v_mat, tab, nslab=nslab, bsz=bsz)
    y = yg.reshape(SSM_GROUPS, nslab, bsz * SSM_GROUP, SLAB_T)

    return _post_call(x, attn, y, w_glu[0].astype(_BF), row(g_mix_out[0]), w_out[0].astype(_BF),
                      row(g_post_mix[0]), row(g_pre_mlp[0]), w_mlp_up[0].astype(_BF),
                      w_mlp_down[0].astype(_BF), row(g_post_mlp[0]), tile=ROW_TILE)
```

```python
import functools
import math

import jax
import jax.numpy as jnp
from jax import lax
from jax.experimental import pallas as pl
from jax.experimental.pallas import tpu as pltpu

D_MODEL = 1024
N_META = 16
ATTN_WIDTH = 512
SSM_WIDTH = 512
N_HEADS = 8
V_HEAD_DIM = 64
QK_NOPE_DIM = 64
QK_ROPE_DIM = 32
QK_HEAD_DIM = QK_NOPE_DIM + QK_ROPE_DIM
Q_LORA_RANK = 384
KV_LORA_RANK = 256
ROPE_BASE = 10000.0
SSM_GROUP = 16
SSM_GROUPS = 32
SSM_STATE = 64
D_FF = 4 * D_MODEL
EPS = 1e-6
OFF_KR = Q_LORA_RANK + KV_LORA_RANK
OFF_U = OFF_KR + QK_ROPE_DIM

LANES = 128
SUBLANES = 8
HEAD_PAD = LANES
ONES_LANE = V_HEAD_DIM
ATTN_TQ = 1024
ATTN_TK = 256
V_ROWS = 112
PW_Q = 0
PW_KV = PW_Q + Q_LORA_RANK
PW_END = PW_KV + KV_LORA_RANK

SSM_CHUNK = 32
CHUNK_W = SSM_CHUNK * SSM_GROUP
STATE_W = 4 * SSM_STATE
SLAB_T = LANES
SLAB_CHUNKS = SLAB_T // SSM_CHUNK

ROW_TILE = 512
PROJ_TILE = 1024
FF_TILE = 1024
POST_SLABS = 2
BOUND_SLACK = 1.0 + 2.0 ** -6
BOUND_LIMIT = 40.0
V7X_VMEM_BYTES = 64 * 1024 * 1024
VMEM_LIMIT = V7X_VMEM_BYTES - 8 * 1024 * 1024

_BF = jnp.bfloat16
_F32 = jnp.float32


def _dot(a, b):
    return jnp.dot(a, b, preferred_element_type=_F32)


def _rms(x, g):
    return x * lax.rsqrt(jnp.mean(x * x, axis=-1, keepdims=True) + EPS) * g


_NT = (((1,), (1,)), ((), ()))


def _rotate(x1, x2, cos_t, sin_t):
    return x1 * cos_t - x2 * sin_t, x1 * sin_t + x2 * cos_t


def _proj_kernel(x_ref, rope_ref, *refs, meta):
    if meta:
        gpre_ref, win_ref, wkr_ref, wu_ref, gkv_ref, wk_ref, wv_ref, k_ref, v_ref, u_ref = refs
    else:
        (gpre_ref, win_ref, wkr_ref, wu_ref, gq_ref, wq_ref, gkv_ref, wk_ref, wv_ref,
         q_ref, k_ref, v_ref, u_ref, qn_ref, qmx_ref, kmx_ref) = refs
    tile = x_ref.shape[0]
    half = QK_ROPE_DIM // 2
    cos_t, sin_t = rope_ref[0], rope_ref[1]
    xn = _rms(x_ref[...], gpre_ref[...]).astype(_BF)
    proj = _dot(xn, win_ref[...])
    kvn = _rms(proj[:, PW_KV:PW_END], gkv_ref[...]).astype(_BF)
    krt = lax.dot_general(wkr_ref[...], xn, _NT, preferred_element_type=_F32)
    r1, r2 = _rotate(krt[:half], krt[half:], cos_t, sin_t)
    kr = jnp.concatenate([jnp.zeros((QK_NOPE_DIM, tile), _F32), r1, r2,
                          jnp.zeros((HEAD_PAD - QK_HEAD_DIM, tile), _F32)], axis=0).T
    kk = _dot(kvn, wk_ref[...])
    vt = lax.dot_general(wv_ref[...], kvn, _NT, preferred_element_type=_F32)
    ones_tail = (lax.broadcasted_iota(jnp.int32, (HEAD_PAD - V_HEAD_DIM, tile), 0) == 0).astype(_F32)
    if meta:
        u_ref[...] = lax.dot_general(xn, wu_ref[...], _NT, preferred_element_type=_F32)
    else:
        ut = lax.dot_general(wu_ref[...], xn, _NT, preferred_element_type=_F32)
        for g in range(SSM_GROUPS):
            for c in range(tile // SLAB_T):
                u_ref[g, c] = ut[g * SSM_GROUP:(g + 1) * SSM_GROUP, c * SLAB_T:(c + 1) * SLAB_T]
        qn = _rms(proj[:, PW_Q:PW_KV], gq_ref[...]).astype(_BF)
        qt = lax.dot_general(wq_ref[...], qn, _NT, preferred_element_type=_F32)
        zero_rows = jnp.zeros((HEAD_PAD - QK_HEAD_DIM, tile), _F32)
    for h in range(N_HEADS):
        k_h = (kk[:, h * HEAD_PAD:(h + 1) * HEAD_PAD] + kr).astype(_BF)
        k_ref[h] = k_h
        v_ref[h] = jnp.concatenate([vt[h * V_HEAD_DIM:(h + 1) * V_HEAD_DIM], ones_tail], axis=0).astype(_BF)
        if not meta:
            blk = qt[h * QK_HEAD_DIM:(h + 1) * QK_HEAD_DIM]
            r1, r2 = _rotate(blk[QK_NOPE_DIM:QK_NOPE_DIM + half], blk[QK_NOPE_DIM + half:], cos_t, sin_t)
            qt_h = jnp.concatenate([blk[:QK_NOPE_DIM], r1, r2, zero_rows], axis=0).astype(_BF)
            q_ref[h] = qt_h
            qt_f = qt_h.astype(_F32)
            q_norm = jnp.sqrt(jnp.sum(qt_f * qt_f, axis=0, keepdims=True))
            qn_ref[h] = q_norm
            qmx_ref[h] = jnp.broadcast_to(jnp.max(q_norm, axis=1, keepdims=True), (1, LANES))
            k_f = k_h.astype(_F32)
            kmx_ref[h] = jnp.broadcast_to(jnp.max(jnp.sum(k_f * k_f, axis=1, keepdims=True), axis=0, keepdims=True),
                                          (1, LANES))


def _const_spec(shape):
    nd = len(shape)
    return pl.BlockSpec(shape, lambda *_: (0,) * nd)


def _proj_call(x, rope, gpre, win, wkr_t, wu_t, gq, wq_t, gkv, wk, wv_t, *, tile, meta):
    bsz, seq, _ = x.shape
    nt = seq // tile
    row_spec = lambda w: pl.BlockSpec((None, tile, w), lambda b, i: (b, i, 0))
    rope_spec = pl.BlockSpec((None, None, 2, QK_ROPE_DIM // 2, tile), lambda b, i: (b, i, 0, 0, 0))
    k_spec = pl.BlockSpec((None, N_HEADS, tile, HEAD_PAD), lambda b, i: (b, 0, i, 0))
    t_spec = pl.BlockSpec((None, N_HEADS, None, HEAD_PAD, tile), lambda b, i: (b, 0, i, 0, 0))
    k_shape = jax.ShapeDtypeStruct((bsz, N_HEADS, seq, HEAD_PAD), _BF)
    t_shape = jax.ShapeDtypeStruct((bsz, N_HEADS, nt, HEAD_PAD, tile), _BF)
    w_specs = lambda *ws: [_const_spec(w.shape) for w in ws]
    if meta:
        args = (x, rope, gpre, win, wkr_t, wu_t, gkv, wk, wv_t)
        in_specs = [row_spec(D_MODEL), rope_spec] + w_specs(*args[2:])
        out_specs = [k_spec, t_spec, row_spec(SSM_WIDTH)]
        out_shape = [k_shape, t_shape, jax.ShapeDtypeStruct((bsz, seq, SSM_WIDTH), _F32)]
    else:
        args = (x, rope, gpre, win, wkr_t, wu_t, gq, wq_t, gkv, wk, wv_t)
        in_specs = [row_spec(D_MODEL), rope_spec] + w_specs(*args[2:])
        norm_spec = lambda w: pl.BlockSpec((None, N_HEADS, None, 1, w), lambda b, i: (b, 0, i, 0, 0))
        u_spec = pl.BlockSpec((SSM_GROUPS, tile // SLAB_T, SSM_GROUP, SLAB_T), lambda b, i: (0, i, b, 0))
        out_specs = [t_spec, k_spec, t_spec, u_spec, norm_spec(tile), norm_spec(LANES), norm_spec(LANES)]
        out_shape = [t_shape, k_shape, t_shape,
                     jax.ShapeDtypeStruct((SSM_GROUPS, seq // SLAB_T, bsz * SSM_GROUP, SLAB_T), _F32),
                     jax.ShapeDtypeStruct((bsz, N_HEADS, nt, 1, tile), _F32),
                     jax.ShapeDtypeStruct((bsz, N_HEADS, nt, 1, LANES), _F32),
                     jax.ShapeDtypeStruct((bsz, N_HEADS, nt, 1, LANES), _F32)]
    return pl.pallas_call(
        functools.partial(_proj_kernel, meta=meta),
        grid=(bsz, nt),
        in_specs=in_specs,
        out_specs=out_specs,
        out_shape=out_shape,
        compiler_params=pltpu.CompilerParams(
            dimension_semantics=("parallel", "parallel"), vmem_limit_bytes=VMEM_LIMIT),
        name="proj_meta" if meta else "proj",
    )(*args)


def _attn_finish(accs, o_ref):
    halves = [(acc * (1.0 / acc[ONES_LANE:ONES_LANE + 1, :]))[:V_HEAD_DIM] for acc in accs]
    o_ref[...] = jnp.concatenate(halves, axis=0).T


def _lane_concat(ref, hh):
    return jnp.concatenate([ref[hh, j] for j in range(ref.shape[1])], axis=1)


def _attn_bounded_kernel(qt_ref, qn_ref, kb_ref, k_ref, vt_ref, km_ref, vtm_ref, o_ref, *, nk, tk):
    tq = o_ref.shape[0]
    per_slab = vt_ref.shape[3] // tk
    key_row = lax.broadcasted_iota(jnp.int32, (LANES, tq), 0)
    accs = []
    for hh in range(2):
        qt = _lane_concat(qt_ref, hh)
        mrow = _lane_concat(qn_ref, hh) * kb_ref[hh][:, 0:1]
        s0 = jnp.where(key_row < N_META, _dot(km_ref[hh], qt), -jnp.inf)
        acc = _dot(vtm_ref[hh, :V_ROWS, :], jnp.exp2(s0 - mrow).astype(_BF))
        scores = lambda c: _dot(k_ref[hh, c * tk:(c + 1) * tk, :], qt)
        s_next = scores(0)
        for c in range(nk):
            s = s_next
            if c + 1 < nk:
                s_next = scores(c + 1)
            vt_c = vt_ref[hh, c // per_slab, :V_ROWS, (c % per_slab) * tk:(c % per_slab + 1) * tk]
            acc = acc + _dot(vt_c, jnp.exp2(s - mrow).astype(_BF))
        accs.append(acc)
    _attn_finish(accs, o_ref)


def _attn_online_kernel(qt_ref, k_ref, vt_ref, km_ref, vtm_ref, o_ref, s0_scr, s1_scr, m_scr, acc_scr, *, nk, tk):
    tq = o_ref.shape[0]
    key_row = lax.broadcasted_iota(jnp.int32, (LANES, tq), 0)
    for hh in range(2):
        s0 = jnp.where(key_row < N_META, _dot(km_ref[hh], _lane_concat(qt_ref, hh)), -jnp.inf)
        m0 = jnp.max(s0, axis=0, keepdims=True)
        m_scr[hh] = m0
        acc_scr[hh] = _dot(vtm_ref[hh, :V_ROWS, :], jnp.exp2(s0 - m0).astype(_BF))

    def scores(buf, c):
        off = pl.multiple_of(c * tk, tk)
        for hh in range(2):
            buf[hh] = _dot(k_ref[hh, pl.ds(off, tk), :], _lane_concat(qt_ref, hh))

    def accumulate(buf, c):
        for hh in range(2):
            s = buf[hh]
            m = m_scr[hh]
            m_new = jnp.maximum(m, jnp.max(s, axis=0, keepdims=True))
            m_scr[hh] = m_new
            p = jnp.exp2(s - m_new).astype(_BF)
            acc_scr[hh] = jnp.exp2(m - m_new) * acc_scr[hh] + _dot(vt_ref[hh, c, :V_ROWS, :], p)

    scores(s0_scr, 0)

    def body(t, _):
        scores(s1_scr, 2 * t + 1)
        accumulate(s0_scr, 2 * t)
        scores(s0_scr, 2 * t + 2)
        accumulate(s1_scr, 2 * t + 1)
        return 0

    lax.fori_loop(0, nk // 2 - 1, body, 0)
    scores(s1_scr, nk - 1)
    accumulate(s0_scr, nk - 2)
    accumulate(s1_scr, nk - 1)
    _attn_finish([acc_scr[0], acc_scr[1]], o_ref)


def _attn_call(qt, q_norm, k_bound, k, vt, km, vtm, *, bounded):
    bsz, _, nslab, _, slab = qt.shape
    seq = k.shape[2]
    tq = min(ATTN_TQ, seq)
    tk = min(ATTN_TK, slab)
    qs = tq // slab
    assert tq % slab == 0 and seq % tq == 0 and slab % tk == 0
    q_spec = lambda rows: pl.BlockSpec((None, 2, qs, rows, slab), lambda b, hp, i: (b, hp, i, 0, 0))
    kv_specs = [pl.BlockSpec((None, 2, seq, HEAD_PAD), lambda b, hp, i: (b, hp, 0, 0)),
                pl.BlockSpec((None, 2, nslab, HEAD_PAD, slab), lambda b, hp, i: (b, hp, 0, 0, 0)),
                pl.BlockSpec((2, LANES, HEAD_PAD), lambda b, hp, i: (hp, 0, 0)),
                pl.BlockSpec((2, HEAD_PAD, LANES), lambda b, hp, i: (hp, 0, 0))]
    if bounded:
        body = functools.partial(_attn_bounded_kernel, nk=seq // tk, tk=tk)
        in_specs = [q_spec(HEAD_PAD), q_spec(1),
                    pl.BlockSpec((None, 2, 1, LANES), lambda b, hp, i: (b, hp, 0, 0))] + kv_specs
        args, scratch = (qt, q_norm, k_bound, k, vt, km, vtm), []
    else:
        assert nslab % 2 == 0 and nslab >= 4
        body = functools.partial(_attn_online_kernel, nk=nslab, tk=slab)
        in_specs = [q_spec(HEAD_PAD)] + kv_specs
        args = (qt, k, vt, km, vtm)
        scratch = [pltpu.VMEM((2, slab, tq), _F32), pltpu.VMEM((2, slab, tq), _F32),
                   pltpu.VMEM((2, 1, tq), _F32), pltpu.VMEM((2, V_ROWS, tq), _F32)]
    return pl.pallas_call(
        body,
        grid=(bsz, N_HEADS // 2, seq // tq),
        in_specs=in_specs,
        out_specs=pl.BlockSpec((None, tq, LANES), lambda b, hp, i: (b, i, hp)),
        out_shape=jax.ShapeDtypeStruct((bsz, seq, ATTN_WIDTH), _F32),
        scratch_shapes=scratch,
        compiler_params=pltpu.CompilerParams(
            dimension_semantics=("parallel", "parallel", "arbitrary"), vmem_limit_bytes=VMEM_LIMIT),
        name="attn_bounded" if bounded else "attn_online",
    )(*args)


def _cmul_add(ar, ai, xr, xi, sr, si):
    return ar * xr - ai * xi + sr, ar * xi + ai * xr + si


def _s5_kernel(a_ref, um_ref, q_ref, bt_ref, c0_ref, d_ref, w_ref, v_ref, t_ref, y_ref, sup_scr, ent_scr,
               *, nslab, bsz):
    rows = nslab * bsz
    half = STATE_W // 2
    a = [a_ref[pl.ds(i, rows, stride=SSM_GROUP), :] for i in range(SSM_GROUP)]
    uc = [jnp.concatenate([x[:, c * SSM_CHUNK:(c + 1) * SSM_CHUNK] for x in a], axis=1).astype(_BF)
          for c in range(SLAB_CHUNKS)]
    w = w_ref[...]
    s = [_dot(u, w) for u in uc]
    sr = [x[:, :half] for x in s]
    si = [x[:, half:] for x in s]
    t = t_ref[...]
    trow = lambda r: t[r:r + 1, :]
    sup_r = sup_i = None
    for c in range(SLAB_CHUNKS):
        cr, ci = trow(c), trow(SLAB_CHUNKS + c)
        pr = cr * sr[c] - ci * si[c]
        pi = cr * si[c] + ci * sr[c]
        sup_r = pr if sup_r is None else sup_r + pr
        sup_i = pi if sup_i is None else sup_i + pi
    sup_scr[:, :half] = sup_r
    sup_scr[:, half:] = sup_i

    lane = lax.broadcasted_iota(jnp.int32, (bsz, half), 1)
    fwd = lane < SSM_STATE
    sm = _dot(um_ref[...], w)
    xr = jnp.where(fwd, sm[:bsz, :half], 0.0)
    xi = jnp.where(fwd, sm[:bsz, half:], 0.0)
    a_slab_r, a_slab_i = trow(2 * SLAB_CHUNKS), trow(2 * SLAB_CHUNKS + 1)
    for j in range(nslab):
        rf = j * bsz
        rb = (nslab - 1 - j) * bsz
        ent_scr[rf:rf + bsz, 0:SSM_STATE] = xr[:, 0:SSM_STATE]
        ent_scr[rb:rb + bsz, SSM_STATE:half] = xr[:, SSM_STATE:half]
        ent_scr[rf:rf + bsz, half:half + SSM_STATE] = xi[:, 0:SSM_STATE]
        ent_scr[rb:rb + bsz, half + SSM_STATE:STATE_W] = xi[:, SSM_STATE:half]
        s_r = jnp.where(fwd, sup_scr[rf:rf + bsz, :half], sup_scr[rb:rb + bsz, :half])
        s_i = jnp.where(fwd, sup_scr[rf:rf + bsz, half:], sup_scr[rb:rb + bsz, half:])
        xr, xi = _cmul_add(a_slab_r, a_slab_i, xr, xi, s_r, s_i)

    ent = ent_scr[...]
    a_r, a_i = trow(2 * SLAB_CHUNKS + 2), trow(2 * SLAB_CHUNKS + 3)
    xf = [(ent[:, :half], ent[:, half:])]
    for c in range(1, SLAB_CHUNKS):
        xf.append(_cmul_add(a_r, a_i, xf[-1][0], xf[-1][1], sr[c - 1], si[c - 1]))
    xb = [(ent[:, :half], ent[:, half:])]
    for c in range(SLAB_CHUNKS - 2, -1, -1):
        xb.insert(0, _cmul_add(a_r, a_i, xb[0][0], xb[0][1], sr[c + 1], si[c + 1]))
    fwd_rows = lax.broadcasted_iota(jnp.int32, (rows, half), 1) < SSM_STATE
    hi = lax.Precision.HIGHEST
    vf = v_ref[...]
    dirs = [jnp.concatenate([vf[d * SSM_STATE:(d + 1) * SSM_STATE],
                             vf[half + d * SSM_STATE:half + (d + 1) * SSM_STATE]], axis=0) for d in range(2)]
    lag_f = jnp.dot(bt_ref[0], dirs[0], precision=hi, preferred_element_type=_F32)
    lag_b = jnp.dot(bt_ref[1], dirs[1], precision=hi, preferred_element_type=_F32)
    center = (jnp.dot(bt_ref[0], c0_ref[0], precision=hi, preferred_element_type=_F32)
              + jnp.dot(bt_ref[1], c0_ref[1], precision=hi, preferred_element_type=_F32) + d_ref[...])
    lane = lax.broadcasted_iota(jnp.int32, (SSM_GROUP, CHUNK_W), 1)
    lag = jnp.concatenate([center, lag_f[:, :CHUNK_W - SSM_GROUP], jnp.where(lane < SSM_GROUP, 0.0, lag_b)], axis=1)
    m = jnp.concatenate(
        [pltpu.roll(jnp.broadcast_to(lag[i:i + 1, :], (SSM_CHUNK, 2 * CHUNK_W)), 0, 1,
                    stride=SSM_GROUP, stride_axis=0)[:, :CHUNK_W] for i in range(SSM_GROUP)], axis=0).astype(_BF)
    q = q_ref[...]
    m = _dot(m, q).astype(_BF)
    v = _dot(vf.astype(_BF), q).astype(_BF)
    ys = []
    for c in range(SLAB_CHUNKS):
        xin = jnp.concatenate([jnp.where(fwd_rows, xf[c][0], xb[c][0]),
                               jnp.where(fwd_rows, xf[c][1], xb[c][1])], axis=1).astype(_BF)
        ys.append(_dot(uc[c], m) + _dot(xin, v))
    for o in range(SSM_GROUP):
        y_ref[pl.ds(o, rows, stride=SSM_GROUP), :] = jnp.concatenate(
            [y[:, o * SSM_CHUNK:(o + 1) * SSM_CHUNK] for y in ys], axis=1)


def _s5_call(a, um, perm, bt, c0, dmat, w_mat, v_mat, tab, *, nslab, bsz):
    n = nslab * bsz * SSM_GROUP
    rows = nslab * bsz
    g_spec = lambda *shape: pl.BlockSpec((None,) + shape, lambda g: (g,) + (0,) * len(shape))
    perm_spec = pl.BlockSpec((CHUNK_W, CHUNK_W), lambda g: (0, 0), pipeline_mode=pl.Buffered(1))
    return pl.pallas_call(
        functools.partial(_s5_kernel, nslab=nslab, bsz=bsz),
        grid=(SSM_GROUPS,),
        in_specs=[g_spec(n, SLAB_T), g_spec(SUBLANES, CHUNK_W), perm_spec,
                  g_spec(2, SSM_GROUP, 2 * SSM_STATE), g_spec(2, 2 * SSM_STATE, SSM_GROUP), g_spec(SSM_GROUP, SSM_GROUP),
                  g_spec(CHUNK_W, STATE_W), g_spec(STATE_W, CHUNK_W),
                  g_spec(2 * SUBLANES, STATE_W // 2)],
        out_specs=g_spec(n, SLAB_T),
        out_shape=jax.ShapeDtypeStruct((SSM_GROUPS, n, SLAB_T), _F32),
        scratch_shapes=[pltpu.VMEM((rows, STATE_W), _F32), pltpu.VMEM((rows, STATE_W), _F32)],
        compiler_params=pltpu.CompilerParams(
            dimension_semantics=("parallel",), vmem_limit_bytes=VMEM_LIMIT),
        name="s5",
    )(a, um, perm, bt, c0, dmat, w_mat, v_mat, tab)


def _s5_matrices(a_re, a_im, log_dt, b_re, b_im, c_re, c_im, d_skip):
    tc = SSM_CHUNK
    lam = lax.complex(jnp.minimum(a_re.astype(_F32), -1e-4), a_im.astype(_F32))
    dt = jnp.exp(log_dt.astype(_F32))[..., None]
    lam_dt = lam * dt
    lam_bar = jnp.exp(lam_dt)
    b_bar = ((lam_bar - 1.0) / lam)[..., None] * lax.complex(b_re.astype(_F32), b_im.astype(_F32))
    c_c = lax.complex(c_re.astype(_F32), c_im.astype(_F32))
    k_idx = jnp.arange(tc + 1, dtype=_F32)
    pw = jnp.exp(lam_dt[:, :, None, :] * k_idx[None, None, :, None])
    bt = jnp.concatenate([jnp.real(b_bar), jnp.imag(b_bar)], axis=2).transpose(1, 0, 3, 2)
    c0 = jnp.concatenate([jnp.real(c_c), -jnp.imag(c_c)], axis=3).transpose(1, 0, 3, 2)
    dmat = jnp.eye(SSM_GROUP, dtype=_F32)[None] * d_skip.astype(_F32).reshape(SSM_GROUPS, 1, SSM_GROUP)
    wf = b_bar[0].transpose(0, 2, 1)[:, :, None, :] * pw[0][:, tc - 1::-1][:, None, :, :]
    wb = b_bar[1].transpose(0, 2, 1)[:, :, None, :] * pw[1][:, :tc][:, None, :, :]
    w_mat = jnp.concatenate([jnp.real(wf), jnp.real(wb), jnp.imag(wf), jnp.imag(wb)], axis=-1)
    w_mat = w_mat.reshape(SSM_GROUPS, CHUNK_W, STATE_W)
    gf = pw[0][:, 1:tc + 1][:, :, None, :] * c_c[0][:, None, :, :]
    gb = pw[1][:, tc:0:-1][:, :, None, :] * c_c[1][:, None, :, :]
    v_mat = jnp.concatenate([jnp.real(gf), jnp.real(gb), -jnp.imag(gf), -jnp.imag(gb)], axis=-1)
    v_mat = v_mat.reshape(SSM_GROUPS, CHUNK_W, STATE_W).transpose(0, 2, 1)
    n_idx = jnp.arange(SLAB_CHUNKS + 1, dtype=_F32) * tc
    pc = jnp.exp(lam_dt[:, :, None, :] * n_idx[None, None, :, None])
    coef = jnp.concatenate([pc[0][:, SLAB_CHUNKS - 1::-1], pc[1][:, :SLAB_CHUNKS]], axis=-1)
    both = lambda n: jnp.concatenate([pc[0][:, n], pc[1][:, n]], axis=-1)[:, None, :]
    a_slab, a_chunk = both(SLAB_CHUNKS), both(1)
    tab = jnp.concatenate([jnp.real(coef), jnp.imag(coef), jnp.real(a_slab), jnp.imag(a_slab),
                           jnp.real(a_chunk), jnp.imag(a_chunk)], axis=1)
    tab = jnp.pad(tab, ((0, 0), (0, 2 * SUBLANES - tab.shape[1]), (0, 0)))
    return bt, c0, dmat, w_mat, v_mat, tab


def _chunk_permutation():
    r = lax.broadcasted_iota(jnp.int32, (CHUNK_W, CHUNK_W), 0)
    c = lax.broadcasted_iota(jnp.int32, (CHUNK_W, CHUNK_W), 1)
    return (r == (c % SSM_CHUNK) * SSM_GROUP + c // SSM_CHUNK).astype(_BF)


def _post_kernel(x_ref, attn_ref, y_ref, wglu_ref, gmix_ref, wout_ref, gpm_ref, gpre_ref,
                 wup_ref, wdn_ref, gpost_ref, o_ref):
    nslab = y_ref.shape[1]
    gmix = gmix_ref[...]
    per = min(POST_SLABS, nslab)
    blocks = range(nslab // per)
    rows = [pl.ds(r * per * SLAB_T, per * SLAB_T) for r in blocks]
    gy = []
    for r in blocks:
        yt = jnp.concatenate([jnp.concatenate([y_ref[g, r * per + c] for c in range(per)], axis=1)
                              for g in range(SSM_GROUPS)], axis=0)
        y = yt.T
        gy.append((0.5 * y * (1.0 + jnp.tanh(math.sqrt(2.0 / math.pi) * (y + 0.044715 * (y * y * y))))).astype(_BF))
    z = [_dot(gy[r], wglu_ref[...]) for r in blocks]
    mix = []
    for r in blocks:
        ssm = z[r][:, :SSM_WIDTH] * (1.0 / (1.0 + jnp.exp(-z[r][:, SSM_WIDTH:])))
        mix.append(jnp.concatenate([_rms(attn_ref[rows[r], :], gmix[:, :ATTN_WIDTH]),
                                    _rms(ssm, gmix[:, ATTN_WIDTH:])], axis=-1).astype(_BF))
    mixed = [_dot(mix[r], wout_ref[...]) for r in blocks]
    h1 = [x_ref[rows[r], :] + _rms(mixed[r], gpm_ref[...]) for r in blocks]
    hn = [_rms(h1[r], gpre_ref[...]).astype(_BF) for r in blocks]
    acc = [None for _ in blocks]
    for c in range(D_FF // FF_TILE):
        for r in blocks:
            up = jnp.maximum(_dot(hn[r], wup_ref[:, c * FF_TILE:(c + 1) * FF_TILE]), 0.0)
            part = _dot((up * up).astype(_BF), wdn_ref[c * FF_TILE:(c + 1) * FF_TILE, :])
            acc[r] = part if acc[r] is None else acc[r] + part
    for r in blocks:
        o_ref[rows[r], :] = h1[r] + _rms(acc[r], gpost_ref[...])


def _post_call(x, attn, y, wglu, gmix, wout, gpm, gpre, wup, wdn, gpost, *, tile):
    bsz, seq, _ = x.shape
    row_spec = lambda w: pl.BlockSpec((None, tile, w), lambda b, i: (b, i, 0))
    wspec = lambda shape: pl.BlockSpec(shape, lambda b, i: (0, 0), pipeline_mode=pl.Buffered(1))
    y_spec = pl.BlockSpec((SSM_GROUPS, tile // SLAB_T, SSM_GROUP, SLAB_T), lambda b, i: (0, i, b, 0))
    return pl.pallas_call(
        _post_kernel,
        grid=(bsz, seq // tile),
        in_specs=[row_spec(D_MODEL), row_spec(ATTN_WIDTH), y_spec,
                  wspec((SSM_WIDTH, 2 * SSM_WIDTH)), wspec((1, D_MODEL)), wspec((D_MODEL, D_MODEL)),
                  wspec((1, D_MODEL)), wspec((1, D_MODEL)), wspec((D_MODEL, D_FF)),
                  wspec((D_FF, D_MODEL)), wspec((1, D_MODEL))],
        out_specs=row_spec(D_MODEL),
        out_shape=jax.ShapeDtypeStruct((bsz, seq, D_MODEL), _F32),
        compiler_params=pltpu.CompilerParams(
            dimension_semantics=("parallel", "parallel"), vmem_limit_bytes=VMEM_LIMIT),
        name="post",
    )(x, attn, y, wglu, gmix, wout, gpm, gpre, wup, wdn, gpost)


def _rope_tables(pos, tile):
    half = QK_ROPE_DIM // 2
    inv = 1.0 / (ROPE_BASE ** (jnp.arange(0, QK_ROPE_DIM, 2, dtype=_F32) / QK_ROPE_DIM))
    ang = pos.astype(_F32)[:, None, :] * inv[None, :, None]
    bsz, seq = pos.shape
    rope = jnp.stack([jnp.cos(ang), jnp.sin(ang)], axis=1)
    return rope.reshape(bsz, 2, half, seq // tile, tile).transpose(0, 3, 1, 2, 4)


def _prep_weights(w_in, w_uq, w_ukv):
    scale = QK_HEAD_DIM ** -0.5 * math.log2(math.e)
    win = w_in[:, :OFF_KR].astype(_BF)
    wkr_t = w_in[:, OFF_KR:OFF_U].T.astype(_BF)
    wu_t = w_in[:, OFF_U:].T.astype(_BF)
    wq_t = (w_uq * scale).T.astype(_BF)
    wkv3 = w_ukv.reshape(KV_LORA_RANK, N_HEADS, QK_NOPE_DIM + V_HEAD_DIM)
    wk = jnp.concatenate([wkv3[..., :QK_NOPE_DIM],
                          jnp.zeros((KV_LORA_RANK, N_HEADS, HEAD_PAD - QK_NOPE_DIM), _F32)], axis=-1)
    wk = wk.reshape(KV_LORA_RANK, N_HEADS * HEAD_PAD).astype(_BF)
    wv_t = wkv3[..., QK_NOPE_DIM:].reshape(KV_LORA_RANK, N_HEADS * V_HEAD_DIM).T.astype(_BF)
    return win, wkr_t, wu_t, wq_t, wk, wv_t


def kernel(x, positions, meta_tokens, g_pre_mix, w_in, g_q_lat, w_uq, g_kv_lat, w_ukv,
           ssm_A_re, ssm_A_im, ssm_log_dt, ssm_B_re, ssm_B_im, ssm_C_re, ssm_C_im, ssm_D,
           w_glu, g_mix_out, w_out, g_post_mix, g_pre_mlp, w_mlp_up, w_mlp_down, g_post_mlp):
    bsz, seq, _ = x.shape
    assert seq % ROW_TILE == 0 and ROW_TILE % SLAB_T == 0 and bsz <= SUBLANES
    assert seq % PROJ_TILE == 0 and PROJ_TILE % SLAB_T == 0
    assert N_META <= SSM_CHUNK
    row = lambda g: g.reshape(1, -1).astype(_F32)

    win, wkr_t, wu_t, wq_t, wk, wv_t = _prep_weights(w_in[0], w_uq[0], w_ukv[0])
    weights = (row(g_pre_mix[0]), win, wkr_t, wu_t, row(g_q_lat[0]), wq_t, row(g_kv_lat[0]), wk, wv_t)
    rope = _rope_tables(positions.astype(jnp.int32) + N_META, PROJ_TILE)
    qt, k, vt, u, q_norm, q_mx, k_sq = _proj_call(x, rope, *weights, tile=PROJ_TILE, meta=False)
    meta_x = jnp.pad(meta_tokens.astype(x.dtype), ((0, LANES - N_META), (0, 0)))[None]
    rope_m = _rope_tables(jnp.arange(LANES, dtype=jnp.int32)[None], LANES)
    k_m, vt_m, u_m = _proj_call(meta_x, rope_m, *weights, tile=LANES, meta=True)
    km, vtm, u_m = k_m[0], vt_m[0, :, 0], u_m[:, :N_META]
    k_m_sq = jnp.max(jnp.sum(jnp.square(k_m[0].astype(_F32)), axis=-1), axis=-1)
    k_bound = jnp.sqrt(jnp.maximum(jnp.max(k_sq, axis=(2, 3, 4)), k_m_sq[None])) * BOUND_SLACK
    safe = jnp.max(jnp.max(q_mx, axis=(2, 3, 4)) * k_bound) <= BOUND_LIMIT
    k_bound = jnp.broadcast_to(k_bound[:, :, None, None], k_bound.shape + (1, LANES))
    attn = lax.cond(safe,
                    lambda: _attn_call(qt, q_norm, k_bound, k, vt, km, vtm, bounded=True),
                    lambda: _attn_call(qt, q_norm, k_bound, k, vt, km, vtm, bounded=False))

    um = u_m[0].astype(_BF).reshape(N_META, SSM_GROUPS, SSM_GROUP).transpose(1, 2, 0)
    um = jnp.pad(um, ((0, 0), (0, 0), (SSM_CHUNK - N_META, 0))).reshape(SSM_GROUPS, 1, CHUNK_W)
    um = jnp.broadcast_to(um, (SSM_GROUPS, SUBLANES, CHUNK_W))
    bt, c0, dmat, w_mat, v_mat, tab = _s5_matrices(ssm_A_re[0], ssm_A_im[0], ssm_log_dt[0], ssm_B_re[0],
                                            ssm_B_im[0], ssm_C_re[0], ssm_C_im[0], ssm_D[0])
    nslab = seq // SLAB_T
    yg = _s5_call(u.reshape(SSM_GROUPS, nslab * bsz * SSM_GROUP, SLAB_T), um, _chunk_permutation(),
                  bt, c0, dmat, w_mat.astype(_BF), v_mat, tab, nslab=nslab, bsz=bsz)
    y = yg.reshape(SSM_GROUPS, nslab, bsz * SSM_GROUP, SLAB_T)

    return _post_call(x, attn, y, w_glu[0].astype(_BF), row(g_mix_out[0]), w_out[0].astype(_BF),
                      row(g_post_mix[0]), row(g_pre_mlp[0]), w_mlp_up[0].astype(_BF),
                      w_mlp_down[0].astype(_BF), row(g_post_mlp[0]), tile=ROW_TILE)
```

```python
import functools
import math

import jax
import jax.numpy as jnp
from jax import lax
from jax.experimental import pallas as pl
from jax.experimental.pallas import tpu as pltpu

D_MODEL = 1024
N_META = 16
ATTN_WIDTH = 512
SSM_WIDTH = 512
N_HEADS = 8
V_HEAD_DIM = 64
QK_NOPE_DIM = 64
QK_ROPE_DIM = 32
QK_HEAD_DIM = QK_NOPE_DIM + QK_ROPE_DIM
Q_LORA_RANK = 384
KV_LORA_RANK = 256
ROPE_BASE = 10000.0
SSM_GROUP = 16
SSM_GROUPS = 32
SSM_STATE = 64
D_FF = 4 * D_MODEL
EPS = 1e-6
OFF_KR = Q_LORA_RANK + KV_LORA_RANK
OFF_U = OFF_KR + QK_ROPE_DIM

LANES = 128
SUBLANES = 8
HEAD_PAD = LANES
ONES_LANE = V_HEAD_DIM
ATTN_TQ = 1024
ATTN_TK = 256
V_ROWS = 112
PW_Q = 0
PW_KV = PW_Q + Q_LORA_RANK
PW_END = PW_KV + KV_LORA_RANK

SSM_CHUNK = 32
CHUNK_W = SSM_CHUNK * SSM_GROUP
STATE_W = 4 * SSM_STATE
SLAB_T = LANES
SLAB_CHUNKS = SLAB_T // SSM_CHUNK

ROW_TILE = 512
PROJ_TILE = 1024
FF_TILE = 1024
POST_SLABS = 2
BOUND_SLACK = 1.0 + 2.0 ** -6
BOUND_LIMIT = 40.0
V7X_VMEM_BYTES = 64 * 1024 * 1024
VMEM_LIMIT = V7X_VMEM_BYTES - 8 * 1024 * 1024

_BF = jnp.bfloat16
_F32 = jnp.float32


def _dot(a, b):
    return jnp.dot(a, b, preferred_element_type=_F32)


def _rms(x, g):
    return x * lax.rsqrt(jnp.mean(x * x, axis=-1, keepdims=True) + EPS) * g


_NT = (((1,), (1,)), ((), ()))


def _rotate(x1, x2, cos_t, sin_t):
    return x1 * cos_t - x2 * sin_t, x1 * sin_t + x2 * cos_t


def _proj_kernel(x_ref, rope_ref, *refs, meta):
    if meta:
        gpre_ref, win_ref, wkr_ref, wu_ref, gkv_ref, wk_ref, wv_ref, k_ref, v_ref, u_ref = refs
    else:
        (gpre_ref, win_ref, wkr_ref, wu_ref, gq_ref, wq_ref, gkv_ref, wk_ref, wv_ref,
         q_ref, k_ref, v_ref, u_ref, qn_ref, kmx_ref) = refs
    tile = x_ref.shape[0]
    half = QK_ROPE_DIM // 2
    cos_t, sin_t = rope_ref[0], rope_ref[1]
    xn = _rms(x_ref[...], gpre_ref[...]).astype(_BF)
    proj = _dot(xn, win_ref[...])
    kvn = _rms(proj[:, PW_KV:PW_END], gkv_ref[...]).astype(_BF)
    krt = lax.dot_general(wkr_ref[...], xn, _NT, preferred_element_type=_F32)
    r1, r2 = _rotate(krt[:half], krt[half:], cos_t, sin_t)
    kr = jnp.concatenate([jnp.zeros((QK_NOPE_DIM, tile), _F32), r1, r2,
                          jnp.zeros((HEAD_PAD - QK_HEAD_DIM, tile), _F32)], axis=0).T
    kk = _dot(kvn, wk_ref[...])
    vt = lax.dot_general(wv_ref[...], kvn, _NT, preferred_element_type=_F32)
    ones_tail = (lax.broadcasted_iota(jnp.int32, (HEAD_PAD - V_HEAD_DIM, tile), 0) == 0).astype(_F32)
    if meta:
        u_ref[...] = lax.dot_general(xn, wu_ref[...], _NT, preferred_element_type=_F32)
    else:
        ut = lax.dot_general(wu_ref[...], xn, _NT, preferred_element_type=_F32)
        for g in range(SSM_GROUPS):
            for c in range(tile // SLAB_T):
                u_ref[g, c] = ut[g * SSM_GROUP:(g + 1) * SSM_GROUP, c * SLAB_T:(c + 1) * SLAB_T]
        qn = _rms(proj[:, PW_Q:PW_KV], gq_ref[...]).astype(_BF)
        qt = lax.dot_general(wq_ref[...], qn, _NT, preferred_element_type=_F32)
        zero_rows = jnp.zeros((HEAD_PAD - QK_HEAD_DIM, tile), _F32)
    for h in range(N_HEADS):
        k_h = (kk[:, h * HEAD_PAD:(h + 1) * HEAD_PAD] + kr).astype(_BF)
        k_ref[h] = k_h
        v_ref[h] = jnp.concatenate([vt[h * V_HEAD_DIM:(h + 1) * V_HEAD_DIM], ones_tail], axis=0).astype(_BF)
        if not meta:
            blk = qt[h * QK_HEAD_DIM:(h + 1) * QK_HEAD_DIM]
            r1, r2 = _rotate(blk[QK_NOPE_DIM:QK_NOPE_DIM + half], blk[QK_NOPE_DIM + half:], cos_t, sin_t)
            qt_h = jnp.concatenate([blk[:QK_NOPE_DIM], r1, r2, zero_rows], axis=0).astype(_BF)
            q_ref[h] = qt_h
            qt_f = qt_h.astype(_F32)
            qn_ref[h] = jnp.sqrt(jnp.sum(qt_f * qt_f, axis=0, keepdims=True))
            k_f = k_h.astype(_F32)
            kmx_ref[h] = jnp.broadcast_to(jnp.max(jnp.sum(k_f * k_f, axis=1, keepdims=True), axis=0, keepdims=True),
                                          (1, LANES))


def _const_spec(shape):
    nd = len(shape)
    return pl.BlockSpec(shape, lambda *_: (0,) * nd)


def _proj_call(x, rope, gpre, win, wkr_t, wu_t, gq, wq_t, gkv, wk, wv_t, *, tile, meta):
    bsz, seq, _ = x.shape
    nt = seq // tile
    row_spec = lambda w: pl.BlockSpec((None, tile, w), lambda b, i: (b, i, 0))
    rope_spec = pl.BlockSpec((None, None, 2, QK_ROPE_DIM // 2, tile), lambda b, i: (b, i, 0, 0, 0))
    k_spec = pl.BlockSpec((None, N_HEADS, tile, HEAD_PAD), lambda b, i: (b, 0, i, 0))
    t_spec = pl.BlockSpec((None, N_HEADS, None, HEAD_PAD, tile), lambda b, i: (b, 0, i, 0, 0))
    k_shape = jax.ShapeDtypeStruct((bsz, N_HEADS, seq, HEAD_PAD), _BF)
    t_shape = jax.ShapeDtypeStruct((bsz, N_HEADS, nt, HEAD_PAD, tile), _BF)
    w_specs = lambda *ws: [_const_spec(w.shape) for w in ws]
    if meta:
        args = (x, rope, gpre, win, wkr_t, wu_t, gkv, wk, wv_t)
        in_specs = [row_spec(D_MODEL), rope_spec] + w_specs(*args[2:])
        out_specs = [k_spec, t_spec, row_spec(SSM_WIDTH)]
        out_shape = [k_shape, t_shape, jax.ShapeDtypeStruct((bsz, seq, SSM_WIDTH), _F32)]
    else:
        args = (x, rope, gpre, win, wkr_t, wu_t, gq, wq_t, gkv, wk, wv_t)
        in_specs = [row_spec(D_MODEL), rope_spec] + w_specs(*args[2:])
        norm_spec = lambda w: pl.BlockSpec((None, N_HEADS, None, 1, w), lambda b, i: (b, 0, i, 0, 0))
        u_spec = pl.BlockSpec((SSM_GROUPS, tile // SLAB_T, SSM_GROUP, SLAB_T), lambda b, i: (0, i, b, 0))
        out_specs = [t_spec, k_spec, t_spec, u_spec, norm_spec(tile), norm_spec(LANES)]
        out_shape = [t_shape, k_shape, t_shape,
                     jax.ShapeDtypeStruct((SSM_GROUPS, seq // SLAB_T, bsz * SSM_GROUP, SLAB_T), _F32),
                     jax.ShapeDtypeStruct((bsz, N_HEADS, nt, 1, tile), _F32),
                     jax.ShapeDtypeStruct((bsz, N_HEADS, nt, 1, LANES), _F32)]
    return pl.pallas_call(
        functools.partial(_proj_kernel, meta=meta),
        grid=(bsz, nt),
        in_specs=in_specs,
        out_specs=out_specs,
        out_shape=out_shape,
        compiler_params=pltpu.CompilerParams(
            dimension_semantics=("parallel", "parallel"), vmem_limit_bytes=VMEM_LIMIT),
        name="proj_meta" if meta else "proj",
    )(*args)


def _attn_finish(accs, o_ref):
    halves = [(acc * (1.0 / acc[ONES_LANE:ONES_LANE + 1, :]))[:V_HEAD_DIM] for acc in accs]
    o_ref[...] = jnp.concatenate(halves, axis=0).T


def _lane_concat(ref, hh):
    return jnp.concatenate([ref[hh, j] for j in range(ref.shape[1])], axis=1)


def _attn_bounded_kernel(qt_ref, mrow_ref, k_ref, vt_ref, km_ref, vtm_ref, o_ref, *, nk, tk):
    tq = o_ref.shape[0]
    per_slab = vt_ref.shape[3] // tk
    key_row = lax.broadcasted_iota(jnp.int32, (LANES, tq), 0)
    accs = []
    for hh in range(2):
        qt = _lane_concat(qt_ref, hh)
        mrow = _lane_concat(mrow_ref, hh)
        s0 = jnp.where(key_row < N_META, _dot(km_ref[hh], qt), -jnp.inf)
        acc = _dot(vtm_ref[hh, :V_ROWS, :], jnp.exp2(s0 - mrow).astype(_BF))
        scores = lambda c: _dot(k_ref[hh, c * tk:(c + 1) * tk, :], qt)
        s_next = scores(0)
        for c in range(nk):
            s = s_next
            if c + 1 < nk:
                s_next = scores(c + 1)
            vt_c = vt_ref[hh, c // per_slab, :V_ROWS, (c % per_slab) * tk:(c % per_slab + 1) * tk]
            acc = acc + _dot(vt_c, jnp.exp2(s - mrow).astype(_BF))
        accs.append(acc)
    _attn_finish(accs, o_ref)


def _attn_online_kernel(qt_ref, k_ref, vt_ref, km_ref, vtm_ref, o_ref, s0_scr, s1_scr, m_scr, acc_scr, *, nk, tk):
    tq = o_ref.shape[0]
    key_row = lax.broadcasted_iota(jnp.int32, (LANES, tq), 0)
    for hh in range(2):
        s0 = jnp.where(key_row < N_META, _dot(km_ref[hh], _lane_concat(qt_ref, hh)), -jnp.inf)
        m0 = jnp.max(s0, axis=0, keepdims=True)
        m_scr[hh] = m0
        acc_scr[hh] = _dot(vtm_ref[hh, :V_ROWS, :], jnp.exp2(s0 - m0).astype(_BF))

    def scores(buf, c):
        off = pl.multiple_of(c * tk, tk)
        for hh in range(2):
            buf[hh] = _dot(k_ref[hh, pl.ds(off, tk), :], _lane_concat(qt_ref, hh))

    def accumulate(buf, c):
        for hh in range(2):
            s = buf[hh]
            m = m_scr[hh]
            m_new = jnp.maximum(m, jnp.max(s, axis=0, keepdims=True))
            m_scr[hh] = m_new
            p = jnp.exp2(s - m_new).astype(_BF)
            acc_scr[hh] = jnp.exp2(m - m_new) * acc_scr[hh] + _dot(vt_ref[hh, c, :V_ROWS, :], p)

    scores(s0_scr, 0)

    def body(t, _):
        scores(s1_scr, 2 * t + 1)
        accumulate(s0_scr, 2 * t)
        scores(s0_scr, 2 * t + 2)
        accumulate(s1_scr, 2 * t + 1)
        return 0

    lax.fori_loop(0, nk // 2 - 1, body, 0)
    scores(s1_scr, nk - 1)
    accumulate(s0_scr, nk - 2)
    accumulate(s1_scr, nk - 1)
    _attn_finish([acc_scr[0], acc_scr[1]], o_ref)


def _attn_call(qt, mrow, k, vt, km, vtm, *, bounded):
    bsz, _, nslab, _, slab = qt.shape
    seq = k.shape[2]
    tq = min(ATTN_TQ, seq)
    tk = min(ATTN_TK, slab)
    qs = tq // slab
    assert tq % slab == 0 and seq % tq == 0 and slab % tk == 0
    q_spec = lambda rows: pl.BlockSpec((None, 2, qs, rows, slab), lambda b, hp, i: (b, hp, i, 0, 0))
    kv_specs = [pl.BlockSpec((None, 2, seq, HEAD_PAD), lambda b, hp, i: (b, hp, 0, 0)),
                pl.BlockSpec((None, 2, nslab, HEAD_PAD, slab), lambda b, hp, i: (b, hp, 0, 0, 0)),
                pl.BlockSpec((2, LANES, HEAD_PAD), lambda b, hp, i: (hp, 0, 0)),
                pl.BlockSpec((2, HEAD_PAD, LANES), lambda b, hp, i: (hp, 0, 0))]
    if bounded:
        body = functools.partial(_attn_bounded_kernel, nk=seq // tk, tk=tk)
        in_specs = [q_spec(HEAD_PAD), q_spec(1)] + kv_specs
        args, scratch = (qt, mrow, k, vt, km, vtm), []
    else:
        assert nslab % 2 == 0 and nslab >= 4
        body = functools.partial(_attn_online_kernel, nk=nslab, tk=slab)
        in_specs = [q_spec(HEAD_PAD)] + kv_specs
        args = (qt, k, vt, km, vtm)
        scratch = [pltpu.VMEM((2, slab, tq), _F32), pltpu.VMEM((2, slab, tq), _F32),
                   pltpu.VMEM((2, 1, tq), _F32), pltpu.VMEM((2, V_ROWS, tq), _F32)]
    return pl.pallas_call(
        body,
        grid=(bsz, N_HEADS // 2, seq // tq),
        in_specs=in_specs,
        out_specs=pl.BlockSpec((None, tq, LANES), lambda b, hp, i: (b, i, hp)),
        out_shape=jax.ShapeDtypeStruct((bsz, seq, ATTN_WIDTH), _F32),
        scratch_shapes=scratch,
        compiler_params=pltpu.CompilerParams(
            dimension_semantics=("parallel", "parallel", "arbitrary"), vmem_limit_bytes=VMEM_LIMIT),
        name="attn_bounded" if bounded else "attn_online",
    )(*args)


def _cmul_add(ar, ai, xr, xi, sr, si):
    return ar * xr - ai * xi + sr, ar * xi + ai * xr + si


def _s5_kernel(a_ref, um_ref, q_ref, bt_ref, c0_ref, d_ref, w_ref, v_ref, t_ref, y_ref, sup_scr, ent_scr,
               *, nslab, bsz):
    rows = nslab * bsz
    half = STATE_W // 2
    a = [a_ref[pl.ds(i, rows, stride=SSM_GROUP), :] for i in range(SSM_GROUP)]
    uc = [jnp.concatenate([x[:, c * SSM_CHUNK:(c + 1) * SSM_CHUNK] for x in a], axis=1).astype(_BF)
          for c in range(SLAB_CHUNKS)]
    w = w_ref[...]
    s = [_dot(u, w) for u in uc]
    sr = [x[:, :half] for x in s]
    si = [x[:, half:] for x in s]
    t = t_ref[...]
    trow = lambda r: t[r:r + 1, :]
    sup_r = sup_i = None
    for c in range(SLAB_CHUNKS):
        cr, ci = trow(c), trow(SLAB_CHUNKS + c)
        pr = cr * sr[c] - ci * si[c]
        pi = cr * si[c] + ci * sr[c]
        sup_r = pr if sup_r is None else sup_r + pr
        sup_i = pi if sup_i is None else sup_i + pi
    sup_scr[:, :half] = sup_r
    sup_scr[:, half:] = sup_i

    lane = lax.broadcasted_iota(jnp.int32, (bsz, half), 1)
    fwd = lane < SSM_STATE
    sm = _dot(um_ref[...], w)
    xr = jnp.where(fwd, sm[:bsz, :half], 0.0)
    xi = jnp.where(fwd, sm[:bsz, half:], 0.0)
    a_slab_r, a_slab_i = trow(2 * SLAB_CHUNKS), trow(2 * SLAB_CHUNKS + 1)
    for j in range(nslab):
        rf = j * bsz
        rb = (nslab - 1 - j) * bsz
        ent_scr[rf:rf + bsz, 0:SSM_STATE] = xr[:, 0:SSM_STATE]
        ent_scr[rb:rb + bsz, SSM_STATE:half] = xr[:, SSM_STATE:half]
        ent_scr[rf:rf + bsz, half:half + SSM_STATE] = xi[:, 0:SSM_STATE]
        ent_scr[rb:rb + bsz, half + SSM_STATE:STATE_W] = xi[:, SSM_STATE:half]
        s_r = jnp.where(fwd, sup_scr[rf:rf + bsz, :half], sup_scr[rb:rb + bsz, :half])
        s_i = jnp.where(fwd, sup_scr[rf:rf + bsz, half:], sup_scr[rb:rb + bsz, half:])
        xr, xi = _cmul_add(a_slab_r, a_slab_i, xr, xi, s_r, s_i)

    ent = ent_scr[...]
    a_r, a_i = trow(2 * SLAB_CHUNKS + 2), trow(2 * SLAB_CHUNKS + 3)
    xf = [(ent[:, :half], ent[:, half:])]
    for c in range(1, SLAB_CHUNKS):
        xf.append(_cmul_add(a_r, a_i, xf[-1][0], xf[-1][1], sr[c - 1], si[c - 1]))
    xb = [(ent[:, :half], ent[:, half:])]
    for c in range(SLAB_CHUNKS - 2, -1, -1):
        xb.insert(0, _cmul_add(a_r, a_i, xb[0][0], xb[0][1], sr[c + 1], si[c + 1]))
    fwd_rows = lax.broadcasted_iota(jnp.int32, (rows, half), 1) < SSM_STATE
    hi = lax.Precision.HIGHEST
    vf = v_ref[...]
    dirs = [jnp.concatenate([vf[d * SSM_STATE:(d + 1) * SSM_STATE],
                             vf[half + d * SSM_STATE:half + (d + 1) * SSM_STATE]], axis=0) for d in range(2)]
    lag_f = jnp.dot(bt_ref[0], dirs[0], precision=hi, preferred_element_type=_F32)
    lag_b = jnp.dot(bt_ref[1], dirs[1], precision=hi, preferred_element_type=_F32)
    center = (jnp.dot(bt_ref[0], c0_ref[0], precision=hi, preferred_element_type=_F32)
              + jnp.dot(bt_ref[1], c0_ref[1], precision=hi, preferred_element_type=_F32) + d_ref[...])
    lane = lax.broadcasted_iota(jnp.int32, (SSM_GROUP, CHUNK_W), 1)
    lag = jnp.concatenate([center, lag_f[:, :CHUNK_W - SSM_GROUP], jnp.where(lane < SSM_GROUP, 0.0, lag_b)], axis=1)
    m = jnp.concatenate(
        [pltpu.roll(jnp.broadcast_to(lag[i:i + 1, :], (SSM_CHUNK, 2 * CHUNK_W)), 0, 1,
                    stride=SSM_GROUP, stride_axis=0)[:, :CHUNK_W] for i in range(SSM_GROUP)], axis=0).astype(_BF)
    q = q_ref[...]
    m = _dot(m, q).astype(_BF)
    v = _dot(vf.astype(_BF), q).astype(_BF)
    ys = []
    for c in range(SLAB_CHUNKS):
        xin = jnp.concatenate([jnp.where(fwd_rows, xf[c][0], xb[c][0]),
                               jnp.where(fwd_rows, xf[c][1], xb[c][1])], axis=1).astype(_BF)
        ys.append(_dot(uc[c], m) + _dot(xin, v))
    for o in range(SSM_GROUP):
        y_ref[pl.ds(o, rows, stride=SSM_GROUP), :] = jnp.concatenate(
            [y[:, o * SSM_CHUNK:(o + 1) * SSM_CHUNK] for y in ys], axis=1)


def _s5_call(a, um, perm, bt, c0, dmat, w_mat, v_mat, tab, *, nslab, bsz):
    n = nslab * bsz * SSM_GROUP
    rows = nslab * bsz
    g_spec = lambda *shape: pl.BlockSpec((None,) + shape, lambda g: (g,) + (0,) * len(shape))
    perm_spec = pl.BlockSpec((CHUNK_W, CHUNK_W), lambda g: (0, 0), pipeline_mode=pl.Buffered(1))
    return pl.pallas_call(
        functools.partial(_s5_kernel, nslab=nslab, bsz=bsz),
        grid=(SSM_GROUPS,),
        in_specs=[g_spec(n, SLAB_T), g_spec(SUBLANES, CHUNK_W), perm_spec,
                  g_spec(2, SSM_GROUP, 2 * SSM_STATE), g_spec(2, 2 * SSM_STATE, SSM_GROUP), g_spec(SSM_GROUP, SSM_GROUP),
                  g_spec(CHUNK_W, STATE_W), g_spec(STATE_W, CHUNK_W),
                  g_spec(2 * SUBLANES, STATE_W // 2)],
        out_specs=g_spec(n, SLAB_T),
        out_shape=jax.ShapeDtypeStruct((SSM_GROUPS, n, SLAB_T), _F32),
        scratch_shapes=[pltpu.VMEM((rows, STATE_W), _F32), pltpu.VMEM((rows, STATE_W), _F32)],
        compiler_params=pltpu.CompilerParams(
            dimension_semantics=("parallel",), vmem_limit_bytes=VMEM_LIMIT),
        name="s5",
    )(a, um, perm, bt, c0, dmat, w_mat, v_mat, tab)


def _s5_matrices(a_re, a_im, log_dt, b_re, b_im, c_re, c_im, d_skip):
    tc = SSM_CHUNK
    lam = lax.complex(jnp.minimum(a_re.astype(_F32), -1e-4), a_im.astype(_F32))
    dt = jnp.exp(log_dt.astype(_F32))[..., None]
    lam_dt = lam * dt
    lam_bar = jnp.exp(lam_dt)
    b_bar = ((lam_bar - 1.0) / lam)[..., None] * lax.complex(b_re.astype(_F32), b_im.astype(_F32))
    c_c = lax.complex(c_re.astype(_F32), c_im.astype(_F32))
    k_idx = jnp.arange(tc + 1, dtype=_F32)
    pw = jnp.exp(lam_dt[:, :, None, :] * k_idx[None, None, :, None])
    bt = jnp.concatenate([jnp.real(b_bar), jnp.imag(b_bar)], axis=2).transpose(1, 0, 3, 2)
    c0 = jnp.concatenate([jnp.real(c_c), -jnp.imag(c_c)], axis=3).transpose(1, 0, 3, 2)
    dmat = jnp.eye(SSM_GROUP, dtype=_F32)[None] * d_skip.astype(_F32).reshape(SSM_GROUPS, 1, SSM_GROUP)
    wf = b_bar[0].transpose(0, 2, 1)[:, :, None, :] * pw[0][:, tc - 1::-1][:, None, :, :]
    wb = b_bar[1].transpose(0, 2, 1)[:, :, None, :] * pw[1][:, :tc][:, None, :, :]
    w_mat = jnp.concatenate([jnp.real(wf), jnp.real(wb), jnp.imag(wf), jnp.imag(wb)], axis=-1)
    w_mat = w_mat.reshape(SSM_GROUPS, CHUNK_W, STATE_W)
    gf = pw[0][:, 1:tc + 1][:, :, None, :] * c_c[0][:, None, :, :]
    gb = pw[1][:, tc:0:-1][:, :, None, :] * c_c[1][:, None, :, :]
    v_mat = jnp.concatenate([jnp.real(gf), jnp.real(gb), -jnp.imag(gf), -jnp.imag(gb)], axis=-1)
    v_mat = v_mat.reshape(SSM_GROUPS, CHUNK_W, STATE_W).transpose(0, 2, 1)
    n_idx = jnp.arange(SLAB_CHUNKS + 1, dtype=_F32) * tc
    pc = jnp.exp(lam_dt[:, :, None, :] * n_idx[None, None, :, None])
    coef = jnp.concatenate([pc[0][:, SLAB_CHUNKS - 1::-1], pc[1][:, :SLAB_CHUNKS]], axis=-1)
    both = lambda n: jnp.concatenate([pc[0][:, n], pc[1][:, n]], axis=-1)[:, None, :]
    a_slab, a_chunk = both(SLAB_CHUNKS), both(1)
    tab = jnp.concatenate([jnp.real(coef), jnp.imag(coef), jnp.real(a_slab), jnp.imag(a_slab),
                           jnp.real(a_chunk), jnp.imag(a_chunk)], axis=1)
    tab = jnp.pad(tab, ((0, 0), (0, 2 * SUBLANES - tab.shape[1]), (0, 0)))
    return bt, c0, dmat, w_mat, v_mat, tab


def _chunk_permutation():
    r = lax.broadcasted_iota(jnp.int32, (CHUNK_W, CHUNK_W), 0)
    c = lax.broadcasted_iota(jnp.int32, (CHUNK_W, CHUNK_W), 1)
    return (r == (c % SSM_CHUNK) * SSM_GROUP + c // SSM_CHUNK).astype(_BF)


def _post_kernel(x_ref, attn_ref, y_ref, wglu_ref, gmix_ref, wout_ref, gpm_ref, gpre_ref,
                 wup_ref, wdn_ref, gpost_ref, o_ref):
    nslab = y_ref.shape[1]
    gmix = gmix_ref[...]
    per = min(POST_SLABS, nslab)
    blocks = range(nslab // per)
    rows = [pl.ds(r * per * SLAB_T, per * SLAB_T) for r in blocks]
    gy = []
    for r in blocks:
        yt = jnp.concatenate([jnp.concatenate([y_ref[g, r * per + c] for c in range(per)], axis=1)
                              for g in range(SSM_GROUPS)], axis=0)
        y = yt.T
        gy.append((0.5 * y * (1.0 + jnp.tanh(math.sqrt(2.0 / math.pi) * (y + 0.044715 * (y * y * y))))).astype(_BF))
    z = [_dot(gy[r], wglu_ref[...]) for r in blocks]
    mix = []
    for r in blocks:
        ssm = z[r][:, :SSM_WIDTH] * (1.0 / (1.0 + jnp.exp(-z[r][:, SSM_WIDTH:])))
        mix.append(jnp.concatenate([_rms(attn_ref[rows[r], :], gmix[:, :ATTN_WIDTH]),
                                    _rms(ssm, gmix[:, ATTN_WIDTH:])], axis=-1).astype(_BF))
    mixed = [_dot(mix[r], wout_ref[...]) for r in blocks]
    h1 = [x_ref[rows[r], :] + _rms(mixed[r], gpm_ref[...]) for r in blocks]
    hn = [_rms(h1[r], gpre_ref[...]).astype(_BF) for r in blocks]
    acc = [None for _ in blocks]
    for c in range(D_FF // FF_TILE):
        ups = [jnp.maximum(_dot(hn[r], wup_ref[:, c * FF_TILE:(c + 1) * FF_TILE]), 0.0) for r in blocks]
        for r in blocks:
            part = _dot((ups[r] * ups[r]).astype(_BF), wdn_ref[c * FF_TILE:(c + 1) * FF_TILE, :])
            acc[r] = part if acc[r] is None else acc[r] + part
    for r in blocks:
        o_ref[rows[r], :] = h1[r] + _rms(acc[r], gpost_ref[...])


def _post_call(x, attn, y, wglu, gmix, wout, gpm, gpre, wup, wdn, gpost, *, tile):
    bsz, seq, _ = x.shape
    row_spec = lambda w: pl.BlockSpec((None, tile, w), lambda b, i: (b, i, 0))
    wspec = lambda shape: pl.BlockSpec(shape, lambda b, i: (0, 0), pipeline_mode=pl.Buffered(1))
    y_spec = pl.BlockSpec((SSM_GROUPS, tile // SLAB_T, SSM_GROUP, SLAB_T), lambda b, i: (0, i, b, 0))
    return pl.pallas_call(
        _post_kernel,
        grid=(bsz, seq // tile),
        in_specs=[row_spec(D_MODEL), row_spec(ATTN_WIDTH), y_spec,
                  wspec((SSM_WIDTH, 2 * SSM_WIDTH)), wspec((1, D_MODEL)), wspec((D_MODEL, D_MODEL)),
                  wspec((1, D_MODEL)), wspec((1, D_MODEL)), wspec((D_MODEL, D_FF)),
                  wspec((D_FF, D_MODEL)), wspec((1, D_MODEL))],
        out_specs=row_spec(D_MODEL),
        out_shape=jax.ShapeDtypeStruct((bsz, seq, D_MODEL), _F32),
        compiler_params=pltpu.CompilerParams(
            dimension_semantics=("parallel", "parallel"), vmem_limit_bytes=VMEM_LIMIT),
        name="post",
    )(x, attn, y, wglu, gmix, wout, gpm, gpre, wup, wdn, gpost)


def _rope_tables(pos, tile):
    half = QK_ROPE_DIM // 2
    inv = 1.0 / (ROPE_BASE ** (jnp.arange(0, QK_ROPE_DIM, 2, dtype=_F32) / QK_ROPE_DIM))
    ang = pos.astype(_F32)[:, None, :] * inv[None, :, None]
    bsz, seq = pos.shape
    rope = jnp.stack([jnp.cos(ang), jnp.sin(ang)], axis=1)
    return rope.reshape(bsz, 2, half, seq // tile, tile).transpose(0, 3, 1, 2, 4)


def _prep_weights(w_in, w_uq, w_ukv):
    scale = QK_HEAD_DIM ** -0.5 * math.log2(math.e)
    win = w_in[:, :OFF_KR].astype(_BF)
    wkr_t = w_in[:, OFF_KR:OFF_U].T.astype(_BF)
    wu_t = w_in[:, OFF_U:].T.astype(_BF)
    wq_t = (w_uq * scale).T.astype(_BF)
    wkv3 = w_ukv.reshape(KV_LORA_RANK, N_HEADS, QK_NOPE_DIM + V_HEAD_DIM)
    wk = jnp.concatenate([wkv3[..., :QK_NOPE_DIM],
                          jnp.zeros((KV_LORA_RANK, N_HEADS, HEAD_PAD - QK_NOPE_DIM), _F32)], axis=-1)
    wk = wk.reshape(KV_LORA_RANK, N_HEADS * HEAD_PAD).astype(_BF)
    wv_t = wkv3[..., QK_NOPE_DIM:].reshape(KV_LORA_RANK, N_HEADS * V_HEAD_DIM).T.astype(_BF)
    return win, wkr_t, wu_t, wq_t, wk, wv_t


def kernel(x, positions, meta_tokens, g_pre_mix, w_in, g_q_lat, w_uq, g_kv_lat, w_ukv,
           ssm_A_re, ssm_A_im, ssm_log_dt, ssm_B_re, ssm_B_im, ssm_C_re, ssm_C_im, ssm_D,
           w_glu, g_mix_out, w_out, g_post_mix, g_pre_mlp, w_mlp_up, w_mlp_down, g_post_mlp):
    bsz, seq, _ = x.shape
    assert seq % ROW_TILE == 0 and ROW_TILE % SLAB_T == 0 and bsz <= SUBLANES
    assert seq % PROJ_TILE == 0 and PROJ_TILE % SLAB_T == 0
    assert N_META <= SSM_CHUNK
    row = lambda g: g.reshape(1, -1).astype(_F32)

    win, wkr_t, wu_t, wq_t, wk, wv_t = _prep_weights(w_in[0], w_uq[0], w_ukv[0])
    weights = (row(g_pre_mix[0]), win, wkr_t, wu_t, row(g_q_lat[0]), wq_t, row(g_kv_lat[0]), wk, wv_t)
    rope = _rope_tables(positions.astype(jnp.int32) + N_META, PROJ_TILE)
    qt, k, vt, u, q_norm, k_sq = _proj_call(x, rope, *weights, tile=PROJ_TILE, meta=False)
    meta_x = jnp.pad(meta_tokens.astype(x.dtype), ((0, LANES - N_META), (0, 0)))[None]
    rope_m = _rope_tables(jnp.arange(LANES, dtype=jnp.int32)[None], LANES)
    k_m, vt_m, u_m = _proj_call(meta_x, rope_m, *weights, tile=LANES, meta=True)
    km, vtm, u_m = k_m[0], vt_m[0, :, 0], u_m[:, :N_META]
    k_m_sq = jnp.max(jnp.sum(jnp.square(k_m[0].astype(_F32)), axis=-1), axis=-1)
    k_max = jnp.sqrt(jnp.maximum(jnp.max(k_sq, axis=(2, 3, 4)), k_m_sq[None]))
    mrow = q_norm * (k_max * BOUND_SLACK)[:, :, None, None, None]
    attn = lax.cond(jnp.max(mrow) <= BOUND_LIMIT,
                    lambda: _attn_call(qt, mrow, k, vt, km, vtm, bounded=True),
                    lambda: _attn_call(qt, mrow, k, vt, km, vtm, bounded=False))

    um = u_m[0].astype(_BF).reshape(N_META, SSM_GROUPS, SSM_GROUP).transpose(1, 2, 0)
    um = jnp.pad(um, ((0, 0), (0, 0), (SSM_CHUNK - N_META, 0))).reshape(SSM_GROUPS, 1, CHUNK_W)
    um = jnp.broadcast_to(um, (SSM_GROUPS, SUBLANES, CHUNK_W))
    bt, c0, dmat, w_mat, v_mat, tab = _s5_matrices(ssm_A_re[0], ssm_A_im[0], ssm_log_dt[0], ssm_B_re[0],
                                            ssm_B_im[0], ssm_C_re[0], ssm_C_im[0], ssm_D[0])
    nslab = seq // SLAB_T
    yg = _s5_call(u.reshape(SSM_GROUPS, nslab * bsz * SSM_GROUP, SLAB_T), um, _chunk_permutation(),
                  bt, c0, dmat, w_mat.astype(_BF), v_mat, tab, nslab=nslab, bsz=bsz)
    y = yg.reshape(SSM_GROUPS, nslab, bsz * SSM_GROUP, SLAB_T)

    return _post_call(x, attn, y, w_glu[0].astype(_BF), row(g_mix_out[0]), w_out[0].astype(_BF),
                      row(g_post_mix[0]), row(g_pre_mlp[0]), w_mlp_up[0].astype(_BF),
                      w_mlp_down[0].astype(_BF), row(g_post_mlp[0]), tile=ROW_TILE)
```

```python
import functools
import math

import jax
import jax.numpy as jnp
from jax import lax
from jax.experimental import pallas as pl
from jax.experimental.pallas import tpu as pltpu

D_MODEL = 1024
N_META = 16
ATTN_WIDTH = 512
SSM_WIDTH = 512
N_HEADS = 8
V_HEAD_DIM = 64
QK_NOPE_DIM = 64
QK_ROPE_DIM = 32
QK_HEAD_DIM = QK_NOPE_DIM + QK_ROPE_DIM
Q_LORA_RANK = 384
KV_LORA_RANK = 256
ROPE_BASE = 10000.0
SSM_GROUP = 16
SSM_GROUPS = 32
SSM_STATE = 64
D_FF = 4 * D_MODEL
EPS = 1e-6
OFF_KR = Q_LORA_RANK + KV_LORA_RANK
OFF_U = OFF_KR + QK_ROPE_DIM

LANES = 128
SUBLANES = 8
HEAD_PAD = LANES
ONES_LANE = V_HEAD_DIM
ATTN_TQ = 1024
ATTN_TK = 256
V_ROWS = 112
PW_Q = 0
PW_KV = PW_Q + Q_LORA_RANK
PW_END = PW_KV + KV_LORA_RANK

SSM_CHUNK = 32
CHUNK_W = SSM_CHUNK * SSM_GROUP
STATE_W = 4 * SSM_STATE
SLAB_T = LANES
SLAB_CHUNKS = SLAB_T // SSM_CHUNK

ROW_TILE = 512
PROJ_TILE = 1024
FF_TILE = 1024
POST_SLABS = 2
BOUND_SLACK = 1.0 + 2.0 ** -6
BOUND_LIMIT = 40.0
V7X_VMEM_BYTES = 64 * 1024 * 1024
VMEM_LIMIT = V7X_VMEM_BYTES - 8 * 1024 * 1024

_BF = jnp.bfloat16
_F32 = jnp.float32


def _dot(a, b):
    return jnp.dot(a, b, preferred_element_type=_F32)


def _rms(x, g):
    return x * lax.rsqrt(jnp.mean(x * x, axis=-1, keepdims=True) + EPS) * g


_NT = (((1,), (1,)), ((), ()))


def _rotate(x1, x2, cos_t, sin_t):
    return x1 * cos_t - x2 * sin_t, x1 * sin_t + x2 * cos_t


def _proj_kernel(x_ref, rope_ref, *refs, meta):
    if meta:
        gpre_ref, win_ref, wkr_ref, wu_ref, gkv_ref, wk_ref, wv_ref, k_ref, v_ref, u_ref = refs
    else:
        (gpre_ref, win_ref, wkr_ref, wu_ref, gq_ref, wq_ref, gkv_ref, wk_ref, wv_ref,
         q_ref, k_ref, v_ref, u_ref, qn_ref, kmx_ref) = refs
    tile = x_ref.shape[0]
    half = QK_ROPE_DIM // 2
    cos_t, sin_t = rope_ref[0], rope_ref[1]
    xn = _rms(x_ref[...], gpre_ref[...]).astype(_BF)
    proj = _dot(xn, win_ref[...])
    kvn = _rms(proj[:, PW_KV:PW_END], gkv_ref[...]).astype(_BF)
    krt = lax.dot_general(wkr_ref[...], xn, _NT, preferred_element_type=_F32)
    r1, r2 = _rotate(krt[:half], krt[half:], cos_t, sin_t)
    kr = jnp.concatenate([jnp.zeros((QK_NOPE_DIM, tile), _F32), r1, r2,
                          jnp.zeros((HEAD_PAD - QK_HEAD_DIM, tile), _F32)], axis=0).T
    kk = _dot(kvn, wk_ref[...])
    vt = lax.dot_general(wv_ref[...], kvn, _NT, preferred_element_type=_F32)
    ones_tail = (lax.broadcasted_iota(jnp.int32, (HEAD_PAD - V_HEAD_DIM, tile), 0) == 0).astype(_F32)
    if meta:
        u_ref[...] = lax.dot_general(xn, wu_ref[...], _NT, preferred_element_type=_F32)
    else:
        ut = lax.dot_general(wu_ref[...], xn, _NT, preferred_element_type=_F32)
        for g in range(SSM_GROUPS):
            for c in range(tile // SLAB_T):
                u_ref[g, c] = ut[g * SSM_GROUP:(g + 1) * SSM_GROUP, c * SLAB_T:(c + 1) * SLAB_T]
        qn = _rms(proj[:, PW_Q:PW_KV], gq_ref[...]).astype(_BF)
        qt = lax.dot_general(wq_ref[...], qn, _NT, preferred_element_type=_F32)
        zero_rows = jnp.zeros((HEAD_PAD - QK_HEAD_DIM, tile), _F32)
    for h in range(N_HEADS):
        k_h = (kk[:, h * HEAD_PAD:(h + 1) * HEAD_PAD] + kr).astype(_BF)
        k_ref[h] = k_h
        v_ref[h] = jnp.concatenate([vt[h * V_HEAD_DIM:(h + 1) * V_HEAD_DIM], ones_tail], axis=0).astype(_BF)
        if not meta:
            blk = qt[h * QK_HEAD_DIM:(h + 1) * QK_HEAD_DIM]
            r1, r2 = _rotate(blk[QK_NOPE_DIM:QK_NOPE_DIM + half], blk[QK_NOPE_DIM + half:], cos_t, sin_t)
            qt_h = jnp.concatenate([blk[:QK_NOPE_DIM], r1, r2, zero_rows], axis=0).astype(_BF)
            q_ref[h] = qt_h
            qt_f = qt_h.astype(_F32)
            qn_ref[h] = jnp.sqrt(jnp.sum(qt_f * qt_f, axis=0, keepdims=True))
            k_f = k_h.astype(_F32)
            kmx_ref[h] = jnp.broadcast_to(jnp.max(jnp.sum(k_f * k_f, axis=1, keepdims=True), axis=0, keepdims=True),
                                          (1, LANES))


def _const_spec(shape):
    nd = len(shape)
    return pl.BlockSpec(shape, lambda *_: (0,) * nd)


def _proj_call(x, rope, gpre, win, wkr_t, wu_t, gq, wq_t, gkv, wk, wv_t, *, tile, meta):
    bsz, seq, _ = x.shape
    nt = seq // tile
    row_spec = lambda w: pl.BlockSpec((None, tile, w), lambda b, i: (b, i, 0))
    rope_spec = pl.BlockSpec((None, None, 2, QK_ROPE_DIM // 2, tile), lambda b, i: (b, i, 0, 0, 0))
    k_spec = pl.BlockSpec((None, N_HEADS, tile, HEAD_PAD), lambda b, i: (b, 0, i, 0))
    t_spec = pl.BlockSpec((None, N_HEADS, None, HEAD_PAD, tile), lambda b, i: (b, 0, i, 0, 0))
    k_shape = jax.ShapeDtypeStruct((bsz, N_HEADS, seq, HEAD_PAD), _BF)
    t_shape = jax.ShapeDtypeStruct((bsz, N_HEADS, nt, HEAD_PAD, tile), _BF)
    w_specs = lambda *ws: [_const_spec(w.shape) for w in ws]
    if meta:
        args = (x, rope, gpre, win, wkr_t, wu_t, gkv, wk, wv_t)
        in_specs = [row_spec(D_MODEL), rope_spec] + w_specs(*args[2:])
        out_specs = [k_spec, t_spec, row_spec(SSM_WIDTH)]
        out_shape = [k_shape, t_shape, jax.ShapeDtypeStruct((bsz, seq, SSM_WIDTH), _F32)]
    else:
        args = (x, rope, gpre, win, wkr_t, wu_t, gq, wq_t, gkv, wk, wv_t)
        in_specs = [row_spec(D_MODEL), rope_spec] + w_specs(*args[2:])
        norm_spec = lambda w: pl.BlockSpec((None, N_HEADS, None, 1, w), lambda b, i: (b, 0, i, 0, 0))
        u_spec = pl.BlockSpec((SSM_GROUPS, tile // SLAB_T, SSM_GROUP, SLAB_T), lambda b, i: (0, i, b, 0))
        out_specs = [t_spec, k_spec, t_spec, u_spec, norm_spec(tile), norm_spec(LANES)]
        out_shape = [t_shape, k_shape, t_shape,
                     jax.ShapeDtypeStruct((SSM_GROUPS, seq // SLAB_T, bsz * SSM_GROUP, SLAB_T), _F32),
                     jax.ShapeDtypeStruct((bsz, N_HEADS, nt, 1, tile), _F32),
                     jax.ShapeDtypeStruct((bsz, N_HEADS, nt, 1, LANES), _F32)]
    return pl.pallas_call(
        functools.partial(_proj_kernel, meta=meta),
        grid=(bsz, nt),
        in_specs=in_specs,
        out_specs=out_specs,
        out_shape=out_shape,
        compiler_params=pltpu.CompilerParams(
            dimension_semantics=("parallel", "parallel"), vmem_limit_bytes=VMEM_LIMIT),
        name="proj_meta" if meta else "proj",
    )(*args)


def _attn_finish(accs, o_ref):
    halves = [(acc * (1.0 / acc[ONES_LANE:ONES_LANE + 1, :]))[:V_HEAD_DIM] for acc in accs]
    o_ref[...] = jnp.concatenate(halves, axis=0).T


def _lane_concat(ref, hh):
    return jnp.concatenate([ref[hh, j] for j in range(ref.shape[1])], axis=1)


def _attn_bounded_kernel(qt_ref, mrow_ref, k_ref, vt_ref, km_ref, vtm_ref, o_ref, *, nk, tk):
    tq = o_ref.shape[0]
    per_slab = vt_ref.shape[3] // tk
    key_row = lax.broadcasted_iota(jnp.int32, (LANES, tq), 0)
    accs = []
    for hh in range(2):
        qt = _lane_concat(qt_ref, hh)
        mrow = _lane_concat(mrow_ref, hh)
        s0 = jnp.where(key_row < N_META, _dot(km_ref[hh], qt), -jnp.inf)
        acc = _dot(vtm_ref[hh, :V_ROWS, :], jnp.exp2(s0 - mrow).astype(_BF))
        scores = lambda c: _dot(k_ref[hh, c * tk:(c + 1) * tk, :], qt)
        s_next = scores(0)
        for c in range(nk):
            s = s_next
            if c + 1 < nk:
                s_next = scores(c + 1)
            vt_c = vt_ref[hh, c // per_slab, :V_ROWS, (c % per_slab) * tk:(c % per_slab + 1) * tk]
            acc = acc + _dot(vt_c, jnp.exp2(s - mrow).astype(_BF))
        accs.append(acc)
    _attn_finish(accs, o_ref)


def _attn_online_kernel(qt_ref, k_ref, vt_ref, km_ref, vtm_ref, o_ref, s0_scr, s1_scr, m_scr, acc_scr, *, nk, tk):
    tq = o_ref.shape[0]
    key_row = lax.broadcasted_iota(jnp.int32, (LANES, tq), 0)
    for hh in range(2):
        s0 = jnp.where(key_row < N_META, _dot(km_ref[hh], _lane_concat(qt_ref, hh)), -jnp.inf)
        m0 = jnp.max(s0, axis=0, keepdims=True)
        m_scr[hh] = m0
        acc_scr[hh] = _dot(vtm_ref[hh, :V_ROWS, :], jnp.exp2(s0 - m0).astype(_BF))

    def scores(buf, c):
        off = pl.multiple_of(c * tk, tk)
        for hh in range(2):
            buf[hh] = _dot(k_ref[hh, pl.ds(off, tk), :], _lane_concat(qt_ref, hh))

    def accumulate(buf, c):
        for hh in range(2):
            s = buf[hh]
            m = m_scr[hh]
            m_new = jnp.maximum(m, jnp.max(s, axis=0, keepdims=True))
            m_scr[hh] = m_new
            p = jnp.exp2(s - m_new).astype(_BF)
            acc_scr[hh] = jnp.exp2(m - m_new) * acc_scr[hh] + _dot(vt_ref[hh, c, :V_ROWS, :], p)

    scores(s0_scr, 0)

    def body(t, _):
        scores(s1_scr, 2 * t + 1)
        accumulate(s0_scr, 2 * t)
        scores(s0_scr, 2 * t + 2)
        accumulate(s1_scr, 2 * t + 1)
        return 0

    lax.fori_loop(0, nk // 2 - 1, body, 0)
    scores(s1_scr, nk - 1)
    accumulate(s0_scr, nk - 2)
    accumulate(s1_scr, nk - 1)
    _attn_finish([acc_scr[0], acc_scr[1]], o_ref)


def _attn_call(qt, mrow, k, vt, km, vtm, *, bounded):
    bsz, _, nslab, _, slab = qt.shape
    seq = k.shape[2]
    tq = min(ATTN_TQ, seq)
    tk = min(ATTN_TK, slab)
    qs = tq // slab
    assert tq % slab == 0 and seq % tq == 0 and slab % tk == 0
    q_spec = lambda rows: pl.BlockSpec((None, 2, qs, rows, slab), lambda b, hp, i: (b, hp, i, 0, 0))
    kv_specs = [pl.BlockSpec((None, 2, seq, HEAD_PAD), lambda b, hp, i: (b, hp, 0, 0)),
                pl.BlockSpec((None, 2, nslab, HEAD_PAD, slab), lambda b, hp, i: (b, hp, 0, 0, 0)),
                pl.BlockSpec((2, LANES, HEAD_PAD), lambda b, hp, i: (hp, 0, 0)),
                pl.BlockSpec((2, HEAD_PAD, LANES), lambda b, hp, i: (hp, 0, 0))]
    if bounded:
        body = functools.partial(_attn_bounded_kernel, nk=seq // tk, tk=tk)
        in_specs = [q_spec(HEAD_PAD), q_spec(1)] + kv_specs
        args, scratch = (qt, mrow, k, vt, km, vtm), []
    else:
        assert nslab % 2 == 0 and nslab >= 4
        body = functools.partial(_attn_online_kernel, nk=nslab, tk=slab)
        in_specs = [q_spec(HEAD_PAD)] + kv_specs
        args = (qt, k, vt, km, vtm)
        scratch = [pltpu.VMEM((2, slab, tq), _F32), pltpu.VMEM((2, slab, tq), _F32),
                   pltpu.VMEM((2, 1, tq), _F32), pltpu.VMEM((2, V_ROWS, tq), _F32)]
    return pl.pallas_call(
        body,
        grid=(bsz, N_HEADS // 2, seq // tq),
        in_specs=in_specs,
        out_specs=pl.BlockSpec((None, tq, LANES), lambda b, hp, i: (b, i, hp)),
        out_shape=jax.ShapeDtypeStruct((bsz, seq, ATTN_WIDTH), _F32),
        scratch_shapes=scratch,
        compiler_params=pltpu.CompilerParams(
            dimension_semantics=("parallel", "parallel", "arbitrary"), vmem_limit_bytes=VMEM_LIMIT),
        name="attn_bounded" if bounded else "attn_online",
    )(*args)


def _cmul_add(ar, ai, xr, xi, sr, si):
    return ar * xr - ai * xi + sr, ar * xi + ai * xr + si


def _s5_kernel(a_ref, um_ref, q_ref, bt_ref, c0_ref, d_ref, w_ref, v_ref, vc_ref, t_ref, y_ref, sup_scr, ent_scr,
               *, nslab, bsz):
    rows = nslab * bsz
    half = STATE_W // 2
    a = [a_ref[pl.ds(i, rows, stride=SSM_GROUP), :] for i in range(SSM_GROUP)]
    uc = [jnp.concatenate([x[:, c * SSM_CHUNK:(c + 1) * SSM_CHUNK] for x in a], axis=1).astype(_BF)
          for c in range(SLAB_CHUNKS)]
    w = w_ref[...]
    s = [_dot(u, w) for u in uc]
    sr = [x[:, :half] for x in s]
    si = [x[:, half:] for x in s]
    t = t_ref[...]
    trow = lambda r: t[r:r + 1, :]
    sup_r = sup_i = None
    for c in range(SLAB_CHUNKS):
        cr, ci = trow(c), trow(SLAB_CHUNKS + c)
        pr = cr * sr[c] - ci * si[c]
        pi = cr * si[c] + ci * sr[c]
        sup_r = pr if sup_r is None else sup_r + pr
        sup_i = pi if sup_i is None else sup_i + pi
    sup_scr[:, :half] = sup_r
    sup_scr[:, half:] = sup_i

    lane = lax.broadcasted_iota(jnp.int32, (bsz, half), 1)
    fwd = lane < SSM_STATE
    sm = _dot(um_ref[...], w)
    xr = jnp.where(fwd, sm[:bsz, :half], 0.0)
    xi = jnp.where(fwd, sm[:bsz, half:], 0.0)
    a_slab_r, a_slab_i = trow(2 * SLAB_CHUNKS), trow(2 * SLAB_CHUNKS + 1)
    for j in range(nslab):
        rf = j * bsz
        rb = (nslab - 1 - j) * bsz
        ent_scr[rf:rf + bsz, 0:SSM_STATE] = xr[:, 0:SSM_STATE]
        ent_scr[rb:rb + bsz, SSM_STATE:half] = xr[:, SSM_STATE:half]
        ent_scr[rf:rf + bsz, half:half + SSM_STATE] = xi[:, 0:SSM_STATE]
        ent_scr[rb:rb + bsz, half + SSM_STATE:STATE_W] = xi[:, SSM_STATE:half]
        s_r = jnp.where(fwd, sup_scr[rf:rf + bsz, :half], sup_scr[rb:rb + bsz, :half])
        s_i = jnp.where(fwd, sup_scr[rf:rf + bsz, half:], sup_scr[rb:rb + bsz, half:])
        xr, xi = _cmul_add(a_slab_r, a_slab_i, xr, xi, s_r, s_i)

    ent = ent_scr[...]
    a_r, a_i = trow(2 * SLAB_CHUNKS + 2), trow(2 * SLAB_CHUNKS + 3)
    xf = [(ent[:, :half], ent[:, half:])]
    for c in range(1, SLAB_CHUNKS):
        xf.append(_cmul_add(a_r, a_i, xf[-1][0], xf[-1][1], sr[c - 1], si[c - 1]))
    xb = [(ent[:, :half], ent[:, half:])]
    for c in range(SLAB_CHUNKS - 2, -1, -1):
        xb.insert(0, _cmul_add(a_r, a_i, xb[0][0], xb[0][1], sr[c + 1], si[c + 1]))
    fwd_rows = lax.broadcasted_iota(jnp.int32, (rows, half), 1) < SSM_STATE
    hi = lax.Precision.HIGHEST
    vf = v_ref[...]
    dirs = [jnp.concatenate([vf[:, d * SSM_STATE:(d + 1) * SSM_STATE],
                             vf[:, half + d * SSM_STATE:half + (d + 1) * SSM_STATE]], axis=1) for d in range(2)]
    nt_hi = lambda a, b: lax.dot_general(a, b, _NT, precision=hi, preferred_element_type=_F32)
    lag_f = nt_hi(bt_ref[0], dirs[0])
    lag_b = nt_hi(bt_ref[1], dirs[1])
    center = (jnp.dot(bt_ref[0], c0_ref[0], precision=hi, preferred_element_type=_F32)
              + jnp.dot(bt_ref[1], c0_ref[1], precision=hi, preferred_element_type=_F32) + d_ref[...])
    lane = lax.broadcasted_iota(jnp.int32, (SSM_GROUP, CHUNK_W), 1)
    lag = jnp.concatenate([center, lag_f[:, :CHUNK_W - SSM_GROUP], jnp.where(lane < SSM_GROUP, 0.0, lag_b)], axis=1)
    m = jnp.concatenate(
        [pltpu.roll(jnp.broadcast_to(lag[i:i + 1, :], (SSM_CHUNK, 2 * CHUNK_W)), 0, 1,
                    stride=SSM_GROUP, stride_axis=0)[:, :CHUNK_W] for i in range(SSM_GROUP)], axis=0).astype(_BF)
    m = _dot(m, q_ref[...]).astype(_BF)
    v = vc_ref[...]
    ys = []
    for c in range(SLAB_CHUNKS):
        xin = jnp.concatenate([jnp.where(fwd_rows, xf[c][0], xb[c][0]),
                               jnp.where(fwd_rows, xf[c][1], xb[c][1])], axis=1).astype(_BF)
        ys.append(_dot(uc[c], m)
                  + lax.dot_general(xin, v, _NT, preferred_element_type=_F32))
    for o in range(SSM_GROUP):
        y_ref[pl.ds(o, rows, stride=SSM_GROUP), :] = jnp.concatenate(
            [y[:, o * SSM_CHUNK:(o + 1) * SSM_CHUNK] for y in ys], axis=1)


def _s5_call(a, um, perm, bt, c0, dmat, w_mat, vt_mat, vc_mat, tab, *, nslab, bsz):
    n = nslab * bsz * SSM_GROUP
    rows = nslab * bsz
    g_spec = lambda *shape: pl.BlockSpec((None,) + shape, lambda g: (g,) + (0,) * len(shape))
    perm_spec = pl.BlockSpec((CHUNK_W, CHUNK_W), lambda g: (0, 0), pipeline_mode=pl.Buffered(1))
    return pl.pallas_call(
        functools.partial(_s5_kernel, nslab=nslab, bsz=bsz),
        grid=(SSM_GROUPS,),
        in_specs=[g_spec(n, SLAB_T), g_spec(SUBLANES, CHUNK_W), perm_spec,
                  g_spec(2, SSM_GROUP, 2 * SSM_STATE), g_spec(2, 2 * SSM_STATE, SSM_GROUP), g_spec(SSM_GROUP, SSM_GROUP),
                  g_spec(CHUNK_W, STATE_W), g_spec(CHUNK_W, STATE_W), g_spec(CHUNK_W, STATE_W),
                  g_spec(2 * SUBLANES, STATE_W // 2)],
        out_specs=g_spec(n, SLAB_T),
        out_shape=jax.ShapeDtypeStruct((SSM_GROUPS, n, SLAB_T), _F32),
        scratch_shapes=[pltpu.VMEM((rows, STATE_W), _F32), pltpu.VMEM((rows, STATE_W), _F32)],
        compiler_params=pltpu.CompilerParams(
            dimension_semantics=("parallel",), vmem_limit_bytes=VMEM_LIMIT),
        name="s5",
    )(a, um, perm, bt, c0, dmat, w_mat, vt_mat, vc_mat, tab)


def _ops_kernel(lam_ref, b_ref, c_ref, w_ref, vt_ref, vc_ref, t_ref):
    tc = SSM_CHUNK
    lr, li = lam_ref[0:1, :], lam_ref[1:2, :]

    def power(k):
        mag = jnp.exp(k * lr)
        return mag * jnp.cos(k * li), mag * jnp.sin(k * li)

    s = lax.broadcasted_iota(jnp.int32, (tc, 2 * SSM_STATE), 0)
    fwd = lax.broadcasted_iota(jnp.int32, (tc, 2 * SSM_STATE), 1) < SSM_STATE
    pr, pi = power(jnp.where(fwd, tc - 1 - s, s).astype(_F32))
    gr, gi = power(jnp.where(fwd, s + 1, tc - s).astype(_F32))
    for h in range(SSM_GROUP):
        br, bi = b_ref[0, h:h + 1, :], b_ref[1, h:h + 1, :]
        w_ref[h * tc:(h + 1) * tc, :] = jnp.concatenate([br * pr - bi * pi, br * pi + bi * pr], axis=1).astype(_BF)
        cr, ci = c_ref[0, h:h + 1, :], c_ref[1, h:h + 1, :]
        vc_ref[h * tc:(h + 1) * tc, :] = jnp.concatenate([cr * gr - ci * gi, -(cr * gi + ci * gr)],
                                                         axis=1).astype(_BF)
    cr, ci = c_ref[0], c_ref[1]
    for t in range(tc):
        tr, ti = gr[t:t + 1, :], gi[t:t + 1, :]
        vt_ref[t * SSM_GROUP:(t + 1) * SSM_GROUP, :] = jnp.concatenate([cr * tr - ci * ti, -(cr * ti + ci * tr)],
                                                                       axis=1)
    r = lax.broadcasted_iota(jnp.int32, (2 * SUBLANES, 2 * SSM_STATE), 0)
    f = lax.broadcasted_iota(jnp.int32, (2 * SUBLANES, 2 * SSM_STATE), 1) < SSM_STATE
    c = jnp.where(r < SLAB_CHUNKS, r, r - SLAB_CHUNKS)
    n = jnp.where(r < 2 * SLAB_CHUNKS, jnp.where(f, SLAB_CHUNKS - 1 - c, c),
                  jnp.where(r < 2 * SLAB_CHUNKS + 2, SLAB_CHUNKS, 1))
    nr, ni = power((n * tc).astype(_F32))
    is_re = (r < SLAB_CHUNKS) | (r == 2 * SLAB_CHUNKS) | (r == 2 * SLAB_CHUNKS + 2)
    t_ref[...] = jnp.where(r < 2 * SLAB_CHUNKS + 4, jnp.where(is_re, nr, ni), 0.0)


def _s5_matrices(a_re, a_im, log_dt, b_re, b_im, c_re, c_im, d_skip):
    lam = lax.complex(jnp.minimum(a_re.astype(_F32), -1e-4), a_im.astype(_F32))
    dt = jnp.exp(log_dt.astype(_F32))[..., None]
    lam_dt = lam * dt
    lam_bar = jnp.exp(lam_dt)
    b_bar = ((lam_bar - 1.0) / lam)[..., None] * lax.complex(b_re.astype(_F32), b_im.astype(_F32))
    c_c = lax.complex(c_re.astype(_F32), c_im.astype(_F32))
    bt = jnp.concatenate([jnp.real(b_bar), jnp.imag(b_bar)], axis=2).transpose(1, 0, 3, 2)
    c0 = jnp.concatenate([jnp.real(c_c), -jnp.imag(c_c)], axis=3).transpose(1, 0, 3, 2)
    dmat = jnp.eye(SSM_GROUP, dtype=_F32)[None] * d_skip.astype(_F32).reshape(SSM_GROUPS, 1, SSM_GROUP)
    lanes = lambda z, perm: z.transpose(perm).reshape(SSM_GROUPS, -1, 2 * SSM_STATE)
    split = lambda z: jnp.stack([jnp.real(z), jnp.imag(z)], axis=1)
    lam_tab = jnp.pad(split(lanes(lam_dt, (1, 0, 2)))[:, :, 0], ((0, 0), (0, SUBLANES - 2), (0, 0)))
    b_tab = split(lanes(b_bar, (1, 3, 0, 2)))
    c_tab = split(lanes(c_c, (1, 2, 0, 3)))
    g_spec = lambda *shape: pl.BlockSpec((None,) + shape, lambda g: (g,) + (0,) * len(shape))
    op_shape = lambda dtype: jax.ShapeDtypeStruct((SSM_GROUPS, CHUNK_W, STATE_W), dtype)
    w_mat, vt_mat, vc_mat, tab = pl.pallas_call(
        _ops_kernel,
        grid=(SSM_GROUPS,),
        in_specs=[g_spec(SUBLANES, 2 * SSM_STATE), g_spec(2, SSM_GROUP, 2 * SSM_STATE),
                  g_spec(2, SSM_GROUP, 2 * SSM_STATE)],
        out_specs=[g_spec(CHUNK_W, STATE_W)] * 3 + [g_spec(2 * SUBLANES, 2 * SSM_STATE)],
        out_shape=[op_shape(_BF), op_shape(_F32), op_shape(_BF),
                   jax.ShapeDtypeStruct((SSM_GROUPS, 2 * SUBLANES, 2 * SSM_STATE), _F32)],
        compiler_params=pltpu.CompilerParams(dimension_semantics=("parallel",)),
        name="s5_ops",
    )(lam_tab, b_tab, c_tab)
    return bt, c0, dmat, w_mat, vt_mat, vc_mat, tab


def _chunk_permutation():
    r = lax.broadcasted_iota(jnp.int32, (CHUNK_W, CHUNK_W), 0)
    c = lax.broadcasted_iota(jnp.int32, (CHUNK_W, CHUNK_W), 1)
    return (r == (c % SSM_CHUNK) * SSM_GROUP + c // SSM_CHUNK).astype(_BF)


def _post_kernel(x_ref, attn_ref, y_ref, wglu_ref, gmix_ref, wout_ref, gpm_ref, gpre_ref,
                 wup_ref, wdn_ref, gpost_ref, o_ref):
    nslab = y_ref.shape[1]
    gmix = gmix_ref[...]
    per = min(POST_SLABS, nslab)
    blocks = range(nslab // per)
    rows = [pl.ds(r * per * SLAB_T, per * SLAB_T) for r in blocks]
    gy = []
    for r in blocks:
        yt = jnp.concatenate([jnp.concatenate([y_ref[g, r * per + c] for c in range(per)], axis=1)
                              for g in range(SSM_GROUPS)], axis=0)
        y = yt.T
        gy.append((0.5 * y * (1.0 + jnp.tanh(math.sqrt(2.0 / math.pi) * (y + 0.044715 * (y * y * y))))).astype(_BF))
    z = [_dot(gy[r], wglu_ref[...]) for r in blocks]
    mix = []
    for r in blocks:
        ssm = z[r][:, :SSM_WIDTH] * (1.0 / (1.0 + jnp.exp(-z[r][:, SSM_WIDTH:])))
        mix.append(jnp.concatenate([_rms(attn_ref[rows[r], :], gmix[:, :ATTN_WIDTH]),
                                    _rms(ssm, gmix[:, ATTN_WIDTH:])], axis=-1).astype(_BF))
    mixed = [_dot(mix[r], wout_ref[...]) for r in blocks]
    h1 = [x_ref[rows[r], :] + _rms(mixed[r], gpm_ref[...]) for r in blocks]
    hn = [_rms(h1[r], gpre_ref[...]).astype(_BF) for r in blocks]
    acc = [None for _ in blocks]
    for c in range(D_FF // FF_TILE):
        for r in blocks:
            up = jnp.maximum(_dot(hn[r], wup_ref[:, c * FF_TILE:(c + 1) * FF_TILE]), 0.0)
            part = _dot((up * up).astype(_BF), wdn_ref[c * FF_TILE:(c + 1) * FF_TILE, :])
            acc[r] = part if acc[r] is None else acc[r] + part
    for r in blocks:
        o_ref[rows[r], :] = h1[r] + _rms(acc[r], gpost_ref[...])


def _post_call(x, attn, y, wglu, gmix, wout, gpm, gpre, wup, wdn, gpost, *, tile):
    bsz, seq, _ = x.shape
    row_spec = lambda w: pl.BlockSpec((None, tile, w), lambda b, i: (b, i, 0))
    wspec = lambda shape: pl.BlockSpec(shape, lambda b, i: (0, 0), pipeline_mode=pl.Buffered(1))
    y_spec = pl.BlockSpec((SSM_GROUPS, tile // SLAB_T, SSM_GROUP, SLAB_T), lambda b, i: (0, i, b, 0))
    return pl.pallas_call(
        _post_kernel,
        grid=(bsz, seq // tile),
        in_specs=[row_spec(D_MODEL), row_spec(ATTN_WIDTH), y_spec,
                  wspec((SSM_WIDTH, 2 * SSM_WIDTH)), wspec((1, D_MODEL)), wspec((D_MODEL, D_MODEL)),
                  wspec((1, D_MODEL)), wspec((1, D_MODEL)), wspec((D_MODEL, D_FF)),
                  wspec((D_FF, D_MODEL)), wspec((1, D_MODEL))],
        out_specs=row_spec(D_MODEL),
        out_shape=jax.ShapeDtypeStruct((bsz, seq, D_MODEL), _F32),
        compiler_params=pltpu.CompilerParams(
            dimension_semantics=("parallel", "parallel"), vmem_limit_bytes=VMEM_LIMIT),
        name="post",
    )(x, attn, y, wglu, gmix, wout, gpm, gpre, wup, wdn, gpost)


def _rope_tables(pos, tile):
    half = QK_ROPE_DIM // 2
    inv = 1.0 / (ROPE_BASE ** (jnp.arange(0, QK_ROPE_DIM, 2, dtype=_F32) / QK_ROPE_DIM))
    ang = pos.astype(_F32)[:, None, :] * inv[None, :, None]
    bsz, seq = pos.shape
    rope = jnp.stack([jnp.cos(ang), jnp.sin(ang)], axis=1)
    return rope.reshape(bsz, 2, half, seq // tile, tile).transpose(0, 3, 1, 2, 4)


def _prep_weights(w_in, w_uq, w_ukv):
    scale = QK_HEAD_DIM ** -0.5 * math.log2(math.e)
    win = w_in[:, :OFF_KR].astype(_BF)
    wkr_t = w_in[:, OFF_KR:OFF_U].T.astype(_BF)
    wu_t = w_in[:, OFF_U:].T.astype(_BF)
    wq_t = (w_uq * scale).T.astype(_BF)
    wkv3 = w_ukv.reshape(KV_LORA_RANK, N_HEADS, QK_NOPE_DIM + V_HEAD_DIM)
    wk = jnp.concatenate([wkv3[..., :QK_NOPE_DIM],
                          jnp.zeros((KV_LORA_RANK, N_HEADS, HEAD_PAD - QK_NOPE_DIM), _F32)], axis=-1)
    wk = wk.reshape(KV_LORA_RANK, N_HEADS * HEAD_PAD).astype(_BF)
    wv_t = wkv3[..., QK_NOPE_DIM:].reshape(KV_LORA_RANK, N_HEADS * V_HEAD_DIM).T.astype(_BF)
    return win, wkr_t, wu_t, wq_t, wk, wv_t


def kernel(x, positions, meta_tokens, g_pre_mix, w_in, g_q_lat, w_uq, g_kv_lat, w_ukv,
           ssm_A_re, ssm_A_im, ssm_log_dt, ssm_B_re, ssm_B_im, ssm_C_re, ssm_C_im, ssm_D,
           w_glu, g_mix_out, w_out, g_post_mix, g_pre_mlp, w_mlp_up, w_mlp_down, g_post_mlp):
    bsz, seq, _ = x.shape
    assert seq % ROW_TILE == 0 and ROW_TILE % SLAB_T == 0 and bsz <= SUBLANES
    assert seq % PROJ_TILE == 0 and PROJ_TILE % SLAB_T == 0
    assert N_META <= SSM_CHUNK
    row = lambda g: g.reshape(1, -1).astype(_F32)

    win, wkr_t, wu_t, wq_t, wk, wv_t = _prep_weights(w_in[0], w_uq[0], w_ukv[0])
    weights = (row(g_pre_mix[0]), win, wkr_t, wu_t, row(g_q_lat[0]), wq_t, row(g_kv_lat[0]), wk, wv_t)
    rope = _rope_tables(positions.astype(jnp.int32) + N_META, PROJ_TILE)
    qt, k, vt, u, q_norm, k_sq = _proj_call(x, rope, *weights, tile=PROJ_TILE, meta=False)
    meta_x = jnp.pad(meta_tokens.astype(x.dtype), ((0, LANES - N_META), (0, 0)))[None]
    rope_m = _rope_tables(jnp.arange(LANES, dtype=jnp.int32)[None], LANES)
    k_m, vt_m, u_m = _proj_call(meta_x, rope_m, *weights, tile=LANES, meta=True)
    km, vtm, u_m = k_m[0], vt_m[0, :, 0], u_m[:, :N_META]
    k_m_sq = jnp.max(jnp.sum(jnp.square(k_m[0].astype(_F32)), axis=-1), axis=-1)
    k_max = jnp.sqrt(jnp.maximum(jnp.max(k_sq, axis=(2, 3, 4)), k_m_sq[None]))
    mrow = q_norm * (k_max * BOUND_SLACK)[:, :, None, None, None]
    attn = lax.cond(jnp.max(mrow) <= BOUND_LIMIT,
                    lambda: _attn_call(qt, mrow, k, vt, km, vtm, bounded=True),
                    lambda: _attn_call(qt, mrow, k, vt, km, vtm, bounded=False))

    um = u_m[0].astype(_BF).reshape(N_META, SSM_GROUPS, SSM_GROUP).transpose(1, 2, 0)
    um = jnp.pad(um, ((0, 0), (0, 0), (SSM_CHUNK - N_META, 0))).reshape(SSM_GROUPS, 1, CHUNK_W)
    um = jnp.broadcast_to(um, (SSM_GROUPS, SUBLANES, CHUNK_W))
    ops = _s5_matrices(ssm_A_re[0], ssm_A_im[0], ssm_log_dt[0], ssm_B_re[0],
                       ssm_B_im[0], ssm_C_re[0], ssm_C_im[0], ssm_D[0])
    nslab = seq // SLAB_T
    yg = _s5_call(u.reshape(SSM_GROUPS, nslab * bsz * SSM_GROUP, SLAB_T), um, _chunk_permutation(),
                  *ops, nslab=nslab, bsz=bsz)
    y = yg.reshape(SSM_GROUPS, nslab, bsz * SSM_GROUP, SLAB_T)

    return _post_call(x, attn, y, w_glu[0].astype(_BF), row(g_mix_out[0]), w_out[0].astype(_BF),
                      row(g_post_mix[0]), row(g_pre_mlp[0]), w_mlp_up[0].astype(_BF),
                      w_mlp_down[0].astype(_BF), row(g_post_mlp[0]), tile=ROW_TILE)
```

```python
import functools
import math

import jax
import jax.numpy as jnp
from jax import lax
from jax.experimental import pallas as pl
from jax.experimental.pallas import tpu as pltpu

D_MODEL = 1024
N_META = 16
ATTN_WIDTH = 512
SSM_WIDTH = 512
N_HEADS = 8
V_HEAD_DIM = 64
QK_NOPE_DIM = 64
QK_ROPE_DIM = 32
QK_HEAD_DIM = QK_NOPE_DIM + QK_ROPE_DIM
Q_LORA_RANK = 384
KV_LORA_RANK = 256
ROPE_BASE = 10000.0
SSM_GROUP = 16
SSM_GROUPS = 32
SSM_STATE = 64
D_FF = 4 * D_MODEL
EPS = 1e-6
OFF_KR = Q_LORA_RANK + KV_LORA_RANK
OFF_U = OFF_KR + QK_ROPE_DIM

LANES = 128
SUBLANES = 8
HEAD_PAD = LANES
ONES_LANE = V_HEAD_DIM
ATTN_TQ = 1024
ATTN_TK = 256
V_ROWS = 112
PW_Q = 0
PW_KV = PW_Q + Q_LORA_RANK
PW_END = PW_KV + KV_LORA_RANK

SSM_CHUNK = 32
CHUNK_W = SSM_CHUNK * SSM_GROUP
STATE_W = 4 * SSM_STATE
SLAB_T = LANES
SLAB_CHUNKS = SLAB_T // SSM_CHUNK

ROW_TILE = 512
PROJ_TILE = 1024
FF_TILE = 1024
POST_SLABS = 2
BOUND_SLACK = 1.0 + 2.0 ** -6
BOUND_LIMIT = 40.0
V7X_VMEM_BYTES = 64 * 1024 * 1024
VMEM_LIMIT = V7X_VMEM_BYTES - 8 * 1024 * 1024

_BF = jnp.bfloat16
_F32 = jnp.float32


def _dot(a, b):
    return jnp.dot(a, b, preferred_element_type=_F32)


def _rms(x, g):
    return x * lax.rsqrt(jnp.mean(x * x, axis=-1, keepdims=True) + EPS) * g


_NT = (((1,), (1,)), ((), ()))


def _rotate(x1, x2, cos_t, sin_t):
    return x1 * cos_t - x2 * sin_t, x1 * sin_t + x2 * cos_t


def _proj_kernel(x_ref, rope_ref, *refs, meta):
    if meta:
        gpre_ref, win_ref, wkr_ref, wu_ref, gkv_ref, wk_ref, wv_ref, k_ref, v_ref, u_ref = refs
    else:
        (gpre_ref, win_ref, wkr_ref, wu_ref, gq_ref, wq_ref, gkv_ref, wk_ref, wv_ref,
         q_ref, k_ref, v_ref, u_ref, qn_ref, kmx_ref) = refs
    tile = x_ref.shape[0]
    half = QK_ROPE_DIM // 2
    cos_t, sin_t = rope_ref[0], rope_ref[1]
    xn = _rms(x_ref[...], gpre_ref[...]).astype(_BF)
    proj = _dot(xn, win_ref[...])
    kvn = _rms(proj[:, PW_KV:PW_END], gkv_ref[...]).astype(_BF)
    krt = lax.dot_general(wkr_ref[...], xn, _NT, preferred_element_type=_F32)
    r1, r2 = _rotate(krt[:half], krt[half:], cos_t, sin_t)
    kr = jnp.concatenate([jnp.zeros((QK_NOPE_DIM, tile), _F32), r1, r2,
                          jnp.zeros((HEAD_PAD - QK_HEAD_DIM, tile), _F32)], axis=0).T
    kk = _dot(kvn, wk_ref[...])
    vt = lax.dot_general(wv_ref[...], kvn, _NT, preferred_element_type=_F32)
    ones_tail = (lax.broadcasted_iota(jnp.int32, (HEAD_PAD - V_HEAD_DIM, tile), 0) == 0).astype(_F32)
    if meta:
        u_ref[...] = lax.dot_general(xn, wu_ref[...], _NT, preferred_element_type=_F32)
    else:
        ut = lax.dot_general(wu_ref[...], xn, _NT, preferred_element_type=_F32)
        for g in range(SSM_GROUPS):
            for c in range(tile // SLAB_T):
                u_ref[g, c] = ut[g * SSM_GROUP:(g + 1) * SSM_GROUP, c * SLAB_T:(c + 1) * SLAB_T]
        qn = _rms(proj[:, PW_Q:PW_KV], gq_ref[...]).astype(_BF)
        qt = lax.dot_general(wq_ref[...], qn, _NT, preferred_element_type=_F32)
        zero_rows = jnp.zeros((HEAD_PAD - QK_HEAD_DIM, tile), _F32)
    for h in range(N_HEADS):
        k_h = (kk[:, h * HEAD_PAD:(h + 1) * HEAD_PAD] + kr).astype(_BF)
        k_ref[h] = k_h
        v_ref[h] = jnp.concatenate([vt[h * V_HEAD_DIM:(h + 1) * V_HEAD_DIM], ones_tail], axis=0).astype(_BF)
        if not meta:
            blk = qt[h * QK_HEAD_DIM:(h + 1) * QK_HEAD_DIM]
            r1, r2 = _rotate(blk[QK_NOPE_DIM:QK_NOPE_DIM + half], blk[QK_NOPE_DIM + half:], cos_t, sin_t)
            qt_h = jnp.concatenate([blk[:QK_NOPE_DIM], r1, r2, zero_rows], axis=0).astype(_BF)
            q_ref[h] = qt_h
            qt_f = qt_h.astype(_F32)
            qn_ref[h] = jnp.sqrt(jnp.sum(qt_f * qt_f, axis=0, keepdims=True))
            k_f = k_h.astype(_F32)
            kmx_ref[h] = jnp.broadcast_to(jnp.max(jnp.sum(k_f * k_f, axis=1, keepdims=True), axis=0, keepdims=True),
                                          (1, LANES))


def _const_spec(shape):
    nd = len(shape)
    return pl.BlockSpec(shape, lambda *_: (0,) * nd)


def _proj_call(x, rope, gpre, win, wkr_t, wu_t, gq, wq_t, gkv, wk, wv_t, *, tile, meta):
    bsz, seq, _ = x.shape
    nt = seq // tile
    row_spec = lambda w: pl.BlockSpec((None, tile, w), lambda b, i: (b, i, 0))
    rope_spec = pl.BlockSpec((None, None, 2, QK_ROPE_DIM // 2, tile), lambda b, i: (b, i, 0, 0, 0))
    k_spec = pl.BlockSpec((None, N_HEADS, tile, HEAD_PAD), lambda b, i: (b, 0, i, 0))
    t_spec = pl.BlockSpec((None, N_HEADS, None, HEAD_PAD, tile), lambda b, i: (b, 0, i, 0, 0))
    k_shape = jax.ShapeDtypeStruct((bsz, N_HEADS, seq, HEAD_PAD), _BF)
    t_shape = jax.ShapeDtypeStruct((bsz, N_HEADS, nt, HEAD_PAD, tile), _BF)
    w_specs = lambda *ws: [_const_spec(w.shape) for w in ws]
    if meta:
        args = (x, rope, gpre, win, wkr_t, wu_t, gkv, wk, wv_t)
        in_specs = [row_spec(D_MODEL), rope_spec] + w_specs(*args[2:])
        out_specs = [k_spec, t_spec, row_spec(SSM_WIDTH)]
        out_shape = [k_shape, t_shape, jax.ShapeDtypeStruct((bsz, seq, SSM_WIDTH), _F32)]
    else:
        args = (x, rope, gpre, win, wkr_t, wu_t, gq, wq_t, gkv, wk, wv_t)
        in_specs = [row_spec(D_MODEL), rope_spec] + w_specs(*args[2:])
        norm_spec = lambda w: pl.BlockSpec((None, N_HEADS, None, 1, w), lambda b, i: (b, 0, i, 0, 0))
        u_spec = pl.BlockSpec((SSM_GROUPS, tile // SLAB_T, SSM_GROUP, SLAB_T), lambda b, i: (0, i, b, 0))
        out_specs = [t_spec, k_spec, t_spec, u_spec, norm_spec(tile), norm_spec(LANES)]
        out_shape = [t_shape, k_shape, t_shape,
                     jax.ShapeDtypeStruct((SSM_GROUPS, seq // SLAB_T, bsz * SSM_GROUP, SLAB_T), _F32),
                     jax.ShapeDtypeStruct((bsz, N_HEADS, nt, 1, tile), _F32),
                     jax.ShapeDtypeStruct((bsz, N_HEADS, nt, 1, LANES), _F32)]
    return pl.pallas_call(
        functools.partial(_proj_kernel, meta=meta),
        grid=(bsz, nt),
        in_specs=in_specs,
        out_specs=out_specs,
        out_shape=out_shape,
        compiler_params=pltpu.CompilerParams(
            dimension_semantics=("parallel", "parallel"), vmem_limit_bytes=VMEM_LIMIT),
        name="proj_meta" if meta else "proj",
    )(*args)


def _attn_finish(accs, o_ref):
    halves = [(acc * (1.0 / acc[ONES_LANE:ONES_LANE + 1, :]))[:V_HEAD_DIM] for acc in accs]
    o_ref[...] = jnp.concatenate(halves, axis=0).T


def _lane_concat(ref, hh):
    return jnp.concatenate([ref[hh, j] for j in range(ref.shape[1])], axis=1)


def _attn_bounded_kernel(qt_ref, mrow_ref, k_ref, vt_ref, km_ref, vtm_ref, o_ref, *, nk, tk):
    tq = o_ref.shape[0]
    per_slab = vt_ref.shape[3] // tk
    key_row = lax.broadcasted_iota(jnp.int32, (LANES, tq), 0)
    accs = []
    for hh in range(2):
        qt = _lane_concat(qt_ref, hh)
        mrow = _lane_concat(mrow_ref, hh)
        s0 = jnp.where(key_row < N_META, _dot(km_ref[hh], qt), -jnp.inf)
        acc = _dot(vtm_ref[hh, :V_ROWS, :], jnp.exp2(s0 - mrow).astype(_BF))
        scores = lambda c: _dot(k_ref[hh, c * tk:(c + 1) * tk, :], qt)
        s_next = scores(0)
        for c in range(nk):
            s = s_next
            if c + 1 < nk:
                s_next = scores(c + 1)
            vt_c = vt_ref[hh, c // per_slab, :V_ROWS, (c % per_slab) * tk:(c % per_slab + 1) * tk]
            acc = acc + _dot(vt_c, jnp.exp2(s - mrow).astype(_BF))
        accs.append(acc)
    _attn_finish(accs, o_ref)


def _attn_online_kernel(qt_ref, k_ref, vt_ref, km_ref, vtm_ref, o_ref, s0_scr, s1_scr, m_scr, acc_scr, *, nk, tk):
    tq = o_ref.shape[0]
    key_row = lax.broadcasted_iota(jnp.int32, (LANES, tq), 0)
    for hh in range(2):
        s0 = jnp.where(key_row < N_META, _dot(km_ref[hh], _lane_concat(qt_ref, hh)), -jnp.inf)
        m0 = jnp.max(s0, axis=0, keepdims=True)
        m_scr[hh] = m0
        acc_scr[hh] = _dot(vtm_ref[hh, :V_ROWS, :], jnp.exp2(s0 - m0).astype(_BF))

    def scores(buf, c):
        off = pl.multiple_of(c * tk, tk)
        for hh in range(2):
            buf[hh] = _dot(k_ref[hh, pl.ds(off, tk), :], _lane_concat(qt_ref, hh))

    def accumulate(buf, c):
        for hh in range(2):
            s = buf[hh]
            m = m_scr[hh]
            m_new = jnp.maximum(m, jnp.max(s, axis=0, keepdims=True))
            m_scr[hh] = m_new
            p = jnp.exp2(s - m_new).astype(_BF)
            acc_scr[hh] = jnp.exp2(m - m_new) * acc_scr[hh] + _dot(vt_ref[hh, c, :V_ROWS, :], p)

    scores(s0_scr, 0)

    def body(t, _):
        scores(s1_scr, 2 * t + 1)
        accumulate(s0_scr, 2 * t)
        scores(s0_scr, 2 * t + 2)
        accumulate(s1_scr, 2 * t + 1)
        return 0

    lax.fori_loop(0, nk // 2 - 1, body, 0)
    scores(s1_scr, nk - 1)
    accumulate(s0_scr, nk - 2)
    accumulate(s1_scr, nk - 1)
    _attn_finish([acc_scr[0], acc_scr[1]], o_ref)


def _attn_call(qt, mrow, k, vt, km, vtm, *, bounded):
    bsz, _, nslab, _, slab = qt.shape
    seq = k.shape[2]
    tq = min(ATTN_TQ, seq)
    tk = min(ATTN_TK, slab)
    qs = tq // slab
    assert tq % slab == 0 and seq % tq == 0 and slab % tk == 0
    q_spec = lambda rows: pl.BlockSpec((None, 2, qs, rows, slab), lambda b, hp, i: (b, hp, i, 0, 0))
    kv_specs = [pl.BlockSpec((None, 2, seq, HEAD_PAD), lambda b, hp, i: (b, hp, 0, 0)),
                pl.BlockSpec((None, 2, nslab, HEAD_PAD, slab), lambda b, hp, i: (b, hp, 0, 0, 0)),
                pl.BlockSpec((2, LANES, HEAD_PAD), lambda b, hp, i: (hp, 0, 0)),
                pl.BlockSpec((2, HEAD_PAD, LANES), lambda b, hp, i: (hp, 0, 0))]
    if bounded:
        body = functools.partial(_attn_bounded_kernel, nk=seq // tk, tk=tk)
        in_specs = [q_spec(HEAD_PAD), q_spec(1)] + kv_specs
        args, scratch = (qt, mrow, k, vt, km, vtm), []
    else:
        assert nslab % 2 == 0 and nslab >= 4
        body = functools.partial(_attn_online_kernel, nk=nslab, tk=slab)
        in_specs = [q_spec(HEAD_PAD)] + kv_specs
        args = (qt, k, vt, km, vtm)
        scratch = [pltpu.VMEM((2, slab, tq), _F32), pltpu.VMEM((2, slab, tq), _F32),
                   pltpu.VMEM((2, 1, tq), _F32), pltpu.VMEM((2, V_ROWS, tq), _F32)]
    return pl.pallas_call(
        body,
        grid=(bsz, N_HEADS // 2, seq // tq),
        in_specs=in_specs,
        out_specs=pl.BlockSpec((None, tq, LANES), lambda b, hp, i: (b, i, hp)),
        out_shape=jax.ShapeDtypeStruct((bsz, seq, ATTN_WIDTH), _F32),
        scratch_shapes=scratch,
        compiler_params=pltpu.CompilerParams(
            dimension_semantics=("parallel", "parallel", "arbitrary"), vmem_limit_bytes=VMEM_LIMIT),
        name="attn_bounded" if bounded else "attn_online",
    )(*args)


def _cmul_add(ar, ai, xr, xi, sr, si):
    return ar * xr - ai * xi + sr, ar * xi + ai * xr + si


def _s5_kernel(a_ref, um_ref, q_ref, bt_ref, c0_ref, d_ref, lam_ref, b_ref, c_ref, y_ref, sup_scr, ent_scr,
               *, nslab, bsz):
    rows = nslab * bsz
    half = STATE_W // 2
    a = [a_ref[pl.ds(i, rows, stride=SSM_GROUP), :] for i in range(SSM_GROUP)]
    uc = [jnp.concatenate([x[:, c * SSM_CHUNK:(c + 1) * SSM_CHUNK] for x in a], axis=1).astype(_BF)
          for c in range(SLAB_CHUNKS)]
    w, vf, v, t = _s5_operators(lam_ref, b_ref, c_ref)
    s = [_dot(u, w) for u in uc]
    sr = [x[:, :half] for x in s]
    si = [x[:, half:] for x in s]
    trow = lambda r: t[r:r + 1, :]
    sup_r = sup_i = None
    for c in range(SLAB_CHUNKS):
        cr, ci = trow(c), trow(SLAB_CHUNKS + c)
        pr = cr * sr[c] - ci * si[c]
        pi = cr * si[c] + ci * sr[c]
        sup_r = pr if sup_r is None else sup_r + pr
        sup_i = pi if sup_i is None else sup_i + pi
    sup_scr[:, :half] = sup_r
    sup_scr[:, half:] = sup_i

    lane = lax.broadcasted_iota(jnp.int32, (bsz, half), 1)
    fwd = lane < SSM_STATE
    sm = _dot(um_ref[...], w)
    xr = jnp.where(fwd, sm[:bsz, :half], 0.0)
    xi = jnp.where(fwd, sm[:bsz, half:], 0.0)
    a_slab_r, a_slab_i = trow(2 * SLAB_CHUNKS), trow(2 * SLAB_CHUNKS + 1)
    for j in range(nslab):
        rf = j * bsz
        rb = (nslab - 1 - j) * bsz
        ent_scr[rf:rf + bsz, 0:SSM_STATE] = xr[:, 0:SSM_STATE]
        ent_scr[rb:rb + bsz, SSM_STATE:half] = xr[:, SSM_STATE:half]
        ent_scr[rf:rf + bsz, half:half + SSM_STATE] = xi[:, 0:SSM_STATE]
        ent_scr[rb:rb + bsz, half + SSM_STATE:STATE_W] = xi[:, SSM_STATE:half]
        s_r = jnp.where(fwd, sup_scr[rf:rf + bsz, :half], sup_scr[rb:rb + bsz, :half])
        s_i = jnp.where(fwd, sup_scr[rf:rf + bsz, half:], sup_scr[rb:rb + bsz, half:])
        xr, xi = _cmul_add(a_slab_r, a_slab_i, xr, xi, s_r, s_i)

    ent = ent_scr[...]
    a_r, a_i = trow(2 * SLAB_CHUNKS + 2), trow(2 * SLAB_CHUNKS + 3)
    xf = [(ent[:, :half], ent[:, half:])]
    for c in range(1, SLAB_CHUNKS):
        xf.append(_cmul_add(a_r, a_i, xf[-1][0], xf[-1][1], sr[c - 1], si[c - 1]))
    xb = [(ent[:, :half], ent[:, half:])]
    for c in range(SLAB_CHUNKS - 2, -1, -1):
        xb.insert(0, _cmul_add(a_r, a_i, xb[0][0], xb[0][1], sr[c + 1], si[c + 1]))
    fwd_rows = lax.broadcasted_iota(jnp.int32, (rows, half), 1) < SSM_STATE
    hi = lax.Precision.HIGHEST
    dirs = [jnp.concatenate([vf[:, d * SSM_STATE:(d + 1) * SSM_STATE],
                             vf[:, half + d * SSM_STATE:half + (d + 1) * SSM_STATE]], axis=1) for d in range(2)]
    nt_hi = lambda a, b: lax.dot_general(a, b, _NT, precision=hi, preferred_element_type=_F32)
    lag_f = nt_hi(bt_ref[0], dirs[0])
    lag_b = nt_hi(bt_ref[1], dirs[1])
    center = (jnp.dot(bt_ref[0], c0_ref[0], precision=hi, preferred_element_type=_F32)
              + jnp.dot(bt_ref[1], c0_ref[1], precision=hi, preferred_element_type=_F32) + d_ref[...])
    lane = lax.broadcasted_iota(jnp.int32, (SSM_GROUP, CHUNK_W), 1)
    lag = jnp.concatenate([center, lag_f[:, :CHUNK_W - SSM_GROUP], jnp.where(lane < SSM_GROUP, 0.0, lag_b)], axis=1)
    m = jnp.concatenate(
        [pltpu.roll(jnp.broadcast_to(lag[i:i + 1, :], (SSM_CHUNK, 2 * CHUNK_W)), 0, 1,
                    stride=SSM_GROUP, stride_axis=0)[:, :CHUNK_W] for i in range(SSM_GROUP)], axis=0).astype(_BF)
    m = _dot(m, q_ref[...]).astype(_BF)
    ys = []
    for c in range(SLAB_CHUNKS):
        xin = jnp.concatenate([jnp.where(fwd_rows, xf[c][0], xb[c][0]),
                               jnp.where(fwd_rows, xf[c][1], xb[c][1])], axis=1).astype(_BF)
        ys.append(_dot(uc[c], m)
                  + lax.dot_general(xin, v, _NT, preferred_element_type=_F32))
    for o in range(SSM_GROUP):
        y_ref[pl.ds(o, rows, stride=SSM_GROUP), :] = jnp.concatenate(
            [y[:, o * SSM_CHUNK:(o + 1) * SSM_CHUNK] for y in ys], axis=1)


def _s5_call(a, um, perm, bt, c0, dmat, lam_tab, b_tab, c_tab, *, nslab, bsz):
    n = nslab * bsz * SSM_GROUP
    rows = nslab * bsz
    g_spec = lambda *shape: pl.BlockSpec((None,) + shape, lambda g: (g,) + (0,) * len(shape))
    perm_spec = pl.BlockSpec((CHUNK_W, CHUNK_W), lambda g: (0, 0), pipeline_mode=pl.Buffered(1))
    return pl.pallas_call(
        functools.partial(_s5_kernel, nslab=nslab, bsz=bsz),
        grid=(SSM_GROUPS,),
        in_specs=[g_spec(n, SLAB_T), g_spec(SUBLANES, CHUNK_W), perm_spec,
                  g_spec(2, SSM_GROUP, 2 * SSM_STATE), g_spec(2, 2 * SSM_STATE, SSM_GROUP), g_spec(SSM_GROUP, SSM_GROUP),
                  g_spec(SUBLANES, 2 * SSM_STATE), g_spec(2, SSM_GROUP, 2 * SSM_STATE),
                  g_spec(2, SSM_GROUP, 2 * SSM_STATE)],
        out_specs=g_spec(n, SLAB_T),
        out_shape=jax.ShapeDtypeStruct((SSM_GROUPS, n, SLAB_T), _F32),
        scratch_shapes=[pltpu.VMEM((rows, STATE_W), _F32), pltpu.VMEM((rows, STATE_W), _F32)],
        compiler_params=pltpu.CompilerParams(
            dimension_semantics=("parallel",), vmem_limit_bytes=VMEM_LIMIT),
        name="s5",
    )(a, um, perm, bt, c0, dmat, lam_tab, b_tab, c_tab)


def _s5_operators(lam_ref, b_ref, c_ref):
    tc = SSM_CHUNK
    lr, li = lam_ref[0:1, :], lam_ref[1:2, :]

    def power(k):
        mag = jnp.exp(k * lr)
        return mag * jnp.cos(k * li), mag * jnp.sin(k * li)

    s = lax.broadcasted_iota(jnp.int32, (tc, 2 * SSM_STATE), 0)
    fwd = lax.broadcasted_iota(jnp.int32, (tc, 2 * SSM_STATE), 1) < SSM_STATE
    pr, pi = power(jnp.where(fwd, tc - 1 - s, s).astype(_F32))
    gr, gi = power(jnp.where(fwd, s + 1, tc - s).astype(_F32))
    w, vc, vt = [], [], []
    for h in range(SSM_GROUP):
        br, bi = b_ref[0, h:h + 1, :], b_ref[1, h:h + 1, :]
        w.append(jnp.concatenate([br * pr - bi * pi, br * pi + bi * pr], axis=1).astype(_BF))
        cr, ci = c_ref[0, h:h + 1, :], c_ref[1, h:h + 1, :]
        vc.append(jnp.concatenate([cr * gr - ci * gi, -(cr * gi + ci * gr)], axis=1).astype(_BF))
    cr, ci = c_ref[0], c_ref[1]
    for t in range(tc):
        tr, ti = gr[t:t + 1, :], gi[t:t + 1, :]
        vt.append(jnp.concatenate([cr * tr - ci * ti, -(cr * ti + ci * tr)], axis=1))
    r = lax.broadcasted_iota(jnp.int32, (2 * SUBLANES, 2 * SSM_STATE), 0)
    f = lax.broadcasted_iota(jnp.int32, (2 * SUBLANES, 2 * SSM_STATE), 1) < SSM_STATE
    c = jnp.where(r < SLAB_CHUNKS, r, r - SLAB_CHUNKS)
    n = jnp.where(r < 2 * SLAB_CHUNKS, jnp.where(f, SLAB_CHUNKS - 1 - c, c),
                  jnp.where(r < 2 * SLAB_CHUNKS + 2, SLAB_CHUNKS, 1))
    nr, ni = power((n * tc).astype(_F32))
    is_re = (r < SLAB_CHUNKS) | (r == 2 * SLAB_CHUNKS) | (r == 2 * SLAB_CHUNKS + 2)
    tab = jnp.where(r < 2 * SLAB_CHUNKS + 4, jnp.where(is_re, nr, ni), 0.0)
    return jnp.concatenate(w, axis=0), jnp.concatenate(vt, axis=0), jnp.concatenate(vc, axis=0), tab


def _s5_matrices(a_re, a_im, log_dt, b_re, b_im, c_re, c_im, d_skip):
    lam = lax.complex(jnp.minimum(a_re.astype(_F32), -1e-4), a_im.astype(_F32))
    dt = jnp.exp(log_dt.astype(_F32))[..., None]
    lam_dt = lam * dt
    lam_bar = jnp.exp(lam_dt)
    b_bar = ((lam_bar - 1.0) / lam)[..., None] * lax.complex(b_re.astype(_F32), b_im.astype(_F32))
    c_c = lax.complex(c_re.astype(_F32), c_im.astype(_F32))
    bt = jnp.concatenate([jnp.real(b_bar), jnp.imag(b_bar)], axis=2).transpose(1, 0, 3, 2)
    c0 = jnp.concatenate([jnp.real(c_c), -jnp.imag(c_c)], axis=3).transpose(1, 0, 3, 2)
    dmat = jnp.eye(SSM_GROUP, dtype=_F32)[None] * d_skip.astype(_F32).reshape(SSM_GROUPS, 1, SSM_GROUP)
    lanes = lambda z, perm: z.transpose(perm).reshape(SSM_GROUPS, -1, 2 * SSM_STATE)
    split = lambda z: jnp.stack([jnp.real(z), jnp.imag(z)], axis=1)
    lam_tab = jnp.pad(split(lanes(lam_dt, (1, 0, 2)))[:, :, 0], ((0, 0), (0, SUBLANES - 2), (0, 0)))
    b_tab = split(lanes(b_bar, (1, 3, 0, 2)))
    c_tab = split(lanes(c_c, (1, 2, 0, 3)))
    return bt, c0, dmat, lam_tab, b_tab, c_tab


def _chunk_permutation():
    r = lax.broadcasted_iota(jnp.int32, (CHUNK_W, CHUNK_W), 0)
    c = lax.broadcasted_iota(jnp.int32, (CHUNK_W, CHUNK_W), 1)
    return (r == (c % SSM_CHUNK) * SSM_GROUP + c // SSM_CHUNK).astype(_BF)


def _post_kernel(x_ref, attn_ref, y_ref, wglu_ref, gmix_ref, wout_ref, gpm_ref, gpre_ref,
                 wup_ref, wdn_ref, gpost_ref, o_ref):
    nslab = y_ref.shape[1]
    gmix = gmix_ref[...]
    per = min(POST_SLABS, nslab)
    blocks = range(nslab // per)
    rows = [pl.ds(r * per * SLAB_T, per * SLAB_T) for r in blocks]
    gy = []
    for r in blocks:
        yt = jnp.concatenate([jnp.concatenate([y_ref[g, r * per + c] for c in range(per)], axis=1)
                              for g in range(SSM_GROUPS)], axis=0)
        y = yt.T
        gy.append((0.5 * y * (1.0 + jnp.tanh(math.sqrt(2.0 / math.pi) * (y + 0.044715 * (y * y * y))))).astype(_BF))
    z = [_dot(gy[r], wglu_ref[...]) for r in blocks]
    mix = []
    for r in blocks:
        ssm = z[r][:, :SSM_WIDTH] * (1.0 / (1.0 + jnp.exp(-z[r][:, SSM_WIDTH:])))
        mix.append(jnp.concatenate([_rms(attn_ref[rows[r], :], gmix[:, :ATTN_WIDTH]),
                                    _rms(ssm, gmix[:, ATTN_WIDTH:])], axis=-1).astype(_BF))
    mixed = [_dot(mix[r], wout_ref[...]) for r in blocks]
    h1 = [x_ref[rows[r], :] + _rms(mixed[r], gpm_ref[...]) for r in blocks]
    hn = [_rms(h1[r], gpre_ref[...]).astype(_BF) for r in blocks]
    acc = [None for _ in blocks]
    for c in range(D_FF // FF_TILE):
        for r in blocks:
            up = jnp.maximum(_dot(hn[r], wup_ref[:, c * FF_TILE:(c + 1) * FF_TILE]), 0.0)
            part = _dot((up * up).astype(_BF), wdn_ref[c * FF_TILE:(c + 1) * FF_TILE, :])
            acc[r] = part if acc[r] is None else acc[r] + part
    for r in blocks:
        o_ref[rows[r], :] = h1[r] + _rms(acc[r], gpost_ref[...])


def _post_call(x, attn, y, wglu, gmix, wout, gpm, gpre, wup, wdn, gpost, *, tile):
    bsz, seq, _ = x.shape
    row_spec = lambda w: pl.BlockSpec((None, tile, w), lambda b, i: (b, i, 0))
    wspec = lambda shape: pl.BlockSpec(shape, lambda b, i: (0, 0), pipeline_mode=pl.Buffered(1))
    y_spec = pl.BlockSpec((SSM_GROUPS, tile // SLAB_T, SSM_GROUP, SLAB_T), lambda b, i: (0, i, b, 0))
    return pl.pallas_call(
        _post_kernel,
        grid=(bsz, seq // tile),
        in_specs=[row_spec(D_MODEL), row_spec(ATTN_WIDTH), y_spec,
                  wspec((SSM_WIDTH, 2 * SSM_WIDTH)), wspec((1, D_MODEL)), wspec((D_MODEL, D_MODEL)),
                  wspec((1, D_MODEL)), wspec((1, D_MODEL)), wspec((D_MODEL, D_FF)),
                  wspec((D_FF, D_MODEL)), wspec((1, D_MODEL))],
        out_specs=row_spec(D_MODEL),
        out_shape=jax.ShapeDtypeStruct((bsz, seq, D_MODEL), _F32),
        compiler_params=pltpu.CompilerParams(
            dimension_semantics=("parallel", "parallel"), vmem_limit_bytes=VMEM_LIMIT),
        name="post",
    )(x, attn, y, wglu, gmix, wout, gpm, gpre, wup, wdn, gpost)


def _rope_tables(pos, tile):
    half = QK_ROPE_DIM // 2
    inv = 1.0 / (ROPE_BASE ** (jnp.arange(0, QK_ROPE_DIM, 2, dtype=_F32) / QK_ROPE_DIM))
    ang = pos.astype(_F32)[:, None, :] * inv[None, :, None]
    bsz, seq = pos.shape
    rope = jnp.stack([jnp.cos(ang), jnp.sin(ang)], axis=1)
    return rope.reshape(bsz, 2, half, seq // tile, tile).transpose(0, 3, 1, 2, 4)


def _prep_weights(w_in, w_uq, w_ukv):
    scale = QK_HEAD_DIM ** -0.5 * math.log2(math.e)
    win = w_in[:, :OFF_KR].astype(_BF)
    wkr_t = w_in[:, OFF_KR:OFF_U].T.astype(_BF)
    wu_t = w_in[:, OFF_U:].T.astype(_BF)
    wq_t = (w_uq * scale).T.astype(_BF)
    wkv3 = w_ukv.reshape(KV_LORA_RANK, N_HEADS, QK_NOPE_DIM + V_HEAD_DIM)
    wk = jnp.concatenate([wkv3[..., :QK_NOPE_DIM],
                          jnp.zeros((KV_LORA_RANK, N_HEADS, HEAD_PAD - QK_NOPE_DIM), _F32)], axis=-1)
    wk = wk.reshape(KV_LORA_RANK, N_HEADS * HEAD_PAD).astype(_BF)
    wv_t = wkv3[..., QK_NOPE_DIM:].reshape(KV_LORA_RANK, N_HEADS * V_HEAD_DIM).T.astype(_BF)
    return win, wkr_t, wu_t, wq_t, wk, wv_t


def kernel(x, positions, meta_tokens, g_pre_mix, w_in, g_q_lat, w_uq, g_kv_lat, w_ukv,
           ssm_A_re, ssm_A_im, ssm_log_dt, ssm_B_re, ssm_B_im, ssm_C_re, ssm_C_im, ssm_D,
           w_glu, g_mix_out, w_out, g_post_mix, g_pre_mlp, w_mlp_up, w_mlp_down, g_post_mlp):
    bsz, seq, _ = x.shape
    assert seq % ROW_TILE == 0 and ROW_TILE % SLAB_T == 0 and bsz <= SUBLANES
    assert seq % PROJ_TILE == 0 and PROJ_TILE % SLAB_T == 0
    assert N_META <= SSM_CHUNK
    row = lambda g: g.reshape(1, -1).astype(_F32)

    win, wkr_t, wu_t, wq_t, wk, wv_t = _prep_weights(w_in[0], w_uq[0], w_ukv[0])
    weights = (row(g_pre_mix[0]), win, wkr_t, wu_t, row(g_q_lat[0]), wq_t, row(g_kv_lat[0]), wk, wv_t)
    rope = _rope_tables(positions.astype(jnp.int32) + N_META, PROJ_TILE)
    qt, k, vt, u, q_norm, k_sq = _proj_call(x, rope, *weights, tile=PROJ_TILE, meta=False)
    meta_x = jnp.pad(meta_tokens.astype(x.dtype), ((0, LANES - N_META), (0, 0)))[None]
    rope_m = _rope_tables(jnp.arange(LANES, dtype=jnp.int32)[None], LANES)
    k_m, vt_m, u_m = _proj_call(meta_x, rope_m, *weights, tile=LANES, meta=True)
    km, vtm, u_m = k_m[0], vt_m[0, :, 0], u_m[:, :N_META]
    k_m_sq = jnp.max(jnp.sum(jnp.square(k_m[0].astype(_F32)), axis=-1), axis=-1)
    k_max = jnp.sqrt(jnp.maximum(jnp.max(k_sq, axis=(2, 3, 4)), k_m_sq[None]))
    mrow = q_norm * (k_max * BOUND_SLACK)[:, :, None, None, None]
    attn = lax.cond(jnp.max(mrow) <= BOUND_LIMIT,
                    lambda: _attn_call(qt, mrow, k, vt, km, vtm, bounded=True),
                    lambda: _attn_call(qt, mrow, k, vt, km, vtm, bounded=False))

    um = u_m[0].astype(_BF).reshape(N_META, SSM_GROUPS, SSM_GROUP).transpose(1, 2, 0)
    um = jnp.pad(um, ((0, 0), (0, 0), (SSM_CHUNK - N_META, 0))).reshape(SSM_GROUPS, 1, CHUNK_W)
    um = jnp.broadcast_to(um, (SSM_GROUPS, SUBLANES, CHUNK_W))
    ops = _s5_matrices(ssm_A_re[0], ssm_A_im[0], ssm_log_dt[0], ssm_B_re[0],
                       ssm_B_im[0], ssm_C_re[0], ssm_C_im[0], ssm_D[0])
    nslab = seq // SLAB_T
    yg = _s5_call(u.reshape(SSM_GROUPS, nslab * bsz * SSM_GROUP, SLAB_T), um, _chunk_permutation(),
                  *ops, nslab=nslab, bsz=bsz)
    y = yg.reshape(SSM_GROUPS, nslab, bsz * SSM_GROUP, SLAB_T)

    return _post_call(x, attn, y, w_glu[0].astype(_BF), row(g_mix_out[0]), w_out[0].astype(_BF),
                      row(g_post_mix[0]), row(g_pre_mlp[0]), w_mlp_up[0].astype(_BF),
                      w_mlp_down[0].astype(_BF), row(g_post_mlp[0]), tile=ROW_TILE)
```

```python
import functools
import math

import jax
import jax.numpy as jnp
from jax import lax
from jax.experimental import pallas as pl
from jax.experimental.pallas import tpu as pltpu

D_MODEL = 1024
N_META = 16
ATTN_WIDTH = 512
SSM_WIDTH = 512
N_HEADS = 8
V_HEAD_DIM = 64
QK_NOPE_DIM = 64
QK_ROPE_DIM = 32
QK_HEAD_DIM = QK_NOPE_DIM + QK_ROPE_DIM
Q_LORA_RANK = 384
KV_LORA_RANK = 256
ROPE_BASE = 10000.0
SSM_GROUP = 16
SSM_GROUPS = 32
SSM_STATE = 64
D_FF = 4 * D_MODEL
EPS = 1e-6
OFF_KR = Q_LORA_RANK + KV_LORA_RANK
OFF_U = OFF_KR + QK_ROPE_DIM

LANES = 128
SUBLANES = 8
HEAD_PAD = LANES
ONES_LANE = V_HEAD_DIM
ATTN_TQ = 1024
ATTN_TK = 256
V_ROWS = 112
PW_Q = 0
PW_KV = PW_Q + Q_LORA_RANK
PW_END = PW_KV + KV_LORA_RANK

SSM_CHUNK = 32
CHUNK_W = SSM_CHUNK * SSM_GROUP
STATE_W = 4 * SSM_STATE
SLAB_T = LANES
SLAB_CHUNKS = SLAB_T // SSM_CHUNK

ROW_TILE = 512
PROJ_TILE = 1024
FF_TILE = 1024
POST_SLABS = 2
BOUND_SLACK = 1.0 + 2.0 ** -6
BOUND_LIMIT = 40.0
V7X_VMEM_BYTES = 64 * 1024 * 1024
VMEM_LIMIT = V7X_VMEM_BYTES - 8 * 1024 * 1024

_BF = jnp.bfloat16
_F32 = jnp.float32


def _dot(a, b):
    return jnp.dot(a, b, preferred_element_type=_F32)


def _rms(x, g):
    return x * lax.rsqrt(jnp.mean(x * x, axis=-1, keepdims=True) + EPS) * g


_NT = (((1,), (1,)), ((), ()))


def _rotate(x1, x2, cos_t, sin_t):
    return x1 * cos_t - x2 * sin_t, x1 * sin_t + x2 * cos_t


def _proj_kernel(x_ref, rope_ref, *refs, meta):
    if meta:
        gpre_ref, win_ref, wkr_ref, wu_ref, gkv_ref, wk_ref, wv_ref, k_ref, v_ref, u_ref = refs
    else:
        (gpre_ref, win_ref, wkr_ref, wu_ref, gq_ref, wq_ref, gkv_ref, wk_ref, wv_ref,
         q_ref, k_ref, v_ref, u_ref, qn_ref, kmx_ref) = refs
    tile = x_ref.shape[0]
    half = QK_ROPE_DIM // 2
    cos_t, sin_t = rope_ref[0], rope_ref[1]
    xn = _rms(x_ref[...], gpre_ref[...]).astype(_BF)
    proj = _dot(xn, win_ref[...])
    kvn = _rms(proj[:, PW_KV:PW_END], gkv_ref[...]).astype(_BF)
    krt = lax.dot_general(wkr_ref[...], xn, _NT, preferred_element_type=_F32)
    r1, r2 = _rotate(krt[:half], krt[half:], cos_t, sin_t)
    kr = jnp.concatenate([jnp.zeros((QK_NOPE_DIM, tile), _F32), r1, r2,
                          jnp.zeros((HEAD_PAD - QK_HEAD_DIM, tile), _F32)], axis=0).T
    kk = _dot(kvn, wk_ref[...])
    vt = lax.dot_general(wv_ref[...], kvn, _NT, preferred_element_type=_F32)
    ones_tail = (lax.broadcasted_iota(jnp.int32, (HEAD_PAD - V_HEAD_DIM, tile), 0) == 0).astype(_F32)
    if meta:
        u_ref[...] = lax.dot_general(xn, wu_ref[...], _NT, preferred_element_type=_F32)
    else:
        ut = lax.dot_general(wu_ref[...], xn, _NT, preferred_element_type=_F32)
        for g in range(SSM_GROUPS):
            for c in range(tile // SLAB_T):
                u_ref[g, c] = ut[g * SSM_GROUP:(g + 1) * SSM_GROUP, c * SLAB_T:(c + 1) * SLAB_T]
        qn = _rms(proj[:, PW_Q:PW_KV], gq_ref[...]).astype(_BF)
        qt = lax.dot_general(wq_ref[...], qn, _NT, preferred_element_type=_F32)
        zero_rows = jnp.zeros((HEAD_PAD - QK_HEAD_DIM, tile), _F32)
    for h in range(N_HEADS):
        k_h = (kk[:, h * HEAD_PAD:(h + 1) * HEAD_PAD] + kr).astype(_BF)
        k_ref[h] = k_h
        v_ref[h] = jnp.concatenate([vt[h * V_HEAD_DIM:(h + 1) * V_HEAD_DIM], ones_tail], axis=0).astype(_BF)
        if not meta:
            blk = qt[h * QK_HEAD_DIM:(h + 1) * QK_HEAD_DIM]
            r1, r2 = _rotate(blk[QK_NOPE_DIM:QK_NOPE_DIM + half], blk[QK_NOPE_DIM + half:], cos_t, sin_t)
            qt_h = jnp.concatenate([blk[:QK_NOPE_DIM], r1, r2, zero_rows], axis=0).astype(_BF)
            q_ref[h] = qt_h
            qt_f = qt_h.astype(_F32)
            qn_ref[h] = jnp.sqrt(jnp.sum(qt_f * qt_f, axis=0, keepdims=True))
            k_f = k_h.astype(_F32)
            kmx_ref[h] = jnp.broadcast_to(jnp.max(jnp.sum(k_f * k_f, axis=1, keepdims=True), axis=0, keepdims=True),
                                          (1, LANES))


def _const_spec(shape):
    nd = len(shape)
    return pl.BlockSpec(shape, lambda *_: (0,) * nd)


def _proj_call(x, rope, gpre, win, wkr_t, wu_t, gq, wq_t, gkv, wk, wv_t, *, tile, meta):
    bsz, seq, _ = x.shape
    nt = seq // tile
    row_spec = lambda w: pl.BlockSpec((None, tile, w), lambda b, i: (b, i, 0))
    rope_spec = pl.BlockSpec((None, None, 2, QK_ROPE_DIM // 2, tile), lambda b, i: (b, i, 0, 0, 0))
    k_spec = pl.BlockSpec((None, N_HEADS, tile, HEAD_PAD), lambda b, i: (b, 0, i, 0))
    t_spec = pl.BlockSpec((None, N_HEADS, None, HEAD_PAD, tile), lambda b, i: (b, 0, i, 0, 0))
    k_shape = jax.ShapeDtypeStruct((bsz, N_HEADS, seq, HEAD_PAD), _BF)
    t_shape = jax.ShapeDtypeStruct((bsz, N_HEADS, nt, HEAD_PAD, tile), _BF)
    w_specs = lambda *ws: [_const_spec(w.shape) for w in ws]
    if meta:
        args = (x, rope, gpre, win, wkr_t, wu_t, gkv, wk, wv_t)
        in_specs = [row_spec(D_MODEL), rope_spec] + w_specs(*args[2:])
        out_specs = [k_spec, t_spec, row_spec(SSM_WIDTH)]
        out_shape = [k_shape, t_shape, jax.ShapeDtypeStruct((bsz, seq, SSM_WIDTH), _F32)]
    else:
        args = (x, rope, gpre, win, wkr_t, wu_t, gq, wq_t, gkv, wk, wv_t)
        in_specs = [row_spec(D_MODEL), rope_spec] + w_specs(*args[2:])
        norm_spec = lambda w: pl.BlockSpec((None, N_HEADS, None, 1, w), lambda b, i: (b, 0, i, 0, 0))
        u_spec = pl.BlockSpec((SSM_GROUPS, tile // SLAB_T, SSM_GROUP, SLAB_T), lambda b, i: (0, i, b, 0))
        out_specs = [t_spec, k_spec, t_spec, u_spec, norm_spec(tile), norm_spec(LANES)]
        out_shape = [t_shape, k_shape, t_shape,
                     jax.ShapeDtypeStruct((SSM_GROUPS, seq // SLAB_T, bsz * SSM_GROUP, SLAB_T), _F32),
                     jax.ShapeDtypeStruct((bsz, N_HEADS, nt, 1, tile), _F32),
                     jax.ShapeDtypeStruct((bsz, N_HEADS, nt, 1, LANES), _F32)]
    return pl.pallas_call(
        functools.partial(_proj_kernel, meta=meta),
        grid=(bsz, nt),
        in_specs=in_specs,
        out_specs=out_specs,
        out_shape=out_shape,
        compiler_params=pltpu.CompilerParams(
            dimension_semantics=("parallel", "parallel"), vmem_limit_bytes=VMEM_LIMIT),
        name="proj_meta" if meta else "proj",
    )(*args)


def _attn_finish(accs, o_ref):
    halves = [(acc * (1.0 / acc[ONES_LANE:ONES_LANE + 1, :]))[:V_HEAD_DIM] for acc in accs]
    o_ref[...] = jnp.concatenate(halves, axis=0).T


def _lane_concat(ref, hh):
    return jnp.concatenate([ref[hh, j] for j in range(ref.shape[1])], axis=1)


def _attn_bounded_kernel(qt_ref, mrow_ref, k_ref, vt_ref, km_ref, vtm_ref, o_ref, *, nk, tk):
    tq = o_ref.shape[0]
    per_slab = vt_ref.shape[3] // tk
    key_row = lax.broadcasted_iota(jnp.int32, (LANES, tq), 0)
    accs = []
    for hh in range(2):
        qt = _lane_concat(qt_ref, hh)
        mrow = _lane_concat(mrow_ref, hh)
        s0 = jnp.where(key_row < N_META, _dot(km_ref[hh], qt), -jnp.inf)
        acc = _dot(vtm_ref[hh, :V_ROWS, :], jnp.exp2(s0 - mrow).astype(_BF))
        scores = lambda c: _dot(k_ref[hh, c * tk:(c + 1) * tk, :], qt)
        s_next = scores(0)
        for c in range(nk):
            s = s_next
            if c + 1 < nk:
                s_next = scores(c + 1)
            vt_c = vt_ref[hh, c // per_slab, :V_ROWS, (c % per_slab) * tk:(c % per_slab + 1) * tk]
            acc = acc + _dot(vt_c, jnp.exp2(s - mrow).astype(_BF))
        accs.append(acc)
    _attn_finish(accs, o_ref)


def _attn_online_kernel(qt_ref, k_ref, vt_ref, km_ref, vtm_ref, o_ref, s0_scr, s1_scr, m_scr, acc_scr, *, nk, tk):
    tq = o_ref.shape[0]
    key_row = lax.broadcasted_iota(jnp.int32, (LANES, tq), 0)
    for hh in range(2):
        s0 = jnp.where(key_row < N_META, _dot(km_ref[hh], _lane_concat(qt_ref, hh)), -jnp.inf)
        m0 = jnp.max(s0, axis=0, keepdims=True)
        m_scr[hh] = m0
        acc_scr[hh] = _dot(vtm_ref[hh, :V_ROWS, :], jnp.exp2(s0 - m0).astype(_BF))

    def scores(buf, c):
        off = pl.multiple_of(c * tk, tk)
        for hh in range(2):
            buf[hh] = _dot(k_ref[hh, pl.ds(off, tk), :], _lane_concat(qt_ref, hh))

    def accumulate(buf, c):
        for hh in range(2):
            s = buf[hh]
            m = m_scr[hh]
            m_new = jnp.maximum(m, jnp.max(s, axis=0, keepdims=True))
            m_scr[hh] = m_new
            p = jnp.exp2(s - m_new).astype(_BF)
            acc_scr[hh] = jnp.exp2(m - m_new) * acc_scr[hh] + _dot(vt_ref[hh, c, :V_ROWS, :], p)

    scores(s0_scr, 0)

    def body(t, _):
        scores(s1_scr, 2 * t + 1)
        accumulate(s0_scr, 2 * t)
        scores(s0_scr, 2 * t + 2)
        accumulate(s1_scr, 2 * t + 1)
        return 0

    lax.fori_loop(0, nk // 2 - 1, body, 0)
    scores(s1_scr, nk - 1)
    accumulate(s0_scr, nk - 2)
    accumulate(s1_scr, nk - 1)
    _attn_finish([acc_scr[0], acc_scr[1]], o_ref)


def _attn_call(qt, mrow, k, vt, km, vtm, *, bounded):
    bsz, _, nslab, _, slab = qt.shape
    seq = k.shape[2]
    tq = min(ATTN_TQ, seq)
    tk = min(ATTN_TK, slab)
    qs = tq // slab
    assert tq % slab == 0 and seq % tq == 0 and slab % tk == 0
    q_spec = lambda rows: pl.BlockSpec((None, 2, qs, rows, slab), lambda b, hp, i: (b, hp, i, 0, 0))
    kv_specs = [pl.BlockSpec((None, 2, seq, HEAD_PAD), lambda b, hp, i: (b, hp, 0, 0)),
                pl.BlockSpec((None, 2, nslab, HEAD_PAD, slab), lambda b, hp, i: (b, hp, 0, 0, 0)),
                pl.BlockSpec((2, LANES, HEAD_PAD), lambda b, hp, i: (hp, 0, 0)),
                pl.BlockSpec((2, HEAD_PAD, LANES), lambda b, hp, i: (hp, 0, 0))]
    if bounded:
        body = functools.partial(_attn_bounded_kernel, nk=seq // tk, tk=tk)
        in_specs = [q_spec(HEAD_PAD), q_spec(1)] + kv_specs
        args, scratch = (qt, mrow, k, vt, km, vtm), []
    else:
        assert nslab % 2 == 0 and nslab >= 4
        body = functools.partial(_attn_online_kernel, nk=nslab, tk=slab)
        in_specs = [q_spec(HEAD_PAD)] + kv_specs
        args = (qt, k, vt, km, vtm)
        scratch = [pltpu.VMEM((2, slab, tq), _F32), pltpu.VMEM((2, slab, tq), _F32),
                   pltpu.VMEM((2, 1, tq), _F32), pltpu.VMEM((2, V_ROWS, tq), _F32)]
    return pl.pallas_call(
        body,
        grid=(bsz, N_HEADS // 2, seq // tq),
        in_specs=in_specs,
        out_specs=pl.BlockSpec((None, tq, LANES), lambda b, hp, i: (b, i, hp)),
        out_shape=jax.ShapeDtypeStruct((bsz, seq, ATTN_WIDTH), _F32),
        scratch_shapes=scratch,
        compiler_params=pltpu.CompilerParams(
            dimension_semantics=("parallel", "parallel", "arbitrary"), vmem_limit_bytes=VMEM_LIMIT),
        name="attn_bounded" if bounded else "attn_online",
    )(*args)


def _cmul_add(ar, ai, xr, xi, sr, si):
    return ar * xr - ai * xi + sr, ar * xi + ai * xr + si


def _s5_kernel(a_ref, um_ref, q_ref, w_ref, vc_ref, lag_ref, t_ref, y_ref, sup_scr, ent_scr, *, nslab, bsz):
    rows = nslab * bsz
    half = STATE_W // 2
    a = [a_ref[pl.ds(i, rows, stride=SSM_GROUP), :] for i in range(SSM_GROUP)]
    uc = [jnp.concatenate([x[:, c * SSM_CHUNK:(c + 1) * SSM_CHUNK] for x in a], axis=1).astype(_BF)
          for c in range(SLAB_CHUNKS)]
    w, t = w_ref[...], t_ref[...]
    s = [_dot(u, w) for u in uc]
    sr = [x[:, :half] for x in s]
    si = [x[:, half:] for x in s]
    trow = lambda r: t[r:r + 1, :]
    sup_r = sup_i = None
    for c in range(SLAB_CHUNKS):
        cr, ci = trow(c), trow(SLAB_CHUNKS + c)
        pr = cr * sr[c] - ci * si[c]
        pi = cr * si[c] + ci * sr[c]
        sup_r = pr if sup_r is None else sup_r + pr
        sup_i = pi if sup_i is None else sup_i + pi
    sup_scr[:, :half] = sup_r
    sup_scr[:, half:] = sup_i

    lane = lax.broadcasted_iota(jnp.int32, (bsz, half), 1)
    fwd = lane < SSM_STATE
    sm = _dot(um_ref[...], w)
    xr = jnp.where(fwd, sm[:bsz, :half], 0.0)
    xi = jnp.where(fwd, sm[:bsz, half:], 0.0)
    a_slab_r, a_slab_i = trow(2 * SLAB_CHUNKS), trow(2 * SLAB_CHUNKS + 1)
    for j in range(nslab):
        rf = j * bsz
        rb = (nslab - 1 - j) * bsz
        ent_scr[rf:rf + bsz, 0:SSM_STATE] = xr[:, 0:SSM_STATE]
        ent_scr[rb:rb + bsz, SSM_STATE:half] = xr[:, SSM_STATE:half]
        ent_scr[rf:rf + bsz, half:half + SSM_STATE] = xi[:, 0:SSM_STATE]
        ent_scr[rb:rb + bsz, half + SSM_STATE:STATE_W] = xi[:, SSM_STATE:half]
        s_r = jnp.where(fwd, sup_scr[rf:rf + bsz, :half], sup_scr[rb:rb + bsz, :half])
        s_i = jnp.where(fwd, sup_scr[rf:rf + bsz, half:], sup_scr[rb:rb + bsz, half:])
        xr, xi = _cmul_add(a_slab_r, a_slab_i, xr, xi, s_r, s_i)

    ent = ent_scr[...]
    a_r, a_i = trow(2 * SLAB_CHUNKS + 2), trow(2 * SLAB_CHUNKS + 3)
    xf = [(ent[:, :half], ent[:, half:])]
    for c in range(1, SLAB_CHUNKS):
        xf.append(_cmul_add(a_r, a_i, xf[-1][0], xf[-1][1], sr[c - 1], si[c - 1]))
    xb = [(ent[:, :half], ent[:, half:])]
    for c in range(SLAB_CHUNKS - 2, -1, -1):
        xb.insert(0, _cmul_add(a_r, a_i, xb[0][0], xb[0][1], sr[c + 1], si[c + 1]))
    fwd_rows = lax.broadcasted_iota(jnp.int32, (rows, half), 1) < SSM_STATE
    lag = lag_ref[...]
    m = jnp.concatenate(
        [pltpu.roll(jnp.broadcast_to(lag[i:i + 1, :], (SSM_CHUNK, 2 * CHUNK_W)), 0, 1,
                    stride=SSM_GROUP, stride_axis=0)[:, :CHUNK_W] for i in range(SSM_GROUP)], axis=0).astype(_BF)
    m = _dot(m, q_ref[...]).astype(_BF)
    v = vc_ref[...]
    ys = []
    for c in range(SLAB_CHUNKS):
        xin = jnp.concatenate([jnp.where(fwd_rows, xf[c][0], xb[c][0]),
                               jnp.where(fwd_rows, xf[c][1], xb[c][1])], axis=1).astype(_BF)
        ys.append(_dot(uc[c], m)
                  + lax.dot_general(xin, v, _NT, preferred_element_type=_F32))
    for o in range(SSM_GROUP):
        y_ref[pl.ds(o, rows, stride=SSM_GROUP), :] = jnp.concatenate(
            [y[:, o * SSM_CHUNK:(o + 1) * SSM_CHUNK] for y in ys], axis=1)


def _s5_call(a, um, perm, w_mat, vc_mat, lag, tab, *, nslab, bsz):
    n = nslab * bsz * SSM_GROUP
    rows = nslab * bsz
    g_spec = lambda *shape: pl.BlockSpec((None,) + shape, lambda g: (g,) + (0,) * len(shape))
    perm_spec = pl.BlockSpec((CHUNK_W, CHUNK_W), lambda g: (0, 0), pipeline_mode=pl.Buffered(1))
    return pl.pallas_call(
        functools.partial(_s5_kernel, nslab=nslab, bsz=bsz),
        grid=(SSM_GROUPS,),
        in_specs=[g_spec(n, SLAB_T), g_spec(SUBLANES, CHUNK_W), perm_spec,
                  g_spec(CHUNK_W, STATE_W), g_spec(CHUNK_W, STATE_W), g_spec(SSM_GROUP, 2 * CHUNK_W),
                  g_spec(2 * SUBLANES, 2 * SSM_STATE)],
        out_specs=g_spec(n, SLAB_T),
        out_shape=jax.ShapeDtypeStruct((SSM_GROUPS, n, SLAB_T), _F32),
        scratch_shapes=[pltpu.VMEM((rows, STATE_W), _F32), pltpu.VMEM((rows, STATE_W), _F32)],
        compiler_params=pltpu.CompilerParams(
            dimension_semantics=("parallel",), vmem_limit_bytes=VMEM_LIMIT),
        name="s5",
    )(a, um, perm, w_mat, vc_mat, lag, tab)


def _s5_operators(lam_ref, b_ref, c_ref):
    tc = SSM_CHUNK
    lr, li = lam_ref[0:1, :], lam_ref[1:2, :]

    def power(k):
        mag = jnp.exp(k * lr)
        return mag * jnp.cos(k * li), mag * jnp.sin(k * li)

    s = lax.broadcasted_iota(jnp.int32, (tc, 2 * SSM_STATE), 0)
    fwd = lax.broadcasted_iota(jnp.int32, (tc, 2 * SSM_STATE), 1) < SSM_STATE
    pr, pi = power(jnp.where(fwd, tc - 1 - s, s).astype(_F32))
    gr, gi = power(jnp.where(fwd, s + 1, tc - s).astype(_F32))
    w, vc, vt = [], [], []
    for h in range(SSM_GROUP):
        br, bi = b_ref[0, h:h + 1, :], b_ref[1, h:h + 1, :]
        w.append(jnp.concatenate([br * pr - bi * pi, br * pi + bi * pr], axis=1).astype(_BF))
        cr, ci = c_ref[0, h:h + 1, :], c_ref[1, h:h + 1, :]
        vc.append(jnp.concatenate([cr * gr - ci * gi, -(cr * gi + ci * gr)], axis=1).astype(_BF))
    cr, ci = c_ref[0], c_ref[1]
    for t in range(tc):
        tr, ti = gr[t:t + 1, :], gi[t:t + 1, :]
        vt.append(jnp.concatenate([cr * tr - ci * ti, -(cr * ti + ci * tr)], axis=1))
    r = lax.broadcasted_iota(jnp.int32, (2 * SUBLANES, 2 * SSM_STATE), 0)
    f = lax.broadcasted_iota(jnp.int32, (2 * SUBLANES, 2 * SSM_STATE), 1) < SSM_STATE
    c = jnp.where(r < SLAB_CHUNKS, r, r - SLAB_CHUNKS)
    n = jnp.where(r < 2 * SLAB_CHUNKS, jnp.where(f, SLAB_CHUNKS - 1 - c, c),
                  jnp.where(r < 2 * SLAB_CHUNKS + 2, SLAB_CHUNKS, 1))
    nr, ni = power((n * tc).astype(_F32))
    is_re = (r < SLAB_CHUNKS) | (r == 2 * SLAB_CHUNKS) | (r == 2 * SLAB_CHUNKS + 2)
    tab = jnp.where(r < 2 * SLAB_CHUNKS + 4, jnp.where(is_re, nr, ni), 0.0)
    return jnp.concatenate(w, axis=0), jnp.concatenate(vt, axis=0), jnp.concatenate(vc, axis=0), tab


def _ops_kernel(lam_ref, b_ref, c_ref, bt_ref, c0_ref, d_ref, w_ref, vc_ref, lag_ref, t_ref):
    half = STATE_W // 2
    w, vf, vc, tab = _s5_operators(lam_ref, b_ref, c_ref)
    w_ref[...] = w
    vc_ref[...] = vc
    t_ref[...] = tab
    hi = lax.Precision.HIGHEST
    dirs = [jnp.concatenate([vf[:, d * SSM_STATE:(d + 1) * SSM_STATE],
                             vf[:, half + d * SSM_STATE:half + (d + 1) * SSM_STATE]], axis=1) for d in range(2)]
    nt_hi = lambda a, b: lax.dot_general(a, b, _NT, precision=hi, preferred_element_type=_F32)
    lag_f = nt_hi(bt_ref[0], dirs[0])
    lag_b = nt_hi(bt_ref[1], dirs[1])
    center = (jnp.dot(bt_ref[0], c0_ref[0], precision=hi, preferred_element_type=_F32)
              + jnp.dot(bt_ref[1], c0_ref[1], precision=hi, preferred_element_type=_F32) + d_ref[...])
    lane = lax.broadcasted_iota(jnp.int32, (SSM_GROUP, CHUNK_W), 1)
    lag_ref[...] = jnp.concatenate([center, lag_f[:, :CHUNK_W - SSM_GROUP], jnp.where(lane < SSM_GROUP, 0.0, lag_b)],
                                   axis=1)


def _s5_matrices(a_re, a_im, log_dt, b_re, b_im, c_re, c_im, d_skip):
    lam = lax.complex(jnp.minimum(a_re.astype(_F32), -1e-4), a_im.astype(_F32))
    dt = jnp.exp(log_dt.astype(_F32))[..., None]
    lam_dt = lam * dt
    lam_bar = jnp.exp(lam_dt)
    b_bar = ((lam_bar - 1.0) / lam)[..., None] * lax.complex(b_re.astype(_F32), b_im.astype(_F32))
    c_c = lax.complex(c_re.astype(_F32), c_im.astype(_F32))
    bt = jnp.concatenate([jnp.real(b_bar), jnp.imag(b_bar)], axis=2).transpose(1, 0, 3, 2)
    c0 = jnp.concatenate([jnp.real(c_c), -jnp.imag(c_c)], axis=3).transpose(1, 0, 3, 2)
    dmat = jnp.eye(SSM_GROUP, dtype=_F32)[None] * d_skip.astype(_F32).reshape(SSM_GROUPS, 1, SSM_GROUP)
    lanes = lambda z, perm: z.transpose(perm).reshape(SSM_GROUPS, -1, 2 * SSM_STATE)
    split = lambda z: jnp.stack([jnp.real(z), jnp.imag(z)], axis=1)
    lam_tab = jnp.pad(split(lanes(lam_dt, (1, 0, 2)))[:, :, 0], ((0, 0), (0, SUBLANES - 2), (0, 0)))
    b_tab = split(lanes(b_bar, (1, 3, 0, 2)))
    c_tab = split(lanes(c_c, (1, 2, 0, 3)))
    g_spec = lambda *shape: pl.BlockSpec((None,) + shape, lambda g: (g,) + (0,) * len(shape))
    op_shape = lambda *shape, dtype: jax.ShapeDtypeStruct((SSM_GROUPS,) + shape, dtype)
    return pl.pallas_call(
        _ops_kernel,
        grid=(SSM_GROUPS,),
        in_specs=[g_spec(SUBLANES, 2 * SSM_STATE), g_spec(2, SSM_GROUP, 2 * SSM_STATE),
                  g_spec(2, SSM_GROUP, 2 * SSM_STATE), g_spec(2, SSM_GROUP, 2 * SSM_STATE),
                  g_spec(2, 2 * SSM_STATE, SSM_GROUP), g_spec(SSM_GROUP, SSM_GROUP)],
        out_specs=[g_spec(CHUNK_W, STATE_W), g_spec(CHUNK_W, STATE_W), g_spec(SSM_GROUP, 2 * CHUNK_W),
                   g_spec(2 * SUBLANES, 2 * SSM_STATE)],
        out_shape=[op_shape(CHUNK_W, STATE_W, dtype=_BF), op_shape(CHUNK_W, STATE_W, dtype=_BF),
                   op_shape(SSM_GROUP, 2 * CHUNK_W, dtype=_F32), op_shape(2 * SUBLANES, 2 * SSM_STATE, dtype=_F32)],
        compiler_params=pltpu.CompilerParams(dimension_semantics=("parallel",)),
        name="s5_ops",
    )(lam_tab, b_tab, c_tab, bt, c0, dmat)


def _chunk_permutation():
    r = lax.broadcasted_iota(jnp.int32, (CHUNK_W, CHUNK_W), 0)
    c = lax.broadcasted_iota(jnp.int32, (CHUNK_W, CHUNK_W), 1)
    return (r == (c % SSM_CHUNK) * SSM_GROUP + c // SSM_CHUNK).astype(_BF)


def _post_kernel(x_ref, attn_ref, y_ref, wglu_ref, gmix_ref, wout_ref, gpm_ref, gpre_ref,
                 wup_ref, wdn_ref, gpost_ref, o_ref):
    nslab = y_ref.shape[1]
    gmix = gmix_ref[...]
    per = min(POST_SLABS, nslab)
    blocks = range(nslab // per)
    rows = [pl.ds(r * per * SLAB_T, per * SLAB_T) for r in blocks]
    gy = []
    for r in blocks:
        yt = jnp.concatenate([jnp.concatenate([y_ref[g, r * per + c] for c in range(per)], axis=1)
                              for g in range(SSM_GROUPS)], axis=0)
        y = yt.T
        gy.append((0.5 * y * (1.0 + jnp.tanh(math.sqrt(2.0 / math.pi) * (y + 0.044715 * (y * y * y))))).astype(_BF))
    z = [_dot(gy[r], wglu_ref[...]) for r in blocks]
    mix = []
    for r in blocks:
        ssm = z[r][:, :SSM_WIDTH] * (1.0 / (1.0 + jnp.exp(-z[r][:, SSM_WIDTH:])))
        mix.append(jnp.concatenate([_rms(attn_ref[rows[r], :], gmix[:, :ATTN_WIDTH]),
                                    _rms(ssm, gmix[:, ATTN_WIDTH:])], axis=-1).astype(_BF))
    mixed = [_dot(mix[r], wout_ref[...]) for r in blocks]
    h1 = [x_ref[rows[r], :] + _rms(mixed[r], gpm_ref[...]) for r in blocks]
    hn = [_rms(h1[r], gpre_ref[...]).astype(_BF) for r in blocks]
    acc = [None for _ in blocks]
    for c in range(D_FF // FF_TILE):
        for r in blocks:
            up = jnp.maximum(_dot(hn[r], wup_ref[:, c * FF_TILE:(c + 1) * FF_TILE]), 0.0)
            part = _dot((up * up).astype(_BF), wdn_ref[c * FF_TILE:(c + 1) * FF_TILE, :])
            acc[r] = part if acc[r] is None else acc[r] + part
    for r in blocks:
        o_ref[rows[r], :] = h1[r] + _rms(acc[r], gpost_ref[...])


def _post_call(x, attn, y, wglu, gmix, wout, gpm, gpre, wup, wdn, gpost, *, tile):
    bsz, seq, _ = x.shape
    row_spec = lambda w: pl.BlockSpec((None, tile, w), lambda b, i: (b, i, 0))
    wspec = lambda shape: pl.BlockSpec(shape, lambda b, i: (0, 0), pipeline_mode=pl.Buffered(1))
    y_spec = pl.BlockSpec((SSM_GROUPS, tile // SLAB_T, SSM_GROUP, SLAB_T), lambda b, i: (0, i, b, 0))
    return pl.pallas_call(
        _post_kernel,
        grid=(bsz, seq // tile),
        in_specs=[row_spec(D_MODEL), row_spec(ATTN_WIDTH), y_spec,
                  wspec((SSM_WIDTH, 2 * SSM_WIDTH)), wspec((1, D_MODEL)), wspec((D_MODEL, D_MODEL)),
                  wspec((1, D_MODEL)), wspec((1, D_MODEL)), wspec((D_MODEL, D_FF)),
                  wspec((D_FF, D_MODEL)), wspec((1, D_MODEL))],
        out_specs=row_spec(D_MODEL),
        out_shape=jax.ShapeDtypeStruct((bsz, seq, D_MODEL), _F32),
        compiler_params=pltpu.CompilerParams(
            dimension_semantics=("parallel", "parallel"), vmem_limit_bytes=VMEM_LIMIT),
        name="post",
    )(x, attn, y, wglu, gmix, wout, gpm, gpre, wup, wdn, gpost)


def _rope_tables(pos, tile):
    half = QK_ROPE_DIM // 2
    inv = 1.0 / (ROPE_BASE ** (jnp.arange(0, QK_ROPE_DIM, 2, dtype=_F32) / QK_ROPE_DIM))
    ang = pos.astype(_F32)[:, None, :] * inv[None, :, None]
    bsz, seq = pos.shape
    rope = jnp.stack([jnp.cos(ang), jnp.sin(ang)], axis=1)
    return rope.reshape(bsz, 2, half, seq // tile, tile).transpose(0, 3, 1, 2, 4)


def _prep_weights(w_in, w_uq, w_ukv):
    scale = QK_HEAD_DIM ** -0.5 * math.log2(math.e)
    win = w_in[:, :OFF_KR].astype(_BF)
    wkr_t = w_in[:, OFF_KR:OFF_U].T.astype(_BF)
    wu_t = w_in[:, OFF_U:].T.astype(_BF)
    wq_t = (w_uq * scale).T.astype(_BF)
    wkv3 = w_ukv.reshape(KV_LORA_RANK, N_HEADS, QK_NOPE_DIM + V_HEAD_DIM)
    wk = jnp.concatenate([wkv3[..., :QK_NOPE_DIM],
                          jnp.zeros((KV_LORA_RANK, N_HEADS, HEAD_PAD - QK_NOPE_DIM), _F32)], axis=-1)
    wk = wk.reshape(KV_LORA_RANK, N_HEADS * HEAD_PAD).astype(_BF)
    wv_t = wkv3[..., QK_NOPE_DIM:].reshape(KV_LORA_RANK, N_HEADS * V_HEAD_DIM).T.astype(_BF)
    return win, wkr_t, wu_t, wq_t, wk, wv_t


def kernel(x, positions, meta_tokens, g_pre_mix, w_in, g_q_lat, w_uq, g_kv_lat, w_ukv,
           ssm_A_re, ssm_A_im, ssm_log_dt, ssm_B_re, ssm_B_im, ssm_C_re, ssm_C_im, ssm_D,
           w_glu, g_mix_out, w_out, g_post_mix, g_pre_mlp, w_mlp_up, w_mlp_down, g_post_mlp):
    bsz, seq, _ = x.shape
    assert seq % ROW_TILE == 0 and ROW_TILE % SLAB_T == 0 and bsz <= SUBLANES
    assert seq % PROJ_TILE == 0 and PROJ_TILE % SLAB_T == 0
    assert N_META <= SSM_CHUNK
    row = lambda g: g.reshape(1, -1).astype(_F32)

    win, wkr_t, wu_t, wq_t, wk, wv_t = _prep_weights(w_in[0], w_uq[0], w_ukv[0])
    weights = (row(g_pre_mix[0]), win, wkr_t, wu_t, row(g_q_lat[0]), wq_t, row(g_kv_lat[0]), wk, wv_t)
    rope = _rope_tables(positions.astype(jnp.int32) + N_META, PROJ_TILE)
    qt, k, vt, u, q_norm, k_sq = _proj_call(x, rope, *weights, tile=PROJ_TILE, meta=False)
    meta_x = jnp.pad(meta_tokens.astype(x.dtype), ((0, LANES - N_META), (0, 0)))[None]
    rope_m = _rope_tables(jnp.arange(LANES, dtype=jnp.int32)[None], LANES)
    k_m, vt_m, u_m = _proj_call(meta_x, rope_m, *weights, tile=LANES, meta=True)
    km, vtm, u_m = k_m[0], vt_m[0, :, 0], u_m[:, :N_META]
    k_m_sq = jnp.max(jnp.sum(jnp.square(k_m[0].astype(_F32)), axis=-1), axis=-1)
    k_max = jnp.sqrt(jnp.maximum(jnp.max(k_sq, axis=(2, 3, 4)), k_m_sq[None]))
    mrow = q_norm * (k_max * BOUND_SLACK)[:, :, None, None, None]
    attn = lax.cond(jnp.max(mrow) <= BOUND_LIMIT,
                    lambda: _attn_call(qt, mrow, k, vt, km, vtm, bounded=True),
                    lambda: _attn_call(qt, mrow, k, vt, km, vtm, bounded=False))

    um = u_m[0].astype(_BF).reshape(N_META, SSM_GROUPS, SSM_GROUP).transpose(1, 2, 0)
    um = jnp.pad(um, ((0, 0), (0, 0), (SSM_CHUNK - N_META, 0))).reshape(SSM_GROUPS, 1, CHUNK_W)
    um = jnp.broadcast_to(um, (SSM_GROUPS, SUBLANES, CHUNK_W))
    ops = _s5_matrices(ssm_A_re[0], ssm_A_im[0], ssm_log_dt[0], ssm_B_re[0],
                       ssm_B_im[0], ssm_C_re[0], ssm_C_im[0], ssm_D[0])
    nslab = seq // SLAB_T
    yg = _s5_call(u.reshape(SSM_GROUPS, nslab * bsz * SSM_GROUP, SLAB_T), um, _chunk_permutation(),
                  *ops, nslab=nslab, bsz=bsz)
    y = yg.reshape(SSM_GROUPS, nslab, bsz * SSM_GROUP, SLAB_T)

    return _post_call(x, attn, y, w_glu[0].astype(_BF), row(g_mix_out[0]), w_out[0].astype(_BF),
                      row(g_post_mix[0]), row(g_pre_mlp[0]), w_mlp_up[0].astype(_BF),
                      w_mlp_down[0].astype(_BF), row(g_post_mlp[0]), tile=ROW_TILE)
```

```python
import functools
import math

import jax
import jax.numpy as jnp
from jax import lax
from jax.experimental import pallas as pl
from jax.experimental.pallas import tpu as pltpu

D_MODEL = 1024
N_META = 16
ATTN_WIDTH = 512
SSM_WIDTH = 512
N_HEADS = 8
V_HEAD_DIM = 64
QK_NOPE_DIM = 64
QK_ROPE_DIM = 32
QK_HEAD_DIM = QK_NOPE_DIM + QK_ROPE_DIM
Q_LORA_RANK = 384
KV_LORA_RANK = 256
ROPE_BASE = 10000.0
SSM_GROUP = 16
SSM_GROUPS = 32
SSM_STATE = 64
D_FF = 4 * D_MODEL
EPS = 1e-6
OFF_KR = Q_LORA_RANK + KV_LORA_RANK
OFF_U = OFF_KR + QK_ROPE_DIM

LANES = 128
SUBLANES = 8
HEAD_PAD = LANES
ONES_LANE = V_HEAD_DIM
ATTN_TQ = 1024
ATTN_TK = 256
V_ROWS = 112
PW_Q = 0
PW_KV = PW_Q + Q_LORA_RANK
PW_END = PW_KV + KV_LORA_RANK

SSM_CHUNK = 32
CHUNK_W = SSM_CHUNK * SSM_GROUP
STATE_W = 4 * SSM_STATE
SLAB_T = LANES
SLAB_CHUNKS = SLAB_T // SSM_CHUNK
OPS_GROUPS = 4

ROW_TILE = 512
PROJ_TILE = 1024
FF_TILE = 1024
POST_SLABS = 2
BOUND_SLACK = 1.0 + 2.0 ** -6
BOUND_LIMIT = 40.0
V7X_VMEM_BYTES = 64 * 1024 * 1024
VMEM_LIMIT = V7X_VMEM_BYTES - 8 * 1024 * 1024

_BF = jnp.bfloat16
_F32 = jnp.float32


def _dot(a, b):
    return jnp.dot(a, b, preferred_element_type=_F32)


def _rms(x, g):
    return x * lax.rsqrt(jnp.mean(x * x, axis=-1, keepdims=True) + EPS) * g


_NT = (((1,), (1,)), ((), ()))


def _rotate(x1, x2, cos_t, sin_t):
    return x1 * cos_t - x2 * sin_t, x1 * sin_t + x2 * cos_t


def _proj_kernel(x_ref, rope_ref, *refs, meta):
    if meta:
        gpre_ref, win_ref, wkr_ref, wu_ref, gkv_ref, wk_ref, wv_ref, k_ref, v_ref, u_ref = refs
    else:
        (gpre_ref, win_ref, wkr_ref, wu_ref, gq_ref, wq_ref, gkv_ref, wk_ref, wv_ref,
         q_ref, k_ref, v_ref, u_ref, qn_ref, kmx_ref) = refs
    tile = x_ref.shape[0]
    half = QK_ROPE_DIM // 2
    cos_t, sin_t = rope_ref[0], rope_ref[1]
    xn = _rms(x_ref[...], gpre_ref[...]).astype(_BF)
    proj = _dot(xn, win_ref[...])
    kvn = _rms(proj[:, PW_KV:PW_END], gkv_ref[...]).astype(_BF)
    krt = lax.dot_general(wkr_ref[...], xn, _NT, preferred_element_type=_F32)
    r1, r2 = _rotate(krt[:half], krt[half:], cos_t, sin_t)
    kr = jnp.concatenate([jnp.zeros((QK_NOPE_DIM, tile), _F32), r1, r2,
                          jnp.zeros((HEAD_PAD - QK_HEAD_DIM, tile), _F32)], axis=0).T
    kk = _dot(kvn, wk_ref[...])
    vt = lax.dot_general(wv_ref[...], kvn, _NT, preferred_element_type=_F32)
    ones_tail = (lax.broadcasted_iota(jnp.int32, (HEAD_PAD - V_HEAD_DIM, tile), 0) == 0).astype(_F32)
    if meta:
        u_ref[...] = lax.dot_general(xn, wu_ref[...], _NT, preferred_element_type=_F32)
    else:
        ut = lax.dot_general(wu_ref[...], xn, _NT, preferred_element_type=_F32)
        for g in range(SSM_GROUPS):
            for c in range(tile // SLAB_T):
                u_ref[g, c] = ut[g * SSM_GROUP:(g + 1) * SSM_GROUP, c * SLAB_T:(c + 1) * SLAB_T]
        qn = _rms(proj[:, PW_Q:PW_KV], gq_ref[...]).astype(_BF)
        qt = lax.dot_general(wq_ref[...], qn, _NT, preferred_element_type=_F32)
        zero_rows = jnp.zeros((HEAD_PAD - QK_HEAD_DIM, tile), _F32)
    for h in range(N_HEADS):
        k_h = (kk[:, h * HEAD_PAD:(h + 1) * HEAD_PAD] + kr).astype(_BF)
        k_ref[h] = k_h
        v_ref[h] = jnp.concatenate([vt[h * V_HEAD_DIM:(h + 1) * V_HEAD_DIM], ones_tail], axis=0).astype(_BF)
        if not meta:
            blk = qt[h * QK_HEAD_DIM:(h + 1) * QK_HEAD_DIM]
            r1, r2 = _rotate(blk[QK_NOPE_DIM:QK_NOPE_DIM + half], blk[QK_NOPE_DIM + half:], cos_t, sin_t)
            qt_h = jnp.concatenate([blk[:QK_NOPE_DIM], r1, r2, zero_rows], axis=0).astype(_BF)
            q_ref[h] = qt_h
            qt_f = qt_h.astype(_F32)
            qn_ref[h] = jnp.sqrt(jnp.sum(qt_f * qt_f, axis=0, keepdims=True))
            k_f = k_h.astype(_F32)
            kmx_ref[h] = jnp.broadcast_to(jnp.max(jnp.sum(k_f * k_f, axis=1, keepdims=True), axis=0, keepdims=True),
                                          (1, LANES))


def _const_spec(shape):
    nd = len(shape)
    return pl.BlockSpec(shape, lambda *_: (0,) * nd)


def _proj_call(x, rope, gpre, win, wkr_t, wu_t, gq, wq_t, gkv, wk, wv_t, *, tile, meta):
    bsz, seq, _ = x.shape
    nt = seq // tile
    row_spec = lambda w: pl.BlockSpec((None, tile, w), lambda b, i: (b, i, 0))
    rope_spec = pl.BlockSpec((None, None, 2, QK_ROPE_DIM // 2, tile), lambda b, i: (b, i, 0, 0, 0))
    k_spec = pl.BlockSpec((None, N_HEADS, tile, HEAD_PAD), lambda b, i: (b, 0, i, 0))
    t_spec = pl.BlockSpec((None, N_HEADS, None, HEAD_PAD, tile), lambda b, i: (b, 0, i, 0, 0))
    k_shape = jax.ShapeDtypeStruct((bsz, N_HEADS, seq, HEAD_PAD), _BF)
    t_shape = jax.ShapeDtypeStruct((bsz, N_HEADS, nt, HEAD_PAD, tile), _BF)
    w_specs = lambda *ws: [_const_spec(w.shape) for w in ws]
    if meta:
        args = (x, rope, gpre, win, wkr_t, wu_t, gkv, wk, wv_t)
        in_specs = [row_spec(D_MODEL), rope_spec] + w_specs(*args[2:])
        out_specs = [k_spec, t_spec, row_spec(SSM_WIDTH)]
        out_shape = [k_shape, t_shape, jax.ShapeDtypeStruct((bsz, seq, SSM_WIDTH), _F32)]
    else:
        args = (x, rope, gpre, win, wkr_t, wu_t, gq, wq_t, gkv, wk, wv_t)
        in_specs = [row_spec(D_MODEL), rope_spec] + w_specs(*args[2:])
        norm_spec = lambda w: pl.BlockSpec((None, N_HEADS, None, 1, w), lambda b, i: (b, 0, i, 0, 0))
        u_spec = pl.BlockSpec((SSM_GROUPS, tile // SLAB_T, SSM_GROUP, SLAB_T), lambda b, i: (0, i, b, 0))
        out_specs = [t_spec, k_spec, t_spec, u_spec, norm_spec(tile), norm_spec(LANES)]
        out_shape = [t_shape, k_shape, t_shape,
                     jax.ShapeDtypeStruct((SSM_GROUPS, seq // SLAB_T, bsz * SSM_GROUP, SLAB_T), _F32),
                     jax.ShapeDtypeStruct((bsz, N_HEADS, nt, 1, tile), _F32),
                     jax.ShapeDtypeStruct((bsz, N_HEADS, nt, 1, LANES), _F32)]
    return pl.pallas_call(
        functools.partial(_proj_kernel, meta=meta),
        grid=(bsz, nt),
        in_specs=in_specs,
        out_specs=out_specs,
        out_shape=out_shape,
        compiler_params=pltpu.CompilerParams(
            dimension_semantics=("parallel", "parallel"), vmem_limit_bytes=VMEM_LIMIT),
        name="proj_meta" if meta else "proj",
    )(*args)


def _attn_finish(accs, o_ref):
    halves = [(acc * (1.0 / acc[ONES_LANE:ONES_LANE + 1, :]))[:V_HEAD_DIM] for acc in accs]
    o_ref[...] = jnp.concatenate(halves, axis=0).T


def _lane_concat(ref, hh):
    return jnp.concatenate([ref[hh, j] for j in range(ref.shape[1])], axis=1)


def _attn_bounded_kernel(qt_ref, mrow_ref, k_ref, vt_ref, km_ref, vtm_ref, o_ref, *, nk, tk):
    tq = o_ref.shape[0]
    per_slab = vt_ref.shape[3] // tk
    key_row = lax.broadcasted_iota(jnp.int32, (LANES, tq), 0)
    accs = []
    for hh in range(2):
        qt = _lane_concat(qt_ref, hh)
        mrow = _lane_concat(mrow_ref, hh)
        s0 = jnp.where(key_row < N_META, _dot(km_ref[hh], qt), -jnp.inf)
        acc = _dot(vtm_ref[hh, :V_ROWS, :], jnp.exp2(s0 - mrow).astype(_BF))
        scores = lambda c: _dot(k_ref[hh, c * tk:(c + 1) * tk, :], qt)
        s_next = scores(0)
        for c in range(nk):
            s = s_next
            if c + 1 < nk:
                s_next = scores(c + 1)
            vt_c = vt_ref[hh, c // per_slab, :V_ROWS, (c % per_slab) * tk:(c % per_slab + 1) * tk]
            acc = acc + _dot(vt_c, jnp.exp2(s - mrow).astype(_BF))
        accs.append(acc)
    _attn_finish(accs, o_ref)


def _attn_online_kernel(qt_ref, k_ref, vt_ref, km_ref, vtm_ref, o_ref, s0_scr, s1_scr, m_scr, acc_scr, *, nk, tk):
    tq = o_ref.shape[0]
    key_row = lax.broadcasted_iota(jnp.int32, (LANES, tq), 0)
    for hh in range(2):
        s0 = jnp.where(key_row < N_META, _dot(km_ref[hh], _lane_concat(qt_ref, hh)), -jnp.inf)
        m0 = jnp.max(s0, axis=0, keepdims=True)
        m_scr[hh] = m0
        acc_scr[hh] = _dot(vtm_ref[hh, :V_ROWS, :], jnp.exp2(s0 - m0).astype(_BF))

    def scores(buf, c):
        off = pl.multiple_of(c * tk, tk)
        for hh in range(2):
            buf[hh] = _dot(k_ref[hh, pl.ds(off, tk), :], _lane_concat(qt_ref, hh))

    def accumulate(buf, c):
        for hh in range(2):
            s = buf[hh]
            m = m_scr[hh]
            m_new = jnp.maximum(m, jnp.max(s, axis=0, keepdims=True))
            m_scr[hh] = m_new
            p = jnp.exp2(s - m_new).astype(_BF)
            acc_scr[hh] = jnp.exp2(m - m_new) * acc_scr[hh] + _dot(vt_ref[hh, c, :V_ROWS, :], p)

    scores(s0_scr, 0)

    def body(t, _):
        scores(s1_scr, 2 * t + 1)
        accumulate(s0_scr, 2 * t)
        scores(s0_scr, 2 * t + 2)
        accumulate(s1_scr, 2 * t + 1)
        return 0

    lax.fori_loop(0, nk // 2 - 1, body, 0)
    scores(s1_scr, nk - 1)
    accumulate(s0_scr, nk - 2)
    accumulate(s1_scr, nk - 1)
    _attn_finish([acc_scr[0], acc_scr[1]], o_ref)


def _attn_call(qt, mrow, k, vt, km, vtm, *, bounded):
    bsz, _, nslab, _, slab = qt.shape
    seq = k.shape[2]
    tq = min(ATTN_TQ, seq)
    tk = min(ATTN_TK, slab)
    qs = tq // slab
    assert tq % slab == 0 and seq % tq == 0 and slab % tk == 0
    q_spec = lambda rows: pl.BlockSpec((None, 2, qs, rows, slab), lambda b, hp, i: (b, hp, i, 0, 0))
    kv_specs = [pl.BlockSpec((None, 2, seq, HEAD_PAD), lambda b, hp, i: (b, hp, 0, 0)),
                pl.BlockSpec((None, 2, nslab, HEAD_PAD, slab), lambda b, hp, i: (b, hp, 0, 0, 0)),
                pl.BlockSpec((2, LANES, HEAD_PAD), lambda b, hp, i: (hp, 0, 0)),
                pl.BlockSpec((2, HEAD_PAD, LANES), lambda b, hp, i: (hp, 0, 0))]
    if bounded:
        body = functools.partial(_attn_bounded_kernel, nk=seq // tk, tk=tk)
        in_specs = [q_spec(HEAD_PAD), q_spec(1)] + kv_specs
        args, scratch = (qt, mrow, k, vt, km, vtm), []
    else:
        assert nslab % 2 == 0 and nslab >= 4
        body = functools.partial(_attn_online_kernel, nk=nslab, tk=slab)
        in_specs = [q_spec(HEAD_PAD)] + kv_specs
        args = (qt, k, vt, km, vtm)
        scratch = [pltpu.VMEM((2, slab, tq), _F32), pltpu.VMEM((2, slab, tq), _F32),
                   pltpu.VMEM((2, 1, tq), _F32), pltpu.VMEM((2, V_ROWS, tq), _F32)]
    return pl.pallas_call(
        body,
        grid=(bsz, N_HEADS // 2, seq // tq),
        in_specs=in_specs,
        out_specs=pl.BlockSpec((None, tq, LANES), lambda b, hp, i: (b, i, hp)),
        out_shape=jax.ShapeDtypeStruct((bsz, seq, ATTN_WIDTH), _F32),
        scratch_shapes=scratch,
        compiler_params=pltpu.CompilerParams(
            dimension_semantics=("parallel", "parallel", "arbitrary"), vmem_limit_bytes=VMEM_LIMIT),
        name="attn_bounded" if bounded else "attn_online",
    )(*args)


def _cmul_add(ar, ai, xr, xi, sr, si):
    return ar * xr - ai * xi + sr, ar * xi + ai * xr + si


def _s5_kernel(a_ref, um_ref, q_ref, bt_ref, c0_ref, d_ref, w_ref, v_ref, vc_ref, t_ref, y_ref, sup_scr, ent_scr,
               *, nslab, bsz):
    rows = nslab * bsz
    half = STATE_W // 2
    a = [a_ref[pl.ds(i, rows, stride=SSM_GROUP), :] for i in range(SSM_GROUP)]
    uc = [jnp.concatenate([x[:, c * SSM_CHUNK:(c + 1) * SSM_CHUNK] for x in a], axis=1).astype(_BF)
          for c in range(SLAB_CHUNKS)]
    w = w_ref[...]
    s = [_dot(u, w) for u in uc]
    sr = [x[:, :half] for x in s]
    si = [x[:, half:] for x in s]
    t = t_ref[...]
    trow = lambda r: t[r:r + 1, :]
    sup_r = sup_i = None
    for c in range(SLAB_CHUNKS):
        cr, ci = trow(c), trow(SLAB_CHUNKS + c)
        pr = cr * sr[c] - ci * si[c]
        pi = cr * si[c] + ci * sr[c]
        sup_r = pr if sup_r is None else sup_r + pr
        sup_i = pi if sup_i is None else sup_i + pi
    sup_scr[:, :half] = sup_r
    sup_scr[:, half:] = sup_i

    lane = lax.broadcasted_iota(jnp.int32, (bsz, half), 1)
    fwd = lane < SSM_STATE
    sm = _dot(um_ref[...], w)
    xr = jnp.where(fwd, sm[:bsz, :half], 0.0)
    xi = jnp.where(fwd, sm[:bsz, half:], 0.0)
    a_slab_r, a_slab_i = trow(2 * SLAB_CHUNKS), trow(2 * SLAB_CHUNKS + 1)
    for j in range(nslab):
        rf = j * bsz
        rb = (nslab - 1 - j) * bsz
        ent_scr[rf:rf + bsz, 0:SSM_STATE] = xr[:, 0:SSM_STATE]
        ent_scr[rb:rb + bsz, SSM_STATE:half] = xr[:, SSM_STATE:half]
        ent_scr[rf:rf + bsz, half:half + SSM_STATE] = xi[:, 0:SSM_STATE]
        ent_scr[rb:rb + bsz, half + SSM_STATE:STATE_W] = xi[:, SSM_STATE:half]
        s_r = jnp.where(fwd, sup_scr[rf:rf + bsz, :half], sup_scr[rb:rb + bsz, :half])
        s_i = jnp.where(fwd, sup_scr[rf:rf + bsz, half:], sup_scr[rb:rb + bsz, half:])
        xr, xi = _cmul_add(a_slab_r, a_slab_i, xr, xi, s_r, s_i)

    ent = ent_scr[...]
    a_r, a_i = trow(2 * SLAB_CHUNKS + 2), trow(2 * SLAB_CHUNKS + 3)
    xf = [(ent[:, :half], ent[:, half:])]
    for c in range(1, SLAB_CHUNKS):
        xf.append(_cmul_add(a_r, a_i, xf[-1][0], xf[-1][1], sr[c - 1], si[c - 1]))
    xb = [(ent[:, :half], ent[:, half:])]
    for c in range(SLAB_CHUNKS - 2, -1, -1):
        xb.insert(0, _cmul_add(a_r, a_i, xb[0][0], xb[0][1], sr[c + 1], si[c + 1]))
    fwd_rows = lax.broadcasted_iota(jnp.int32, (rows, half), 1) < SSM_STATE
    hi = lax.Precision.HIGHEST
    vf = v_ref[...]
    dirs = [jnp.concatenate([vf[:, d * SSM_STATE:(d + 1) * SSM_STATE],
                             vf[:, half + d * SSM_STATE:half + (d + 1) * SSM_STATE]], axis=1) for d in range(2)]
    nt_hi = lambda a, b: lax.dot_general(a, b, _NT, precision=hi, preferred_element_type=_F32)
    lag_f = nt_hi(bt_ref[0], dirs[0])
    lag_b = nt_hi(bt_ref[1], dirs[1])
    center = (jnp.dot(bt_ref[0], c0_ref[0], precision=hi, preferred_element_type=_F32)
              + jnp.dot(bt_ref[1], c0_ref[1], precision=hi, preferred_element_type=_F32) + d_ref[...])
    lane = lax.broadcasted_iota(jnp.int32, (SSM_GROUP, CHUNK_W), 1)
    lag = jnp.concatenate([center, lag_f[:, :CHUNK_W - SSM_GROUP], jnp.where(lane < SSM_GROUP, 0.0, lag_b)], axis=1)
    m = jnp.concatenate(
        [pltpu.roll(jnp.broadcast_to(lag[i:i + 1, :], (SSM_CHUNK, 2 * CHUNK_W)), 0, 1,
                    stride=SSM_GROUP, stride_axis=0)[:, :CHUNK_W] for i in range(SSM_GROUP)], axis=0).astype(_BF)
    m = _dot(m, q_ref[...]).astype(_BF)
    v = vc_ref[...]
    ys = []
    for c in range(SLAB_CHUNKS):
        xin = jnp.concatenate([jnp.where(fwd_rows, xf[c][0], xb[c][0]),
                               jnp.where(fwd_rows, xf[c][1], xb[c][1])], axis=1).astype(_BF)
        ys.append(_dot(uc[c], m)
                  + lax.dot_general(xin, v, _NT, preferred_element_type=_F32))
    for o in range(SSM_GROUP):
        y_ref[pl.ds(o, rows, stride=SSM_GROUP), :] = jnp.concatenate(
            [y[:, o * SSM_CHUNK:(o + 1) * SSM_CHUNK] for y in ys], axis=1)


def _s5_call(a, um, perm, bt, c0, dmat, w_mat, vt_mat, vc_mat, tab, *, nslab, bsz):
    n = nslab * bsz * SSM_GROUP
    rows = nslab * bsz
    g_spec = lambda *shape: pl.BlockSpec((None,) + shape, lambda g: (g,) + (0,) * len(shape))
    perm_spec = pl.BlockSpec((CHUNK_W, CHUNK_W), lambda g: (0, 0), pipeline_mode=pl.Buffered(1))
    return pl.pallas_call(
        functools.partial(_s5_kernel, nslab=nslab, bsz=bsz),
        grid=(SSM_GROUPS,),
        in_specs=[g_spec(n, SLAB_T), g_spec(SUBLANES, CHUNK_W), perm_spec,
                  g_spec(2, SSM_GROUP, 2 * SSM_STATE), g_spec(2, 2 * SSM_STATE, SSM_GROUP), g_spec(SSM_GROUP, SSM_GROUP),
                  g_spec(CHUNK_W, STATE_W), g_spec(CHUNK_W, STATE_W), g_spec(CHUNK_W, STATE_W),
                  g_spec(2 * SUBLANES, STATE_W // 2)],
        out_specs=g_spec(n, SLAB_T),
        out_shape=jax.ShapeDtypeStruct((SSM_GROUPS, n, SLAB_T), _F32),
        scratch_shapes=[pltpu.VMEM((rows, STATE_W), _F32), pltpu.VMEM((rows, STATE_W), _F32)],
        compiler_params=pltpu.CompilerParams(
            dimension_semantics=("parallel",), vmem_limit_bytes=VMEM_LIMIT),
        name="s5",
    )(a, um, perm, bt, c0, dmat, w_mat, vt_mat, vc_mat, tab)


def _ops_kernel(*refs):
    for j in range(OPS_GROUPS):
        _ops_group(*[ref.at[j] for ref in refs])


def _ops_group(lam_ref, b_ref, c_ref, w_ref, vt_ref, vc_ref, t_ref):
    tc = SSM_CHUNK
    lr, li = lam_ref[0:1, :], lam_ref[1:2, :]

    def power(k):
        mag = jnp.exp(k * lr)
        return mag * jnp.cos(k * li), mag * jnp.sin(k * li)

    s = lax.broadcasted_iota(jnp.int32, (tc, 2 * SSM_STATE), 0)
    fwd = lax.broadcasted_iota(jnp.int32, (tc, 2 * SSM_STATE), 1) < SSM_STATE
    pr, pi = power(jnp.where(fwd, tc - 1 - s, s).astype(_F32))
    gr, gi = power(jnp.where(fwd, s + 1, tc - s).astype(_F32))
    for h in range(SSM_GROUP):
        br, bi = b_ref[0, h:h + 1, :], b_ref[1, h:h + 1, :]
        w_ref[h * tc:(h + 1) * tc, :] = jnp.concatenate([br * pr - bi * pi, br * pi + bi * pr], axis=1).astype(_BF)
        cr, ci = c_ref[0, h:h + 1, :], c_ref[1, h:h + 1, :]
        vc_ref[h * tc:(h + 1) * tc, :] = jnp.concatenate([cr * gr - ci * gi, -(cr * gi + ci * gr)],
                                                         axis=1).astype(_BF)
    cr, ci = c_ref[0], c_ref[1]
    for t in range(tc):
        tr, ti = gr[t:t + 1, :], gi[t:t + 1, :]
        vt_ref[t * SSM_GROUP:(t + 1) * SSM_GROUP, :] = jnp.concatenate([cr * tr - ci * ti, -(cr * ti + ci * tr)],
                                                                       axis=1)
    r = lax.broadcasted_iota(jnp.int32, (2 * SUBLANES, 2 * SSM_STATE), 0)
    f = lax.broadcasted_iota(jnp.int32, (2 * SUBLANES, 2 * SSM_STATE), 1) < SSM_STATE
    c = jnp.where(r < SLAB_CHUNKS, r, r - SLAB_CHUNKS)
    n = jnp.where(r < 2 * SLAB_CHUNKS, jnp.where(f, SLAB_CHUNKS - 1 - c, c),
                  jnp.where(r < 2 * SLAB_CHUNKS + 2, SLAB_CHUNKS, 1))
    nr, ni = power((n * tc).astype(_F32))
    is_re = (r < SLAB_CHUNKS) | (r == 2 * SLAB_CHUNKS) | (r == 2 * SLAB_CHUNKS + 2)
    t_ref[...] = jnp.where(r < 2 * SLAB_CHUNKS + 4, jnp.where(is_re, nr, ni), 0.0)


def _s5_matrices(a_re, a_im, log_dt, b_re, b_im, c_re, c_im, d_skip):
    lam = lax.complex(jnp.minimum(a_re.astype(_F32), -1e-4), a_im.astype(_F32))
    dt = jnp.exp(log_dt.astype(_F32))[..., None]
    lam_dt = lam * dt
    lam_bar = jnp.exp(lam_dt)
    b_bar = ((lam_bar - 1.0) / lam)[..., None] * lax.complex(b_re.astype(_F32), b_im.astype(_F32))
    c_c = lax.complex(c_re.astype(_F32), c_im.astype(_F32))
    bt = jnp.concatenate([jnp.real(b_bar), jnp.imag(b_bar)], axis=2).transpose(1, 0, 3, 2)
    c0 = jnp.concatenate([jnp.real(c_c), -jnp.imag(c_c)], axis=3).transpose(1, 0, 3, 2)
    dmat = jnp.eye(SSM_GROUP, dtype=_F32)[None] * d_skip.astype(_F32).reshape(SSM_GROUPS, 1, SSM_GROUP)
    lanes = lambda z, perm: z.transpose(perm).reshape(SSM_GROUPS, -1, 2 * SSM_STATE)
    split = lambda z: jnp.stack([jnp.real(z), jnp.imag(z)], axis=1)
    lam_tab = jnp.pad(split(lanes(lam_dt, (1, 0, 2)))[:, :, 0], ((0, 0), (0, SUBLANES - 2), (0, 0)))
    b_tab = split(lanes(b_bar, (1, 3, 0, 2)))
    c_tab = split(lanes(c_c, (1, 2, 0, 3)))
    g_spec = lambda *shape: pl.BlockSpec((OPS_GROUPS,) + shape, lambda g: (g,) + (0,) * len(shape))
    op_shape = lambda dtype: jax.ShapeDtypeStruct((SSM_GROUPS, CHUNK_W, STATE_W), dtype)
    w_mat, vt_mat, vc_mat, tab = pl.pallas_call(
        _ops_kernel,
        grid=(SSM_GROUPS // OPS_GROUPS,),
        in_specs=[g_spec(SUBLANES, 2 * SSM_STATE), g_spec(2, SSM_GROUP, 2 * SSM_STATE),
                  g_spec(2, SSM_GROUP, 2 * SSM_STATE)],
        out_specs=[g_spec(CHUNK_W, STATE_W)] * 3 + [g_spec(2 * SUBLANES, 2 * SSM_STATE)],
        out_shape=[op_shape(_BF), op_shape(_F32), op_shape(_BF),
                   jax.ShapeDtypeStruct((SSM_GROUPS, 2 * SUBLANES, 2 * SSM_STATE), _F32)],
        compiler_params=pltpu.CompilerParams(dimension_semantics=("parallel",)),
        name="s5_ops",
    )(lam_tab, b_tab, c_tab)
    return bt, c0, dmat, w_mat, vt_mat, vc_mat, tab


def _chunk_permutation():
    r = lax.broadcasted_iota(jnp.int32, (CHUNK_W, CHUNK_W), 0)
    c = lax.broadcasted_iota(jnp.int32, (CHUNK_W, CHUNK_W), 1)
    return (r == (c % SSM_CHUNK) * SSM_GROUP + c // SSM_CHUNK).astype(_BF)


def _post_kernel(x_ref, attn_ref, y_ref, wglu_ref, gmix_ref, wout_ref, gpm_ref, gpre_ref,
                 wup_ref, wdn_ref, gpost_ref, o_ref):
    nslab = y_ref.shape[1]
    gmix = gmix_ref[...]
    per = min(POST_SLABS, nslab)
    blocks = range(nslab // per)
    rows = [pl.ds(r * per * SLAB_T, per * SLAB_T) for r in blocks]
    gy = []
    for r in blocks:
        yt = jnp.concatenate([jnp.concatenate([y_ref[g, r * per + c] for c in range(per)], axis=1)
                              for g in range(SSM_GROUPS)], axis=0)
        y = yt.T
        gy.append((0.5 * y * (1.0 + jnp.tanh(math.sqrt(2.0 / math.pi) * (y + 0.044715 * (y * y * y))))).astype(_BF))
    z = [_dot(gy[r], wglu_ref[...]) for r in blocks]
    mix = []
    for r in blocks:
        ssm = z[r][:, :SSM_WIDTH] * (1.0 / (1.0 + jnp.exp(-z[r][:, SSM_WIDTH:])))
        mix.append(jnp.concatenate([_rms(attn_ref[rows[r], :], gmix[:, :ATTN_WIDTH]),
                                    _rms(ssm, gmix[:, ATTN_WIDTH:])], axis=-1).astype(_BF))
    mixed = [_dot(mix[r], wout_ref[...]) for r in blocks]
    h1 = [x_ref[rows[r], :] + _rms(mixed[r], gpm_ref[...]) for r in blocks]
    hn = [_rms(h1[r], gpre_ref[...]).astype(_BF) for r in blocks]
    acc = [None for _ in blocks]
    for c in range(D_FF // FF_TILE):
        for r in blocks:
            up = jnp.maximum(_dot(hn[r], wup_ref[:, c * FF_TILE:(c + 1) * FF_TILE]), 0.0)
            part = _dot((up * up).astype(_BF), wdn_ref[c * FF_TILE:(c + 1) * FF_TILE, :])
            acc[r] = part if acc[r] is None else acc[r] + part
    for r in blocks:
        o_ref[rows[r], :] = h1[r] + _rms(acc[r], gpost_ref[...])


def _post_call(x, attn, y, wglu, gmix, wout, gpm, gpre, wup, wdn, gpost, *, tile):
    bsz, seq, _ = x.shape
    row_spec = lambda w: pl.BlockSpec((None, tile, w), lambda b, i: (b, i, 0))
    wspec = lambda shape: pl.BlockSpec(shape, lambda b, i: (0, 0), pipeline_mode=pl.Buffered(1))
    y_spec = pl.BlockSpec((SSM_GROUPS, tile // SLAB_T, SSM_GROUP, SLAB_T), lambda b, i: (0, i, b, 0))
    return pl.pallas_call(
        _post_kernel,
        grid=(bsz, seq // tile),
        in_specs=[row_spec(D_MODEL), row_spec(ATTN_WIDTH), y_spec,
                  wspec((SSM_WIDTH, 2 * SSM_WIDTH)), wspec((1, D_MODEL)), wspec((D_MODEL, D_MODEL)),
                  wspec((1, D_MODEL)), wspec((1, D_MODEL)), wspec((D_MODEL, D_FF)),
                  wspec((D_FF, D_MODEL)), wspec((1, D_MODEL))],
        out_specs=row_spec(D_MODEL),
        out_shape=jax.ShapeDtypeStruct((bsz, seq, D_MODEL), _F32),
        compiler_params=pltpu.CompilerParams(
            dimension_semantics=("parallel", "parallel"), vmem_limit_bytes=VMEM_LIMIT),
        name="post",
    )(x, attn, y, wglu, gmix, wout, gpm, gpre, wup, wdn, gpost)


def _rope_tables(pos, tile):
    half = QK_ROPE_DIM // 2
    inv = 1.0 / (ROPE_BASE ** (jnp.arange(0, QK_ROPE_DIM, 2, dtype=_F32) / QK_ROPE_DIM))
    ang = pos.astype(_F32)[:, None, :] * inv[None, :, None]
    bsz, seq = pos.shape
    rope = jnp.stack([jnp.cos(ang), jnp.sin(ang)], axis=1)
    return rope.reshape(bsz, 2, half, seq // tile, tile).transpose(0, 3, 1, 2, 4)


def _prep_weights(w_in, w_uq, w_ukv):
    scale = QK_HEAD_DIM ** -0.5 * math.log2(math.e)
    win = w_in[:, :OFF_KR].astype(_BF)
    wkr_t = w_in[:, OFF_KR:OFF_U].T.astype(_BF)
    wu_t = w_in[:, OFF_U:].T.astype(_BF)
    wq_t = (w_uq * scale).T.astype(_BF)
    wkv3 = w_ukv.reshape(KV_LORA_RANK, N_HEADS, QK_NOPE_DIM + V_HEAD_DIM)
    wk = jnp.concatenate([wkv3[..., :QK_NOPE_DIM],
                          jnp.zeros((KV_LORA_RANK, N_HEADS, HEAD_PAD - QK_NOPE_DIM), _F32)], axis=-1)
    wk = wk.reshape(KV_LORA_RANK, N_HEADS * HEAD_PAD).astype(_BF)
    wv_t = wkv3[..., QK_NOPE_DIM:].reshape(KV_LORA_RANK, N_HEADS * V_HEAD_DIM).T.astype(_BF)
    return win, wkr_t, wu_t, wq_t, wk, wv_t


def kernel(x, positions, meta_tokens, g_pre_mix, w_in, g_q_lat, w_uq, g_kv_lat, w_ukv,
           ssm_A_re, ssm_A_im, ssm_log_dt, ssm_B_re, ssm_B_im, ssm_C_re, ssm_C_im, ssm_D,
           w_glu, g_mix_out, w_out, g_post_mix, g_pre_mlp, w_mlp_up, w_mlp_down, g_post_mlp):
    bsz, seq, _ = x.shape
    assert seq % ROW_TILE == 0 and ROW_TILE % SLAB_T == 0 and bsz <= SUBLANES
    assert seq % PROJ_TILE == 0 and PROJ_TILE % SLAB_T == 0
    assert N_META <= SSM_CHUNK
    row = lambda g: g.reshape(1, -1).astype(_F32)

    win, wkr_t, wu_t, wq_t, wk, wv_t = _prep_weights(w_in[0], w_uq[0], w_ukv[0])
    weights = (row(g_pre_mix[0]), win, wkr_t, wu_t, row(g_q_lat[0]), wq_t, row(g_kv_lat[0]), wk, wv_t)
    rope = _rope_tables(positions.astype(jnp.int32) + N_META, PROJ_TILE)
    qt, k, vt, u, q_norm, k_sq = _proj_call(x, rope, *weights, tile=PROJ_TILE, meta=False)
    meta_x = jnp.pad(meta_tokens.astype(x.dtype), ((0, LANES - N_META), (0, 0)))[None]
    rope_m = _rope_tables(jnp.arange(LANES, dtype=jnp.int32)[None], LANES)
    k_m, vt_m, u_m = _proj_call(meta_x, rope_m, *weights, tile=LANES, meta=True)
    km, vtm, u_m = k_m[0], vt_m[0, :, 0], u_m[:, :N_META]
    k_m_sq = jnp.max(jnp.sum(jnp.square(k_m[0].astype(_F32)), axis=-1), axis=-1)
    k_max = jnp.sqrt(jnp.maximum(jnp.max(k_sq, axis=(2, 3, 4)), k_m_sq[None]))
    mrow = q_norm * (k_max * BOUND_SLACK)[:, :, None, None, None]
    attn = lax.cond(jnp.max(mrow) <= BOUND_LIMIT,
                    lambda: _attn_call(qt, mrow, k, vt, km, vtm, bounded=True),
                    lambda: _attn_call(qt, mrow, k, vt, km, vtm, bounded=False))

    um = u_m[0].astype(_BF).reshape(N_META, SSM_GROUPS, SSM_GROUP).transpose(1, 2, 0)
    um = jnp.pad(um, ((0, 0), (0, 0), (SSM_CHUNK - N_META, 0))).reshape(SSM_GROUPS, 1, CHUNK_W)
    um = jnp.broadcast_to(um, (SSM_GROUPS, SUBLANES, CHUNK_W))
    ops = _s5_matrices(ssm_A_re[0], ssm_A_im[0], ssm_log_dt[0], ssm_B_re[0],
                       ssm_B_im[0], ssm_C_re[0], ssm_C_im[0], ssm_D[0])
    nslab = seq // SLAB_T
    yg = _s5_call(u.reshape(SSM_GROUPS, nslab * bsz * SSM_GROUP, SLAB_T), um, _chunk_permutation(),
                  *ops, nslab=nslab, bsz=bsz)
    y = yg.reshape(SSM_GROUPS, nslab, bsz * SSM_GROUP, SLAB_T)

    return _post_call(x, attn, y, w_glu[0].astype(_BF), row(g_mix_out[0]), w_out[0].astype(_BF),
                      row(g_post_mix[0]), row(g_pre_mlp[0]), w_mlp_up[0].astype(_BF),
                      w_mlp_down[0].astype(_BF), row(g_post_mlp[0]), tile=ROW_TILE)
```
